```python
import functools
import jax
import jax.numpy as jnp
from jax import lax
import numpy as np

D_MODEL = 1024
BATCH = 32
SEQ = 256
DEPTH = 2
DEC_BATCH = 4
DEC_SEQ = 2048
PAST_LEN = 256

GRID_W = 64
N_AB_LAYERS = (DEPTH + 1) // 2
N_NA_LAYERS = DEPTH // 2
A_HEADS = 8
A_KV_HEADS = 2
A_HEAD_DIM = 64
A_WIDTH = A_HEADS * A_HEAD_DIM
A_KV_WIDTH = A_KV_HEADS * A_HEAD_DIM
B_HEADS = 4
B_HEAD_DIM = 128
B_WIDTH = B_HEADS * B_HEAD_DIM
MLSTM_CHUNK = 128
AB_SIZES = (A_WIDTH, A_KV_WIDTH, A_KV_WIDTH, B_WIDTH, B_WIDTH, B_WIDTH, B_WIDTH, 4 * B_HEADS)
AB_IN_WIDTH = A_WIDTH + 2 * A_KV_WIDTH + 4 * B_WIDTH + 4 * B_HEADS
AB_OUT_WIDTH = A_WIDTH + B_WIDTH
C_HEADS = 16
C_HEAD_DIM = 64
C_WIDTH = C_HEADS * C_HEAD_DIM
NA_ROWS_MAX = 8
NA_COLS = 16
N_GROUPS = 4
EXPERTS_PER_GROUP = 8
N_EXPERTS = N_GROUPS * EXPERTS_PER_GROUP
TOP_K_IN_GROUP = 2
D_EXPERT = 512
MOE_BLOCK = 128
QUERY_BLOCK = 128
ROPE_BASE = 10000.0
NORM_EPS = 1e-6

kernel_name = "hybrid_dit_gqa_mlstm_natten_hmoe_step"


def rms_norm(x, g):
    xf = x.astype(jnp.float32)
    y = xf * lax.rsqrt(jnp.mean(xf * xf, axis=-1, keepdims=True) + NORM_EPS)
    return (y * g.astype(jnp.float32)).astype(x.dtype)


def axial_rope(x):
    n, d = x.shape[1], x.shape[-1]
    half = d // 2
    t = jnp.arange(n)
    row = (t // GRID_W).astype(jnp.float32)
    col = (t % GRID_W).astype(jnp.float32)
    freqs = 1.0 / (ROPE_BASE ** (jnp.arange(0, half, 2, dtype=jnp.float32) / half))

    def rot(xh, pos):
        ang = pos[:, None] * freqs[None, :]
        cos = jnp.cos(ang)[None, :, None, :]
        sin = jnp.sin(ang)[None, :, None, :]
        x1, x2 = jnp.split(xh, 2, axis=-1)
        return jnp.concatenate([x1 * cos - x2 * sin, x1 * sin + x2 * cos], axis=-1)

    xf = x.astype(jnp.float32)
    out = jnp.concatenate([rot(xf[..., :half], row), rot(xf[..., half:], col)], axis=-1)
    return out.astype(x.dtype)


def blocked_attention(q, k, v):
    b, n, hq, d = q.shape
    hkv = k.shape[2]
    g = hq // hkv
    scale = d ** -0.5
    qb = q.reshape(b, n // QUERY_BLOCK, QUERY_BLOCK, hkv, g, d).transpose(1, 0, 2, 3, 4, 5)

    def one_block(qi):
        s = jnp.einsum('bqkgd,btkd->bkgqt', qi, k).astype(jnp.float32) * scale
        p = jax.nn.softmax(s, axis=-1).astype(v.dtype)
        return jnp.einsum('bkgqt,btkd->bqkgd', p, v)

    o = lax.map(one_block, qb)
    return o.transpose(1, 0, 2, 3, 4, 5).reshape(b, n, hq, d)


def neighbourhood_attention(q, k, v, ctx_k, ctx_v, rpb):
    b, n, hh, d = q.shape
    rows = n // GRID_W
    kr = min(NA_ROWS_MAX, rows)
    kc = NA_COLS
    scale = d ** -0.5
    kg = k.reshape(b, rows, GRID_W, hh, d)
    vg = v.reshape(b, rows, GRID_W, hh, d)
    qg = jnp.moveaxis(q.reshape(b, rows, GRID_W, hh, d), 1, 0)
    r_idx = jnp.arange(rows)
    row_start = jnp.clip(r_idx - kr // 2, 0, rows - kr)
    cols = jnp.arange(GRID_W)
    col_start = jnp.clip(cols - kc // 2, 0, GRID_W - kc)
    col_idx = col_start[:, None] + jnp.arange(kc)[None, :]
    dcol = col_idx - cols[:, None] + (NA_COLS - 1)

    def one_row(args):
        qr, r, rs = args
        kb = lax.dynamic_slice_in_dim(kg, rs, kr, axis=1)
        vb = lax.dynamic_slice_in_dim(vg, rs, kr, axis=1)
        kw = kb[:, :, col_idx]
        vw = vb[:, :, col_idx]
        drow = rs + jnp.arange(kr) - r + (NA_ROWS_MAX - 1)
        bias = rpb[:, drow][:, :, dcol]
        bias = jnp.transpose(bias, (0, 2, 1, 3)).reshape(hh, GRID_W, kr * kc)
        s_win = jnp.einsum('bqhd,brqjhd->bhqrj', qr, kw).astype(jnp.float32)
        s_win = s_win.reshape(b, hh, GRID_W, kr * kc) * scale + bias.astype(jnp.float32)
        s_ctx = jnp.einsum('bqhd,bphd->bhqp', qr, ctx_k).astype(jnp.float32) * scale
        p = jax.nn.softmax(jnp.concatenate([s_win, s_ctx], axis=-1), axis=-1).astype(v.dtype)
        p_win = p[..., :kr * kc].reshape(b, hh, GRID_W, kr, kc)
        p_ctx = p[..., kr * kc:]
        return (jnp.einsum('bhqrj,brqjhd->bqhd', p_win, vw)
                + jnp.einsum('bhqp,bphd->bqhd', p_ctx, ctx_v))

    o = lax.map(one_row, (qg, r_idx, row_start))
    return jnp.moveaxis(o, 0, 1).reshape(b, n, hh, d)


def mlstm_scan(q, k, v, i_pre, f_pre, c0, n0, m0):
    b, n, h, d = q.shape
    nc = n // MLSTM_CHUNK
    L = MLSTM_CHUNK

    def chunks(x):
        x = x.astype(jnp.float32).reshape((b, nc, L, h) + x.shape[3:])
        return jnp.swapaxes(jnp.moveaxis(x, 1, 0), 2, 3)

    qc = chunks(q)
    kc = chunks(k) * (d ** -0.5)
    vc = chunks(v)
    li = chunks(i_pre)
    lf = jax.nn.log_sigmoid(chunks(f_pre))
    causal = jnp.tril(jnp.ones((L, L), dtype=bool))

    def step(carry, inp):
        C, nv, m = carry
        qx, kx, vx, lix, lfx = inp
        bcum = jnp.cumsum(lfx, axis=-1)
        dlog = jnp.where(causal, bcum[..., :, None] - bcum[..., None, :] + lix[..., None, :], -jnp.inf)
        inter = bcum + m[..., None]
        m_t = jnp.maximum(inter, jnp.max(dlog, axis=-1))
        w_in = jnp.exp(dlog - m_t[..., None])
        w_st = jnp.exp(inter - m_t)
        a = jnp.einsum('bhtd,bhsd->bhts', qx, kx) * w_in
        num = (w_st[..., None] * jnp.einsum('bhtd,bhde->bhte', qx, C)
               + jnp.einsum('bhts,bhse->bhte', a, vx))
        den = w_st * jnp.einsum('bhtd,bhd->bht', qx, nv) + jnp.sum(a, axis=-1)
        hx = num / jnp.maximum(jnp.abs(den), jnp.exp(-m_t))[..., None]
        b_end = bcum[..., -1]
        log_ws = b_end[..., None] - bcum + lix
        m_new = jnp.maximum(b_end + m, jnp.max(log_ws, axis=-1))
        ws = jnp.exp(log_ws - m_new[..., None])
        wc = jnp.exp(b_end + m - m_new)
        C_new = wc[..., None, None] * C + jnp.einsum('bhs,bhsd,bhse->bhde', ws, kx, vx)
        n_new = wc[..., None] * nv + jnp.einsum('bhs,bhsd->bhd', ws, kx)
        return (C_new, n_new, m_new), hx

    carry0 = (c0.astype(jnp.float32), n0.astype(jnp.float32), m0.astype(jnp.float32))
    (cT, nT, mT), hs = lax.scan(step, carry0, (qc, kc, vc, li, lf))
    hs = jnp.moveaxis(jnp.swapaxes(hs, 2, 3), 0, 1).reshape(b, n, h, d)
    return hs, cT, nT, mT


def mlstm_bidir(q, k, v, gates, c0, n0, m0):
    hf, cf, nf, mf = mlstm_scan(q, k, v, gates[:, :, 0], gates[:, :, 1], c0[:, 0], n0[:, 0], m0[:, 0])
    flip = lambda x: jnp.flip(x, axis=1)
    hb, cb, nb, mb = mlstm_scan(flip(q), flip(k), flip(v), flip(gates[:, :, 2]), flip(gates[:, :, 3]),
                                c0[:, 1], n0[:, 1], m0[:, 1])
    h = hf + flip(hb)
    return h, jnp.stack([cf, cb], axis=1), jnp.stack([nf, nb], axis=1), jnp.stack([mf, mb], axis=1)


def ab_mixer(w_in, w_out, q_norm, k_norm, gate_bias, out_norm, h, ctx=None):
    b, n, _ = h.shape
    split_idx = [int(s) for s in np.cumsum(AB_SIZES)[:-1]]
    aq, ak, av, bq, bk, bv, bo, bg = jnp.split(h @ w_in, split_idx, axis=-1)
    aq = rms_norm(aq.reshape(b, n, A_HEADS, A_HEAD_DIM), q_norm)
    ak = rms_norm(ak.reshape(b, n, A_KV_HEADS, A_HEAD_DIM), k_norm)
    av = av.reshape(b, n, A_KV_HEADS, A_HEAD_DIM)
    bq = bq.reshape(b, n, B_HEADS, B_HEAD_DIM)
    bk = bk.reshape(b, n, B_HEADS, B_HEAD_DIM)
    bv = bv.reshape(b, n, B_HEADS, B_HEAD_DIM)
    gates = bg.reshape(b, n, 4, B_HEADS) + gate_bias
    if ctx is None:
        a_out = blocked_attention(aq, ak, av)
        c0 = jnp.zeros((b, 2, B_HEADS, B_HEAD_DIM, B_HEAD_DIM), jnp.float32)
        n0 = jnp.zeros((b, 2, B_HEADS, B_HEAD_DIM), jnp.float32)
        m0 = jnp.zeros((b, 2, B_HEADS), jnp.float32)
    else:
        ctx_k, ctx_v, c0, n0, m0 = ctx
        aq, ak = axial_rope(aq), axial_rope(ak)
        a_out = blocked_attention(aq, jnp.concatenate([ctx_k, ak], axis=1),
                                  jnp.concatenate([ctx_v, av], axis=1))
    hm, cT, nT, mT = mlstm_bidir(bq, bk, bv, gates, c0, n0, m0)
    b_out = rms_norm(hm, out_norm).astype(h.dtype) * jax.nn.sigmoid(bo.reshape(b, n, B_HEADS, B_HEAD_DIM))
    out = jnp.concatenate([a_out.reshape(b, n, A_WIDTH), b_out.reshape(b, n, B_WIDTH)], axis=-1) @ w_out
    if ctx is None:
        return out, (ak, av, cT, nT, mT)
    return out, ()


def na_mixer(w_in, w_out, rpb, h, ctx=None):
    b, n, _ = h.shape
    q, k, v = jnp.split(h @ w_in, 3, axis=-1)
    q = q.reshape(b, n, C_HEADS, C_HEAD_DIM)
    k = k.reshape(b, n, C_HEADS, C_HEAD_DIM)
    v = v.reshape(b, n, C_HEADS, C_HEAD_DIM)
    if ctx is None:
        o = blocked_attention(q, k, v)
        state = (k, v)
    else:
        o = neighbourhood_attention(q, k, v, ctx[0], ctx[1], rpb)
        state = ()
    return o.reshape(b, n, C_WIDTH) @ w_out, state


def hier_moe(x, wg, bg, we, be, w1, w3, w2):
    b, n, dm = x.shape
    t = b * n
    xt = x.reshape(t, dm)
    g_logits = (xt @ wg + bg).astype(jnp.float32)
    g_sel = jnp.argmax(g_logits, axis=-1)
    g_prob = jnp.take_along_axis(jax.nn.softmax(g_logits, axis=-1), g_sel[:, None], axis=1)[:, 0]
    e_logits = (xt @ we + be).astype(jnp.float32).reshape(t, N_GROUPS, EXPERTS_PER_GROUP)
    e_logits = jnp.take_along_axis(e_logits, g_sel[:, None, None], axis=1)[:, 0]
    top_v, top_i = lax.top_k(e_logits, TOP_K_IN_GROUP)
    weights = (jax.nn.softmax(top_v, axis=-1) * g_prob[:, None]).reshape(-1)
    eid = (g_sel[:, None] * EXPERTS_PER_GROUP + top_i).reshape(-1).astype(jnp.int32)
    tok = jnp.repeat(jnp.arange(t, dtype=jnp.int32), TOP_K_IN_GROUP)
    onehot = (eid[:, None] == jnp.arange(N_EXPERTS, dtype=jnp.int32)[None, :]).astype(jnp.int32)
    rank = jnp.take_along_axis(jnp.cumsum(onehot, axis=0), eid[:, None], axis=1)[:, 0] - 1
    counts = jnp.sum(onehot, axis=0)
    padded = (counts + MOE_BLOCK - 1) // MOE_BLOCK * MOE_BLOCK
    seg_end = jnp.cumsum(padded)
    seg_start = seg_end - padded
    dest = seg_start[eid] + rank
    n_blocks = (t * TOP_K_IN_GROUP + N_EXPERTS * (MOE_BLOCK - 1) + MOE_BLOCK - 1) // MOE_BLOCK
    cap = n_blocks * MOE_BLOCK
    slot_tok = jnp.full((cap,), t, dtype=jnp.int32).at[dest].set(tok)
    slot_w = jnp.zeros((cap,), x.dtype).at[dest].set(weights.astype(x.dtype))
    block_expert = jnp.minimum(
        jnp.searchsorted(seg_end, jnp.arange(n_blocks, dtype=jnp.int32) * MOE_BLOCK, side='right'),
        N_EXPERTS - 1)
    x_pad = jnp.concatenate([xt, jnp.zeros((1, dm), xt.dtype)], axis=0)
    xb = x_pad[slot_tok].reshape(n_blocks, MOE_BLOCK, dm)

    def expert_block(args):
        xi, e = args
        hid = jax.nn.silu(xi @ w1[e]) * (xi @ w3[e])
        return hid @ w2[e]

    yb = lax.map(expert_block, (xb, block_expert)).reshape(cap, dm)
    y = jax.ops.segment_sum(yb * slot_w[:, None], slot_tok, num_segments=t + 1)[:t]
    return y.reshape(b, n, dm)


def adaln_layer(x, cond, g_mix, g_ffn, ada_w, ada_b, mixer, moe_params):
    mod = (jax.nn.silu(cond) @ ada_w + ada_b)[:, None, :]
    sh1, sc1, g1, sh2, sc2, g2 = jnp.split(mod, 6, axis=-1)
    out, state = mixer(rms_norm(x, g_mix) * (1 + sc1) + sh1)
    x = x + g1 * out
    x = x + g2 * hier_moe(rms_norm(x, g_ffn) * (1 + sc2) + sh2, *moe_params)
    return x, state


def setup_inputs(seed: int = 0) -> dict:
    key = jax.random.key(seed)
    ks = jax.random.split(key, 40)
    nrm = lambda i, shape, s: jax.random.normal(ks[i], shape, jnp.float32) * s
    D = D_MODEL
    return {
        "x_prompt": nrm(0, (BATCH, SEQ, D), 1.0),
        "x_sample": nrm(1, (DEC_BATCH, DEC_SEQ, D), 1.0),
        "cache_attn_k": nrm(2, (DEC_BATCH, N_AB_LAYERS, PAST_LEN, A_KV_HEADS, A_HEAD_DIM), 1.0),
        "cache_attn_v": nrm(3, (DEC_BATCH, N_AB_LAYERS, PAST_LEN, A_KV_HEADS, A_HEAD_DIM), 1.0),
        "state_mlstm_C": nrm(4, (DEC_BATCH, N_AB_LAYERS, 2, B_HEADS, B_HEAD_DIM, B_HEAD_DIM), 0.1),
        "state_mlstm_n": nrm(5, (DEC_BATCH, N_AB_LAYERS, 2, B_HEADS, B_HEAD_DIM), 0.1),
        "state_mlstm_m": nrm(6, (DEC_BATCH, N_AB_LAYERS, 2, B_HEADS), 0.5),
        "cache_na_k": nrm(7, (DEC_BATCH, N_NA_LAYERS, PAST_LEN, C_HEADS, C_HEAD_DIM), 1.0),
        "cache_na_v": nrm(8, (DEC_BATCH, N_NA_LAYERS, PAST_LEN, C_HEADS, C_HEAD_DIM), 1.0),
        "c": nrm(9, (DEC_BATCH, D), 1.0),
        "c_ctx": nrm(10, (D,), 1.0),
        "norm_mix": 1.0 + nrm(11, (DEPTH, D), 0.01),
        "norm_ffn": 1.0 + nrm(12, (DEPTH, D), 0.01),
        "norm_final": 1.0 + nrm(13, (D,), 0.01),
        "ada_w": nrm(14, (DEPTH, D, 6 * D), 0.5 * D ** -0.5),
        "ada_b": nrm(15, (DEPTH, 6 * D), 0.01),
        "ab_w_in": nrm(16, (N_AB_LAYERS, D, AB_IN_WIDTH), D ** -0.5),
        "ab_w_out": nrm(17, (N_AB_LAYERS, AB_OUT_WIDTH, D), AB_OUT_WIDTH ** -0.5),
        "ab_q_norm": 1.0 + nrm(18, (N_AB_LAYERS, A_HEAD_DIM), 0.01),
        "ab_k_norm": 1.0 + nrm(19, (N_AB_LAYERS, A_HEAD_DIM), 0.01),
        "ab_gate_bias": jnp.array([0.0, 3.0, 0.0, 3.0], jnp.float32)[None, :, None]
                        + nrm(20, (N_AB_LAYERS, 4, B_HEADS), 0.1),
        "ab_out_norm": 1.0 + nrm(21, (N_AB_LAYERS, B_HEADS, B_HEAD_DIM), 0.01),
        "na_w_in": nrm(22, (N_NA_LAYERS, D, 3 * C_WIDTH), D ** -0.5),
        "na_w_out": nrm(23, (N_NA_LAYERS, C_WIDTH, D), C_WIDTH ** -0.5),
        "na_rpb": nrm(24, (N_NA_LAYERS, C_HEADS, 2 * NA_ROWS_MAX - 1, 2 * NA_COLS - 1), 0.1),
        "moe_wg": nrm(25, (DEPTH, D, N_GROUPS), D ** -0.5),
        "moe_bg": nrm(26, (DEPTH, N_GROUPS), 0.01),
        "moe_we": nrm(27, (DEPTH, D, N_EXPERTS), D ** -0.5),
        "moe_be": nrm(28, (DEPTH, N_EXPERTS), 0.01),
        "moe_w1": nrm(29, (DEPTH, N_EXPERTS, D, D_EXPERT), D ** -0.5),
        "moe_w3": nrm(30, (DEPTH, N_EXPERTS, D, D_EXPERT), D ** -0.5),
        "moe_w2": nrm(31, (DEPTH, N_EXPERTS, D_EXPERT, D), D_EXPERT ** -0.5),
    }


def reference(x_prompt, x_sample, cache_attn_k, cache_attn_v, state_mlstm_C, state_mlstm_n, state_mlstm_m,
              cache_na_k, cache_na_v, c, c_ctx, norm_mix, norm_ffn, norm_final, ada_w, ada_b,
              ab_w_in, ab_w_out, ab_q_norm, ab_k_norm, ab_gate_bias, ab_out_norm,
              na_w_in, na_w_out, na_rpb, moe_wg, moe_bg, moe_we, moe_be, moe_w1, moe_w3, moe_w2):
    cond_ctx = c_ctx[None, :]
    xp, xs = x_prompt, x_sample
    ak_l, av_l, mc_l, mn_l, mm_l, nk_l, nv_l = [], [], [], [], [], [], []
    for l in range(DEPTH):
        j = l // 2
        common = (norm_mix[l], norm_ffn[l], ada_w[l], ada_b[l])
        moe_p = (moe_wg[l], moe_bg[l], moe_we[l], moe_be[l], moe_w1[l], moe_w3[l], moe_w2[l])
        if l % 2 == 0:
            wts = (ab_w_in[j], ab_w_out[j], ab_q_norm[j], ab_k_norm[j], ab_gate_bias[j], ab_out_norm[j])
            xp, st = adaln_layer(xp, cond_ctx, *common, functools.partial(ab_mixer, *wts), moe_p)
            ak_l.append(st[0]); av_l.append(st[1]); mc_l.append(st[2]); mn_l.append(st[3]); mm_l.append(st[4])
            ctx = (cache_attn_k[:, j], cache_attn_v[:, j], state_mlstm_C[:, j], state_mlstm_n[:, j],
                   state_mlstm_m[:, j])
            xs, _ = adaln_layer(xs, c, *common, functools.partial(ab_mixer, *wts, ctx=ctx), moe_p)
        else:
            wts = (na_w_in[j], na_w_out[j], na_rpb[j])
            xp, st = adaln_layer(xp, cond_ctx, *common, functools.partial(na_mixer, *wts), moe_p)
            nk_l.append(st[0]); nv_l.append(st[1])
            ctx = (cache_na_k[:, j], cache_na_v[:, j])
            xs, _ = adaln_layer(xs, c, *common, functools.partial(na_mixer, *wts, ctx=ctx), moe_p)
    y_prompt = rms_norm(xp, norm_final)
    y_sample = rms_norm(xs, norm_final)
    new_attn_k = jnp.stack(ak_l, axis=1)
    new_attn_v = jnp.stack(av_l, axis=1)
    new_mlstm_C = jnp.stack(mc_l, axis=1)
    new_mlstm_n = jnp.stack(mn_l, axis=1)
    new_mlstm_m = jnp.stack(mm_l, axis=1)
    new_na_k = jnp.stack(nk_l, axis=1)
    new_na_v = jnp.stack(nv_l, axis=1)
    return (y_prompt, y_sample, new_attn_k, new_attn_v, new_mlstm_C, new_mlstm_n, new_mlstm_m, new_na_k, new_na_v)
```

```python
import functools

import numpy as np
import jax
import jax.numpy as jnp
from jax import lax
from jax.experimental import pallas as pl
from jax.experimental.pallas import tpu as pltpu

F32 = jnp.float32
BF16 = jnp.bfloat16

D_MODEL = 1024
BATCH = 32
SEQ = 256
DEC_BATCH = 4
DEC_SEQ = 2048
PAST_LEN = 256
GRID_W = 64
A_HEADS = 8
A_KV_HEADS = 2
A_HEAD_DIM = 64
A_WIDTH = A_HEADS * A_HEAD_DIM
A_KV_WIDTH = A_KV_HEADS * A_HEAD_DIM
B_HEADS = 4
B_HEAD_DIM = 128
B_WIDTH = B_HEADS * B_HEAD_DIM
MLSTM_CHUNK = 128
C_HEADS = 16
C_HEAD_DIM = 64
C_WIDTH = C_HEADS * C_HEAD_DIM
NA_ROWS = 8
NA_COLS = 16
N_GROUPS = 4
EXPERTS_PER_GROUP = 8
N_EXPERTS = N_GROUPS * EXPERTS_PER_GROUP
D_EXPERT = 512
MOE_BLOCK = 128
ROPE_BASE = 10000.0
NORM_EPS = 1e-6

T_PROMPT = BATCH * SEQ
T_SAMPLE = DEC_BATCH * DEC_SEQ
T_ALL = T_PROMPT + T_SAMPLE
N_COND = 8
LANES = 128
TOK_BLOCK = 256
NA_QROWS = 4
NA_KROWS = 12
NEG = -1e30
VMEM_LIMIT = 56 * 1024 * 1024


def _cparams(sem):
    return pltpu.CompilerParams(dimension_semantics=sem, vmem_limit_bytes=VMEM_LIMIT)


def _split2(x):
    hi = x.astype(BF16)
    lo = (x - hi.astype(F32)).astype(BF16)
    return hi, lo


def _split3(x):
    hi = x.astype(BF16)
    r = x - hi.astype(F32)
    mid = r.astype(BF16)
    lo = (r - mid.astype(F32)).astype(BF16)
    return hi, mid, lo


def _dot(a, b):
    return jnp.dot(a, b, preferred_element_type=F32)


def _dot_nt(a, b):
    return lax.dot_general(a, b, (((1,), (1,)), ((), ())), preferred_element_type=F32)


def _dot_exact_rhs(x, b):
    hi, mid, lo = _split3(x)
    return _dot(hi, b) + _dot(mid, b) + _dot(lo, b)


def _rms(x, g):
    ms = jnp.mean(x * x, axis=-1, keepdims=True)
    return (x * lax.rsqrt(ms + NORM_EPS)) * g


def _mod_row(i, tm):
    tok0 = i * tm
    return jnp.where(tok0 < T_PROMPT, 0, 1 + (tok0 - T_PROMPT) // DEC_SEQ)


def _mod_part(mod_ref, r, idx):
    return mod_ref[pl.ds(r, 1), idx * D_MODEL:(idx + 1) * D_MODEL]


def _head_rms(x, w, bd):
    hi, lo = _split2(x * x)
    ms = _dot(hi, bd) + _dot(lo, bd)
    return (x * lax.rsqrt(ms + NORM_EPS)) * w


def _lane(shape):
    return lax.broadcasted_iota(jnp.int32, shape, len(shape) - 1)


def _dup_half(x, g):
    xr = pltpu.roll(x, 64, 1)
    lo = _lane(x.shape) < 64
    return jnp.where(lo, x, xr) if g == 0 else jnp.where(lo, xr, x)


def _rope(x, cos, sin_signed):
    lane = _lane(x.shape)
    partner = jnp.where((lane % 32) < 16, pltpu.roll(x, LANES - 16, 1), pltpu.roll(x, 16, 1))
    return x * cos + partner * sin_signed


def _softmax_pair(qp, score_fn, value_fn):
    lo = _lane(qp.shape) < 64
    outs = []
    for half in range(2):
        qm = jnp.where(lo if half == 0 else jnp.logical_not(lo), qp, 0.0).astype(BF16)
        ss = score_fn(qm)
        m = ss[0].max(axis=-1, keepdims=True)
        for s in ss[1:]:
            m = jnp.maximum(m, s.max(axis=-1, keepdims=True))
        es = [jnp.exp(s - m) for s in ss]
        l = es[0].sum(axis=-1, keepdims=True)
        for e in es[1:]:
            l = l + e.sum(axis=-1, keepdims=True)
        o = value_fn([e.astype(BF16) for e in es])
        outs.append(o / l)
    return jnp.where(lo, outs[0], outs[1])


def _mod_kernel(cond_ref, w_ref, b_ref, o_ref):
    c = cond_ref[...]
    s = c * jax.nn.sigmoid(c)
    s_hi, s_lo = _split2(s)
    w_hi, w_lo = _split2(w_ref[0])
    o_ref[0] = _dot(s_hi, w_hi) + _dot(s_lo, w_hi) + _dot(s_hi, w_lo) + b_ref[0]


def _modulation(cond, ada_w, ada_b):
    depth, d, n = ada_w.shape
    tn = 1536
    return pl.pallas_call(
        _mod_kernel,
        grid=(depth, n // tn),
        in_specs=[pl.BlockSpec((N_COND, d), lambda l, j: (0, 0)),
                  pl.BlockSpec((1, d, tn), lambda l, j: (l, 0, j)),
                  pl.BlockSpec((1, 1, tn), lambda l, j: (l, 0, j))],
        out_specs=pl.BlockSpec((1, N_COND, tn), lambda l, j: (l, 0, j)),
        out_shape=jax.ShapeDtypeStruct((depth, N_COND, n), F32),
        compiler_params=_cparams(("arbitrary", "arbitrary")),
        name="adaln_modulation",
    )(cond, ada_w, ada_b.reshape(depth, 1, n))


def _norm_mod(x_ref, mod_ref, g_ref, shift_idx, scale_idx, tm):
    r = _mod_row(pl.program_id(0), tm)
    h = _rms(x_ref[...], g_ref[...])
    return h * (1.0 + _mod_part(mod_ref, r, scale_idx)) + _mod_part(mod_ref, r, shift_idx)


def _proj_ab_kernel(x_ref, mod_ref, g_ref, wb_ref, wq_ref, wkv_ref, wgt_ref, ob_ref, oq_ref, okv_ref, ogt_ref):
    tm = x_ref.shape[0]
    hb = _norm_mod(x_ref, mod_ref, g_ref, 0, 1, tm).astype(BF16)
    ob_ref[...] = _dot(hb, wb_ref[...])
    oq_ref[...] = _dot(hb, wq_ref[...])
    okv_ref[...] = _dot(hb, wkv_ref[...])
    gt = _dot_nt(wgt_ref[...], hb)
    for j in range(tm // LANES):
        ogt_ref[j] = gt[:, j * LANES:(j + 1) * LANES]


def _proj_ab(x, mod, g, wb, wq, wkv, wgt):
    t = x.shape[0]
    tm = TOK_BLOCK
    full = lambda a: pl.BlockSpec(a.shape, lambda i: (0,) * a.ndim)
    return pl.pallas_call(
        _proj_ab_kernel,
        grid=(t // tm,),
        in_specs=[pl.BlockSpec((tm, D_MODEL), lambda i: (i, 0)), full(mod), full(g),
                  full(wb), full(wq), full(wkv), full(wgt)],
        out_specs=[pl.BlockSpec((tm, 4 * B_WIDTH), lambda i: (i, 0)),
                   pl.BlockSpec((tm, A_WIDTH), lambda i: (i, 0)),
                   pl.BlockSpec((tm, 2 * A_KV_WIDTH), lambda i: (i, 0)),
                   pl.BlockSpec((tm // LANES, 8 * B_HEADS, LANES), lambda i: (i, 0, 0))],
        out_shape=[jax.ShapeDtypeStruct((t, 4 * B_WIDTH), F32),
                   jax.ShapeDtypeStruct((t, A_WIDTH), F32),
                   jax.ShapeDtypeStruct((t, 2 * A_KV_WIDTH), F32),
                   jax.ShapeDtypeStruct((t // LANES, 8 * B_HEADS, LANES), F32)],
        compiler_params=_cparams(("arbitrary",)),
        name="proj_ab",
    )(x, mod, g, wb, wq, wkv, wgt)


def _proj_na_kernel(x_ref, mod_ref, g_ref, w_ref, o_ref):
    tm = x_ref.shape[0]
    hb = _norm_mod(x_ref, mod_ref, g_ref, 0, 1, tm).astype(BF16)
    o_ref[...] = _dot(hb, w_ref[...])


def _proj_na(x, mod, g, w):
    t = x.shape[0]
    tm = TOK_BLOCK
    n = w.shape[1]
    full = lambda a: pl.BlockSpec(a.shape, lambda i: (0,) * a.ndim)
    return pl.pallas_call(
        _proj_na_kernel,
        grid=(t // tm,),
        in_specs=[pl.BlockSpec((tm, D_MODEL), lambda i: (i, 0)), full(mod), full(g), full(w)],
        out_specs=pl.BlockSpec((tm, n), lambda i: (i, 0)),
        out_shape=jax.ShapeDtypeStruct((t, n), F32),
        compiler_params=_cparams(("arbitrary",)),
        name="proj_na",
    )(x, mod, g, w)


def _gqa_block(q_ref, qn, bd, kd_ref, vd_ref, o_ref, rope=None):
    for p in range(A_HEADS // 2):
        g = p // (A_HEADS // 2 // A_KV_HEADS)
        qp = _head_rms(q_ref[:, p * LANES:(p + 1) * LANES], qn, bd)
        if rope is not None:
            qp = _rope(qp, rope[0], rope[1])
        qp = qp * (A_HEAD_DIM ** -0.5)
        o_ref[:, p * LANES:(p + 1) * LANES] = _softmax_pair(
            qp, lambda qm: [_dot_nt(qm, kd_ref[g])], lambda ps: _dot(ps[0], vd_ref[g]))


def _attn_a_prompt_kernel(q_ref, kv_ref, qn_ref, kn_ref, bd_ref, o_ref, knew_ref, vnew_ref, kd_ref, vd_ref):
    bd = bd_ref[...]
    k = _head_rms(kv_ref[:, :LANES], kn_ref[...], bd)
    v = kv_ref[:, LANES:]
    knew_ref[...] = k
    vnew_ref[...] = v
    for g in range(A_KV_HEADS):
        kd_ref[g] = _dup_half(k, g).astype(BF16)
        vd_ref[g] = _dup_half(v, g).astype(BF16)
    _gqa_block(q_ref, qn_ref[...], bd, kd_ref, vd_ref, o_ref)


def _attn_a_prompt(q, kv, qn, kn, bd):
    nb = BATCH
    full = lambda a: pl.BlockSpec(a.shape, lambda b: (0,) * a.ndim)
    return pl.pallas_call(
        _attn_a_prompt_kernel,
        grid=(nb,),
        in_specs=[pl.BlockSpec((SEQ, A_WIDTH), lambda b: (b, 0)),
                  pl.BlockSpec((SEQ, 2 * A_KV_WIDTH), lambda b: (b, 0)),
                  full(qn), full(kn), full(bd)],
        out_specs=[pl.BlockSpec((SEQ, A_WIDTH), lambda b: (b, 0)),
                   pl.BlockSpec((SEQ, A_KV_WIDTH), lambda b: (b, 0)),
                   pl.BlockSpec((SEQ, A_KV_WIDTH), lambda b: (b, 0))],
        out_shape=[jax.ShapeDtypeStruct((T_PROMPT, A_WIDTH), F32),
                   jax.ShapeDtypeStruct((T_PROMPT, A_KV_WIDTH), F32),
                   jax.ShapeDtypeStruct((T_PROMPT, A_KV_WIDTH), F32)],
        scratch_shapes=[pltpu.VMEM((A_KV_HEADS, SEQ, LANES), BF16),
                        pltpu.VMEM((A_KV_HEADS, SEQ, LANES), BF16)],
        compiler_params=_cparams(("arbitrary",)),
        name="attn_a_prompt",
    )(q, kv, qn, kn, bd)


_A_QBLOCK = 256


def _attn_a_sample_kernel(q_ref, kv_ref, ck_ref, cv_ref, cos_ref, sin_ref, cosq_ref, sinq_ref,
                          qn_ref, kn_ref, bd_ref, o_ref, kd_ref, vd_ref):
    bd = bd_ref[...]

    @pl.when(pl.program_id(1) == 0)
    def _():
        for g in range(A_KV_HEADS):
            kd_ref[g, :PAST_LEN] = _dup_half(ck_ref[0], g).astype(BF16)
            vd_ref[g, :PAST_LEN] = _dup_half(cv_ref[0], g).astype(BF16)
        rows = 256
        for c in range(DEC_SEQ // rows):
            sl = slice(c * rows, (c + 1) * rows)
            k = _head_rms(kv_ref[sl, :LANES], kn_ref[...], bd)
            k = _rope(k, cos_ref[sl, :], sin_ref[sl, :])
            v = kv_ref[sl, LANES:]
            dst = slice(PAST_LEN + c * rows, PAST_LEN + (c + 1) * rows)
            for g in range(A_KV_HEADS):
                kd_ref[g, dst] = _dup_half(k, g).astype(BF16)
                vd_ref[g, dst] = _dup_half(v, g).astype(BF16)

    _gqa_block(q_ref, qn_ref[...], bd, kd_ref, vd_ref, o_ref, rope=(cosq_ref[...], sinq_ref[...]))


def _attn_a_sample(q, kv, ck, cv, cos, sin, qn, kn, bd):
    nq = DEC_SEQ // _A_QBLOCK
    off = T_PROMPT // _A_QBLOCK
    offb = T_PROMPT // DEC_SEQ
    full = lambda a: pl.BlockSpec(a.shape, lambda b, i: (0,) * a.ndim)
    tk = PAST_LEN + DEC_SEQ
    return pl.pallas_call(
        _attn_a_sample_kernel,
        grid=(DEC_BATCH, nq),
        in_specs=[pl.BlockSpec((_A_QBLOCK, A_WIDTH), lambda b, i: (off + b * nq + i, 0)),
                  pl.BlockSpec((DEC_SEQ, 2 * A_KV_WIDTH), lambda b, i: (offb + b, 0)),
                  pl.BlockSpec((1, PAST_LEN, A_KV_WIDTH), lambda b, i: (b, 0, 0)),
                  pl.BlockSpec((1, PAST_LEN, A_KV_WIDTH), lambda b, i: (b, 0, 0)),
                  full(cos), full(sin),
                  pl.BlockSpec((_A_QBLOCK, LANES), lambda b, i: (i, 0)),
                  pl.BlockSpec((_A_QBLOCK, LANES), lambda b, i: (i, 0)),
                  full(qn), full(kn), full(bd)],
        out_specs=pl.BlockSpec((_A_QBLOCK, A_WIDTH), lambda b, i: (b * nq + i, 0)),
        out_shape=jax.ShapeDtypeStruct((T_SAMPLE, A_WIDTH), F32),
        scratch_shapes=[pltpu.VMEM((A_KV_HEADS, tk, LANES), BF16),
                        pltpu.VMEM((A_KV_HEADS, tk, LANES), BF16)],
        compiler_params=_cparams(("arbitrary", "arbitrary")),
        name="attn_a_sample",
    )(q, kv, ck, cv, cos, sin, cos, sin, qn, kn, bd)


def _log_sigmoid(x):
    return -(jnp.maximum(-x, 0.0) + jnp.log1p(jnp.exp(-jnp.abs(x))))


def _mlstm_kernel(*refs, n, has_init, emit_state):
    it = iter(refs)
    q_ref, k_ref, v_ref, og_ref, gt_ref, brow_ref, onorm_ref, tri_ref = [next(it) for _ in range(8)]
    if has_init:
        c0_ref, n0_ref, m0_ref = [next(it) for _ in range(3)]
    out_ref = next(it)
    if emit_state:
        cT_ref, nT_ref, mT_ref = [next(it) for _ in range(3)]
    h_ref, row_ref, col_ref, tbuf_ref, c_ref, n_ref = [next(it) for _ in range(6)]

    L = MLSTM_CHUNK
    nc = n // L

    gt = gt_ref[...] + brow_ref[0][None]
    lf = _log_sigmoid(gt).reshape(nc * 8, L)
    cum_f = _dot_exact_rhs(lf, tri_ref[0]).reshape(nc, 8, L)
    cum_b = _dot_exact_rhs(lf, tri_ref[1]).reshape(nc, 8, L)
    cf = cum_f[:, 1:2, :]
    cb = cum_b[:, 3:4, :]
    row_ref[:, 0:1, :] = cf
    row_ref[:, 1:2, :] = gt[:, 0:1, :] - cf
    row_ref[:, 2:3, :] = cb
    row_ref[:, 3:4, :] = gt[:, 2:3, :] - cb
    lf3 = lf.reshape(nc, 8, L)
    row_ref[:, 4:5, :] = jnp.broadcast_to(lf3[:, 1:2, :].sum(axis=-1, keepdims=True), (nc, 1, L))
    row_ref[:, 5:6, :] = jnp.broadcast_to(lf3[:, 3:4, :].sum(axis=-1, keepdims=True), (nc, 1, L))
    row_ref[:, 6:8, :] = jnp.zeros((nc, 2, L), F32)

    tbuf_ref[...] = jnp.zeros((L, L), F32)

    def to_cols(c, carry):
        tbuf_ref[0:8, :] = row_ref[c]
        col_ref[pl.ds(pl.multiple_of(c * L, L), L), :] = tbuf_ref[...].T
        return carry

    lax.fori_loop(0, nc, to_cols, 0)

    if has_init:
        c_ref[...] = c0_ref[0, :, 0]
        n_ref[0:2, :] = n0_ref[0, 0]
        m_init = (m0_ref[0, 0, 0:1, 0:1], m0_ref[0, 0, 1:2, 0:1])
    else:
        c_ref[...] = jnp.zeros((2, L, L), F32)
        n_ref[...] = jnp.zeros((8, L), F32)
        m_init = (jnp.zeros((1, 1), F32), jnp.zeros((1, 1), F32))

    t_idx = lax.broadcasted_iota(jnp.int32, (L, L), 0)
    s_idx = lax.broadcasted_iota(jnp.int32, (L, L), 1)
    masks = (s_idx <= t_idx, s_idx >= t_idx)

    def step(c, d, m):
        r0 = pl.multiple_of(c * L, L)
        q = q_ref[pl.ds(r0, L), :]
        ks = k_ref[pl.ds(r0, L), :] * (B_HEAD_DIM ** -0.5)
        vb = v_ref[pl.ds(r0, L), :].astype(BF16)
        qb = q.astype(BF16)
        rows = row_ref[c]
        cols = col_ref[pl.ds(r0, L), :]
        a_row = rows[2 * d + 1:2 * d + 2, :]
        tot = rows[4 + d:5 + d, 0:1]
        cum_col = cols[:, 2 * d:2 * d + 1]
        a_col = cols[:, 2 * d + 1:2 * d + 2]
        dlog = jnp.where(masks[d], cum_col + a_row, -jnp.inf)
        inter = cum_col + m
        m_t = jnp.maximum(inter, dlog.max(axis=-1, keepdims=True))
        w_in = jnp.exp(dlog - m_t)
        w_st = jnp.exp(inter - m_t)
        a = _dot_nt(qb, ks.astype(BF16)) * w_in
        cmat = c_ref[d]
        nvec = n_ref[d:d + 1, :]
        num = w_st * _dot(qb, cmat.astype(BF16)) + _dot(a.astype(BF16), vb)
        den = w_st * (q * nvec).sum(axis=-1, keepdims=True) + a.sum(axis=-1, keepdims=True)
        hx = num / jnp.maximum(jnp.abs(den), jnp.exp(-m_t))
        if d == 0:
            h_ref[0, pl.ds(r0, L), :] = hx
        else:
            h_ref[1, pl.ds(r0, L), :] = hx
        m_new = jnp.maximum(tot + m, a_row.max(axis=-1, keepdims=True) + tot)
        ws = jnp.exp(a_col + tot - m_new)
        wc = jnp.exp(tot + m - m_new)
        kw = ks * ws
        c_ref[d] = wc * cmat + _dot(kw.T.astype(BF16), vb)
        n_ref[d:d + 1, :] = wc * nvec + kw.sum(axis=0, keepdims=True)
        return m_new

    def body(i, carry):
        return step(i, 0, carry[0]), step(nc - 1 - i, 1, carry[1])

    m_f, m_b = lax.fori_loop(0, nc, body, m_init)

    hm = h_ref[0] + h_ref[1]
    y = _rms(hm, onorm_ref[0])
    out_ref[...] = y * jax.nn.sigmoid(og_ref[...])

    if emit_state:
        cT_ref[0, :, 0] = c_ref[...]
        nT_ref[0, 0] = n_ref[0:2, :]
        mT_ref[0, 0, 0:1, :] = jnp.broadcast_to(m_f, (1, L))
        mT_ref[0, 0, 1:2, :] = jnp.broadcast_to(m_b, (1, L))


def _mlstm(ob, ogt, brow, onorm, tri, *, n, seq0, nseq, init=None, emit_state=False):
    L = MLSTM_CHUNK
    nc = n // L
    H = B_HEADS
    col = lambda part: (lambda b, h: (seq0 + b, part * H + h))
    in_specs = [pl.BlockSpec((n, L), col(0)), pl.BlockSpec((n, L), col(1)), pl.BlockSpec((n, L), col(2)),
                pl.BlockSpec((n, L), col(3)),
                pl.BlockSpec((nc, 8, L), lambda b, h: (seq0 + b, h, 0)),
                pl.BlockSpec((1, 8, L), lambda b, h: (h, 0, 0)),
                pl.BlockSpec((1, 1, L), lambda b, h: (h, 0, 0)),
                pl.BlockSpec(tri.shape, lambda b, h: (0, 0, 0))]
    args = [ob, ob, ob, ob, ogt, brow, onorm, tri]
    if init is not None:
        c0, n0, m0 = init
        in_specs += [pl.BlockSpec((1, 2, 1, L, L), lambda b, h: (b, 0, h, 0, 0)),
                     pl.BlockSpec((1, 1, 2, L), lambda b, h: (b, h, 0, 0)),
                     pl.BlockSpec((1, 1, 2, L), lambda b, h: (b, h, 0, 0))]
        args += [c0, n0, m0]
    out_specs = [pl.BlockSpec((n, L), lambda b, h: (b, h))]
    out_shape = [jax.ShapeDtypeStruct((nseq * n, B_WIDTH), F32)]
    if emit_state:
        out_specs += [pl.BlockSpec((1, 2, 1, L, L), lambda b, h: (b, 0, h, 0, 0)),
                      pl.BlockSpec((1, 1, 2, L), lambda b, h: (b, h, 0, 0)),
                      pl.BlockSpec((1, 1, 2, L), lambda b, h: (b, h, 0, 0))]
        out_shape += [jax.ShapeDtypeStruct((nseq, 2, H, L, L), F32),
                      jax.ShapeDtypeStruct((nseq, H, 2, L), F32),
                      jax.ShapeDtypeStruct((nseq, H, 2, L), F32)]
    return pl.pallas_call(
        functools.partial(_mlstm_kernel, n=n, has_init=init is not None, emit_state=emit_state),
        grid=(nseq, H),
        in_specs=in_specs,
        out_specs=out_specs,
        out_shape=out_shape,
        scratch_shapes=[pltpu.VMEM((2, n, L), F32),
                        pltpu.VMEM((nc, 8, L), F32),
                        pltpu.VMEM((n, L), F32),
                        pltpu.VMEM((L, L), F32),
                        pltpu.VMEM((2, L, L), F32),
                        pltpu.VMEM((8, L), F32)],
        compiler_params=_cparams(("arbitrary", "arbitrary")),
        name="mlstm_init" if init is not None else "mlstm",
    )(*args)


def _router(logits):
    lane = _lane(logits.shape).astype(F32)
    big = 1e9
    gl = jnp.where(lane < N_GROUPS, logits, -jnp.inf)
    gmax = gl.max(axis=-1, keepdims=True)
    g_sel = jnp.where(gl == gmax, lane, big).min(axis=-1, keepdims=True)
    g_prob = 1.0 / jnp.exp(gl - gmax).sum(axis=-1, keepdims=True)
    lo = N_GROUPS + EXPERTS_PER_GROUP * g_sel
    el = jnp.where(lane >= lo, jnp.where(lane < lo + EXPERTS_PER_GROUP, logits, -jnp.inf), -jnp.inf)
    v1 = el.max(axis=-1, keepdims=True)
    i1 = jnp.where(el == v1, lane, big).min(axis=-1, keepdims=True)
    el2 = jnp.where(lane == i1, -jnp.inf, el)
    v2 = el2.max(axis=-1, keepdims=True)
    i2 = jnp.where(el2 == v2, lane, big).min(axis=-1, keepdims=True)
    e2 = jnp.exp(v2 - v1)
    w1 = g_prob / (1.0 + e2)
    w2 = g_prob * e2 / (1.0 + e2)
    return jnp.where(lane == 0, i1 - N_GROUPS,
                     jnp.where(lane == 1, i2 - N_GROUPS,
                               jnp.where(lane == 2, w1, jnp.where(lane == 3, w2, 0.0))))


def _post_kernel(*refs, n_in):
    x_ref = refs[0]
    a_refs = refs[1:1 + n_in]
    w_refs = refs[1 + n_in:1 + 2 * n_in]
    mod_ref, g_ref, wrh_ref, wrl_ref, br_ref, xnew_ref, xn_ref, rt_ref = refs[1 + 2 * n_in:]
    tm = x_ref.shape[0]
    r = _mod_row(pl.program_id(0), tm)
    acc = _dot(a_refs[0][...].astype(BF16), w_refs[0][...])
    for a_ref, w_ref in zip(a_refs[1:], w_refs[1:]):
        acc = acc + _dot(a_ref[...].astype(BF16), w_ref[...])
    xnew = x_ref[...] + _mod_part(mod_ref, r, 2) * acc
    xnew_ref[...] = xnew
    xn = _rms(xnew, g_ref[...]) * (1.0 + _mod_part(mod_ref, r, 4)) + _mod_part(mod_ref, r, 3)
    xn_ref[...] = xn
    x_hi, x_lo = _split2(xn)
    logits = _dot(x_hi, wrh_ref[...]) + _dot(x_lo, wrh_ref[...]) + _dot(x_hi, wrl_ref[...]) + br_ref[...]
    rt_ref[...] = _router(logits)


def _post(x, a_list, w_list, mod, g, wrh, wrl, br):
    t = x.shape[0]
    tm = TOK_BLOCK
    full = lambda a: pl.BlockSpec(a.shape, lambda i: (0,) * a.ndim)
    row = lambda a: pl.BlockSpec((tm, a.shape[1]), lambda i: (i, 0))
    return pl.pallas_call(
        functools.partial(_post_kernel, n_in=len(a_list)),
        grid=(t // tm,),
        in_specs=[row(x)] + [row(a) for a in a_list] + [full(w) for w in w_list]
                 + [full(mod), full(g), full(wrh), full(wrl), full(br)],
        out_specs=[pl.BlockSpec((tm, D_MODEL), lambda i: (i, 0)),
                   pl.BlockSpec((tm, D_MODEL), lambda i: (i, 0)),
                   pl.BlockSpec((tm, LANES), lambda i: (i, 0))],
        out_shape=[jax.ShapeDtypeStruct((t, D_MODEL), F32),
                   jax.ShapeDtypeStruct((t, D_MODEL), F32),
                   jax.ShapeDtypeStruct((t, LANES), F32)],
        compiler_params=_cparams(("arbitrary",)),
        name="post_mixer_router",
    )(x, *a_list, *w_list, mod, g, wrh, wrl, br)


def _expert_kernel(be_ref, st_ref, xn_hbm, w1_ref, w3_ref, w2_ref, o_ref, xbuf, sem, w1b, w3b, w2b):
    i = pl.program_id(0)
    nblk = pl.num_programs(0)
    slot = i % 2

    def issue(blk, s):
        def body(r, carry):
            tok = st_ref[blk * MOE_BLOCK + r]
            pltpu.make_async_copy(xn_hbm.at[pl.ds(tok, 1), :], xbuf.at[s, pl.ds(r, 1), :], sem.at[s]).start()
            return carry
        lax.fori_loop(0, MOE_BLOCK, body, 0, unroll=8)

    @pl.when(i == 0)
    def _():
        issue(0, 0)

    @pl.when(i + 1 < nblk)
    def _():
        issue(i + 1, 1 - slot)

    changed = jnp.logical_or(i == 0, be_ref[i] != be_ref[jnp.maximum(i - 1, 0)])

    @pl.when(changed)
    def _():
        w1b[...] = w1_ref[0].astype(BF16)
        w3b[...] = w3_ref[0].astype(BF16)
        w2b[...] = w2_ref[0].astype(BF16)

    pltpu.make_async_copy(xn_hbm.at[pl.ds(0, MOE_BLOCK), :], xbuf.at[slot], sem.at[slot]).wait()
    x = xbuf[slot].astype(BF16)
    h1 = _dot(x, w1b[...])
    h3 = _dot(x, w3b[...])
    hid = (h1 * jax.nn.sigmoid(h1)) * h3
    o_ref[...] = _dot(hid.astype(BF16), w2b[...])


def _experts(block_expert, slot_tok, xn, w1, w3, w2):
    nblk = block_expert.shape[0]
    grid_spec = pltpu.PrefetchScalarGridSpec(
        num_scalar_prefetch=2,
        grid=(nblk,),
        in_specs=[pl.BlockSpec(memory_space=pl.ANY),
                  pl.BlockSpec((1, D_MODEL, D_EXPERT), lambda i, be, st: (be[i], 0, 0)),
                  pl.BlockSpec((1, D_MODEL, D_EXPERT), lambda i, be, st: (be[i], 0, 0)),
                  pl.BlockSpec((1, D_EXPERT, D_MODEL), lambda i, be, st: (be[i], 0, 0))],
        out_specs=pl.BlockSpec((MOE_BLOCK, D_MODEL), lambda i, be, st: (i, 0)),
        scratch_shapes=[pltpu.VMEM((2, MOE_BLOCK, D_MODEL), F32),
                        pltpu.SemaphoreType.DMA((2,)),
                        pltpu.VMEM((D_MODEL, D_EXPERT), BF16),
                        pltpu.VMEM((D_MODEL, D_EXPERT), BF16),
                        pltpu.VMEM((D_EXPERT, D_MODEL), BF16)])
    return pl.pallas_call(
        _expert_kernel,
        grid_spec=grid_spec,
        out_shape=jax.ShapeDtypeStruct((nblk * MOE_BLOCK, D_MODEL), F32),
        compiler_params=_cparams(("arbitrary",)),
        name="moe_experts",
    )(block_expert, slot_tok, xn, w1, w3, w2)


_COMBINE_BLOCK = 128


def _combine_kernel(dest_ref, yb_hbm, x_ref, rt_ref, mod_ref, gfin_ref, o_ref, ybuf, sem, *, final_norm):
    i = pl.program_id(0)
    nblk = pl.num_programs(0)
    slot = i % 2
    tm = _COMBINE_BLOCK

    def issue(blk, s):
        def body(j, carry):
            for c in range(2):
                d = dest_ref[2 * (blk * tm + j) + c]
                pltpu.make_async_copy(yb_hbm.at[pl.ds(d, 1), :], ybuf.at[s, pl.ds(c * tm + j, 1), :],
                                      sem.at[s]).start()
            return carry
        lax.fori_loop(0, tm, body, 0, unroll=4)

    @pl.when(i == 0)
    def _():
        issue(0, 0)

    @pl.when(i + 1 < nblk)
    def _():
        issue(i + 1, 1 - slot)

    pltpu.make_async_copy(yb_hbm.at[pl.ds(0, 2 * tm), :], ybuf.at[slot], sem.at[slot]).wait()
    r = _mod_row(i, tm)
    rt = rt_ref[...]
    y = rt[:, 2:3] * ybuf[slot, 0:tm, :] + rt[:, 3:4] * ybuf[slot, tm:2 * tm, :]
    out = x_ref[...] + _mod_part(mod_ref, r, 5) * y
    if final_norm:
        out = _rms(out, gfin_ref[...])
    o_ref[...] = out


def _combine(dest, yb, x, rt, mod, gfin, final_norm):
    t = x.shape[0]
    tm = _COMBINE_BLOCK
    grid_spec = pltpu.PrefetchScalarGridSpec(
        num_scalar_prefetch=1,
        grid=(t // tm,),
        in_specs=[pl.BlockSpec(memory_space=pl.ANY),
                  pl.BlockSpec((tm, D_MODEL), lambda i, d: (i, 0)),
                  pl.BlockSpec((tm, LANES), lambda i, d: (i, 0)),
                  pl.BlockSpec(mod.shape, lambda i, d: (0, 0)),
                  pl.BlockSpec(gfin.shape, lambda i, d: (0, 0))],
        out_specs=pl.BlockSpec((tm, D_MODEL), lambda i, d: (i, 0)),
        scratch_shapes=[pltpu.VMEM((2, 2 * tm, D_MODEL), F32),
                        pltpu.SemaphoreType.DMA((2,))])
    return pl.pallas_call(
        functools.partial(_combine_kernel, final_norm=final_norm),
        grid_spec=grid_spec,
        out_shape=jax.ShapeDtypeStruct((t, D_MODEL), F32),
        compiler_params=_cparams(("arbitrary",)),
        name="moe_combine",
    )(dest, yb, x, rt, mod, gfin)


def _moe_plan(rt):
    t = rt.shape[0]
    eid = rt[:, 0:2].astype(jnp.int32).reshape(-1)
    onehot = (eid[:, None] == jnp.arange(N_EXPERTS, dtype=jnp.int32)[None, :]).astype(jnp.int32)
    csum = jnp.cumsum(onehot, axis=0)
    rank = jnp.take_along_axis(csum, eid[:, None], axis=1)[:, 0] - 1
    counts = csum[-1]
    padded = (counts + MOE_BLOCK - 1) // MOE_BLOCK * MOE_BLOCK
    seg_end = jnp.cumsum(padded)
    seg_start = seg_end - padded
    dest = (seg_start[eid] + rank).astype(jnp.int32)
    n_blocks = (2 * t + N_EXPERTS * (MOE_BLOCK - 1) + MOE_BLOCK - 1) // MOE_BLOCK
    tok = jnp.repeat(jnp.arange(t, dtype=jnp.int32), 2)
    slot_tok = jnp.zeros((n_blocks * MOE_BLOCK,), jnp.int32).at[dest].set(tok)
    block_expert = jnp.minimum(
        jnp.searchsorted(seg_end, jnp.arange(n_blocks, dtype=jnp.int32) * MOE_BLOCK, side='right'),
        N_EXPERTS - 1).astype(jnp.int32)
    return dest, slot_tok, block_expert


def _attn_c_prompt_kernel(q_ref, k_ref, v_ref, o_ref):
    for p in range(C_HEADS // 2):
        sl = slice(p * LANES, (p + 1) * LANES)
        kb = k_ref[:, sl].astype(BF16)
        vb = v_ref[:, sl].astype(BF16)
        qp = q_ref[:, sl] * (C_HEAD_DIM ** -0.5)
        o_ref[:, sl] = _softmax_pair(qp, lambda qm: [_dot_nt(qm, kb)], lambda ps: _dot(ps[0], vb))


def _attn_c_prompt(qkv):
    blk = lambda part: pl.BlockSpec((SEQ, C_WIDTH), lambda b: (b, part))
    return pl.pallas_call(
        _attn_c_prompt_kernel,
        grid=(BATCH,),
        in_specs=[blk(0), blk(1), blk(2)],
        out_specs=pl.BlockSpec((SEQ, C_WIDTH), lambda b: (b, 0)),
        out_shape=jax.ShapeDtypeStruct((T_PROMPT, C_WIDTH), F32),
        compiler_params=_cparams(("arbitrary",)),
        name="attn_c_prompt",
    )(qkv, qkv, qkv)


def _na_key_start(r0):
    rows = DEC_SEQ // GRID_W
    return jnp.minimum(jnp.clip(r0 - NA_ROWS // 2, 0, rows - NA_ROWS), rows - NA_KROWS)


def _attn_c_sample_kernel(q_ref, k_ref, v_ref, ck_ref, cv_ref, bias_ref, o_ref):
    rows = DEC_SEQ // GRID_W
    nblk = rows // NA_QROWS
    i = pl.program_id(2)
    r0 = i * NA_QROWS
    k0 = pl.multiple_of(_na_key_start(r0) * GRID_W, GRID_W)
    btype = jnp.where(i == 0, 0, jnp.where(i == nblk - 1, 2, 1))
    nk = NA_KROWS * GRID_W
    kw = k_ref[pl.ds(k0, nk), :].astype(BF16)
    vw = v_ref[pl.ds(k0, nk), :].astype(BF16)
    kc = ck_ref[0].astype(BF16)
    vc = cv_ref[0].astype(BF16)
    qp = q_ref[...] * (C_HEAD_DIM ** -0.5)
    lo = _lane(qp.shape) < 64
    outs = []
    for half in range(2):
        qm = jnp.where(lo if half == 0 else jnp.logical_not(lo), qp, 0.0).astype(BF16)
        s_win = _dot_nt(qm, kw) + bias_ref[0, btype, half]
        s_ctx = _dot_nt(qm, kc)
        m = jnp.maximum(s_win.max(axis=-1, keepdims=True), s_ctx.max(axis=-1, keepdims=True))
        e_win = jnp.exp(s_win - m)
        e_ctx = jnp.exp(s_ctx - m)
        l = e_win.sum(axis=-1, keepdims=True) + e_ctx.sum(axis=-1, keepdims=True)
        o = _dot(e_win.astype(BF16), vw) + _dot(e_ctx.astype(BF16), vc)
        outs.append(o / l)
    o_ref[...] = jnp.where(lo, outs[0], outs[1])


def _attn_c_sample(qkv, ck, cv, bias):
    rows = DEC_SEQ // GRID_W
    nblk = rows // NA_QROWS
    qrows = NA_QROWS * GRID_W
    npair = C_HEADS // 2
    offb = T_PROMPT // DEC_SEQ
    offq = T_PROMPT // qrows
    return pl.pallas_call(
        _attn_c_sample_kernel,
        grid=(npair, DEC_BATCH, nblk),
        in_specs=[pl.BlockSpec((qrows, LANES), lambda p, b, i: (offq + b * nblk + i, p)),
                  pl.BlockSpec((DEC_SEQ, LANES), lambda p, b, i: (offb + b, npair + p)),
                  pl.BlockSpec((DEC_SEQ, LANES), lambda p, b, i: (offb + b, 2 * npair + p)),
                  pl.BlockSpec((1, PAST_LEN, LANES), lambda p, b, i: (b, 0, p)),
                  pl.BlockSpec((1, PAST_LEN, LANES), lambda p, b, i: (b, 0, p)),
                  pl.BlockSpec((1, 3, 2, qrows, NA_KROWS * GRID_W), lambda p, b, i: (p, 0, 0, 0, 0))],
        out_specs=pl.BlockSpec((qrows, LANES), lambda p, b, i: (b * nblk + i, p)),
        out_shape=jax.ShapeDtypeStruct((T_SAMPLE, C_WIDTH), F32),
        compiler_params=_cparams(("arbitrary", "arbitrary", "arbitrary")),
        name="attn_c_sample",
    )(qkv, qkv, qkv, ck, cv, bias)


def _na_bias_tables(rpb):
    rows = DEC_SEQ // GRID_W
    nblk = rows // NA_QROWS
    c = np.arange(GRID_W)
    cs = np.clip(c - NA_COLS // 2, 0, GRID_W - NA_COLS)
    kc = np.arange(GRID_W)
    col_ok = (kc[None, :] >= cs[:, None]) & (kc[None, :] < cs[:, None] + NA_COLS)
    dcol = np.clip(kc[None, :] - c[:, None] + NA_COLS - 1, 0, 2 * NA_COLS - 2)
    drow = np.zeros((3, NA_QROWS, NA_KROWS), np.int32)
    row_ok = np.zeros((3, NA_QROWS, NA_KROWS), bool)
    for t, blk in enumerate((0, 1, nblk - 1)):
        r0 = blk * NA_QROWS
        ks = min(int(np.clip(r0 - NA_ROWS // 2, 0, rows - NA_ROWS)), rows - NA_KROWS)
        for i in range(NA_QROWS):
            r = r0 + i
            rs = int(np.clip(r - NA_ROWS // 2, 0, rows - NA_ROWS))
            for j in range(NA_KROWS):
                kr = ks + j
                row_ok[t, i, j] = rs <= kr < rs + NA_ROWS
                drow[t, i, j] = int(np.clip(kr - r + NA_ROWS - 1, 0, 2 * NA_ROWS - 2))
    toe = rpb[:, :, dcol]
    toe = jnp.where(col_ok[None, None], toe, NEG)
    tab = toe[:, drow]
    tab = jnp.where(row_ok[None, :, :, :, None, None], tab, NEG)
    tab = tab.transpose(0, 1, 2, 4, 3, 5).reshape(C_HEADS, 3, NA_QROWS * GRID_W, NA_KROWS * GRID_W)
    return tab.reshape(C_HEADS // 2, 2, 3, NA_QROWS * GRID_W, NA_KROWS * GRID_W).transpose(0, 2, 1, 3, 4)


def _rope_tables():
    half = A_HEAD_DIM // 2
    t = jnp.arange(DEC_SEQ)
    row = (t // GRID_W).astype(F32)
    colp = (t % GRID_W).astype(F32)
    freqs = 1.0 / (ROPE_BASE ** (jnp.arange(0, half, 2, dtype=F32) / half))
    d = np.arange(LANES) % A_HEAD_DIM
    pos = jnp.where(jnp.asarray(d < half)[None, :], row[:, None], colp[:, None])
    ang = pos * freqs[d % (half // 2)][None, :]
    sign = jnp.asarray(np.where((d % half) < half // 2, -1.0, 1.0), F32)[None, :]
    return jnp.cos(ang), jnp.sin(ang) * sign


def _head_avg_matrix():
    idx = np.arange(LANES) // A_HEAD_DIM
    return jnp.asarray((idx[:, None] == idx[None, :]).astype(np.float32) / A_HEAD_DIM, BF16)


def _tri_matrices():
    i = np.arange(MLSTM_CHUNK)
    upper = (i[:, None] <= i[None, :]).astype(np.float32)
    lower = (i[:, None] >= i[None, :]).astype(np.float32)
    return jnp.asarray(np.stack([upper, lower]), BF16)


def _router_weights(wg, bg, we, be):
    w = jnp.zeros((D_MODEL, LANES), F32).at[:, :N_GROUPS].set(wg).at[:, N_GROUPS:N_GROUPS + N_EXPERTS].set(we)
    b = jnp.zeros((1, LANES), F32).at[0, :N_GROUPS].set(bg).at[0, N_GROUPS:N_GROUPS + N_EXPERTS].set(be)
    hi = w.astype(BF16)
    lo = (w - hi.astype(F32)).astype(BF16)
    return hi, lo, b


def _moe(x_new, xn, rt, mod_l, w1, w3, w2, gfin, final_norm):
    dest, slot_tok, block_expert = _moe_plan(rt)
    yb = _experts(block_expert, slot_tok, xn, w1, w3, w2)
    return _combine(dest, yb, x_new, rt, mod_l, gfin, final_norm)


def kernel(x_prompt, x_sample, cache_attn_k, cache_attn_v, state_mlstm_C, state_mlstm_n, state_mlstm_m,
           cache_na_k, cache_na_v, c, c_ctx, norm_mix, norm_ffn, norm_final, ada_w, ada_b,
           ab_w_in, ab_w_out, ab_q_norm, ab_k_norm, ab_gate_bias, ab_out_norm,
           na_w_in, na_w_out, na_rpb, moe_wg, moe_bg, moe_we, moe_be, moe_w1, moe_w3, moe_w2):
    x = jnp.concatenate([x_prompt.reshape(T_PROMPT, D_MODEL), x_sample.reshape(T_SAMPLE, D_MODEL)], axis=0)
    cond = jnp.zeros((N_COND, D_MODEL), F32).at[0].set(c_ctx).at[1:1 + DEC_BATCH].set(c)
    mod = _modulation(cond, ada_w, ada_b)
    gfin = norm_final.reshape(1, D_MODEL)

    w_in = ab_w_in[0]
    o_aq, o_ak, o_av, o_bq, o_bk, o_bv, o_bo, o_bg = np.cumsum((0,) + (A_WIDTH, A_KV_WIDTH, A_KV_WIDTH,
                                                                       B_WIDTH, B_WIDTH, B_WIDTH, B_WIDTH))
    wb = w_in[:, o_bq:o_bg].astype(BF16)
    wq = w_in[:, o_aq:o_ak].astype(BF16)
    wkv = w_in[:, o_ak:o_bq].astype(BF16)
    wg = w_in[:, o_bg:o_bg + 4 * B_HEADS].reshape(D_MODEL, 4, B_HEADS)
    wgt = jnp.zeros((B_HEADS, 8, D_MODEL), F32).at[:, :4, :].set(wg.transpose(2, 1, 0))
    wgt = wgt.reshape(8 * B_HEADS, D_MODEL).astype(BF16)
    g_mix = norm_mix[0].reshape(1, D_MODEL)
    ob, oq, okv, ogt = _proj_ab(x, mod[0], g_mix, wb, wq, wkv, wgt)

    qn = jnp.tile(ab_q_norm[0], 2).reshape(1, LANES)
    kn = jnp.tile(ab_k_norm[0], 2).reshape(1, LANES)
    bd = _head_avg_matrix()
    cos, sin = _rope_tables()
    a_p, new_k, new_v = _attn_a_prompt(oq, okv, qn, kn, bd)
    ck = cache_attn_k[:, 0].reshape(DEC_BATCH, PAST_LEN, A_KV_WIDTH)
    cv = cache_attn_v[:, 0].reshape(DEC_BATCH, PAST_LEN, A_KV_WIDTH)
    a_s = _attn_a_sample(oq, okv, ck, cv, cos, sin, qn, kn, bd)
    a_out = jnp.concatenate([a_p, a_s], axis=0)

    gb = ab_gate_bias[0]
    brow = jnp.zeros((B_HEADS, 8, LANES), F32).at[:, :4, :].set(
        jnp.broadcast_to(gb.T[:, :, None], (B_HEADS, 4, LANES)))
    onorm = ab_out_norm[0].reshape(B_HEADS, 1, B_HEAD_DIM)
    tri = _tri_matrices()
    ogt_h = ogt
    b_p, cT, nT, mT = _mlstm(ob, ogt_h, brow, onorm, tri, n=SEQ, seq0=0, nseq=BATCH, emit_state=True)
    c0 = state_mlstm_C[:, 0]
    n0 = state_mlstm_n[:, 0].transpose(0, 2, 1, 3)
    m0 = jnp.broadcast_to(state_mlstm_m[:, 0].transpose(0, 2, 1)[..., None], (DEC_BATCH, B_HEADS, 2, LANES))
    (b_s,) = _mlstm(ob, ogt_h, brow, onorm, tri, n=DEC_SEQ, seq0=T_PROMPT // DEC_SEQ, nseq=DEC_BATCH,
                    init=(c0, n0, m0))
    b_out = jnp.concatenate([b_p, b_s], axis=0)

    w_out = ab_w_out[0].astype(BF16)
    wrh, wrl, br = _router_weights(moe_wg[0], moe_bg[0], moe_we[0], moe_be[0])
    x1, xn, rt = _post(x, [a_out, b_out], [w_out[:A_WIDTH], w_out[A_WIDTH:]], mod[0],
                       norm_ffn[0].reshape(1, D_MODEL), wrh, wrl, br)
    x = _moe(x1, xn, rt, mod[0], moe_w1[0], moe_w3[0], moe_w2[0], gfin, False)

    qkv = _proj_na(x, mod[1], norm_mix[1].reshape(1, D_MODEL), na_w_in[0].astype(BF16))
    o_p = _attn_c_prompt(qkv)
    nck = cache_na_k[:, 0].reshape(DEC_BATCH, PAST_LEN, C_WIDTH)
    ncv = cache_na_v[:, 0].reshape(DEC_BATCH, PAST_LEN, C_WIDTH)
    o_s = _attn_c_sample(qkv, nck, ncv, _na_bias_tables(na_rpb[0]))
    o = jnp.concatenate([o_p, o_s], axis=0)
    wrh, wrl, br = _router_weights(moe_wg[1], moe_bg[1], moe_we[1], moe_be[1])
    x1, xn, rt = _post(x, [o], [na_w_out[0].astype(BF16)], mod[1], norm_ffn[1].reshape(1, D_MODEL), wrh, wrl, br)
    y = _moe(x1, xn, rt, mod[1], moe_w1[1], moe_w3[1], moe_w2[1], gfin, True)

    y_prompt = y[:T_PROMPT].reshape(BATCH, SEQ, D_MODEL)
    y_sample = y[T_PROMPT:].reshape(DEC_BATCH, DEC_SEQ, D_MODEL)
    new_attn_k = new_k.reshape(BATCH, 1, SEQ, A_KV_HEADS, A_HEAD_DIM)
    new_attn_v = new_v.reshape(BATCH, 1, SEQ, A_KV_HEADS, A_HEAD_DIM)
    new_mlstm_C = cT[:, None]
    new_mlstm_n = nT.transpose(0, 2, 1, 3)[:, None]
    new_mlstm_m = mT[..., 0].transpose(0, 2, 1)[:, None]
    new_na_k = qkv[:T_PROMPT, C_WIDTH:2 * C_WIDTH].reshape(BATCH, 1, SEQ, C_HEADS, C_HEAD_DIM)
    new_na_v = qkv[:T_PROMPT, 2 * C_WIDTH:].reshape(BATCH, 1, SEQ, C_HEADS, C_HEAD_DIM)
    return (y_prompt, y_sample, new_attn_k, new_attn_v, new_mlstm_C, new_mlstm_n, new_mlstm_m,
            new_na_k, new_na_v)
```

```python
import functools

import numpy as np
import jax
import jax.numpy as jnp
from jax import lax
from jax.experimental import pallas as pl
from jax.experimental.pallas import tpu as pltpu

F32 = jnp.float32
BF16 = jnp.bfloat16

D_MODEL = 1024
BATCH = 32
SEQ = 256
DEC_BATCH = 4
DEC_SEQ = 2048
PAST_LEN = 256
GRID_W = 64
A_HEADS = 8
A_KV_HEADS = 2
A_HEAD_DIM = 64
A_WIDTH = A_HEADS * A_HEAD_DIM
A_KV_WIDTH = A_KV_HEADS * A_HEAD_DIM
B_HEADS = 4
B_HEAD_DIM = 128
B_WIDTH = B_HEADS * B_HEAD_DIM
MLSTM_CHUNK = 128
C_HEADS = 16
C_HEAD_DIM = 64
C_WIDTH = C_HEADS * C_HEAD_DIM
NA_ROWS = 8
NA_COLS = 16
N_GROUPS = 4
EXPERTS_PER_GROUP = 8
N_EXPERTS = N_GROUPS * EXPERTS_PER_GROUP
D_EXPERT = 512
MOE_BLOCK = 128
ROPE_BASE = 10000.0
NORM_EPS = 1e-6

T_PROMPT = BATCH * SEQ
T_SAMPLE = DEC_BATCH * DEC_SEQ
T_ALL = T_PROMPT + T_SAMPLE
N_COND = 8
LANES = 128
TOK_BLOCK = 256
EXPERT_ROWS = 256
NA_QROWS = 4
NA_KROWS = 12
NEG = -1e30
VMEM_LIMIT = 56 * 1024 * 1024


def _cparams(sem):
    return pltpu.CompilerParams(dimension_semantics=sem, vmem_limit_bytes=VMEM_LIMIT)


def _split2(x):
    hi = x.astype(BF16)
    lo = (x - hi.astype(F32)).astype(BF16)
    return hi, lo


def _split3(x):
    hi = x.astype(BF16)
    r = x - hi.astype(F32)
    mid = r.astype(BF16)
    lo = (r - mid.astype(F32)).astype(BF16)
    return hi, mid, lo


def _dot(a, b):
    return jnp.dot(a, b, preferred_element_type=F32)


def _dot_nt(a, b):
    return lax.dot_general(a, b, (((1,), (1,)), ((), ())), preferred_element_type=F32)


def _dot_exact_rhs(x, b):
    hi, mid, lo = _split3(x)
    return _dot(hi, b) + _dot(mid, b) + _dot(lo, b)


def _rms(x, g):
    ms = jnp.mean(x * x, axis=-1, keepdims=True)
    return (x * lax.rsqrt(ms + NORM_EPS)) * g


def _mod_row(tok0):
    return jnp.where(tok0 < T_PROMPT, 0, 1 + (tok0 - T_PROMPT) // DEC_SEQ)


def _mod_part(mod_ref, r, idx):
    return mod_ref[pl.ds(r, 1), idx * D_MODEL:(idx + 1) * D_MODEL]


def _head_rms(x, w, bd):
    hi, lo = _split2(x * x)
    ms = _dot(hi, bd) + _dot(lo, bd)
    return (x * lax.rsqrt(ms + NORM_EPS)) * w


def _lane(shape):
    return lax.broadcasted_iota(jnp.int32, shape, len(shape) - 1)


def _dup_half(x, g):
    xr = pltpu.roll(x, 64, 1)
    lo = _lane(x.shape) < 64
    return jnp.where(lo, x, xr) if g == 0 else jnp.where(lo, xr, x)


def _rope(x, cos, sin_signed):
    lane = _lane(x.shape)
    partner = jnp.where((lane % 32) < 16, pltpu.roll(x, LANES - 16, 1), pltpu.roll(x, 16, 1))
    return x * cos + partner * sin_signed


def _softmax_pair(qp, score_fn, value_fn):
    lo = _lane(qp.shape) < 64
    outs = []
    for half in range(2):
        qm = jnp.where(lo if half == 0 else jnp.logical_not(lo), qp, 0.0).astype(BF16)
        ss = score_fn(qm)
        m = ss[0].max(axis=-1, keepdims=True)
        for s in ss[1:]:
            m = jnp.maximum(m, s.max(axis=-1, keepdims=True))
        es = [jnp.exp(s - m) for s in ss]
        l = es[0].sum(axis=-1, keepdims=True)
        for e in es[1:]:
            l = l + e.sum(axis=-1, keepdims=True)
        o = value_fn([e.astype(BF16) for e in es])
        outs.append(o / l)
    return jnp.where(lo, outs[0], outs[1])


def _mod_kernel(cond_ref, w_ref, b_ref, o_ref):
    c = cond_ref[...]
    s = c * jax.nn.sigmoid(c)
    s_hi, s_lo = _split2(s)
    w_hi, w_lo = _split2(w_ref[0])
    o_ref[0] = _dot(s_hi, w_hi) + _dot(s_lo, w_hi) + _dot(s_hi, w_lo) + b_ref[0]


def _modulation(cond, ada_w, ada_b):
    depth, d, n = ada_w.shape
    tn = 1536
    return pl.pallas_call(
        _mod_kernel,
        grid=(depth, n // tn),
        in_specs=[pl.BlockSpec((N_COND, d), lambda l, j: (0, 0)),
                  pl.BlockSpec((1, d, tn), lambda l, j: (l, 0, j)),
                  pl.BlockSpec((1, 1, tn), lambda l, j: (l, 0, j))],
        out_specs=pl.BlockSpec((1, N_COND, tn), lambda l, j: (l, 0, j)),
        out_shape=jax.ShapeDtypeStruct((depth, N_COND, n), F32),
        compiler_params=_cparams(("arbitrary", "arbitrary")),
        name="adaln_modulation",
    )(cond, ada_w, ada_b.reshape(depth, 1, n))


def _norm_mod(x_ref, mod_ref, g_ref, shift_idx, scale_idx, row0):
    r = _mod_row(row0 + pl.program_id(0) * x_ref.shape[0])
    h = _rms(x_ref[...], g_ref[...])
    return h * (1.0 + _mod_part(mod_ref, r, scale_idx)) + _mod_part(mod_ref, r, shift_idx)


def _proj_ab_kernel(x_ref, mod_ref, g_ref, wb_ref, wq_ref, wkv_ref, wgt_ref, ob_ref, oq_ref, okv_ref, ogt_ref,
                    *, row0):
    tm = x_ref.shape[0]
    hb = _norm_mod(x_ref, mod_ref, g_ref, 0, 1, row0).astype(BF16)
    ob_ref[...] = _dot(hb, wb_ref[...])
    oq_ref[...] = _dot(hb, wq_ref[...])
    okv_ref[...] = _dot(hb, wkv_ref[...])
    gt = _dot_nt(wgt_ref[...], hb)
    for j in range(tm // LANES):
        ogt_ref[j] = gt[:, j * LANES:(j + 1) * LANES]


def _proj_ab(x, mod, g, wb, wq, wkv, wgt, row0):
    t = x.shape[0]
    tm = TOK_BLOCK
    full = lambda a: pl.BlockSpec(a.shape, lambda i: (0,) * a.ndim)
    return pl.pallas_call(
        functools.partial(_proj_ab_kernel, row0=row0),
        grid=(t // tm,),
        in_specs=[pl.BlockSpec((tm, D_MODEL), lambda i: (i, 0)), full(mod), full(g),
                  full(wb), full(wq), full(wkv), full(wgt)],
        out_specs=[pl.BlockSpec((tm, 4 * B_WIDTH), lambda i: (i, 0)),
                   pl.BlockSpec((tm, A_WIDTH), lambda i: (i, 0)),
                   pl.BlockSpec((tm, 2 * A_KV_WIDTH), lambda i: (i, 0)),
                   pl.BlockSpec((tm // LANES, 8 * B_HEADS, LANES), lambda i: (i, 0, 0))],
        out_shape=[jax.ShapeDtypeStruct((t, 4 * B_WIDTH), F32),
                   jax.ShapeDtypeStruct((t, A_WIDTH), F32),
                   jax.ShapeDtypeStruct((t, 2 * A_KV_WIDTH), F32),
                   jax.ShapeDtypeStruct((t // LANES, 8 * B_HEADS, LANES), F32)],
        compiler_params=_cparams(("arbitrary",)),
        name="proj_ab",
    )(x, mod, g, wb, wq, wkv, wgt)


def _proj_na_kernel(x_ref, mod_ref, g_ref, w_ref, q_ref, k_ref, v_ref, *, row0):
    hb = _norm_mod(x_ref, mod_ref, g_ref, 0, 1, row0).astype(BF16)
    for j, o_ref in enumerate((q_ref, k_ref, v_ref)):
        o_ref[...] = _dot(hb, w_ref[:, j * C_WIDTH:(j + 1) * C_WIDTH])


def _proj_na(x, mod, g, w, row0, t):
    tm = TOK_BLOCK
    blk0 = row0 // tm
    full = lambda a: pl.BlockSpec(a.shape, lambda i: (0,) * a.ndim)
    return pl.pallas_call(
        functools.partial(_proj_na_kernel, row0=row0),
        grid=(t // tm,),
        in_specs=[pl.BlockSpec((tm, D_MODEL), lambda i: (blk0 + i, 0)), full(mod), full(g), full(w)],
        out_specs=[pl.BlockSpec((tm, C_WIDTH), lambda i: (i, 0))] * 3,
        out_shape=[jax.ShapeDtypeStruct((t, C_WIDTH), F32)] * 3,
        compiler_params=_cparams(("arbitrary",)),
        name="proj_na",
    )(x, mod, g, w)


def _gqa_block(q_ref, qn, bd, kd_ref, vd_ref, o_ref, rope=None):
    for p in range(A_HEADS // 2):
        g = p // (A_HEADS // 2 // A_KV_HEADS)
        qp = _head_rms(q_ref[:, p * LANES:(p + 1) * LANES], qn, bd)
        if rope is not None:
            qp = _rope(qp, rope[0], rope[1])
        qp = qp * (A_HEAD_DIM ** -0.5)
        o_ref[:, p * LANES:(p + 1) * LANES] = _softmax_pair(
            qp, lambda qm: [_dot_nt(qm, kd_ref[g])], lambda ps: _dot(ps[0], vd_ref[g]))


def _attn_a_prompt_kernel(q_ref, kv_ref, qn_ref, kn_ref, bd_ref, o_ref, knew_ref, vnew_ref, kd_ref, vd_ref):
    bd = bd_ref[...]
    k = _head_rms(kv_ref[:, :LANES], kn_ref[...], bd)
    v = kv_ref[:, LANES:]
    knew_ref[...] = k
    vnew_ref[...] = v
    for g in range(A_KV_HEADS):
        kd_ref[g] = _dup_half(k, g).astype(BF16)
        vd_ref[g] = _dup_half(v, g).astype(BF16)
    _gqa_block(q_ref, qn_ref[...], bd, kd_ref, vd_ref, o_ref)


def _attn_a_prompt(q, kv, qn, kn, bd):
    nb = BATCH
    full = lambda a: pl.BlockSpec(a.shape, lambda b: (0,) * a.ndim)
    return pl.pallas_call(
        _attn_a_prompt_kernel,
        grid=(nb,),
        in_specs=[pl.BlockSpec((SEQ, A_WIDTH), lambda b: (b, 0)),
                  pl.BlockSpec((SEQ, 2 * A_KV_WIDTH), lambda b: (b, 0)),
                  full(qn), full(kn), full(bd)],
        out_specs=[pl.BlockSpec((SEQ, A_WIDTH), lambda b: (b, 0)),
                   pl.BlockSpec((SEQ, A_KV_WIDTH), lambda b: (b, 0)),
                   pl.BlockSpec((SEQ, A_KV_WIDTH), lambda b: (b, 0))],
        out_shape=[jax.ShapeDtypeStruct((T_PROMPT, A_WIDTH), F32),
                   jax.ShapeDtypeStruct((T_PROMPT, A_KV_WIDTH), F32),
                   jax.ShapeDtypeStruct((T_PROMPT, A_KV_WIDTH), F32)],
        scratch_shapes=[pltpu.VMEM((A_KV_HEADS, SEQ, LANES), BF16),
                        pltpu.VMEM((A_KV_HEADS, SEQ, LANES), BF16)],
        compiler_params=_cparams(("arbitrary",)),
        name="attn_a_prompt",
    )(q, kv, qn, kn, bd)


_A_QBLOCK = 256


def _attn_a_sample_kernel(q_ref, kv_ref, ck_ref, cv_ref, cos_ref, sin_ref, cosq_ref, sinq_ref,
                          qn_ref, kn_ref, bd_ref, o_ref, kd_ref, vd_ref):
    bd = bd_ref[...]

    @pl.when(pl.program_id(1) == 0)
    def _():
        for g in range(A_KV_HEADS):
            kd_ref[g, :PAST_LEN] = _dup_half(ck_ref[0], g).astype(BF16)
            vd_ref[g, :PAST_LEN] = _dup_half(cv_ref[0], g).astype(BF16)
        rows = 256
        for c in range(DEC_SEQ // rows):
            sl = slice(c * rows, (c + 1) * rows)
            k = _head_rms(kv_ref[sl, :LANES], kn_ref[...], bd)
            k = _rope(k, cos_ref[sl, :], sin_ref[sl, :])
            v = kv_ref[sl, LANES:]
            dst = slice(PAST_LEN + c * rows, PAST_LEN + (c + 1) * rows)
            for g in range(A_KV_HEADS):
                kd_ref[g, dst] = _dup_half(k, g).astype(BF16)
                vd_ref[g, dst] = _dup_half(v, g).astype(BF16)

    _gqa_block(q_ref, qn_ref[...], bd, kd_ref, vd_ref, o_ref, rope=(cosq_ref[...], sinq_ref[...]))


def _attn_a_sample(q, kv, ck, cv, cos, sin, qn, kn, bd):
    nq = DEC_SEQ // _A_QBLOCK
    full = lambda a: pl.BlockSpec(a.shape, lambda b, i: (0,) * a.ndim)
    tk = PAST_LEN + DEC_SEQ
    return pl.pallas_call(
        _attn_a_sample_kernel,
        grid=(DEC_BATCH, nq),
        in_specs=[pl.BlockSpec((_A_QBLOCK, A_WIDTH), lambda b, i: (b * nq + i, 0)),
                  pl.BlockSpec((DEC_SEQ, 2 * A_KV_WIDTH), lambda b, i: (b, 0)),
                  pl.BlockSpec((1, PAST_LEN, A_KV_WIDTH), lambda b, i: (b, 0, 0)),
                  pl.BlockSpec((1, PAST_LEN, A_KV_WIDTH), lambda b, i: (b, 0, 0)),
                  full(cos), full(sin),
                  pl.BlockSpec((_A_QBLOCK, LANES), lambda b, i: (i, 0)),
                  pl.BlockSpec((_A_QBLOCK, LANES), lambda b, i: (i, 0)),
                  full(qn), full(kn), full(bd)],
        out_specs=pl.BlockSpec((_A_QBLOCK, A_WIDTH), lambda b, i: (b * nq + i, 0)),
        out_shape=jax.ShapeDtypeStruct((T_SAMPLE, A_WIDTH), F32),
        scratch_shapes=[pltpu.VMEM((A_KV_HEADS, tk, LANES), BF16),
                        pltpu.VMEM((A_KV_HEADS, tk, LANES), BF16)],
        compiler_params=_cparams(("arbitrary", "arbitrary")),
        name="attn_a_sample",
    )(q, kv, ck, cv, cos, sin, cos, sin, qn, kn, bd)


def _log_sigmoid(x):
    return -(jnp.maximum(-x, 0.0) + jnp.log1p(jnp.exp(-jnp.abs(x))))


def _mlstm_kernel(*refs, n, has_init, emit_state):
    it = iter(refs)
    q_ref, k_ref, v_ref, og_ref, gt_ref, brow_ref, onorm_ref, tri_ref = [next(it) for _ in range(8)]
    if has_init:
        c0_ref, n0_ref, m0_ref = [next(it) for _ in range(3)]
    out_ref = next(it)
    if emit_state:
        cT_ref, nT_ref, mT_ref = [next(it) for _ in range(3)]
    h_ref, row_ref, col_ref, tbuf_ref, c_ref, n_ref = [next(it) for _ in range(6)]

    L = MLSTM_CHUNK
    nc = n // L

    gt = gt_ref[...] + brow_ref[0][None]
    lf = _log_sigmoid(gt).reshape(nc * 8, L)
    cum_f = _dot_exact_rhs(lf, tri_ref[0]).reshape(nc, 8, L)
    cum_b = _dot_exact_rhs(lf, tri_ref[1]).reshape(nc, 8, L)
    cf = cum_f[:, 1:2, :]
    cb = cum_b[:, 3:4, :]
    row_ref[:, 0:1, :] = cf
    row_ref[:, 1:2, :] = gt[:, 0:1, :] - cf
    row_ref[:, 2:3, :] = cb
    row_ref[:, 3:4, :] = gt[:, 2:3, :] - cb
    lf3 = lf.reshape(nc, 8, L)
    row_ref[:, 4:5, :] = jnp.broadcast_to(lf3[:, 1:2, :].sum(axis=-1, keepdims=True), (nc, 1, L))
    row_ref[:, 5:6, :] = jnp.broadcast_to(lf3[:, 3:4, :].sum(axis=-1, keepdims=True), (nc, 1, L))
    row_ref[:, 6:8, :] = jnp.zeros((nc, 2, L), F32)

    tbuf_ref[...] = jnp.zeros((L, L), F32)

    def to_cols(c, carry):
        tbuf_ref[0:8, :] = row_ref[c]
        col_ref[pl.ds(pl.multiple_of(c * L, L), L), :] = tbuf_ref[...].T
        return carry

    lax.fori_loop(0, nc, to_cols, 0)

    if has_init:
        c_ref[...] = c0_ref[0, :, 0]
        n_ref[0:2, :] = n0_ref[0, 0]
        m_init = (m0_ref[0, 0, 0:1, 0:1], m0_ref[0, 0, 1:2, 0:1])
    else:
        c_ref[...] = jnp.zeros((2, L, L), F32)
        n_ref[...] = jnp.zeros((8, L), F32)
        m_init = (jnp.zeros((1, 1), F32), jnp.zeros((1, 1), F32))

    t_idx = lax.broadcasted_iota(jnp.int32, (L, L), 0)
    s_idx = lax.broadcasted_iota(jnp.int32, (L, L), 1)
    masks = (s_idx <= t_idx, s_idx >= t_idx)

    def step(c, d, m):
        r0 = pl.multiple_of(c * L, L)
        q = q_ref[pl.ds(r0, L), :]
        ks = k_ref[pl.ds(r0, L), :] * (B_HEAD_DIM ** -0.5)
        vb = v_ref[pl.ds(r0, L), :].astype(BF16)
        qb = q.astype(BF16)
        rows = row_ref[c]
        cols = col_ref[pl.ds(r0, L), :]
        a_row = rows[2 * d + 1:2 * d + 2, :]
        tot = rows[4 + d:5 + d, 0:1]
        cum_col = cols[:, 2 * d:2 * d + 1]
        a_col = cols[:, 2 * d + 1:2 * d + 2]
        dlog = jnp.where(masks[d], cum_col + a_row, -jnp.inf)
        inter = cum_col + m
        m_t = jnp.maximum(inter, dlog.max(axis=-1, keepdims=True))
        w_in = jnp.exp(dlog - m_t)
        w_st = jnp.exp(inter - m_t)
        a = _dot_nt(qb, ks.astype(BF16)) * w_in
        cmat = c_ref[d]
        nvec = n_ref[d:d + 1, :]
        num = w_st * _dot(qb, cmat.astype(BF16)) + _dot(a.astype(BF16), vb)
        den = w_st * (q * nvec).sum(axis=-1, keepdims=True) + a.sum(axis=-1, keepdims=True)
        hx = num / jnp.maximum(jnp.abs(den), jnp.exp(-m_t))
        if d == 0:
            h_ref[0, pl.ds(r0, L), :] = hx
        else:
            h_ref[1, pl.ds(r0, L), :] = hx
        m_new = jnp.maximum(tot + m, a_row.max(axis=-1, keepdims=True) + tot)
        ws = jnp.exp(a_col + tot - m_new)
        wc = jnp.exp(tot + m - m_new)
        kw = ks * ws
        c_ref[d] = wc * cmat + _dot(kw.T.astype(BF16), vb)
        n_ref[d:d + 1, :] = wc * nvec + kw.sum(axis=0, keepdims=True)
        return m_new

    def body(i, carry):
        return step(i, 0, carry[0]), step(nc - 1 - i, 1, carry[1])

    m_f, m_b = lax.fori_loop(0, nc, body, m_init)

    hm = h_ref[0] + h_ref[1]
    y = _rms(hm, onorm_ref[0])
    out_ref[...] = y * jax.nn.sigmoid(og_ref[...])

    if emit_state:
        cT_ref[0, :, 0] = c_ref[...]
        nT_ref[0, 0] = n_ref[0:2, :]
        mT_ref[0, 0, 0:1, :] = jnp.broadcast_to(m_f, (1, L))
        mT_ref[0, 0, 1:2, :] = jnp.broadcast_to(m_b, (1, L))


def _mlstm(ob, ogt, brow, onorm, tri, *, n, nseq, init=None, emit_state=False):
    L = MLSTM_CHUNK
    nc = n // L
    H = B_HEADS
    col = lambda part: (lambda b, h: (b, part * H + h))
    in_specs = [pl.BlockSpec((n, L), col(0)), pl.BlockSpec((n, L), col(1)), pl.BlockSpec((n, L), col(2)),
                pl.BlockSpec((n, L), col(3)),
                pl.BlockSpec((nc, 8, L), lambda b, h: (b, h, 0)),
                pl.BlockSpec((1, 8, L), lambda b, h: (h, 0, 0)),
                pl.BlockSpec((1, 1, L), lambda b, h: (h, 0, 0)),
                pl.BlockSpec(tri.shape, lambda b, h: (0, 0, 0))]
    args = [ob, ob, ob, ob, ogt, brow, onorm, tri]
    if init is not None:
        c0, n0, m0 = init
        in_specs += [pl.BlockSpec((1, 2, 1, L, L), lambda b, h: (b, 0, h, 0, 0)),
                     pl.BlockSpec((1, 1, 2, L), lambda b, h: (b, h, 0, 0)),
                     pl.BlockSpec((1, 1, 2, L), lambda b, h: (b, h, 0, 0))]
        args += [c0, n0, m0]
    out_specs = [pl.BlockSpec((n, L), lambda b, h: (b, h))]
    out_shape = [jax.ShapeDtypeStruct((nseq * n, B_WIDTH), F32)]
    if emit_state:
        out_specs += [pl.BlockSpec((1, 2, 1, L, L), lambda b, h: (b, 0, h, 0, 0)),
                      pl.BlockSpec((1, 1, 2, L), lambda b, h: (b, h, 0, 0)),
                      pl.BlockSpec((1, 1, 2, L), lambda b, h: (b, h, 0, 0))]
        out_shape += [jax.ShapeDtypeStruct((nseq, 2, H, L, L), F32),
                      jax.ShapeDtypeStruct((nseq, H, 2, L), F32),
                      jax.ShapeDtypeStruct((nseq, H, 2, L), F32)]
    return pl.pallas_call(
        functools.partial(_mlstm_kernel, n=n, has_init=init is not None, emit_state=emit_state),
        grid=(nseq, H),
        in_specs=in_specs,
        out_specs=out_specs,
        out_shape=out_shape,
        scratch_shapes=[pltpu.VMEM((2, n, L), F32),
                        pltpu.VMEM((nc, 8, L), F32),
                        pltpu.VMEM((n, L), F32),
                        pltpu.VMEM((L, L), F32),
                        pltpu.VMEM((2, L, L), F32),
                        pltpu.VMEM((8, L), F32)],
        compiler_params=_cparams(("arbitrary", "arbitrary")),
        name="mlstm_init" if init is not None else "mlstm",
    )(*args)


def _router(logits):
    lane = _lane(logits.shape).astype(F32)
    big = 1e9
    gl = jnp.where(lane < N_GROUPS, logits, -jnp.inf)
    gmax = gl.max(axis=-1, keepdims=True)
    g_sel = jnp.where(gl == gmax, lane, big).min(axis=-1, keepdims=True)
    g_prob = 1.0 / jnp.exp(gl - gmax).sum(axis=-1, keepdims=True)
    lo = N_GROUPS + EXPERTS_PER_GROUP * g_sel
    el = jnp.where(lane >= lo, jnp.where(lane < lo + EXPERTS_PER_GROUP, logits, -jnp.inf), -jnp.inf)
    v1 = el.max(axis=-1, keepdims=True)
    i1 = jnp.where(el == v1, lane, big).min(axis=-1, keepdims=True)
    el2 = jnp.where(lane == i1, -jnp.inf, el)
    v2 = el2.max(axis=-1, keepdims=True)
    i2 = jnp.where(el2 == v2, lane, big).min(axis=-1, keepdims=True)
    e2 = jnp.exp(v2 - v1)
    w1 = g_prob / (1.0 + e2)
    w2 = g_prob * e2 / (1.0 + e2)
    return i1, i2, w1, w2


def _read_tokens(refs, is_prompt):
    if len(refs) == 1:
        return refs[0][...]
    return jnp.where(is_prompt, refs[0][...], refs[1][...])


def _post_kernel(*refs, groups):
    it = iter(refs)
    tok_refs = [[next(it) for _ in range(n)] for n in groups]
    w_refs = [next(it) for _ in range(len(groups) - 1)]
    mod_ref, g_ref, wrh_ref, wrl_ref, br_ref, ls_ref, sel_ref = [next(it) for _ in range(7)]
    xnew_ref, xn_ref, rt_ref, rtt_ref, cnt_ref = [next(it) for _ in range(5)]
    run_ref = next(it)
    i = pl.program_id(0)
    tm = xnew_ref.shape[0]
    is_prompt = i * tm < T_PROMPT
    r = _mod_row(i * tm)
    acc = None
    for a_refs, w_ref in zip(tok_refs[1:], w_refs):
        d = _dot(_read_tokens(a_refs, is_prompt).astype(BF16), w_ref[...])
        acc = d if acc is None else acc + d
    xnew = _read_tokens(tok_refs[0], is_prompt) + _mod_part(mod_ref, r, 2) * acc
    xnew_ref[...] = xnew
    xn = _rms(xnew, g_ref[...]) * (1.0 + _mod_part(mod_ref, r, 4)) + _mod_part(mod_ref, r, 3)
    xn_ref[...] = xn
    x_hi, x_lo = _split2(xn)
    logits = _dot(x_hi, wrh_ref[...]) + _dot(x_lo, wrh_ref[...]) + _dot(x_hi, wrl_ref[...]) + br_ref[...]
    i1, i2, w1, w2 = _router(logits)

    @pl.when(i == 0)
    def _():
        run_ref[...] = jnp.zeros(run_ref.shape, F32)

    lane = _lane(logits.shape).astype(F32)
    member = jnp.where(lane == i1, 1.0, jnp.where(lane == i2, 1.0, 0.0))
    before = _dot(ls_ref[...], member.astype(BF16)) + run_ref[...]
    rank1 = jnp.where(lane == i1, before, 0.0).sum(axis=-1, keepdims=True)
    rank2 = jnp.where(lane == i2, before, 0.0).sum(axis=-1, keepdims=True)
    run_ref[...] = run_ref[...] + member.sum(axis=0, keepdims=True)
    cnt_ref[...] = run_ref[...]
    cols = (i1 - N_GROUPS, i2 - N_GROUPS, w1, w2, rank1, rank2)
    rt = jnp.zeros(logits.shape, F32)
    for k, c in enumerate(cols):
        rt = jnp.where(lane == k, c, rt)
    rt_ref[...] = rt
    hi, mid, lo = _split3(rt)
    sel = sel_ref[...]
    rtt_ref[...] = _dot_nt(sel, hi) + _dot_nt(sel, mid) + _dot_nt(sel, lo)


def _post(tok_ops, w_list, mod, g, wrh, wrl, br):
    t = T_ALL
    tm = TOK_BLOCK
    npb = T_PROMPT // tm
    full = lambda a: pl.BlockSpec(a.shape, lambda i: (0,) * a.ndim)
    specs, args, groups = [], [], []
    for op in tok_ops:
        if isinstance(op, tuple):
            w = op[0].shape[1]
            specs += [pl.BlockSpec((tm, w), lambda i: (jnp.minimum(i, npb - 1), 0)),
                      pl.BlockSpec((tm, w), lambda i: (jnp.maximum(i - npb, 0), 0))]
            args += list(op)
            groups.append(2)
        else:
            specs.append(pl.BlockSpec((tm, op.shape[1]), lambda i: (i, 0)))
            args.append(op)
            groups.append(1)
    idx = np.arange(tm)
    ls = jnp.asarray(idx[:, None] > idx[None, :], BF16)
    sel = jnp.asarray(np.arange(8)[:, None] == np.arange(LANES)[None, :], BF16)
    consts = [mod, g, wrh, wrl, br, ls, sel]
    return pl.pallas_call(
        functools.partial(_post_kernel, groups=tuple(groups)),
        grid=(t // tm,),
        in_specs=specs + [full(w) for w in w_list] + [full(a) for a in consts],
        out_specs=[pl.BlockSpec((tm, D_MODEL), lambda i: (i, 0)),
                   pl.BlockSpec((tm, D_MODEL), lambda i: (i, 0)),
                   pl.BlockSpec((tm, LANES), lambda i: (i, 0)),
                   pl.BlockSpec((8, tm), lambda i: (0, i)),
                   pl.BlockSpec((1, LANES), lambda i: (0, 0))],
        out_shape=[jax.ShapeDtypeStruct((t, D_MODEL), F32),
                   jax.ShapeDtypeStruct((t, D_MODEL), F32),
                   jax.ShapeDtypeStruct((t, LANES), F32),
                   jax.ShapeDtypeStruct((8, t), F32),
                   jax.ShapeDtypeStruct((1, LANES), F32)],
        scratch_shapes=[pltpu.VMEM((1, LANES), F32)],
        compiler_params=_cparams(("arbitrary",)),
        name="post_mixer_router",
    )(*args, *w_list, *consts)


def _expert_kernel(be_ref, st_ref, xn_hbm, w1_ref, w3_ref, w2_ref, o_ref, xa, xb, sem, w1b, w3b, w2b):
    i = pl.program_id(0)
    nblk = pl.num_programs(0)
    rows = EXPERT_ROWS

    def gather(blk, buf, s):
        for r in range(rows):
            tok = st_ref[blk * rows + r]
            pltpu.make_async_copy(xn_hbm.at[pl.ds(tok, 1), :], buf.at[pl.ds(r, 1), :], sem.at[s]).start()

    def wait(buf, s):
        pltpu.make_async_copy(xn_hbm.at[pl.ds(0, rows), :], buf, sem.at[s]).wait()

    @pl.when(i == 0)
    def _():
        gather(0, xa, 0)

    changed = jnp.logical_or(i == 0, be_ref[i] != be_ref[jnp.maximum(i - 1, 0)])

    @pl.when(changed)
    def _():
        w1b[...] = w1_ref[0].astype(BF16)
        w3b[...] = w3_ref[0].astype(BF16)
        w2b[...] = w2_ref[0].astype(BF16)

    nxt = jnp.minimum(i + 1, nblk - 1)

    def step(cur, s_cur, oth, s_oth):
        wait(cur, s_cur)
        gather(nxt, oth, s_oth)
        x = cur[...].astype(BF16)
        h1 = _dot(x, w1b[...])
        h3 = _dot(x, w3b[...])
        hid = (h1 * jax.nn.sigmoid(h1)) * h3
        o_ref[...] = _dot(hid.astype(BF16), w2b[...])

    @pl.when(i % 2 == 0)
    def _():
        step(xa, 0, xb, 1)

    @pl.when(i % 2 == 1)
    def _():
        step(xb, 1, xa, 0)

    @pl.when(i == nblk - 1)
    def _():
        @pl.when(i % 2 == 0)
        def _():
            wait(xb, 1)

        @pl.when(i % 2 == 1)
        def _():
            wait(xa, 0)


def _experts(block_expert, slot_tok, xn, w1, w3, w2):
    nblk = block_expert.shape[0]
    rows = EXPERT_ROWS
    grid_spec = pltpu.PrefetchScalarGridSpec(
        num_scalar_prefetch=2,
        grid=(nblk,),
        in_specs=[pl.BlockSpec(memory_space=pl.ANY),
                  pl.BlockSpec((1, D_MODEL, D_EXPERT), lambda i, be, st: (be[i], 0, 0)),
                  pl.BlockSpec((1, D_MODEL, D_EXPERT), lambda i, be, st: (be[i], 0, 0)),
                  pl.BlockSpec((1, D_EXPERT, D_MODEL), lambda i, be, st: (be[i], 0, 0))],
        out_specs=pl.BlockSpec((rows, D_MODEL), lambda i, be, st: (i, 0)),
        scratch_shapes=[pltpu.VMEM((rows, D_MODEL), F32),
                        pltpu.VMEM((rows, D_MODEL), F32),
                        pltpu.SemaphoreType.DMA((2,)),
                        pltpu.VMEM((D_MODEL, D_EXPERT), BF16),
                        pltpu.VMEM((D_MODEL, D_EXPERT), BF16),
                        pltpu.VMEM((D_EXPERT, D_MODEL), BF16)])
    return pl.pallas_call(
        _expert_kernel,
        grid_spec=grid_spec,
        out_shape=jax.ShapeDtypeStruct((nblk * rows, D_MODEL), F32),
        compiler_params=_cparams(("arbitrary",)),
        name="moe_experts",
    )(block_expert, slot_tok, xn, w1, w3, w2)


_COMBINE_BLOCK = 128


def _combine_kernel(dest_ref, yb_hbm, x_ref, rt_ref, mod_ref, gfin_ref, o_ref, ybuf, sem, *, final_norm, row0):
    i = pl.program_id(0)
    nblk = pl.num_programs(0)
    slot = i % 2
    tm = _COMBINE_BLOCK

    def issue(blk, s):
        def body(j, carry):
            for c in range(2):
                d = dest_ref[c * T_ALL + row0 + blk * tm + j]
                pltpu.make_async_copy(yb_hbm.at[pl.ds(d, 1), :], ybuf.at[s, pl.ds(c * tm + j, 1), :],
                                      sem.at[s]).start()
            return carry
        lax.fori_loop(0, tm, body, 0, unroll=4)

    @pl.when(i == 0)
    def _():
        issue(0, 0)

    @pl.when(i + 1 < nblk)
    def _():
        issue(i + 1, 1 - slot)

    pltpu.make_async_copy(yb_hbm.at[pl.ds(0, 2 * tm), :], ybuf.at[slot], sem.at[slot]).wait()
    r = _mod_row(row0 + i * tm)
    rt = rt_ref[...]
    y = rt[:, 2:3] * ybuf[slot, 0:tm, :] + rt[:, 3:4] * ybuf[slot, tm:2 * tm, :]
    out = x_ref[...] + _mod_part(mod_ref, r, 5) * y
    if final_norm:
        out = _rms(out, gfin_ref[...])
    o_ref[...] = out


def _combine(dest, yb, x, rt, mod, gfin, final_norm, row0=0, t=T_ALL):
    tm = _COMBINE_BLOCK
    blk0 = row0 // tm
    grid_spec = pltpu.PrefetchScalarGridSpec(
        num_scalar_prefetch=1,
        grid=(t // tm,),
        in_specs=[pl.BlockSpec(memory_space=pl.ANY),
                  pl.BlockSpec((tm, D_MODEL), lambda i, d: (blk0 + i, 0)),
                  pl.BlockSpec((tm, LANES), lambda i, d: (blk0 + i, 0)),
                  pl.BlockSpec(mod.shape, lambda i, d: (0, 0)),
                  pl.BlockSpec(gfin.shape, lambda i, d: (0, 0))],
        out_specs=pl.BlockSpec((tm, D_MODEL), lambda i, d: (i, 0)),
        scratch_shapes=[pltpu.VMEM((2, 2 * tm, D_MODEL), F32),
                        pltpu.SemaphoreType.DMA((2,))])
    return pl.pallas_call(
        functools.partial(_combine_kernel, final_norm=final_norm, row0=row0),
        grid_spec=grid_spec,
        out_shape=jax.ShapeDtypeStruct((t, D_MODEL), F32),
        compiler_params=_cparams(("arbitrary",)),
        name="moe_combine",
    )(dest, yb, x, rt, mod, gfin)


def _moe_plan(rtt, cnt):
    t = rtt.shape[1]
    eid = rtt[0:2].astype(jnp.int32)
    rank = rtt[4:6].astype(jnp.int32)
    counts = cnt[0, N_GROUPS:N_GROUPS + N_EXPERTS].astype(jnp.int32)
    padded = (counts + EXPERT_ROWS - 1) // EXPERT_ROWS * EXPERT_ROWS
    seg_end = jnp.cumsum(padded)
    seg_start = seg_end - padded
    experts = jnp.arange(N_EXPERTS, dtype=jnp.int32)
    start = jnp.sum(jnp.where(eid[..., None] == experts, seg_start, 0), axis=-1)
    dest = (start + rank).reshape(-1)
    n_blocks = (2 * t + N_EXPERTS * (EXPERT_ROWS - 1) + EXPERT_ROWS - 1) // EXPERT_ROWS
    tok = jnp.tile(jnp.arange(t, dtype=jnp.int32), 2)
    slot_tok = jnp.zeros((n_blocks * EXPERT_ROWS,), jnp.int32).at[dest].set(tok)
    first_row = jnp.arange(n_blocks, dtype=jnp.int32) * EXPERT_ROWS
    block_expert = jnp.minimum(jnp.sum((seg_end[None, :] <= first_row[:, None]).astype(jnp.int32), axis=1),
                               N_EXPERTS - 1)
    return dest, slot_tok, block_expert


def _attn_c_prompt_kernel(q_ref, k_ref, v_ref, o_ref):
    for p in range(C_HEADS // 2):
        sl = slice(p * LANES, (p + 1) * LANES)
        kb = k_ref[:, sl].astype(BF16)
        vb = v_ref[:, sl].astype(BF16)
        qp = q_ref[:, sl] * (C_HEAD_DIM ** -0.5)
        o_ref[:, sl] = _softmax_pair(qp, lambda qm: [_dot_nt(qm, kb)], lambda ps: _dot(ps[0], vb))


def _attn_c_prompt(q, k, v):
    blk = pl.BlockSpec((SEQ, C_WIDTH), lambda b: (b, 0))
    return pl.pallas_call(
        _attn_c_prompt_kernel,
        grid=(BATCH,),
        in_specs=[blk, blk, blk],
        out_specs=blk,
        out_shape=jax.ShapeDtypeStruct((T_PROMPT, C_WIDTH), F32),
        compiler_params=_cparams(("arbitrary",)),
        name="attn_c_prompt",
    )(q, k, v)


def _na_key_start(r0):
    rows = DEC_SEQ // GRID_W
    return jnp.minimum(jnp.clip(r0 - NA_ROWS // 2, 0, rows - NA_ROWS), rows - NA_KROWS)


def _attn_c_sample_kernel(q_ref, k_ref, v_ref, ck_ref, cv_ref, bias_ref, o_ref):
    rows = DEC_SEQ // GRID_W
    nblk = rows // NA_QROWS
    i = pl.program_id(2)
    r0 = i * NA_QROWS
    k0 = pl.multiple_of(_na_key_start(r0) * GRID_W, GRID_W)
    btype = jnp.where(i == 0, 0, jnp.where(i == nblk - 1, 2, 1))
    nk = NA_KROWS * GRID_W
    kw = k_ref[pl.ds(k0, nk), :].astype(BF16)
    vw = v_ref[pl.ds(k0, nk), :].astype(BF16)
    kc = ck_ref[0].astype(BF16)
    vc = cv_ref[0].astype(BF16)
    qp = q_ref[...] * (C_HEAD_DIM ** -0.5)
    lo = _lane(qp.shape) < 64
    outs = []
    for half in range(2):
        qm = jnp.where(lo if half == 0 else jnp.logical_not(lo), qp, 0.0).astype(BF16)
        s_win = _dot_nt(qm, kw) + bias_ref[0, btype, half]
        s_ctx = _dot_nt(qm, kc)
        m = jnp.maximum(s_win.max(axis=-1, keepdims=True), s_ctx.max(axis=-1, keepdims=True))
        e_win = jnp.exp(s_win - m)
        e_ctx = jnp.exp(s_ctx - m)
        l = e_win.sum(axis=-1, keepdims=True) + e_ctx.sum(axis=-1, keepdims=True)
        o = _dot(e_win.astype(BF16), vw) + _dot(e_ctx.astype(BF16), vc)
        outs.append(o / l)
    o_ref[...] = jnp.where(lo, outs[0], outs[1])


def _attn_c_sample(q, k, v, ck, cv, bias):
    rows = DEC_SEQ // GRID_W
    nblk = rows // NA_QROWS
    qrows = NA_QROWS * GRID_W
    npair = C_HEADS // 2
    return pl.pallas_call(
        _attn_c_sample_kernel,
        grid=(npair, DEC_BATCH, nblk),
        in_specs=[pl.BlockSpec((qrows, LANES), lambda p, b, i: (b * nblk + i, p)),
                  pl.BlockSpec((DEC_SEQ, LANES), lambda p, b, i: (b, p)),
                  pl.BlockSpec((DEC_SEQ, LANES), lambda p, b, i: (b, p)),
                  pl.BlockSpec((1, PAST_LEN, LANES), lambda p, b, i: (b, 0, p)),
                  pl.BlockSpec((1, PAST_LEN, LANES), lambda p, b, i: (b, 0, p)),
                  pl.BlockSpec((1, 3, 2, qrows, NA_KROWS * GRID_W), lambda p, b, i: (p, 0, 0, 0, 0))],
        out_specs=pl.BlockSpec((qrows, LANES), lambda p, b, i: (b * nblk + i, p)),
        out_shape=jax.ShapeDtypeStruct((T_SAMPLE, C_WIDTH), F32),
        compiler_params=_cparams(("arbitrary", "arbitrary", "arbitrary")),
        name="attn_c_sample",
    )(q, k, v, ck, cv, bias)


def _na_bias_tables(rpb):
    rows = DEC_SEQ // GRID_W
    nblk = rows // NA_QROWS
    w = GRID_W
    nd_r, nd_c = 2 * NA_ROWS - 1, 2 * NA_COLS - 1
    c = np.arange(w)
    cs = np.clip(c - NA_COLS // 2, 0, w - NA_COLS)
    col_ok = (c[None, :] >= cs[:, None]) & (c[None, :] < cs[:, None] + NA_COLS)
    left = w - NA_COLS
    u = jnp.pad(rpb, ((0, 0), (0, 0), (left, 2 * w - left - nd_c)))
    x = jnp.tile(u, (1, 1, w))[..., :w * (2 * w - 1)].reshape(C_HEADS, nd_r, w, 2 * w - 1)
    toe = jnp.where(col_ok[None, None], x[..., w - 1:], NEG)
    pad = NA_KROWS
    toe = jnp.pad(toe, ((0, 0), (pad, pad), (0, 0), (0, 0)), constant_values=NEG)
    toe = toe.reshape(C_HEADS // 2, 2, nd_r + 2 * pad, w, w)
    types = []
    for blk in (0, 1, nblk - 1):
        r0 = blk * NA_QROWS
        ks = min(int(np.clip(r0 - NA_ROWS // 2, 0, rows - NA_ROWS)), rows - NA_KROWS)
        per_row = []
        for i in range(NA_QROWS):
            r = r0 + i
            rs = int(np.clip(r - NA_ROWS // 2, 0, rows - NA_ROWS))
            start = ks - r + NA_ROWS - 1 + pad
            sl = toe[:, :, start:start + NA_KROWS]
            ok = np.array([rs <= ks + j < rs + NA_ROWS for j in range(NA_KROWS)])
            sl = jnp.where(ok[None, None, :, None, None], sl, NEG)
            per_row.append(sl.transpose(0, 1, 3, 2, 4))
        types.append(jnp.stack(per_row, axis=2))
    tab = jnp.stack(types, axis=1)
    return tab.reshape(C_HEADS // 2, 3, 2, NA_QROWS * w, NA_KROWS * w)


def _rope_tables():
    half = A_HEAD_DIM // 2
    t = jnp.arange(DEC_SEQ)
    row = (t // GRID_W).astype(F32)
    colp = (t % GRID_W).astype(F32)
    freqs = 1.0 / (ROPE_BASE ** (jnp.arange(0, half, 2, dtype=F32) / half))
    d = np.arange(LANES) % A_HEAD_DIM
    pos = jnp.where(jnp.asarray(d < half)[None, :], row[:, None], colp[:, None])
    ang = pos * freqs[d % (half // 2)][None, :]
    sign = jnp.asarray(np.where((d % half) < half // 2, -1.0, 1.0), F32)[None, :]
    return jnp.cos(ang), jnp.sin(ang) * sign


def _head_avg_matrix():
    idx = np.arange(LANES) // A_HEAD_DIM
    return jnp.asarray((idx[:, None] == idx[None, :]).astype(np.float32) / A_HEAD_DIM, BF16)


def _tri_matrices():
    i = np.arange(MLSTM_CHUNK)
    upper = (i[:, None] <= i[None, :]).astype(np.float32)
    lower = (i[:, None] >= i[None, :]).astype(np.float32)
    return jnp.asarray(np.stack([upper, lower]), BF16)


def _router_weights(wg, bg, we, be):
    w = jnp.zeros((D_MODEL, LANES), F32).at[:, :N_GROUPS].set(wg).at[:, N_GROUPS:N_GROUPS + N_EXPERTS].set(we)
    b = jnp.zeros((1, LANES), F32).at[0, :N_GROUPS].set(bg).at[0, N_GROUPS:N_GROUPS + N_EXPERTS].set(be)
    hi = w.astype(BF16)
    lo = (w - hi.astype(F32)).astype(BF16)
    return hi, lo, b


def _moe(xn, rtt, cnt, w1, w3, w2):
    dest, slot_tok, block_expert = _moe_plan(rtt, cnt)
    return dest, _experts(block_expert, slot_tok, xn, w1, w3, w2)


def kernel(x_prompt, x_sample, cache_attn_k, cache_attn_v, state_mlstm_C, state_mlstm_n, state_mlstm_m,
           cache_na_k, cache_na_v, c, c_ctx, norm_mix, norm_ffn, norm_final, ada_w, ada_b,
           ab_w_in, ab_w_out, ab_q_norm, ab_k_norm, ab_gate_bias, ab_out_norm,
           na_w_in, na_w_out, na_rpb, moe_wg, moe_bg, moe_we, moe_be, moe_w1, moe_w3, moe_w2):
    xp = x_prompt.reshape(T_PROMPT, D_MODEL)
    xs = x_sample.reshape(T_SAMPLE, D_MODEL)
    cond =jnp.zeros((N_COND, D_MODEL), F32).at[0].set(c_ctx).at[1:1 + DEC_BATCH].set(c)
    mod = _modulation(cond, ada_w, ada_b)
    gfin = norm_final.reshape(1, D_MODEL)

    w_in = ab_w_in[0]
    o_aq, o_ak, o_av, o_bq, o_bk, o_bv, o_bo, o_bg = np.cumsum((0,) + (A_WIDTH, A_KV_WIDTH, A_KV_WIDTH,
                                                                       B_WIDTH, B_WIDTH, B_WIDTH, B_WIDTH))
    wb = w_in[:, o_bq:o_bg].astype(BF16)
    wq = w_in[:, o_aq:o_ak].astype(BF16)
    wkv = w_in[:, o_ak:o_bq].astype(BF16)
    wg = w_in[:, o_bg:o_bg + 4 * B_HEADS].reshape(D_MODEL, 4, B_HEADS)
    wgt = jnp.zeros((B_HEADS, 8, D_MODEL), F32).at[:, :4, :].set(wg.transpose(2, 1, 0))
    wgt = wgt.reshape(8 * B_HEADS, D_MODEL).astype(BF16)
    g_mix = norm_mix[0].reshape(1, D_MODEL)
    ob_p, oq_p, okv_p, ogt_p = _proj_ab(xp, mod[0], g_mix, wb, wq, wkv, wgt, 0)
    ob_s, oq_s, okv_s, ogt_s = _proj_ab(xs, mod[0], g_mix, wb, wq, wkv, wgt, T_PROMPT)

    qn = jnp.tile(ab_q_norm[0], 2).reshape(1, LANES)
    kn = jnp.tile(ab_k_norm[0], 2).reshape(1, LANES)
    bd = _head_avg_matrix()
    cos, sin = _rope_tables()
    a_p, new_k, new_v = _attn_a_prompt(oq_p, okv_p, qn, kn, bd)
    ck = cache_attn_k[:, 0].reshape(DEC_BATCH, PAST_LEN, A_KV_WIDTH)
    cv = cache_attn_v[:, 0].reshape(DEC_BATCH, PAST_LEN, A_KV_WIDTH)
    a_s = _attn_a_sample(oq_s, okv_s, ck, cv, cos, sin, qn, kn, bd)

    gb = ab_gate_bias[0]
    brow = jnp.zeros((B_HEADS, 8, LANES), F32).at[:, :4, :].set(
        jnp.broadcast_to(gb.T[:, :, None], (B_HEADS, 4, LANES)))
    onorm = ab_out_norm[0].reshape(B_HEADS, 1, B_HEAD_DIM)
    tri = _tri_matrices()
    b_p, cT, nT, mT = _mlstm(ob_p, ogt_p, brow, onorm, tri, n=SEQ, nseq=BATCH, emit_state=True)
    c0 = state_mlstm_C[:, 0]
    n0 = state_mlstm_n[:, 0].transpose(0, 2, 1, 3)
    m0 = jnp.broadcast_to(state_mlstm_m[:, 0].transpose(0, 2, 1)[..., None], (DEC_BATCH, B_HEADS, 2, LANES))
    (b_s,) = _mlstm(ob_s, ogt_s, brow, onorm, tri, n=DEC_SEQ, nseq=DEC_BATCH, init=(c0, n0, m0))

    w_out = ab_w_out[0].astype(BF16)
    wrh, wrl, br = _router_weights(moe_wg[0], moe_bg[0], moe_we[0], moe_be[0])
    x1, xn, rt, rtt, cnt = _post([(xp, xs), (a_p, a_s), (b_p, b_s)], [w_out[:A_WIDTH], w_out[A_WIDTH:]], mod[0],
                                 norm_ffn[0].reshape(1, D_MODEL), wrh, wrl, br)
    dest, yb = _moe(xn, rtt, cnt, moe_w1[0], moe_w3[0], moe_w2[0])
    x = _combine(dest, yb, x1, rt, mod[0], gfin, False)

    g_mix = norm_mix[1].reshape(1, D_MODEL)
    w_in = na_w_in[0].astype(BF16)
    q_p, k_p, v_p = _proj_na(x, mod[1], g_mix, w_in, 0, T_PROMPT)
    q_s, k_s, v_s = _proj_na(x, mod[1], g_mix, w_in, T_PROMPT, T_SAMPLE)
    o_p = _attn_c_prompt(q_p, k_p, v_p)
    nck = cache_na_k[:, 0].reshape(DEC_BATCH, PAST_LEN, C_WIDTH)
    ncv = cache_na_v[:, 0].reshape(DEC_BATCH, PAST_LEN, C_WIDTH)
    o_s = _attn_c_sample(q_s, k_s, v_s, nck, ncv, _na_bias_tables(na_rpb[0]))
    wrh, wrl, br = _router_weights(moe_wg[1], moe_bg[1], moe_we[1], moe_be[1])
    x1, xn, rt, rtt, cnt = _post([x, (o_p, o_s)], [na_w_out[0].astype(BF16)], mod[1],
                                 norm_ffn[1].reshape(1, D_MODEL), wrh, wrl, br)
    dest, yb = _moe(xn, rtt, cnt, moe_w1[1], moe_w3[1], moe_w2[1])
    y_prompt = _combine(dest, yb, x1, rt, mod[1], gfin, True, 0, T_PROMPT).reshape(BATCH, SEQ, D_MODEL)
    y_sample = _combine(dest, yb, x1, rt, mod[1], gfin, True, T_PROMPT, T_SAMPLE).reshape(DEC_BATCH, DEC_SEQ, D_MODEL)
    new_attn_k = new_k.reshape(BATCH, 1, SEQ, A_KV_HEADS, A_HEAD_DIM)
    new_attn_v = new_v.reshape(BATCH, 1, SEQ, A_KV_HEADS, A_HEAD_DIM)
    new_mlstm_C = cT[:, None]
    new_mlstm_n = nT.transpose(0, 2, 1, 3)[:, None]
    new_mlstm_m = mT[..., 0].transpose(0, 2, 1)[:, None]
    new_na_k = k_p.reshape(BATCH, 1, SEQ, C_HEADS, C_HEAD_DIM)
    new_na_v = v_p.reshape(BATCH, 1, SEQ, C_HEADS, C_HEAD_DIM)
    return (y_prompt, y_sample, new_attn_k, new_attn_v, new_mlstm_C, new_mlstm_n, new_mlstm_m,
            new_na_k, new_na_v)
```

```python
import functools

import numpy as np
import jax
import jax.numpy as jnp
from jax import lax
from jax.experimental import pallas as pl
from jax.experimental.pallas import tpu as pltpu

F32 = jnp.float32
BF16 = jnp.bfloat16

D_MODEL = 1024
BATCH = 32
SEQ = 256
DEC_BATCH = 4
DEC_SEQ = 2048
PAST_LEN = 256
GRID_W = 64
A_HEADS = 8
A_KV_HEADS = 2
A_HEAD_DIM = 64
A_WIDTH = A_HEADS * A_HEAD_DIM
A_KV_WIDTH = A_KV_HEADS * A_HEAD_DIM
B_HEADS = 4
B_HEAD_DIM = 128
B_WIDTH = B_HEADS * B_HEAD_DIM
MLSTM_CHUNK = 128
C_HEADS = 16
C_HEAD_DIM = 64
C_WIDTH = C_HEADS * C_HEAD_DIM
NA_ROWS = 8
NA_COLS = 16
N_GROUPS = 4
EXPERTS_PER_GROUP = 8
N_EXPERTS = N_GROUPS * EXPERTS_PER_GROUP
D_EXPERT = 512
MOE_BLOCK = 128
ROPE_BASE = 10000.0
NORM_EPS = 1e-6

T_PROMPT = BATCH * SEQ
T_SAMPLE = DEC_BATCH * DEC_SEQ
T_ALL = T_PROMPT + T_SAMPLE
N_COND = 8
LANES = 128
SLABS = D_MODEL // LANES
TOK_BLOCK = 256
EXPERT_ROWS = 256
NA_QROWS = 4
NA_KROWS = 12
NEG = -1e30
VMEM_LIMIT = 56 * 1024 * 1024


def _cparams(sem):
    return pltpu.CompilerParams(dimension_semantics=sem, vmem_limit_bytes=VMEM_LIMIT)


def _split2(x):
    hi = x.astype(BF16)
    lo = (x - hi.astype(F32)).astype(BF16)
    return hi, lo


def _split3(x):
    hi = x.astype(BF16)
    r = x - hi.astype(F32)
    mid = r.astype(BF16)
    lo = (r - mid.astype(F32)).astype(BF16)
    return hi, mid, lo


def _dot(a, b):
    return jnp.dot(a, b, preferred_element_type=F32)


def _dot_nt(a, b):
    return lax.dot_general(a, b, (((1,), (1,)), ((), ())), preferred_element_type=F32)


def _dot_exact_rhs(x, b):
    hi, mid, lo = _split3(x)
    return _dot(hi, b) + _dot(mid, b) + _dot(lo, b)


def _rms(x, g):
    ms = jnp.mean(x * x, axis=-1, keepdims=True)
    return (x * lax.rsqrt(ms + NORM_EPS)) * g


def _mod_row(tok0):
    return jnp.where(tok0 < T_PROMPT, 0, 1 + (tok0 - T_PROMPT) // DEC_SEQ)


def _mod_part(mod_ref, r, idx):
    return mod_ref[pl.ds(r, 1), idx * D_MODEL:(idx + 1) * D_MODEL]


def _head_rms(x, w, bd):
    hi, lo = _split2(x * x)
    ms = _dot(hi, bd) + _dot(lo, bd)
    return (x * lax.rsqrt(ms + NORM_EPS)) * w


def _to_token_tiles(ref, x):
    m = x.shape[0]
    for s in range(SLABS):
        ref[pl.ds(s, m, stride=SLABS), :] = x[:, s * LANES:(s + 1) * LANES]


def _from_token_tiles(ref, tile0, m):
    return jnp.concatenate([ref[pl.ds(tile0 * SLABS + s, m, stride=SLABS), :] for s in range(SLABS)], axis=1)


def _lane(shape):
    return lax.broadcasted_iota(jnp.int32, shape, len(shape) - 1)


def _dup_half(x, g):
    xr = pltpu.roll(x, 64, 1)
    lo = _lane(x.shape) < 64
    return jnp.where(lo, x, xr) if g == 0 else jnp.where(lo, xr, x)


def _rope(x, cos, sin_signed):
    lane = _lane(x.shape)
    partner = jnp.where((lane % 32) < 16, pltpu.roll(x, LANES - 16, 1), pltpu.roll(x, 16, 1))
    return x * cos + partner * sin_signed


def _softmax_pair(qp, score_fn, value_fn):
    lo = _lane(qp.shape) < 64
    outs = []
    for half in range(2):
        qm = jnp.where(lo if half == 0 else jnp.logical_not(lo), qp, 0.0).astype(BF16)
        ss = score_fn(qm)
        m = ss[0].max(axis=-1, keepdims=True)
        for s in ss[1:]:
            m = jnp.maximum(m, s.max(axis=-1, keepdims=True))
        es = [jnp.exp(s - m) for s in ss]
        l = es[0].sum(axis=-1, keepdims=True)
        for e in es[1:]:
            l = l + e.sum(axis=-1, keepdims=True)
        o = value_fn([e.astype(BF16) for e in es])
        outs.append(o / l)
    return jnp.where(lo, outs[0], outs[1])


def _mod_kernel(cond_ref, w_ref, b_ref, o_ref):
    c = cond_ref[...]
    s = c * jax.nn.sigmoid(c)
    s_hi, s_lo = _split2(s)
    w_hi, w_lo = _split2(w_ref[0])
    o_ref[0] = _dot(s_hi, w_hi) + _dot(s_lo, w_hi) + _dot(s_hi, w_lo) + b_ref[0]


def _modulation(cond, ada_w, ada_b):
    depth, d, n = ada_w.shape
    tn = 1536
    return pl.pallas_call(
        _mod_kernel,
        grid=(depth, n // tn),
        in_specs=[pl.BlockSpec((N_COND, d), lambda l, j: (0, 0)),
                  pl.BlockSpec((1, d, tn), lambda l, j: (l, 0, j)),
                  pl.BlockSpec((1, 1, tn), lambda l, j: (l, 0, j))],
        out_specs=pl.BlockSpec((1, N_COND, tn), lambda l, j: (l, 0, j)),
        out_shape=jax.ShapeDtypeStruct((depth, N_COND, n), F32),
        compiler_params=_cparams(("arbitrary", "arbitrary")),
        name="adaln_modulation",
    )(cond, ada_w, ada_b.reshape(depth, 1, n))


def _norm_mod(x_ref, mod_ref, g_ref, shift_idx, scale_idx, row0):
    r = _mod_row(row0 + pl.program_id(0) * x_ref.shape[0])
    h = _rms(x_ref[...], g_ref[...])
    return h * (1.0 + _mod_part(mod_ref, r, scale_idx)) + _mod_part(mod_ref, r, shift_idx)


def _proj_ab_kernel(x_ref, mod_ref, g_ref, wb_ref, wq_ref, wkv_ref, wgt_ref, ob_ref, oq_ref, okv_ref, ogt_ref,
                    *, row0):
    tm = x_ref.shape[0]
    hb = _norm_mod(x_ref, mod_ref, g_ref, 0, 1, row0).astype(BF16)
    ob_ref[...] = _dot(hb, wb_ref[...])
    oq_ref[...] = _dot(hb, wq_ref[...])
    okv_ref[...] = _dot(hb, wkv_ref[...])
    gt = _dot_nt(wgt_ref[...], hb)
    for j in range(tm // LANES):
        ogt_ref[j] = gt[:, j * LANES:(j + 1) * LANES]


def _proj_ab(x, mod, g, wb, wq, wkv, wgt, row0):
    t = x.shape[0]
    tm = TOK_BLOCK
    full = lambda a: pl.BlockSpec(a.shape, lambda i: (0,) * a.ndim)
    return pl.pallas_call(
        functools.partial(_proj_ab_kernel, row0=row0),
        grid=(t // tm,),
        in_specs=[pl.BlockSpec((tm, D_MODEL), lambda i: (i, 0)), full(mod), full(g),
                  full(wb), full(wq), full(wkv), full(wgt)],
        out_specs=[pl.BlockSpec((tm, 4 * B_WIDTH), lambda i: (i, 0)),
                   pl.BlockSpec((tm, A_WIDTH), lambda i: (i, 0)),
                   pl.BlockSpec((tm, 2 * A_KV_WIDTH), lambda i: (i, 0)),
                   pl.BlockSpec((tm // LANES, 8 * B_HEADS, LANES), lambda i: (i, 0, 0))],
        out_shape=[jax.ShapeDtypeStruct((t, 4 * B_WIDTH), F32),
                   jax.ShapeDtypeStruct((t, A_WIDTH), F32),
                   jax.ShapeDtypeStruct((t, 2 * A_KV_WIDTH), F32),
                   jax.ShapeDtypeStruct((t // LANES, 8 * B_HEADS, LANES), F32)],
        compiler_params=_cparams(("arbitrary",)),
        name="proj_ab",
    )(x, mod, g, wb, wq, wkv, wgt)


def _proj_na_kernel(x_ref, mod_ref, g_ref, w_ref, q_ref, k_ref, v_ref, *, row0):
    hb = _norm_mod(x_ref, mod_ref, g_ref, 0, 1, row0).astype(BF16)
    for j, o_ref in enumerate((q_ref, k_ref, v_ref)):
        o_ref[...] = _dot(hb, w_ref[:, j * C_WIDTH:(j + 1) * C_WIDTH])


def _proj_na(x, mod, g, w, row0, t):
    tm = TOK_BLOCK
    blk0 = row0 // tm
    full = lambda a: pl.BlockSpec(a.shape, lambda i: (0,) * a.ndim)
    return pl.pallas_call(
        functools.partial(_proj_na_kernel, row0=row0),
        grid=(t // tm,),
        in_specs=[pl.BlockSpec((tm, D_MODEL), lambda i: (blk0 + i, 0)), full(mod), full(g), full(w)],
        out_specs=[pl.BlockSpec((tm, C_WIDTH), lambda i: (i, 0))] * 3,
        out_shape=[jax.ShapeDtypeStruct((t, C_WIDTH), F32)] * 3,
        compiler_params=_cparams(("arbitrary",)),
        name="proj_na",
    )(x, mod, g, w)


def _gqa_block(q_ref, qn, bd, kd_ref, vd_ref, o_ref, rope=None):
    for p in range(A_HEADS // 2):
        g = p // (A_HEADS // 2 // A_KV_HEADS)
        qp = _head_rms(q_ref[:, p * LANES:(p + 1) * LANES], qn, bd)
        if rope is not None:
            qp = _rope(qp, rope[0], rope[1])
        qp = qp * (A_HEAD_DIM ** -0.5)
        o_ref[:, p * LANES:(p + 1) * LANES] = _softmax_pair(
            qp, lambda qm: [_dot_nt(qm, kd_ref[g])], lambda ps: _dot(ps[0], vd_ref[g]))


def _attn_a_prompt_kernel(q_ref, kv_ref, qn_ref, kn_ref, bd_ref, o_ref, knew_ref, vnew_ref, kd_ref, vd_ref):
    bd = bd_ref[...]
    k = _head_rms(kv_ref[:, :LANES], kn_ref[...], bd)
    v = kv_ref[:, LANES:]
    knew_ref[...] = k
    vnew_ref[...] = v
    for g in range(A_KV_HEADS):
        kd_ref[g] = _dup_half(k, g).astype(BF16)
        vd_ref[g] = _dup_half(v, g).astype(BF16)
    _gqa_block(q_ref, qn_ref[...], bd, kd_ref, vd_ref, o_ref)


def _attn_a_prompt(q, kv, qn, kn, bd):
    nb = BATCH
    full = lambda a: pl.BlockSpec(a.shape, lambda b: (0,) * a.ndim)
    return pl.pallas_call(
        _attn_a_prompt_kernel,
        grid=(nb,),
        in_specs=[pl.BlockSpec((SEQ, A_WIDTH), lambda b: (b, 0)),
                  pl.BlockSpec((SEQ, 2 * A_KV_WIDTH), lambda b: (b, 0)),
                  full(qn), full(kn), full(bd)],
        out_specs=[pl.BlockSpec((SEQ, A_WIDTH), lambda b: (b, 0)),
                   pl.BlockSpec((SEQ, A_KV_WIDTH), lambda b: (b, 0)),
                   pl.BlockSpec((SEQ, A_KV_WIDTH), lambda b: (b, 0))],
        out_shape=[jax.ShapeDtypeStruct((T_PROMPT, A_WIDTH), F32),
                   jax.ShapeDtypeStruct((T_PROMPT, A_KV_WIDTH), F32),
                   jax.ShapeDtypeStruct((T_PROMPT, A_KV_WIDTH), F32)],
        scratch_shapes=[pltpu.VMEM((A_KV_HEADS, SEQ, LANES), BF16),
                        pltpu.VMEM((A_KV_HEADS, SEQ, LANES), BF16)],
        compiler_params=_cparams(("arbitrary",)),
        name="attn_a_prompt",
    )(q, kv, qn, kn, bd)


_A_QBLOCK = 256


def _attn_a_sample_kernel(q_ref, kv_ref, ck_ref, cv_ref, cos_ref, sin_ref, cosq_ref, sinq_ref,
                          qn_ref, kn_ref, bd_ref, o_ref, kd_ref, vd_ref):
    bd = bd_ref[...]

    @pl.when(pl.program_id(1) == 0)
    def _():
        for g in range(A_KV_HEADS):
            kd_ref[g, :PAST_LEN] = _dup_half(ck_ref[0], g).astype(BF16)
            vd_ref[g, :PAST_LEN] = _dup_half(cv_ref[0], g).astype(BF16)
        rows = 256
        for c in range(DEC_SEQ // rows):
            sl = slice(c * rows, (c + 1) * rows)
            k = _head_rms(kv_ref[sl, :LANES], kn_ref[...], bd)
            k = _rope(k, cos_ref[sl, :], sin_ref[sl, :])
            v = kv_ref[sl, LANES:]
            dst = slice(PAST_LEN + c * rows, PAST_LEN + (c + 1) * rows)
            for g in range(A_KV_HEADS):
                kd_ref[g, dst] = _dup_half(k, g).astype(BF16)
                vd_ref[g, dst] = _dup_half(v, g).astype(BF16)

    _gqa_block(q_ref, qn_ref[...], bd, kd_ref, vd_ref, o_ref, rope=(cosq_ref[...], sinq_ref[...]))


def _attn_a_sample(q, kv, ck, cv, cos, sin, qn, kn, bd):
    nq = DEC_SEQ // _A_QBLOCK
    full = lambda a: pl.BlockSpec(a.shape, lambda b, i: (0,) * a.ndim)
    tk = PAST_LEN + DEC_SEQ
    return pl.pallas_call(
        _attn_a_sample_kernel,
        grid=(DEC_BATCH, nq),
        in_specs=[pl.BlockSpec((_A_QBLOCK, A_WIDTH), lambda b, i: (b * nq + i, 0)),
                  pl.BlockSpec((DEC_SEQ, 2 * A_KV_WIDTH), lambda b, i: (b, 0)),
                  pl.BlockSpec((1, PAST_LEN, A_KV_WIDTH), lambda b, i: (b, 0, 0)),
                  pl.BlockSpec((1, PAST_LEN, A_KV_WIDTH), lambda b, i: (b, 0, 0)),
                  full(cos), full(sin),
                  pl.BlockSpec((_A_QBLOCK, LANES), lambda b, i: (i, 0)),
                  pl.BlockSpec((_A_QBLOCK, LANES), lambda b, i: (i, 0)),
                  full(qn), full(kn), full(bd)],
        out_specs=pl.BlockSpec((_A_QBLOCK, A_WIDTH), lambda b, i: (b * nq + i, 0)),
        out_shape=jax.ShapeDtypeStruct((T_SAMPLE, A_WIDTH), F32),
        scratch_shapes=[pltpu.VMEM((A_KV_HEADS, tk, LANES), BF16),
                        pltpu.VMEM((A_KV_HEADS, tk, LANES), BF16)],
        compiler_params=_cparams(("arbitrary", "arbitrary")),
        name="attn_a_sample",
    )(q, kv, ck, cv, cos, sin, cos, sin, qn, kn, bd)


def _log_sigmoid(x):
    return -(jnp.maximum(-x, 0.0) + jnp.log1p(jnp.exp(-jnp.abs(x))))


def _mlstm_kernel(*refs, n, has_init, emit_state):
    it = iter(refs)
    q_ref, k_ref, v_ref, og_ref, gt_ref, brow_ref, onorm_ref, tri_ref = [next(it) for _ in range(8)]
    if has_init:
        c0_ref, n0_ref, m0_ref = [next(it) for _ in range(3)]
    out_ref = next(it)
    if emit_state:
        cT_ref, nT_ref, mT_ref = [next(it) for _ in range(3)]
    h_ref, row_ref, col_ref, tbuf_ref, c_ref, n_ref = [next(it) for _ in range(6)]

    L = MLSTM_CHUNK
    nc = n // L

    gt = gt_ref[...] + brow_ref[0][None]
    lf = _log_sigmoid(gt).reshape(nc * 8, L)
    cum_f = _dot_exact_rhs(lf, tri_ref[0]).reshape(nc, 8, L)
    cum_b = _dot_exact_rhs(lf, tri_ref[1]).reshape(nc, 8, L)
    cf = cum_f[:, 1:2, :]
    cb = cum_b[:, 3:4, :]
    row_ref[:, 0:1, :] = cf
    row_ref[:, 1:2, :] = gt[:, 0:1, :] - cf
    row_ref[:, 2:3, :] = cb
    row_ref[:, 3:4, :] = gt[:, 2:3, :] - cb
    lf3 = lf.reshape(nc, 8, L)
    row_ref[:, 4:5, :] = jnp.broadcast_to(lf3[:, 1:2, :].sum(axis=-1, keepdims=True), (nc, 1, L))
    row_ref[:, 5:6, :] = jnp.broadcast_to(lf3[:, 3:4, :].sum(axis=-1, keepdims=True), (nc, 1, L))
    row_ref[:, 6:8, :] = jnp.zeros((nc, 2, L), F32)

    tbuf_ref[...] = jnp.zeros((L, L), F32)

    def to_cols(c, carry):
        tbuf_ref[0:8, :] = row_ref[c]
        col_ref[pl.ds(pl.multiple_of(c * L, L), L), :] = tbuf_ref[...].T
        return carry

    lax.fori_loop(0, nc, to_cols, 0)

    if has_init:
        c_ref[...] = c0_ref[0, :, 0]
        n_ref[0:2, :] = n0_ref[0, 0]
        m_init = (m0_ref[0, 0, 0:1, 0:1], m0_ref[0, 0, 1:2, 0:1])
    else:
        c_ref[...] = jnp.zeros((2, L, L), F32)
        n_ref[...] = jnp.zeros((8, L), F32)
        m_init = (jnp.zeros((1, 1), F32), jnp.zeros((1, 1), F32))

    t_idx = lax.broadcasted_iota(jnp.int32, (L, L), 0)
    s_idx = lax.broadcasted_iota(jnp.int32, (L, L), 1)
    masks = (s_idx <= t_idx, s_idx >= t_idx)

    def step(c, d, m):
        r0 = pl.multiple_of(c * L, L)
        q = q_ref[pl.ds(r0, L), :]
        ks = k_ref[pl.ds(r0, L), :] * (B_HEAD_DIM ** -0.5)
        vb = v_ref[pl.ds(r0, L), :].astype(BF16)
        qb = q.astype(BF16)
        rows = row_ref[c]
        cols = col_ref[pl.ds(r0, L), :]
        a_row = rows[2 * d + 1:2 * d + 2, :]
        tot = rows[4 + d:5 + d, 0:1]
        cum_col = cols[:, 2 * d:2 * d + 1]
        a_col = cols[:, 2 * d + 1:2 * d + 2]
        dlog = jnp.where(masks[d], cum_col + a_row, -jnp.inf)
        inter = cum_col + m
        m_t = jnp.maximum(inter, dlog.max(axis=-1, keepdims=True))
        w_in = jnp.exp(dlog - m_t)
        w_st = jnp.exp(inter - m_t)
        a = _dot_nt(qb, ks.astype(BF16)) * w_in
        cmat = c_ref[d]
        nvec = n_ref[d:d + 1, :]
        num = w_st * _dot(qb, cmat.astype(BF16)) + _dot(a.astype(BF16), vb)
        den = w_st * (q * nvec).sum(axis=-1, keepdims=True) + a.sum(axis=-1, keepdims=True)
        hx = num / jnp.maximum(jnp.abs(den), jnp.exp(-m_t))
        if d == 0:
            h_ref[0, pl.ds(r0, L), :] = hx
        else:
            h_ref[1, pl.ds(r0, L), :] = hx
        m_new = jnp.maximum(tot + m, a_row.max(axis=-1, keepdims=True) + tot)
        ws = jnp.exp(a_col + tot - m_new)
        wc = jnp.exp(tot + m - m_new)
        kw = ks * ws
        c_ref[d] = wc * cmat + _dot(kw.T.astype(BF16), vb)
        n_ref[d:d + 1, :] = wc * nvec + kw.sum(axis=0, keepdims=True)
        return m_new

    def body(i, carry):
        return step(i, 0, carry[0]), step(nc - 1 - i, 1, carry[1])

    m_f, m_b = lax.fori_loop(0, nc, body, m_init)

    hm = h_ref[0] + h_ref[1]
    y = _rms(hm, onorm_ref[0])
    out_ref[...] = y * jax.nn.sigmoid(og_ref[...])

    if emit_state:
        cT_ref[0, :, 0] = c_ref[...]
        nT_ref[0, 0] = n_ref[0:2, :]
        mT_ref[0, 0, 0:1, :] = jnp.broadcast_to(m_f, (1, L))
        mT_ref[0, 0, 1:2, :] = jnp.broadcast_to(m_b, (1, L))


def _mlstm(ob, ogt, brow, onorm, tri, *, n, nseq, init=None, emit_state=False):
    L = MLSTM_CHUNK
    nc = n // L
    H = B_HEADS
    col = lambda part: (lambda b, h: (b, part * H + h))
    in_specs = [pl.BlockSpec((n, L), col(0)), pl.BlockSpec((n, L), col(1)), pl.BlockSpec((n, L), col(2)),
                pl.BlockSpec((n, L), col(3)),
                pl.BlockSpec((nc, 8, L), lambda b, h: (b, h, 0)),
                pl.BlockSpec((1, 8, L), lambda b, h: (h, 0, 0)),
                pl.BlockSpec((1, 1, L), lambda b, h: (h, 0, 0)),
                pl.BlockSpec(tri.shape, lambda b, h: (0, 0, 0))]
    args = [ob, ob, ob, ob, ogt, brow, onorm, tri]
    if init is not None:
        c0, n0, m0 = init
        in_specs += [pl.BlockSpec((1, 2, 1, L, L), lambda b, h: (b, 0, h, 0, 0)),
                     pl.BlockSpec((1, 1, 2, L), lambda b, h: (b, h, 0, 0)),
                     pl.BlockSpec((1, 1, 2, L), lambda b, h: (b, h, 0, 0))]
        args += [c0, n0, m0]
    out_specs = [pl.BlockSpec((n, L), lambda b, h: (b, h))]
    out_shape = [jax.ShapeDtypeStruct((nseq * n, B_WIDTH), F32)]
    if emit_state:
        out_specs += [pl.BlockSpec((1, 2, 1, L, L), lambda b, h: (b, 0, h, 0, 0)),
                      pl.BlockSpec((1, 1, 2, L), lambda b, h: (b, h, 0, 0)),
                      pl.BlockSpec((1, 1, 2, L), lambda b, h: (b, h, 0, 0))]
        out_shape += [jax.ShapeDtypeStruct((nseq, 2, H, L, L), F32),
                      jax.ShapeDtypeStruct((nseq, H, 2, L), F32),
                      jax.ShapeDtypeStruct((nseq, H, 2, L), F32)]
    return pl.pallas_call(
        functools.partial(_mlstm_kernel, n=n, has_init=init is not None, emit_state=emit_state),
        grid=(nseq, H),
        in_specs=in_specs,
        out_specs=out_specs,
        out_shape=out_shape,
        scratch_shapes=[pltpu.VMEM((2, n, L), F32),
                        pltpu.VMEM((nc, 8, L), F32),
                        pltpu.VMEM((n, L), F32),
                        pltpu.VMEM((L, L), F32),
                        pltpu.VMEM((2, L, L), F32),
                        pltpu.VMEM((8, L), F32)],
        compiler_params=_cparams(("arbitrary", "arbitrary")),
        name="mlstm_init" if init is not None else "mlstm",
    )(*args)


def _router(logits):
    lane = _lane(logits.shape).astype(F32)
    big = 1e9
    gl = jnp.where(lane < N_GROUPS, logits, -jnp.inf)
    gmax = gl.max(axis=-1, keepdims=True)
    g_sel = jnp.where(gl == gmax, lane, big).min(axis=-1, keepdims=True)
    g_prob = 1.0 / jnp.exp(gl - gmax).sum(axis=-1, keepdims=True)
    lo = N_GROUPS + EXPERTS_PER_GROUP * g_sel
    el = jnp.where(lane >= lo, jnp.where(lane < lo + EXPERTS_PER_GROUP, logits, -jnp.inf), -jnp.inf)
    v1 = el.max(axis=-1, keepdims=True)
    i1 = jnp.where(el == v1, lane, big).min(axis=-1, keepdims=True)
    el2 = jnp.where(lane == i1, -jnp.inf, el)
    v2 = el2.max(axis=-1, keepdims=True)
    i2 = jnp.where(el2 == v2, lane, big).min(axis=-1, keepdims=True)
    e2 = jnp.exp(v2 - v1)
    w1 = g_prob / (1.0 + e2)
    w2 = g_prob * e2 / (1.0 + e2)
    return i1, i2, w1, w2


def _read_tokens(refs, is_prompt):
    if len(refs) == 1:
        return refs[0][...]
    return jnp.where(is_prompt, refs[0][...], refs[1][...])


def _post_kernel(*refs, groups):
    it = iter(refs)
    tok_refs = [[next(it) for _ in range(n)] for n in groups]
    w_refs = [next(it) for _ in range(len(groups) - 1)]
    mod_ref, g_ref, wrh_ref, wrl_ref, br_ref, ls_ref, sel_ref = [next(it) for _ in range(7)]
    xnew_ref, xn_ref, rt_ref, rtt_ref, cnt_ref = [next(it) for _ in range(5)]
    run_ref = next(it)
    i = pl.program_id(0)
    tm = xnew_ref.shape[0]
    is_prompt = i * tm < T_PROMPT
    r = _mod_row(i * tm)
    acc = None
    for a_refs, w_ref in zip(tok_refs[1:], w_refs):
        d = _dot(_read_tokens(a_refs, is_prompt).astype(BF16), w_ref[...])
        acc = d if acc is None else acc + d
    xnew = _read_tokens(tok_refs[0], is_prompt) + _mod_part(mod_ref, r, 2) * acc
    xnew_ref[...] = xnew
    xn = _rms(xnew, g_ref[...]) * (1.0 + _mod_part(mod_ref, r, 4)) + _mod_part(mod_ref, r, 3)
    _to_token_tiles(xn_ref, xn)
    x_hi, x_lo = _split2(xn)
    logits = _dot(x_hi, wrh_ref[...]) + _dot(x_lo, wrh_ref[...]) + _dot(x_hi, wrl_ref[...]) + br_ref[...]
    i1, i2, w1, w2 = _router(logits)

    @pl.when(i == 0)
    def _():
        run_ref[...] = jnp.zeros(run_ref.shape, F32)

    lane = _lane(logits.shape).astype(F32)
    member = jnp.where(lane == i1, 1.0, jnp.where(lane == i2, 1.0, 0.0))
    before = _dot(ls_ref[...], member.astype(BF16)) + run_ref[...]
    rank1 = jnp.where(lane == i1, before, 0.0).sum(axis=-1, keepdims=True)
    rank2 = jnp.where(lane == i2, before, 0.0).sum(axis=-1, keepdims=True)
    run_ref[...] = run_ref[...] + member.sum(axis=0, keepdims=True)
    cnt_ref[...] = run_ref[...]
    cols = (i1 - N_GROUPS, i2 - N_GROUPS, w1, w2, rank1, rank2)
    rt = jnp.zeros(logits.shape, F32)
    for k, c in enumerate(cols):
        rt = jnp.where(lane == k, c, rt)
    rt_ref[...] = rt
    hi, mid, lo = _split3(rt)
    sel = sel_ref[...]
    rtt_ref[...] = _dot_nt(sel, hi) + _dot_nt(sel, mid) + _dot_nt(sel, lo)


def _post(tok_ops, w_list, mod, g, wrh, wrl, br):
    t = T_ALL
    tm = TOK_BLOCK
    npb = T_PROMPT // tm
    full = lambda a: pl.BlockSpec(a.shape, lambda i: (0,) * a.ndim)
    specs, args, groups = [], [], []
    for op in tok_ops:
        if isinstance(op, tuple):
            w = op[0].shape[1]
            specs += [pl.BlockSpec((tm, w), lambda i: (jnp.minimum(i, npb - 1), 0)),
                      pl.BlockSpec((tm, w), lambda i: (jnp.maximum(i - npb, 0), 0))]
            args += list(op)
            groups.append(2)
        else:
            specs.append(pl.BlockSpec((tm, op.shape[1]), lambda i: (i, 0)))
            args.append(op)
            groups.append(1)
    idx = np.arange(tm)
    ls = jnp.asarray(idx[:, None] > idx[None, :], BF16)
    sel = jnp.asarray(np.arange(8)[:, None] == np.arange(LANES)[None, :], BF16)
    consts = [mod, g, wrh, wrl, br, ls, sel]
    return pl.pallas_call(
        functools.partial(_post_kernel, groups=tuple(groups)),
        grid=(t // tm,),
        in_specs=specs + [full(w) for w in w_list] + [full(a) for a in consts],
        out_specs=[pl.BlockSpec((tm, D_MODEL), lambda i: (i, 0)),
                   pl.BlockSpec((tm * SLABS, LANES), lambda i: (i, 0)),
                   pl.BlockSpec((tm, LANES), lambda i: (i, 0)),
                   pl.BlockSpec((8, tm), lambda i: (0, i)),
                   pl.BlockSpec((1, LANES), lambda i: (0, 0))],
        out_shape=[jax.ShapeDtypeStruct((t, D_MODEL), F32),
                   jax.ShapeDtypeStruct((t * SLABS, LANES), F32),
                   jax.ShapeDtypeStruct((t, LANES), F32),
                   jax.ShapeDtypeStruct((8, t), F32),
                   jax.ShapeDtypeStruct((1, LANES), F32)],
        scratch_shapes=[pltpu.VMEM((1, LANES), F32)],
        compiler_params=_cparams(("arbitrary",)),
        name="post_mixer_router",
    )(*args, *w_list, *consts)


def _expert_kernel(be_ref, st_ref, xn_hbm, w1_ref, w3_ref, w2_ref, o_ref, xa, xb, sem, w1b, w3b, w2b):
    i = pl.program_id(0)
    nblk = pl.num_programs(0)
    rows = EXPERT_ROWS

    def gather(blk, buf, s):
        for r in range(rows):
            tok = st_ref[blk * rows + r]
            pltpu.make_async_copy(xn_hbm.at[pl.ds(pl.multiple_of(tok * SLABS, SLABS), SLABS), :],
                                  buf.at[pl.ds(r * SLABS, SLABS), :], sem.at[s]).start()

    def wait(buf, s):
        pltpu.make_async_copy(xn_hbm.at[pl.ds(0, rows * SLABS), :], buf, sem.at[s]).wait()

    @pl.when(i == 0)
    def _():
        gather(0, xa, 0)

    changed = jnp.logical_or(i == 0, be_ref[i] != be_ref[jnp.maximum(i - 1, 0)])

    @pl.when(changed)
    def _():
        w1b[...] = w1_ref[0].astype(BF16)
        w3b[...] = w3_ref[0].astype(BF16)
        w2b[...] = w2_ref[0].astype(BF16)

    nxt = jnp.minimum(i + 1, nblk - 1)

    def step(cur, s_cur, oth, s_oth):
        wait(cur, s_cur)
        gather(nxt, oth, s_oth)
        x = _from_token_tiles(cur, 0, rows).astype(BF16)
        h1 = _dot(x, w1b[...])
        h3 = _dot(x, w3b[...])
        hid = (h1 * jax.nn.sigmoid(h1)) * h3
        _to_token_tiles(o_ref, _dot(hid.astype(BF16), w2b[...]))

    @pl.when(i % 2 == 0)
    def _():
        step(xa, 0, xb, 1)

    @pl.when(i % 2 == 1)
    def _():
        step(xb, 1, xa, 0)

    @pl.when(i == nblk - 1)
    def _():
        @pl.when(i % 2 == 0)
        def _():
            wait(xb, 1)

        @pl.when(i % 2 == 1)
        def _():
            wait(xa, 0)


def _experts(block_expert, slot_tok, xn, w1, w3, w2):
    nblk = block_expert.shape[0]
    rows = EXPERT_ROWS
    grid_spec = pltpu.PrefetchScalarGridSpec(
        num_scalar_prefetch=2,
        grid=(nblk,),
        in_specs=[pl.BlockSpec(memory_space=pl.ANY),
                  pl.BlockSpec((1, D_MODEL, D_EXPERT), lambda i, be, st: (be[i], 0, 0)),
                  pl.BlockSpec((1, D_MODEL, D_EXPERT), lambda i, be, st: (be[i], 0, 0)),
                  pl.BlockSpec((1, D_EXPERT, D_MODEL), lambda i, be, st: (be[i], 0, 0))],
        out_specs=pl.BlockSpec((rows * SLABS, LANES), lambda i, be, st: (i, 0)),
        scratch_shapes=[pltpu.VMEM((rows * SLABS, LANES), F32),
                        pltpu.VMEM((rows * SLABS, LANES), F32),
                        pltpu.SemaphoreType.DMA((2,)),
                        pltpu.VMEM((D_MODEL, D_EXPERT), BF16),
                        pltpu.VMEM((D_MODEL, D_EXPERT), BF16),
                        pltpu.VMEM((D_EXPERT, D_MODEL), BF16)])
    return pl.pallas_call(
        _expert_kernel,
        grid_spec=grid_spec,
        out_shape=jax.ShapeDtypeStruct((nblk * rows * SLABS, LANES), F32),
        compiler_params=_cparams(("arbitrary",)),
        name="moe_experts",
    )(block_expert, slot_tok, xn, w1, w3, w2)


_COMBINE_BLOCK = 128


def _combine_kernel(dest_ref, yb_hbm, x_ref, rt_ref, mod_ref, gfin_ref, o_ref, ybuf, sem, *, final_norm, row0):
    i = pl.program_id(0)
    nblk = pl.num_programs(0)
    slot = i % 2
    tm = _COMBINE_BLOCK

    def issue(blk, s):
        def body(j, carry):
            for c in range(2):
                d = dest_ref[c * T_ALL + row0 + blk * tm + j]
                pltpu.make_async_copy(yb_hbm.at[pl.ds(pl.multiple_of(d * SLABS, SLABS), SLABS), :],
                                      ybuf.at[s, pl.ds(pl.multiple_of((c * tm + j) * SLABS, SLABS), SLABS), :],
                                      sem.at[s]).start()
            return carry
        lax.fori_loop(0, tm, body, 0, unroll=4)

    @pl.when(i == 0)
    def _():
        issue(0, 0)

    @pl.when(i + 1 < nblk)
    def _():
        issue(i + 1, 1 - slot)

    pltpu.make_async_copy(yb_hbm.at[pl.ds(0, 2 * tm * SLABS), :], ybuf.at[slot], sem.at[slot]).wait()
    r = _mod_row(row0 + i * tm)
    rt = rt_ref[...]
    yv = ybuf.at[slot]
    y = rt[:, 2:3] * _from_token_tiles(yv, 0, tm) + rt[:, 3:4] * _from_token_tiles(yv, tm, tm)
    out = x_ref[...] + _mod_part(mod_ref, r, 5) * y
    if final_norm:
        out = _rms(out, gfin_ref[...])
    o_ref[...] = out


def _combine(dest, yb, x, rt, mod, gfin, final_norm, row0=0, t=T_ALL):
    tm = _COMBINE_BLOCK
    blk0 = row0 // tm
    grid_spec = pltpu.PrefetchScalarGridSpec(
        num_scalar_prefetch=1,
        grid=(t // tm,),
        in_specs=[pl.BlockSpec(memory_space=pl.ANY),
                  pl.BlockSpec((tm, D_MODEL), lambda i, d: (blk0 + i, 0)),
                  pl.BlockSpec((tm, LANES), lambda i, d: (blk0 + i, 0)),
                  pl.BlockSpec(mod.shape, lambda i, d: (0, 0)),
                  pl.BlockSpec(gfin.shape, lambda i, d: (0, 0))],
        out_specs=pl.BlockSpec((tm, D_MODEL), lambda i, d: (i, 0)),
        scratch_shapes=[pltpu.VMEM((2, 2 * tm * SLABS, LANES), F32),
                        pltpu.SemaphoreType.DMA((2,))])
    return pl.pallas_call(
        functools.partial(_combine_kernel, final_norm=final_norm, row0=row0),
        grid_spec=grid_spec,
        out_shape=jax.ShapeDtypeStruct((t, D_MODEL), F32),
        compiler_params=_cparams(("arbitrary",)),
        name="moe_combine",
    )(dest, yb, x, rt, mod, gfin)


def _moe_plan(rtt, cnt):
    t = rtt.shape[1]
    eid = rtt[0:2].astype(jnp.int32)
    rank = rtt[4:6].astype(jnp.int32)
    counts = cnt[0, N_GROUPS:N_GROUPS + N_EXPERTS].astype(jnp.int32)
    padded = (counts + EXPERT_ROWS - 1) // EXPERT_ROWS * EXPERT_ROWS
    seg_end = jnp.cumsum(padded)
    seg_start = seg_end - padded
    experts = jnp.arange(N_EXPERTS, dtype=jnp.int32)
    start = jnp.sum(jnp.where(eid[..., None] == experts, seg_start, 0), axis=-1)
    dest = (start + rank).reshape(-1)
    n_blocks = (2 * t + N_EXPERTS * (EXPERT_ROWS - 1) + EXPERT_ROWS - 1) // EXPERT_ROWS
    tok = jnp.tile(jnp.arange(t, dtype=jnp.int32), 2)
    slot_tok = jnp.zeros((n_blocks * EXPERT_ROWS,), jnp.int32).at[dest].set(tok)
    first_row = jnp.arange(n_blocks, dtype=jnp.int32) * EXPERT_ROWS
    block_expert = jnp.minimum(jnp.sum((seg_end[None, :] <= first_row[:, None]).astype(jnp.int32), axis=1),
                               N_EXPERTS - 1)
    return dest, slot_tok, block_expert


def _attn_c_prompt_kernel(q_ref, k_ref, v_ref, o_ref):
    for p in range(C_HEADS // 2):
        sl = slice(p * LANES, (p + 1) * LANES)
        kb = k_ref[:, sl].astype(BF16)
        vb = v_ref[:, sl].astype(BF16)
        qp = q_ref[:, sl] * (C_HEAD_DIM ** -0.5)
        o_ref[:, sl] = _softmax_pair(qp, lambda qm: [_dot_nt(qm, kb)], lambda ps: _dot(ps[0], vb))


def _attn_c_prompt(q, k, v):
    blk = pl.BlockSpec((SEQ, C_WIDTH), lambda b: (b, 0))
    return pl.pallas_call(
        _attn_c_prompt_kernel,
        grid=(BATCH,),
        in_specs=[blk, blk, blk],
        out_specs=blk,
        out_shape=jax.ShapeDtypeStruct((T_PROMPT, C_WIDTH), F32),
        compiler_params=_cparams(("arbitrary",)),
        name="attn_c_prompt",
    )(q, k, v)


def _na_key_start(r0):
    rows = DEC_SEQ // GRID_W
    return jnp.minimum(jnp.clip(r0 - NA_ROWS // 2, 0, rows - NA_ROWS), rows - NA_KROWS)


def _attn_c_sample_kernel(q_ref, k_ref, v_ref, ck_ref, cv_ref, bias_ref, o_ref):
    rows = DEC_SEQ // GRID_W
    nblk = rows // NA_QROWS
    i = pl.program_id(2)
    r0 = i * NA_QROWS
    k0 = pl.multiple_of(_na_key_start(r0) * GRID_W, GRID_W)
    btype = jnp.where(i == 0, 0, jnp.where(i == nblk - 1, 2, 1))
    nk = NA_KROWS * GRID_W
    kw = k_ref[pl.ds(k0, nk), :].astype(BF16)
    vw = v_ref[pl.ds(k0, nk), :].astype(BF16)
    kc = ck_ref[0].astype(BF16)
    vc = cv_ref[0].astype(BF16)
    qp = q_ref[...] * (C_HEAD_DIM ** -0.5)
    lo = _lane(qp.shape) < 64
    outs = []
    for half in range(2):
        qm = jnp.where(lo if half == 0 else jnp.logical_not(lo), qp, 0.0).astype(BF16)
        s_win = _dot_nt(qm, kw) + bias_ref[0, btype, half]
        s_ctx = _dot_nt(qm, kc)
        m = jnp.maximum(s_win.max(axis=-1, keepdims=True), s_ctx.max(axis=-1, keepdims=True))
        e_win = jnp.exp(s_win - m)
        e_ctx = jnp.exp(s_ctx - m)
        l = e_win.sum(axis=-1, keepdims=True) + e_ctx.sum(axis=-1, keepdims=True)
        o = _dot(e_win.astype(BF16), vw) + _dot(e_ctx.astype(BF16), vc)
        outs.append(o / l)
    o_ref[...] = jnp.where(lo, outs[0], outs[1])


def _attn_c_sample(q, k, v, ck, cv, bias):
    rows = DEC_SEQ // GRID_W
    nblk = rows // NA_QROWS
    qrows = NA_QROWS * GRID_W
    npair = C_HEADS // 2
    return pl.pallas_call(
        _attn_c_sample_kernel,
        grid=(npair, DEC_BATCH, nblk),
        in_specs=[pl.BlockSpec((qrows, LANES), lambda p, b, i: (b * nblk + i, p)),
                  pl.BlockSpec((DEC_SEQ, LANES), lambda p, b, i: (b, p)),
                  pl.BlockSpec((DEC_SEQ, LANES), lambda p, b, i: (b, p)),
                  pl.BlockSpec((1, PAST_LEN, LANES), lambda p, b, i: (b, 0, p)),
                  pl.BlockSpec((1, PAST_LEN, LANES), lambda p, b, i: (b, 0, p)),
                  pl.BlockSpec((1, 3, 2, qrows, NA_KROWS * GRID_W), lambda p, b, i: (p, 0, 0, 0, 0))],
        out_specs=pl.BlockSpec((qrows, LANES), lambda p, b, i: (b * nblk + i, p)),
        out_shape=jax.ShapeDtypeStruct((T_SAMPLE, C_WIDTH), F32),
        compiler_params=_cparams(("arbitrary", "arbitrary", "arbitrary")),
        name="attn_c_sample",
    )(q, k, v, ck, cv, bias)


def _na_bias_tables(rpb):
    rows = DEC_SEQ // GRID_W
    nblk = rows // NA_QROWS
    w = GRID_W
    nd_r, nd_c = 2 * NA_ROWS - 1, 2 * NA_COLS - 1
    c = np.arange(w)
    cs = np.clip(c - NA_COLS // 2, 0, w - NA_COLS)
    col_ok = (c[None, :] >= cs[:, None]) & (c[None, :] < cs[:, None] + NA_COLS)
    left = w - NA_COLS
    u = jnp.pad(rpb, ((0, 0), (0, 0), (left, 2 * w - left - nd_c)))
    x = jnp.tile(u, (1, 1, w))[..., :w * (2 * w - 1)].reshape(C_HEADS, nd_r, w, 2 * w - 1)
    toe = jnp.where(col_ok[None, None], x[..., w - 1:], NEG)
    pad = NA_KROWS
    toe = jnp.pad(toe, ((0, 0), (pad, pad), (0, 0), (0, 0)), constant_values=NEG)
    toe = toe.reshape(C_HEADS // 2, 2, nd_r + 2 * pad, w, w)
    types = []
    for blk in (0, 1, nblk - 1):
        r0 = blk * NA_QROWS
        ks = min(int(np.clip(r0 - NA_ROWS // 2, 0, rows - NA_ROWS)), rows - NA_KROWS)
        per_row = []
        for i in range(NA_QROWS):
            r = r0 + i
            rs = int(np.clip(r - NA_ROWS // 2, 0, rows - NA_ROWS))
            start = ks - r + NA_ROWS - 1 + pad
            sl = toe[:, :, start:start + NA_KROWS]
            ok = np.array([rs <= ks + j < rs + NA_ROWS for j in range(NA_KROWS)])
            sl = jnp.where(ok[None, None, :, None, None], sl, NEG)
            per_row.append(sl.transpose(0, 1, 3, 2, 4))
        types.append(jnp.stack(per_row, axis=2))
    tab = jnp.stack(types, axis=1)
    return tab.reshape(C_HEADS // 2, 3, 2, NA_QROWS * w, NA_KROWS * w)


def _rope_tables():
    half = A_HEAD_DIM // 2
    t = jnp.arange(DEC_SEQ)
    row = (t // GRID_W).astype(F32)
    colp = (t % GRID_W).astype(F32)
    freqs = 1.0 / (ROPE_BASE ** (jnp.arange(0, half, 2, dtype=F32) / half))
    d = np.arange(LANES) % A_HEAD_DIM
    pos = jnp.where(jnp.asarray(d < half)[None, :], row[:, None], colp[:, None])
    ang = pos * freqs[d % (half // 2)][None, :]
    sign = jnp.asarray(np.where((d % half) < half // 2, -1.0, 1.0), F32)[None, :]
    return jnp.cos(ang), jnp.sin(ang) * sign


def _head_avg_matrix():
    idx = np.arange(LANES) // A_HEAD_DIM
    return jnp.asarray((idx[:, None] == idx[None, :]).astype(np.float32) / A_HEAD_DIM, BF16)


def _tri_matrices():
    i = np.arange(MLSTM_CHUNK)
    upper = (i[:, None] <= i[None, :]).astype(np.float32)
    lower = (i[:, None] >= i[None, :]).astype(np.float32)
    return jnp.asarray(np.stack([upper, lower]), BF16)


def _router_weights(wg, bg, we, be):
    w = jnp.zeros((D_MODEL, LANES), F32).at[:, :N_GROUPS].set(wg).at[:, N_GROUPS:N_GROUPS + N_EXPERTS].set(we)
    b = jnp.zeros((1, LANES), F32).at[0, :N_GROUPS].set(bg).at[0, N_GROUPS:N_GROUPS + N_EXPERTS].set(be)
    hi = w.astype(BF16)
    lo = (w - hi.astype(F32)).astype(BF16)
    return hi, lo, b


def _moe(xn, rtt, cnt, w1, w3, w2):
    dest, slot_tok, block_expert = _moe_plan(rtt, cnt)
    return dest, _experts(block_expert, slot_tok, xn, w1, w3, w2)


def kernel(x_prompt, x_sample, cache_attn_k, cache_attn_v, state_mlstm_C, state_mlstm_n, state_mlstm_m,
           cache_na_k, cache_na_v, c, c_ctx, norm_mix, norm_ffn, norm_final, ada_w, ada_b,
           ab_w_in, ab_w_out, ab_q_norm, ab_k_norm, ab_gate_bias, ab_out_norm,
           na_w_in, na_w_out, na_rpb, moe_wg, moe_bg, moe_we, moe_be, moe_w1, moe_w3, moe_w2):
    xp = x_prompt.reshape(T_PROMPT, D_MODEL)
    xs = x_sample.reshape(T_SAMPLE, D_MODEL)
    cond =jnp.zeros((N_COND, D_MODEL), F32).at[0].set(c_ctx).at[1:1 + DEC_BATCH].set(c)
    mod = _modulation(cond, ada_w, ada_b)
    gfin = norm_final.reshape(1, D_MODEL)

    w_in = ab_w_in[0]
    o_aq, o_ak, o_av, o_bq, o_bk, o_bv, o_bo, o_bg = np.cumsum((0,) + (A_WIDTH, A_KV_WIDTH, A_KV_WIDTH,
                                                                       B_WIDTH, B_WIDTH, B_WIDTH, B_WIDTH))
    wb = w_in[:, o_bq:o_bg].astype(BF16)
    wq = w_in[:, o_aq:o_ak].astype(BF16)
    wkv = w_in[:, o_ak:o_bq].astype(BF16)
    wg = w_in[:, o_bg:o_bg + 4 * B_HEADS].reshape(D_MODEL, 4, B_HEADS)
    wgt = jnp.zeros((B_HEADS, 8, D_MODEL), F32).at[:, :4, :].set(wg.transpose(2, 1, 0))
    wgt = wgt.reshape(8 * B_HEADS, D_MODEL).astype(BF16)
    g_mix = norm_mix[0].reshape(1, D_MODEL)
    ob_p, oq_p, okv_p, ogt_p = _proj_ab(xp, mod[0], g_mix, wb, wq, wkv, wgt, 0)
    ob_s, oq_s, okv_s, ogt_s = _proj_ab(xs, mod[0], g_mix, wb, wq, wkv, wgt, T_PROMPT)

    qn = jnp.tile(ab_q_norm[0], 2).reshape(1, LANES)
    kn = jnp.tile(ab_k_norm[0], 2).reshape(1, LANES)
    bd = _head_avg_matrix()
    cos, sin = _rope_tables()
    a_p, new_k, new_v = _attn_a_prompt(oq_p, okv_p, qn, kn, bd)
    ck = cache_attn_k[:, 0].reshape(DEC_BATCH, PAST_LEN, A_KV_WIDTH)
    cv = cache_attn_v[:, 0].reshape(DEC_BATCH, PAST_LEN, A_KV_WIDTH)
    a_s = _attn_a_sample(oq_s, okv_s, ck, cv, cos, sin, qn, kn, bd)

    gb = ab_gate_bias[0]
    brow = jnp.zeros((B_HEADS, 8, LANES), F32).at[:, :4, :].set(
        jnp.broadcast_to(gb.T[:, :, None], (B_HEADS, 4, LANES)))
    onorm = ab_out_norm[0].reshape(B_HEADS, 1, B_HEAD_DIM)
    tri = _tri_matrices()
    b_p, cT, nT, mT = _mlstm(ob_p, ogt_p, brow, onorm, tri, n=SEQ, nseq=BATCH, emit_state=True)
    c0 = state_mlstm_C[:, 0]
    n0 = state_mlstm_n[:, 0].transpose(0, 2, 1, 3)
    m0 = jnp.broadcast_to(state_mlstm_m[:, 0].transpose(0, 2, 1)[..., None], (DEC_BATCH, B_HEADS, 2, LANES))
    (b_s,) = _mlstm(ob_s, ogt_s, brow, onorm, tri, n=DEC_SEQ, nseq=DEC_BATCH, init=(c0, n0, m0))

    w_out = ab_w_out[0].astype(BF16)
    wrh, wrl, br = _router_weights(moe_wg[0], moe_bg[0], moe_we[0], moe_be[0])
    x1, xn, rt, rtt, cnt = _post([(xp, xs), (a_p, a_s), (b_p, b_s)], [w_out[:A_WIDTH], w_out[A_WIDTH:]], mod[0],
                                 norm_ffn[0].reshape(1, D_MODEL), wrh, wrl, br)
    dest, yb = _moe(xn, rtt, cnt, moe_w1[0], moe_w3[0], moe_w2[0])
    x = _combine(dest, yb, x1, rt, mod[0], gfin, False)

    g_mix = norm_mix[1].reshape(1, D_MODEL)
    w_in = na_w_in[0].astype(BF16)
    q_p, k_p, v_p = _proj_na(x, mod[1], g_mix, w_in, 0, T_PROMPT)
    q_s, k_s, v_s = _proj_na(x, mod[1], g_mix, w_in, T_PROMPT, T_SAMPLE)
    o_p = _attn_c_prompt(q_p, k_p, v_p)
    nck = cache_na_k[:, 0].reshape(DEC_BATCH, PAST_LEN, C_WIDTH)
    ncv = cache_na_v[:, 0].reshape(DEC_BATCH, PAST_LEN, C_WIDTH)
    o_s = _attn_c_sample(q_s, k_s, v_s, nck, ncv, _na_bias_tables(na_rpb[0]))
    wrh, wrl, br = _router_weights(moe_wg[1], moe_bg[1], moe_we[1], moe_be[1])
    x1, xn, rt, rtt, cnt = _post([x, (o_p, o_s)], [na_w_out[0].astype(BF16)], mod[1],
                                 norm_ffn[1].reshape(1, D_MODEL), wrh, wrl, br)
    dest, yb = _moe(xn, rtt, cnt, moe_w1[1], moe_w3[1], moe_w2[1])
    y_prompt = _combine(dest, yb, x1, rt, mod[1], gfin, True, 0, T_PROMPT).reshape(BATCH, SEQ, D_MODEL)
    y_sample = _combine(dest, yb, x1, rt, mod[1], gfin, True, T_PROMPT, T_SAMPLE).reshape(DEC_BATCH, DEC_SEQ, D_MODEL)
    new_attn_k = new_k.reshape(BATCH, 1, SEQ, A_KV_HEADS, A_HEAD_DIM)
    new_attn_v = new_v.reshape(BATCH, 1, SEQ, A_KV_HEADS, A_HEAD_DIM)
    new_mlstm_C = cT[:, None]
    new_mlstm_n = nT.transpose(0, 2, 1, 3)[:, None]
    new_mlstm_m = mT[..., 0].transpose(0, 2, 1)[:, None]
    new_na_k = k_p.reshape(BATCH, 1, SEQ, C_HEADS, C_HEAD_DIM)
    new_na_v = v_p.reshape(BATCH, 1, SEQ, C_HEADS, C_HEAD_DIM)
    return (y_prompt, y_sample, new_attn_k, new_attn_v, new_mlstm_C, new_mlstm_n, new_mlstm_m,
            new_na_k, new_na_v)
```

```python
import functools

import numpy as np
import jax
import jax.numpy as jnp
from jax import lax
from jax.experimental import pallas as pl
from jax.experimental.pallas import tpu as pltpu

F32 = jnp.float32
BF16 = jnp.bfloat16

D_MODEL = 1024
BATCH = 32
SEQ = 256
DEC_BATCH = 4
DEC_SEQ = 2048
PAST_LEN = 256
GRID_W = 64
A_HEADS = 8
A_KV_HEADS = 2
A_HEAD_DIM = 64
A_WIDTH = A_HEADS * A_HEAD_DIM
A_KV_WIDTH = A_KV_HEADS * A_HEAD_DIM
B_HEADS = 4
B_HEAD_DIM = 128
B_WIDTH = B_HEADS * B_HEAD_DIM
MLSTM_CHUNK = 128
C_HEADS = 16
C_HEAD_DIM = 64
C_WIDTH = C_HEADS * C_HEAD_DIM
NA_ROWS = 8
NA_COLS = 16
N_GROUPS = 4
EXPERTS_PER_GROUP = 8
N_EXPERTS = N_GROUPS * EXPERTS_PER_GROUP
D_EXPERT = 512
MOE_BLOCK = 128
ROPE_BASE = 10000.0
NORM_EPS = 1e-6

T_PROMPT = BATCH * SEQ
T_SAMPLE = DEC_BATCH * DEC_SEQ
T_ALL = T_PROMPT + T_SAMPLE
N_COND = 8
LANES = 128
SLABS = D_MODEL // LANES
TOK_BLOCK = 256
EXPERT_ROWS = 256
NA_QROWS = 4
NA_KROWS = 12
NEG = -1e30
VMEM_LIMIT = 56 * 1024 * 1024


def _cparams(sem):
    return pltpu.CompilerParams(dimension_semantics=sem, vmem_limit_bytes=VMEM_LIMIT)


def _split2(x):
    hi = x.astype(BF16)
    lo = (x - hi.astype(F32)).astype(BF16)
    return hi, lo


def _split3(x):
    hi = x.astype(BF16)
    r = x - hi.astype(F32)
    mid = r.astype(BF16)
    lo = (r - mid.astype(F32)).astype(BF16)
    return hi, mid, lo


def _dot(a, b):
    return jnp.dot(a, b, preferred_element_type=F32)


def _dot_nt(a, b):
    return lax.dot_general(a, b, (((1,), (1,)), ((), ())), preferred_element_type=F32)


def _dot_exact_rhs(x, b):
    hi, mid, lo = _split3(x)
    return _dot(hi, b) + _dot(mid, b) + _dot(lo, b)


def _rms(x, g):
    ms = jnp.mean(x * x, axis=-1, keepdims=True)
    return (x * lax.rsqrt(ms + NORM_EPS)) * g


def _mod_row(tok0):
    return jnp.where(tok0 < T_PROMPT, 0, 1 + (tok0 - T_PROMPT) // DEC_SEQ)


def _mod_part(mod_ref, r, idx):
    return mod_ref[pl.ds(r, 1), idx * D_MODEL:(idx + 1) * D_MODEL]


def _head_rms(x, w, bd):
    hi, lo = _split2(x * x)
    ms = _dot(hi, bd) + _dot(lo, bd)
    return (x * lax.rsqrt(ms + NORM_EPS)) * w


def _to_token_tiles(ref, x):
    m = x.shape[0]
    for s in range(SLABS):
        ref[pl.ds(s, m, stride=SLABS), :] = x[:, s * LANES:(s + 1) * LANES]


def _from_token_tiles(ref, tile0, m):
    return jnp.concatenate([ref[pl.ds(tile0 * SLABS + s, m, stride=SLABS), :] for s in range(SLABS)], axis=1)


def _lane(shape):
    return lax.broadcasted_iota(jnp.int32, shape, len(shape) - 1)


def _dup_half(x, g):
    xr = pltpu.roll(x, 64, 1)
    lo = _lane(x.shape) < 64
    return jnp.where(lo, x, xr) if g == 0 else jnp.where(lo, xr, x)


def _rope(x, cos, sin_signed):
    lane = _lane(x.shape)
    partner = jnp.where((lane % 32) < 16, pltpu.roll(x, LANES - 16, 1), pltpu.roll(x, 16, 1))
    return x * cos + partner * sin_signed


def _softmax_pair(qp, score_fn, value_fn):
    lo = _lane(qp.shape) < 64
    outs = []
    for half in range(2):
        qm = jnp.where(lo if half == 0 else jnp.logical_not(lo), qp, 0.0).astype(BF16)
        ss = score_fn(qm)
        m = ss[0].max(axis=-1, keepdims=True)
        for s in ss[1:]:
            m = jnp.maximum(m, s.max(axis=-1, keepdims=True))
        es = [jnp.exp(s - m) for s in ss]
        l = es[0].sum(axis=-1, keepdims=True)
        for e in es[1:]:
            l = l + e.sum(axis=-1, keepdims=True)
        o = value_fn([e.astype(BF16) for e in es])
        outs.append(o / l)
    return jnp.where(lo, outs[0], outs[1])


def _mod_kernel(cond_ref, w_ref, b_ref, o_ref):
    c = cond_ref[...]
    s = c * jax.nn.sigmoid(c)
    s_hi, s_lo = _split2(s)
    w_hi, w_lo = _split2(w_ref[0])
    o_ref[0] = _dot(s_hi, w_hi) + _dot(s_lo, w_hi) + _dot(s_hi, w_lo) + b_ref[0]


def _modulation(cond, ada_w, ada_b):
    depth, d, n = ada_w.shape
    tn = 1536
    return pl.pallas_call(
        _mod_kernel,
        grid=(depth, n // tn),
        in_specs=[pl.BlockSpec((N_COND, d), lambda l, j: (0, 0)),
                  pl.BlockSpec((1, d, tn), lambda l, j: (l, 0, j)),
                  pl.BlockSpec((1, 1, tn), lambda l, j: (l, 0, j))],
        out_specs=pl.BlockSpec((1, N_COND, tn), lambda l, j: (l, 0, j)),
        out_shape=jax.ShapeDtypeStruct((depth, N_COND, n), F32),
        compiler_params=_cparams(("arbitrary", "arbitrary")),
        name="adaln_modulation",
    )(cond, ada_w, ada_b.reshape(depth, 1, n))


def _norm_mod(x_ref, mod_ref, g_ref, shift_idx, scale_idx, row0):
    r = _mod_row(row0 + pl.program_id(0) * x_ref.shape[0])
    h = _rms(x_ref[...], g_ref[...])
    return h * (1.0 + _mod_part(mod_ref, r, scale_idx)) + _mod_part(mod_ref, r, shift_idx)


def _proj_ab_kernel(x_ref, mod_ref, g_ref, wb_ref, wq_ref, wkv_ref, wgt_ref, ob_ref, oq_ref, okv_ref, ogt_ref,
                    *, row0):
    tm = x_ref.shape[0]
    hb = _norm_mod(x_ref, mod_ref, g_ref, 0, 1, row0).astype(BF16)
    ob_ref[...] = _dot(hb, wb_ref[...])
    oq_ref[...] = _dot(hb, wq_ref[...])
    okv_ref[...] = _dot(hb, wkv_ref[...])
    gt = _dot_nt(wgt_ref[...], hb)
    for j in range(tm // LANES):
        ogt_ref[j] = gt[:, j * LANES:(j + 1) * LANES]


def _proj_ab(x, mod, g, wb, wq, wkv, wgt, row0):
    t = x.shape[0]
    tm = TOK_BLOCK
    full = lambda a: pl.BlockSpec(a.shape, lambda i: (0,) * a.ndim)
    return pl.pallas_call(
        functools.partial(_proj_ab_kernel, row0=row0),
        grid=(t // tm,),
        in_specs=[pl.BlockSpec((tm, D_MODEL), lambda i: (i, 0)), full(mod), full(g),
                  full(wb), full(wq), full(wkv), full(wgt)],
        out_specs=[pl.BlockSpec((tm, 4 * B_WIDTH), lambda i: (i, 0)),
                   pl.BlockSpec((tm, A_WIDTH), lambda i: (i, 0)),
                   pl.BlockSpec((tm, 2 * A_KV_WIDTH), lambda i: (i, 0)),
                   pl.BlockSpec((tm // LANES, 8 * B_HEADS, LANES), lambda i: (i, 0, 0))],
        out_shape=[jax.ShapeDtypeStruct((t, 4 * B_WIDTH), F32),
                   jax.ShapeDtypeStruct((t, A_WIDTH), F32),
                   jax.ShapeDtypeStruct((t, 2 * A_KV_WIDTH), F32),
                   jax.ShapeDtypeStruct((t // LANES, 8 * B_HEADS, LANES), F32)],
        compiler_params=_cparams(("arbitrary",)),
        name="proj_ab",
    )(x, mod, g, wb, wq, wkv, wgt)


def _proj_na_kernel(x_ref, mod_ref, g_ref, w_ref, q_ref, k_ref, v_ref, *, row0):
    hb = _norm_mod(x_ref, mod_ref, g_ref, 0, 1, row0).astype(BF16)
    for j, o_ref in enumerate((q_ref, k_ref, v_ref)):
        o_ref[...] = _dot(hb, w_ref[:, j * C_WIDTH:(j + 1) * C_WIDTH])


def _proj_na(x, mod, g, w, row0, t):
    tm = TOK_BLOCK
    blk0 = row0 // tm
    full = lambda a: pl.BlockSpec(a.shape, lambda i: (0,) * a.ndim)
    return pl.pallas_call(
        functools.partial(_proj_na_kernel, row0=row0),
        grid=(t // tm,),
        in_specs=[pl.BlockSpec((tm, D_MODEL), lambda i: (blk0 + i, 0)), full(mod), full(g), full(w)],
        out_specs=[pl.BlockSpec((tm, C_WIDTH), lambda i: (i, 0))] * 3,
        out_shape=[jax.ShapeDtypeStruct((t, C_WIDTH), F32)] * 3,
        compiler_params=_cparams(("arbitrary",)),
        name="proj_na",
    )(x, mod, g, w)


def _gqa_block(q_ref, qn, bd, kd_ref, vd_ref, o_ref, rope=None):
    for p in range(A_HEADS // 2):
        g = p // (A_HEADS // 2 // A_KV_HEADS)
        qp = _head_rms(q_ref[:, p * LANES:(p + 1) * LANES], qn, bd)
        if rope is not None:
            qp = _rope(qp, rope[0], rope[1])
        qp = qp * (A_HEAD_DIM ** -0.5)
        o_ref[:, p * LANES:(p + 1) * LANES] = _softmax_pair(
            qp, lambda qm: [_dot_nt(qm, kd_ref[g])], lambda ps: _dot(ps[0], vd_ref[g]))


def _attn_a_prompt_kernel(q_ref, kv_ref, qn_ref, kn_ref, bd_ref, o_ref, knew_ref, vnew_ref, kd_ref, vd_ref):
    bd = bd_ref[...]
    k = _head_rms(kv_ref[:, :LANES], kn_ref[...], bd)
    v = kv_ref[:, LANES:]
    knew_ref[...] = k
    vnew_ref[...] = v
    for g in range(A_KV_HEADS):
        kd_ref[g] = _dup_half(k, g).astype(BF16)
        vd_ref[g] = _dup_half(v, g).astype(BF16)
    _gqa_block(q_ref, qn_ref[...], bd, kd_ref, vd_ref, o_ref)


def _attn_a_prompt(q, kv, qn, kn, bd):
    nb = BATCH
    full = lambda a: pl.BlockSpec(a.shape, lambda b: (0,) * a.ndim)
    return pl.pallas_call(
        _attn_a_prompt_kernel,
        grid=(nb,),
        in_specs=[pl.BlockSpec((SEQ, A_WIDTH), lambda b: (b, 0)),
                  pl.BlockSpec((SEQ, 2 * A_KV_WIDTH), lambda b: (b, 0)),
                  full(qn), full(kn), full(bd)],
        out_specs=[pl.BlockSpec((SEQ, A_WIDTH), lambda b: (b, 0)),
                   pl.BlockSpec((SEQ, A_KV_WIDTH), lambda b: (b, 0)),
                   pl.BlockSpec((SEQ, A_KV_WIDTH), lambda b: (b, 0))],
        out_shape=[jax.ShapeDtypeStruct((T_PROMPT, A_WIDTH), F32),
                   jax.ShapeDtypeStruct((T_PROMPT, A_KV_WIDTH), F32),
                   jax.ShapeDtypeStruct((T_PROMPT, A_KV_WIDTH), F32)],
        scratch_shapes=[pltpu.VMEM((A_KV_HEADS, SEQ, LANES), BF16),
                        pltpu.VMEM((A_KV_HEADS, SEQ, LANES), BF16)],
        compiler_params=_cparams(("arbitrary",)),
        name="attn_a_prompt",
    )(q, kv, qn, kn, bd)


_A_QBLOCK = 256


def _attn_a_sample_kernel(q_ref, kv_ref, ck_ref, cv_ref, cos_ref, sin_ref, cosq_ref, sinq_ref,
                          qn_ref, kn_ref, bd_ref, o_ref, kd_ref, vd_ref):
    bd = bd_ref[...]

    @pl.when(pl.program_id(1) == 0)
    def _():
        for g in range(A_KV_HEADS):
            kd_ref[g, :PAST_LEN] = _dup_half(ck_ref[0], g).astype(BF16)
            vd_ref[g, :PAST_LEN] = _dup_half(cv_ref[0], g).astype(BF16)
        rows = 256
        for c in range(DEC_SEQ // rows):
            sl = slice(c * rows, (c + 1) * rows)
            k = _head_rms(kv_ref[sl, :LANES], kn_ref[...], bd)
            k = _rope(k, cos_ref[sl, :], sin_ref[sl, :])
            v = kv_ref[sl, LANES:]
            dst = slice(PAST_LEN + c * rows, PAST_LEN + (c + 1) * rows)
            for g in range(A_KV_HEADS):
                kd_ref[g, dst] = _dup_half(k, g).astype(BF16)
                vd_ref[g, dst] = _dup_half(v, g).astype(BF16)

    _gqa_block(q_ref, qn_ref[...], bd, kd_ref, vd_ref, o_ref, rope=(cosq_ref[...], sinq_ref[...]))


def _attn_a_sample(q, kv, ck, cv, cos, sin, qn, kn, bd):
    nq = DEC_SEQ // _A_QBLOCK
    full = lambda a: pl.BlockSpec(a.shape, lambda b, i: (0,) * a.ndim)
    tk = PAST_LEN + DEC_SEQ
    return pl.pallas_call(
        _attn_a_sample_kernel,
        grid=(DEC_BATCH, nq),
        in_specs=[pl.BlockSpec((_A_QBLOCK, A_WIDTH), lambda b, i: (b * nq + i, 0)),
                  pl.BlockSpec((DEC_SEQ, 2 * A_KV_WIDTH), lambda b, i: (b, 0)),
                  pl.BlockSpec((1, PAST_LEN, A_KV_WIDTH), lambda b, i: (b, 0, 0)),
                  pl.BlockSpec((1, PAST_LEN, A_KV_WIDTH), lambda b, i: (b, 0, 0)),
                  full(cos), full(sin),
                  pl.BlockSpec((_A_QBLOCK, LANES), lambda b, i: (i, 0)),
                  pl.BlockSpec((_A_QBLOCK, LANES), lambda b, i: (i, 0)),
                  full(qn), full(kn), full(bd)],
        out_specs=pl.BlockSpec((_A_QBLOCK, A_WIDTH), lambda b, i: (b * nq + i, 0)),
        out_shape=jax.ShapeDtypeStruct((T_SAMPLE, A_WIDTH), F32),
        scratch_shapes=[pltpu.VMEM((A_KV_HEADS, tk, LANES), BF16),
                        pltpu.VMEM((A_KV_HEADS, tk, LANES), BF16)],
        compiler_params=_cparams(("arbitrary", "arbitrary")),
        name="attn_a_sample",
    )(q, kv, ck, cv, cos, sin, cos, sin, qn, kn, bd)


def _log_sigmoid(x):
    return -(jnp.maximum(-x, 0.0) + jnp.log1p(jnp.exp(-jnp.abs(x))))


def _mlstm_kernel(*refs, n, has_init, emit_state):
    it = iter(refs)
    q_ref, k_ref, v_ref, og_ref, gt_ref, brow_ref, onorm_ref, tri_ref = [next(it) for _ in range(8)]
    if has_init:
        c0_ref, n0_ref, m0_ref = [next(it) for _ in range(3)]
    out_ref = next(it)
    if emit_state:
        cT_ref, nT_ref, mT_ref = [next(it) for _ in range(3)]
    h_ref, row_ref, col_ref, tbuf_ref, c_ref, n_ref = [next(it) for _ in range(6)]

    L = MLSTM_CHUNK
    nc = n // L

    gt = gt_ref[...] + brow_ref[0][None]
    lf = _log_sigmoid(gt).reshape(nc * 8, L)
    cum_f = _dot_exact_rhs(lf, tri_ref[0]).reshape(nc, 8, L)
    cum_b = _dot_exact_rhs(lf, tri_ref[1]).reshape(nc, 8, L)
    cf = cum_f[:, 1:2, :]
    cb = cum_b[:, 3:4, :]
    row_ref[:, 0:1, :] = cf
    row_ref[:, 1:2, :] = gt[:, 0:1, :] - cf
    row_ref[:, 2:3, :] = cb
    row_ref[:, 3:4, :] = gt[:, 2:3, :] - cb
    lf3 = lf.reshape(nc, 8, L)
    row_ref[:, 4:5, :] = jnp.broadcast_to(lf3[:, 1:2, :].sum(axis=-1, keepdims=True), (nc, 1, L))
    row_ref[:, 5:6, :] = jnp.broadcast_to(lf3[:, 3:4, :].sum(axis=-1, keepdims=True), (nc, 1, L))
    row_ref[:, 6:8, :] = jnp.zeros((nc, 2, L), F32)

    tbuf_ref[...] = jnp.zeros((L, L), F32)

    def to_cols(c, carry):
        tbuf_ref[0:8, :] = row_ref[c]
        col_ref[pl.ds(pl.multiple_of(c * L, L), L), :] = tbuf_ref[...].T
        return carry

    lax.fori_loop(0, nc, to_cols, 0)

    if has_init:
        c_ref[...] = c0_ref[0, :, 0]
        n_ref[0:2, :] = n0_ref[0, 0]
        m_init = (m0_ref[0, 0, 0:1, 0:1], m0_ref[0, 0, 1:2, 0:1])
    else:
        c_ref[...] = jnp.zeros((2, L, L), F32)
        n_ref[...] = jnp.zeros((8, L), F32)
        m_init = (jnp.zeros((1, 1), F32), jnp.zeros((1, 1), F32))

    t_idx = lax.broadcasted_iota(jnp.int32, (L, L), 0)
    s_idx = lax.broadcasted_iota(jnp.int32, (L, L), 1)
    masks = (s_idx <= t_idx, s_idx >= t_idx)

    def step(c, d, m):
        r0 = pl.multiple_of(c * L, L)
        q = q_ref[pl.ds(r0, L), :]
        ks = k_ref[pl.ds(r0, L), :] * (B_HEAD_DIM ** -0.5)
        vb = v_ref[pl.ds(r0, L), :].astype(BF16)
        qb = q.astype(BF16)
        rows = row_ref[c]
        cols = col_ref[pl.ds(r0, L), :]
        a_row = rows[2 * d + 1:2 * d + 2, :]
        tot = rows[4 + d:5 + d, 0:1]
        cum_col = cols[:, 2 * d:2 * d + 1]
        a_col = cols[:, 2 * d + 1:2 * d + 2]
        dlog = jnp.where(masks[d], cum_col + a_row, -jnp.inf)
        inter = cum_col + m
        m_t = jnp.maximum(inter, dlog.max(axis=-1, keepdims=True))
        w_in = jnp.exp(dlog - m_t)
        w_st = jnp.exp(inter - m_t)
        a = _dot_nt(qb, ks.astype(BF16)) * w_in
        cmat = c_ref[d]
        nvec = n_ref[d:d + 1, :]
        num = w_st * _dot(qb, cmat.astype(BF16)) + _dot(a.astype(BF16), vb)
        den = w_st * (q * nvec).sum(axis=-1, keepdims=True) + a.sum(axis=-1, keepdims=True)
        hx = num / jnp.maximum(jnp.abs(den), jnp.exp(-m_t))
        if d == 0:
            h_ref[0, pl.ds(r0, L), :] = hx
        else:
            h_ref[1, pl.ds(r0, L), :] = hx
        m_new = jnp.maximum(tot + m, a_row.max(axis=-1, keepdims=True) + tot)
        ws = jnp.exp(a_col + tot - m_new)
        wc = jnp.exp(tot + m - m_new)
        kw = ks * ws
        c_ref[d] = wc * cmat + _dot(kw.T.astype(BF16), vb)
        n_ref[d:d + 1, :] = wc * nvec + kw.sum(axis=0, keepdims=True)
        return m_new

    def body(i, carry):
        return step(i, 0, carry[0]), step(nc - 1 - i, 1, carry[1])

    m_f, m_b = lax.fori_loop(0, nc, body, m_init)

    hm = h_ref[0] + h_ref[1]
    y = _rms(hm, onorm_ref[0])
    out_ref[...] = y * jax.nn.sigmoid(og_ref[...])

    if emit_state:
        cT_ref[0, :, 0] = c_ref[...]
        nT_ref[0, 0] = n_ref[0:2, :]
        mT_ref[0, 0, 0:1, :] = jnp.broadcast_to(m_f, (1, L))
        mT_ref[0, 0, 1:2, :] = jnp.broadcast_to(m_b, (1, L))


def _mlstm(ob, ogt, brow, onorm, tri, *, n, nseq, init=None, emit_state=False):
    L = MLSTM_CHUNK
    nc = n // L
    H = B_HEADS
    col = lambda part: (lambda b, h: (b, part * H + h))
    in_specs = [pl.BlockSpec((n, L), col(0)), pl.BlockSpec((n, L), col(1)), pl.BlockSpec((n, L), col(2)),
                pl.BlockSpec((n, L), col(3)),
                pl.BlockSpec((nc, 8, L), lambda b, h: (b, h, 0)),
                pl.BlockSpec((1, 8, L), lambda b, h: (h, 0, 0)),
                pl.BlockSpec((1, 1, L), lambda b, h: (h, 0, 0)),
                pl.BlockSpec(tri.shape, lambda b, h: (0, 0, 0))]
    args = [ob, ob, ob, ob, ogt, brow, onorm, tri]
    if init is not None:
        c0, n0, m0 = init
        in_specs += [pl.BlockSpec((1, 2, 1, L, L), lambda b, h: (b, 0, h, 0, 0)),
                     pl.BlockSpec((1, 1, 2, L), lambda b, h: (b, h, 0, 0)),
                     pl.BlockSpec((1, 1, 2, L), lambda b, h: (b, h, 0, 0))]
        args += [c0, n0, m0]
    out_specs = [pl.BlockSpec((n, L), lambda b, h: (b, h))]
    out_shape = [jax.ShapeDtypeStruct((nseq * n, B_WIDTH), F32)]
    if emit_state:
        out_specs += [pl.BlockSpec((1, 2, 1, L, L), lambda b, h: (b, 0, h, 0, 0)),
                      pl.BlockSpec((1, 1, 2, L), lambda b, h: (b, h, 0, 0)),
                      pl.BlockSpec((1, 1, 2, L), lambda b, h: (b, h, 0, 0))]
        out_shape += [jax.ShapeDtypeStruct((nseq, 2, H, L, L), F32),
                      jax.ShapeDtypeStruct((nseq, H, 2, L), F32),
                      jax.ShapeDtypeStruct((nseq, H, 2, L), F32)]
    return pl.pallas_call(
        functools.partial(_mlstm_kernel, n=n, has_init=init is not None, emit_state=emit_state),
        grid=(nseq, H),
        in_specs=in_specs,
        out_specs=out_specs,
        out_shape=out_shape,
        scratch_shapes=[pltpu.VMEM((2, n, L), F32),
                        pltpu.VMEM((nc, 8, L), F32),
                        pltpu.VMEM((n, L), F32),
                        pltpu.VMEM((L, L), F32),
                        pltpu.VMEM((2, L, L), F32),
                        pltpu.VMEM((8, L), F32)],
        compiler_params=_cparams(("arbitrary", "arbitrary")),
        name="mlstm_init" if init is not None else "mlstm",
    )(*args)


def _router(logits):
    lane = _lane(logits.shape).astype(F32)
    big = 1e9
    gl = jnp.where(lane < N_GROUPS, logits, -jnp.inf)
    gmax = gl.max(axis=-1, keepdims=True)
    g_sel = jnp.where(gl == gmax, lane, big).min(axis=-1, keepdims=True)
    g_prob = 1.0 / jnp.exp(gl - gmax).sum(axis=-1, keepdims=True)
    lo = N_GROUPS + EXPERTS_PER_GROUP * g_sel
    el = jnp.where(lane >= lo, jnp.where(lane < lo + EXPERTS_PER_GROUP, logits, -jnp.inf), -jnp.inf)
    v1 = el.max(axis=-1, keepdims=True)
    i1 = jnp.where(el == v1, lane, big).min(axis=-1, keepdims=True)
    el2 = jnp.where(lane == i1, -jnp.inf, el)
    v2 = el2.max(axis=-1, keepdims=True)
    i2 = jnp.where(el2 == v2, lane, big).min(axis=-1, keepdims=True)
    e2 = jnp.exp(v2 - v1)
    w1 = g_prob / (1.0 + e2)
    w2 = g_prob * e2 / (1.0 + e2)
    return i1, i2, w1, w2


def _read_tokens(refs, is_prompt):
    if len(refs) == 1:
        return refs[0][...]
    return jnp.where(is_prompt, refs[0][...], refs[1][...])


def _post_kernel(*refs, groups):
    it = iter(refs)
    tok_refs = [[next(it) for _ in range(n)] for n in groups]
    w_refs = [next(it) for _ in range(len(groups) - 1)]
    mod_ref, g_ref, wrh_ref, wrl_ref, br_ref, ls_ref, sel_ref = [next(it) for _ in range(7)]
    xnew_ref, xn_ref, rt_ref, rtt_ref, cnt_ref = [next(it) for _ in range(5)]
    run_ref = next(it)
    i = pl.program_id(0)
    tm = xnew_ref.shape[0]
    is_prompt = i * tm < T_PROMPT
    r = _mod_row(i * tm)
    acc = None
    for a_refs, w_ref in zip(tok_refs[1:], w_refs):
        d = _dot(_read_tokens(a_refs, is_prompt).astype(BF16), w_ref[...])
        acc = d if acc is None else acc + d
    xnew = _read_tokens(tok_refs[0], is_prompt) + _mod_part(mod_ref, r, 2) * acc
    xnew_ref[...] = xnew
    xn = _rms(xnew, g_ref[...]) * (1.0 + _mod_part(mod_ref, r, 4)) + _mod_part(mod_ref, r, 3)
    _to_token_tiles(xn_ref, xn)
    x_hi, x_lo = _split2(xn)
    logits = _dot(x_hi, wrh_ref[...]) + _dot(x_lo, wrh_ref[...]) + _dot(x_hi, wrl_ref[...]) + br_ref[...]
    i1, i2, w1, w2 = _router(logits)

    @pl.when(i == 0)
    def _():
        run_ref[...] = jnp.zeros(run_ref.shape, F32)

    lane = _lane(logits.shape).astype(F32)
    member = jnp.where(lane == i1, 1.0, jnp.where(lane == i2, 1.0, 0.0))
    before = _dot(ls_ref[...], member.astype(BF16)) + run_ref[...]
    rank1 = jnp.where(lane == i1, before, 0.0).sum(axis=-1, keepdims=True)
    rank2 = jnp.where(lane == i2, before, 0.0).sum(axis=-1, keepdims=True)
    run_ref[...] = run_ref[...] + member.sum(axis=0, keepdims=True)
    cnt_ref[...] = run_ref[...]
    cols = (i1 - N_GROUPS, i2 - N_GROUPS, w1, w2, rank1, rank2)
    rt = jnp.zeros(logits.shape, F32)
    for k, c in enumerate(cols):
        rt = jnp.where(lane == k, c, rt)
    rt_ref[...] = rt
    hi, mid, lo = _split3(rt)
    sel = sel_ref[...]
    rtt_ref[...] = _dot_nt(sel, hi) + _dot_nt(sel, mid) + _dot_nt(sel, lo)


def _post(tok_ops, w_list, mod, g, wrh, wrl, br):
    t = T_ALL
    tm = TOK_BLOCK
    npb = T_PROMPT // tm
    full = lambda a: pl.BlockSpec(a.shape, lambda i: (0,) * a.ndim)
    specs, args, groups = [], [], []
    for op in tok_ops:
        if isinstance(op, tuple):
            w = op[0].shape[1]
            specs += [pl.BlockSpec((tm, w), lambda i: (jnp.minimum(i, npb - 1), 0)),
                      pl.BlockSpec((tm, w), lambda i: (jnp.maximum(i - npb, 0), 0))]
            args += list(op)
            groups.append(2)
        else:
            specs.append(pl.BlockSpec((tm, op.shape[1]), lambda i: (i, 0)))
            args.append(op)
            groups.append(1)
    idx = np.arange(tm)
    ls = jnp.asarray(idx[:, None] > idx[None, :], BF16)
    sel = jnp.asarray(np.arange(8)[:, None] == np.arange(LANES)[None, :], BF16)
    consts = [mod, g, wrh, wrl, br, ls, sel]
    return pl.pallas_call(
        functools.partial(_post_kernel, groups=tuple(groups)),
        grid=(t // tm,),
        in_specs=specs + [full(w) for w in w_list] + [full(a) for a in consts],
        out_specs=[pl.BlockSpec((tm, D_MODEL), lambda i: (i, 0)),
                   pl.BlockSpec((tm * SLABS, LANES), lambda i: (i, 0)),
                   pl.BlockSpec((tm, LANES), lambda i: (i, 0)),
                   pl.BlockSpec((8, tm), lambda i: (0, i)),
                   pl.BlockSpec((1, LANES), lambda i: (0, 0))],
        out_shape=[jax.ShapeDtypeStruct((t, D_MODEL), F32),
                   jax.ShapeDtypeStruct((t * SLABS, LANES), F32),
                   jax.ShapeDtypeStruct((t, LANES), F32),
                   jax.ShapeDtypeStruct((8, t), F32),
                   jax.ShapeDtypeStruct((1, LANES), F32)],
        scratch_shapes=[pltpu.VMEM((1, LANES), F32)],
        compiler_params=_cparams(("arbitrary",)),
        name="post_mixer_router",
    )(*args, *w_list, *consts)


def _expert_kernel(be_ref, st_ref, xn_hbm, w1_ref, w3_ref, w2_ref, o_ref, xa, xb, sem, w1b, w3b, w2b):
    i = pl.program_id(0)
    nblk = pl.num_programs(0)
    rows = EXPERT_ROWS

    def gather(blk, buf, s):
        for r in range(rows):
            tok = st_ref[blk * rows + r]
            pltpu.make_async_copy(xn_hbm.at[pl.ds(pl.multiple_of(tok * SLABS, SLABS), SLABS), :],
                                  buf.at[pl.ds(r * SLABS, SLABS), :], sem.at[s]).start()

    def wait(buf, s):
        pltpu.make_async_copy(xn_hbm.at[pl.ds(0, rows * SLABS), :], buf, sem.at[s]).wait()

    @pl.when(i == 0)
    def _():
        gather(0, xa, 0)

    changed = jnp.logical_or(i == 0, be_ref[i] != be_ref[jnp.maximum(i - 1, 0)])

    @pl.when(changed)
    def _():
        w1b[...] = w1_ref[0, 0].astype(BF16)
        w3b[...] = w3_ref[0, 0].astype(BF16)
        w2b[...] = w2_ref[0, 0].astype(BF16)

    nxt = jnp.minimum(i + 1, nblk - 1)

    def step(cur, s_cur, oth, s_oth):
        wait(cur, s_cur)
        gather(nxt, oth, s_oth)
        x = _from_token_tiles(cur, 0, rows).astype(BF16)
        h1 = _dot(x, w1b[...])
        h3 = _dot(x, w3b[...])
        hid = (h1 * jax.nn.sigmoid(h1)) * h3
        _to_token_tiles(o_ref, _dot(hid.astype(BF16), w2b[...]))

    @pl.when(i % 2 == 0)
    def _():
        step(xa, 0, xb, 1)

    @pl.when(i % 2 == 1)
    def _():
        step(xb, 1, xa, 0)

    @pl.when(i == nblk - 1)
    def _():
        @pl.when(i % 2 == 0)
        def _():
            wait(xb, 1)

        @pl.when(i % 2 == 1)
        def _():
            wait(xa, 0)


def _experts(block_expert, slot_tok, xn, w1, w3, w2, layer):
    nblk = block_expert.shape[0]
    rows = EXPERT_ROWS
    grid_spec = pltpu.PrefetchScalarGridSpec(
        num_scalar_prefetch=2,
        grid=(nblk,),
        in_specs=[pl.BlockSpec(memory_space=pl.ANY),
                  pl.BlockSpec((1, 1, D_MODEL, D_EXPERT), lambda i, be, st: (layer, be[i], 0, 0)),
                  pl.BlockSpec((1, 1, D_MODEL, D_EXPERT), lambda i, be, st: (layer, be[i], 0, 0)),
                  pl.BlockSpec((1, 1, D_EXPERT, D_MODEL), lambda i, be, st: (layer, be[i], 0, 0))],
        out_specs=pl.BlockSpec((rows * SLABS, LANES), lambda i, be, st: (i, 0)),
        scratch_shapes=[pltpu.VMEM((rows * SLABS, LANES), F32),
                        pltpu.VMEM((rows * SLABS, LANES), F32),
                        pltpu.SemaphoreType.DMA((2,)),
                        pltpu.VMEM((D_MODEL, D_EXPERT), BF16),
                        pltpu.VMEM((D_MODEL, D_EXPERT), BF16),
                        pltpu.VMEM((D_EXPERT, D_MODEL), BF16)])
    return pl.pallas_call(
        _expert_kernel,
        grid_spec=grid_spec,
        out_shape=jax.ShapeDtypeStruct((nblk * rows * SLABS, LANES), F32),
        compiler_params=_cparams(("arbitrary",)),
        name="moe_experts",
    )(block_expert, slot_tok, xn, w1, w3, w2)


_COMBINE_BLOCK = 128


def _combine_kernel(dest_ref, yb_hbm, x_ref, rt_ref, mod_ref, gfin_ref, o_ref, ybuf, sem, *, final_norm, row0):
    i = pl.program_id(0)
    nblk = pl.num_programs(0)
    slot = i % 2
    tm = _COMBINE_BLOCK

    def issue(blk, s):
        def body(j, carry):
            for c in range(2):
                d = dest_ref[c * T_ALL + row0 + blk * tm + j]
                pltpu.make_async_copy(yb_hbm.at[pl.ds(pl.multiple_of(d * SLABS, SLABS), SLABS), :],
                                      ybuf.at[s, pl.ds(pl.multiple_of((c * tm + j) * SLABS, SLABS), SLABS), :],
                                      sem.at[s]).start()
            return carry
        lax.fori_loop(0, tm, body, 0, unroll=4)

    @pl.when(i == 0)
    def _():
        issue(0, 0)

    @pl.when(i + 1 < nblk)
    def _():
        issue(i + 1, 1 - slot)

    pltpu.make_async_copy(yb_hbm.at[pl.ds(0, 2 * tm * SLABS), :], ybuf.at[slot], sem.at[slot]).wait()
    r = _mod_row(row0 + i * tm)
    rt = rt_ref[...]
    yv = ybuf.at[slot]
    y = rt[:, 2:3] * _from_token_tiles(yv, 0, tm) + rt[:, 3:4] * _from_token_tiles(yv, tm, tm)
    out = x_ref[...] + _mod_part(mod_ref, r, 5) * y
    if final_norm:
        out = _rms(out, gfin_ref[...])
    o_ref[...] = out


def _combine(dest, yb, x, rt, mod, gfin, final_norm, row0=0, t=T_ALL):
    tm = _COMBINE_BLOCK
    blk0 = row0 // tm
    grid_spec = pltpu.PrefetchScalarGridSpec(
        num_scalar_prefetch=1,
        grid=(t // tm,),
        in_specs=[pl.BlockSpec(memory_space=pl.ANY),
                  pl.BlockSpec((tm, D_MODEL), lambda i, d: (blk0 + i, 0)),
                  pl.BlockSpec((tm, LANES), lambda i, d: (blk0 + i, 0)),
                  pl.BlockSpec(mod.shape, lambda i, d: (0, 0)),
                  pl.BlockSpec(gfin.shape, lambda i, d: (0, 0))],
        out_specs=pl.BlockSpec((tm, D_MODEL), lambda i, d: (i, 0)),
        scratch_shapes=[pltpu.VMEM((2, 2 * tm * SLABS, LANES), F32),
                        pltpu.SemaphoreType.DMA((2,))])
    return pl.pallas_call(
        functools.partial(_combine_kernel, final_norm=final_norm, row0=row0),
        grid_spec=grid_spec,
        out_shape=jax.ShapeDtypeStruct((t, D_MODEL), F32),
        compiler_params=_cparams(("arbitrary",)),
        name="moe_combine",
    )(dest, yb, x, rt, mod, gfin)


def _moe_plan(rtt, cnt):
    t = rtt.shape[1]
    eid = rtt[0:2].astype(jnp.int32)
    rank = rtt[4:6].astype(jnp.int32)
    counts = cnt[0, N_GROUPS:N_GROUPS + N_EXPERTS].astype(jnp.int32)
    padded = (counts + EXPERT_ROWS - 1) // EXPERT_ROWS * EXPERT_ROWS
    seg_end = jnp.cumsum(padded)
    seg_start = seg_end - padded
    experts = jnp.arange(N_EXPERTS, dtype=jnp.int32)
    start = jnp.sum(jnp.where(eid[..., None] == experts, seg_start, 0), axis=-1)
    dest = (start + rank).reshape(-1)
    n_blocks = (2 * t + N_EXPERTS * (EXPERT_ROWS - 1) + EXPERT_ROWS - 1) // EXPERT_ROWS
    tok = jnp.tile(jnp.arange(t, dtype=jnp.int32), 2)
    slot_tok = (jnp.arange(n_blocks * EXPERT_ROWS, dtype=jnp.int32) % t).at[dest].set(tok)
    first_row = jnp.arange(n_blocks, dtype=jnp.int32) * EXPERT_ROWS
    block_expert = jnp.minimum(jnp.sum((seg_end[None, :] <= first_row[:, None]).astype(jnp.int32), axis=1),
                               N_EXPERTS - 1)
    return dest, slot_tok, block_expert


def _attn_c_prompt_kernel(q_ref, k_ref, v_ref, o_ref):
    for p in range(C_HEADS // 2):
        sl = slice(p * LANES, (p + 1) * LANES)
        kb = k_ref[:, sl].astype(BF16)
        vb = v_ref[:, sl].astype(BF16)
        qp = q_ref[:, sl] * (C_HEAD_DIM ** -0.5)
        o_ref[:, sl] = _softmax_pair(qp, lambda qm: [_dot_nt(qm, kb)], lambda ps: _dot(ps[0], vb))


def _attn_c_prompt(q, k, v):
    blk = pl.BlockSpec((SEQ, C_WIDTH), lambda b: (b, 0))
    return pl.pallas_call(
        _attn_c_prompt_kernel,
        grid=(BATCH,),
        in_specs=[blk, blk, blk],
        out_specs=blk,
        out_shape=jax.ShapeDtypeStruct((T_PROMPT, C_WIDTH), F32),
        compiler_params=_cparams(("arbitrary",)),
        name="attn_c_prompt",
    )(q, k, v)


def _na_key_start(r0):
    rows = DEC_SEQ // GRID_W
    return jnp.minimum(jnp.clip(r0 - NA_ROWS // 2, 0, rows - NA_ROWS), rows - NA_KROWS)


def _na_block_plan():
    rows = DEC_SEQ // GRID_W
    nblk = rows // NA_QROWS
    plan = []
    for blk in (0, 1, nblk - 1):
        r0 = blk * NA_QROWS
        ks = min(int(np.clip(r0 - NA_ROWS // 2, 0, rows - NA_ROWS)), rows - NA_KROWS)
        per_row = []
        for i in range(NA_QROWS):
            r = r0 + i
            rs = int(np.clip(r - NA_ROWS // 2, 0, rows - NA_ROWS))
            start = ks - r + NA_ROWS - 1 + NA_KROWS
            ok = [rs <= ks + j < rs + NA_ROWS for j in range(NA_KROWS)]
            per_row.append((start, ok))
        plan.append(per_row)
    return plan


def _attn_c_sample_kernel(q_ref, k_ref, v_ref, ck_ref, cv_ref, toe_ref, o_ref, bias_ref):
    rows = DEC_SEQ // GRID_W
    nblk = rows // NA_QROWS
    w = GRID_W

    @pl.when(jnp.logical_and(pl.program_id(1) == 0, pl.program_id(2) == 0))
    def _():
        neg = jnp.full((w, w), NEG, F32)
        for t, per_row in enumerate(_na_block_plan()):
            for half in range(2):
                for i, (start, ok) in enumerate(per_row):
                    for j in range(0, NA_KROWS, 2):
                        pieces = [toe_ref[0, half, start + jj] if ok[jj] else neg for jj in (j, j + 1)]
                        bias_ref[t, half, i * w:(i + 1) * w, j * w:(j + 2) * w] = jnp.concatenate(pieces, axis=1)

    i = pl.program_id(2)
    r0 = i * NA_QROWS
    k0 = pl.multiple_of(_na_key_start(r0) * GRID_W, GRID_W)
    btype = jnp.where(i == 0, 0, jnp.where(i == nblk - 1, 2, 1))
    nk = NA_KROWS * GRID_W
    kw = k_ref[pl.ds(k0, nk), :].astype(BF16)
    vw = v_ref[pl.ds(k0, nk), :].astype(BF16)
    kc = ck_ref[0].astype(BF16)
    vc = cv_ref[0].astype(BF16)
    qp = q_ref[...] * (C_HEAD_DIM ** -0.5)
    lo = _lane(qp.shape) < 64
    outs = []
    for half in range(2):
        qm = jnp.where(lo if half == 0 else jnp.logical_not(lo), qp, 0.0).astype(BF16)
        s_win = _dot_nt(qm, kw) + bias_ref[btype, half]
        s_ctx = _dot_nt(qm, kc)
        m = jnp.maximum(s_win.max(axis=-1, keepdims=True), s_ctx.max(axis=-1, keepdims=True))
        e_win = jnp.exp(s_win - m)
        e_ctx = jnp.exp(s_ctx - m)
        l = e_win.sum(axis=-1, keepdims=True) + e_ctx.sum(axis=-1, keepdims=True)
        o = _dot(e_win.astype(BF16), vw) + _dot(e_ctx.astype(BF16), vc)
        outs.append(o / l)
    o_ref[...] = jnp.where(lo, outs[0], outs[1])


def _attn_c_sample(q, k, v, ck, cv, toe):
    rows = DEC_SEQ // GRID_W
    nblk = rows // NA_QROWS
    qrows = NA_QROWS * GRID_W
    npair = C_HEADS // 2
    return pl.pallas_call(
        _attn_c_sample_kernel,
        grid=(npair, DEC_BATCH, nblk),
        in_specs=[pl.BlockSpec((qrows, LANES), lambda p, b, i: (b * nblk + i, p)),
                  pl.BlockSpec((DEC_SEQ, LANES), lambda p, b, i: (b, p)),
                  pl.BlockSpec((DEC_SEQ, LANES), lambda p, b, i: (b, p)),
                  pl.BlockSpec((1, PAST_LEN, LANES), lambda p, b, i: (b, 0, p)),
                  pl.BlockSpec((1, PAST_LEN, LANES), lambda p, b, i: (b, 0, p)),
                  pl.BlockSpec((1,) + toe.shape[1:], lambda p, b, i: (p, 0, 0, 0, 0))],
        out_specs=pl.BlockSpec((qrows, LANES), lambda p, b, i: (b * nblk + i, p)),
        out_shape=jax.ShapeDtypeStruct((T_SAMPLE, C_WIDTH), F32),
        scratch_shapes=[pltpu.VMEM((3, 2, qrows, NA_KROWS * GRID_W), F32)],
        compiler_params=_cparams(("arbitrary", "arbitrary", "arbitrary")),
        name="attn_c_sample",
    )(q, k, v, ck, cv, toe)


def _na_toeplitz(rpb):
    w = GRID_W
    nd_r, nd_c = 2 * NA_ROWS - 1, 2 * NA_COLS - 1
    c = np.arange(w)
    cs = np.clip(c - NA_COLS // 2, 0, w - NA_COLS)
    col_ok = (c[None, :] >= cs[:, None]) & (c[None, :] < cs[:, None] + NA_COLS)
    dcol = c[None, :] - c[:, None] + NA_COLS - 1
    onehot = (np.arange(nd_c)[:, None, None] == dcol[None]).reshape(nd_c, w * w)
    toe = jnp.dot(rpb.reshape(C_HEADS * nd_r, nd_c), jnp.asarray(onehot, F32), precision=lax.Precision.HIGHEST)
    toe = jnp.where(col_ok[None, None], toe.reshape(C_HEADS, nd_r, w, w), NEG)
    toe = jnp.pad(toe, ((0, 0), (NA_KROWS, NA_KROWS), (0, 0), (0, 0)), constant_values=NEG)
    return toe.reshape(C_HEADS // 2, 2, nd_r + 2 * NA_KROWS, w, w)


def _rope_tables():
    half = A_HEAD_DIM // 2
    t = jnp.arange(DEC_SEQ)
    row = (t // GRID_W).astype(F32)
    colp = (t % GRID_W).astype(F32)
    freqs = 1.0 / (ROPE_BASE ** (jnp.arange(0, half, 2, dtype=F32) / half))
    d = np.arange(LANES) % A_HEAD_DIM
    pos = jnp.where(jnp.asarray(d < half)[None, :], row[:, None], colp[:, None])
    ang = pos * freqs[d % (half // 2)][None, :]
    sign = jnp.asarray(np.where((d % half) < half // 2, -1.0, 1.0), F32)[None, :]
    return jnp.cos(ang), jnp.sin(ang) * sign


def _head_avg_matrix():
    idx = np.arange(LANES) // A_HEAD_DIM
    return jnp.asarray((idx[:, None] == idx[None, :]).astype(np.float32) / A_HEAD_DIM, BF16)


def _tri_matrices():
    i = np.arange(MLSTM_CHUNK)
    upper = (i[:, None] <= i[None, :]).astype(np.float32)
    lower = (i[:, None] >= i[None, :]).astype(np.float32)
    return jnp.asarray(np.stack([upper, lower]), BF16)


def _router_weights(wg, bg, we, be):
    w = jnp.zeros((D_MODEL, LANES), F32).at[:, :N_GROUPS].set(wg).at[:, N_GROUPS:N_GROUPS + N_EXPERTS].set(we)
    b = jnp.zeros((1, LANES), F32).at[0, :N_GROUPS].set(bg).at[0, N_GROUPS:N_GROUPS + N_EXPERTS].set(be)
    hi = w.astype(BF16)
    lo = (w - hi.astype(F32)).astype(BF16)
    return hi, lo, b


def _moe(xn, rtt, cnt, w1, w3, w2, layer):
    dest, slot_tok, block_expert = _moe_plan(rtt, cnt)
    return dest, _experts(block_expert, slot_tok, xn, w1, w3, w2, layer)


def kernel(x_prompt, x_sample, cache_attn_k, cache_attn_v, state_mlstm_C, state_mlstm_n, state_mlstm_m,
           cache_na_k, cache_na_v, c, c_ctx, norm_mix, norm_ffn, norm_final, ada_w, ada_b,
           ab_w_in, ab_w_out, ab_q_norm, ab_k_norm, ab_gate_bias, ab_out_norm,
           na_w_in, na_w_out, na_rpb, moe_wg, moe_bg, moe_we, moe_be, moe_w1, moe_w3, moe_w2):
    xp = x_prompt.reshape(T_PROMPT, D_MODEL)
    xs = x_sample.reshape(T_SAMPLE, D_MODEL)
    cond =jnp.zeros((N_COND, D_MODEL), F32).at[0].set(c_ctx).at[1:1 + DEC_BATCH].set(c)
    mod = _modulation(cond, ada_w, ada_b)
    gfin = norm_final.reshape(1, D_MODEL)

    w_in = ab_w_in[0]
    o_aq, o_ak, o_av, o_bq, o_bk, o_bv, o_bo, o_bg = np.cumsum((0,) + (A_WIDTH, A_KV_WIDTH, A_KV_WIDTH,
                                                                       B_WIDTH, B_WIDTH, B_WIDTH, B_WIDTH))
    wb = w_in[:, o_bq:o_bg].astype(BF16)
    wq = w_in[:, o_aq:o_ak].astype(BF16)
    wkv = w_in[:, o_ak:o_bq].astype(BF16)
    wg = w_in[:, o_bg:o_bg + 4 * B_HEADS].reshape(D_MODEL, 4, B_HEADS)
    wgt = jnp.zeros((B_HEADS, 8, D_MODEL), F32).at[:, :4, :].set(wg.transpose(2, 1, 0))
    wgt = wgt.reshape(8 * B_HEADS, D_MODEL).astype(BF16)
    g_mix = norm_mix[0].reshape(1, D_MODEL)
    ob_p, oq_p, okv_p, ogt_p = _proj_ab(xp, mod[0], g_mix, wb, wq, wkv, wgt, 0)
    ob_s, oq_s, okv_s, ogt_s = _proj_ab(xs, mod[0], g_mix, wb, wq, wkv, wgt, T_PROMPT)

    qn = jnp.tile(ab_q_norm[0], 2).reshape(1, LANES)
    kn = jnp.tile(ab_k_norm[0], 2).reshape(1, LANES)
    bd = _head_avg_matrix()
    cos, sin = _rope_tables()
    a_p, new_k, new_v = _attn_a_prompt(oq_p, okv_p, qn, kn, bd)
    ck = cache_attn_k[:, 0].reshape(DEC_BATCH, PAST_LEN, A_KV_WIDTH)
    cv = cache_attn_v[:, 0].reshape(DEC_BATCH, PAST_LEN, A_KV_WIDTH)
    a_s = _attn_a_sample(oq_s, okv_s, ck, cv, cos, sin, qn, kn, bd)

    gb = ab_gate_bias[0]
    brow = jnp.zeros((B_HEADS, 8, LANES), F32).at[:, :4, :].set(
        jnp.broadcast_to(gb.T[:, :, None], (B_HEADS, 4, LANES)))
    onorm = ab_out_norm[0].reshape(B_HEADS, 1, B_HEAD_DIM)
    tri = _tri_matrices()
    b_p, cT, nT, mT = _mlstm(ob_p, ogt_p, brow, onorm, tri, n=SEQ, nseq=BATCH, emit_state=True)
    c0 = state_mlstm_C[:, 0]
    n0 = state_mlstm_n[:, 0].transpose(0, 2, 1, 3)
    m0 = jnp.broadcast_to(state_mlstm_m[:, 0].transpose(0, 2, 1)[..., None], (DEC_BATCH, B_HEADS, 2, LANES))
    (b_s,) = _mlstm(ob_s, ogt_s, brow, onorm, tri, n=DEC_SEQ, nseq=DEC_BATCH, init=(c0, n0, m0))

    w_out = ab_w_out[0].astype(BF16)
    wrh, wrl, br = _router_weights(moe_wg[0], moe_bg[0], moe_we[0], moe_be[0])
    x1, xn, rt, rtt, cnt = _post([(xp, xs), (a_p, a_s), (b_p, b_s)], [w_out[:A_WIDTH], w_out[A_WIDTH:]], mod[0],
                                 norm_ffn[0].reshape(1, D_MODEL), wrh, wrl, br)
    dest, yb = _moe(xn, rtt, cnt, moe_w1, moe_w3, moe_w2, 0)
    x = _combine(dest, yb, x1, rt, mod[0], gfin, False)

    g_mix = norm_mix[1].reshape(1, D_MODEL)
    w_in = na_w_in[0].astype(BF16)
    q_p, k_p, v_p = _proj_na(x, mod[1], g_mix, w_in, 0, T_PROMPT)
    q_s, k_s, v_s = _proj_na(x, mod[1], g_mix, w_in, T_PROMPT, T_SAMPLE)
    o_p = _attn_c_prompt(q_p, k_p, v_p)
    nck = cache_na_k[:, 0].reshape(DEC_BATCH, PAST_LEN, C_WIDTH)
    ncv = cache_na_v[:, 0].reshape(DEC_BATCH, PAST_LEN, C_WIDTH)
    o_s = _attn_c_sample(q_s, k_s, v_s, nck, ncv, _na_toeplitz(na_rpb[0]))
    wrh, wrl, br = _router_weights(moe_wg[1], moe_bg[1], moe_we[1], moe_be[1])
    x1, xn, rt, rtt, cnt = _post([x, (o_p, o_s)], [na_w_out[0].astype(BF16)], mod[1],
                                 norm_ffn[1].reshape(1, D_MODEL), wrh, wrl, br)
    dest, yb = _moe(xn, rtt, cnt, moe_w1, moe_w3, moe_w2, 1)
    y_prompt = _combine(dest, yb, x1, rt, mod[1], gfin, True, 0, T_PROMPT).reshape(BATCH, SEQ, D_MODEL)
    y_sample = _combine(dest, yb, x1, rt, mod[1], gfin, True, T_PROMPT, T_SAMPLE).reshape(DEC_BATCH, DEC_SEQ, D_MODEL)
    new_attn_k = new_k.reshape(BATCH, 1, SEQ, A_KV_HEADS, A_HEAD_DIM)
    new_attn_v = new_v.reshape(BATCH, 1, SEQ, A_KV_HEADS, A_HEAD_DIM)
    new_mlstm_C = cT[:, None]
    new_mlstm_n = nT.transpose(0, 2, 1, 3)[:, None]
    new_mlstm_m = mT[..., 0].transpose(0, 2, 1)[:, None]
    new_na_k = k_p.reshape(BATCH, 1, SEQ, C_HEADS, C_HEAD_DIM)
    new_na_v = v_p.reshape(BATCH, 1, SEQ, C_HEADS, C_HEAD_DIM)
    return (y_prompt, y_sample, new_attn_k, new_attn_v, new_mlstm_C, new_mlstm_n, new_mlstm_m,
            new_na_k, new_na_v)
```

```python
import functools

import numpy as np
import jax
import jax.numpy as jnp
from jax import lax
from jax.experimental import pallas as pl
from jax.experimental.pallas import tpu as pltpu

F32 = jnp.float32
BF16 = jnp.bfloat16

D_MODEL = 1024
BATCH = 32
SEQ = 256
DEC_BATCH = 4
DEC_SEQ = 2048
PAST_LEN = 256
GRID_W = 64
A_HEADS = 8
A_KV_HEADS = 2
A_HEAD_DIM = 64
A_WIDTH = A_HEADS * A_HEAD_DIM
A_KV_WIDTH = A_KV_HEADS * A_HEAD_DIM
B_HEADS = 4
B_HEAD_DIM = 128
B_WIDTH = B_HEADS * B_HEAD_DIM
MLSTM_CHUNK = 128
C_HEADS = 16
C_HEAD_DIM = 64
C_WIDTH = C_HEADS * C_HEAD_DIM
NA_ROWS = 8
NA_COLS = 16
N_GROUPS = 4
EXPERTS_PER_GROUP = 8
N_EXPERTS = N_GROUPS * EXPERTS_PER_GROUP
D_EXPERT = 512
MOE_BLOCK = 128
ROPE_BASE = 10000.0
NORM_EPS = 1e-6

T_PROMPT = BATCH * SEQ
T_SAMPLE = DEC_BATCH * DEC_SEQ
T_ALL = T_PROMPT + T_SAMPLE
N_COND = 8
LANES = 128
SLABS = D_MODEL // LANES
TOK_BLOCK = 256
MLSTM_HEADS_PER_STEP = 2
EXPERT_ROWS = 256
NA_QROWS = 4
NA_KROWS = 12
NEG = -1e30
VMEM_LIMIT = 56 * 1024 * 1024


def _cparams(sem):
    return pltpu.CompilerParams(dimension_semantics=sem, vmem_limit_bytes=VMEM_LIMIT)


def _split2(x):
    hi = x.astype(BF16)
    lo = (x - hi.astype(F32)).astype(BF16)
    return hi, lo


def _split3(x):
    hi = x.astype(BF16)
    r = x - hi.astype(F32)
    mid = r.astype(BF16)
    lo = (r - mid.astype(F32)).astype(BF16)
    return hi, mid, lo


def _dot(a, b):
    return jnp.dot(a, b, preferred_element_type=F32)


def _dot_nt(a, b):
    return lax.dot_general(a, b, (((1,), (1,)), ((), ())), preferred_element_type=F32)


def _dot_exact_rhs(x, b):
    hi, mid, lo = _split3(x)
    return _dot(hi, b) + _dot(mid, b) + _dot(lo, b)


def _rms(x, g):
    ms = jnp.mean(x * x, axis=-1, keepdims=True)
    return (x * lax.rsqrt(ms + NORM_EPS)) * g


def _mod_row(tok0):
    return jnp.where(tok0 < T_PROMPT, 0, 1 + (tok0 - T_PROMPT) // DEC_SEQ)


def _mod_part(mod_ref, r, idx):
    return mod_ref[pl.ds(r, 1), idx * D_MODEL:(idx + 1) * D_MODEL]


def _head_rms(x, w, bd):
    hi, lo = _split2(x * x)
    ms = _dot(hi, bd) + _dot(lo, bd)
    return (x * lax.rsqrt(ms + NORM_EPS)) * w


def _to_token_tiles(ref, x):
    m = x.shape[0]
    for s in range(SLABS):
        ref[pl.ds(s, m, stride=SLABS), :] = x[:, s * LANES:(s + 1) * LANES]


def _from_token_tiles(ref, tile0, m):
    return jnp.concatenate([ref[pl.ds(tile0 * SLABS + s, m, stride=SLABS), :] for s in range(SLABS)], axis=1)


def _lane(shape):
    return lax.broadcasted_iota(jnp.int32, shape, len(shape) - 1)


def _dup_half(x, g):
    xr = pltpu.roll(x, 64, 1)
    lo = _lane(x.shape) < 64
    return jnp.where(lo, x, xr) if g == 0 else jnp.where(lo, xr, x)


def _rope(x, cos, sin_signed):
    lane = _lane(x.shape)
    partner = jnp.where((lane % 32) < 16, pltpu.roll(x, LANES - 16, 1), pltpu.roll(x, 16, 1))
    return x * cos + partner * sin_signed


def _softmax_pair(qp, score_fn, value_fn):
    lo = _lane(qp.shape) < 64
    outs = []
    for half in range(2):
        qm = jnp.where(lo if half == 0 else jnp.logical_not(lo), qp, 0.0).astype(BF16)
        ss = score_fn(qm)
        m = ss[0].max(axis=-1, keepdims=True)
        for s in ss[1:]:
            m = jnp.maximum(m, s.max(axis=-1, keepdims=True))
        es = [jnp.exp(s - m) for s in ss]
        l = es[0].sum(axis=-1, keepdims=True)
        for e in es[1:]:
            l = l + e.sum(axis=-1, keepdims=True)
        o = value_fn([e.astype(BF16) for e in es])
        outs.append(o / l)
    return jnp.where(lo, outs[0], outs[1])


def _mod_kernel(cond_ref, w_ref, b_ref, o_ref):
    c = cond_ref[...]
    s = c * jax.nn.sigmoid(c)
    s_hi, s_lo = _split2(s)
    w_hi, w_lo = _split2(w_ref[0])
    o_ref[0] = _dot(s_hi, w_hi) + _dot(s_lo, w_hi) + _dot(s_hi, w_lo) + b_ref[0]


def _modulation(cond, ada_w, ada_b):
    depth, d, n = ada_w.shape
    tn = 1536
    return pl.pallas_call(
        _mod_kernel,
        grid=(depth, n // tn),
        in_specs=[pl.BlockSpec((N_COND, d), lambda l, j: (0, 0)),
                  pl.BlockSpec((1, d, tn), lambda l, j: (l, 0, j)),
                  pl.BlockSpec((1, 1, tn), lambda l, j: (l, 0, j))],
        out_specs=pl.BlockSpec((1, N_COND, tn), lambda l, j: (l, 0, j)),
        out_shape=jax.ShapeDtypeStruct((depth, N_COND, n), F32),
        compiler_params=_cparams(("arbitrary", "arbitrary")),
        name="adaln_modulation",
    )(cond, ada_w, ada_b.reshape(depth, 1, n))


def _norm_mod(x_ref, mod_ref, g_ref, shift_idx, scale_idx, row0):
    r = _mod_row(row0 + pl.program_id(0) * x_ref.shape[0])
    h = _rms(x_ref[...], g_ref[...])
    return h * (1.0 + _mod_part(mod_ref, r, scale_idx)) + _mod_part(mod_ref, r, shift_idx)


def _proj_ab_kernel(x_ref, mod_ref, g_ref, wb_ref, wq_ref, wkv_ref, wgt_ref, ob_ref, oq_ref, okv_ref, ogt_ref,
                    *, row0):
    tm = x_ref.shape[0]
    hb = _norm_mod(x_ref, mod_ref, g_ref, 0, 1, row0).astype(BF16)
    ob_ref[...] = _dot(hb, wb_ref[...])
    oq_ref[...] = _dot(hb, wq_ref[...])
    okv_ref[...] = _dot(hb, wkv_ref[...])
    gt = _dot_nt(wgt_ref[...], hb)
    for j in range(tm // LANES):
        ogt_ref[j] = gt[:, j * LANES:(j + 1) * LANES]


def _proj_ab(x, mod, g, wb, wq, wkv, wgt, row0):
    t = x.shape[0]
    tm = TOK_BLOCK
    full = lambda a: pl.BlockSpec(a.shape, lambda i: (0,) * a.ndim)
    return pl.pallas_call(
        functools.partial(_proj_ab_kernel, row0=row0),
        grid=(t // tm,),
        in_specs=[pl.BlockSpec((tm, D_MODEL), lambda i: (i, 0)), full(mod), full(g),
                  full(wb), full(wq), full(wkv), full(wgt)],
        out_specs=[pl.BlockSpec((tm, 4 * B_WIDTH), lambda i: (i, 0)),
                   pl.BlockSpec((tm, A_WIDTH), lambda i: (i, 0)),
                   pl.BlockSpec((tm, 2 * A_KV_WIDTH), lambda i: (i, 0)),
                   pl.BlockSpec((tm // LANES, 8 * B_HEADS, LANES), lambda i: (i, 0, 0))],
        out_shape=[jax.ShapeDtypeStruct((t, 4 * B_WIDTH), F32),
                   jax.ShapeDtypeStruct((t, A_WIDTH), F32),
                   jax.ShapeDtypeStruct((t, 2 * A_KV_WIDTH), F32),
                   jax.ShapeDtypeStruct((t // LANES, 8 * B_HEADS, LANES), F32)],
        compiler_params=_cparams(("arbitrary",)),
        name="proj_ab",
    )(x, mod, g, wb, wq, wkv, wgt)


def _gqa_block(q_ref, qn, bd, kd_ref, vd_ref, o_ref, rope=None):
    for p in range(A_HEADS // 2):
        g = p // (A_HEADS // 2 // A_KV_HEADS)
        qp = _head_rms(q_ref[:, p * LANES:(p + 1) * LANES], qn, bd)
        if rope is not None:
            qp = _rope(qp, rope[0], rope[1])
        qp = qp * (A_HEAD_DIM ** -0.5)
        o_ref[:, p * LANES:(p + 1) * LANES] = _softmax_pair(
            qp, lambda qm: [_dot_nt(qm, kd_ref[g])], lambda ps: _dot(ps[0], vd_ref[g]))


def _attn_a_prompt_kernel(q_ref, kv_ref, qn_ref, kn_ref, bd_ref, o_ref, knew_ref, vnew_ref, kd_ref, vd_ref):
    bd = bd_ref[...]
    k = _head_rms(kv_ref[:, :LANES], kn_ref[...], bd)
    v = kv_ref[:, LANES:]
    knew_ref[...] = k
    vnew_ref[...] = v
    for g in range(A_KV_HEADS):
        kd_ref[g] = _dup_half(k, g).astype(BF16)
        vd_ref[g] = _dup_half(v, g).astype(BF16)
    _gqa_block(q_ref, qn_ref[...], bd, kd_ref, vd_ref, o_ref)


def _attn_a_prompt(q, kv, qn, kn, bd):
    nb = BATCH
    full = lambda a: pl.BlockSpec(a.shape, lambda b: (0,) * a.ndim)
    return pl.pallas_call(
        _attn_a_prompt_kernel,
        grid=(nb,),
        in_specs=[pl.BlockSpec((SEQ, A_WIDTH), lambda b: (b, 0)),
                  pl.BlockSpec((SEQ, 2 * A_KV_WIDTH), lambda b: (b, 0)),
                  full(qn), full(kn), full(bd)],
        out_specs=[pl.BlockSpec((SEQ, A_WIDTH), lambda b: (b, 0)),
                   pl.BlockSpec((SEQ, A_KV_WIDTH), lambda b: (b, 0)),
                   pl.BlockSpec((SEQ, A_KV_WIDTH), lambda b: (b, 0))],
        out_shape=[jax.ShapeDtypeStruct((T_PROMPT, A_WIDTH), F32),
                   jax.ShapeDtypeStruct((T_PROMPT, A_KV_WIDTH), F32),
                   jax.ShapeDtypeStruct((T_PROMPT, A_KV_WIDTH), F32)],
        scratch_shapes=[pltpu.VMEM((A_KV_HEADS, SEQ, LANES), BF16),
                        pltpu.VMEM((A_KV_HEADS, SEQ, LANES), BF16)],
        compiler_params=_cparams(("arbitrary",)),
        name="attn_a_prompt",
    )(q, kv, qn, kn, bd)


_A_QBLOCK = 256


def _attn_a_sample_kernel(q_ref, kv_ref, ck_ref, cv_ref, cos_ref, sin_ref, cosq_ref, sinq_ref,
                          qn_ref, kn_ref, bd_ref, o_ref, kd_ref, vd_ref):
    bd = bd_ref[...]

    @pl.when(pl.program_id(1) == 0)
    def _():
        for g in range(A_KV_HEADS):
            kd_ref[g, :PAST_LEN] = _dup_half(ck_ref[0], g).astype(BF16)
            vd_ref[g, :PAST_LEN] = _dup_half(cv_ref[0], g).astype(BF16)
        rows = 256
        for c in range(DEC_SEQ // rows):
            sl = slice(c * rows, (c + 1) * rows)
            k = _head_rms(kv_ref[sl, :LANES], kn_ref[...], bd)
            k = _rope(k, cos_ref[sl, :], sin_ref[sl, :])
            v = kv_ref[sl, LANES:]
            dst = slice(PAST_LEN + c * rows, PAST_LEN + (c + 1) * rows)
            for g in range(A_KV_HEADS):
                kd_ref[g, dst] = _dup_half(k, g).astype(BF16)
                vd_ref[g, dst] = _dup_half(v, g).astype(BF16)

    _gqa_block(q_ref, qn_ref[...], bd, kd_ref, vd_ref, o_ref, rope=(cosq_ref[...], sinq_ref[...]))


def _attn_a_sample(q, kv, ck, cv, cos, sin, qn, kn, bd):
    nq = DEC_SEQ // _A_QBLOCK
    full = lambda a: pl.BlockSpec(a.shape, lambda b, i: (0,) * a.ndim)
    tk = PAST_LEN + DEC_SEQ
    return pl.pallas_call(
        _attn_a_sample_kernel,
        grid=(DEC_BATCH, nq),
        in_specs=[pl.BlockSpec((_A_QBLOCK, A_WIDTH), lambda b, i: (b * nq + i, 0)),
                  pl.BlockSpec((DEC_SEQ, 2 * A_KV_WIDTH), lambda b, i: (b, 0)),
                  pl.BlockSpec((1, PAST_LEN, A_KV_WIDTH), lambda b, i: (b, 0, 0)),
                  pl.BlockSpec((1, PAST_LEN, A_KV_WIDTH), lambda b, i: (b, 0, 0)),
                  full(cos), full(sin),
                  pl.BlockSpec((_A_QBLOCK, LANES), lambda b, i: (i, 0)),
                  pl.BlockSpec((_A_QBLOCK, LANES), lambda b, i: (i, 0)),
                  full(qn), full(kn), full(bd)],
        out_specs=pl.BlockSpec((_A_QBLOCK, A_WIDTH), lambda b, i: (b * nq + i, 0)),
        out_shape=jax.ShapeDtypeStruct((T_SAMPLE, A_WIDTH), F32),
        scratch_shapes=[pltpu.VMEM((A_KV_HEADS, tk, LANES), BF16),
                        pltpu.VMEM((A_KV_HEADS, tk, LANES), BF16)],
        compiler_params=_cparams(("arbitrary", "arbitrary")),
        name="attn_a_sample",
    )(q, kv, ck, cv, cos, sin, cos, sin, qn, kn, bd)


def _log_sigmoid(x):
    return -(jnp.maximum(-x, 0.0) + jnp.log1p(jnp.exp(-jnp.abs(x))))


def _mlstm_kernel(*refs, n, has_init, emit_state):
    it = iter(refs)
    q_ref, k_ref, v_ref, og_ref, gt_ref, brow_ref, onorm_ref, tri_ref = [next(it) for _ in range(8)]
    if has_init:
        c0_ref, n0_ref, m0_ref = [next(it) for _ in range(3)]
    out_ref = next(it)
    if emit_state:
        cT_ref, nT_ref, mT_ref = [next(it) for _ in range(3)]
    h_ref, row_ref, col_ref, tbuf_ref, c_ref, n_ref = [next(it) for _ in range(6)]

    L = MLSTM_CHUNK
    nc = n // L

    hb = MLSTM_HEADS_PER_STEP

    for hh in range(hb):
        gt = gt_ref[:, 8 * hh:8 * hh + 8, :] + brow_ref[hh][None]
        lf = _log_sigmoid(gt).reshape(nc * 8, L)
        cum_f = _dot_exact_rhs(lf, tri_ref[0]).reshape(nc, 8, L)
        cum_b = _dot_exact_rhs(lf, tri_ref[1]).reshape(nc, 8, L)
        cf = cum_f[:, 1:2, :]
        cb = cum_b[:, 3:4, :]
        o = 8 * hh
        row_ref[:, o + 0:o + 1, :] = cf
        row_ref[:, o + 1:o + 2, :] = gt[:, 0:1, :] - cf
        row_ref[:, o + 2:o + 3, :] = cb
        row_ref[:, o + 3:o + 4, :] = gt[:, 2:3, :] - cb
        lf3 = lf.reshape(nc, 8, L)
        row_ref[:, o + 4:o + 5, :] = jnp.broadcast_to(lf3[:, 1:2, :].sum(axis=-1, keepdims=True), (nc, 1, L))
        row_ref[:, o + 5:o + 6, :] = jnp.broadcast_to(lf3[:, 3:4, :].sum(axis=-1, keepdims=True), (nc, 1, L))
        row_ref[:, o + 6:o + 8, :] = jnp.zeros((nc, 2, L), F32)

    tbuf_ref[...] = jnp.zeros((L, L), F32)

    def to_cols(c, carry):
        tbuf_ref[0:8 * hb, :] = row_ref[c]
        col_ref[pl.ds(pl.multiple_of(c * L, L), L), :] = tbuf_ref[...].T
        return carry

    lax.fori_loop(0, nc, to_cols, 0)

    chains = [(hh, d) for hh in range(hb) for d in range(2)]
    if has_init:
        m_init = []
        for hh, d in chains:
            c_ref[2 * hh + d] = c0_ref[0, d, hh]
            n_ref[8 * hh + d:8 * hh + d + 1, :] = n0_ref[0, hh, d:d + 1, :]
            m_init.append(m0_ref[0, hh, d:d + 1, 0:1])
        m_init = tuple(m_init)
    else:
        c_ref[...] = jnp.zeros(c_ref.shape, F32)
        n_ref[...] = jnp.zeros(n_ref.shape, F32)
        m_init = tuple(jnp.zeros((1, 1), F32) for _ in chains)

    t_idx = lax.broadcasted_iota(jnp.int32, (L, L), 0)
    s_idx = lax.broadcasted_iota(jnp.int32, (L, L), 1)
    masks = (s_idx <= t_idx, s_idx >= t_idx)

    def step(c, hh, d, m):
        r0 = pl.multiple_of(c * L, L)
        hl = slice(hh * L, (hh + 1) * L)
        q = q_ref[pl.ds(r0, L), hl]
        ks = k_ref[pl.ds(r0, L), hl] * (B_HEAD_DIM ** -0.5)
        vb = v_ref[pl.ds(r0, L), hl].astype(BF16)
        qb = q.astype(BF16)
        rows = row_ref[c]
        cols = col_ref[pl.ds(r0, L), :]
        o = 8 * hh + 2 * d
        a_row = rows[o + 1:o + 2, :]
        tot = rows[8 * hh + 4 + d:8 * hh + 5 + d, 0:1]
        cum_col = cols[:, o:o + 1]
        a_col = cols[:, o + 1:o + 2]
        dlog = jnp.where(masks[d], cum_col + a_row, -jnp.inf)
        inter = cum_col + m
        m_t = jnp.maximum(inter, dlog.max(axis=-1, keepdims=True))
        w_in = jnp.exp(dlog - m_t)
        w_st = jnp.exp(inter - m_t)
        a = _dot_nt(qb, ks.astype(BF16)) * w_in
        ci = 2 * hh + d
        ni = 8 * hh + d
        cmat = c_ref[ci]
        nvec = n_ref[ni:ni + 1, :]
        num = w_st * _dot(qb, cmat.astype(BF16)) + _dot(a.astype(BF16), vb)
        den = w_st * (q * nvec).sum(axis=-1, keepdims=True) + a.sum(axis=-1, keepdims=True)
        h_ref[ci, pl.ds(r0, L), :] = num / jnp.maximum(jnp.abs(den), jnp.exp(-m_t))
        m_new = jnp.maximum(tot + m, a_row.max(axis=-1, keepdims=True) + tot)
        ws = jnp.exp(a_col + tot - m_new)
        wc = jnp.exp(tot + m - m_new)
        kw = ks * ws
        c_ref[ci] = wc * cmat + _dot(kw.T.astype(BF16), vb)
        n_ref[ni:ni + 1, :] = wc * nvec + kw.sum(axis=0, keepdims=True)
        return m_new

    def body(i, carry):
        return tuple(step(i if d == 0 else nc - 1 - i, hh, d, m) for (hh, d), m in zip(chains, carry))

    m_fin = lax.fori_loop(0, nc, body, m_init)

    for hh in range(hb):
        hm = h_ref[2 * hh] + h_ref[2 * hh + 1]
        hl = slice(hh * L, (hh + 1) * L)
        out_ref[:, hl] = _rms(hm, onorm_ref[hh]) * jax.nn.sigmoid(og_ref[:, hl])

    if emit_state:
        for k, (hh, d) in enumerate(chains):
            cT_ref[0, d, hh] = c_ref[2 * hh + d]
            nT_ref[0, hh, d:d + 1, :] = n_ref[8 * hh + d:8 * hh + d + 1, :]
            mT_ref[0, hh, d:d + 1, :] = jnp.broadcast_to(m_fin[k], (1, L))


def _mlstm(ob, ogt, brow, onorm, tri, *, n, nseq, init=None, emit_state=False):
    L = MLSTM_CHUNK
    nc = n // L
    H = B_HEADS
    hb = MLSTM_HEADS_PER_STEP
    ng = H // hb
    col = lambda part: (lambda b, g: (b, part * ng + g))
    in_specs = [pl.BlockSpec((n, hb * L), col(0)), pl.BlockSpec((n, hb * L), col(1)),
                pl.BlockSpec((n, hb * L), col(2)), pl.BlockSpec((n, hb * L), col(3)),
                pl.BlockSpec((nc, 8 * hb, L), lambda b, g: (b, g, 0)),
                pl.BlockSpec((hb, 8, L), lambda b, g: (g, 0, 0)),
                pl.BlockSpec((hb, 1, L), lambda b, g: (g, 0, 0)),
                pl.BlockSpec(tri.shape, lambda b, g: (0, 0, 0))]
    args = [ob, ob, ob, ob, ogt, brow, onorm, tri]
    state_specs = [pl.BlockSpec((1, 2, hb, L, L), lambda b, g: (b, 0, g, 0, 0)),
                   pl.BlockSpec((1, hb, 2, L), lambda b, g: (b, g, 0, 0)),
                   pl.BlockSpec((1, hb, 2, L), lambda b, g: (b, g, 0, 0))]
    if init is not None:
        in_specs += state_specs
        args += list(init)
    out_specs = [pl.BlockSpec((n, hb * L), lambda b, g: (b, g))]
    out_shape = [jax.ShapeDtypeStruct((nseq * n, B_WIDTH), F32)]
    if emit_state:
        out_specs += state_specs
        out_shape += [jax.ShapeDtypeStruct((nseq, 2, H, L, L), F32),
                      jax.ShapeDtypeStruct((nseq, H, 2, L), F32),
                      jax.ShapeDtypeStruct((nseq, H, 2, L), F32)]
    return pl.pallas_call(
        functools.partial(_mlstm_kernel, n=n, has_init=init is not None, emit_state=emit_state),
        grid=(nseq, ng),
        in_specs=in_specs,
        out_specs=out_specs,
        out_shape=out_shape,
        scratch_shapes=[pltpu.VMEM((2 * hb, n, L), F32),
                        pltpu.VMEM((nc, 8 * hb, L), F32),
                        pltpu.VMEM((n, L), F32),
                        pltpu.VMEM((L, L), F32),
                        pltpu.VMEM((2 * hb, L, L), F32),
                        pltpu.VMEM((8 * hb, L), F32)],
        compiler_params=_cparams(("arbitrary", "arbitrary")),
        name="mlstm_init" if init is not None else "mlstm",
    )(*args)


def _router(logits):
    lane = _lane(logits.shape).astype(F32)
    big = 1e9
    gl = jnp.where(lane < N_GROUPS, logits, -jnp.inf)
    gmax = gl.max(axis=-1, keepdims=True)
    g_sel = jnp.where(gl == gmax, lane, big).min(axis=-1, keepdims=True)
    g_prob = 1.0 / jnp.exp(gl - gmax).sum(axis=-1, keepdims=True)
    lo = N_GROUPS + EXPERTS_PER_GROUP * g_sel
    el = jnp.where(lane >= lo, jnp.where(lane < lo + EXPERTS_PER_GROUP, logits, -jnp.inf), -jnp.inf)
    v1 = el.max(axis=-1, keepdims=True)
    i1 = jnp.where(el == v1, lane, big).min(axis=-1, keepdims=True)
    el2 = jnp.where(lane == i1, -jnp.inf, el)
    v2 = el2.max(axis=-1, keepdims=True)
    i2 = jnp.where(el2 == v2, lane, big).min(axis=-1, keepdims=True)
    e2 = jnp.exp(v2 - v1)
    w1 = g_prob / (1.0 + e2)
    w2 = g_prob * e2 / (1.0 + e2)
    return i1, i2, w1, w2


def _read_tokens(refs, is_prompt):
    if len(refs) == 1:
        return refs[0][...]
    return jnp.where(is_prompt, refs[0][...], refs[1][...])


def _post_kernel(*refs, groups):
    it = iter(refs)
    tok_refs = [[next(it) for _ in range(n)] for n in groups]
    w_refs = [next(it) for _ in range(len(groups) - 1)]
    mod_ref, g_ref, wr_ref, br_ref, ls_ref, sel_ref = [next(it) for _ in range(6)]
    xnew_ref, xn_ref, rt_ref, rtt_ref, cnt_ref = [next(it) for _ in range(5)]
    run_ref = next(it)
    i = pl.program_id(0)
    tm = xnew_ref.shape[0]
    is_prompt = i * tm < T_PROMPT
    r = _mod_row(i * tm)
    acc = None
    for a_refs, w_ref in zip(tok_refs[1:], w_refs):
        d = _dot(_read_tokens(a_refs, is_prompt).astype(BF16), w_ref[...])
        acc = d if acc is None else acc + d
    xnew = _read_tokens(tok_refs[0], is_prompt) + _mod_part(mod_ref, r, 2) * acc
    xnew_ref[...] = xnew
    xn = _rms(xnew, g_ref[...]) * (1.0 + _mod_part(mod_ref, r, 4)) + _mod_part(mod_ref, r, 3)
    _to_token_tiles(xn_ref, xn)
    x_hi, x_lo = _split2(xn)
    both = _dot(x_hi, wr_ref[...])
    logits = both[:, :LANES] + both[:, LANES:] + _dot(x_lo, wr_ref[:, :LANES]) + br_ref[...]
    i1, i2, w1, w2 = _router(logits)

    @pl.when(i == 0)
    def _():
        run_ref[...] = jnp.zeros(run_ref.shape, F32)

    lane = _lane(logits.shape).astype(F32)
    member = jnp.where(lane == i1, 1.0, jnp.where(lane == i2, 1.0, 0.0))
    before = _dot(ls_ref[...], member.astype(BF16)) + run_ref[...]
    rank1 = jnp.where(lane == i1, before, 0.0).sum(axis=-1, keepdims=True)
    rank2 = jnp.where(lane == i2, before, 0.0).sum(axis=-1, keepdims=True)
    run_ref[...] = run_ref[...] + member.sum(axis=0, keepdims=True)
    cnt_ref[...] = run_ref[...]
    cols = (i1 - N_GROUPS, i2 - N_GROUPS, w1, w2, rank1, rank2)
    rt = jnp.zeros(logits.shape, F32)
    for k, c in enumerate(cols):
        rt = jnp.where(lane == k, c, rt)
    rt_ref[...] = rt
    hi, mid, lo = _split3(rt)
    sel = sel_ref[...]
    rtt_ref[...] = _dot_nt(sel, hi) + _dot_nt(sel, mid) + _dot_nt(sel, lo)


def _post(tok_ops, w_list, mod, g, wr, br):
    t = T_ALL
    tm = TOK_BLOCK
    npb = T_PROMPT // tm
    full = lambda a: pl.BlockSpec(a.shape, lambda i: (0,) * a.ndim)
    specs, args, groups = [], [], []
    for op in tok_ops:
        if isinstance(op, tuple):
            w = op[0].shape[1]
            specs += [pl.BlockSpec((tm, w), lambda i: (jnp.minimum(i, npb - 1), 0)),
                      pl.BlockSpec((tm, w), lambda i: (jnp.maximum(i - npb, 0), 0))]
            args += list(op)
            groups.append(2)
        else:
            specs.append(pl.BlockSpec((tm, op.shape[1]), lambda i: (i, 0)))
            args.append(op)
            groups.append(1)
    idx = np.arange(tm)
    ls = jnp.asarray(idx[:, None] > idx[None, :], BF16)
    sel = jnp.asarray(np.arange(8)[:, None] == np.arange(LANES)[None, :], BF16)
    consts = [mod, g, wr, br, ls, sel]
    return pl.pallas_call(
        functools.partial(_post_kernel, groups=tuple(groups)),
        grid=(t // tm,),
        in_specs=specs + [full(w) for w in w_list] + [full(a) for a in consts],
        out_specs=[pl.BlockSpec((tm, D_MODEL), lambda i: (i, 0)),
                   pl.BlockSpec((tm * SLABS, LANES), lambda i: (i, 0)),
                   pl.BlockSpec((tm, LANES), lambda i: (i, 0)),
                   pl.BlockSpec((8, tm), lambda i: (0, i)),
                   pl.BlockSpec((1, LANES), lambda i: (0, 0))],
        out_shape=[jax.ShapeDtypeStruct((t, D_MODEL), F32),
                   jax.ShapeDtypeStruct((t * SLABS, LANES), F32),
                   jax.ShapeDtypeStruct((t, LANES), F32),
                   jax.ShapeDtypeStruct((8, t), F32),
                   jax.ShapeDtypeStruct((1, LANES), F32)],
        scratch_shapes=[pltpu.VMEM((1, LANES), F32)],
        compiler_params=_cparams(("arbitrary",)),
        name="post_mixer_router",
    )(*args, *w_list, *consts)


def _expert_kernel(be_ref, st_ref, xn_hbm, w1_ref, w3_ref, w2_ref, o_ref, xa, xb, sem, w1b, w3b, w2b):
    i = pl.program_id(0)
    nblk = pl.num_programs(0)
    rows = EXPERT_ROWS

    def gather(blk, buf, s):
        for r in range(rows):
            tok = st_ref[blk * rows + r]
            pltpu.make_async_copy(xn_hbm.at[pl.ds(pl.multiple_of(tok * SLABS, SLABS), SLABS), :],
                                  buf.at[pl.ds(r * SLABS, SLABS), :], sem.at[s]).start()

    def wait(buf, s):
        pltpu.make_async_copy(xn_hbm.at[pl.ds(0, rows * SLABS), :], buf, sem.at[s]).wait()

    @pl.when(i == 0)
    def _():
        gather(0, xa, 0)

    changed = jnp.logical_or(i == 0, be_ref[i] != be_ref[jnp.maximum(i - 1, 0)])

    @pl.when(changed)
    def _():
        w1b[...] = w1_ref[0, 0].astype(BF16)
        w3b[...] = w3_ref[0, 0].astype(BF16)
        w2b[...] = w2_ref[0, 0].astype(BF16)

    nxt = jnp.minimum(i + 1, nblk - 1)

    def step(cur, s_cur, oth, s_oth):
        wait(cur, s_cur)
        gather(nxt, oth, s_oth)
        x = _from_token_tiles(cur, 0, rows).astype(BF16)
        h1 = _dot(x, w1b[...])
        h3 = _dot(x, w3b[...])
        hid = (h1 * jax.nn.sigmoid(h1)) * h3
        _to_token_tiles(o_ref, _dot(hid.astype(BF16), w2b[...]))

    @pl.when(i % 2 == 0)
    def _():
        step(xa, 0, xb, 1)

    @pl.when(i % 2 == 1)
    def _():
        step(xb, 1, xa, 0)

    @pl.when(i == nblk - 1)
    def _():
        @pl.when(i % 2 == 0)
        def _():
            wait(xb, 1)

        @pl.when(i % 2 == 1)
        def _():
            wait(xa, 0)


def _experts(block_expert, slot_tok, xn, w1, w3, w2, layer):
    nblk = block_expert.shape[0]
    rows = EXPERT_ROWS
    grid_spec = pltpu.PrefetchScalarGridSpec(
        num_scalar_prefetch=2,
        grid=(nblk,),
        in_specs=[pl.BlockSpec(memory_space=pl.ANY),
                  pl.BlockSpec((1, 1, D_MODEL, D_EXPERT), lambda i, be, st: (layer, be[i], 0, 0)),
                  pl.BlockSpec((1, 1, D_MODEL, D_EXPERT), lambda i, be, st: (layer, be[i], 0, 0)),
                  pl.BlockSpec((1, 1, D_EXPERT, D_MODEL), lambda i, be, st: (layer, be[i], 0, 0))],
        out_specs=pl.BlockSpec((rows * SLABS, LANES), lambda i, be, st: (i, 0)),
        scratch_shapes=[pltpu.VMEM((rows * SLABS, LANES), F32),
                        pltpu.VMEM((rows * SLABS, LANES), F32),
                        pltpu.SemaphoreType.DMA((2,)),
                        pltpu.VMEM((D_MODEL, D_EXPERT), BF16),
                        pltpu.VMEM((D_MODEL, D_EXPERT), BF16),
                        pltpu.VMEM((D_EXPERT, D_MODEL), BF16)])
    return pl.pallas_call(
        _expert_kernel,
        grid_spec=grid_spec,
        out_shape=jax.ShapeDtypeStruct((nblk * rows * SLABS, LANES), F32),
        compiler_params=_cparams(("arbitrary",)),
        name="moe_experts",
    )(block_expert, slot_tok, xn, w1, w3, w2)


_COMBINE_BLOCK = 128


def _combine_kernel(dest_ref, yb_hbm, x_ref, rt_ref, mod_ref, gfin_ref, o_ref, ybuf, sem, *, final_norm, row0):
    i = pl.program_id(0)
    nblk = pl.num_programs(0)
    slot = i % 2
    tm = _COMBINE_BLOCK

    def issue(blk, s):
        def body(j, carry):
            for c in range(2):
                d = dest_ref[c * T_ALL + row0 + blk * tm + j]
                pltpu.make_async_copy(yb_hbm.at[pl.ds(pl.multiple_of(d * SLABS, SLABS), SLABS), :],
                                      ybuf.at[s, pl.ds(pl.multiple_of((c * tm + j) * SLABS, SLABS), SLABS), :],
                                      sem.at[s]).start()
            return carry
        lax.fori_loop(0, tm, body, 0, unroll=4)

    @pl.when(i == 0)
    def _():
        issue(0, 0)

    @pl.when(i + 1 < nblk)
    def _():
        issue(i + 1, 1 - slot)

    pltpu.make_async_copy(yb_hbm.at[pl.ds(0, 2 * tm * SLABS), :], ybuf.at[slot], sem.at[slot]).wait()
    r = _mod_row(row0 + i * tm)
    rt = rt_ref[...]
    yv = ybuf.at[slot]
    y = rt[:, 2:3] * _from_token_tiles(yv, 0, tm) + rt[:, 3:4] * _from_token_tiles(yv, tm, tm)
    out = x_ref[...] + _mod_part(mod_ref, r, 5) * y
    if final_norm:
        out = _rms(out, gfin_ref[...])
    o_ref[...] = out


def _combine(dest, yb, x, rt, mod, gfin, final_norm, row0=0, t=T_ALL):
    tm = _COMBINE_BLOCK
    blk0 = row0 // tm
    grid_spec = pltpu.PrefetchScalarGridSpec(
        num_scalar_prefetch=1,
        grid=(t // tm,),
        in_specs=[pl.BlockSpec(memory_space=pl.ANY),
                  pl.BlockSpec((tm, D_MODEL), lambda i, d: (blk0 + i, 0)),
                  pl.BlockSpec((tm, LANES), lambda i, d: (blk0 + i, 0)),
                  pl.BlockSpec(mod.shape, lambda i, d: (0, 0)),
                  pl.BlockSpec(gfin.shape, lambda i, d: (0, 0))],
        out_specs=pl.BlockSpec((tm, D_MODEL), lambda i, d: (i, 0)),
        scratch_shapes=[pltpu.VMEM((2, 2 * tm * SLABS, LANES), F32),
                        pltpu.SemaphoreType.DMA((2,))])
    return pl.pallas_call(
        functools.partial(_combine_kernel, final_norm=final_norm, row0=row0),
        grid_spec=grid_spec,
        out_shape=jax.ShapeDtypeStruct((t, D_MODEL), F32),
        compiler_params=_cparams(("arbitrary",)),
        name="moe_combine",
    )(dest, yb, x, rt, mod, gfin)


def _combine_proj_kernel(dest_ref, yb_hbm, x_ref, rt_ref, mod0_ref, mod1_ref, g_ref, w_ref,
                         xo_ref, q_ref, k_ref, v_ref, ya, yb, sem, *, row0):
    i = pl.program_id(0)
    nblk = pl.num_programs(0)
    tm = x_ref.shape[0]

    def gather(blk, buf, s):
        for j in range(tm):
            for c in range(2):
                d = dest_ref[c * T_ALL + row0 + blk * tm + j]
                pltpu.make_async_copy(yb_hbm.at[pl.ds(pl.multiple_of(d * SLABS, SLABS), SLABS), :],
                                      buf.at[pl.ds((c * tm + j) * SLABS, SLABS), :], sem.at[s]).start()

    def wait(buf, s):
        pltpu.make_async_copy(yb_hbm.at[pl.ds(0, 2 * tm * SLABS), :], buf, sem.at[s]).wait()

    @pl.when(i == 0)
    def _():
        gather(0, ya, 0)

    nxt = jnp.minimum(i + 1, nblk - 1)

    def step(cur, s_cur, oth, s_oth):
        wait(cur, s_cur)
        gather(nxt, oth, s_oth)
        r = _mod_row(row0 + i * tm)
        rt = rt_ref[...]
        y = rt[:, 2:3] * _from_token_tiles(cur, 0, tm) + rt[:, 3:4] * _from_token_tiles(cur, tm, tm)
        x = x_ref[...] + _mod_part(mod0_ref, r, 5) * y
        xo_ref[...] = x
        h = _rms(x, g_ref[...]) * (1.0 + _mod_part(mod1_ref, r, 1)) + _mod_part(mod1_ref, r, 0)
        hb = h.astype(BF16)
        for j, o_ref in enumerate((q_ref, k_ref, v_ref)):
            o_ref[...] = _dot(hb, w_ref[:, j * C_WIDTH:(j + 1) * C_WIDTH])

    @pl.when(i % 2 == 0)
    def _():
        step(ya, 0, yb, 1)

    @pl.when(i % 2 == 1)
    def _():
        step(yb, 1, ya, 0)

    @pl.when(i == nblk - 1)
    def _():
        @pl.when(i % 2 == 0)
        def _():
            wait(yb, 1)

        @pl.when(i % 2 == 1)
        def _():
            wait(ya, 0)


def _combine_proj(dest, yb, x, rt, mod0, mod1, g, w, row0, t):
    tm = TOK_BLOCK
    blk0 = row0 // tm
    full = lambda a: pl.BlockSpec(a.shape, lambda i, d: (0,) * a.ndim)
    grid_spec = pltpu.PrefetchScalarGridSpec(
        num_scalar_prefetch=1,
        grid=(t // tm,),
        in_specs=[pl.BlockSpec(memory_space=pl.ANY),
                  pl.BlockSpec((tm, D_MODEL), lambda i, d: (blk0 + i, 0)),
                  pl.BlockSpec((tm, LANES), lambda i, d: (blk0 + i, 0)),
                  full(mod0), full(mod1), full(g), full(w)],
        out_specs=[pl.BlockSpec((tm, D_MODEL), lambda i, d: (i, 0))] * 4,
        scratch_shapes=[pltpu.VMEM((2 * tm * SLABS, LANES), F32),
                        pltpu.VMEM((2 * tm * SLABS, LANES), F32),
                        pltpu.SemaphoreType.DMA((2,))])
    return pl.pallas_call(
        functools.partial(_combine_proj_kernel, row0=row0),
        grid_spec=grid_spec,
        out_shape=[jax.ShapeDtypeStruct((t, D_MODEL), F32)] * 4,
        compiler_params=_cparams(("arbitrary",)),
        name="moe_combine_proj",
    )(dest, yb, x, rt, mod0, mod1, g, w)


def _moe_plan(rtt, cnt):
    t = rtt.shape[1]
    eid = rtt[0:2].astype(jnp.int32)
    rank = rtt[4:6].astype(jnp.int32)
    counts = cnt[0, N_GROUPS:N_GROUPS + N_EXPERTS].astype(jnp.int32)
    padded = (counts + EXPERT_ROWS - 1) // EXPERT_ROWS * EXPERT_ROWS
    seg_end = jnp.cumsum(padded)
    seg_start = seg_end - padded
    experts = jnp.arange(N_EXPERTS, dtype=jnp.int32)
    start = jnp.sum(jnp.where(eid[..., None] == experts, seg_start, 0), axis=-1)
    dest = (start + rank).reshape(-1)
    n_blocks = (2 * t + N_EXPERTS * (EXPERT_ROWS - 1) + EXPERT_ROWS - 1) // EXPERT_ROWS
    tok = jnp.tile(jnp.arange(t, dtype=jnp.int32), 2)
    slot_tok = (jnp.arange(n_blocks * EXPERT_ROWS, dtype=jnp.int32) % t).at[dest].set(tok)
    first_row = jnp.arange(n_blocks, dtype=jnp.int32) * EXPERT_ROWS
    block_expert = jnp.minimum(jnp.sum((seg_end[None, :] <= first_row[:, None]).astype(jnp.int32), axis=1),
                               N_EXPERTS - 1)
    return dest, slot_tok, block_expert


def _attn_c_prompt_kernel(q_ref, k_ref, v_ref, o_ref):
    for p in range(C_HEADS // 2):
        sl = slice(p * LANES, (p + 1) * LANES)
        kb = k_ref[:, sl].astype(BF16)
        vb = v_ref[:, sl].astype(BF16)
        qp = q_ref[:, sl] * (C_HEAD_DIM ** -0.5)
        o_ref[:, sl] = _softmax_pair(qp, lambda qm: [_dot_nt(qm, kb)], lambda ps: _dot(ps[0], vb))


def _attn_c_prompt(q, k, v):
    blk = pl.BlockSpec((SEQ, C_WIDTH), lambda b: (b, 0))
    return pl.pallas_call(
        _attn_c_prompt_kernel,
        grid=(BATCH,),
        in_specs=[blk, blk, blk],
        out_specs=blk,
        out_shape=jax.ShapeDtypeStruct((T_PROMPT, C_WIDTH), F32),
        compiler_params=_cparams(("arbitrary",)),
        name="attn_c_prompt",
    )(q, k, v)


def _na_key_start(r0):
    rows = DEC_SEQ // GRID_W
    return jnp.minimum(jnp.clip(r0 - NA_ROWS // 2, 0, rows - NA_ROWS), rows - NA_KROWS)


def _na_block_plan():
    rows = DEC_SEQ // GRID_W
    nblk = rows // NA_QROWS
    plan = []
    for blk in (0, 1, nblk - 1):
        r0 = blk * NA_QROWS
        ks = min(int(np.clip(r0 - NA_ROWS // 2, 0, rows - NA_ROWS)), rows - NA_KROWS)
        per_row = []
        for i in range(NA_QROWS):
            r = r0 + i
            rs = int(np.clip(r - NA_ROWS // 2, 0, rows - NA_ROWS))
            start = ks - r + NA_ROWS - 1 + NA_KROWS
            ok = [rs <= ks + j < rs + NA_ROWS for j in range(NA_KROWS)]
            per_row.append((start, ok))
        plan.append(per_row)
    return plan


def _attn_c_sample_kernel(q_ref, k_ref, v_ref, ck_ref, cv_ref, toe_ref, o_ref, bias_ref):
    rows = DEC_SEQ // GRID_W
    nblk = rows // NA_QROWS
    w = GRID_W

    @pl.when(jnp.logical_and(pl.program_id(1) == 0, pl.program_id(2) == 0))
    def _():
        neg = jnp.full((w, w), NEG, F32)
        for t, per_row in enumerate(_na_block_plan()):
            for half in range(2):
                for i, (start, ok) in enumerate(per_row):
                    for j in range(0, NA_KROWS, 2):
                        pieces = [toe_ref[0, half, start + jj] if ok[jj] else neg for jj in (j, j + 1)]
                        bias_ref[t, half, i * w:(i + 1) * w, j * w:(j + 2) * w] = jnp.concatenate(pieces, axis=1)

    i = pl.program_id(2)
    r0 = i * NA_QROWS
    k0 = pl.multiple_of(_na_key_start(r0) * GRID_W, GRID_W)
    btype = jnp.where(i == 0, 0, jnp.where(i == nblk - 1, 2, 1))
    nk = NA_KROWS * GRID_W
    kw = k_ref[pl.ds(k0, nk), :].astype(BF16)
    vw = v_ref[pl.ds(k0, nk), :].astype(BF16)
    kc = ck_ref[0].astype(BF16)
    vc = cv_ref[0].astype(BF16)
    qp = q_ref[...] * (C_HEAD_DIM ** -0.5)
    lo = _lane(qp.shape) < 64
    outs = []
    for half in range(2):
        qm = jnp.where(lo if half == 0 else jnp.logical_not(lo), qp, 0.0).astype(BF16)
        s_win = _dot_nt(qm, kw) + bias_ref[btype, half]
        s_ctx = _dot_nt(qm, kc)
        m = jnp.maximum(s_win.max(axis=-1, keepdims=True), s_ctx.max(axis=-1, keepdims=True))
        e_win = jnp.exp(s_win - m)
        e_ctx = jnp.exp(s_ctx - m)
        l = e_win.sum(axis=-1, keepdims=True) + e_ctx.sum(axis=-1, keepdims=True)
        o = _dot(e_win.astype(BF16), vw) + _dot(e_ctx.astype(BF16), vc)
        outs.append(o / l)
    o_ref[...] = jnp.where(lo, outs[0], outs[1])


def _attn_c_sample(q, k, v, ck, cv, toe):
    rows = DEC_SEQ // GRID_W
    nblk = rows // NA_QROWS
    qrows = NA_QROWS * GRID_W
    npair = C_HEADS // 2
    return pl.pallas_call(
        _attn_c_sample_kernel,
        grid=(npair, DEC_BATCH, nblk),
        in_specs=[pl.BlockSpec((qrows, LANES), lambda p, b, i: (b * nblk + i, p)),
                  pl.BlockSpec((DEC_SEQ, LANES), lambda p, b, i: (b, p)),
                  pl.BlockSpec((DEC_SEQ, LANES), lambda p, b, i: (b, p)),
                  pl.BlockSpec((1, PAST_LEN, LANES), lambda p, b, i: (b, 0, p)),
                  pl.BlockSpec((1, PAST_LEN, LANES), lambda p, b, i: (b, 0, p)),
                  pl.BlockSpec((1,) + toe.shape[1:], lambda p, b, i: (p, 0, 0, 0, 0))],
        out_specs=pl.BlockSpec((qrows, LANES), lambda p, b, i: (b * nblk + i, p)),
        out_shape=jax.ShapeDtypeStruct((T_SAMPLE, C_WIDTH), F32),
        scratch_shapes=[pltpu.VMEM((3, 2, qrows, NA_KROWS * GRID_W), F32)],
        compiler_params=_cparams(("arbitrary", "arbitrary", "arbitrary")),
        name="attn_c_sample",
    )(q, k, v, ck, cv, toe)


def _na_toeplitz(rpb):
    w = GRID_W
    nd_r, nd_c = 2 * NA_ROWS - 1, 2 * NA_COLS - 1
    c = np.arange(w)
    cs = np.clip(c - NA_COLS // 2, 0, w - NA_COLS)
    col_ok = (c[None, :] >= cs[:, None]) & (c[None, :] < cs[:, None] + NA_COLS)
    dcol = c[None, :] - c[:, None] + NA_COLS - 1
    onehot = (np.arange(nd_c)[:, None, None] == dcol[None]).reshape(nd_c, w * w)
    toe = jnp.dot(rpb.reshape(C_HEADS * nd_r, nd_c), jnp.asarray(onehot, F32), precision=lax.Precision.HIGHEST)
    toe = jnp.where(col_ok[None, None], toe.reshape(C_HEADS, nd_r, w, w), NEG)
    toe = jnp.pad(toe, ((0, 0), (NA_KROWS, NA_KROWS), (0, 0), (0, 0)), constant_values=NEG)
    return toe.reshape(C_HEADS // 2, 2, nd_r + 2 * NA_KROWS, w, w)


def _rope_tables():
    half = A_HEAD_DIM // 2
    t = jnp.arange(DEC_SEQ)
    row = (t // GRID_W).astype(F32)
    colp = (t % GRID_W).astype(F32)
    freqs = 1.0 / (ROPE_BASE ** (jnp.arange(0, half, 2, dtype=F32) / half))
    d = np.arange(LANES) % A_HEAD_DIM
    pos = jnp.where(jnp.asarray(d < half)[None, :], row[:, None], colp[:, None])
    ang = pos * freqs[d % (half // 2)][None, :]
    sign = jnp.asarray(np.where((d % half) < half // 2, -1.0, 1.0), F32)[None, :]
    return jnp.cos(ang), jnp.sin(ang) * sign


def _head_avg_matrix():
    idx = np.arange(LANES) // A_HEAD_DIM
    return jnp.asarray((idx[:, None] == idx[None, :]).astype(np.float32) / A_HEAD_DIM, BF16)


def _tri_matrices():
    i = np.arange(MLSTM_CHUNK)
    upper = (i[:, None] <= i[None, :]).astype(np.float32)
    lower = (i[:, None] >= i[None, :]).astype(np.float32)
    return jnp.asarray(np.stack([upper, lower]), BF16)


def _router_weights(wg, bg, we, be):
    w = jnp.zeros((D_MODEL, LANES), F32).at[:, :N_GROUPS].set(wg).at[:, N_GROUPS:N_GROUPS + N_EXPERTS].set(we)
    b = jnp.zeros((1, LANES), F32).at[0, :N_GROUPS].set(bg).at[0, N_GROUPS:N_GROUPS + N_EXPERTS].set(be)
    hi = w.astype(BF16)
    lo = (w - hi.astype(F32)).astype(BF16)
    return jnp.concatenate([hi, lo], axis=1), b


def _moe(xn, rtt, cnt, w1, w3, w2, layer):
    dest, slot_tok, block_expert = _moe_plan(rtt, cnt)
    return dest, _experts(block_expert, slot_tok, xn, w1, w3, w2, layer)


def kernel(x_prompt, x_sample, cache_attn_k, cache_attn_v, state_mlstm_C, state_mlstm_n, state_mlstm_m,
           cache_na_k, cache_na_v, c, c_ctx, norm_mix, norm_ffn, norm_final, ada_w, ada_b,
           ab_w_in, ab_w_out, ab_q_norm, ab_k_norm, ab_gate_bias, ab_out_norm,
           na_w_in, na_w_out, na_rpb, moe_wg, moe_bg, moe_we, moe_be, moe_w1, moe_w3, moe_w2):
    xp = x_prompt.reshape(T_PROMPT, D_MODEL)
    xs = x_sample.reshape(T_SAMPLE, D_MODEL)
    cond =jnp.zeros((N_COND, D_MODEL), F32).at[0].set(c_ctx).at[1:1 + DEC_BATCH].set(c)
    mod = _modulation(cond, ada_w, ada_b)
    gfin = norm_final.reshape(1, D_MODEL)

    w_in = ab_w_in[0]
    o_aq, o_ak, o_av, o_bq, o_bk, o_bv, o_bo, o_bg = np.cumsum((0,) + (A_WIDTH, A_KV_WIDTH, A_KV_WIDTH,
                                                                       B_WIDTH, B_WIDTH, B_WIDTH, B_WIDTH))
    wb = w_in[:, o_bq:o_bg].astype(BF16)
    wq = w_in[:, o_aq:o_ak].astype(BF16)
    wkv = w_in[:, o_ak:o_bq].astype(BF16)
    wg = w_in[:, o_bg:o_bg + 4 * B_HEADS].reshape(D_MODEL, 4, B_HEADS)
    wgt = jnp.zeros((B_HEADS, 8, D_MODEL), F32).at[:, :4, :].set(wg.transpose(2, 1, 0))
    wgt = wgt.reshape(8 * B_HEADS, D_MODEL).astype(BF16)
    g_mix = norm_mix[0].reshape(1, D_MODEL)
    ob_p, oq_p, okv_p, ogt_p = _proj_ab(xp, mod[0], g_mix, wb, wq, wkv, wgt, 0)
    ob_s, oq_s, okv_s, ogt_s = _proj_ab(xs, mod[0], g_mix, wb, wq, wkv, wgt, T_PROMPT)

    qn = jnp.tile(ab_q_norm[0], 2).reshape(1, LANES)
    kn = jnp.tile(ab_k_norm[0], 2).reshape(1, LANES)
    bd = _head_avg_matrix()
    cos, sin = _rope_tables()
    a_p, new_k, new_v = _attn_a_prompt(oq_p, okv_p, qn, kn, bd)
    ck = cache_attn_k[:, 0].reshape(DEC_BATCH, PAST_LEN, A_KV_WIDTH)
    cv = cache_attn_v[:, 0].reshape(DEC_BATCH, PAST_LEN, A_KV_WIDTH)
    a_s = _attn_a_sample(oq_s, okv_s, ck, cv, cos, sin, qn, kn, bd)

    gb = ab_gate_bias[0]
    brow = jnp.zeros((B_HEADS, 8, LANES), F32).at[:, :4, :].set(
        jnp.broadcast_to(gb.T[:, :, None], (B_HEADS, 4, LANES)))
    onorm = ab_out_norm[0].reshape(B_HEADS, 1, B_HEAD_DIM)
    tri = _tri_matrices()
    b_p, cT, nT, mT = _mlstm(ob_p, ogt_p, brow, onorm, tri, n=SEQ, nseq=BATCH, emit_state=True)
    c0 = state_mlstm_C[:, 0]
    n0 = state_mlstm_n[:, 0].transpose(0, 2, 1, 3)
    m0 = jnp.broadcast_to(state_mlstm_m[:, 0].transpose(0, 2, 1)[..., None], (DEC_BATCH, B_HEADS, 2, LANES))
    (b_s,) = _mlstm(ob_s, ogt_s, brow, onorm, tri, n=DEC_SEQ, nseq=DEC_BATCH, init=(c0, n0, m0))

    w_out = ab_w_out[0].astype(BF16)
    wr, br = _router_weights(moe_wg[0], moe_bg[0], moe_we[0], moe_be[0])
    x1, xn, rt, rtt, cnt = _post([(xp, xs), (a_p, a_s), (b_p, b_s)], [w_out[:A_WIDTH], w_out[A_WIDTH:]], mod[0],
                                 norm_ffn[0].reshape(1, D_MODEL), wr, br)
    dest, yb = _moe(xn, rtt, cnt, moe_w1, moe_w3, moe_w2, 0)

    g_mix = norm_mix[1].reshape(1, D_MODEL)
    w_in = na_w_in[0].astype(BF16)
    x_p, q_p, k_p, v_p = _combine_proj(dest, yb, x1, rt, mod[0], mod[1], g_mix, w_in, 0, T_PROMPT)
    x_s, q_s, k_s, v_s = _combine_proj(dest, yb, x1, rt, mod[0], mod[1], g_mix, w_in, T_PROMPT, T_SAMPLE)
    o_p = _attn_c_prompt(q_p, k_p, v_p)
    nck = cache_na_k[:, 0].reshape(DEC_BATCH, PAST_LEN, C_WIDTH)
    ncv = cache_na_v[:, 0].reshape(DEC_BATCH, PAST_LEN, C_WIDTH)
    o_s = _attn_c_sample(q_s, k_s, v_s, nck, ncv, _na_toeplitz(na_rpb[0]))
    wr, br = _router_weights(moe_wg[1], moe_bg[1], moe_we[1], moe_be[1])
    x1, xn, rt, rtt, cnt = _post([(x_p, x_s), (o_p, o_s)], [na_w_out[0].astype(BF16)], mod[1],
                                 norm_ffn[1].reshape(1, D_MODEL), wr, br)
    dest, yb = _moe(xn, rtt, cnt, moe_w1, moe_w3, moe_w2, 1)
    y_prompt = _combine(dest, yb, x1, rt, mod[1], gfin, True, 0, T_PROMPT).reshape(BATCH, SEQ, D_MODEL)
    y_sample = _combine(dest, yb, x1, rt, mod[1], gfin, True, T_PROMPT, T_SAMPLE).reshape(DEC_BATCH, DEC_SEQ, D_MODEL)
    new_attn_k = new_k.reshape(BATCH, 1, SEQ, A_KV_HEADS, A_HEAD_DIM)
    new_attn_v = new_v.reshape(BATCH, 1, SEQ, A_KV_HEADS, A_HEAD_DIM)
    new_mlstm_C = cT[:, None]
    new_mlstm_n = nT.transpose(0, 2, 1, 3)[:, None]
    new_mlstm_m = mT[..., 0].transpose(0, 2, 1)[:, None]
    new_na_k = k_p.reshape(BATCH, 1, SEQ, C_HEADS, C_HEAD_DIM)
    new_na_v = v_p.reshape(BATCH, 1, SEQ, C_HEADS, C_HEAD_DIM)
    return (y_prompt, y_sample, new_attn_k, new_attn_v, new_mlstm_C, new_mlstm_n, new_mlstm_m,
            new_na_k, new_na_v)
```

```python
import functools

import numpy as np
import jax
import jax.numpy as jnp
from jax import lax
from jax.experimental import pallas as pl
from jax.experimental.pallas import tpu as pltpu

F32 = jnp.float32
BF16 = jnp.bfloat16

D_MODEL = 1024
BATCH = 32
SEQ = 256
DEC_BATCH = 4
DEC_SEQ = 2048
PAST_LEN = 256
GRID_W = 64
A_HEADS = 8
A_KV_HEADS = 2
A_HEAD_DIM = 64
A_WIDTH = A_HEADS * A_HEAD_DIM
A_KV_WIDTH = A_KV_HEADS * A_HEAD_DIM
B_HEADS = 4
B_HEAD_DIM = 128
B_WIDTH = B_HEADS * B_HEAD_DIM
MLSTM_CHUNK = 128
C_HEADS = 16
C_HEAD_DIM = 64
C_WIDTH = C_HEADS * C_HEAD_DIM
NA_ROWS = 8
NA_COLS = 16
N_GROUPS = 4
EXPERTS_PER_GROUP = 8
N_EXPERTS = N_GROUPS * EXPERTS_PER_GROUP
D_EXPERT = 512
MOE_BLOCK = 128
ROPE_BASE = 10000.0
NORM_EPS = 1e-6

T_PROMPT = BATCH * SEQ
T_SAMPLE = DEC_BATCH * DEC_SEQ
T_ALL = T_PROMPT + T_SAMPLE
N_COND = 8
LANES = 128
SLABS = D_MODEL // LANES
TOK_BLOCK = 256
MLSTM_HEADS_PER_STEP = 2
EXPERT_ROWS = 256
NA_QROWS = 4
NA_KROWS = 12
NEG = -1e30
VMEM_LIMIT = 56 * 1024 * 1024


def _cparams(sem):
    return pltpu.CompilerParams(dimension_semantics=sem, vmem_limit_bytes=VMEM_LIMIT)


def _split2(x):
    hi = x.astype(BF16)
    lo = (x - hi.astype(F32)).astype(BF16)
    return hi, lo


def _split3(x):
    hi = x.astype(BF16)
    r = x - hi.astype(F32)
    mid = r.astype(BF16)
    lo = (r - mid.astype(F32)).astype(BF16)
    return hi, mid, lo


def _dot(a, b):
    return jnp.dot(a, b, preferred_element_type=F32)


def _dot_nt(a, b):
    return lax.dot_general(a, b, (((1,), (1,)), ((), ())), preferred_element_type=F32)


def _dot_exact_rhs(x, b):
    hi, mid, lo = _split3(x)
    return _dot(hi, b) + _dot(mid, b) + _dot(lo, b)


def _rms(x, g):
    ms = jnp.mean(x * x, axis=-1, keepdims=True)
    return (x * lax.rsqrt(ms + NORM_EPS)) * g


def _mod_row(tok0):
    return jnp.where(tok0 < T_PROMPT, 0, 1 + (tok0 - T_PROMPT) // DEC_SEQ)


def _mod_part(mod_ref, r, idx):
    return mod_ref[pl.ds(r, 1), idx * D_MODEL:(idx + 1) * D_MODEL]


def _head_rms(x, w, bd):
    hi, lo = _split2(x * x)
    ms = _dot(hi, bd) + _dot(lo, bd)
    return (x * lax.rsqrt(ms + NORM_EPS)) * w


def _to_token_tiles(ref, x):
    m = x.shape[0]
    for s in range(SLABS):
        ref[pl.ds(s, m, stride=SLABS), :] = x[:, s * LANES:(s + 1) * LANES]


def _from_token_tiles(ref, tile0, m):
    return jnp.concatenate([ref[pl.ds(tile0 * SLABS + s, m, stride=SLABS), :] for s in range(SLABS)], axis=1)


def _lane(shape):
    return lax.broadcasted_iota(jnp.int32, shape, len(shape) - 1)


def _dup_half(x, g):
    xr = pltpu.roll(x, 64, 1)
    lo = _lane(x.shape) < 64
    return jnp.where(lo, x, xr) if g == 0 else jnp.where(lo, xr, x)


def _rope(x, cos, sin_signed):
    lane = _lane(x.shape)
    partner = jnp.where((lane % 32) < 16, pltpu.roll(x, LANES - 16, 1), pltpu.roll(x, 16, 1))
    return x * cos + partner * sin_signed


def _softmax_pair(qp, score_fn, value_fn):
    lo = _lane(qp.shape) < 64
    outs = []
    for half in range(2):
        qm = jnp.where(lo if half == 0 else jnp.logical_not(lo), qp, 0.0).astype(BF16)
        ss = score_fn(qm)
        m = ss[0].max(axis=-1, keepdims=True)
        for s in ss[1:]:
            m = jnp.maximum(m, s.max(axis=-1, keepdims=True))
        es = [jnp.exp(s - m) for s in ss]
        l = es[0].sum(axis=-1, keepdims=True)
        for e in es[1:]:
            l = l + e.sum(axis=-1, keepdims=True)
        o = value_fn([e.astype(BF16) for e in es])
        outs.append(o / l)
    return jnp.where(lo, outs[0], outs[1])


def _mod_kernel(cond_ref, w_ref, b_ref, o_ref):
    c = cond_ref[...]
    s = c * jax.nn.sigmoid(c)
    s_hi, s_lo = _split2(s)
    w_hi, w_lo = _split2(w_ref[0])
    o_ref[0] = _dot(s_hi, w_hi) + _dot(s_lo, w_hi) + _dot(s_hi, w_lo) + b_ref[0]


def _modulation(cond, ada_w, ada_b):
    depth, d, n = ada_w.shape
    tn = 1536
    return pl.pallas_call(
        _mod_kernel,
        grid=(depth, n // tn),
        in_specs=[pl.BlockSpec((N_COND, d), lambda l, j: (0, 0)),
                  pl.BlockSpec((1, d, tn), lambda l, j: (l, 0, j)),
                  pl.BlockSpec((1, 1, tn), lambda l, j: (l, 0, j))],
        out_specs=pl.BlockSpec((1, N_COND, tn), lambda l, j: (l, 0, j)),
        out_shape=jax.ShapeDtypeStruct((depth, N_COND, n), F32),
        compiler_params=_cparams(("arbitrary", "arbitrary")),
        name="adaln_modulation",
    )(cond, ada_w, ada_b.reshape(depth, 1, n))


def _norm_mod(x_ref, mod_ref, g_ref, shift_idx, scale_idx, row0):
    r = _mod_row(row0 + pl.program_id(0) * x_ref.shape[0])
    h = _rms(x_ref[...], g_ref[...])
    return h * (1.0 + _mod_part(mod_ref, r, scale_idx)) + _mod_part(mod_ref, r, shift_idx)


def _proj_ab_kernel(x_ref, mod_ref, g_ref, wb_ref, wq_ref, wkv_ref, wgt_ref, ob_ref, oq_ref, okv_ref, ogt_ref,
                    *, row0):
    tm = x_ref.shape[0]
    hb = _norm_mod(x_ref, mod_ref, g_ref, 0, 1, row0).astype(BF16)
    ob_ref[...] = _dot(hb, wb_ref[...])
    oq_ref[...] = _dot(hb, wq_ref[...])
    okv_ref[...] = _dot(hb, wkv_ref[...])
    gt = _dot_nt(wgt_ref[...], hb)
    for j in range(tm // LANES):
        ogt_ref[j] = gt[:, j * LANES:(j + 1) * LANES]


def _proj_ab(x, mod, g, wb, wq, wkv, wgt, row0):
    t = x.shape[0]
    tm = TOK_BLOCK
    full = lambda a: pl.BlockSpec(a.shape, lambda i: (0,) * a.ndim)
    return pl.pallas_call(
        functools.partial(_proj_ab_kernel, row0=row0),
        grid=(t // tm,),
        in_specs=[pl.BlockSpec((tm, D_MODEL), lambda i: (i, 0)), full(mod), full(g),
                  full(wb), full(wq), full(wkv), full(wgt)],
        out_specs=[pl.BlockSpec((tm, wb.shape[1]), lambda i: (i, 0)),
                   pl.BlockSpec((tm, A_WIDTH), lambda i: (i, 0)),
                   pl.BlockSpec((tm, 2 * A_KV_WIDTH), lambda i: (i, 0)),
                   pl.BlockSpec((tm // LANES, wgt.shape[0], LANES), lambda i: (i, 0, 0))],
        out_shape=[jax.ShapeDtypeStruct((t, wb.shape[1]), F32),
                   jax.ShapeDtypeStruct((t, A_WIDTH), F32),
                   jax.ShapeDtypeStruct((t, 2 * A_KV_WIDTH), F32),
                   jax.ShapeDtypeStruct((t // LANES, wgt.shape[0], LANES), F32)],
        compiler_params=_cparams(("arbitrary",)),
        name="proj_ab",
    )(x, mod, g, wb, wq, wkv, wgt)


def _gqa_block(q_ref, qn, bd, kd_ref, vd_ref, o_ref, rope=None):
    for p in range(A_HEADS // 2):
        g = p // (A_HEADS // 2 // A_KV_HEADS)
        qp = _head_rms(q_ref[:, p * LANES:(p + 1) * LANES], qn, bd)
        if rope is not None:
            qp = _rope(qp, rope[0], rope[1])
        qp = qp * (A_HEAD_DIM ** -0.5)
        o_ref[:, p * LANES:(p + 1) * LANES] = _softmax_pair(
            qp, lambda qm: [_dot_nt(qm, kd_ref[g])], lambda ps: _dot(ps[0], vd_ref[g]))


def _attn_a_prompt_kernel(q_ref, kv_ref, qn_ref, kn_ref, bd_ref, o_ref, knew_ref, vnew_ref, kd_ref, vd_ref):
    bd = bd_ref[...]
    k = _head_rms(kv_ref[:, :LANES], kn_ref[...], bd)
    v = kv_ref[:, LANES:]
    knew_ref[...] = k
    vnew_ref[...] = v
    for g in range(A_KV_HEADS):
        kd_ref[g] = _dup_half(k, g).astype(BF16)
        vd_ref[g] = _dup_half(v, g).astype(BF16)
    _gqa_block(q_ref, qn_ref[...], bd, kd_ref, vd_ref, o_ref)


def _attn_a_prompt(q, kv, qn, kn, bd):
    nb = BATCH
    full = lambda a: pl.BlockSpec(a.shape, lambda b: (0,) * a.ndim)
    return pl.pallas_call(
        _attn_a_prompt_kernel,
        grid=(nb,),
        in_specs=[pl.BlockSpec((SEQ, A_WIDTH), lambda b: (b, 0)),
                  pl.BlockSpec((SEQ, 2 * A_KV_WIDTH), lambda b: (b, 0)),
                  full(qn), full(kn), full(bd)],
        out_specs=[pl.BlockSpec((SEQ, A_WIDTH), lambda b: (b, 0)),
                   pl.BlockSpec((SEQ, A_KV_WIDTH), lambda b: (b, 0)),
                   pl.BlockSpec((SEQ, A_KV_WIDTH), lambda b: (b, 0))],
        out_shape=[jax.ShapeDtypeStruct((T_PROMPT, A_WIDTH), F32),
                   jax.ShapeDtypeStruct((T_PROMPT, A_KV_WIDTH), F32),
                   jax.ShapeDtypeStruct((T_PROMPT, A_KV_WIDTH), F32)],
        scratch_shapes=[pltpu.VMEM((A_KV_HEADS, SEQ, LANES), BF16),
                        pltpu.VMEM((A_KV_HEADS, SEQ, LANES), BF16)],
        compiler_params=_cparams(("arbitrary",)),
        name="attn_a_prompt",
    )(q, kv, qn, kn, bd)


_A_QBLOCK = 256


def _attn_a_sample_kernel(q_ref, kv_ref, ck_ref, cv_ref, cos_ref, sin_ref, cosq_ref, sinq_ref,
                          qn_ref, kn_ref, bd_ref, o_ref, kd_ref, vd_ref):
    bd = bd_ref[...]

    @pl.when(pl.program_id(1) == 0)
    def _():
        for g in range(A_KV_HEADS):
            kd_ref[g, :PAST_LEN] = _dup_half(ck_ref[0], g).astype(BF16)
            vd_ref[g, :PAST_LEN] = _dup_half(cv_ref[0], g).astype(BF16)
        rows = 256
        for c in range(DEC_SEQ // rows):
            sl = slice(c * rows, (c + 1) * rows)
            k = _head_rms(kv_ref[sl, :LANES], kn_ref[...], bd)
            k = _rope(k, cos_ref[sl, :], sin_ref[sl, :])
            v = kv_ref[sl, LANES:]
            dst = slice(PAST_LEN + c * rows, PAST_LEN + (c + 1) * rows)
            for g in range(A_KV_HEADS):
                kd_ref[g, dst] = _dup_half(k, g).astype(BF16)
                vd_ref[g, dst] = _dup_half(v, g).astype(BF16)

    _gqa_block(q_ref, qn_ref[...], bd, kd_ref, vd_ref, o_ref, rope=(cosq_ref[...], sinq_ref[...]))


def _attn_a_sample(q, kv, ck, cv, cos, sin, qn, kn, bd):
    nq = DEC_SEQ // _A_QBLOCK
    full = lambda a: pl.BlockSpec(a.shape, lambda b, i: (0,) * a.ndim)
    tk = PAST_LEN + DEC_SEQ
    return pl.pallas_call(
        _attn_a_sample_kernel,
        grid=(DEC_BATCH, nq),
        in_specs=[pl.BlockSpec((_A_QBLOCK, A_WIDTH), lambda b, i: (b * nq + i, 0)),
                  pl.BlockSpec((DEC_SEQ, 2 * A_KV_WIDTH), lambda b, i: (b, 0)),
                  pl.BlockSpec((1, PAST_LEN, A_KV_WIDTH), lambda b, i: (b, 0, 0)),
                  pl.BlockSpec((1, PAST_LEN, A_KV_WIDTH), lambda b, i: (b, 0, 0)),
                  full(cos), full(sin),
                  pl.BlockSpec((_A_QBLOCK, LANES), lambda b, i: (i, 0)),
                  pl.BlockSpec((_A_QBLOCK, LANES), lambda b, i: (i, 0)),
                  full(qn), full(kn), full(bd)],
        out_specs=pl.BlockSpec((_A_QBLOCK, A_WIDTH), lambda b, i: (b * nq + i, 0)),
        out_shape=jax.ShapeDtypeStruct((T_SAMPLE, A_WIDTH), F32),
        scratch_shapes=[pltpu.VMEM((A_KV_HEADS, tk, LANES), BF16),
                        pltpu.VMEM((A_KV_HEADS, tk, LANES), BF16)],
        compiler_params=_cparams(("arbitrary", "arbitrary")),
        name="attn_a_sample",
    )(q, kv, ck, cv, cos, sin, cos, sin, qn, kn, bd)


def _log_sigmoid(x):
    return -(jnp.maximum(-x, 0.0) + jnp.log1p(jnp.exp(-jnp.abs(x))))


def _mlstm_kernel(*refs, n, has_init, emit_state):
    it = iter(refs)
    q_ref, v_ref, og_ref, kt_ref, gt_ref, brow_ref, onorm_ref, tri_ref = [next(it) for _ in range(8)]
    if has_init:
        c0_ref, m0_ref = [next(it) for _ in range(2)]
    out_ref = next(it)
    if emit_state:
        cT_ref, nT_ref, mT_ref = [next(it) for _ in range(3)]
    h_ref, row_ref, c_ref = [next(it) for _ in range(3)]

    L = MLSTM_CHUNK
    D = B_HEAD_DIM
    nc = n // L
    hb = MLSTM_HEADS_PER_STEP

    lane = lax.broadcasted_iota(jnp.int32, (nc * 8, L), 1)
    is_fwd = lax.broadcasted_iota(jnp.int32, (nc * 8, L), 0) % 8 == 0

    def running_max(x, suffix):
        for sh in (1, 2, 4, 8, 16, 32, 64):
            if suffix:
                x = jnp.where(lane < L - sh, jnp.maximum(x, pltpu.roll(x, L - sh, 1)), x)
            else:
                x = jnp.where(lane >= sh, jnp.maximum(x, pltpu.roll(x, sh, 1)), x)
        return x

    for hh in range(hb):
        gt = gt_ref[:, 16 * hh:16 * hh + 16, :] + brow_ref[hh][None]
        li = gt[:, 0:8, :].reshape(nc * 8, L)
        lf = _log_sigmoid(gt[:, 8:16, :]).reshape(nc * 8, L)
        cum = jnp.where(is_fwd, _dot_exact_rhs(lf, tri_ref[0]), _dot_exact_rhs(lf, tri_ref[1]))
        a = li - cum
        planes = (cum, a, jnp.broadcast_to(lf.sum(axis=-1, keepdims=True), (nc * 8, L)),
                  jnp.where(is_fwd, running_max(a, False), running_max(a, True)),
                  jnp.broadcast_to(a.max(axis=-1, keepdims=True), (nc * 8, L)))
        for p, val in enumerate(planes):
            row_ref[hh, p] = val.reshape(nc, 8, L)

    chains = [(hh, d) for hh in range(hb) for d in range(2)]
    if has_init:
        m_init = []
        for hh, d in chains:
            c_ref[2 * hh + d] = c0_ref[0, d, hh]
            m_init.append(m0_ref[0, hh, d:d + 1, :])
        m_init = tuple(m_init)
    else:
        c_ref[...] = jnp.zeros(c_ref.shape, F32)
        m_init = tuple(jnp.zeros((1, L), F32) for _ in chains)

    t_idx = lax.broadcasted_iota(jnp.int32, (L, L), 0)
    s_idx = lax.broadcasted_iota(jnp.int32, (L, L), 1)
    masks = (s_idx <= t_idx, s_idx >= t_idx)
    ones = jnp.ones((L, L), F32)

    def step(c, hh, d, m):
        r0 = pl.multiple_of(c * L, L)
        hl = slice(hh * L, (hh + 1) * L)
        qb = q_ref[pl.ds(r0, L), hl].astype(BF16)
        kst = kt_ref[c, hl, :] * (D ** -0.5)
        v_ext = jnp.concatenate([v_ref[pl.ds(r0, L), hl], ones], axis=1).astype(BF16)
        cum, a_row, tot, amax_run, amax = [row_ref[hh, p, c][d:d + 1, :] for p in range(5)]
        m_cum = jnp.broadcast_to(cum, (L, L)).T
        m_run = jnp.broadcast_to(amax_run, (L, L)).T
        dlog = jnp.where(masks[d], m_cum + a_row, -jnp.inf)
        inter = m_cum + m
        m_t = jnp.maximum(inter, m_cum + m_run)
        w_in = jnp.exp(dlog - m_t)
        w_st = jnp.exp(inter - m_t)
        a = _dot(qb, kst.astype(BF16)) * w_in
        ci = 2 * hh + d
        cext = c_ref[ci]
        p_state = _dot(qb, cext.astype(BF16))
        p_intra = _dot(a.astype(BF16), v_ext)
        num = w_st * p_state[:, :D] + p_intra[:, :D]
        den = w_st * p_state[:, D:] + p_intra[:, D:]
        h_ref[ci, pl.ds(r0, L), :] = num / jnp.maximum(jnp.abs(den), jnp.exp(-m_t))
        m_new = jnp.maximum(tot + m, amax + tot)
        ws = jnp.exp(a_row + tot - m_new)
        wc = jnp.exp(tot + m - m_new)
        c_ref[ci] = jnp.concatenate([wc, wc], axis=1) * cext + _dot((kst * ws).astype(BF16), v_ext)
        return m_new

    def body(i, carry):
        return tuple(step(i if d == 0 else nc - 1 - i, hh, d, m) for (hh, d), m in zip(chains, carry))

    m_fin = lax.fori_loop(0, nc, body, m_init)

    for hh in range(hb):
        hm = h_ref[2 * hh] + h_ref[2 * hh + 1]
        hl = slice(hh * L, (hh + 1) * L)
        out_ref[:, hl] = _rms(hm, onorm_ref[hh]) * jax.nn.sigmoid(og_ref[:, hl])

    if emit_state:
        for k, (hh, d) in enumerate(chains):
            cext = c_ref[2 * hh + d]
            cT_ref[0, d, hh] = cext[:, :D]
            nT_ref[0, hh, d:d + 1, :] = cext[:, D:].T[0:1, :]
            mT_ref[0, hh, d:d + 1, :] = m_fin[k]


def _mlstm(ob, okt, brow, onorm, tri, *, n, nseq, init=None, emit_state=False):
    L = MLSTM_CHUNK
    nc = n // L
    H = B_HEADS
    hb = MLSTM_HEADS_PER_STEP
    ng = H // hb
    col = lambda part: (lambda b, g: (b, part * ng + g))
    gate_blk0 = B_WIDTH // (16 * hb)
    in_specs = [pl.BlockSpec((n, hb * L), col(0)), pl.BlockSpec((n, hb * L), col(1)),
                pl.BlockSpec((n, hb * L), col(2)),
                pl.BlockSpec((nc, hb * L, L), lambda b, g: (b, g, 0)),
                pl.BlockSpec((nc, 16 * hb, L), lambda b, g: (b, gate_blk0 + g, 0)),
                pl.BlockSpec((hb, 16, L), lambda b, g: (g, 0, 0)),
                pl.BlockSpec((hb, 1, L), lambda b, g: (g, 0, 0)),
                pl.BlockSpec(tri.shape, lambda b, g: (0, 0, 0))]
    args = [ob, ob, ob, okt, okt, brow, onorm, tri]
    if init is not None:
        in_specs += [pl.BlockSpec((1, 2, hb, L, 2 * L), lambda b, g: (b, 0, g, 0, 0)),
                     pl.BlockSpec((1, hb, 2, L), lambda b, g: (b, g, 0, 0))]
        args += list(init)
    out_specs = [pl.BlockSpec((n, hb * L), lambda b, g: (b, g))]
    out_shape = [jax.ShapeDtypeStruct((nseq * n, B_WIDTH), F32)]
    if emit_state:
        out_specs += [pl.BlockSpec((1, 2, hb, L, L), lambda b, g: (b, 0, g, 0, 0)),
                      pl.BlockSpec((1, hb, 2, L), lambda b, g: (b, g, 0, 0)),
                      pl.BlockSpec((1, hb, 2, L), lambda b, g: (b, g, 0, 0))]
        out_shape += [jax.ShapeDtypeStruct((nseq, 2, H, L, L), F32),
                      jax.ShapeDtypeStruct((nseq, H, 2, L), F32),
                      jax.ShapeDtypeStruct((nseq, H, 2, L), F32)]
    return pl.pallas_call(
        functools.partial(_mlstm_kernel, n=n, has_init=init is not None, emit_state=emit_state),
        grid=(nseq, ng),
        in_specs=in_specs,
        out_specs=out_specs,
        out_shape=out_shape,
        scratch_shapes=[pltpu.VMEM((2 * hb, n, L), F32),
                        pltpu.VMEM((hb, 5, nc, 8, L), F32),
                        pltpu.VMEM((2 * hb, L, 2 * L), F32)],
        compiler_params=_cparams(("arbitrary", "arbitrary")),
        name="mlstm_init" if init is not None else "mlstm",
    )(*args)


def _router(logits):
    lane = _lane(logits.shape).astype(F32)
    big = 1e9
    gl = jnp.where(lane < N_GROUPS, logits, -jnp.inf)
    gmax = gl.max(axis=-1, keepdims=True)
    g_sel = jnp.where(gl == gmax, lane, big).min(axis=-1, keepdims=True)
    g_prob = 1.0 / jnp.exp(gl - gmax).sum(axis=-1, keepdims=True)
    lo = N_GROUPS + EXPERTS_PER_GROUP * g_sel
    el = jnp.where(lane >= lo, jnp.where(lane < lo + EXPERTS_PER_GROUP, logits, -jnp.inf), -jnp.inf)
    v1 = el.max(axis=-1, keepdims=True)
    i1 = jnp.where(el == v1, lane, big).min(axis=-1, keepdims=True)
    el2 = jnp.where(lane == i1, -jnp.inf, el)
    v2 = el2.max(axis=-1, keepdims=True)
    i2 = jnp.where(el2 == v2, lane, big).min(axis=-1, keepdims=True)
    e2 = jnp.exp(v2 - v1)
    w1 = g_prob / (1.0 + e2)
    w2 = g_prob * e2 / (1.0 + e2)
    return i1, i2, w1, w2


def _read_tokens(refs, is_prompt):
    if len(refs) == 1:
        return refs[0][...]
    return jnp.where(is_prompt, refs[0][...], refs[1][...])


def _post_kernel(*refs, groups):
    it = iter(refs)
    tok_refs = [[next(it) for _ in range(n)] for n in groups]
    w_refs = [next(it) for _ in range(len(groups) - 1)]
    mod_ref, g_ref, wr_ref, br_ref, ls_ref, sel_ref = [next(it) for _ in range(6)]
    xnew_ref, xn_ref, rt_ref, rtt_ref, cnt_ref = [next(it) for _ in range(5)]
    run_ref = next(it)
    i = pl.program_id(0)
    tm = xnew_ref.shape[0]
    is_prompt = i * tm < T_PROMPT
    r = _mod_row(i * tm)
    acc = None
    for a_refs, w_ref in zip(tok_refs[1:], w_refs):
        d = _dot(_read_tokens(a_refs, is_prompt).astype(BF16), w_ref[...])
        acc = d if acc is None else acc + d
    xnew = _read_tokens(tok_refs[0], is_prompt) + _mod_part(mod_ref, r, 2) * acc
    xnew_ref[...] = xnew
    xn = _rms(xnew, g_ref[...]) * (1.0 + _mod_part(mod_ref, r, 4)) + _mod_part(mod_ref, r, 3)
    _to_token_tiles(xn_ref, xn)
    x_hi, x_lo = _split2(xn)
    both = _dot(x_hi, wr_ref[...])
    logits = both[:, :LANES] + both[:, LANES:] + _dot(x_lo, wr_ref[:, :LANES]) + br_ref[...]
    i1, i2, w1, w2 = _router(logits)

    @pl.when(i == 0)
    def _():
        run_ref[...] = jnp.zeros(run_ref.shape, F32)

    lane = _lane(logits.shape).astype(F32)
    member = jnp.where(lane == i1, 1.0, jnp.where(lane == i2, 1.0, 0.0))
    before = _dot(ls_ref[...], member.astype(BF16)) + run_ref[...]
    rank1 = jnp.where(lane == i1, before, 0.0).sum(axis=-1, keepdims=True)
    rank2 = jnp.where(lane == i2, before, 0.0).sum(axis=-1, keepdims=True)
    run_ref[...] = run_ref[...] + member.sum(axis=0, keepdims=True)
    cnt_ref[...] = run_ref[...]
    cols = (i1 - N_GROUPS, i2 - N_GROUPS, w1, w2, rank1, rank2)
    rt = jnp.zeros(logits.shape, F32)
    for k, c in enumerate(cols):
        rt = jnp.where(lane == k, c, rt)
    rt_ref[...] = rt
    hi, mid, lo = _split3(rt)
    sel = sel_ref[...]
    rtt_ref[...] = _dot_nt(sel, hi) + _dot_nt(sel, mid) + _dot_nt(sel, lo)


def _post(tok_ops, w_list, mod, g, wr, br):
    t = T_ALL
    tm = TOK_BLOCK
    npb = T_PROMPT // tm
    full = lambda a: pl.BlockSpec(a.shape, lambda i: (0,) * a.ndim)
    specs, args, groups = [], [], []
    for op in tok_ops:
        if isinstance(op, tuple):
            w = op[0].shape[1]
            specs += [pl.BlockSpec((tm, w), lambda i: (jnp.minimum(i, npb - 1), 0)),
                      pl.BlockSpec((tm, w), lambda i: (jnp.maximum(i - npb, 0), 0))]
            args += list(op)
            groups.append(2)
        else:
            specs.append(pl.BlockSpec((tm, op.shape[1]), lambda i: (i, 0)))
            args.append(op)
            groups.append(1)
    idx = np.arange(tm)
    ls = jnp.asarray(idx[:, None] > idx[None, :], BF16)
    sel = jnp.asarray(np.arange(8)[:, None] == np.arange(LANES)[None, :], BF16)
    consts = [mod, g, wr, br, ls, sel]
    return pl.pallas_call(
        functools.partial(_post_kernel, groups=tuple(groups)),
        grid=(t // tm,),
        in_specs=specs + [full(w) for w in w_list] + [full(a) for a in consts],
        out_specs=[pl.BlockSpec((tm, D_MODEL), lambda i: (i, 0)),
                   pl.BlockSpec((tm * SLABS, LANES), lambda i: (i, 0)),
                   pl.BlockSpec((tm, LANES), lambda i: (i, 0)),
                   pl.BlockSpec((8, tm), lambda i: (0, i)),
                   pl.BlockSpec((1, LANES), lambda i: (0, 0))],
        out_shape=[jax.ShapeDtypeStruct((t, D_MODEL), F32),
                   jax.ShapeDtypeStruct((t * SLABS, LANES), F32),
                   jax.ShapeDtypeStruct((t, LANES), F32),
                   jax.ShapeDtypeStruct((8, t), F32),
                   jax.ShapeDtypeStruct((1, LANES), F32)],
        scratch_shapes=[pltpu.VMEM((1, LANES), F32)],
        compiler_params=_cparams(("arbitrary",)),
        name="post_mixer_router",
    )(*args, *w_list, *consts)


def _expert_kernel(be_ref, st_ref, xn_hbm, w1_ref, w3_ref, w2_ref, o_ref, xa, xb, sem, w1b, w3b, w2b):
    i = pl.program_id(0)
    nblk = pl.num_programs(0)
    rows = EXPERT_ROWS

    def gather(blk, buf, s):
        for r in range(rows):
            tok = st_ref[blk * rows + r]
            pltpu.make_async_copy(xn_hbm.at[pl.ds(pl.multiple_of(tok * SLABS, SLABS), SLABS), :],
                                  buf.at[pl.ds(r * SLABS, SLABS), :], sem.at[s]).start()

    def wait(buf, s):
        pltpu.make_async_copy(xn_hbm.at[pl.ds(0, rows * SLABS), :], buf, sem.at[s]).wait()

    @pl.when(i == 0)
    def _():
        gather(0, xa, 0)

    changed = jnp.logical_or(i == 0, be_ref[i] != be_ref[jnp.maximum(i - 1, 0)])

    @pl.when(changed)
    def _():
        w1b[...] = w1_ref[0, 0].astype(BF16)
        w3b[...] = w3_ref[0, 0].astype(BF16)
        w2b[...] = w2_ref[0, 0].astype(BF16)

    nxt = jnp.minimum(i + 1, nblk - 1)

    def step(cur, s_cur, oth, s_oth):
        wait(cur, s_cur)
        gather(nxt, oth, s_oth)
        x = _from_token_tiles(cur, 0, rows).astype(BF16)
        h1 = _dot(x, w1b[...])
        h3 = _dot(x, w3b[...])
        hid = (h1 * jax.nn.sigmoid(h1)) * h3
        _to_token_tiles(o_ref, _dot(hid.astype(BF16), w2b[...]))

    @pl.when(i % 2 == 0)
    def _():
        step(xa, 0, xb, 1)

    @pl.when(i % 2 == 1)
    def _():
        step(xb, 1, xa, 0)

    @pl.when(i == nblk - 1)
    def _():
        @pl.when(i % 2 == 0)
        def _():
            wait(xb, 1)

        @pl.when(i % 2 == 1)
        def _():
            wait(xa, 0)


def _experts(block_expert, slot_tok, xn, w1, w3, w2, layer):
    nblk = block_expert.shape[0]
    rows = EXPERT_ROWS
    grid_spec = pltpu.PrefetchScalarGridSpec(
        num_scalar_prefetch=2,
        grid=(nblk,),
        in_specs=[pl.BlockSpec(memory_space=pl.ANY),
                  pl.BlockSpec((1, 1, D_MODEL, D_EXPERT), lambda i, be, st: (layer, be[i], 0, 0)),
                  pl.BlockSpec((1, 1, D_MODEL, D_EXPERT), lambda i, be, st: (layer, be[i], 0, 0)),
                  pl.BlockSpec((1, 1, D_EXPERT, D_MODEL), lambda i, be, st: (layer, be[i], 0, 0))],
        out_specs=pl.BlockSpec((rows * SLABS, LANES), lambda i, be, st: (i, 0)),
        scratch_shapes=[pltpu.VMEM((rows * SLABS, LANES), F32),
                        pltpu.VMEM((rows * SLABS, LANES), F32),
                        pltpu.SemaphoreType.DMA((2,)),
                        pltpu.VMEM((D_MODEL, D_EXPERT), BF16),
                        pltpu.VMEM((D_MODEL, D_EXPERT), BF16),
                        pltpu.VMEM((D_EXPERT, D_MODEL), BF16)])
    return pl.pallas_call(
        _expert_kernel,
        grid_spec=grid_spec,
        out_shape=jax.ShapeDtypeStruct((nblk * rows * SLABS, LANES), F32),
        compiler_params=_cparams(("arbitrary",)),
        name="moe_experts",
    )(block_expert, slot_tok, xn, w1, w3, w2)


_COMBINE_BLOCK = 128


def _combine_kernel(dest_ref, yb_hbm, x_ref, rt_ref, mod_ref, gfin_ref, o_ref, ybuf, sem, *, final_norm, row0):
    i = pl.program_id(0)
    nblk = pl.num_programs(0)
    slot = i % 2
    tm = _COMBINE_BLOCK

    def issue(blk, s):
        def body(j, carry):
            for c in range(2):
                d = dest_ref[c * T_ALL + row0 + blk * tm + j]
                pltpu.make_async_copy(yb_hbm.at[pl.ds(pl.multiple_of(d * SLABS, SLABS), SLABS), :],
                                      ybuf.at[s, pl.ds(pl.multiple_of((c * tm + j) * SLABS, SLABS), SLABS), :],
                                      sem.at[s]).start()
            return carry
        lax.fori_loop(0, tm, body, 0, unroll=4)

    @pl.when(i == 0)
    def _():
        issue(0, 0)

    @pl.when(i + 1 < nblk)
    def _():
        issue(i + 1, 1 - slot)

    pltpu.make_async_copy(yb_hbm.at[pl.ds(0, 2 * tm * SLABS), :], ybuf.at[slot], sem.at[slot]).wait()
    r = _mod_row(row0 + i * tm)
    rt = rt_ref[...]
    yv = ybuf.at[slot]
    y = rt[:, 2:3] * _from_token_tiles(yv, 0, tm) + rt[:, 3:4] * _from_token_tiles(yv, tm, tm)
    out = x_ref[...] + _mod_part(mod_ref, r, 5) * y
    if final_norm:
        out = _rms(out, gfin_ref[...])
    o_ref[...] = out


def _combine(dest, yb, x, rt, mod, gfin, final_norm, row0=0, t=T_ALL):
    tm = _COMBINE_BLOCK
    blk0 = row0 // tm
    grid_spec = pltpu.PrefetchScalarGridSpec(
        num_scalar_prefetch=1,
        grid=(t // tm,),
        in_specs=[pl.BlockSpec(memory_space=pl.ANY),
                  pl.BlockSpec((tm, D_MODEL), lambda i, d: (blk0 + i, 0)),
                  pl.BlockSpec((tm, LANES), lambda i, d: (blk0 + i, 0)),
                  pl.BlockSpec(mod.shape, lambda i, d: (0, 0)),
                  pl.BlockSpec(gfin.shape, lambda i, d: (0, 0))],
        out_specs=pl.BlockSpec((tm, D_MODEL), lambda i, d: (i, 0)),
        scratch_shapes=[pltpu.VMEM((2, 2 * tm * SLABS, LANES), F32),
                        pltpu.SemaphoreType.DMA((2,))])
    return pl.pallas_call(
        functools.partial(_combine_kernel, final_norm=final_norm, row0=row0),
        grid_spec=grid_spec,
        out_shape=jax.ShapeDtypeStruct((t, D_MODEL), F32),
        compiler_params=_cparams(("arbitrary",)),
        name="moe_combine",
    )(dest, yb, x, rt, mod, gfin)


def _combine_proj_kernel(dest_ref, yb_hbm, x_ref, rt_ref, mod0_ref, mod1_ref, g_ref, w_ref,
                         xo_ref, q_ref, k_ref, v_ref, ya, yb, sem, *, row0):
    i = pl.program_id(0)
    nblk = pl.num_programs(0)
    tm = x_ref.shape[0]

    def gather(blk, buf, s):
        for j in range(tm):
            for c in range(2):
                d = dest_ref[c * T_ALL + row0 + blk * tm + j]
                pltpu.make_async_copy(yb_hbm.at[pl.ds(pl.multiple_of(d * SLABS, SLABS), SLABS), :],
                                      buf.at[pl.ds((c * tm + j) * SLABS, SLABS), :], sem.at[s]).start()

    def wait(buf, s):
        pltpu.make_async_copy(yb_hbm.at[pl.ds(0, 2 * tm * SLABS), :], buf, sem.at[s]).wait()

    @pl.when(i == 0)
    def _():
        gather(0, ya, 0)

    nxt = jnp.minimum(i + 1, nblk - 1)

    def step(cur, s_cur, oth, s_oth):
        wait(cur, s_cur)
        gather(nxt, oth, s_oth)
        r = _mod_row(row0 + i * tm)
        rt = rt_ref[...]
        y = rt[:, 2:3] * _from_token_tiles(cur, 0, tm) + rt[:, 3:4] * _from_token_tiles(cur, tm, tm)
        x = x_ref[...] + _mod_part(mod0_ref, r, 5) * y
        xo_ref[...] = x
        h = _rms(x, g_ref[...]) * (1.0 + _mod_part(mod1_ref, r, 1)) + _mod_part(mod1_ref, r, 0)
        hb = h.astype(BF16)
        for j, o_ref in enumerate((q_ref, k_ref, v_ref)):
            o_ref[...] = _dot(hb, w_ref[:, j * C_WIDTH:(j + 1) * C_WIDTH])

    @pl.when(i % 2 == 0)
    def _():
        step(ya, 0, yb, 1)

    @pl.when(i % 2 == 1)
    def _():
        step(yb, 1, ya, 0)

    @pl.when(i == nblk - 1)
    def _():
        @pl.when(i % 2 == 0)
        def _():
            wait(yb, 1)

        @pl.when(i % 2 == 1)
        def _():
            wait(ya, 0)


def _combine_proj(dest, yb, x, rt, mod0, mod1, g, w, row0, t):
    tm = TOK_BLOCK
    blk0 = row0 // tm
    full = lambda a: pl.BlockSpec(a.shape, lambda i, d: (0,) * a.ndim)
    grid_spec = pltpu.PrefetchScalarGridSpec(
        num_scalar_prefetch=1,
        grid=(t // tm,),
        in_specs=[pl.BlockSpec(memory_space=pl.ANY),
                  pl.BlockSpec((tm, D_MODEL), lambda i, d: (blk0 + i, 0)),
                  pl.BlockSpec((tm, LANES), lambda i, d: (blk0 + i, 0)),
                  full(mod0), full(mod1), full(g), full(w)],
        out_specs=[pl.BlockSpec((tm, D_MODEL), lambda i, d: (i, 0))] * 4,
        scratch_shapes=[pltpu.VMEM((2 * tm * SLABS, LANES), F32),
                        pltpu.VMEM((2 * tm * SLABS, LANES), F32),
                        pltpu.SemaphoreType.DMA((2,))])
    return pl.pallas_call(
        functools.partial(_combine_proj_kernel, row0=row0),
        grid_spec=grid_spec,
        out_shape=[jax.ShapeDtypeStruct((t, D_MODEL), F32)] * 4,
        compiler_params=_cparams(("arbitrary",)),
        name="moe_combine_proj",
    )(dest, yb, x, rt, mod0, mod1, g, w)


def _moe_plan(rtt, cnt):
    t = rtt.shape[1]
    eid = rtt[0:2].astype(jnp.int32)
    rank = rtt[4:6].astype(jnp.int32)
    counts = cnt[0, N_GROUPS:N_GROUPS + N_EXPERTS].astype(jnp.int32)
    padded = (counts + EXPERT_ROWS - 1) // EXPERT_ROWS * EXPERT_ROWS
    seg_end = jnp.cumsum(padded)
    seg_start = seg_end - padded
    experts = jnp.arange(N_EXPERTS, dtype=jnp.int32)
    start = jnp.sum(jnp.where(eid[..., None] == experts, seg_start, 0), axis=-1)
    dest = (start + rank).reshape(-1)
    n_blocks = (2 * t + N_EXPERTS * (EXPERT_ROWS - 1) + EXPERT_ROWS - 1) // EXPERT_ROWS
    tok = jnp.tile(jnp.arange(t, dtype=jnp.int32), 2)
    slot_tok = (jnp.arange(n_blocks * EXPERT_ROWS, dtype=jnp.int32) % t).at[dest].set(tok)
    first_row = jnp.arange(n_blocks, dtype=jnp.int32) * EXPERT_ROWS
    block_expert = jnp.minimum(jnp.sum((seg_end[None, :] <= first_row[:, None]).astype(jnp.int32), axis=1),
                               N_EXPERTS - 1)
    return dest, slot_tok, block_expert


def _attn_c_prompt_kernel(q_ref, k_ref, v_ref, o_ref):
    for p in range(C_HEADS // 2):
        sl = slice(p * LANES, (p + 1) * LANES)
        kb = k_ref[:, sl].astype(BF16)
        vb = v_ref[:, sl].astype(BF16)
        qp = q_ref[:, sl] * (C_HEAD_DIM ** -0.5)
        o_ref[:, sl] = _softmax_pair(qp, lambda qm: [_dot_nt(qm, kb)], lambda ps: _dot(ps[0], vb))


def _attn_c_prompt(q, k, v):
    blk = pl.BlockSpec((SEQ, C_WIDTH), lambda b: (b, 0))
    return pl.pallas_call(
        _attn_c_prompt_kernel,
        grid=(BATCH,),
        in_specs=[blk, blk, blk],
        out_specs=blk,
        out_shape=jax.ShapeDtypeStruct((T_PROMPT, C_WIDTH), F32),
        compiler_params=_cparams(("arbitrary",)),
        name="attn_c_prompt",
    )(q, k, v)


def _na_key_start(r0):
    rows = DEC_SEQ // GRID_W
    return jnp.minimum(jnp.clip(r0 - NA_ROWS // 2, 0, rows - NA_ROWS), rows - NA_KROWS)


def _na_block_plan():
    rows = DEC_SEQ // GRID_W
    nblk = rows // NA_QROWS
    plan = []
    for blk in (0, 1, nblk - 1):
        r0 = blk * NA_QROWS
        ks = min(int(np.clip(r0 - NA_ROWS // 2, 0, rows - NA_ROWS)), rows - NA_KROWS)
        per_row = []
        for i in range(NA_QROWS):
            r = r0 + i
            rs = int(np.clip(r - NA_ROWS // 2, 0, rows - NA_ROWS))
            start = ks - r + NA_ROWS - 1 + NA_KROWS
            ok = [rs <= ks + j < rs + NA_ROWS for j in range(NA_KROWS)]
            per_row.append((start, ok))
        plan.append(per_row)
    return plan


def _attn_c_sample_kernel(q_ref, k_ref, v_ref, ck_ref, cv_ref, toe_ref, o_ref, bias_ref):
    rows = DEC_SEQ // GRID_W
    nblk = rows // NA_QROWS
    w = GRID_W

    @pl.when(jnp.logical_and(pl.program_id(1) == 0, pl.program_id(2) == 0))
    def _():
        neg = jnp.full((w, w), NEG, F32)
        for t, per_row in enumerate(_na_block_plan()):
            for half in range(2):
                for i, (start, ok) in enumerate(per_row):
                    for j in range(0, NA_KROWS, 2):
                        pieces = [toe_ref[0, half, start + jj] if ok[jj] else neg for jj in (j, j + 1)]
                        bias_ref[t, half, i * w:(i + 1) * w, j * w:(j + 2) * w] = jnp.concatenate(pieces, axis=1)

    i = pl.program_id(2)
    r0 = i * NA_QROWS
    k0 = pl.multiple_of(_na_key_start(r0) * GRID_W, GRID_W)
    btype = jnp.where(i == 0, 0, jnp.where(i == nblk - 1, 2, 1))
    nk = NA_KROWS * GRID_W
    kw = k_ref[pl.ds(k0, nk), :].astype(BF16)
    vw = v_ref[pl.ds(k0, nk), :].astype(BF16)
    kc = ck_ref[0].astype(BF16)
    vc = cv_ref[0].astype(BF16)
    qp = q_ref[...] * (C_HEAD_DIM ** -0.5)
    lo = _lane(qp.shape) < 64
    outs = []
    for half in range(2):
        qm = jnp.where(lo if half == 0 else jnp.logical_not(lo), qp, 0.0).astype(BF16)
        s_win = _dot_nt(qm, kw) + bias_ref[btype, half]
        s_ctx = _dot_nt(qm, kc)
        m = jnp.maximum(s_win.max(axis=-1, keepdims=True), s_ctx.max(axis=-1, keepdims=True))
        e_win = jnp.exp(s_win - m)
        e_ctx = jnp.exp(s_ctx - m)
        l = e_win.sum(axis=-1, keepdims=True) + e_ctx.sum(axis=-1, keepdims=True)
        o = _dot(e_win.astype(BF16), vw) + _dot(e_ctx.astype(BF16), vc)
        outs.append(o / l)
    o_ref[...] = jnp.where(lo, outs[0], outs[1])


def _attn_c_sample(q, k, v, ck, cv, toe):
    rows = DEC_SEQ // GRID_W
    nblk = rows // NA_QROWS
    qrows = NA_QROWS * GRID_W
    npair = C_HEADS // 2
    return pl.pallas_call(
        _attn_c_sample_kernel,
        grid=(npair, DEC_BATCH, nblk),
        in_specs=[pl.BlockSpec((qrows, LANES), lambda p, b, i: (b * nblk + i, p)),
                  pl.BlockSpec((DEC_SEQ, LANES), lambda p, b, i: (b, p)),
                  pl.BlockSpec((DEC_SEQ, LANES), lambda p, b, i: (b, p)),
                  pl.BlockSpec((1, PAST_LEN, LANES), lambda p, b, i: (b, 0, p)),
                  pl.BlockSpec((1, PAST_LEN, LANES), lambda p, b, i: (b, 0, p)),
                  pl.BlockSpec((1,) + toe.shape[1:], lambda p, b, i: (p, 0, 0, 0, 0))],
        out_specs=pl.BlockSpec((qrows, LANES), lambda p, b, i: (b * nblk + i, p)),
        out_shape=jax.ShapeDtypeStruct((T_SAMPLE, C_WIDTH), F32),
        scratch_shapes=[pltpu.VMEM((3, 2, qrows, NA_KROWS * GRID_W), F32)],
        compiler_params=_cparams(("arbitrary", "arbitrary", "arbitrary")),
        name="attn_c_sample",
    )(q, k, v, ck, cv, toe)


def _na_toeplitz(rpb):
    w = GRID_W
    nd_r, nd_c = 2 * NA_ROWS - 1, 2 * NA_COLS - 1
    c = np.arange(w)
    cs = np.clip(c - NA_COLS // 2, 0, w - NA_COLS)
    col_ok = (c[None, :] >= cs[:, None]) & (c[None, :] < cs[:, None] + NA_COLS)
    dcol = c[None, :] - c[:, None] + NA_COLS - 1
    onehot = (np.arange(nd_c)[:, None, None] == dcol[None]).reshape(nd_c, w * w)
    toe = jnp.dot(rpb.reshape(C_HEADS * nd_r, nd_c), jnp.asarray(onehot, F32), precision=lax.Precision.HIGHEST)
    toe = jnp.where(col_ok[None, None], toe.reshape(C_HEADS, nd_r, w, w), NEG)
    toe = jnp.pad(toe, ((0, 0), (NA_KROWS, NA_KROWS), (0, 0), (0, 0)), constant_values=NEG)
    return toe.reshape(C_HEADS // 2, 2, nd_r + 2 * NA_KROWS, w, w)


def _rope_tables():
    half = A_HEAD_DIM // 2
    t = jnp.arange(DEC_SEQ)
    row = (t // GRID_W).astype(F32)
    colp = (t % GRID_W).astype(F32)
    freqs = 1.0 / (ROPE_BASE ** (jnp.arange(0, half, 2, dtype=F32) / half))
    d = np.arange(LANES) % A_HEAD_DIM
    pos = jnp.where(jnp.asarray(d < half)[None, :], row[:, None], colp[:, None])
    ang = pos * freqs[d % (half // 2)][None, :]
    sign = jnp.asarray(np.where((d % half) < half // 2, -1.0, 1.0), F32)[None, :]
    return jnp.cos(ang), jnp.sin(ang) * sign


def _head_avg_matrix():
    idx = np.arange(LANES) // A_HEAD_DIM
    return jnp.asarray((idx[:, None] == idx[None, :]).astype(np.float32) / A_HEAD_DIM, BF16)


def _tri_matrices():
    i = np.arange(MLSTM_CHUNK)
    upper = (i[:, None] <= i[None, :]).astype(np.float32)
    lower = (i[:, None] >= i[None, :]).astype(np.float32)
    return jnp.asarray(np.stack([upper, lower]), BF16)


def _router_weights(wg, bg, we, be):
    w = jnp.zeros((D_MODEL, LANES), F32).at[:, :N_GROUPS].set(wg).at[:, N_GROUPS:N_GROUPS + N_EXPERTS].set(we)
    b = jnp.zeros((1, LANES), F32).at[0, :N_GROUPS].set(bg).at[0, N_GROUPS:N_GROUPS + N_EXPERTS].set(be)
    hi = w.astype(BF16)
    lo = (w - hi.astype(F32)).astype(BF16)
    return jnp.concatenate([hi, lo], axis=1), b


def _moe(xn, rtt, cnt, w1, w3, w2, layer):
    dest, slot_tok, block_expert = _moe_plan(rtt, cnt)
    return dest, _experts(block_expert, slot_tok, xn, w1, w3, w2, layer)


def kernel(x_prompt, x_sample, cache_attn_k, cache_attn_v, state_mlstm_C, state_mlstm_n, state_mlstm_m,
           cache_na_k, cache_na_v, c, c_ctx, norm_mix, norm_ffn, norm_final, ada_w, ada_b,
           ab_w_in, ab_w_out, ab_q_norm, ab_k_norm, ab_gate_bias, ab_out_norm,
           na_w_in, na_w_out, na_rpb, moe_wg, moe_bg, moe_we, moe_be, moe_w1, moe_w3, moe_w2):
    xp = x_prompt.reshape(T_PROMPT, D_MODEL)
    xs = x_sample.reshape(T_SAMPLE, D_MODEL)
    cond =jnp.zeros((N_COND, D_MODEL), F32).at[0].set(c_ctx).at[1:1 + DEC_BATCH].set(c)
    mod = _modulation(cond, ada_w, ada_b)
    gfin = norm_final.reshape(1, D_MODEL)

    w_in = ab_w_in[0]
    o_aq, o_ak, o_av, o_bq, o_bk, o_bv, o_bo, o_bg = np.cumsum((0,) + (A_WIDTH, A_KV_WIDTH, A_KV_WIDTH,
                                                                       B_WIDTH, B_WIDTH, B_WIDTH, B_WIDTH))
    wb = jnp.concatenate([w_in[:, o_bq:o_bk], w_in[:, o_bv:o_bg]], axis=1).astype(BF16)
    wq = w_in[:, o_aq:o_ak].astype(BF16)
    wkv = w_in[:, o_ak:o_bq].astype(BF16)
    gate_rows = np.array([0, 8, 1, 9])
    wg = w_in[:, o_bg:o_bg + 4 * B_HEADS].reshape(D_MODEL, 4, B_HEADS)
    wgt = jnp.zeros((B_HEADS, 16, D_MODEL), F32).at[:, gate_rows, :].set(wg.transpose(2, 1, 0))
    wgt = jnp.concatenate([w_in[:, o_bk:o_bv].T, wgt.reshape(16 * B_HEADS, D_MODEL)], axis=0).astype(BF16)
    g_mix = norm_mix[0].reshape(1, D_MODEL)
    ob_p, oq_p, okv_p, ogt_p = _proj_ab(xp, mod[0], g_mix, wb, wq, wkv, wgt, 0)
    ob_s, oq_s, okv_s, ogt_s = _proj_ab(xs, mod[0], g_mix, wb, wq, wkv, wgt, T_PROMPT)

    qn = jnp.tile(ab_q_norm[0], 2).reshape(1, LANES)
    kn = jnp.tile(ab_k_norm[0], 2).reshape(1, LANES)
    bd = _head_avg_matrix()
    cos, sin = _rope_tables()
    a_p, new_k, new_v = _attn_a_prompt(oq_p, okv_p, qn, kn, bd)
    ck = cache_attn_k[:, 0].reshape(DEC_BATCH, PAST_LEN, A_KV_WIDTH)
    cv = cache_attn_v[:, 0].reshape(DEC_BATCH, PAST_LEN, A_KV_WIDTH)
    a_s = _attn_a_sample(oq_s, okv_s, ck, cv, cos, sin, qn, kn, bd)

    gb = ab_gate_bias[0]
    brow = jnp.zeros((B_HEADS, 16, LANES), F32).at[:, gate_rows, :].set(
        jnp.broadcast_to(gb.T[:, :, None], (B_HEADS, 4, LANES)))
    onorm = ab_out_norm[0].reshape(B_HEADS, 1, B_HEAD_DIM)
    tri = _tri_matrices()
    b_p, cT, nT, mT = _mlstm(ob_p, ogt_p, brow, onorm, tri, n=SEQ, nseq=BATCH, emit_state=True)
    n0 = jnp.broadcast_to(state_mlstm_n[:, 0][..., None], state_mlstm_C[:, 0].shape)
    c0 = jnp.concatenate([state_mlstm_C[:, 0], n0], axis=-1)
    m0 = jnp.broadcast_to(state_mlstm_m[:, 0].transpose(0, 2, 1)[..., None], (DEC_BATCH, B_HEADS, 2, LANES))
    (b_s,) = _mlstm(ob_s, ogt_s, brow, onorm, tri, n=DEC_SEQ, nseq=DEC_BATCH, init=(c0, m0))

    w_out = ab_w_out[0].astype(BF16)
    wr, br = _router_weights(moe_wg[0], moe_bg[0], moe_we[0], moe_be[0])
    x1, xn, rt, rtt, cnt = _post([(xp, xs), (a_p, a_s), (b_p, b_s)], [w_out[:A_WIDTH], w_out[A_WIDTH:]], mod[0],
                                 norm_ffn[0].reshape(1, D_MODEL), wr, br)
    dest, yb = _moe(xn, rtt, cnt, moe_w1, moe_w3, moe_w2, 0)

    g_mix = norm_mix[1].reshape(1, D_MODEL)
    w_in = na_w_in[0].astype(BF16)
    x_p, q_p, k_p, v_p = _combine_proj(dest, yb, x1, rt, mod[0], mod[1], g_mix, w_in, 0, T_PROMPT)
    x_s, q_s, k_s, v_s = _combine_proj(dest, yb, x1, rt, mod[0], mod[1], g_mix, w_in, T_PROMPT, T_SAMPLE)
    o_p = _attn_c_prompt(q_p, k_p, v_p)
    nck = cache_na_k[:, 0].reshape(DEC_BATCH, PAST_LEN, C_WIDTH)
    ncv = cache_na_v[:, 0].reshape(DEC_BATCH, PAST_LEN, C_WIDTH)
    o_s = _attn_c_sample(q_s, k_s, v_s, nck, ncv, _na_toeplitz(na_rpb[0]))
    wr, br = _router_weights(moe_wg[1], moe_bg[1], moe_we[1], moe_be[1])
    x1, xn, rt, rtt, cnt = _post([(x_p, x_s), (o_p, o_s)], [na_w_out[0].astype(BF16)], mod[1],
                                 norm_ffn[1].reshape(1, D_MODEL), wr, br)
    dest, yb = _moe(xn, rtt, cnt, moe_w1, moe_w3, moe_w2, 1)
    y_prompt = _combine(dest, yb, x1, rt, mod[1], gfin, True, 0, T_PROMPT).reshape(BATCH, SEQ, D_MODEL)
    y_sample = _combine(dest, yb, x1, rt, mod[1], gfin, True, T_PROMPT, T_SAMPLE).reshape(DEC_BATCH, DEC_SEQ, D_MODEL)
    new_attn_k = new_k.reshape(BATCH, 1, SEQ, A_KV_HEADS, A_HEAD_DIM)
    new_attn_v = new_v.reshape(BATCH, 1, SEQ, A_KV_HEADS, A_HEAD_DIM)
    new_mlstm_C = cT[:, None]
    new_mlstm_n = nT.transpose(0, 2, 1, 3)[:, None]
    new_mlstm_m = mT[..., 0].transpose(0, 2, 1)[:, None]
    new_na_k = k_p.reshape(BATCH, 1, SEQ, C_HEADS, C_HEAD_DIM)
    new_na_v = v_p.reshape(BATCH, 1, SEQ, C_HEADS, C_HEAD_DIM)
    return (y_prompt, y_sample, new_attn_k, new_attn_v, new_mlstm_C, new_mlstm_n, new_mlstm_m,
            new_na_k, new_na_v)
```

```python
import functools

import numpy as np
import jax
import jax.numpy as jnp
from jax import lax
from jax.experimental import pallas as pl
from jax.experimental.pallas import tpu as pltpu

F32 = jnp.float32
BF16 = jnp.bfloat16

D_MODEL = 1024
BATCH = 32
SEQ = 256
DEC_BATCH = 4
DEC_SEQ = 2048
PAST_LEN = 256
GRID_W = 64
A_HEADS = 8
A_KV_HEADS = 2
A_HEAD_DIM = 64
A_WIDTH = A_HEADS * A_HEAD_DIM
A_KV_WIDTH = A_KV_HEADS * A_HEAD_DIM
B_HEADS = 4
B_HEAD_DIM = 128
B_WIDTH = B_HEADS * B_HEAD_DIM
MLSTM_CHUNK = 128
C_HEADS = 16
C_HEAD_DIM = 64
C_WIDTH = C_HEADS * C_HEAD_DIM
NA_ROWS = 8
NA_COLS = 16
N_GROUPS = 4
EXPERTS_PER_GROUP = 8
N_EXPERTS = N_GROUPS * EXPERTS_PER_GROUP
D_EXPERT = 512
MOE_BLOCK = 128
ROPE_BASE = 10000.0
NORM_EPS = 1e-6

T_PROMPT = BATCH * SEQ
T_SAMPLE = DEC_BATCH * DEC_SEQ
T_ALL = T_PROMPT + T_SAMPLE
N_COND = 8
LANES = 128
SLABS = D_MODEL // LANES
TOK_BLOCK = 256
MLSTM_HEADS_PER_STEP = 2
EXPERT_ROWS = 512
NA_QROWS = 4
NA_KROWS = 12
NEG = -1e30
VMEM_LIMIT = 56 * 1024 * 1024


def _cparams(sem):
    return pltpu.CompilerParams(dimension_semantics=sem, vmem_limit_bytes=VMEM_LIMIT)


def _split2(x):
    hi = x.astype(BF16)
    lo = (x - hi.astype(F32)).astype(BF16)
    return hi, lo


def _split3(x):
    hi = x.astype(BF16)
    r = x - hi.astype(F32)
    mid = r.astype(BF16)
    lo = (r - mid.astype(F32)).astype(BF16)
    return hi, mid, lo


def _dot(a, b):
    return jnp.dot(a, b, preferred_element_type=F32)


def _dot_nt(a, b):
    return lax.dot_general(a, b, (((1,), (1,)), ((), ())), preferred_element_type=F32)


def _dot_exact_rhs(x, b):
    hi, mid, lo = _split3(x)
    return _dot(hi, b) + _dot(mid, b) + _dot(lo, b)


def _rms(x, g):
    ms = jnp.mean(x * x, axis=-1, keepdims=True)
    return (x * lax.rsqrt(ms + NORM_EPS)) * g


def _mod_row(tok0):
    return jnp.where(tok0 < T_PROMPT, 0, 1 + (tok0 - T_PROMPT) // DEC_SEQ)


def _mod_part(mod_ref, r, idx):
    return mod_ref[pl.ds(r, 1), idx * D_MODEL:(idx + 1) * D_MODEL]


def _head_rms(x, w, bd):
    hi, lo = _split2(x * x)
    ms = _dot(hi, bd) + _dot(lo, bd)
    return (x * lax.rsqrt(ms + NORM_EPS)) * w


def _to_token_tiles(ref, x):
    m = x.shape[0]
    for s in range(SLABS):
        ref[pl.ds(s, m, stride=SLABS), :] = x[:, s * LANES:(s + 1) * LANES]


def _from_token_tiles(ref, tile0, m):
    return jnp.concatenate([ref[pl.ds(tile0 * SLABS + s, m, stride=SLABS), :] for s in range(SLABS)], axis=1)


def _lane(shape):
    return lax.broadcasted_iota(jnp.int32, shape, len(shape) - 1)


def _dup_half(x, g):
    xr = pltpu.roll(x, 64, 1)
    lo = _lane(x.shape) < 64
    return jnp.where(lo, x, xr) if g == 0 else jnp.where(lo, xr, x)


def _rope(x, cos, sin_signed):
    lane = _lane(x.shape)
    partner = jnp.where((lane % 32) < 16, pltpu.roll(x, LANES - 16, 1), pltpu.roll(x, 16, 1))
    return x * cos + partner * sin_signed


def _softmax_pair(qp, score_fn, value_fn):
    lo = _lane(qp.shape) < 64
    outs = []
    for half in range(2):
        qm = jnp.where(lo if half == 0 else jnp.logical_not(lo), qp, 0.0).astype(BF16)
        ss = score_fn(qm)
        m = ss[0].max(axis=-1, keepdims=True)
        for s in ss[1:]:
            m = jnp.maximum(m, s.max(axis=-1, keepdims=True))
        es = [jnp.exp(s - m) for s in ss]
        l = es[0].sum(axis=-1, keepdims=True)
        for e in es[1:]:
            l = l + e.sum(axis=-1, keepdims=True)
        o = value_fn([e.astype(BF16) for e in es])
        outs.append(o / l)
    return jnp.where(lo, outs[0], outs[1])


def _mod_kernel(cond_ref, w_ref, b_ref, o_ref):
    c = cond_ref[...]
    s = c * jax.nn.sigmoid(c)
    s_hi, s_lo = _split2(s)
    w_hi, w_lo = _split2(w_ref[0])
    o_ref[0] = _dot(s_hi, w_hi) + _dot(s_lo, w_hi) + _dot(s_hi, w_lo) + b_ref[0]


def _modulation(cond, ada_w, ada_b):
    depth, d, n = ada_w.shape
    tn = 1536
    return pl.pallas_call(
        _mod_kernel,
        grid=(depth, n // tn),
        in_specs=[pl.BlockSpec((N_COND, d), lambda l, j: (0, 0)),
                  pl.BlockSpec((1, d, tn), lambda l, j: (l, 0, j)),
                  pl.BlockSpec((1, 1, tn), lambda l, j: (l, 0, j))],
        out_specs=pl.BlockSpec((1, N_COND, tn), lambda l, j: (l, 0, j)),
        out_shape=jax.ShapeDtypeStruct((depth, N_COND, n), F32),
        compiler_params=_cparams(("arbitrary", "arbitrary")),
        name="adaln_modulation",
    )(cond, ada_w, ada_b.reshape(depth, 1, n))


def _norm_mod(x_ref, mod_ref, g_ref, shift_idx, scale_idx, row0):
    r = _mod_row(row0 + pl.program_id(0) * x_ref.shape[0])
    h = _rms(x_ref[...], g_ref[...])
    return h * (1.0 + _mod_part(mod_ref, r, scale_idx)) + _mod_part(mod_ref, r, shift_idx)


def _proj_ab_kernel(x_ref, mod_ref, g_ref, wb_ref, wq_ref, wkv_ref, wgt_ref, ob_ref, oq_ref, okv_ref, ogt_ref,
                    *, row0):
    tm = x_ref.shape[0]
    hb = _norm_mod(x_ref, mod_ref, g_ref, 0, 1, row0).astype(BF16)
    ob_ref[...] = _dot(hb, wb_ref[...])
    oq_ref[...] = _dot(hb, wq_ref[...])
    okv_ref[...] = _dot(hb, wkv_ref[...])
    gt = _dot_nt(wgt_ref[...], hb)
    for j in range(tm // LANES):
        ogt_ref[j] = gt[:, j * LANES:(j + 1) * LANES]


def _proj_ab(x, mod, g, wb, wq, wkv, wgt, row0):
    t = x.shape[0]
    tm = TOK_BLOCK
    full = lambda a: pl.BlockSpec(a.shape, lambda i: (0,) * a.ndim)
    return pl.pallas_call(
        functools.partial(_proj_ab_kernel, row0=row0),
        grid=(t // tm,),
        in_specs=[pl.BlockSpec((tm, D_MODEL), lambda i: (i, 0)), full(mod), full(g),
                  full(wb), full(wq), full(wkv), full(wgt)],
        out_specs=[pl.BlockSpec((tm, wb.shape[1]), lambda i: (i, 0)),
                   pl.BlockSpec((tm, A_WIDTH), lambda i: (i, 0)),
                   pl.BlockSpec((tm, 2 * A_KV_WIDTH), lambda i: (i, 0)),
                   pl.BlockSpec((tm // LANES, wgt.shape[0], LANES), lambda i: (i, 0, 0))],
        out_shape=[jax.ShapeDtypeStruct((t, wb.shape[1]), F32),
                   jax.ShapeDtypeStruct((t, A_WIDTH), F32),
                   jax.ShapeDtypeStruct((t, 2 * A_KV_WIDTH), F32),
                   jax.ShapeDtypeStruct((t // LANES, wgt.shape[0], LANES), F32)],
        compiler_params=_cparams(("arbitrary",)),
        name="proj_ab",
    )(x, mod, g, wb, wq, wkv, wgt)


def _gqa_block(q_ref, qn, bd, kd_ref, vd_ref, o_ref, rope=None):
    for p in range(A_HEADS // 2):
        g = p // (A_HEADS // 2 // A_KV_HEADS)
        qp = _head_rms(q_ref[:, p * LANES:(p + 1) * LANES], qn, bd)
        if rope is not None:
            qp = _rope(qp, rope[0], rope[1])
        qp = qp * (A_HEAD_DIM ** -0.5)
        o_ref[:, p * LANES:(p + 1) * LANES] = _softmax_pair(
            qp, lambda qm: [_dot_nt(qm, kd_ref[g])], lambda ps: _dot(ps[0], vd_ref[g]))


def _attn_a_prompt_kernel(q_ref, kv_ref, qn_ref, kn_ref, bd_ref, o_ref, knew_ref, vnew_ref, kd_ref, vd_ref):
    bd = bd_ref[...]
    k = _head_rms(kv_ref[:, :LANES], kn_ref[...], bd)
    v = kv_ref[:, LANES:]
    knew_ref[...] = k
    vnew_ref[...] = v
    for g in range(A_KV_HEADS):
        kd_ref[g] = _dup_half(k, g).astype(BF16)
        vd_ref[g] = _dup_half(v, g).astype(BF16)
    _gqa_block(q_ref, qn_ref[...], bd, kd_ref, vd_ref, o_ref)


def _attn_a_prompt(q, kv, qn, kn, bd):
    nb = BATCH
    full = lambda a: pl.BlockSpec(a.shape, lambda b: (0,) * a.ndim)
    return pl.pallas_call(
        _attn_a_prompt_kernel,
        grid=(nb,),
        in_specs=[pl.BlockSpec((SEQ, A_WIDTH), lambda b: (b, 0)),
                  pl.BlockSpec((SEQ, 2 * A_KV_WIDTH), lambda b: (b, 0)),
                  full(qn), full(kn), full(bd)],
        out_specs=[pl.BlockSpec((SEQ, A_WIDTH), lambda b: (b, 0)),
                   pl.BlockSpec((SEQ, A_KV_WIDTH), lambda b: (b, 0)),
                   pl.BlockSpec((SEQ, A_KV_WIDTH), lambda b: (b, 0))],
        out_shape=[jax.ShapeDtypeStruct((T_PROMPT, A_WIDTH), F32),
                   jax.ShapeDtypeStruct((T_PROMPT, A_KV_WIDTH), F32),
                   jax.ShapeDtypeStruct((T_PROMPT, A_KV_WIDTH), F32)],
        scratch_shapes=[pltpu.VMEM((A_KV_HEADS, SEQ, LANES), BF16),
                        pltpu.VMEM((A_KV_HEADS, SEQ, LANES), BF16)],
        compiler_params=_cparams(("arbitrary",)),
        name="attn_a_prompt",
    )(q, kv, qn, kn, bd)


_A_QBLOCK = 256


def _attn_a_sample_kernel(q_ref, kv_ref, ck_ref, cv_ref, cos_ref, sin_ref, cosq_ref, sinq_ref,
                          qn_ref, kn_ref, bd_ref, o_ref, kd_ref, vd_ref):
    bd = bd_ref[...]

    @pl.when(pl.program_id(1) == 0)
    def _():
        for g in range(A_KV_HEADS):
            kd_ref[g, :PAST_LEN] = _dup_half(ck_ref[0], g).astype(BF16)
            vd_ref[g, :PAST_LEN] = _dup_half(cv_ref[0], g).astype(BF16)
        rows = 256
        for c in range(DEC_SEQ // rows):
            sl = slice(c * rows, (c + 1) * rows)
            k = _head_rms(kv_ref[sl, :LANES], kn_ref[...], bd)
            k = _rope(k, cos_ref[sl, :], sin_ref[sl, :])
            v = kv_ref[sl, LANES:]
            dst = slice(PAST_LEN + c * rows, PAST_LEN + (c + 1) * rows)
            for g in range(A_KV_HEADS):
                kd_ref[g, dst] = _dup_half(k, g).astype(BF16)
                vd_ref[g, dst] = _dup_half(v, g).astype(BF16)

    _gqa_block(q_ref, qn_ref[...], bd, kd_ref, vd_ref, o_ref, rope=(cosq_ref[...], sinq_ref[...]))


def _attn_a_sample(q, kv, ck, cv, cos, sin, qn, kn, bd):
    nq = DEC_SEQ // _A_QBLOCK
    full = lambda a: pl.BlockSpec(a.shape, lambda b, i: (0,) * a.ndim)
    tk = PAST_LEN + DEC_SEQ
    return pl.pallas_call(
        _attn_a_sample_kernel,
        grid=(DEC_BATCH, nq),
        in_specs=[pl.BlockSpec((_A_QBLOCK, A_WIDTH), lambda b, i: (b * nq + i, 0)),
                  pl.BlockSpec((DEC_SEQ, 2 * A_KV_WIDTH), lambda b, i: (b, 0)),
                  pl.BlockSpec((1, PAST_LEN, A_KV_WIDTH), lambda b, i: (b, 0, 0)),
                  pl.BlockSpec((1, PAST_LEN, A_KV_WIDTH), lambda b, i: (b, 0, 0)),
                  full(cos), full(sin),
                  pl.BlockSpec((_A_QBLOCK, LANES), lambda b, i: (i, 0)),
                  pl.BlockSpec((_A_QBLOCK, LANES), lambda b, i: (i, 0)),
                  full(qn), full(kn), full(bd)],
        out_specs=pl.BlockSpec((_A_QBLOCK, A_WIDTH), lambda b, i: (b * nq + i, 0)),
        out_shape=jax.ShapeDtypeStruct((T_SAMPLE, A_WIDTH), F32),
        scratch_shapes=[pltpu.VMEM((A_KV_HEADS, tk, LANES), BF16),
                        pltpu.VMEM((A_KV_HEADS, tk, LANES), BF16)],
        compiler_params=_cparams(("arbitrary", "arbitrary")),
        name="attn_a_sample",
    )(q, kv, ck, cv, cos, sin, cos, sin, qn, kn, bd)


def _log_sigmoid(x):
    return -(jnp.maximum(-x, 0.0) + jnp.log1p(jnp.exp(-jnp.abs(x))))


def _mlstm_kernel(*refs, n, has_init, emit_state):
    it = iter(refs)
    q_ref, v_ref, og_ref, kt_ref, gt_ref, brow_ref, onorm_ref, tri_ref = [next(it) for _ in range(8)]
    if has_init:
        c0_ref, m0_ref = [next(it) for _ in range(2)]
    out_ref = next(it)
    if emit_state:
        cT_ref, nT_ref, mT_ref = [next(it) for _ in range(3)]
    h_ref, row_ref, c_ref = [next(it) for _ in range(3)]

    L = MLSTM_CHUNK
    D = B_HEAD_DIM
    nc = n // L
    hb = MLSTM_HEADS_PER_STEP

    lane = lax.broadcasted_iota(jnp.int32, (nc * 8, L), 1)
    is_fwd = lax.broadcasted_iota(jnp.int32, (nc * 8, L), 0) % 8 == 0

    def running_max(x, suffix):
        for sh in (1, 2, 4, 8, 16, 32, 64):
            if suffix:
                x = jnp.where(lane < L - sh, jnp.maximum(x, pltpu.roll(x, L - sh, 1)), x)
            else:
                x = jnp.where(lane >= sh, jnp.maximum(x, pltpu.roll(x, sh, 1)), x)
        return x

    for hh in range(hb):
        gt = gt_ref[:, 16 * hh:16 * hh + 16, :] + brow_ref[hh][None]
        li = gt[:, 0:8, :].reshape(nc * 8, L)
        lf = _log_sigmoid(gt[:, 8:16, :]).reshape(nc * 8, L)
        cum = jnp.where(is_fwd, _dot_exact_rhs(lf, tri_ref[0]), _dot_exact_rhs(lf, tri_ref[1]))
        a = li - cum
        planes = (cum, a, jnp.broadcast_to(lf.sum(axis=-1, keepdims=True), (nc * 8, L)),
                  jnp.where(is_fwd, running_max(a, False), running_max(a, True)),
                  jnp.broadcast_to(a.max(axis=-1, keepdims=True), (nc * 8, L)))
        for p, val in enumerate(planes):
            row_ref[hh, p] = val.reshape(nc, 8, L)

    chains = [(hh, d) for hh in range(hb) for d in range(2)]
    if has_init:
        m_init = []
        for hh, d in chains:
            c_ref[2 * hh + d] = c0_ref[0, d, hh]
            m_init.append(m0_ref[0, hh, d:d + 1, :])
        m_init = tuple(m_init)
    else:
        c_ref[...] = jnp.zeros(c_ref.shape, F32)
        m_init = tuple(jnp.zeros((1, L), F32) for _ in chains)

    t_idx = lax.broadcasted_iota(jnp.int32, (L, L), 0)
    s_idx = lax.broadcasted_iota(jnp.int32, (L, L), 1)
    masks = (s_idx <= t_idx, s_idx >= t_idx)
    ones = jnp.ones((L, L), F32)

    def step(c, hh, d, m):
        r0 = pl.multiple_of(c * L, L)
        hl = slice(hh * L, (hh + 1) * L)
        qb = q_ref[pl.ds(r0, L), hl].astype(BF16)
        kst = kt_ref[c, hl, :] * (D ** -0.5)
        v_ext = jnp.concatenate([v_ref[pl.ds(r0, L), hl], ones], axis=1).astype(BF16)
        cum, a_row, tot, amax_run, amax = [row_ref[hh, p, c][d:d + 1, :] for p in range(5)]
        m_cum = jnp.broadcast_to(cum, (L, L)).T
        m_run = jnp.broadcast_to(amax_run, (L, L)).T
        dlog = jnp.where(masks[d], m_cum + a_row, -jnp.inf)
        inter = m_cum + m
        m_t = jnp.maximum(inter, m_cum + m_run)
        w_in = jnp.exp(dlog - m_t)
        w_st = jnp.exp(inter - m_t)
        a = _dot(qb, kst.astype(BF16)) * w_in
        ci = 2 * hh + d
        cext = c_ref[ci]
        p_state = _dot(qb, cext.astype(BF16))
        p_intra = _dot(a.astype(BF16), v_ext)
        num = w_st * p_state[:, :D] + p_intra[:, :D]
        den = w_st * p_state[:, D:] + p_intra[:, D:]
        h_ref[ci, pl.ds(r0, L), :] = num / jnp.maximum(jnp.abs(den), jnp.exp(-m_t))
        m_new = jnp.maximum(tot + m, amax + tot)
        ws = jnp.exp(a_row + tot - m_new)
        wc = jnp.exp(tot + m - m_new)
        c_ref[ci] = jnp.concatenate([wc, wc], axis=1) * cext + _dot((kst * ws).astype(BF16), v_ext)
        return m_new

    def body(i, carry):
        return tuple(step(i if d == 0 else nc - 1 - i, hh, d, m) for (hh, d), m in zip(chains, carry))

    m_fin = lax.fori_loop(0, nc, body, m_init)

    for hh in range(hb):
        hm = h_ref[2 * hh] + h_ref[2 * hh + 1]
        hl = slice(hh * L, (hh + 1) * L)
        out_ref[:, hl] = _rms(hm, onorm_ref[hh]) * jax.nn.sigmoid(og_ref[:, hl])

    if emit_state:
        for k, (hh, d) in enumerate(chains):
            cext = c_ref[2 * hh + d]
            cT_ref[0, d, hh] = cext[:, :D]
            nT_ref[0, hh, d:d + 1, :] = cext[:, D:].T[0:1, :]
            mT_ref[0, hh, d:d + 1, :] = m_fin[k]


def _mlstm(ob, okt, brow, onorm, tri, *, n, nseq, init=None, emit_state=False):
    L = MLSTM_CHUNK
    nc = n // L
    H = B_HEADS
    hb = MLSTM_HEADS_PER_STEP
    ng = H // hb
    col = lambda part: (lambda b, g: (b, part * ng + g))
    gate_blk0 = B_WIDTH // (16 * hb)
    in_specs = [pl.BlockSpec((n, hb * L), col(0)), pl.BlockSpec((n, hb * L), col(1)),
                pl.BlockSpec((n, hb * L), col(2)),
                pl.BlockSpec((nc, hb * L, L), lambda b, g: (b, g, 0)),
                pl.BlockSpec((nc, 16 * hb, L), lambda b, g: (b, gate_blk0 + g, 0)),
                pl.BlockSpec((hb, 16, L), lambda b, g: (g, 0, 0)),
                pl.BlockSpec((hb, 1, L), lambda b, g: (g, 0, 0)),
                pl.BlockSpec(tri.shape, lambda b, g: (0, 0, 0))]
    args = [ob, ob, ob, okt, okt, brow, onorm, tri]
    if init is not None:
        in_specs += [pl.BlockSpec((1, 2, hb, L, 2 * L), lambda b, g: (b, 0, g, 0, 0)),
                     pl.BlockSpec((1, hb, 2, L), lambda b, g: (b, g, 0, 0))]
        args += list(init)
    out_specs = [pl.BlockSpec((n, hb * L), lambda b, g: (b, g))]
    out_shape = [jax.ShapeDtypeStruct((nseq * n, B_WIDTH), F32)]
    if emit_state:
        out_specs += [pl.BlockSpec((1, 2, hb, L, L), lambda b, g: (b, 0, g, 0, 0)),
                      pl.BlockSpec((1, hb, 2, L), lambda b, g: (b, g, 0, 0)),
                      pl.BlockSpec((1, hb, 2, L), lambda b, g: (b, g, 0, 0))]
        out_shape += [jax.ShapeDtypeStruct((nseq, 2, H, L, L), F32),
                      jax.ShapeDtypeStruct((nseq, H, 2, L), F32),
                      jax.ShapeDtypeStruct((nseq, H, 2, L), F32)]
    return pl.pallas_call(
        functools.partial(_mlstm_kernel, n=n, has_init=init is not None, emit_state=emit_state),
        grid=(nseq, ng),
        in_specs=in_specs,
        out_specs=out_specs,
        out_shape=out_shape,
        scratch_shapes=[pltpu.VMEM((2 * hb, n, L), F32),
                        pltpu.VMEM((hb, 5, nc, 8, L), F32),
                        pltpu.VMEM((2 * hb, L, 2 * L), F32)],
        compiler_params=_cparams(("arbitrary", "arbitrary")),
        name="mlstm_init" if init is not None else "mlstm",
    )(*args)


def _router(logits):
    lane = _lane(logits.shape).astype(F32)
    big = 1e9
    gl = jnp.where(lane < N_GROUPS, logits, -jnp.inf)
    gmax = gl.max(axis=-1, keepdims=True)
    g_sel = jnp.where(gl == gmax, lane, big).min(axis=-1, keepdims=True)
    g_prob = 1.0 / jnp.exp(gl - gmax).sum(axis=-1, keepdims=True)
    lo = N_GROUPS + EXPERTS_PER_GROUP * g_sel
    el = jnp.where(lane >= lo, jnp.where(lane < lo + EXPERTS_PER_GROUP, logits, -jnp.inf), -jnp.inf)
    v1 = el.max(axis=-1, keepdims=True)
    i1 = jnp.where(el == v1, lane, big).min(axis=-1, keepdims=True)
    el2 = jnp.where(lane == i1, -jnp.inf, el)
    v2 = el2.max(axis=-1, keepdims=True)
    i2 = jnp.where(el2 == v2, lane, big).min(axis=-1, keepdims=True)
    e2 = jnp.exp(v2 - v1)
    w1 = g_prob / (1.0 + e2)
    w2 = g_prob * e2 / (1.0 + e2)
    return i1, i2, w1, w2


def _read_tokens(refs, is_prompt):
    if len(refs) == 1:
        return refs[0][...]
    return jnp.where(is_prompt, refs[0][...], refs[1][...])


def _post_kernel(*refs, groups):
    it = iter(refs)
    tok_refs = [[next(it) for _ in range(n)] for n in groups]
    w_refs = [next(it) for _ in range(len(groups) - 1)]
    mod_ref, g_ref, wr_ref, br_ref, ls_ref, sel_ref = [next(it) for _ in range(6)]
    xnew_ref, xn_ref, rt_ref, rtt_ref, cnt_ref = [next(it) for _ in range(5)]
    run_ref = next(it)
    i = pl.program_id(0)
    tm = xnew_ref.shape[0]
    is_prompt = i * tm < T_PROMPT
    r = _mod_row(i * tm)
    acc = None
    for a_refs, w_ref in zip(tok_refs[1:], w_refs):
        d = _dot(_read_tokens(a_refs, is_prompt).astype(BF16), w_ref[...])
        acc = d if acc is None else acc + d
    xnew = _read_tokens(tok_refs[0], is_prompt) + _mod_part(mod_ref, r, 2) * acc
    xnew_ref[...] = xnew
    xn = _rms(xnew, g_ref[...]) * (1.0 + _mod_part(mod_ref, r, 4)) + _mod_part(mod_ref, r, 3)
    _to_token_tiles(xn_ref, xn)
    x_hi, x_lo = _split2(xn)
    both = _dot(x_hi, wr_ref[...])
    logits = both[:, :LANES] + both[:, LANES:] + _dot(x_lo, wr_ref[:, :LANES]) + br_ref[...]
    i1, i2, w1, w2 = _router(logits)

    @pl.when(i == 0)
    def _():
        run_ref[...] = jnp.zeros(run_ref.shape, F32)

    lane = _lane(logits.shape).astype(F32)
    member = jnp.where(lane == i1, 1.0, jnp.where(lane == i2, 1.0, 0.0))
    before = _dot(ls_ref[...], member.astype(BF16)) + run_ref[...]
    rank1 = jnp.where(lane == i1, before, 0.0).sum(axis=-1, keepdims=True)
    rank2 = jnp.where(lane == i2, before, 0.0).sum(axis=-1, keepdims=True)
    run_ref[...] = run_ref[...] + member.sum(axis=0, keepdims=True)
    cnt_ref[...] = run_ref[...]
    cols = (i1 - N_GROUPS, i2 - N_GROUPS, w1, w2, rank1, rank2)
    rt = jnp.zeros(logits.shape, F32)
    for k, c in enumerate(cols):
        rt = jnp.where(lane == k, c, rt)
    rt_ref[...] = rt
    hi, mid, lo = _split3(rt)
    sel = sel_ref[...]
    rtt_ref[...] = _dot_nt(sel, hi) + _dot_nt(sel, mid) + _dot_nt(sel, lo)


def _post(tok_ops, w_list, mod, g, wr, br):
    t = T_ALL
    tm = TOK_BLOCK
    npb = T_PROMPT // tm
    full = lambda a: pl.BlockSpec(a.shape, lambda i: (0,) * a.ndim)
    specs, args, groups = [], [], []
    for op in tok_ops:
        if isinstance(op, tuple):
            w = op[0].shape[1]
            specs += [pl.BlockSpec((tm, w), lambda i: (jnp.minimum(i, npb - 1), 0)),
                      pl.BlockSpec((tm, w), lambda i: (jnp.maximum(i - npb, 0), 0))]
            args += list(op)
            groups.append(2)
        else:
            specs.append(pl.BlockSpec((tm, op.shape[1]), lambda i: (i, 0)))
            args.append(op)
            groups.append(1)
    idx = np.arange(tm)
    ls = jnp.asarray(idx[:, None] > idx[None, :], BF16)
    sel = jnp.asarray(np.arange(8)[:, None] == np.arange(LANES)[None, :], BF16)
    consts = [mod, g, wr, br, ls, sel]
    return pl.pallas_call(
        functools.partial(_post_kernel, groups=tuple(groups)),
        grid=(t // tm,),
        in_specs=specs + [full(w) for w in w_list] + [full(a) for a in consts],
        out_specs=[pl.BlockSpec((tm, D_MODEL), lambda i: (i, 0)),
                   pl.BlockSpec((tm * SLABS, LANES), lambda i: (i, 0)),
                   pl.BlockSpec((tm, LANES), lambda i: (i, 0)),
                   pl.BlockSpec((8, tm), lambda i: (0, i)),
                   pl.BlockSpec((1, LANES), lambda i: (0, 0))],
        out_shape=[jax.ShapeDtypeStruct((t, D_MODEL), F32),
                   jax.ShapeDtypeStruct((t * SLABS, LANES), F32),
                   jax.ShapeDtypeStruct((t, LANES), F32),
                   jax.ShapeDtypeStruct((8, t), F32),
                   jax.ShapeDtypeStruct((1, LANES), F32)],
        scratch_shapes=[pltpu.VMEM((1, LANES), F32)],
        compiler_params=_cparams(("arbitrary",)),
        name="post_mixer_router",
    )(*args, *w_list, *consts)


def _expert_kernel(be_ref, st_ref, nv_ref, xn_hbm, w1_ref, w3_ref, w2_ref, o_ref, xa, xb, sem, w1b, w3b, w2b):
    i = pl.program_id(0)
    nv = nv_ref[0]
    active = i < nv
    rows = EXPERT_ROWS

    def gather(blk, buf, s):
        for r in range(rows):
            tok = st_ref[blk * rows + r]
            pltpu.make_async_copy(xn_hbm.at[pl.ds(pl.multiple_of(tok * SLABS, SLABS), SLABS), :],
                                  buf.at[pl.ds(r * SLABS, SLABS), :], sem.at[s]).start()

    def wait(buf, s):
        pltpu.make_async_copy(xn_hbm.at[pl.ds(0, rows * SLABS), :], buf, sem.at[s]).wait()

    @pl.when(i == 0)
    def _():
        gather(0, xa, 0)

    changed = jnp.logical_or(i == 0, be_ref[i] != be_ref[jnp.maximum(i - 1, 0)])

    @pl.when(jnp.logical_and(changed, active))
    def _():
        w1b[...] = w1_ref[0, 0].astype(BF16)
        w3b[...] = w3_ref[0, 0].astype(BF16)
        w2b[...] = w2_ref[0, 0].astype(BF16)

    nxt = jnp.minimum(i + 1, nv - 1)

    def step(cur, s_cur, oth, s_oth):
        wait(cur, s_cur)
        gather(nxt, oth, s_oth)
        x = _from_token_tiles(cur, 0, rows).astype(BF16)
        h1 = _dot(x, w1b[...])
        h3 = _dot(x, w3b[...])
        hid = (h1 * jax.nn.sigmoid(h1)) * h3
        _to_token_tiles(o_ref, _dot(hid.astype(BF16), w2b[...]))

    @pl.when(jnp.logical_and(active, i % 2 == 0))
    def _():
        step(xa, 0, xb, 1)

    @pl.when(jnp.logical_and(active, i % 2 == 1))
    def _():
        step(xb, 1, xa, 0)

    @pl.when(jnp.logical_not(active))
    def _():
        o_ref[...] = jnp.zeros(o_ref.shape, F32)

    @pl.when(i == nv - 1)
    def _():
        @pl.when(i % 2 == 0)
        def _():
            wait(xb, 1)

        @pl.when(i % 2 == 1)
        def _():
            wait(xa, 0)


def _experts(block_expert, slot_tok, n_valid, xn, w1, w3, w2, layer):
    nblk = block_expert.shape[0]
    rows = EXPERT_ROWS
    grid_spec = pltpu.PrefetchScalarGridSpec(
        num_scalar_prefetch=3,
        grid=(nblk,),
        in_specs=[pl.BlockSpec(memory_space=pl.ANY),
                  pl.BlockSpec((1, 1, D_MODEL, D_EXPERT), lambda i, be, st, nv: (layer, be[i], 0, 0)),
                  pl.BlockSpec((1, 1, D_MODEL, D_EXPERT), lambda i, be, st, nv: (layer, be[i], 0, 0)),
                  pl.BlockSpec((1, 1, D_EXPERT, D_MODEL), lambda i, be, st, nv: (layer, be[i], 0, 0))],
        out_specs=pl.BlockSpec((rows * SLABS, LANES), lambda i, be, st, nv: (i, 0)),
        scratch_shapes=[pltpu.VMEM((rows * SLABS, LANES), F32),
                        pltpu.VMEM((rows * SLABS, LANES), F32),
                        pltpu.SemaphoreType.DMA((2,)),
                        pltpu.VMEM((D_MODEL, D_EXPERT), BF16),
                        pltpu.VMEM((D_MODEL, D_EXPERT), BF16),
                        pltpu.VMEM((D_EXPERT, D_MODEL), BF16)])
    return pl.pallas_call(
        _expert_kernel,
        grid_spec=grid_spec,
        out_shape=jax.ShapeDtypeStruct((nblk * rows * SLABS, LANES), F32),
        compiler_params=_cparams(("arbitrary",)),
        name="moe_experts",
    )(block_expert, slot_tok, n_valid, xn, w1, w3, w2)


_COMBINE_BLOCK = 128


def _combine_kernel(dest_ref, yb_hbm, x_ref, rt_ref, mod_ref, gfin_ref, o_ref, ybuf, sem, *, final_norm, row0):
    i = pl.program_id(0)
    nblk = pl.num_programs(0)
    slot = i % 2
    tm = _COMBINE_BLOCK

    def issue(blk, s):
        def body(j, carry):
            for c in range(2):
                d = dest_ref[c * T_ALL + row0 + blk * tm + j]
                pltpu.make_async_copy(yb_hbm.at[pl.ds(pl.multiple_of(d * SLABS, SLABS), SLABS), :],
                                      ybuf.at[s, pl.ds(pl.multiple_of((c * tm + j) * SLABS, SLABS), SLABS), :],
                                      sem.at[s]).start()
            return carry
        lax.fori_loop(0, tm, body, 0, unroll=4)

    @pl.when(i == 0)
    def _():
        issue(0, 0)

    @pl.when(i + 1 < nblk)
    def _():
        issue(i + 1, 1 - slot)

    pltpu.make_async_copy(yb_hbm.at[pl.ds(0, 2 * tm * SLABS), :], ybuf.at[slot], sem.at[slot]).wait()
    r = _mod_row(row0 + i * tm)
    rt = rt_ref[...]
    yv = ybuf.at[slot]
    y = rt[:, 2:3] * _from_token_tiles(yv, 0, tm) + rt[:, 3:4] * _from_token_tiles(yv, tm, tm)
    out = x_ref[...] + _mod_part(mod_ref, r, 5) * y
    if final_norm:
        out = _rms(out, gfin_ref[...])
    o_ref[...] = out


def _combine(dest, yb, x, rt, mod, gfin, final_norm, row0=0, t=T_ALL):
    tm = _COMBINE_BLOCK
    blk0 = row0 // tm
    grid_spec = pltpu.PrefetchScalarGridSpec(
        num_scalar_prefetch=1,
        grid=(t // tm,),
        in_specs=[pl.BlockSpec(memory_space=pl.ANY),
                  pl.BlockSpec((tm, D_MODEL), lambda i, d: (blk0 + i, 0)),
                  pl.BlockSpec((tm, LANES), lambda i, d: (blk0 + i, 0)),
                  pl.BlockSpec(mod.shape, lambda i, d: (0, 0)),
                  pl.BlockSpec(gfin.shape, lambda i, d: (0, 0))],
        out_specs=pl.BlockSpec((tm, D_MODEL), lambda i, d: (i, 0)),
        scratch_shapes=[pltpu.VMEM((2, 2 * tm * SLABS, LANES), F32),
                        pltpu.SemaphoreType.DMA((2,))])
    return pl.pallas_call(
        functools.partial(_combine_kernel, final_norm=final_norm, row0=row0),
        grid_spec=grid_spec,
        out_shape=jax.ShapeDtypeStruct((t, D_MODEL), F32),
        compiler_params=_cparams(("arbitrary",)),
        name="moe_combine",
    )(dest, yb, x, rt, mod, gfin)


def _combine_proj_kernel(dest_ref, yb_hbm, x_ref, rt_ref, mod0_ref, mod1_ref, g_ref, w_ref,
                         xo_ref, q_ref, k_ref, v_ref, ya, yb, sem, *, row0):
    i = pl.program_id(0)
    nblk = pl.num_programs(0)
    tm = x_ref.shape[0]

    def gather(blk, buf, s):
        for j in range(tm):
            for c in range(2):
                d = dest_ref[c * T_ALL + row0 + blk * tm + j]
                pltpu.make_async_copy(yb_hbm.at[pl.ds(pl.multiple_of(d * SLABS, SLABS), SLABS), :],
                                      buf.at[pl.ds((c * tm + j) * SLABS, SLABS), :], sem.at[s]).start()

    def wait(buf, s):
        pltpu.make_async_copy(yb_hbm.at[pl.ds(0, 2 * tm * SLABS), :], buf, sem.at[s]).wait()

    @pl.when(i == 0)
    def _():
        gather(0, ya, 0)

    nxt = jnp.minimum(i + 1, nblk - 1)

    def step(cur, s_cur, oth, s_oth):
        wait(cur, s_cur)
        gather(nxt, oth, s_oth)
        r = _mod_row(row0 + i * tm)
        rt = rt_ref[...]
        y = rt[:, 2:3] * _from_token_tiles(cur, 0, tm) + rt[:, 3:4] * _from_token_tiles(cur, tm, tm)
        x = x_ref[...] + _mod_part(mod0_ref, r, 5) * y
        xo_ref[...] = x
        h = _rms(x, g_ref[...]) * (1.0 + _mod_part(mod1_ref, r, 1)) + _mod_part(mod1_ref, r, 0)
        hb = h.astype(BF16)
        for j, o_ref in enumerate((q_ref, k_ref, v_ref)):
            o_ref[...] = _dot(hb, w_ref[:, j * C_WIDTH:(j + 1) * C_WIDTH])

    @pl.when(i % 2 == 0)
    def _():
        step(ya, 0, yb, 1)

    @pl.when(i % 2 == 1)
    def _():
        step(yb, 1, ya, 0)

    @pl.when(i == nblk - 1)
    def _():
        @pl.when(i % 2 == 0)
        def _():
            wait(yb, 1)

        @pl.when(i % 2 == 1)
        def _():
            wait(ya, 0)


def _combine_proj(dest, yb, x, rt, mod0, mod1, g, w, row0, t):
    tm = TOK_BLOCK
    blk0 = row0 // tm
    full = lambda a: pl.BlockSpec(a.shape, lambda i, d: (0,) * a.ndim)
    grid_spec = pltpu.PrefetchScalarGridSpec(
        num_scalar_prefetch=1,
        grid=(t // tm,),
        in_specs=[pl.BlockSpec(memory_space=pl.ANY),
                  pl.BlockSpec((tm, D_MODEL), lambda i, d: (blk0 + i, 0)),
                  pl.BlockSpec((tm, LANES), lambda i, d: (blk0 + i, 0)),
                  full(mod0), full(mod1), full(g), full(w)],
        out_specs=[pl.BlockSpec((tm, D_MODEL), lambda i, d: (i, 0))] * 4,
        scratch_shapes=[pltpu.VMEM((2 * tm * SLABS, LANES), F32),
                        pltpu.VMEM((2 * tm * SLABS, LANES), F32),
                        pltpu.SemaphoreType.DMA((2,))])
    return pl.pallas_call(
        functools.partial(_combine_proj_kernel, row0=row0),
        grid_spec=grid_spec,
        out_shape=[jax.ShapeDtypeStruct((t, D_MODEL), F32)] * 4,
        compiler_params=_cparams(("arbitrary",)),
        name="moe_combine_proj",
    )(dest, yb, x, rt, mod0, mod1, g, w)


def _moe_plan(rtt, cnt):
    t = rtt.shape[1]
    eid = rtt[0:2].astype(jnp.int32)
    rank = rtt[4:6].astype(jnp.int32)
    counts = cnt[0, N_GROUPS:N_GROUPS + N_EXPERTS].astype(jnp.int32)
    padded = (counts + EXPERT_ROWS - 1) // EXPERT_ROWS * EXPERT_ROWS
    seg_end = jnp.cumsum(padded)
    seg_start = seg_end - padded
    experts = jnp.arange(N_EXPERTS, dtype=jnp.int32)
    start = jnp.sum(jnp.where(eid[..., None] == experts, seg_start, 0), axis=-1)
    dest = (start + rank).reshape(-1)
    n_blocks = (2 * t + N_EXPERTS * (EXPERT_ROWS - 1) + EXPERT_ROWS - 1) // EXPERT_ROWS
    tok = jnp.tile(jnp.arange(t, dtype=jnp.int32), 2)
    slot_tok = (jnp.arange(n_blocks * EXPERT_ROWS, dtype=jnp.int32) % t).at[dest].set(
        tok, unique_indices=True, mode='promise_in_bounds')
    first_row = jnp.arange(n_blocks, dtype=jnp.int32) * EXPERT_ROWS
    block_expert = jnp.minimum(jnp.sum((seg_end[None, :] <= first_row[:, None]).astype(jnp.int32), axis=1),
                               N_EXPERTS - 1)
    n_valid = (seg_end[-1:] // EXPERT_ROWS).astype(jnp.int32)
    return dest, slot_tok, block_expert, n_valid


def _attn_c_prompt_kernel(q_ref, k_ref, v_ref, o_ref):
    for p in range(C_HEADS // 2):
        sl = slice(p * LANES, (p + 1) * LANES)
        kb = k_ref[:, sl].astype(BF16)
        vb = v_ref[:, sl].astype(BF16)
        qp = q_ref[:, sl] * (C_HEAD_DIM ** -0.5)
        o_ref[:, sl] = _softmax_pair(qp, lambda qm: [_dot_nt(qm, kb)], lambda ps: _dot(ps[0], vb))


def _attn_c_prompt(q, k, v):
    blk = pl.BlockSpec((SEQ, C_WIDTH), lambda b: (b, 0))
    return pl.pallas_call(
        _attn_c_prompt_kernel,
        grid=(BATCH,),
        in_specs=[blk, blk, blk],
        out_specs=blk,
        out_shape=jax.ShapeDtypeStruct((T_PROMPT, C_WIDTH), F32),
        compiler_params=_cparams(("arbitrary",)),
        name="attn_c_prompt",
    )(q, k, v)


def _na_key_start(r0):
    rows = DEC_SEQ // GRID_W
    return jnp.minimum(jnp.clip(r0 - NA_ROWS // 2, 0, rows - NA_ROWS), rows - NA_KROWS)


def _na_block_plan():
    rows = DEC_SEQ // GRID_W
    nblk = rows // NA_QROWS
    plan = []
    for blk in (0, 1, nblk - 1):
        r0 = blk * NA_QROWS
        ks = min(int(np.clip(r0 - NA_ROWS // 2, 0, rows - NA_ROWS)), rows - NA_KROWS)
        per_row = []
        for i in range(NA_QROWS):
            r = r0 + i
            rs = int(np.clip(r - NA_ROWS // 2, 0, rows - NA_ROWS))
            start = ks - r + NA_ROWS - 1 + NA_KROWS
            ok = [rs <= ks + j < rs + NA_ROWS for j in range(NA_KROWS)]
            per_row.append((start, ok))
        plan.append(per_row)
    return plan


def _attn_c_sample_kernel(q_ref, k_ref, v_ref, ck_ref, cv_ref, toe_ref, o_ref, bias_ref):
    rows = DEC_SEQ // GRID_W
    nblk = rows // NA_QROWS
    w = GRID_W

    @pl.when(jnp.logical_and(pl.program_id(1) == 0, pl.program_id(2) == 0))
    def _():
        neg = jnp.full((w, w), NEG, F32)
        for t, per_row in enumerate(_na_block_plan()):
            for half in range(2):
                for i, (start, ok) in enumerate(per_row):
                    for j in range(0, NA_KROWS, 2):
                        pieces = [toe_ref[0, half, start + jj] if ok[jj] else neg for jj in (j, j + 1)]
                        bias_ref[t, half, i * w:(i + 1) * w, j * w:(j + 2) * w] = jnp.concatenate(pieces, axis=1)

    i = pl.program_id(2)
    r0 = i * NA_QROWS
    k0 = pl.multiple_of(_na_key_start(r0) * GRID_W, GRID_W)
    btype = jnp.where(i == 0, 0, jnp.where(i == nblk - 1, 2, 1))
    nk = NA_KROWS * GRID_W
    kw = k_ref[pl.ds(k0, nk), :].astype(BF16)
    vw = v_ref[pl.ds(k0, nk), :].astype(BF16)
    kc = ck_ref[0].astype(BF16)
    vc = cv_ref[0].astype(BF16)
    qp = q_ref[...] * (C_HEAD_DIM ** -0.5)
    lo = _lane(qp.shape) < 64
    outs = []
    for half in range(2):
        qm = jnp.where(lo if half == 0 else jnp.logical_not(lo), qp, 0.0).astype(BF16)
        s_win = _dot_nt(qm, kw) + bias_ref[btype, half]
        s_ctx = _dot_nt(qm, kc)
        m = jnp.maximum(s_win.max(axis=-1, keepdims=True), s_ctx.max(axis=-1, keepdims=True))
        e_win = jnp.exp(s_win - m)
        e_ctx = jnp.exp(s_ctx - m)
        l = e_win.sum(axis=-1, keepdims=True) + e_ctx.sum(axis=-1, keepdims=True)
        o = _dot(e_win.astype(BF16), vw) + _dot(e_ctx.astype(BF16), vc)
        outs.append(o / l)
    o_ref[...] = jnp.where(lo, outs[0], outs[1])


def _attn_c_sample(q, k, v, ck, cv, toe):
    rows = DEC_SEQ // GRID_W
    nblk = rows // NA_QROWS
    qrows = NA_QROWS * GRID_W
    npair = C_HEADS // 2
    return pl.pallas_call(
        _attn_c_sample_kernel,
        grid=(npair, DEC_BATCH, nblk),
        in_specs=[pl.BlockSpec((qrows, LANES), lambda p, b, i: (b * nblk + i, p)),
                  pl.BlockSpec((DEC_SEQ, LANES), lambda p, b, i: (b, p)),
                  pl.BlockSpec((DEC_SEQ, LANES), lambda p, b, i: (b, p)),
                  pl.BlockSpec((1, PAST_LEN, LANES), lambda p, b, i: (b, 0, p)),
                  pl.BlockSpec((1, PAST_LEN, LANES), lambda p, b, i: (b, 0, p)),
                  pl.BlockSpec((1,) + toe.shape[1:], lambda p, b, i: (p, 0, 0, 0, 0))],
        out_specs=pl.BlockSpec((qrows, LANES), lambda p, b, i: (b * nblk + i, p)),
        out_shape=jax.ShapeDtypeStruct((T_SAMPLE, C_WIDTH), F32),
        scratch_shapes=[pltpu.VMEM((3, 2, qrows, NA_KROWS * GRID_W), F32)],
        compiler_params=_cparams(("arbitrary", "arbitrary", "arbitrary")),
        name="attn_c_sample",
    )(q, k, v, ck, cv, toe)


def _na_toeplitz(rpb):
    w = GRID_W
    nd_r, nd_c = 2 * NA_ROWS - 1, 2 * NA_COLS - 1
    c = np.arange(w)
    cs = np.clip(c - NA_COLS // 2, 0, w - NA_COLS)
    col_ok = (c[None, :] >= cs[:, None]) & (c[None, :] < cs[:, None] + NA_COLS)
    dcol = c[None, :] - c[:, None] + NA_COLS - 1
    onehot = (np.arange(nd_c)[:, None, None] == dcol[None]).reshape(nd_c, w * w)
    toe = jnp.dot(rpb.reshape(C_HEADS * nd_r, nd_c), jnp.asarray(onehot, F32), precision=lax.Precision.HIGHEST)
    toe = jnp.where(col_ok[None, None], toe.reshape(C_HEADS, nd_r, w, w), NEG)
    toe = jnp.pad(toe, ((0, 0), (NA_KROWS, NA_KROWS), (0, 0), (0, 0)), constant_values=NEG)
    return toe.reshape(C_HEADS // 2, 2, nd_r + 2 * NA_KROWS, w, w)


def _rope_tables():
    half = A_HEAD_DIM // 2
    t = jnp.arange(DEC_SEQ)
    row = (t // GRID_W).astype(F32)
    colp = (t % GRID_W).astype(F32)
    freqs = 1.0 / (ROPE_BASE ** (jnp.arange(0, half, 2, dtype=F32) / half))
    d = np.arange(LANES) % A_HEAD_DIM
    pos = jnp.where(jnp.asarray(d < half)[None, :], row[:, None], colp[:, None])
    ang = pos * freqs[d % (half // 2)][None, :]
    sign = jnp.asarray(np.where((d % half) < half // 2, -1.0, 1.0), F32)[None, :]
    return jnp.cos(ang), jnp.sin(ang) * sign


def _head_avg_matrix():
    idx = np.arange(LANES) // A_HEAD_DIM
    return jnp.asarray((idx[:, None] == idx[None, :]).astype(np.float32) / A_HEAD_DIM, BF16)


def _tri_matrices():
    i = np.arange(MLSTM_CHUNK)
    upper = (i[:, None] <= i[None, :]).astype(np.float32)
    lower = (i[:, None] >= i[None, :]).astype(np.float32)
    return jnp.asarray(np.stack([upper, lower]), BF16)


def _router_weights(wg, bg, we, be):
    w = jnp.zeros((D_MODEL, LANES), F32).at[:, :N_GROUPS].set(wg).at[:, N_GROUPS:N_GROUPS + N_EXPERTS].set(we)
    b = jnp.zeros((1, LANES), F32).at[0, :N_GROUPS].set(bg).at[0, N_GROUPS:N_GROUPS + N_EXPERTS].set(be)
    hi = w.astype(BF16)
    lo = (w - hi.astype(F32)).astype(BF16)
    return jnp.concatenate([hi, lo], axis=1), b


def _moe(xn, rtt, cnt, w1, w3, w2, layer):
    dest, slot_tok, block_expert, n_valid = _moe_plan(rtt, cnt)
    return dest, _experts(block_expert, slot_tok, n_valid, xn, w1, w3, w2, layer)


def kernel(x_prompt, x_sample, cache_attn_k, cache_attn_v, state_mlstm_C, state_mlstm_n, state_mlstm_m,
           cache_na_k, cache_na_v, c, c_ctx, norm_mix, norm_ffn, norm_final, ada_w, ada_b,
           ab_w_in, ab_w_out, ab_q_norm, ab_k_norm, ab_gate_bias, ab_out_norm,
           na_w_in, na_w_out, na_rpb, moe_wg, moe_bg, moe_we, moe_be, moe_w1, moe_w3, moe_w2):
    xp = x_prompt.reshape(T_PROMPT, D_MODEL)
    xs = x_sample.reshape(T_SAMPLE, D_MODEL)
    cond =jnp.zeros((N_COND, D_MODEL), F32).at[0].set(c_ctx).at[1:1 + DEC_BATCH].set(c)
    mod = _modulation(cond, ada_w, ada_b)
    gfin = norm_final.reshape(1, D_MODEL)

    w_in = ab_w_in[0]
    o_aq, o_ak, o_av, o_bq, o_bk, o_bv, o_bo, o_bg = np.cumsum((0,) + (A_WIDTH, A_KV_WIDTH, A_KV_WIDTH,
                                                                       B_WIDTH, B_WIDTH, B_WIDTH, B_WIDTH))
    wb = jnp.concatenate([w_in[:, o_bq:o_bk], w_in[:, o_bv:o_bg]], axis=1).astype(BF16)
    wq = w_in[:, o_aq:o_ak].astype(BF16)
    wkv = w_in[:, o_ak:o_bq].astype(BF16)
    gate_rows = np.array([0, 8, 1, 9])
    wg = w_in[:, o_bg:o_bg + 4 * B_HEADS].reshape(D_MODEL, 4, B_HEADS)
    wgt = jnp.zeros((B_HEADS, 16, D_MODEL), F32).at[:, gate_rows, :].set(wg.transpose(2, 1, 0))
    wgt = jnp.concatenate([w_in[:, o_bk:o_bv].T, wgt.reshape(16 * B_HEADS, D_MODEL)], axis=0).astype(BF16)
    g_mix = norm_mix[0].reshape(1, D_MODEL)
    ob_p, oq_p, okv_p, ogt_p = _proj_ab(xp, mod[0], g_mix, wb, wq, wkv, wgt, 0)
    ob_s, oq_s, okv_s, ogt_s = _proj_ab(xs, mod[0], g_mix, wb, wq, wkv, wgt, T_PROMPT)

    qn = jnp.tile(ab_q_norm[0], 2).reshape(1, LANES)
    kn = jnp.tile(ab_k_norm[0], 2).reshape(1, LANES)
    bd = _head_avg_matrix()
    cos, sin = _rope_tables()
    a_p, new_k, new_v = _attn_a_prompt(oq_p, okv_p, qn, kn, bd)
    ck = cache_attn_k[:, 0].reshape(DEC_BATCH, PAST_LEN, A_KV_WIDTH)
    cv = cache_attn_v[:, 0].reshape(DEC_BATCH, PAST_LEN, A_KV_WIDTH)
    a_s = _attn_a_sample(oq_s, okv_s, ck, cv, cos, sin, qn, kn, bd)

    gb = ab_gate_bias[0]
    brow = jnp.zeros((B_HEADS, 16, LANES), F32).at[:, gate_rows, :].set(
        jnp.broadcast_to(gb.T[:, :, None], (B_HEADS, 4, LANES)))
    onorm = ab_out_norm[0].reshape(B_HEADS, 1, B_HEAD_DIM)
    tri = _tri_matrices()
    b_p, cT, nT, mT = _mlstm(ob_p, ogt_p, brow, onorm, tri, n=SEQ, nseq=BATCH, emit_state=True)
    n0 = jnp.broadcast_to(state_mlstm_n[:, 0][..., None], state_mlstm_C[:, 0].shape)
    c0 = jnp.concatenate([state_mlstm_C[:, 0], n0], axis=-1)
    m0 = jnp.broadcast_to(state_mlstm_m[:, 0].transpose(0, 2, 1)[..., None], (DEC_BATCH, B_HEADS, 2, LANES))
    (b_s,) = _mlstm(ob_s, ogt_s, brow, onorm, tri, n=DEC_SEQ, nseq=DEC_BATCH, init=(c0, m0))

    w_out = ab_w_out[0].astype(BF16)
    wr, br = _router_weights(moe_wg[0], moe_bg[0], moe_we[0], moe_be[0])
    x1, xn, rt, rtt, cnt = _post([(xp, xs), (a_p, a_s), (b_p, b_s)], [w_out[:A_WIDTH], w_out[A_WIDTH:]], mod[0],
                                 norm_ffn[0].reshape(1, D_MODEL), wr, br)
    dest, yb = _moe(xn, rtt, cnt, moe_w1, moe_w3, moe_w2, 0)

    g_mix = norm_mix[1].reshape(1, D_MODEL)
    w_in = na_w_in[0].astype(BF16)
    x_p, q_p, k_p, v_p = _combine_proj(dest, yb, x1, rt, mod[0], mod[1], g_mix, w_in, 0, T_PROMPT)
    x_s, q_s, k_s, v_s = _combine_proj(dest, yb, x1, rt, mod[0], mod[1], g_mix, w_in, T_PROMPT, T_SAMPLE)
    o_p = _attn_c_prompt(q_p, k_p, v_p)
    nck = cache_na_k[:, 0].reshape(DEC_BATCH, PAST_LEN, C_WIDTH)
    ncv = cache_na_v[:, 0].reshape(DEC_BATCH, PAST_LEN, C_WIDTH)
    o_s = _attn_c_sample(q_s, k_s, v_s, nck, ncv, _na_toeplitz(na_rpb[0]))
    wr, br = _router_weights(moe_wg[1], moe_bg[1], moe_we[1], moe_be[1])
    x1, xn, rt, rtt, cnt = _post([(x_p, x_s), (o_p, o_s)], [na_w_out[0].astype(BF16)], mod[1],
                                 norm_ffn[1].reshape(1, D_MODEL), wr, br)
    dest, yb = _moe(xn, rtt, cnt, moe_w1, moe_w3, moe_w2, 1)
    y_prompt = _combine(dest, yb, x1, rt, mod[1], gfin, True, 0, T_PROMPT).reshape(BATCH, SEQ, D_MODEL)
    y_sample = _combine(dest, yb, x1, rt, mod[1], gfin, True, T_PROMPT, T_SAMPLE).reshape(DEC_BATCH, DEC_SEQ, D_MODEL)
    new_attn_k = new_k.reshape(BATCH, 1, SEQ, A_KV_HEADS, A_HEAD_DIM)
    new_attn_v = new_v.reshape(BATCH, 1, SEQ, A_KV_HEADS, A_HEAD_DIM)
    new_mlstm_C = cT[:, None]
    new_mlstm_n = nT.transpose(0, 2, 1, 3)[:, None]
    new_mlstm_m = mT[..., 0].transpose(0, 2, 1)[:, None]
    new_na_k = k_p.reshape(BATCH, 1, SEQ, C_HEADS, C_HEAD_DIM)
    new_na_v = v_p.reshape(BATCH, 1, SEQ, C_HEADS, C_HEAD_DIM)
    return (y_prompt, y_sample, new_attn_k, new_attn_v, new_mlstm_C, new_mlstm_n, new_mlstm_m,
            new_na_k, new_na_v)
```

```python
import functools

import numpy as np
import jax
import jax.numpy as jnp
from jax import lax
from jax.experimental import pallas as pl
from jax.experimental.pallas import tpu as pltpu

F32 = jnp.float32
BF16 = jnp.bfloat16

D_MODEL = 1024
BATCH = 32
SEQ = 256
DEC_BATCH = 4
DEC_SEQ = 2048
PAST_LEN = 256
GRID_W = 64
A_HEADS = 8
A_KV_HEADS = 2
A_HEAD_DIM = 64
A_WIDTH = A_HEADS * A_HEAD_DIM
A_KV_WIDTH = A_KV_HEADS * A_HEAD_DIM
B_HEADS = 4
B_HEAD_DIM = 128
B_WIDTH = B_HEADS * B_HEAD_DIM
MLSTM_CHUNK = 128
C_HEADS = 16
C_HEAD_DIM = 64
C_WIDTH = C_HEADS * C_HEAD_DIM
NA_ROWS = 8
NA_COLS = 16
N_GROUPS = 4
EXPERTS_PER_GROUP = 8
N_EXPERTS = N_GROUPS * EXPERTS_PER_GROUP
D_EXPERT = 512
MOE_BLOCK = 128
ROPE_BASE = 10000.0
NORM_EPS = 1e-6

T_PROMPT = BATCH * SEQ
T_SAMPLE = DEC_BATCH * DEC_SEQ
T_ALL = T_PROMPT + T_SAMPLE
N_COND = 8
LANES = 128
SLABS = D_MODEL // LANES
TOK_BLOCK = 256
MLSTM_HEADS_PER_STEP = 2
EXPERT_ROWS = 512
NA_QROWS = 4
NA_KROWS = 12
NEG = -1e30
VMEM_LIMIT = 56 * 1024 * 1024


def _cparams(sem):
    return pltpu.CompilerParams(dimension_semantics=sem, vmem_limit_bytes=VMEM_LIMIT)


def _split2(x):
    hi = x.astype(BF16)
    lo = (x - hi.astype(F32)).astype(BF16)
    return hi, lo


def _split3(x):
    hi = x.astype(BF16)
    r = x - hi.astype(F32)
    mid = r.astype(BF16)
    lo = (r - mid.astype(F32)).astype(BF16)
    return hi, mid, lo


def _dot(a, b):
    return jnp.dot(a, b, preferred_element_type=F32)


def _dot_nt(a, b):
    return lax.dot_general(a, b, (((1,), (1,)), ((), ())), preferred_element_type=F32)


def _dot_exact_rhs(x, b):
    hi, mid, lo = _split3(x)
    return _dot(hi, b) + _dot(mid, b) + _dot(lo, b)


def _rms(x, g):
    ms = jnp.mean(x * x, axis=-1, keepdims=True)
    return (x * lax.rsqrt(ms + NORM_EPS)) * g


def _mod_row(tok0):
    return jnp.where(tok0 < T_PROMPT, 0, 1 + (tok0 - T_PROMPT) // DEC_SEQ)


def _mod_part(mod_ref, r, idx):
    return mod_ref[pl.ds(r, 1), idx * D_MODEL:(idx + 1) * D_MODEL]


def _head_rms(x, w, bd):
    hi, lo = _split2(x * x)
    ms = _dot(hi, bd) + _dot(lo, bd)
    return (x * lax.rsqrt(ms + NORM_EPS)) * w


def _to_token_tiles(ref, x):
    m = x.shape[0]
    for s in range(SLABS):
        ref[pl.ds(s, m, stride=SLABS), :] = x[:, s * LANES:(s + 1) * LANES]


def _from_token_tiles(ref, tile0, m):
    return jnp.concatenate([ref[pl.ds(tile0 * SLABS + s, m, stride=SLABS), :] for s in range(SLABS)], axis=1)


def _lane(shape):
    return lax.broadcasted_iota(jnp.int32, shape, len(shape) - 1)


def _dup_half(x, g):
    xr = pltpu.roll(x, 64, 1)
    lo = _lane(x.shape) < 64
    return jnp.where(lo, x, xr) if g == 0 else jnp.where(lo, xr, x)


def _rope(x, cos, sin_signed):
    lane = _lane(x.shape)
    partner = jnp.where((lane % 32) < 16, pltpu.roll(x, LANES - 16, 1), pltpu.roll(x, 16, 1))
    return x * cos + partner * sin_signed


def _softmax_pair(qp, score_fn, value_fn):
    lo = _lane(qp.shape) < 64
    outs = []
    for half in range(2):
        qm = jnp.where(lo if half == 0 else jnp.logical_not(lo), qp, 0.0).astype(BF16)
        ss = score_fn(qm)
        m = ss[0].max(axis=-1, keepdims=True)
        for s in ss[1:]:
            m = jnp.maximum(m, s.max(axis=-1, keepdims=True))
        es = [jnp.exp(s - m) for s in ss]
        l = es[0].sum(axis=-1, keepdims=True)
        for e in es[1:]:
            l = l + e.sum(axis=-1, keepdims=True)
        o = value_fn([e.astype(BF16) for e in es])
        outs.append(o / l)
    return jnp.where(lo, outs[0], outs[1])


def _mod_kernel(cond_ref, w_ref, b_ref, o_ref):
    c = cond_ref[...]
    s = c * jax.nn.sigmoid(c)
    s_hi, s_lo = _split2(s)
    w_hi, w_lo = _split2(w_ref[0])
    o_ref[0] = _dot(s_hi, w_hi) + _dot(s_lo, w_hi) + _dot(s_hi, w_lo) + b_ref[0]


def _modulation(cond, ada_w, ada_b):
    depth, d, n = ada_w.shape
    tn = 1536
    return pl.pallas_call(
        _mod_kernel,
        grid=(depth, n // tn),
        in_specs=[pl.BlockSpec((N_COND, d), lambda l, j: (0, 0)),
                  pl.BlockSpec((1, d, tn), lambda l, j: (l, 0, j)),
                  pl.BlockSpec((1, 1, tn), lambda l, j: (l, 0, j))],
        out_specs=pl.BlockSpec((1, N_COND, tn), lambda l, j: (l, 0, j)),
        out_shape=jax.ShapeDtypeStruct((depth, N_COND, n), F32),
        compiler_params=_cparams(("arbitrary", "arbitrary")),
        name="adaln_modulation",
    )(cond, ada_w, ada_b.reshape(depth, 1, n))


def _norm_mod(x_ref, mod_ref, g_ref, shift_idx, scale_idx, row0):
    r = _mod_row(row0 + pl.program_id(0) * x_ref.shape[0])
    h = _rms(x_ref[...], g_ref[...])
    return h * (1.0 + _mod_part(mod_ref, r, scale_idx)) + _mod_part(mod_ref, r, shift_idx)


def _proj_ab_kernel(x_ref, mod_ref, g_ref, wb_ref, wq_ref, wkv_ref, wgt_ref, ob_ref, oq_ref, okv_ref, ogt_ref,
                    *, row0):
    tm = x_ref.shape[0]
    hb = _norm_mod(x_ref, mod_ref, g_ref, 0, 1, row0).astype(BF16)
    ob_ref[...] = _dot(hb, wb_ref[...])
    oq_ref[...] = _dot(hb, wq_ref[...])
    okv_ref[...] = _dot(hb, wkv_ref[...])
    gt = _dot_nt(wgt_ref[...], hb)
    for j in range(tm // LANES):
        ogt_ref[j] = gt[:, j * LANES:(j + 1) * LANES]


def _proj_ab(x, mod, g, wb, wq, wkv, wgt, row0):
    t = x.shape[0]
    tm = TOK_BLOCK
    full = lambda a: pl.BlockSpec(a.shape, lambda i: (0,) * a.ndim)
    return pl.pallas_call(
        functools.partial(_proj_ab_kernel, row0=row0),
        grid=(t // tm,),
        in_specs=[pl.BlockSpec((tm, D_MODEL), lambda i: (i, 0)), full(mod), full(g),
                  full(wb), full(wq), full(wkv), full(wgt)],
        out_specs=[pl.BlockSpec((tm, wb.shape[1]), lambda i: (i, 0)),
                   pl.BlockSpec((tm, A_WIDTH), lambda i: (i, 0)),
                   pl.BlockSpec((tm, 2 * A_KV_WIDTH), lambda i: (i, 0)),
                   pl.BlockSpec((tm // LANES, wgt.shape[0], LANES), lambda i: (i, 0, 0))],
        out_shape=[jax.ShapeDtypeStruct((t, wb.shape[1]), F32),
                   jax.ShapeDtypeStruct((t, A_WIDTH), F32),
                   jax.ShapeDtypeStruct((t, 2 * A_KV_WIDTH), F32),
                   jax.ShapeDtypeStruct((t // LANES, wgt.shape[0], LANES), F32)],
        compiler_params=_cparams(("arbitrary",)),
        name="proj_ab",
    )(x, mod, g, wb, wq, wkv, wgt)


def _gqa_block(q_ref, qn, bd, kd_ref, vd_ref, o_ref, rope=None):
    for p in range(A_HEADS // 2):
        g = p // (A_HEADS // 2 // A_KV_HEADS)
        qp = _head_rms(q_ref[:, p * LANES:(p + 1) * LANES], qn, bd)
        if rope is not None:
            qp = _rope(qp, rope[0], rope[1])
        qp = qp * (A_HEAD_DIM ** -0.5)
        o_ref[:, p * LANES:(p + 1) * LANES] = _softmax_pair(
            qp, lambda qm: [_dot_nt(qm, kd_ref[g])], lambda ps: _dot(ps[0], vd_ref[g]))


def _attn_a_prompt_kernel(q_ref, kv_ref, qn_ref, kn_ref, bd_ref, o_ref, knew_ref, vnew_ref, kd_ref, vd_ref):
    bd = bd_ref[...]
    k = _head_rms(kv_ref[:, :LANES], kn_ref[...], bd)
    v = kv_ref[:, LANES:]
    knew_ref[...] = k
    vnew_ref[...] = v
    for g in range(A_KV_HEADS):
        kd_ref[g] = _dup_half(k, g).astype(BF16)
        vd_ref[g] = _dup_half(v, g).astype(BF16)
    _gqa_block(q_ref, qn_ref[...], bd, kd_ref, vd_ref, o_ref)


def _attn_a_prompt(q, kv, qn, kn, bd):
    nb = BATCH
    full = lambda a: pl.BlockSpec(a.shape, lambda b: (0,) * a.ndim)
    return pl.pallas_call(
        _attn_a_prompt_kernel,
        grid=(nb,),
        in_specs=[pl.BlockSpec((SEQ, A_WIDTH), lambda b: (b, 0)),
                  pl.BlockSpec((SEQ, 2 * A_KV_WIDTH), lambda b: (b, 0)),
                  full(qn), full(kn), full(bd)],
        out_specs=[pl.BlockSpec((SEQ, A_WIDTH), lambda b: (b, 0)),
                   pl.BlockSpec((SEQ, A_KV_WIDTH), lambda b: (b, 0)),
                   pl.BlockSpec((SEQ, A_KV_WIDTH), lambda b: (b, 0))],
        out_shape=[jax.ShapeDtypeStruct((T_PROMPT, A_WIDTH), F32),
                   jax.ShapeDtypeStruct((T_PROMPT, A_KV_WIDTH), F32),
                   jax.ShapeDtypeStruct((T_PROMPT, A_KV_WIDTH), F32)],
        scratch_shapes=[pltpu.VMEM((A_KV_HEADS, SEQ, LANES), BF16),
                        pltpu.VMEM((A_KV_HEADS, SEQ, LANES), BF16)],
        compiler_params=_cparams(("arbitrary",)),
        name="attn_a_prompt",
    )(q, kv, qn, kn, bd)


_A_QBLOCK = 256


def _attn_a_sample_kernel(q_ref, kv_ref, ck_ref, cv_ref, cos_ref, sin_ref, cosq_ref, sinq_ref,
                          qn_ref, kn_ref, bd_ref, o_ref, kd_ref, vd_ref):
    bd = bd_ref[...]

    @pl.when(pl.program_id(1) == 0)
    def _():
        for g in range(A_KV_HEADS):
            kd_ref[g, :PAST_LEN] = _dup_half(ck_ref[0], g).astype(BF16)
            vd_ref[g, :PAST_LEN] = _dup_half(cv_ref[0], g).astype(BF16)
        rows = 256
        for c in range(DEC_SEQ // rows):
            sl = slice(c * rows, (c + 1) * rows)
            k = _head_rms(kv_ref[sl, :LANES], kn_ref[...], bd)
            k = _rope(k, cos_ref[sl, :], sin_ref[sl, :])
            v = kv_ref[sl, LANES:]
            dst = slice(PAST_LEN + c * rows, PAST_LEN + (c + 1) * rows)
            for g in range(A_KV_HEADS):
                kd_ref[g, dst] = _dup_half(k, g).astype(BF16)
                vd_ref[g, dst] = _dup_half(v, g).astype(BF16)

    _gqa_block(q_ref, qn_ref[...], bd, kd_ref, vd_ref, o_ref, rope=(cosq_ref[...], sinq_ref[...]))


def _attn_a_sample(q, kv, ck, cv, cos, sin, qn, kn, bd):
    nq = DEC_SEQ // _A_QBLOCK
    full = lambda a: pl.BlockSpec(a.shape, lambda b, i: (0,) * a.ndim)
    tk = PAST_LEN + DEC_SEQ
    return pl.pallas_call(
        _attn_a_sample_kernel,
        grid=(DEC_BATCH, nq),
        in_specs=[pl.BlockSpec((_A_QBLOCK, A_WIDTH), lambda b, i: (b * nq + i, 0)),
                  pl.BlockSpec((DEC_SEQ, 2 * A_KV_WIDTH), lambda b, i: (b, 0)),
                  pl.BlockSpec((1, PAST_LEN, A_KV_WIDTH), lambda b, i: (b, 0, 0)),
                  pl.BlockSpec((1, PAST_LEN, A_KV_WIDTH), lambda b, i: (b, 0, 0)),
                  full(cos), full(sin),
                  pl.BlockSpec((_A_QBLOCK, LANES), lambda b, i: (i, 0)),
                  pl.BlockSpec((_A_QBLOCK, LANES), lambda b, i: (i, 0)),
                  full(qn), full(kn), full(bd)],
        out_specs=pl.BlockSpec((_A_QBLOCK, A_WIDTH), lambda b, i: (b * nq + i, 0)),
        out_shape=jax.ShapeDtypeStruct((T_SAMPLE, A_WIDTH), F32),
        scratch_shapes=[pltpu.VMEM((A_KV_HEADS, tk, LANES), BF16),
                        pltpu.VMEM((A_KV_HEADS, tk, LANES), BF16)],
        compiler_params=_cparams(("arbitrary", "arbitrary")),
        name="attn_a_sample",
    )(q, kv, ck, cv, cos, sin, cos, sin, qn, kn, bd)


def _log_sigmoid(x):
    return -(jnp.maximum(-x, 0.0) + jnp.log1p(jnp.exp(-jnp.abs(x))))


def _mlstm_kernel(*refs, n, has_init, emit_state):
    it = iter(refs)
    q_ref, v_ref, og_ref, kt_ref, gt_ref, brow_ref, onorm_ref, tri_ref = [next(it) for _ in range(8)]
    if has_init:
        c0_ref, m0_ref = [next(it) for _ in range(2)]
    out_ref = next(it)
    if emit_state:
        cT_ref, nT_ref, mT_ref = [next(it) for _ in range(3)]
    h_ref, row_ref, c_ref = [next(it) for _ in range(3)]

    L = MLSTM_CHUNK
    D = B_HEAD_DIM
    nc = n // L
    hb = MLSTM_HEADS_PER_STEP

    lane = lax.broadcasted_iota(jnp.int32, (nc * 8, L), 1)
    is_fwd = lax.broadcasted_iota(jnp.int32, (nc * 8, L), 0) % 8 == 0

    def running_max(x, suffix):
        for sh in (1, 2, 4, 8, 16, 32, 64):
            if suffix:
                x = jnp.where(lane < L - sh, jnp.maximum(x, pltpu.roll(x, L - sh, 1)), x)
            else:
                x = jnp.where(lane >= sh, jnp.maximum(x, pltpu.roll(x, sh, 1)), x)
        return x

    for hh in range(hb):
        gt = gt_ref[:, 16 * hh:16 * hh + 16, :] + brow_ref[hh][None]
        li = gt[:, 0:8, :].reshape(nc * 8, L)
        lf = _log_sigmoid(gt[:, 8:16, :]).reshape(nc * 8, L)
        cum = jnp.where(is_fwd, _dot_exact_rhs(lf, tri_ref[0]), _dot_exact_rhs(lf, tri_ref[1]))
        a = li - cum
        planes = (cum, a, jnp.broadcast_to(lf.sum(axis=-1, keepdims=True), (nc * 8, L)),
                  jnp.where(is_fwd, running_max(a, False), running_max(a, True)),
                  jnp.broadcast_to(a.max(axis=-1, keepdims=True), (nc * 8, L)))
        for p, val in enumerate(planes):
            row_ref[hh, p] = val.reshape(nc, 8, L)

    chains = [(hh, d) for hh in range(hb) for d in range(2)]
    if has_init:
        m_init = []
        for hh, d in chains:
            c_ref[2 * hh + d] = c0_ref[0, d, hh]
            m_init.append(m0_ref[0, hh, d:d + 1, :])
        m_init = tuple(m_init)
    else:
        c_ref[...] = jnp.zeros(c_ref.shape, F32)
        m_init = tuple(jnp.zeros((1, L), F32) for _ in chains)

    t_idx = lax.broadcasted_iota(jnp.int32, (L, L), 0)
    s_idx = lax.broadcasted_iota(jnp.int32, (L, L), 1)
    masks = (s_idx <= t_idx, s_idx >= t_idx)
    ones = jnp.ones((L, L), F32)

    def step(c, hh, d, m):
        r0 = pl.multiple_of(c * L, L)
        hl = slice(hh * L, (hh + 1) * L)
        qb = q_ref[pl.ds(r0, L), hl].astype(BF16)
        kst = kt_ref[c, hl, :] * (D ** -0.5)
        v_ext = jnp.concatenate([v_ref[pl.ds(r0, L), hl], ones], axis=1).astype(BF16)
        cum, a_row, tot, amax_run, amax = [row_ref[hh, p, c][d:d + 1, :] for p in range(5)]
        m_cum = jnp.broadcast_to(cum, (L, L)).T
        m_run = jnp.broadcast_to(amax_run, (L, L)).T
        dlog = jnp.where(masks[d], m_cum + a_row, -jnp.inf)
        inter = m_cum + m
        m_t = jnp.maximum(inter, m_cum + m_run)
        w_in = jnp.exp(dlog - m_t)
        w_st = jnp.exp(inter - m_t)
        a = _dot(qb, kst.astype(BF16)) * w_in
        ci = 2 * hh + d
        cext = c_ref[ci]
        p_state = _dot(qb, cext.astype(BF16))
        p_intra = _dot(a.astype(BF16), v_ext)
        num = w_st * p_state[:, :D] + p_intra[:, :D]
        den = w_st * p_state[:, D:] + p_intra[:, D:]
        h_ref[ci, pl.ds(r0, L), :] = num / jnp.maximum(jnp.abs(den), jnp.exp(-m_t))
        m_new = jnp.maximum(tot + m, amax + tot)
        ws = jnp.exp(a_row + tot - m_new)
        wc = jnp.exp(tot + m - m_new)
        c_ref[ci] = jnp.concatenate([wc, wc], axis=1) * cext + _dot((kst * ws).astype(BF16), v_ext)
        return m_new

    def body(i, carry):
        return tuple(step(i if d == 0 else nc - 1 - i, hh, d, m) for (hh, d), m in zip(chains, carry))

    m_fin = lax.fori_loop(0, nc, body, m_init)

    for hh in range(hb):
        hm = h_ref[2 * hh] + h_ref[2 * hh + 1]
        hl = slice(hh * L, (hh + 1) * L)
        out_ref[:, hl] = _rms(hm, onorm_ref[hh]) * jax.nn.sigmoid(og_ref[:, hl])

    if emit_state:
        for k, (hh, d) in enumerate(chains):
            cext = c_ref[2 * hh + d]
            cT_ref[0, d, hh] = cext[:, :D]
            nT_ref[0, hh, d:d + 1, :] = cext[:, D:].T[0:1, :]
            mT_ref[0, hh, d:d + 1, :] = m_fin[k]


def _mlstm(ob, okt, brow, onorm, tri, *, n, nseq, init=None, emit_state=False):
    L = MLSTM_CHUNK
    nc = n // L
    H = B_HEADS
    hb = MLSTM_HEADS_PER_STEP
    ng = H // hb
    col = lambda part: (lambda b, g: (b, part * ng + g))
    gate_blk0 = B_WIDTH // (16 * hb)
    in_specs = [pl.BlockSpec((n, hb * L), col(0)), pl.BlockSpec((n, hb * L), col(1)),
                pl.BlockSpec((n, hb * L), col(2)),
                pl.BlockSpec((nc, hb * L, L), lambda b, g: (b, g, 0)),
                pl.BlockSpec((nc, 16 * hb, L), lambda b, g: (b, gate_blk0 + g, 0)),
                pl.BlockSpec((hb, 16, L), lambda b, g: (g, 0, 0)),
                pl.BlockSpec((hb, 1, L), lambda b, g: (g, 0, 0)),
                pl.BlockSpec(tri.shape, lambda b, g: (0, 0, 0))]
    args = [ob, ob, ob, okt, okt, brow, onorm, tri]
    if init is not None:
        in_specs += [pl.BlockSpec((1, 2, hb, L, 2 * L), lambda b, g: (b, 0, g, 0, 0)),
                     pl.BlockSpec((1, hb, 2, L), lambda b, g: (b, g, 0, 0))]
        args += list(init)
    out_specs = [pl.BlockSpec((n, hb * L), lambda b, g: (b, g))]
    out_shape = [jax.ShapeDtypeStruct((nseq * n, B_WIDTH), F32)]
    if emit_state:
        out_specs += [pl.BlockSpec((1, 2, hb, L, L), lambda b, g: (b, 0, g, 0, 0)),
                      pl.BlockSpec((1, hb, 2, L), lambda b, g: (b, g, 0, 0)),
                      pl.BlockSpec((1, hb, 2, L), lambda b, g: (b, g, 0, 0))]
        out_shape += [jax.ShapeDtypeStruct((nseq, 2, H, L, L), F32),
                      jax.ShapeDtypeStruct((nseq, H, 2, L), F32),
                      jax.ShapeDtypeStruct((nseq, H, 2, L), F32)]
    return pl.pallas_call(
        functools.partial(_mlstm_kernel, n=n, has_init=init is not None, emit_state=emit_state),
        grid=(nseq, ng),
        in_specs=in_specs,
        out_specs=out_specs,
        out_shape=out_shape,
        scratch_shapes=[pltpu.VMEM((2 * hb, n, L), F32),
                        pltpu.VMEM((hb, 5, nc, 8, L), F32),
                        pltpu.VMEM((2 * hb, L, 2 * L), F32)],
        compiler_params=_cparams(("arbitrary", "arbitrary")),
        name="mlstm_init" if init is not None else "mlstm",
    )(*args)


def _router(logits):
    lane = _lane(logits.shape).astype(F32)
    big = 1e9
    gl = jnp.where(lane < N_GROUPS, logits, -jnp.inf)
    gmax = gl.max(axis=-1, keepdims=True)
    g_sel = jnp.where(gl == gmax, lane, big).min(axis=-1, keepdims=True)
    g_prob = 1.0 / jnp.exp(gl - gmax).sum(axis=-1, keepdims=True)
    lo = N_GROUPS + EXPERTS_PER_GROUP * g_sel
    el = jnp.where(lane >= lo, jnp.where(lane < lo + EXPERTS_PER_GROUP, logits, -jnp.inf), -jnp.inf)
    v1 = el.max(axis=-1, keepdims=True)
    i1 = jnp.where(el == v1, lane, big).min(axis=-1, keepdims=True)
    el2 = jnp.where(lane == i1, -jnp.inf, el)
    v2 = el2.max(axis=-1, keepdims=True)
    i2 = jnp.where(el2 == v2, lane, big).min(axis=-1, keepdims=True)
    e2 = jnp.exp(v2 - v1)
    w1 = g_prob / (1.0 + e2)
    w2 = g_prob * e2 / (1.0 + e2)
    return i1, i2, w1, w2


def _read_tokens(refs, is_prompt):
    if len(refs) == 1:
        return refs[0][...]
    return jnp.where(is_prompt, refs[0][...], refs[1][...])


def _post_kernel(*refs, groups):
    it = iter(refs)
    tok_refs = [[next(it) for _ in range(n)] for n in groups]
    w_refs = [next(it) for _ in range(len(groups) - 1)]
    mod_ref, g_ref, wr_ref, br_ref, ls_ref, sel_ref = [next(it) for _ in range(6)]
    xnew_ref, xn_ref, rt_ref, rtt_ref, cnt_ref = [next(it) for _ in range(5)]
    run_ref = next(it)
    i = pl.program_id(0)
    tm = xnew_ref.shape[0]
    is_prompt = i * tm < T_PROMPT
    r = _mod_row(i * tm)
    acc = None
    for a_refs, w_ref in zip(tok_refs[1:], w_refs):
        d = _dot(_read_tokens(a_refs, is_prompt).astype(BF16), w_ref[...])
        acc = d if acc is None else acc + d
    xnew = _read_tokens(tok_refs[0], is_prompt) + _mod_part(mod_ref, r, 2) * acc
    xnew_ref[...] = xnew
    xn = _rms(xnew, g_ref[...]) * (1.0 + _mod_part(mod_ref, r, 4)) + _mod_part(mod_ref, r, 3)
    _to_token_tiles(xn_ref, xn)
    x_hi, x_lo = _split2(xn)
    both = _dot(x_hi, wr_ref[...])
    logits = both[:, :LANES] + both[:, LANES:] + _dot(x_lo, wr_ref[:, :LANES]) + br_ref[...]
    i1, i2, w1, w2 = _router(logits)

    @pl.when(i == 0)
    def _():
        run_ref[...] = jnp.zeros(run_ref.shape, F32)

    lane = _lane(logits.shape).astype(F32)
    member = jnp.where(lane == i1, 1.0, jnp.where(lane == i2, 1.0, 0.0))
    before = _dot(ls_ref[...], member.astype(BF16)) + run_ref[...]
    rank1 = jnp.where(lane == i1, before, 0.0).sum(axis=-1, keepdims=True)
    rank2 = jnp.where(lane == i2, before, 0.0).sum(axis=-1, keepdims=True)
    run_ref[...] = run_ref[...] + member.sum(axis=0, keepdims=True)
    cnt_ref[...] = run_ref[...]
    cols = (i1 - N_GROUPS, i2 - N_GROUPS, w1, w2, rank1, rank2)
    rt = jnp.zeros(logits.shape, F32)
    for k, c in enumerate(cols):
        rt = jnp.where(lane == k, c, rt)
    rt_ref[...] = rt
    hi, mid, lo = _split3(rt)
    sel = sel_ref[...]
    rtt_ref[...] = _dot_nt(sel, hi) + _dot_nt(sel, mid) + _dot_nt(sel, lo)


def _post(tok_ops, w_list, mod, g, wr, br):
    t = T_ALL
    tm = TOK_BLOCK
    npb = T_PROMPT // tm
    full = lambda a: pl.BlockSpec(a.shape, lambda i: (0,) * a.ndim)
    specs, args, groups = [], [], []
    for op in tok_ops:
        if isinstance(op, tuple):
            w = op[0].shape[1]
            specs += [pl.BlockSpec((tm, w), lambda i: (jnp.minimum(i, npb - 1), 0)),
                      pl.BlockSpec((tm, w), lambda i: (jnp.maximum(i - npb, 0), 0))]
            args += list(op)
            groups.append(2)
        else:
            specs.append(pl.BlockSpec((tm, op.shape[1]), lambda i: (i, 0)))
            args.append(op)
            groups.append(1)
    idx = np.arange(tm)
    ls = jnp.asarray(idx[:, None] > idx[None, :], BF16)
    sel = jnp.asarray(np.arange(8)[:, None] == np.arange(LANES)[None, :], BF16)
    consts = [mod, g, wr, br, ls, sel]
    return pl.pallas_call(
        functools.partial(_post_kernel, groups=tuple(groups)),
        grid=(t // tm,),
        in_specs=specs + [full(w) for w in w_list] + [full(a) for a in consts],
        out_specs=[pl.BlockSpec((tm, D_MODEL), lambda i: (i, 0)),
                   pl.BlockSpec((tm * SLABS, LANES), lambda i: (i, 0)),
                   pl.BlockSpec((tm, LANES), lambda i: (i, 0)),
                   pl.BlockSpec((8, tm), lambda i: (0, i)),
                   pl.BlockSpec((1, LANES), lambda i: (0, 0))],
        out_shape=[jax.ShapeDtypeStruct((t, D_MODEL), F32),
                   jax.ShapeDtypeStruct((t * SLABS, LANES), F32),
                   jax.ShapeDtypeStruct((t, LANES), F32),
                   jax.ShapeDtypeStruct((8, t), F32),
                   jax.ShapeDtypeStruct((1, LANES), F32)],
        scratch_shapes=[pltpu.VMEM((1, LANES), F32)],
        compiler_params=_cparams(("arbitrary",)),
        name="post_mixer_router",
    )(*args, *w_list, *consts)


def _expert_kernel(be_ref, st_ref, nv_ref, xn_hbm, w1_ref, w3_ref, w2_ref, o_ref, xa, xb, sem, w1b, w3b, w2b):
    i = pl.program_id(0)
    nv = nv_ref[0]
    active = i < nv
    rows = EXPERT_ROWS

    def gather(blk, buf, s):
        for r in range(rows):
            tok = st_ref[blk * rows + r]
            pltpu.make_async_copy(xn_hbm.at[pl.ds(pl.multiple_of(tok * SLABS, SLABS), SLABS), :],
                                  buf.at[pl.ds(r * SLABS, SLABS), :], sem.at[s]).start()

    def wait(buf, s):
        pltpu.make_async_copy(xn_hbm.at[pl.ds(0, rows * SLABS), :], buf, sem.at[s]).wait()

    @pl.when(i == 0)
    def _():
        gather(0, xa, 0)

    changed = jnp.logical_or(i == 0, be_ref[i] != be_ref[jnp.maximum(i - 1, 0)])

    @pl.when(jnp.logical_and(changed, active))
    def _():
        w1b[...] = w1_ref[0, 0].astype(BF16)
        w3b[...] = w3_ref[0, 0].astype(BF16)
        w2b[...] = w2_ref[0, 0].astype(BF16)

    nxt = jnp.minimum(i + 1, nv - 1)

    def step(cur, s_cur, oth, s_oth):
        wait(cur, s_cur)
        gather(nxt, oth, s_oth)
        x = _from_token_tiles(cur, 0, rows).astype(BF16)
        h1 = _dot(x, w1b[...])
        h3 = _dot(x, w3b[...])
        hid = (h1 * jax.nn.sigmoid(h1)) * h3
        _to_token_tiles(o_ref, _dot(hid.astype(BF16), w2b[...]))

    @pl.when(jnp.logical_and(active, i % 2 == 0))
    def _():
        step(xa, 0, xb, 1)

    @pl.when(jnp.logical_and(active, i % 2 == 1))
    def _():
        step(xb, 1, xa, 0)

    @pl.when(jnp.logical_not(active))
    def _():
        o_ref[...] = jnp.zeros(o_ref.shape, F32)

    @pl.when(i == nv - 1)
    def _():
        @pl.when(i % 2 == 0)
        def _():
            wait(xb, 1)

        @pl.when(i % 2 == 1)
        def _():
            wait(xa, 0)


def _experts(block_expert, slot_tok, n_valid, xn, w1, w3, w2, layer):
    nblk = block_expert.shape[0]
    rows = EXPERT_ROWS
    grid_spec = pltpu.PrefetchScalarGridSpec(
        num_scalar_prefetch=3,
        grid=(nblk,),
        in_specs=[pl.BlockSpec(memory_space=pl.ANY),
                  pl.BlockSpec((1, 1, D_MODEL, D_EXPERT), lambda i, be, st, nv: (layer, be[i], 0, 0)),
                  pl.BlockSpec((1, 1, D_MODEL, D_EXPERT), lambda i, be, st, nv: (layer, be[i], 0, 0)),
                  pl.BlockSpec((1, 1, D_EXPERT, D_MODEL), lambda i, be, st, nv: (layer, be[i], 0, 0))],
        out_specs=pl.BlockSpec((rows * SLABS, LANES), lambda i, be, st, nv: (i, 0)),
        scratch_shapes=[pltpu.VMEM((rows * SLABS, LANES), F32),
                        pltpu.VMEM((rows * SLABS, LANES), F32),
                        pltpu.SemaphoreType.DMA((2,)),
                        pltpu.VMEM((D_MODEL, D_EXPERT), BF16),
                        pltpu.VMEM((D_MODEL, D_EXPERT), BF16),
                        pltpu.VMEM((D_EXPERT, D_MODEL), BF16)])
    return pl.pallas_call(
        _expert_kernel,
        grid_spec=grid_spec,
        out_shape=jax.ShapeDtypeStruct((nblk * rows * SLABS, LANES), F32),
        compiler_params=_cparams(("arbitrary",)),
        name="moe_experts",
    )(block_expert, slot_tok, n_valid, xn, w1, w3, w2)


_COMBINE_BLOCK = 128


def _combine_kernel(dest_ref, yb_hbm, x_ref, rt_ref, mod_ref, gfin_ref, o_ref, ybuf, sem, *, final_norm, row0):
    i = pl.program_id(0)
    nblk = pl.num_programs(0)
    slot = i % 2
    tm = _COMBINE_BLOCK

    def issue(blk, s):
        def body(j, carry):
            for c in range(2):
                d = dest_ref[c * T_ALL + row0 + blk * tm + j]
                pltpu.make_async_copy(yb_hbm.at[pl.ds(pl.multiple_of(d * SLABS, SLABS), SLABS), :],
                                      ybuf.at[s, pl.ds(pl.multiple_of((c * tm + j) * SLABS, SLABS), SLABS), :],
                                      sem.at[s]).start()
            return carry
        lax.fori_loop(0, tm, body, 0, unroll=4)

    @pl.when(i == 0)
    def _():
        issue(0, 0)

    @pl.when(i + 1 < nblk)
    def _():
        issue(i + 1, 1 - slot)

    pltpu.make_async_copy(yb_hbm.at[pl.ds(0, 2 * tm * SLABS), :], ybuf.at[slot], sem.at[slot]).wait()
    r = _mod_row(row0 + i * tm)
    rt = rt_ref[...]
    yv = ybuf.at[slot]
    y = rt[:, 2:3] * _from_token_tiles(yv, 0, tm) + rt[:, 3:4] * _from_token_tiles(yv, tm, tm)
    out = x_ref[...] + _mod_part(mod_ref, r, 5) * y
    if final_norm:
        out = _rms(out, gfin_ref[...])
    o_ref[...] = out


def _combine(dest, yb, x, rt, mod, gfin, final_norm, row0=0, t=T_ALL):
    tm = _COMBINE_BLOCK
    blk0 = row0 // tm
    grid_spec = pltpu.PrefetchScalarGridSpec(
        num_scalar_prefetch=1,
        grid=(t // tm,),
        in_specs=[pl.BlockSpec(memory_space=pl.ANY),
                  pl.BlockSpec((tm, D_MODEL), lambda i, d: (blk0 + i, 0)),
                  pl.BlockSpec((tm, LANES), lambda i, d: (blk0 + i, 0)),
                  pl.BlockSpec(mod.shape, lambda i, d: (0, 0)),
                  pl.BlockSpec(gfin.shape, lambda i, d: (0, 0))],
        out_specs=pl.BlockSpec((tm, D_MODEL), lambda i, d: (i, 0)),
        scratch_shapes=[pltpu.VMEM((2, 2 * tm * SLABS, LANES), F32),
                        pltpu.SemaphoreType.DMA((2,))])
    return pl.pallas_call(
        functools.partial(_combine_kernel, final_norm=final_norm, row0=row0),
        grid_spec=grid_spec,
        out_shape=jax.ShapeDtypeStruct((t, D_MODEL), F32),
        compiler_params=_cparams(("arbitrary",)),
        name="moe_combine",
    )(dest, yb, x, rt, mod, gfin)


def _combine_proj_kernel(dest_ref, yb_hbm, x_ref, rt_ref, mod0_ref, mod1_ref, g_ref, w_ref,
                         xo_ref, q_ref, k_ref, v_ref, ya, yb, sem, *, row0):
    i = pl.program_id(0)
    nblk = pl.num_programs(0)
    tm = x_ref.shape[0]

    def gather(blk, buf, s):
        for j in range(tm):
            for c in range(2):
                d = dest_ref[c * T_ALL + row0 + blk * tm + j]
                pltpu.make_async_copy(yb_hbm.at[pl.ds(pl.multiple_of(d * SLABS, SLABS), SLABS), :],
                                      buf.at[pl.ds((c * tm + j) * SLABS, SLABS), :], sem.at[s]).start()

    def wait(buf, s):
        pltpu.make_async_copy(yb_hbm.at[pl.ds(0, 2 * tm * SLABS), :], buf, sem.at[s]).wait()

    @pl.when(i == 0)
    def _():
        gather(0, ya, 0)

    nxt = jnp.minimum(i + 1, nblk - 1)

    def step(cur, s_cur, oth, s_oth):
        wait(cur, s_cur)
        gather(nxt, oth, s_oth)
        r = _mod_row(row0 + i * tm)
        rt = rt_ref[...]
        y = rt[:, 2:3] * _from_token_tiles(cur, 0, tm) + rt[:, 3:4] * _from_token_tiles(cur, tm, tm)
        x = x_ref[...] + _mod_part(mod0_ref, r, 5) * y
        xo_ref[...] = x
        h = _rms(x, g_ref[...]) * (1.0 + _mod_part(mod1_ref, r, 1)) + _mod_part(mod1_ref, r, 0)
        hb = h.astype(BF16)
        for j, o_ref in enumerate((q_ref, k_ref, v_ref)):
            o_ref[...] = _dot(hb, w_ref[:, j * C_WIDTH:(j + 1) * C_WIDTH])

    @pl.when(i % 2 == 0)
    def _():
        step(ya, 0, yb, 1)

    @pl.when(i % 2 == 1)
    def _():
        step(yb, 1, ya, 0)

    @pl.when(i == nblk - 1)
    def _():
        @pl.when(i % 2 == 0)
        def _():
            wait(yb, 1)

        @pl.when(i % 2 == 1)
        def _():
            wait(ya, 0)


def _combine_proj(dest, yb, x, rt, mod0, mod1, g, w, row0, t):
    tm = TOK_BLOCK
    blk0 = row0 // tm
    full = lambda a: pl.BlockSpec(a.shape, lambda i, d: (0,) * a.ndim)
    grid_spec = pltpu.PrefetchScalarGridSpec(
        num_scalar_prefetch=1,
        grid=(t // tm,),
        in_specs=[pl.BlockSpec(memory_space=pl.ANY),
                  pl.BlockSpec((tm, D_MODEL), lambda i, d: (blk0 + i, 0)),
                  pl.BlockSpec((tm, LANES), lambda i, d: (blk0 + i, 0)),
                  full(mod0), full(mod1), full(g), full(w)],
        out_specs=[pl.BlockSpec((tm, D_MODEL), lambda i, d: (i, 0))] * 4,
        scratch_shapes=[pltpu.VMEM((2 * tm * SLABS, LANES), F32),
                        pltpu.VMEM((2 * tm * SLABS, LANES), F32),
                        pltpu.SemaphoreType.DMA((2,))])
    return pl.pallas_call(
        functools.partial(_combine_proj_kernel, row0=row0),
        grid_spec=grid_spec,
        out_shape=[jax.ShapeDtypeStruct((t, D_MODEL), F32)] * 4,
        compiler_params=_cparams(("arbitrary",)),
        name="moe_combine_proj",
    )(dest, yb, x, rt, mod0, mod1, g, w)


def _slot_table_kernel(dest_ref, pad_ref, o_ref):
    n_pairs = dest_ref.shape[0]
    mask = T_ALL - 1

    def fill_range(e, carry):
        def fill(s, c):
            o_ref[s] = s & mask
            return c
        lax.fori_loop(pad_ref[0, e], pad_ref[1, e], fill, 0)
        return carry

    lax.fori_loop(0, pad_ref.shape[1], fill_range, 0)

    def place(a, carry):
        o_ref[dest_ref[a]] = a & mask
        return carry

    lax.fori_loop(0, n_pairs, place, 0, unroll=8)


def _slot_table(dest, pad_ranges, n_slots):
    smem = pl.BlockSpec(memory_space=pltpu.SMEM)
    return pl.pallas_call(
        _slot_table_kernel,
        in_specs=[smem, smem],
        out_specs=smem,
        out_shape=jax.ShapeDtypeStruct((n_slots,), jnp.int32),
        name="moe_slot_table",
    )(dest, pad_ranges)


def _moe_plan(rtt, cnt):
    t = rtt.shape[1]
    eid = rtt[0:2].astype(jnp.int32)
    rank = rtt[4:6].astype(jnp.int32)
    counts = cnt[0, N_GROUPS:N_GROUPS + N_EXPERTS].astype(jnp.int32)
    padded = (counts + EXPERT_ROWS - 1) // EXPERT_ROWS * EXPERT_ROWS
    seg_end = jnp.cumsum(padded)
    seg_start = seg_end - padded
    experts = jnp.arange(N_EXPERTS, dtype=jnp.int32)
    start = jnp.sum(jnp.where(eid[..., None] == experts, seg_start, 0), axis=-1)
    dest = (start + rank).reshape(-1)
    n_blocks = (2 * t + N_EXPERTS * (EXPERT_ROWS - 1) + EXPERT_ROWS - 1) // EXPERT_ROWS
    n_slots = n_blocks * EXPERT_ROWS
    pad_lo = jnp.concatenate([seg_start + counts, seg_end[-1:]])
    pad_hi = jnp.concatenate([seg_end, jnp.full((1,), n_slots, jnp.int32)])
    slot_tok = _slot_table(dest, jnp.stack([pad_lo, pad_hi]).astype(jnp.int32), n_slots)
    first_row = jnp.arange(n_blocks, dtype=jnp.int32) * EXPERT_ROWS
    block_expert = jnp.minimum(jnp.sum((seg_end[None, :] <= first_row[:, None]).astype(jnp.int32), axis=1),
                               N_EXPERTS - 1)
    n_valid = (seg_end[-1:] // EXPERT_ROWS).astype(jnp.int32)
    return dest, slot_tok, block_expert, n_valid


def _attn_c_prompt_kernel(q_ref, k_ref, v_ref, o_ref):
    for p in range(C_HEADS // 2):
        sl = slice(p * LANES, (p + 1) * LANES)
        kb = k_ref[:, sl].astype(BF16)
        vb = v_ref[:, sl].astype(BF16)
        qp = q_ref[:, sl] * (C_HEAD_DIM ** -0.5)
        o_ref[:, sl] = _softmax_pair(qp, lambda qm: [_dot_nt(qm, kb)], lambda ps: _dot(ps[0], vb))


def _attn_c_prompt(q, k, v):
    blk = pl.BlockSpec((SEQ, C_WIDTH), lambda b: (b, 0))
    return pl.pallas_call(
        _attn_c_prompt_kernel,
        grid=(BATCH,),
        in_specs=[blk, blk, blk],
        out_specs=blk,
        out_shape=jax.ShapeDtypeStruct((T_PROMPT, C_WIDTH), F32),
        compiler_params=_cparams(("arbitrary",)),
        name="attn_c_prompt",
    )(q, k, v)


def _na_key_start(r0):
    rows = DEC_SEQ // GRID_W
    return jnp.minimum(jnp.clip(r0 - NA_ROWS // 2, 0, rows - NA_ROWS), rows - NA_KROWS)


def _na_block_plan():
    rows = DEC_SEQ // GRID_W
    nblk = rows // NA_QROWS
    plan = []
    for blk in (0, 1, nblk - 1):
        r0 = blk * NA_QROWS
        ks = min(int(np.clip(r0 - NA_ROWS // 2, 0, rows - NA_ROWS)), rows - NA_KROWS)
        per_row = []
        for i in range(NA_QROWS):
            r = r0 + i
            rs = int(np.clip(r - NA_ROWS // 2, 0, rows - NA_ROWS))
            start = ks - r + NA_ROWS - 1 + NA_KROWS
            ok = [rs <= ks + j < rs + NA_ROWS for j in range(NA_KROWS)]
            per_row.append((start, ok))
        plan.append(per_row)
    return plan


def _attn_c_sample_kernel(q_ref, k_ref, v_ref, ck_ref, cv_ref, toe_ref, o_ref, bias_ref):
    rows = DEC_SEQ // GRID_W
    nblk = rows // NA_QROWS
    w = GRID_W

    @pl.when(jnp.logical_and(pl.program_id(1) == 0, pl.program_id(2) == 0))
    def _():
        neg = jnp.full((w, w), NEG, F32)
        for t, per_row in enumerate(_na_block_plan()):
            for half in range(2):
                for i, (start, ok) in enumerate(per_row):
                    for j in range(0, NA_KROWS, 2):
                        pieces = [toe_ref[0, half, start + jj] if ok[jj] else neg for jj in (j, j + 1)]
                        bias_ref[t, half, i * w:(i + 1) * w, j * w:(j + 2) * w] = jnp.concatenate(pieces, axis=1)

    i = pl.program_id(2)
    r0 = i * NA_QROWS
    k0 = pl.multiple_of(_na_key_start(r0) * GRID_W, GRID_W)
    btype = jnp.where(i == 0, 0, jnp.where(i == nblk - 1, 2, 1))
    nk = NA_KROWS * GRID_W
    kw = k_ref[pl.ds(k0, nk), :].astype(BF16)
    vw = v_ref[pl.ds(k0, nk), :].astype(BF16)
    kc = ck_ref[0].astype(BF16)
    vc = cv_ref[0].astype(BF16)
    qp = q_ref[...] * (C_HEAD_DIM ** -0.5)
    lo = _lane(qp.shape) < 64
    outs = []
    for half in range(2):
        qm = jnp.where(lo if half == 0 else jnp.logical_not(lo), qp, 0.0).astype(BF16)
        s_win = _dot_nt(qm, kw) + bias_ref[btype, half]
        s_ctx = _dot_nt(qm, kc)
        m = jnp.maximum(s_win.max(axis=-1, keepdims=True), s_ctx.max(axis=-1, keepdims=True))
        e_win = jnp.exp(s_win - m)
        e_ctx = jnp.exp(s_ctx - m)
        l = e_win.sum(axis=-1, keepdims=True) + e_ctx.sum(axis=-1, keepdims=True)
        o = _dot(e_win.astype(BF16), vw) + _dot(e_ctx.astype(BF16), vc)
        outs.append(o / l)
    o_ref[...] = jnp.where(lo, outs[0], outs[1])


def _attn_c_sample(q, k, v, ck, cv, toe):
    rows = DEC_SEQ // GRID_W
    nblk = rows // NA_QROWS
    qrows = NA_QROWS * GRID_W
    npair = C_HEADS // 2
    return pl.pallas_call(
        _attn_c_sample_kernel,
        grid=(npair, DEC_BATCH, nblk),
        in_specs=[pl.BlockSpec((qrows, LANES), lambda p, b, i: (b * nblk + i, p)),
                  pl.BlockSpec((DEC_SEQ, LANES), lambda p, b, i: (b, p)),
                  pl.BlockSpec((DEC_SEQ, LANES), lambda p, b, i: (b, p)),
                  pl.BlockSpec((1, PAST_LEN, LANES), lambda p, b, i: (b, 0, p)),
                  pl.BlockSpec((1, PAST_LEN, LANES), lambda p, b, i: (b, 0, p)),
                  pl.BlockSpec((1,) + toe.shape[1:], lambda p, b, i: (p, 0, 0, 0, 0))],
        out_specs=pl.BlockSpec((qrows, LANES), lambda p, b, i: (b * nblk + i, p)),
        out_shape=jax.ShapeDtypeStruct((T_SAMPLE, C_WIDTH), F32),
        scratch_shapes=[pltpu.VMEM((3, 2, qrows, NA_KROWS * GRID_W), F32)],
        compiler_params=_cparams(("arbitrary", "arbitrary", "arbitrary")),
        name="attn_c_sample",
    )(q, k, v, ck, cv, toe)


def _na_toeplitz(rpb):
    w = GRID_W
    nd_r, nd_c = 2 * NA_ROWS - 1, 2 * NA_COLS - 1
    c = np.arange(w)
    cs = np.clip(c - NA_COLS // 2, 0, w - NA_COLS)
    col_ok = (c[None, :] >= cs[:, None]) & (c[None, :] < cs[:, None] + NA_COLS)
    dcol = c[None, :] - c[:, None] + NA_COLS - 1
    onehot = (np.arange(nd_c)[:, None, None] == dcol[None]).reshape(nd_c, w * w)
    toe = jnp.dot(rpb.reshape(C_HEADS * nd_r, nd_c), jnp.asarray(onehot, F32), precision=lax.Precision.HIGHEST)
    toe = jnp.where(col_ok[None, None], toe.reshape(C_HEADS, nd_r, w, w), NEG)
    toe = jnp.pad(toe, ((0, 0), (NA_KROWS, NA_KROWS), (0, 0), (0, 0)), constant_values=NEG)
    return toe.reshape(C_HEADS // 2, 2, nd_r + 2 * NA_KROWS, w, w)


def _rope_tables():
    half = A_HEAD_DIM // 2
    t = jnp.arange(DEC_SEQ)
    row = (t // GRID_W).astype(F32)
    colp = (t % GRID_W).astype(F32)
    freqs = 1.0 / (ROPE_BASE ** (jnp.arange(0, half, 2, dtype=F32) / half))
    d = np.arange(LANES) % A_HEAD_DIM
    pos = jnp.where(jnp.asarray(d < half)[None, :], row[:, None], colp[:, None])
    ang = pos * freqs[d % (half // 2)][None, :]
    sign = jnp.asarray(np.where((d % half) < half // 2, -1.0, 1.0), F32)[None, :]
    return jnp.cos(ang), jnp.sin(ang) * sign


def _head_avg_matrix():
    idx = np.arange(LANES) // A_HEAD_DIM
    return jnp.asarray((idx[:, None] == idx[None, :]).astype(np.float32) / A_HEAD_DIM, BF16)


def _tri_matrices():
    i = np.arange(MLSTM_CHUNK)
    upper = (i[:, None] <= i[None, :]).astype(np.float32)
    lower = (i[:, None] >= i[None, :]).astype(np.float32)
    return jnp.asarray(np.stack([upper, lower]), BF16)


def _router_weights(wg, bg, we, be):
    w = jnp.zeros((D_MODEL, LANES), F32).at[:, :N_GROUPS].set(wg).at[:, N_GROUPS:N_GROUPS + N_EXPERTS].set(we)
    b = jnp.zeros((1, LANES), F32).at[0, :N_GROUPS].set(bg).at[0, N_GROUPS:N_GROUPS + N_EXPERTS].set(be)
    hi = w.astype(BF16)
    lo = (w - hi.astype(F32)).astype(BF16)
    return jnp.concatenate([hi, lo], axis=1), b


def _moe(xn, rtt, cnt, w1, w3, w2, layer):
    dest, slot_tok, block_expert, n_valid = _moe_plan(rtt, cnt)
    return dest, _experts(block_expert, slot_tok, n_valid, xn, w1, w3, w2, layer)


def kernel(x_prompt, x_sample, cache_attn_k, cache_attn_v, state_mlstm_C, state_mlstm_n, state_mlstm_m,
           cache_na_k, cache_na_v, c, c_ctx, norm_mix, norm_ffn, norm_final, ada_w, ada_b,
           ab_w_in, ab_w_out, ab_q_norm, ab_k_norm, ab_gate_bias, ab_out_norm,
           na_w_in, na_w_out, na_rpb, moe_wg, moe_bg, moe_we, moe_be, moe_w1, moe_w3, moe_w2):
    xp = x_prompt.reshape(T_PROMPT, D_MODEL)
    xs = x_sample.reshape(T_SAMPLE, D_MODEL)
    cond =jnp.zeros((N_COND, D_MODEL), F32).at[0].set(c_ctx).at[1:1 + DEC_BATCH].set(c)
    mod = _modulation(cond, ada_w, ada_b)
    gfin = norm_final.reshape(1, D_MODEL)

    w_in = ab_w_in[0]
    o_aq, o_ak, o_av, o_bq, o_bk, o_bv, o_bo, o_bg = np.cumsum((0,) + (A_WIDTH, A_KV_WIDTH, A_KV_WIDTH,
                                                                       B_WIDTH, B_WIDTH, B_WIDTH, B_WIDTH))
    wb = jnp.concatenate([w_in[:, o_bq:o_bk], w_in[:, o_bv:o_bg]], axis=1).astype(BF16)
    wq = w_in[:, o_aq:o_ak].astype(BF16)
    wkv = w_in[:, o_ak:o_bq].astype(BF16)
    gate_rows = np.array([0, 8, 1, 9])
    wg = w_in[:, o_bg:o_bg + 4 * B_HEADS].reshape(D_MODEL, 4, B_HEADS)
    wgt = jnp.zeros((B_HEADS, 16, D_MODEL), F32).at[:, gate_rows, :].set(wg.transpose(2, 1, 0))
    wgt = jnp.concatenate([w_in[:, o_bk:o_bv].T, wgt.reshape(16 * B_HEADS, D_MODEL)], axis=0).astype(BF16)
    g_mix = norm_mix[0].reshape(1, D_MODEL)
    ob_p, oq_p, okv_p, ogt_p = _proj_ab(xp, mod[0], g_mix, wb, wq, wkv, wgt, 0)
    ob_s, oq_s, okv_s, ogt_s = _proj_ab(xs, mod[0], g_mix, wb, wq, wkv, wgt, T_PROMPT)

    qn = jnp.tile(ab_q_norm[0], 2).reshape(1, LANES)
    kn = jnp.tile(ab_k_norm[0], 2).reshape(1, LANES)
    bd = _head_avg_matrix()
    cos, sin = _rope_tables()
    a_p, new_k, new_v = _attn_a_prompt(oq_p, okv_p, qn, kn, bd)
    ck = cache_attn_k[:, 0].reshape(DEC_BATCH, PAST_LEN, A_KV_WIDTH)
    cv = cache_attn_v[:, 0].reshape(DEC_BATCH, PAST_LEN, A_KV_WIDTH)
    a_s = _attn_a_sample(oq_s, okv_s, ck, cv, cos, sin, qn, kn, bd)

    gb = ab_gate_bias[0]
    brow = jnp.zeros((B_HEADS, 16, LANES), F32).at[:, gate_rows, :].set(
        jnp.broadcast_to(gb.T[:, :, None], (B_HEADS, 4, LANES)))
    onorm = ab_out_norm[0].reshape(B_HEADS, 1, B_HEAD_DIM)
    tri = _tri_matrices()
    b_p, cT, nT, mT = _mlstm(ob_p, ogt_p, brow, onorm, tri, n=SEQ, nseq=BATCH, emit_state=True)
    n0 = jnp.broadcast_to(state_mlstm_n[:, 0][..., None], state_mlstm_C[:, 0].shape)
    c0 = jnp.concatenate([state_mlstm_C[:, 0], n0], axis=-1)
    m0 = jnp.broadcast_to(state_mlstm_m[:, 0].transpose(0, 2, 1)[..., None], (DEC_BATCH, B_HEADS, 2, LANES))
    (b_s,) = _mlstm(ob_s, ogt_s, brow, onorm, tri, n=DEC_SEQ, nseq=DEC_BATCH, init=(c0, m0))

    w_out = ab_w_out[0].astype(BF16)
    wr, br = _router_weights(moe_wg[0], moe_bg[0], moe_we[0], moe_be[0])
    x1, xn, rt, rtt, cnt = _post([(xp, xs), (a_p, a_s), (b_p, b_s)], [w_out[:A_WIDTH], w_out[A_WIDTH:]], mod[0],
                                 norm_ffn[0].reshape(1, D_MODEL), wr, br)
    dest, yb = _moe(xn, rtt, cnt, moe_w1, moe_w3, moe_w2, 0)

    g_mix = norm_mix[1].reshape(1, D_MODEL)
    w_in = na_w_in[0].astype(BF16)
    x_p, q_p, k_p, v_p = _combine_proj(dest, yb, x1, rt, mod[0], mod[1], g_mix, w_in, 0, T_PROMPT)
    x_s, q_s, k_s, v_s = _combine_proj(dest, yb, x1, rt, mod[0], mod[1], g_mix, w_in, T_PROMPT, T_SAMPLE)
    o_p = _attn_c_prompt(q_p, k_p, v_p)
    nck = cache_na_k[:, 0].reshape(DEC_BATCH, PAST_LEN, C_WIDTH)
    ncv = cache_na_v[:, 0].reshape(DEC_BATCH, PAST_LEN, C_WIDTH)
    o_s = _attn_c_sample(q_s, k_s, v_s, nck, ncv, _na_toeplitz(na_rpb[0]))
    wr, br = _router_weights(moe_wg[1], moe_bg[1], moe_we[1], moe_be[1])
    x1, xn, rt, rtt, cnt = _post([(x_p, x_s), (o_p, o_s)], [na_w_out[0].astype(BF16)], mod[1],
                                 norm_ffn[1].reshape(1, D_MODEL), wr, br)
    dest, yb = _moe(xn, rtt, cnt, moe_w1, moe_w3, moe_w2, 1)
    y_prompt = _combine(dest, yb, x1, rt, mod[1], gfin, True, 0, T_PROMPT).reshape(BATCH, SEQ, D_MODEL)
    y_sample = _combine(dest, yb, x1, rt, mod[1], gfin, True, T_PROMPT, T_SAMPLE).reshape(DEC_BATCH, DEC_SEQ, D_MODEL)
    new_attn_k = new_k.reshape(BATCH, 1, SEQ, A_KV_HEADS, A_HEAD_DIM)
    new_attn_v = new_v.reshape(BATCH, 1, SEQ, A_KV_HEADS, A_HEAD_DIM)
    new_mlstm_C = cT[:, None]
    new_mlstm_n = nT.transpose(0, 2, 1, 3)[:, None]
    new_mlstm_m = mT[..., 0].transpose(0, 2, 1)[:, None]
    new_na_k = k_p.reshape(BATCH, 1, SEQ, C_HEADS, C_HEAD_DIM)
    new_na_v = v_p.reshape(BATCH, 1, SEQ, C_HEADS, C_HEAD_DIM)
    return (y_prompt, y_sample, new_attn_k, new_attn_v, new_mlstm_C, new_mlstm_n, new_mlstm_m,
            new_na_k, new_na_v)
```

```python
import functools

import numpy as np
import jax
import jax.numpy as jnp
from jax import lax
from jax.experimental import pallas as pl
from jax.experimental.pallas import tpu as pltpu

F32 = jnp.float32
BF16 = jnp.bfloat16

D_MODEL = 1024
BATCH = 32
SEQ = 256
DEC_BATCH = 4
DEC_SEQ = 2048
PAST_LEN = 256
GRID_W = 64
A_HEADS = 8
A_KV_HEADS = 2
A_HEAD_DIM = 64
A_WIDTH = A_HEADS * A_HEAD_DIM
A_KV_WIDTH = A_KV_HEADS * A_HEAD_DIM
B_HEADS = 4
B_HEAD_DIM = 128
B_WIDTH = B_HEADS * B_HEAD_DIM
MLSTM_CHUNK = 128
C_HEADS = 16
C_HEAD_DIM = 64
C_WIDTH = C_HEADS * C_HEAD_DIM
NA_ROWS = 8
NA_COLS = 16
N_GROUPS = 4
EXPERTS_PER_GROUP = 8
N_EXPERTS = N_GROUPS * EXPERTS_PER_GROUP
D_EXPERT = 512
MOE_BLOCK = 128
ROPE_BASE = 10000.0
NORM_EPS = 1e-6

T_PROMPT = BATCH * SEQ
T_SAMPLE = DEC_BATCH * DEC_SEQ
T_ALL = T_PROMPT + T_SAMPLE
N_COND = 8
LANES = 128
SLABS = D_MODEL // LANES
TOK_BLOCK = 256
MLSTM_HEADS_PER_STEP = 2
EXPERT_ROWS = 512
NA_QROWS = 4
NA_KROWS = 12
NEG = -1e30
VMEM_LIMIT = 56 * 1024 * 1024


def _cparams(sem):
    return pltpu.CompilerParams(dimension_semantics=sem, vmem_limit_bytes=VMEM_LIMIT)


def _split2(x):
    hi = x.astype(BF16)
    lo = (x - hi.astype(F32)).astype(BF16)
    return hi, lo


def _split3(x):
    hi = x.astype(BF16)
    r = x - hi.astype(F32)
    mid = r.astype(BF16)
    lo = (r - mid.astype(F32)).astype(BF16)
    return hi, mid, lo


def _dot(a, b):
    return jnp.dot(a, b, preferred_element_type=F32)


def _dot_nt(a, b):
    return lax.dot_general(a, b, (((1,), (1,)), ((), ())), preferred_element_type=F32)


def _dot_exact_rhs(x, b):
    hi, mid, lo = _split3(x)
    return _dot(hi, b) + _dot(mid, b) + _dot(lo, b)


def _rms(x, g):
    ms = jnp.mean(x * x, axis=-1, keepdims=True)
    return (x * lax.rsqrt(ms + NORM_EPS)) * g


def _mod_row(tok0):
    return jnp.where(tok0 < T_PROMPT, 0, 1 + (tok0 - T_PROMPT) // DEC_SEQ)


def _mod_part(mod_ref, r, idx):
    return mod_ref[pl.ds(r, 1), idx * D_MODEL:(idx + 1) * D_MODEL]


def _head_rms(x, w, bd):
    hi, lo = _split2(x * x)
    ms = _dot(hi, bd) + _dot(lo, bd)
    return (x * lax.rsqrt(ms + NORM_EPS)) * w


def _to_token_tiles(ref, x):
    m = x.shape[0]
    for s in range(SLABS):
        ref[pl.ds(s, m, stride=SLABS), :] = x[:, s * LANES:(s + 1) * LANES]


def _from_token_tiles(ref, tile0, m):
    return jnp.concatenate([ref[pl.ds(tile0 * SLABS + s, m, stride=SLABS), :] for s in range(SLABS)], axis=1)


def _lane(shape):
    return lax.broadcasted_iota(jnp.int32, shape, len(shape) - 1)


def _dup_half(x, g):
    xr = pltpu.roll(x, 64, 1)
    lo = _lane(x.shape) < 64
    return jnp.where(lo, x, xr) if g == 0 else jnp.where(lo, xr, x)


def _rope(x, cos, sin_signed):
    lane = _lane(x.shape)
    partner = jnp.where((lane % 32) < 16, pltpu.roll(x, LANES - 16, 1), pltpu.roll(x, 16, 1))
    return x * cos + partner * sin_signed


def _softmax_pair(qp, score_fn, value_fn):
    lo = _lane(qp.shape) < 64
    outs = []
    for half in range(2):
        qm = jnp.where(lo if half == 0 else jnp.logical_not(lo), qp, 0.0).astype(BF16)
        ss = score_fn(qm)
        m = ss[0].max(axis=-1, keepdims=True)
        for s in ss[1:]:
            m = jnp.maximum(m, s.max(axis=-1, keepdims=True))
        es = [jnp.exp(s - m) for s in ss]
        l = es[0].sum(axis=-1, keepdims=True)
        for e in es[1:]:
            l = l + e.sum(axis=-1, keepdims=True)
        o = value_fn([e.astype(BF16) for e in es])
        outs.append(o / l)
    return jnp.where(lo, outs[0], outs[1])


def _mod_kernel(cond_ref, w_ref, b_ref, o_ref):
    c = cond_ref[...]
    s = c * jax.nn.sigmoid(c)
    s_hi, s_lo = _split2(s)
    w_hi, w_lo = _split2(w_ref[0])
    o_ref[0] = _dot(s_hi, w_hi) + _dot(s_lo, w_hi) + _dot(s_hi, w_lo) + b_ref[0]


def _modulation(cond, ada_w, ada_b):
    depth, d, n = ada_w.shape
    tn = 1536
    return pl.pallas_call(
        _mod_kernel,
        grid=(depth, n // tn),
        in_specs=[pl.BlockSpec((N_COND, d), lambda l, j: (0, 0)),
                  pl.BlockSpec((1, d, tn), lambda l, j: (l, 0, j)),
                  pl.BlockSpec((1, 1, tn), lambda l, j: (l, 0, j))],
        out_specs=pl.BlockSpec((1, N_COND, tn), lambda l, j: (l, 0, j)),
        out_shape=jax.ShapeDtypeStruct((depth, N_COND, n), F32),
        compiler_params=_cparams(("arbitrary", "arbitrary")),
        name="adaln_modulation",
    )(cond, ada_w, ada_b.reshape(depth, 1, n))


def _norm_mod(x_ref, mod_ref, g_ref, shift_idx, scale_idx, row0):
    r = _mod_row(row0 + pl.program_id(0) * x_ref.shape[0])
    h = _rms(x_ref[...], g_ref[...])
    return h * (1.0 + _mod_part(mod_ref, r, scale_idx)) + _mod_part(mod_ref, r, shift_idx)


def _proj_ab_kernel(x_ref, mod_ref, g_ref, wb_ref, wq_ref, wkv_ref, wgt_ref,
                    ob_ref, oq_ref, okv_ref, okt_ref, ogt_ref, *, row0):
    tm = x_ref.shape[0]
    hb = _norm_mod(x_ref, mod_ref, g_ref, 0, 1, row0).astype(BF16)
    ob_ref[...] = _dot(hb, wb_ref[...]).astype(ob_ref.dtype)
    oq_ref[...] = _dot(hb, wq_ref[...]).astype(oq_ref.dtype)
    okv_ref[...] = _dot(hb, wkv_ref[...]).astype(okv_ref.dtype)
    gt = _dot_nt(wgt_ref[...], hb)
    for j in range(tm // LANES):
        okt_ref[j] = gt[:B_WIDTH, j * LANES:(j + 1) * LANES].astype(okt_ref.dtype)
        ogt_ref[j] = gt[B_WIDTH:, j * LANES:(j + 1) * LANES]


def _proj_ab(x, mod, g, wb, wq, wkv, wgt, row0):
    t = x.shape[0]
    tm = TOK_BLOCK
    full = lambda a: pl.BlockSpec(a.shape, lambda i: (0,) * a.ndim)
    return pl.pallas_call(
        functools.partial(_proj_ab_kernel, row0=row0),
        grid=(t // tm,),
        in_specs=[pl.BlockSpec((tm, D_MODEL), lambda i: (i, 0)), full(mod), full(g),
                  full(wb), full(wq), full(wkv), full(wgt)],
        out_specs=[pl.BlockSpec((tm, wb.shape[1]), lambda i: (i, 0)),
                   pl.BlockSpec((tm, A_WIDTH), lambda i: (i, 0)),
                   pl.BlockSpec((tm, 2 * A_KV_WIDTH), lambda i: (i, 0)),
                   pl.BlockSpec((tm // LANES, B_WIDTH, LANES), lambda i: (i, 0, 0)),
                   pl.BlockSpec((tm // LANES, wgt.shape[0] - B_WIDTH, LANES), lambda i: (i, 0, 0))],
        out_shape=[jax.ShapeDtypeStruct((t, wb.shape[1]), BF16),
                   jax.ShapeDtypeStruct((t, A_WIDTH), BF16),
                   jax.ShapeDtypeStruct((t, 2 * A_KV_WIDTH), BF16),
                   jax.ShapeDtypeStruct((t // LANES, B_WIDTH, LANES), BF16),
                   jax.ShapeDtypeStruct((t // LANES, wgt.shape[0] - B_WIDTH, LANES), F32)],
        compiler_params=_cparams(("arbitrary",)),
        name="proj_ab",
    )(x, mod, g, wb, wq, wkv, wgt)


def _gqa_block(q_ref, qn, bd, kd_ref, vd_ref, o_ref, rope=None):
    for p in range(A_HEADS // 2):
        g = p // (A_HEADS // 2 // A_KV_HEADS)
        qp = _head_rms(q_ref[:, p * LANES:(p + 1) * LANES].astype(F32), qn, bd)
        if rope is not None:
            qp = _rope(qp, rope[0], rope[1])
        qp = qp * (A_HEAD_DIM ** -0.5)
        o_ref[:, p * LANES:(p + 1) * LANES] = _softmax_pair(
            qp, lambda qm: [_dot_nt(qm, kd_ref[g])], lambda ps: _dot(ps[0], vd_ref[g])).astype(o_ref.dtype)


def _attn_a_prompt_kernel(q_ref, kv_ref, qn_ref, kn_ref, bd_ref, o_ref, knew_ref, vnew_ref, kd_ref, vd_ref):
    bd = bd_ref[...]
    k = _head_rms(kv_ref[:, :LANES].astype(F32), kn_ref[...], bd)
    v = kv_ref[:, LANES:].astype(F32)
    knew_ref[...] = k
    vnew_ref[...] = v
    for g in range(A_KV_HEADS):
        kd_ref[g] = _dup_half(k, g).astype(BF16)
        vd_ref[g] = _dup_half(v, g).astype(BF16)
    _gqa_block(q_ref, qn_ref[...], bd, kd_ref, vd_ref, o_ref)


def _attn_a_prompt(q, kv, qn, kn, bd):
    nb = BATCH
    full = lambda a: pl.BlockSpec(a.shape, lambda b: (0,) * a.ndim)
    return pl.pallas_call(
        _attn_a_prompt_kernel,
        grid=(nb,),
        in_specs=[pl.BlockSpec((SEQ, A_WIDTH), lambda b: (b, 0)),
                  pl.BlockSpec((SEQ, 2 * A_KV_WIDTH), lambda b: (b, 0)),
                  full(qn), full(kn), full(bd)],
        out_specs=[pl.BlockSpec((SEQ, A_WIDTH), lambda b: (b, 0)),
                   pl.BlockSpec((SEQ, A_KV_WIDTH), lambda b: (b, 0)),
                   pl.BlockSpec((SEQ, A_KV_WIDTH), lambda b: (b, 0))],
        out_shape=[jax.ShapeDtypeStruct((T_PROMPT, A_WIDTH), BF16),
                   jax.ShapeDtypeStruct((T_PROMPT, A_KV_WIDTH), F32),
                   jax.ShapeDtypeStruct((T_PROMPT, A_KV_WIDTH), F32)],
        scratch_shapes=[pltpu.VMEM((A_KV_HEADS, SEQ, LANES), BF16),
                        pltpu.VMEM((A_KV_HEADS, SEQ, LANES), BF16)],
        compiler_params=_cparams(("arbitrary",)),
        name="attn_a_prompt",
    )(q, kv, qn, kn, bd)


_A_QBLOCK = 256


def _attn_a_sample_kernel(q_ref, kv_ref, ck_ref, cv_ref, cos_ref, sin_ref, cosq_ref, sinq_ref,
                          qn_ref, kn_ref, bd_ref, o_ref, kd_ref, vd_ref):
    bd = bd_ref[...]

    @pl.when(pl.program_id(1) == 0)
    def _():
        for g in range(A_KV_HEADS):
            kd_ref[g, :PAST_LEN] = _dup_half(ck_ref[0], g).astype(BF16)
            vd_ref[g, :PAST_LEN] = _dup_half(cv_ref[0], g).astype(BF16)
        rows = 256
        for c in range(DEC_SEQ // rows):
            sl = slice(c * rows, (c + 1) * rows)
            k = _head_rms(kv_ref[sl, :LANES].astype(F32), kn_ref[...], bd)
            k = _rope(k, cos_ref[sl, :], sin_ref[sl, :])
            v = kv_ref[sl, LANES:].astype(F32)
            dst = slice(PAST_LEN + c * rows, PAST_LEN + (c + 1) * rows)
            for g in range(A_KV_HEADS):
                kd_ref[g, dst] = _dup_half(k, g).astype(BF16)
                vd_ref[g, dst] = _dup_half(v, g).astype(BF16)

    _gqa_block(q_ref, qn_ref[...], bd, kd_ref, vd_ref, o_ref, rope=(cosq_ref[...], sinq_ref[...]))


def _attn_a_sample(q, kv, ck, cv, cos, sin, qn, kn, bd):
    nq = DEC_SEQ // _A_QBLOCK
    full = lambda a: pl.BlockSpec(a.shape, lambda b, i: (0,) * a.ndim)
    tk = PAST_LEN + DEC_SEQ
    return pl.pallas_call(
        _attn_a_sample_kernel,
        grid=(DEC_BATCH, nq),
        in_specs=[pl.BlockSpec((_A_QBLOCK, A_WIDTH), lambda b, i: (b * nq + i, 0)),
                  pl.BlockSpec((DEC_SEQ, 2 * A_KV_WIDTH), lambda b, i: (b, 0)),
                  pl.BlockSpec((1, PAST_LEN, A_KV_WIDTH), lambda b, i: (b, 0, 0)),
                  pl.BlockSpec((1, PAST_LEN, A_KV_WIDTH), lambda b, i: (b, 0, 0)),
                  full(cos), full(sin),
                  pl.BlockSpec((_A_QBLOCK, LANES), lambda b, i: (i, 0)),
                  pl.BlockSpec((_A_QBLOCK, LANES), lambda b, i: (i, 0)),
                  full(qn), full(kn), full(bd)],
        out_specs=pl.BlockSpec((_A_QBLOCK, A_WIDTH), lambda b, i: (b * nq + i, 0)),
        out_shape=jax.ShapeDtypeStruct((T_SAMPLE, A_WIDTH), BF16),
        scratch_shapes=[pltpu.VMEM((A_KV_HEADS, tk, LANES), BF16),
                        pltpu.VMEM((A_KV_HEADS, tk, LANES), BF16)],
        compiler_params=_cparams(("arbitrary", "arbitrary")),
        name="attn_a_sample",
    )(q, kv, ck, cv, cos, sin, cos, sin, qn, kn, bd)


def _log_sigmoid(x):
    return -(jnp.maximum(-x, 0.0) + jnp.log1p(jnp.exp(-jnp.abs(x))))


def _mlstm_kernel(*refs, n, has_init, emit_state):
    it = iter(refs)
    q_ref, v_ref, og_ref, kt_ref, gt_ref, brow_ref, onorm_ref, tri_ref = [next(it) for _ in range(8)]
    if has_init:
        c0_ref, m0_ref = [next(it) for _ in range(2)]
    out_ref = next(it)
    if emit_state:
        cT_ref, nT_ref, mT_ref = [next(it) for _ in range(3)]
    h_ref, row_ref, c_ref = [next(it) for _ in range(3)]

    L = MLSTM_CHUNK
    D = B_HEAD_DIM
    nc = n // L
    hb = MLSTM_HEADS_PER_STEP

    lane = lax.broadcasted_iota(jnp.int32, (nc * 8, L), 1)
    is_fwd = lax.broadcasted_iota(jnp.int32, (nc * 8, L), 0) % 8 == 0

    def running_max(x, suffix):
        for sh in (1, 2, 4, 8, 16, 32, 64):
            if suffix:
                x = jnp.where(lane < L - sh, jnp.maximum(x, pltpu.roll(x, L - sh, 1)), x)
            else:
                x = jnp.where(lane >= sh, jnp.maximum(x, pltpu.roll(x, sh, 1)), x)
        return x

    for hh in range(hb):
        gt = gt_ref[:, 16 * hh:16 * hh + 16, :] + brow_ref[hh][None]
        li = gt[:, 0:8, :].reshape(nc * 8, L)
        lf = _log_sigmoid(gt[:, 8:16, :]).reshape(nc * 8, L)
        cum = jnp.where(is_fwd, _dot_exact_rhs(lf, tri_ref[0]), _dot_exact_rhs(lf, tri_ref[1]))
        a = li - cum
        planes = (cum, a, jnp.broadcast_to(lf.sum(axis=-1, keepdims=True), (nc * 8, L)),
                  jnp.where(is_fwd, running_max(a, False), running_max(a, True)),
                  jnp.broadcast_to(a.max(axis=-1, keepdims=True), (nc * 8, L)))
        for p, val in enumerate(planes):
            row_ref[hh, p] = val.reshape(nc, 8, L)

    chains = [(hh, d) for hh in range(hb) for d in range(2)]
    if has_init:
        m_init = []
        for hh, d in chains:
            c_ref[2 * hh + d] = c0_ref[0, d, hh]
            m_init.append(m0_ref[0, hh, d:d + 1, :])
        m_init = tuple(m_init)
    else:
        c_ref[...] = jnp.zeros(c_ref.shape, F32)
        m_init = tuple(jnp.zeros((1, L), F32) for _ in chains)

    t_idx = lax.broadcasted_iota(jnp.int32, (L, L), 0)
    s_idx = lax.broadcasted_iota(jnp.int32, (L, L), 1)
    masks = (s_idx <= t_idx, s_idx >= t_idx)
    ones = jnp.ones((L, L), F32)

    def step(c, hh, d, m):
        r0 = pl.multiple_of(c * L, L)
        hl = slice(hh * L, (hh + 1) * L)
        qb = q_ref[pl.ds(r0, L), hl].astype(BF16)
        kst = kt_ref[c, hl, :].astype(F32) * (D ** -0.5)
        v_ext = jnp.concatenate([v_ref[pl.ds(r0, L), hl], ones], axis=1).astype(BF16)
        cum, a_row, tot, amax_run, amax = [row_ref[hh, p, c][d:d + 1, :] for p in range(5)]
        m_cum = jnp.broadcast_to(cum, (L, L)).T
        m_run = jnp.broadcast_to(amax_run, (L, L)).T
        dlog = jnp.where(masks[d], m_cum + a_row, -jnp.inf)
        inter = m_cum + m
        m_t = jnp.maximum(inter, m_cum + m_run)
        w_in = jnp.exp(dlog - m_t)
        w_st = jnp.exp(inter - m_t)
        a = _dot(qb, kst.astype(BF16)) * w_in
        ci = 2 * hh + d
        cext = c_ref[ci]
        p_state = _dot(qb, cext.astype(BF16))
        p_intra = _dot(a.astype(BF16), v_ext)
        num = w_st * p_state[:, :D] + p_intra[:, :D]
        den = w_st * p_state[:, D:] + p_intra[:, D:]
        h_ref[ci, pl.ds(r0, L), :] = num / jnp.maximum(jnp.abs(den), jnp.exp(-m_t))
        m_new = jnp.maximum(tot + m, amax + tot)
        ws = jnp.exp(a_row + tot - m_new)
        wc = jnp.exp(tot + m - m_new)
        c_ref[ci] = jnp.concatenate([wc, wc], axis=1) * cext + _dot((kst * ws).astype(BF16), v_ext)
        return m_new

    def body(i, carry):
        return tuple(step(i if d == 0 else nc - 1 - i, hh, d, m) for (hh, d), m in zip(chains, carry))

    m_fin = lax.fori_loop(0, nc, body, m_init)

    for hh in range(hb):
        hm = h_ref[2 * hh] + h_ref[2 * hh + 1]
        hl = slice(hh * L, (hh + 1) * L)
        gate = jax.nn.sigmoid(og_ref[:, hl].astype(F32))
        out_ref[:, hl] = (_rms(hm, onorm_ref[hh]) * gate).astype(out_ref.dtype)

    if emit_state:
        for k, (hh, d) in enumerate(chains):
            cext = c_ref[2 * hh + d]
            cT_ref[0, d, hh] = cext[:, :D]
            nT_ref[0, hh, d:d + 1, :] = cext[:, D:].T[0:1, :]
            mT_ref[0, hh, d:d + 1, :] = m_fin[k]


def _mlstm(ob, okt, ogt, brow, onorm, tri, *, n, nseq, init=None, emit_state=False):
    L = MLSTM_CHUNK
    nc = n // L
    H = B_HEADS
    hb = MLSTM_HEADS_PER_STEP
    ng = H // hb
    col = lambda part: (lambda b, g: (b, part * ng + g))
    in_specs = [pl.BlockSpec((n, hb * L), col(0)), pl.BlockSpec((n, hb * L), col(1)),
                pl.BlockSpec((n, hb * L), col(2)),
                pl.BlockSpec((nc, hb * L, L), lambda b, g: (b, g, 0)),
                pl.BlockSpec((nc, 16 * hb, L), lambda b, g: (b, g, 0)),
                pl.BlockSpec((hb, 16, L), lambda b, g: (g, 0, 0)),
                pl.BlockSpec((hb, 1, L), lambda b, g: (g, 0, 0)),
                pl.BlockSpec(tri.shape, lambda b, g: (0, 0, 0))]
    args = [ob, ob, ob, okt, ogt, brow, onorm, tri]
    if init is not None:
        in_specs += [pl.BlockSpec((1, 2, hb, L, 2 * L), lambda b, g: (b, 0, g, 0, 0)),
                     pl.BlockSpec((1, hb, 2, L), lambda b, g: (b, g, 0, 0))]
        args += list(init)
    out_specs = [pl.BlockSpec((n, hb * L), lambda b, g: (b, g))]
    out_shape = [jax.ShapeDtypeStruct((nseq * n, B_WIDTH), BF16)]
    if emit_state:
        out_specs += [pl.BlockSpec((1, 2, hb, L, L), lambda b, g: (b, 0, g, 0, 0)),
                      pl.BlockSpec((1, hb, 2, L), lambda b, g: (b, g, 0, 0)),
                      pl.BlockSpec((1, hb, 2, L), lambda b, g: (b, g, 0, 0))]
        out_shape += [jax.ShapeDtypeStruct((nseq, 2, H, L, L), F32),
                      jax.ShapeDtypeStruct((nseq, H, 2, L), F32),
                      jax.ShapeDtypeStruct((nseq, H, 2, L), F32)]
    return pl.pallas_call(
        functools.partial(_mlstm_kernel, n=n, has_init=init is not None, emit_state=emit_state),
        grid=(nseq, ng),
        in_specs=in_specs,
        out_specs=out_specs,
        out_shape=out_shape,
        scratch_shapes=[pltpu.VMEM((2 * hb, n, L), F32),
                        pltpu.VMEM((hb, 5, nc, 8, L), F32),
                        pltpu.VMEM((2 * hb, L, 2 * L), F32)],
        compiler_params=_cparams(("arbitrary", "arbitrary")),
        name="mlstm_init" if init is not None else "mlstm",
    )(*args)


def _router(logits):
    lane = _lane(logits.shape).astype(F32)
    big = 1e9
    gl = jnp.where(lane < N_GROUPS, logits, -jnp.inf)
    gmax = gl.max(axis=-1, keepdims=True)
    g_sel = jnp.where(gl == gmax, lane, big).min(axis=-1, keepdims=True)
    g_prob = 1.0 / jnp.exp(gl - gmax).sum(axis=-1, keepdims=True)
    lo = N_GROUPS + EXPERTS_PER_GROUP * g_sel
    el = jnp.where(lane >= lo, jnp.where(lane < lo + EXPERTS_PER_GROUP, logits, -jnp.inf), -jnp.inf)
    v1 = el.max(axis=-1, keepdims=True)
    i1 = jnp.where(el == v1, lane, big).min(axis=-1, keepdims=True)
    el2 = jnp.where(lane == i1, -jnp.inf, el)
    v2 = el2.max(axis=-1, keepdims=True)
    i2 = jnp.where(el2 == v2, lane, big).min(axis=-1, keepdims=True)
    e2 = jnp.exp(v2 - v1)
    w1 = g_prob / (1.0 + e2)
    w2 = g_prob * e2 / (1.0 + e2)
    return i1, i2, w1, w2


def _read_tokens(refs, is_prompt):
    if len(refs) == 1:
        return refs[0][...]
    return jnp.where(is_prompt, refs[0][...], refs[1][...])


def _post_kernel(*refs, groups):
    it = iter(refs)
    tok_refs = [[next(it) for _ in range(n)] for n in groups]
    w_refs = [next(it) for _ in range(len(groups) - 1)]
    mod_ref, g_ref, wr_ref, br_ref, ls_ref, sel_ref = [next(it) for _ in range(6)]
    xnew_ref, xn_ref, rt_ref, rtt_ref, cnt_ref = [next(it) for _ in range(5)]
    run_ref = next(it)
    i = pl.program_id(0)
    tm = xnew_ref.shape[0]
    is_prompt = i * tm < T_PROMPT
    r = _mod_row(i * tm)
    acc = None
    for a_refs, w_ref in zip(tok_refs[1:], w_refs):
        d = _dot(_read_tokens(a_refs, is_prompt).astype(BF16), w_ref[...])
        acc = d if acc is None else acc + d
    xnew = _read_tokens(tok_refs[0], is_prompt) + _mod_part(mod_ref, r, 2) * acc
    xnew_ref[...] = xnew
    xn = _rms(xnew, g_ref[...]) * (1.0 + _mod_part(mod_ref, r, 4)) + _mod_part(mod_ref, r, 3)
    _to_token_tiles(xn_ref, xn)
    x_hi, x_lo = _split2(xn)
    both = _dot(x_hi, wr_ref[...])
    logits = both[:, :LANES] + both[:, LANES:] + _dot(x_lo, wr_ref[:, :LANES]) + br_ref[...]
    i1, i2, w1, w2 = _router(logits)

    @pl.when(i == 0)
    def _():
        run_ref[...] = jnp.zeros(run_ref.shape, F32)

    lane = _lane(logits.shape).astype(F32)
    member = jnp.where(lane == i1, 1.0, jnp.where(lane == i2, 1.0, 0.0))
    before = _dot(ls_ref[...], member.astype(BF16)) + run_ref[...]
    rank1 = jnp.where(lane == i1, before, 0.0).sum(axis=-1, keepdims=True)
    rank2 = jnp.where(lane == i2, before, 0.0).sum(axis=-1, keepdims=True)
    run_ref[...] = run_ref[...] + member.sum(axis=0, keepdims=True)
    cnt_ref[...] = run_ref[...]
    cols = (i1 - N_GROUPS, i2 - N_GROUPS, w1, w2, rank1, rank2)
    rt = jnp.zeros(logits.shape, F32)
    for k, c in enumerate(cols):
        rt = jnp.where(lane == k, c, rt)
    rt_ref[...] = rt
    hi, mid, lo = _split3(rt)
    sel = sel_ref[...]
    rtt_ref[...] = _dot_nt(sel, hi) + _dot_nt(sel, mid) + _dot_nt(sel, lo)


def _post(tok_ops, w_list, mod, g, wr, br):
    t = T_ALL
    tm = TOK_BLOCK
    npb = T_PROMPT // tm
    full = lambda a: pl.BlockSpec(a.shape, lambda i: (0,) * a.ndim)
    specs, args, groups = [], [], []
    for op in tok_ops:
        if isinstance(op, tuple):
            w = op[0].shape[1]
            specs += [pl.BlockSpec((tm, w), lambda i: (jnp.minimum(i, npb - 1), 0)),
                      pl.BlockSpec((tm, w), lambda i: (jnp.maximum(i - npb, 0), 0))]
            args += list(op)
            groups.append(2)
        else:
            specs.append(pl.BlockSpec((tm, op.shape[1]), lambda i: (i, 0)))
            args.append(op)
            groups.append(1)
    idx = np.arange(tm)
    ls = jnp.asarray(idx[:, None] > idx[None, :], BF16)
    sel = jnp.asarray(np.arange(8)[:, None] == np.arange(LANES)[None, :], BF16)
    consts = [mod, g, wr, br, ls, sel]
    return pl.pallas_call(
        functools.partial(_post_kernel, groups=tuple(groups)),
        grid=(t // tm,),
        in_specs=specs + [full(w) for w in w_list] + [full(a) for a in consts],
        out_specs=[pl.BlockSpec((tm, D_MODEL), lambda i: (i, 0)),
                   pl.BlockSpec((tm * SLABS, LANES), lambda i: (i, 0)),
                   pl.BlockSpec((tm, LANES), lambda i: (i, 0)),
                   pl.BlockSpec((8, tm), lambda i: (0, i)),
                   pl.BlockSpec((1, LANES), lambda i: (0, 0))],
        out_shape=[jax.ShapeDtypeStruct((t, D_MODEL), F32),
                   jax.ShapeDtypeStruct((t * SLABS, LANES), F32),
                   jax.ShapeDtypeStruct((t, LANES), F32),
                   jax.ShapeDtypeStruct((8, t), F32),
                   jax.ShapeDtypeStruct((1, LANES), F32)],
        scratch_shapes=[pltpu.VMEM((1, LANES), F32)],
        compiler_params=_cparams(("arbitrary",)),
        name="post_mixer_router",
    )(*args, *w_list, *consts)


def _expert_kernel(be_ref, st_ref, nv_ref, xn_hbm, w1_ref, w3_ref, w2_ref, o_ref, xa, xb, sem, w1b, w3b, w2b):
    i = pl.program_id(0)
    nv = nv_ref[0]
    active = i < nv
    rows = EXPERT_ROWS

    def gather(blk, buf, s):
        for r in range(rows):
            tok = st_ref[blk * rows + r]
            pltpu.make_async_copy(xn_hbm.at[pl.ds(pl.multiple_of(tok * SLABS, SLABS), SLABS), :],
                                  buf.at[pl.ds(r * SLABS, SLABS), :], sem.at[s]).start()

    def wait(buf, s):
        pltpu.make_async_copy(xn_hbm.at[pl.ds(0, rows * SLABS), :], buf, sem.at[s]).wait()

    @pl.when(i == 0)
    def _():
        gather(0, xa, 0)

    changed = jnp.logical_or(i == 0, be_ref[i] != be_ref[jnp.maximum(i - 1, 0)])

    @pl.when(jnp.logical_and(changed, active))
    def _():
        w1b[...] = w1_ref[0, 0].astype(BF16)
        w3b[...] = w3_ref[0, 0].astype(BF16)
        w2b[...] = w2_ref[0, 0].astype(BF16)

    nxt = jnp.minimum(i + 1, nv - 1)

    def step(cur, s_cur, oth, s_oth):
        wait(cur, s_cur)
        gather(nxt, oth, s_oth)
        x = _from_token_tiles(cur, 0, rows).astype(BF16)
        h1 = _dot(x, w1b[...])
        h3 = _dot(x, w3b[...])
        hid = (h1 * jax.nn.sigmoid(h1)) * h3
        _to_token_tiles(o_ref, _dot(hid.astype(BF16), w2b[...]))

    @pl.when(jnp.logical_and(active, i % 2 == 0))
    def _():
        step(xa, 0, xb, 1)

    @pl.when(jnp.logical_and(active, i % 2 == 1))
    def _():
        step(xb, 1, xa, 0)

    @pl.when(jnp.logical_not(active))
    def _():
        o_ref[...] = jnp.zeros(o_ref.shape, F32)

    @pl.when(i == nv - 1)
    def _():
        @pl.when(i % 2 == 0)
        def _():
            wait(xb, 1)

        @pl.when(i % 2 == 1)
        def _():
            wait(xa, 0)


def _experts(block_expert, slot_tok, n_valid, xn, w1, w3, w2, layer):
    nblk = block_expert.shape[0]
    rows = EXPERT_ROWS
    grid_spec = pltpu.PrefetchScalarGridSpec(
        num_scalar_prefetch=3,
        grid=(nblk,),
        in_specs=[pl.BlockSpec(memory_space=pl.ANY),
                  pl.BlockSpec((1, 1, D_MODEL, D_EXPERT), lambda i, be, st, nv: (layer, be[i], 0, 0)),
                  pl.BlockSpec((1, 1, D_MODEL, D_EXPERT), lambda i, be, st, nv: (layer, be[i], 0, 0)),
                  pl.BlockSpec((1, 1, D_EXPERT, D_MODEL), lambda i, be, st, nv: (layer, be[i], 0, 0))],
        out_specs=pl.BlockSpec((rows * SLABS, LANES), lambda i, be, st, nv: (i, 0)),
        scratch_shapes=[pltpu.VMEM((rows * SLABS, LANES), F32),
                        pltpu.VMEM((rows * SLABS, LANES), F32),
                        pltpu.SemaphoreType.DMA((2,)),
                        pltpu.VMEM((D_MODEL, D_EXPERT), BF16),
                        pltpu.VMEM((D_MODEL, D_EXPERT), BF16),
                        pltpu.VMEM((D_EXPERT, D_MODEL), BF16)])
    return pl.pallas_call(
        _expert_kernel,
        grid_spec=grid_spec,
        out_shape=jax.ShapeDtypeStruct((nblk * rows * SLABS, LANES), F32),
        compiler_params=_cparams(("arbitrary",)),
        name="moe_experts",
    )(block_expert, slot_tok, n_valid, xn, w1, w3, w2)


_COMBINE_BLOCK = 128


def _combine_kernel(dest_ref, yb_hbm, x_ref, rt_ref, mod_ref, gfin_ref, o_ref, ybuf, sem, *, final_norm, row0):
    i = pl.program_id(0)
    nblk = pl.num_programs(0)
    slot = i % 2
    tm = _COMBINE_BLOCK

    def issue(blk, s):
        def body(j, carry):
            for c in range(2):
                d = dest_ref[c * T_ALL + row0 + blk * tm + j]
                pltpu.make_async_copy(yb_hbm.at[pl.ds(pl.multiple_of(d * SLABS, SLABS), SLABS), :],
                                      ybuf.at[s, pl.ds(pl.multiple_of((c * tm + j) * SLABS, SLABS), SLABS), :],
                                      sem.at[s]).start()
            return carry
        lax.fori_loop(0, tm, body, 0, unroll=4)

    @pl.when(i == 0)
    def _():
        issue(0, 0)

    @pl.when(i + 1 < nblk)
    def _():
        issue(i + 1, 1 - slot)

    pltpu.make_async_copy(yb_hbm.at[pl.ds(0, 2 * tm * SLABS), :], ybuf.at[slot], sem.at[slot]).wait()
    r = _mod_row(row0 + i * tm)
    rt = rt_ref[...]
    yv = ybuf.at[slot]
    y = rt[:, 2:3] * _from_token_tiles(yv, 0, tm) + rt[:, 3:4] * _from_token_tiles(yv, tm, tm)
    out = x_ref[...] + _mod_part(mod_ref, r, 5) * y
    if final_norm:
        out = _rms(out, gfin_ref[...])
    o_ref[...] = out


def _combine(dest, yb, x, rt, mod, gfin, final_norm, row0=0, t=T_ALL):
    tm = _COMBINE_BLOCK
    blk0 = row0 // tm
    grid_spec = pltpu.PrefetchScalarGridSpec(
        num_scalar_prefetch=1,
        grid=(t // tm,),
        in_specs=[pl.BlockSpec(memory_space=pl.ANY),
                  pl.BlockSpec((tm, D_MODEL), lambda i, d: (blk0 + i, 0)),
                  pl.BlockSpec((tm, LANES), lambda i, d: (blk0 + i, 0)),
                  pl.BlockSpec(mod.shape, lambda i, d: (0, 0)),
                  pl.BlockSpec(gfin.shape, lambda i, d: (0, 0))],
        out_specs=pl.BlockSpec((tm, D_MODEL), lambda i, d: (i, 0)),
        scratch_shapes=[pltpu.VMEM((2, 2 * tm * SLABS, LANES), F32),
                        pltpu.SemaphoreType.DMA((2,))])
    return pl.pallas_call(
        functools.partial(_combine_kernel, final_norm=final_norm, row0=row0),
        grid_spec=grid_spec,
        out_shape=jax.ShapeDtypeStruct((t, D_MODEL), F32),
        compiler_params=_cparams(("arbitrary",)),
        name="moe_combine",
    )(dest, yb, x, rt, mod, gfin)


def _combine_proj_kernel(dest_ref, yb_hbm, x_ref, rt_ref, mod0_ref, mod1_ref, g_ref, w_ref,
                         xo_ref, q_ref, k_ref, v_ref, ya, yb, sem, *, row0):
    i = pl.program_id(0)
    nblk = pl.num_programs(0)
    tm = x_ref.shape[0]

    def gather(blk, buf, s):
        for j in range(tm):
            for c in range(2):
                d = dest_ref[c * T_ALL + row0 + blk * tm + j]
                pltpu.make_async_copy(yb_hbm.at[pl.ds(pl.multiple_of(d * SLABS, SLABS), SLABS), :],
                                      buf.at[pl.ds((c * tm + j) * SLABS, SLABS), :], sem.at[s]).start()

    def wait(buf, s):
        pltpu.make_async_copy(yb_hbm.at[pl.ds(0, 2 * tm * SLABS), :], buf, sem.at[s]).wait()

    @pl.when(i == 0)
    def _():
        gather(0, ya, 0)

    nxt = jnp.minimum(i + 1, nblk - 1)

    def step(cur, s_cur, oth, s_oth):
        wait(cur, s_cur)
        gather(nxt, oth, s_oth)
        r = _mod_row(row0 + i * tm)
        rt = rt_ref[...]
        y = rt[:, 2:3] * _from_token_tiles(cur, 0, tm) + rt[:, 3:4] * _from_token_tiles(cur, tm, tm)
        x = x_ref[...] + _mod_part(mod0_ref, r, 5) * y
        xo_ref[...] = x
        h = _rms(x, g_ref[...]) * (1.0 + _mod_part(mod1_ref, r, 1)) + _mod_part(mod1_ref, r, 0)
        hb = h.astype(BF16)
        for j, o_ref in enumerate((q_ref, k_ref, v_ref)):
            o_ref[...] = _dot(hb, w_ref[:, j * C_WIDTH:(j + 1) * C_WIDTH]).astype(o_ref.dtype)

    @pl.when(i % 2 == 0)
    def _():
        step(ya, 0, yb, 1)

    @pl.when(i % 2 == 1)
    def _():
        step(yb, 1, ya, 0)

    @pl.when(i == nblk - 1)
    def _():
        @pl.when(i % 2 == 0)
        def _():
            wait(yb, 1)

        @pl.when(i % 2 == 1)
        def _():
            wait(ya, 0)


def _combine_proj(dest, yb, x, rt, mod0, mod1, g, w, row0, t):
    tm = TOK_BLOCK
    blk0 = row0 // tm
    full = lambda a: pl.BlockSpec(a.shape, lambda i, d: (0,) * a.ndim)
    grid_spec = pltpu.PrefetchScalarGridSpec(
        num_scalar_prefetch=1,
        grid=(t // tm,),
        in_specs=[pl.BlockSpec(memory_space=pl.ANY),
                  pl.BlockSpec((tm, D_MODEL), lambda i, d: (blk0 + i, 0)),
                  pl.BlockSpec((tm, LANES), lambda i, d: (blk0 + i, 0)),
                  full(mod0), full(mod1), full(g), full(w)],
        out_specs=[pl.BlockSpec((tm, D_MODEL), lambda i, d: (i, 0))] * 4,
        scratch_shapes=[pltpu.VMEM((2 * tm * SLABS, LANES), F32),
                        pltpu.VMEM((2 * tm * SLABS, LANES), F32),
                        pltpu.SemaphoreType.DMA((2,))])
    return pl.pallas_call(
        functools.partial(_combine_proj_kernel, row0=row0),
        grid_spec=grid_spec,
        out_shape=[jax.ShapeDtypeStruct((t, D_MODEL), F32)] + [jax.ShapeDtypeStruct((t, C_WIDTH), BF16)] * 3,
        compiler_params=_cparams(("arbitrary",)),
        name="moe_combine_proj",
    )(dest, yb, x, rt, mod0, mod1, g, w)


def _moe_plan(rtt, cnt):
    t = rtt.shape[1]
    eid = rtt[0:2].astype(jnp.int32)
    rank = rtt[4:6].astype(jnp.int32)
    counts = cnt[0, N_GROUPS:N_GROUPS + N_EXPERTS].astype(jnp.int32)
    padded = (counts + EXPERT_ROWS - 1) // EXPERT_ROWS * EXPERT_ROWS
    seg_end = jnp.cumsum(padded)
    seg_start = seg_end - padded
    experts = jnp.arange(N_EXPERTS, dtype=jnp.int32)
    start = jnp.sum(jnp.where(eid[..., None] == experts, seg_start, 0), axis=-1)
    dest = (start + rank).reshape(-1)
    n_blocks = (2 * t + N_EXPERTS * (EXPERT_ROWS - 1) + EXPERT_ROWS - 1) // EXPERT_ROWS
    tok = jnp.tile(jnp.arange(t, dtype=jnp.int32), 2)
    slot_tok = (jnp.arange(n_blocks * EXPERT_ROWS, dtype=jnp.int32) % t).at[dest].set(
        tok, unique_indices=True, mode='promise_in_bounds')
    first_row = jnp.arange(n_blocks, dtype=jnp.int32) * EXPERT_ROWS
    block_expert = jnp.minimum(jnp.sum((seg_end[None, :] <= first_row[:, None]).astype(jnp.int32), axis=1),
                               N_EXPERTS - 1)
    n_valid = (seg_end[-1:] // EXPERT_ROWS).astype(jnp.int32)
    return dest, slot_tok, block_expert, n_valid


def _attn_c_prompt_kernel(q_ref, k_ref, v_ref, o_ref):
    for p in range(C_HEADS // 2):
        sl = slice(p * LANES, (p + 1) * LANES)
        kb = k_ref[:, sl].astype(BF16)
        vb = v_ref[:, sl].astype(BF16)
        qp = q_ref[:, sl].astype(F32) * (C_HEAD_DIM ** -0.5)
        o_ref[:, sl] = _softmax_pair(qp, lambda qm: [_dot_nt(qm, kb)],
                                     lambda ps: _dot(ps[0], vb)).astype(o_ref.dtype)


def _attn_c_prompt(q, k, v):
    blk = pl.BlockSpec((SEQ, C_WIDTH), lambda b: (b, 0))
    return pl.pallas_call(
        _attn_c_prompt_kernel,
        grid=(BATCH,),
        in_specs=[blk, blk, blk],
        out_specs=blk,
        out_shape=jax.ShapeDtypeStruct((T_PROMPT, C_WIDTH), BF16),
        compiler_params=_cparams(("arbitrary",)),
        name="attn_c_prompt",
    )(q, k, v)


def _na_key_start(r0):
    rows = DEC_SEQ // GRID_W
    return jnp.minimum(jnp.clip(r0 - NA_ROWS // 2, 0, rows - NA_ROWS), rows - NA_KROWS)


def _na_block_plan():
    rows = DEC_SEQ // GRID_W
    nblk = rows // NA_QROWS
    plan = []
    for blk in (0, 1, nblk - 1):
        r0 = blk * NA_QROWS
        ks = min(int(np.clip(r0 - NA_ROWS // 2, 0, rows - NA_ROWS)), rows - NA_KROWS)
        per_row = []
        for i in range(NA_QROWS):
            r = r0 + i
            rs = int(np.clip(r - NA_ROWS // 2, 0, rows - NA_ROWS))
            start = ks - r + NA_ROWS - 1 + NA_KROWS
            ok = [rs <= ks + j < rs + NA_ROWS for j in range(NA_KROWS)]
            per_row.append((start, ok))
        plan.append(per_row)
    return plan


def _attn_c_sample_kernel(q_ref, k_ref, v_ref, ck_ref, cv_ref, toe_ref, o_ref, bias_ref):
    rows = DEC_SEQ // GRID_W
    nblk = rows // NA_QROWS
    w = GRID_W

    @pl.when(jnp.logical_and(pl.program_id(1) == 0, pl.program_id(2) == 0))
    def _():
        neg = jnp.full((w, w), NEG, F32)
        for t, per_row in enumerate(_na_block_plan()):
            for half in range(2):
                for i, (start, ok) in enumerate(per_row):
                    for j in range(0, NA_KROWS, 2):
                        pieces = [toe_ref[0, half, start + jj] if ok[jj] else neg for jj in (j, j + 1)]
                        bias_ref[t, half, i * w:(i + 1) * w, j * w:(j + 2) * w] = jnp.concatenate(pieces, axis=1)

    i = pl.program_id(2)
    r0 = i * NA_QROWS
    k0 = pl.multiple_of(_na_key_start(r0) * GRID_W, GRID_W)
    btype = jnp.where(i == 0, 0, jnp.where(i == nblk - 1, 2, 1))
    nk = NA_KROWS * GRID_W
    kw = k_ref[pl.ds(k0, nk), :].astype(BF16)
    vw = v_ref[pl.ds(k0, nk), :].astype(BF16)
    kc = ck_ref[0].astype(BF16)
    vc = cv_ref[0].astype(BF16)
    qp = q_ref[...].astype(F32) * (C_HEAD_DIM ** -0.5)
    lo = _lane(qp.shape) < 64
    outs = []
    for half in range(2):
        qm = jnp.where(lo if half == 0 else jnp.logical_not(lo), qp, 0.0).astype(BF16)
        s_win = _dot_nt(qm, kw) + bias_ref[btype, half]
        s_ctx = _dot_nt(qm, kc)
        m = jnp.maximum(s_win.max(axis=-1, keepdims=True), s_ctx.max(axis=-1, keepdims=True))
        e_win = jnp.exp(s_win - m)
        e_ctx = jnp.exp(s_ctx - m)
        l = e_win.sum(axis=-1, keepdims=True) + e_ctx.sum(axis=-1, keepdims=True)
        o = _dot(e_win.astype(BF16), vw) + _dot(e_ctx.astype(BF16), vc)
        outs.append(o / l)
    o_ref[...] = jnp.where(lo, outs[0], outs[1]).astype(o_ref.dtype)


def _attn_c_sample(q, k, v, ck, cv, toe):
    rows = DEC_SEQ // GRID_W
    nblk = rows // NA_QROWS
    qrows = NA_QROWS * GRID_W
    npair = C_HEADS // 2
    return pl.pallas_call(
        _attn_c_sample_kernel,
        grid=(npair, DEC_BATCH, nblk),
        in_specs=[pl.BlockSpec((qrows, LANES), lambda p, b, i: (b * nblk + i, p)),
                  pl.BlockSpec((DEC_SEQ, LANES), lambda p, b, i: (b, p)),
                  pl.BlockSpec((DEC_SEQ, LANES), lambda p, b, i: (b, p)),
                  pl.BlockSpec((1, PAST_LEN, LANES), lambda p, b, i: (b, 0, p)),
                  pl.BlockSpec((1, PAST_LEN, LANES), lambda p, b, i: (b, 0, p)),
                  pl.BlockSpec((1,) + toe.shape[1:], lambda p, b, i: (p, 0, 0, 0, 0))],
        out_specs=pl.BlockSpec((qrows, LANES), lambda p, b, i: (b * nblk + i, p)),
        out_shape=jax.ShapeDtypeStruct((T_SAMPLE, C_WIDTH), BF16),
        scratch_shapes=[pltpu.VMEM((3, 2, qrows, NA_KROWS * GRID_W), F32)],
        compiler_params=_cparams(("arbitrary", "arbitrary", "arbitrary")),
        name="attn_c_sample",
    )(q, k, v, ck, cv, toe)


def _na_toeplitz(rpb):
    w = GRID_W
    nd_r, nd_c = 2 * NA_ROWS - 1, 2 * NA_COLS - 1
    c = np.arange(w)
    cs = np.clip(c - NA_COLS // 2, 0, w - NA_COLS)
    col_ok = (c[None, :] >= cs[:, None]) & (c[None, :] < cs[:, None] + NA_COLS)
    dcol = c[None, :] - c[:, None] + NA_COLS - 1
    onehot = (np.arange(nd_c)[:, None, None] == dcol[None]).reshape(nd_c, w * w)
    toe = jnp.dot(rpb.reshape(C_HEADS * nd_r, nd_c), jnp.asarray(onehot, F32), precision=lax.Precision.HIGHEST)
    toe = jnp.where(col_ok[None, None], toe.reshape(C_HEADS, nd_r, w, w), NEG)
    toe = jnp.pad(toe, ((0, 0), (NA_KROWS, NA_KROWS), (0, 0), (0, 0)), constant_values=NEG)
    return toe.reshape(C_HEADS // 2, 2, nd_r + 2 * NA_KROWS, w, w)


def _rope_tables():
    half = A_HEAD_DIM // 2
    t = jnp.arange(DEC_SEQ)
    row = (t // GRID_W).astype(F32)
    colp = (t % GRID_W).astype(F32)
    freqs = 1.0 / (ROPE_BASE ** (jnp.arange(0, half, 2, dtype=F32) / half))
    d = np.arange(LANES) % A_HEAD_DIM
    pos = jnp.where(jnp.asarray(d < half)[None, :], row[:, None], colp[:, None])
    ang = pos * freqs[d % (half // 2)][None, :]
    sign = jnp.asarray(np.where((d % half) < half // 2, -1.0, 1.0), F32)[None, :]
    return jnp.cos(ang), jnp.sin(ang) * sign


def _head_avg_matrix():
    idx = np.arange(LANES) // A_HEAD_DIM
    return jnp.asarray((idx[:, None] == idx[None, :]).astype(np.float32) / A_HEAD_DIM, BF16)


def _tri_matrices():
    i = np.arange(MLSTM_CHUNK)
    upper = (i[:, None] <= i[None, :]).astype(np.float32)
    lower = (i[:, None] >= i[None, :]).astype(np.float32)
    return jnp.asarray(np.stack([upper, lower]), BF16)


def _router_weights(wg, bg, we, be):
    w = jnp.zeros((D_MODEL, LANES), F32).at[:, :N_GROUPS].set(wg).at[:, N_GROUPS:N_GROUPS + N_EXPERTS].set(we)
    b = jnp.zeros((1, LANES), F32).at[0, :N_GROUPS].set(bg).at[0, N_GROUPS:N_GROUPS + N_EXPERTS].set(be)
    hi = w.astype(BF16)
    lo = (w - hi.astype(F32)).astype(BF16)
    return jnp.concatenate([hi, lo], axis=1), b


def _moe(xn, rtt, cnt, w1, w3, w2, layer):
    dest, slot_tok, block_expert, n_valid = _moe_plan(rtt, cnt)
    return dest, _experts(block_expert, slot_tok, n_valid, xn, w1, w3, w2, layer)


def kernel(x_prompt, x_sample, cache_attn_k, cache_attn_v, state_mlstm_C, state_mlstm_n, state_mlstm_m,
           cache_na_k, cache_na_v, c, c_ctx, norm_mix, norm_ffn, norm_final, ada_w, ada_b,
           ab_w_in, ab_w_out, ab_q_norm, ab_k_norm, ab_gate_bias, ab_out_norm,
           na_w_in, na_w_out, na_rpb, moe_wg, moe_bg, moe_we, moe_be, moe_w1, moe_w3, moe_w2):
    xp = x_prompt.reshape(T_PROMPT, D_MODEL)
    xs = x_sample.reshape(T_SAMPLE, D_MODEL)
    cond =jnp.zeros((N_COND, D_MODEL), F32).at[0].set(c_ctx).at[1:1 + DEC_BATCH].set(c)
    mod = _modulation(cond, ada_w, ada_b)
    gfin = norm_final.reshape(1, D_MODEL)

    w_in = ab_w_in[0]
    o_aq, o_ak, o_av, o_bq, o_bk, o_bv, o_bo, o_bg = np.cumsum((0,) + (A_WIDTH, A_KV_WIDTH, A_KV_WIDTH,
                                                                       B_WIDTH, B_WIDTH, B_WIDTH, B_WIDTH))
    wb = jnp.concatenate([w_in[:, o_bq:o_bk], w_in[:, o_bv:o_bg]], axis=1).astype(BF16)
    wq = w_in[:, o_aq:o_ak].astype(BF16)
    wkv = w_in[:, o_ak:o_bq].astype(BF16)
    gate_rows = np.array([0, 8, 1, 9])
    wg = w_in[:, o_bg:o_bg + 4 * B_HEADS].reshape(D_MODEL, 4, B_HEADS)
    wgt = jnp.zeros((B_HEADS, 16, D_MODEL), F32).at[:, gate_rows, :].set(wg.transpose(2, 1, 0))
    wgt = jnp.concatenate([w_in[:, o_bk:o_bv].T, wgt.reshape(16 * B_HEADS, D_MODEL)], axis=0).astype(BF16)
    g_mix = norm_mix[0].reshape(1, D_MODEL)
    ob_p, oq_p, okv_p, okt_p, ogt_p = _proj_ab(xp, mod[0], g_mix, wb, wq, wkv, wgt, 0)
    ob_s, oq_s, okv_s, okt_s, ogt_s = _proj_ab(xs, mod[0], g_mix, wb, wq, wkv, wgt, T_PROMPT)

    qn = jnp.tile(ab_q_norm[0], 2).reshape(1, LANES)
    kn = jnp.tile(ab_k_norm[0], 2).reshape(1, LANES)
    bd = _head_avg_matrix()
    cos, sin = _rope_tables()
    a_p, new_k, new_v = _attn_a_prompt(oq_p, okv_p, qn, kn, bd)
    ck = cache_attn_k[:, 0].reshape(DEC_BATCH, PAST_LEN, A_KV_WIDTH)
    cv = cache_attn_v[:, 0].reshape(DEC_BATCH, PAST_LEN, A_KV_WIDTH)
    a_s = _attn_a_sample(oq_s, okv_s, ck, cv, cos, sin, qn, kn, bd)

    gb = ab_gate_bias[0]
    brow = jnp.zeros((B_HEADS, 16, LANES), F32).at[:, gate_rows, :].set(
        jnp.broadcast_to(gb.T[:, :, None], (B_HEADS, 4, LANES)))
    onorm = ab_out_norm[0].reshape(B_HEADS, 1, B_HEAD_DIM)
    tri = _tri_matrices()
    b_p, cT, nT, mT = _mlstm(ob_p, okt_p, ogt_p, brow, onorm, tri, n=SEQ, nseq=BATCH, emit_state=True)
    n0 = jnp.broadcast_to(state_mlstm_n[:, 0][..., None], state_mlstm_C[:, 0].shape)
    c0 = jnp.concatenate([state_mlstm_C[:, 0], n0], axis=-1)
    m0 = jnp.broadcast_to(state_mlstm_m[:, 0].transpose(0, 2, 1)[..., None], (DEC_BATCH, B_HEADS, 2, LANES))
    (b_s,) = _mlstm(ob_s, okt_s, ogt_s, brow, onorm, tri, n=DEC_SEQ, nseq=DEC_BATCH, init=(c0, m0))

    w_out = ab_w_out[0].astype(BF16)
    wr, br = _router_weights(moe_wg[0], moe_bg[0], moe_we[0], moe_be[0])
    x1, xn, rt, rtt, cnt = _post([(xp, xs), (a_p, a_s), (b_p, b_s)], [w_out[:A_WIDTH], w_out[A_WIDTH:]], mod[0],
                                 norm_ffn[0].reshape(1, D_MODEL), wr, br)
    dest, yb = _moe(xn, rtt, cnt, moe_w1, moe_w3, moe_w2, 0)

    g_mix = norm_mix[1].reshape(1, D_MODEL)
    w_in = na_w_in[0].astype(BF16)
    x_p, q_p, k_p, v_p = _combine_proj(dest, yb, x1, rt, mod[0], mod[1], g_mix, w_in, 0, T_PROMPT)
    x_s, q_s, k_s, v_s = _combine_proj(dest, yb, x1, rt, mod[0], mod[1], g_mix, w_in, T_PROMPT, T_SAMPLE)
    o_p = _attn_c_prompt(q_p, k_p, v_p)
    nck = cache_na_k[:, 0].reshape(DEC_BATCH, PAST_LEN, C_WIDTH)
    ncv = cache_na_v[:, 0].reshape(DEC_BATCH, PAST_LEN, C_WIDTH)
    o_s = _attn_c_sample(q_s, k_s, v_s, nck, ncv, _na_toeplitz(na_rpb[0]))
    wr, br = _router_weights(moe_wg[1], moe_bg[1], moe_we[1], moe_be[1])
    x1, xn, rt, rtt, cnt = _post([(x_p, x_s), (o_p, o_s)], [na_w_out[0].astype(BF16)], mod[1],
                                 norm_ffn[1].reshape(1, D_MODEL), wr, br)
    dest, yb = _moe(xn, rtt, cnt, moe_w1, moe_w3, moe_w2, 1)
    y_prompt = _combine(dest, yb, x1, rt, mod[1], gfin, True, 0, T_PROMPT).reshape(BATCH, SEQ, D_MODEL)
    y_sample = _combine(dest, yb, x1, rt, mod[1], gfin, True, T_PROMPT, T_SAMPLE).reshape(DEC_BATCH, DEC_SEQ, D_MODEL)
    new_attn_k = new_k.reshape(BATCH, 1, SEQ, A_KV_HEADS, A_HEAD_DIM)
    new_attn_v = new_v.reshape(BATCH, 1, SEQ, A_KV_HEADS, A_HEAD_DIM)
    new_mlstm_C = cT[:, None]
    new_mlstm_n = nT.transpose(0, 2, 1, 3)[:, None]
    new_mlstm_m = mT[..., 0].transpose(0, 2, 1)[:, None]
    new_na_k = k_p.astype(F32).reshape(BATCH, 1, SEQ, C_HEADS, C_HEAD_DIM)
    new_na_v = v_p.astype(F32).reshape(BATCH, 1, SEQ, C_HEADS, C_HEAD_DIM)
    return (y_prompt, y_sample, new_attn_k, new_attn_v, new_mlstm_C, new_mlstm_n, new_mlstm_m,
            new_na_k, new_na_v)
```

```python
import functools

import numpy as np
import jax
import jax.numpy as jnp
from jax import lax
from jax.experimental import pallas as pl
from jax.experimental.pallas import tpu as pltpu

F32 = jnp.float32
BF16 = jnp.bfloat16

D_MODEL = 1024
BATCH = 32
SEQ = 256
DEC_BATCH = 4
DEC_SEQ = 2048
PAST_LEN = 256
GRID_W = 64
A_HEADS = 8
A_KV_HEADS = 2
A_HEAD_DIM = 64
A_WIDTH = A_HEADS * A_HEAD_DIM
A_KV_WIDTH = A_KV_HEADS * A_HEAD_DIM
B_HEADS = 4
B_HEAD_DIM = 128
B_WIDTH = B_HEADS * B_HEAD_DIM
MLSTM_CHUNK = 128
C_HEADS = 16
C_HEAD_DIM = 64
C_WIDTH = C_HEADS * C_HEAD_DIM
NA_ROWS = 8
NA_COLS = 16
N_GROUPS = 4
EXPERTS_PER_GROUP = 8
N_EXPERTS = N_GROUPS * EXPERTS_PER_GROUP
D_EXPERT = 512
MOE_BLOCK = 128
ROPE_BASE = 10000.0
NORM_EPS = 1e-6

T_PROMPT = BATCH * SEQ
T_SAMPLE = DEC_BATCH * DEC_SEQ
T_ALL = T_PROMPT + T_SAMPLE
N_COND = 8
LANES = 128
SLABS = D_MODEL // LANES
TOK_BLOCK = 256
POST_BLOCK = 2 * TOK_BLOCK
MLSTM_HEADS_PER_STEP = 2
EXPERT_ROWS = 512
NA_QROWS = 4
NA_KROWS = 12
NEG = -1e30
VMEM_LIMIT = 56 * 1024 * 1024


def _cparams(sem):
    return pltpu.CompilerParams(dimension_semantics=sem, vmem_limit_bytes=VMEM_LIMIT)


def _split2(x):
    hi = x.astype(BF16)
    lo = (x - hi.astype(F32)).astype(BF16)
    return hi, lo


def _split3(x):
    hi = x.astype(BF16)
    r = x - hi.astype(F32)
    mid = r.astype(BF16)
    lo = (r - mid.astype(F32)).astype(BF16)
    return hi, mid, lo


def _dot(a, b):
    return jnp.dot(a, b, preferred_element_type=F32)


def _dot_nt(a, b):
    return lax.dot_general(a, b, (((1,), (1,)), ((), ())), preferred_element_type=F32)


def _dot_exact_rhs(x, b):
    hi, mid, lo = _split3(x)
    return _dot(hi, b) + _dot(mid, b) + _dot(lo, b)


def _rms(x, g):
    ms = jnp.mean(x * x, axis=-1, keepdims=True)
    return (x * lax.rsqrt(ms + NORM_EPS)) * g


def _mod_row(tok0):
    return jnp.where(tok0 < T_PROMPT, 0, 1 + (tok0 - T_PROMPT) // DEC_SEQ)


def _mod_part(mod_ref, r, idx):
    return mod_ref[pl.ds(r, 1), idx * D_MODEL:(idx + 1) * D_MODEL]


def _head_rms(x, w, bd):
    hi, lo = _split2(x * x)
    ms = _dot(hi, bd) + _dot(lo, bd)
    return (x * lax.rsqrt(ms + NORM_EPS)) * w


def _to_token_tiles(ref, x, tile0=0):
    m = x.shape[0]
    for s in range(SLABS):
        ref[pl.ds(tile0 * SLABS + s, m, stride=SLABS), :] = x[:, s * LANES:(s + 1) * LANES]


def _from_token_tiles(ref, tile0, m):
    return jnp.concatenate([ref[pl.ds(tile0 * SLABS + s, m, stride=SLABS), :] for s in range(SLABS)], axis=1)


def _lane(shape):
    return lax.broadcasted_iota(jnp.int32, shape, len(shape) - 1)


def _dup_half(x, g):
    xr = pltpu.roll(x, 64, 1)
    lo = _lane(x.shape) < 64
    return jnp.where(lo, x, xr) if g == 0 else jnp.where(lo, xr, x)


def _rope(x, cos, sin_signed):
    lane = _lane(x.shape)
    partner = jnp.where((lane % 32) < 16, pltpu.roll(x, LANES - 16, 1), pltpu.roll(x, 16, 1))
    return x * cos + partner * sin_signed


def _with_ones(v):
    return jnp.concatenate([v.astype(BF16), jnp.ones(v.shape, BF16)], axis=1)


def _softmax_pair(qp, score_fn, value_fn):
    lo = _lane(qp.shape) < 64
    outs = []
    for half in range(2):
        qm = jnp.where(lo if half == 0 else jnp.logical_not(lo), qp, 0.0).astype(BF16)
        ss = score_fn(qm)
        m = ss[0].max(axis=-1, keepdims=True)
        for s in ss[1:]:
            m = jnp.maximum(m, s.max(axis=-1, keepdims=True))
        o = value_fn([jnp.exp(s - m).astype(BF16) for s in ss])
        outs.append(o[:, :LANES] / o[:, LANES:])
    return jnp.where(lo, outs[0], outs[1])


def _mod_kernel(cond_ref, w_ref, b_ref, o_ref):
    c = cond_ref[...]
    s = c * jax.nn.sigmoid(c)
    s_hi, s_lo = _split2(s)
    w_hi, w_lo = _split2(w_ref[0])
    o_ref[0] = _dot(s_hi, w_hi) + _dot(s_lo, w_hi) + _dot(s_hi, w_lo) + b_ref[0]


def _modulation(cond, ada_w, ada_b):
    depth, d, n = ada_w.shape
    tn = 1536
    return pl.pallas_call(
        _mod_kernel,
        grid=(depth, n // tn),
        in_specs=[pl.BlockSpec((N_COND, d), lambda l, j: (0, 0)),
                  pl.BlockSpec((1, d, tn), lambda l, j: (l, 0, j)),
                  pl.BlockSpec((1, 1, tn), lambda l, j: (l, 0, j))],
        out_specs=pl.BlockSpec((1, N_COND, tn), lambda l, j: (l, 0, j)),
        out_shape=jax.ShapeDtypeStruct((depth, N_COND, n), F32),
        compiler_params=_cparams(("arbitrary", "arbitrary")),
        name="adaln_modulation",
    )(cond, ada_w, ada_b.reshape(depth, 1, n))


def _norm_mod(x_ref, mod_ref, g_ref, shift_idx, scale_idx, row0):
    r = _mod_row(row0 + pl.program_id(0) * x_ref.shape[0])
    h = _rms(x_ref[...], g_ref[...])
    return h * (1.0 + _mod_part(mod_ref, r, scale_idx)) + _mod_part(mod_ref, r, shift_idx)


def _proj_ab_kernel(x_ref, mod_ref, g_ref, wb_ref, wq_ref, wkv_ref, wgt_ref,
                    ob_ref, oq_ref, okv_ref, okt_ref, ogt_ref, *, row0):
    tm = x_ref.shape[0]
    hb = _norm_mod(x_ref, mod_ref, g_ref, 0, 1, row0).astype(BF16)
    ob_ref[...] = _dot(hb, wb_ref[...]).astype(ob_ref.dtype)
    oq_ref[...] = _dot(hb, wq_ref[...]).astype(oq_ref.dtype)
    okv_ref[...] = _dot(hb, wkv_ref[...]).astype(okv_ref.dtype)
    gt = _dot_nt(wgt_ref[...], hb)
    for j in range(tm // LANES):
        okt_ref[j] = gt[:B_WIDTH, j * LANES:(j + 1) * LANES].astype(okt_ref.dtype)
        ogt_ref[j] = gt[B_WIDTH:, j * LANES:(j + 1) * LANES]


def _proj_ab(x, mod, g, wb, wq, wkv, wgt, row0):
    t = x.shape[0]
    tm = TOK_BLOCK
    full = lambda a: pl.BlockSpec(a.shape, lambda i: (0,) * a.ndim)
    return pl.pallas_call(
        functools.partial(_proj_ab_kernel, row0=row0),
        grid=(t // tm,),
        in_specs=[pl.BlockSpec((tm, D_MODEL), lambda i: (i, 0)), full(mod), full(g),
                  full(wb), full(wq), full(wkv), full(wgt)],
        out_specs=[pl.BlockSpec((tm, wb.shape[1]), lambda i: (i, 0)),
                   pl.BlockSpec((tm, A_WIDTH), lambda i: (i, 0)),
                   pl.BlockSpec((tm, 2 * A_KV_WIDTH), lambda i: (i, 0)),
                   pl.BlockSpec((tm // LANES, B_WIDTH, LANES), lambda i: (i, 0, 0)),
                   pl.BlockSpec((tm // LANES, wgt.shape[0] - B_WIDTH, LANES), lambda i: (i, 0, 0))],
        out_shape=[jax.ShapeDtypeStruct((t, wb.shape[1]), BF16),
                   jax.ShapeDtypeStruct((t, A_WIDTH), BF16),
                   jax.ShapeDtypeStruct((t, 2 * A_KV_WIDTH), BF16),
                   jax.ShapeDtypeStruct((t // LANES, B_WIDTH, LANES), BF16),
                   jax.ShapeDtypeStruct((t // LANES, wgt.shape[0] - B_WIDTH, LANES), F32)],
        compiler_params=_cparams(("arbitrary",)),
        name="proj_ab",
    )(x, mod, g, wb, wq, wkv, wgt)


def _gqa_block(q_ref, qn, bd, kd_ref, vd_ref, o_ref, rope=None):
    for p in range(A_HEADS // 2):
        g = p // (A_HEADS // 2 // A_KV_HEADS)
        qp = _head_rms(q_ref[:, p * LANES:(p + 1) * LANES].astype(F32), qn, bd)
        if rope is not None:
            qp = _rope(qp, rope[0], rope[1])
        qp = qp * (A_HEAD_DIM ** -0.5)
        o_ref[:, p * LANES:(p + 1) * LANES] = _softmax_pair(
            qp, lambda qm: [_dot_nt(qm, kd_ref[g])], lambda ps: _dot(ps[0], vd_ref[g])).astype(o_ref.dtype)


def _attn_a_prompt_kernel(q_ref, kv_ref, qn_ref, kn_ref, bd_ref, o_ref, knew_ref, vnew_ref, kd_ref, vd_ref):
    bd = bd_ref[...]
    k = _head_rms(kv_ref[:, :LANES].astype(F32), kn_ref[...], bd)
    v = kv_ref[:, LANES:].astype(F32)
    knew_ref[...] = k
    vnew_ref[...] = v
    for g in range(A_KV_HEADS):
        kd_ref[g] = _dup_half(k, g).astype(BF16)
        vd_ref[g] = _with_ones(_dup_half(v, g))
    _gqa_block(q_ref, qn_ref[...], bd, kd_ref, vd_ref, o_ref)


def _attn_a_prompt(q, kv, qn, kn, bd):
    nb = BATCH
    full = lambda a: pl.BlockSpec(a.shape, lambda b: (0,) * a.ndim)
    return pl.pallas_call(
        _attn_a_prompt_kernel,
        grid=(nb,),
        in_specs=[pl.BlockSpec((SEQ, A_WIDTH), lambda b: (b, 0)),
                  pl.BlockSpec((SEQ, 2 * A_KV_WIDTH), lambda b: (b, 0)),
                  full(qn), full(kn), full(bd)],
        out_specs=[pl.BlockSpec((SEQ, A_WIDTH), lambda b: (b, 0)),
                   pl.BlockSpec((SEQ, A_KV_WIDTH), lambda b: (b, 0)),
                   pl.BlockSpec((SEQ, A_KV_WIDTH), lambda b: (b, 0))],
        out_shape=[jax.ShapeDtypeStruct((T_PROMPT, A_WIDTH), BF16),
                   jax.ShapeDtypeStruct((T_PROMPT, A_KV_WIDTH), F32),
                   jax.ShapeDtypeStruct((T_PROMPT, A_KV_WIDTH), F32)],
        scratch_shapes=[pltpu.VMEM((A_KV_HEADS, SEQ, LANES), BF16),
                        pltpu.VMEM((A_KV_HEADS, SEQ, 2 * LANES), BF16)],
        compiler_params=_cparams(("arbitrary",)),
        name="attn_a_prompt",
    )(q, kv, qn, kn, bd)


_A_QBLOCK = 256


def _attn_a_sample_kernel(q_ref, kv_ref, ck_ref, cv_ref, cos_ref, sin_ref, cosq_ref, sinq_ref,
                          qn_ref, kn_ref, bd_ref, o_ref, kd_ref, vd_ref):
    bd = bd_ref[...]

    @pl.when(pl.program_id(1) == 0)
    def _():
        for g in range(A_KV_HEADS):
            kd_ref[g, :PAST_LEN] = _dup_half(ck_ref[0], g).astype(BF16)
            vd_ref[g, :PAST_LEN] = _with_ones(_dup_half(cv_ref[0], g))
        rows = 256
        for c in range(DEC_SEQ // rows):
            sl = slice(c * rows, (c + 1) * rows)
            k = _head_rms(kv_ref[sl, :LANES].astype(F32), kn_ref[...], bd)
            k = _rope(k, cos_ref[sl, :], sin_ref[sl, :])
            v = kv_ref[sl, LANES:].astype(F32)
            dst = slice(PAST_LEN + c * rows, PAST_LEN + (c + 1) * rows)
            for g in range(A_KV_HEADS):
                kd_ref[g, dst] = _dup_half(k, g).astype(BF16)
                vd_ref[g, dst] = _with_ones(_dup_half(v, g))

    _gqa_block(q_ref, qn_ref[...], bd, kd_ref, vd_ref, o_ref, rope=(cosq_ref[...], sinq_ref[...]))


def _attn_a_sample(q, kv, ck, cv, cos, sin, qn, kn, bd):
    nq = DEC_SEQ // _A_QBLOCK
    full = lambda a: pl.BlockSpec(a.shape, lambda b, i: (0,) * a.ndim)
    tk = PAST_LEN + DEC_SEQ
    return pl.pallas_call(
        _attn_a_sample_kernel,
        grid=(DEC_BATCH, nq),
        in_specs=[pl.BlockSpec((_A_QBLOCK, A_WIDTH), lambda b, i: (b * nq + i, 0)),
                  pl.BlockSpec((DEC_SEQ, 2 * A_KV_WIDTH), lambda b, i: (b, 0)),
                  pl.BlockSpec((1, PAST_LEN, A_KV_WIDTH), lambda b, i: (b, 0, 0)),
                  pl.BlockSpec((1, PAST_LEN, A_KV_WIDTH), lambda b, i: (b, 0, 0)),
                  full(cos), full(sin),
                  pl.BlockSpec((_A_QBLOCK, LANES), lambda b, i: (i, 0)),
                  pl.BlockSpec((_A_QBLOCK, LANES), lambda b, i: (i, 0)),
                  full(qn), full(kn), full(bd)],
        out_specs=pl.BlockSpec((_A_QBLOCK, A_WIDTH), lambda b, i: (b * nq + i, 0)),
        out_shape=jax.ShapeDtypeStruct((T_SAMPLE, A_WIDTH), BF16),
        scratch_shapes=[pltpu.VMEM((A_KV_HEADS, tk, LANES), BF16),
                        pltpu.VMEM((A_KV_HEADS, tk, 2 * LANES), BF16)],
        compiler_params=_cparams(("arbitrary", "arbitrary")),
        name="attn_a_sample",
    )(q, kv, ck, cv, cos, sin, cos, sin, qn, kn, bd)


def _log_sigmoid(x):
    return -(jnp.maximum(-x, 0.0) + jnp.log1p(jnp.exp(-jnp.abs(x))))


def _mlstm_kernel(*refs, n, has_init, emit_state):
    it = iter(refs)
    q_ref, v_ref, og_ref, kt_ref, gt_ref, brow_ref, onorm_ref, tri_ref = [next(it) for _ in range(8)]
    if has_init:
        c0_ref, m0_ref = [next(it) for _ in range(2)]
    out_ref = next(it)
    if emit_state:
        cT_ref, nT_ref, mT_ref = [next(it) for _ in range(3)]
    h_ref, row_ref, c_ref = [next(it) for _ in range(3)]

    L = MLSTM_CHUNK
    D = B_HEAD_DIM
    nc = n // L
    hb = MLSTM_HEADS_PER_STEP

    lane = lax.broadcasted_iota(jnp.int32, (nc * 8, L), 1)
    is_fwd = lax.broadcasted_iota(jnp.int32, (nc * 8, L), 0) % 8 == 0

    def running_max(x, suffix):
        for sh in (1, 2, 4, 8, 16, 32, 64):
            if suffix:
                x = jnp.where(lane < L - sh, jnp.maximum(x, pltpu.roll(x, L - sh, 1)), x)
            else:
                x = jnp.where(lane >= sh, jnp.maximum(x, pltpu.roll(x, sh, 1)), x)
        return x

    for hh in range(hb):
        gt = gt_ref[:, 16 * hh:16 * hh + 16, :] + brow_ref[hh][None]
        li = gt[:, 0:8, :].reshape(nc * 8, L)
        lf = _log_sigmoid(gt[:, 8:16, :]).reshape(nc * 8, L)
        cum = jnp.where(is_fwd, _dot_exact_rhs(lf, tri_ref[0]), _dot_exact_rhs(lf, tri_ref[1]))
        a = li - cum
        planes = (cum, a, jnp.broadcast_to(lf.sum(axis=-1, keepdims=True), (nc * 8, L)),
                  jnp.where(is_fwd, running_max(a, False), running_max(a, True)),
                  jnp.broadcast_to(a.max(axis=-1, keepdims=True), (nc * 8, L)))
        for p, val in enumerate(planes):
            row_ref[hh, p] = val.reshape(nc, 8, L)

    chains = [(hh, d) for hh in range(hb) for d in range(2)]
    if has_init:
        m_init = []
        for hh, d in chains:
            c_ref[2 * hh + d] = c0_ref[0, d, hh]
            m_init.append(m0_ref[0, hh, d:d + 1, :])
        m_init = tuple(m_init)
    else:
        c_ref[...] = jnp.zeros(c_ref.shape, F32)
        m_init = tuple(jnp.zeros((1, L), F32) for _ in chains)

    t_idx = lax.broadcasted_iota(jnp.int32, (L, L), 0)
    s_idx = lax.broadcasted_iota(jnp.int32, (L, L), 1)
    masks = (s_idx <= t_idx, s_idx >= t_idx)
    ones = jnp.ones((L, L), F32)

    def step(c, hh, d, m):
        r0 = pl.multiple_of(c * L, L)
        hl = slice(hh * L, (hh + 1) * L)
        qb = q_ref[pl.ds(r0, L), hl].astype(BF16)
        kst = kt_ref[c, hl, :].astype(F32) * (D ** -0.5)
        v_ext = jnp.concatenate([v_ref[pl.ds(r0, L), hl], ones], axis=1).astype(BF16)
        cum, a_row, tot, amax_run, amax = [row_ref[hh, p, c][d:d + 1, :] for p in range(5)]
        m_cum = jnp.broadcast_to(cum, (L, L)).T
        m_run = jnp.broadcast_to(amax_run, (L, L)).T
        dlog = jnp.where(masks[d], m_cum + a_row, -jnp.inf)
        inter = m_cum + m
        m_t = jnp.maximum(inter, m_cum + m_run)
        w_in = jnp.exp(dlog - m_t)
        w_st = jnp.exp(inter - m_t)
        a = _dot(qb, kst.astype(BF16)) * w_in
        ci = 2 * hh + d
        cext = c_ref[ci]
        p_state = _dot(qb, cext.astype(BF16))
        p_intra = _dot(a.astype(BF16), v_ext)
        num = w_st * p_state[:, :D] + p_intra[:, :D]
        den = w_st * p_state[:, D:] + p_intra[:, D:]
        h_ref[ci, pl.ds(r0, L), :] = num / jnp.maximum(jnp.abs(den), jnp.exp(-m_t))
        m_new = jnp.maximum(tot + m, amax + tot)
        ws = jnp.exp(a_row + tot - m_new)
        wc = jnp.exp(tot + m - m_new)
        c_ref[ci] = jnp.concatenate([wc, wc], axis=1) * cext + _dot((kst * ws).astype(BF16), v_ext)
        return m_new

    def body(i, carry):
        return tuple(step(i if d == 0 else nc - 1 - i, hh, d, m) for (hh, d), m in zip(chains, carry))

    m_fin = lax.fori_loop(0, nc, body, m_init)

    for hh in range(hb):
        hm = h_ref[2 * hh] + h_ref[2 * hh + 1]
        hl = slice(hh * L, (hh + 1) * L)
        gate = jax.nn.sigmoid(og_ref[:, hl].astype(F32))
        out_ref[:, hl] = (_rms(hm, onorm_ref[hh]) * gate).astype(out_ref.dtype)

    if emit_state:
        for k, (hh, d) in enumerate(chains):
            cext = c_ref[2 * hh + d]
            cT_ref[0, d, hh] = cext[:, :D]
            nT_ref[0, hh, d:d + 1, :] = cext[:, D:].T[0:1, :]
            mT_ref[0, hh, d:d + 1, :] = m_fin[k]


def _mlstm(ob, okt, ogt, brow, onorm, tri, *, n, nseq, init=None, emit_state=False):
    L = MLSTM_CHUNK
    nc = n // L
    H = B_HEADS
    hb = MLSTM_HEADS_PER_STEP
    ng = H // hb
    col = lambda part: (lambda b, g: (b, part * ng + g))
    in_specs = [pl.BlockSpec((n, hb * L), col(0)), pl.BlockSpec((n, hb * L), col(1)),
                pl.BlockSpec((n, hb * L), col(2)),
                pl.BlockSpec((nc, hb * L, L), lambda b, g: (b, g, 0)),
                pl.BlockSpec((nc, 16 * hb, L), lambda b, g: (b, g, 0)),
                pl.BlockSpec((hb, 16, L), lambda b, g: (g, 0, 0)),
                pl.BlockSpec((hb, 1, L), lambda b, g: (g, 0, 0)),
                pl.BlockSpec(tri.shape, lambda b, g: (0, 0, 0))]
    args = [ob, ob, ob, okt, ogt, brow, onorm, tri]
    if init is not None:
        in_specs += [pl.BlockSpec((1, 2, hb, L, 2 * L), lambda b, g: (b, 0, g, 0, 0)),
                     pl.BlockSpec((1, hb, 2, L), lambda b, g: (b, g, 0, 0))]
        args += list(init)
    out_specs = [pl.BlockSpec((n, hb * L), lambda b, g: (b, g))]
    out_shape = [jax.ShapeDtypeStruct((nseq * n, B_WIDTH), BF16)]
    if emit_state:
        out_specs += [pl.BlockSpec((1, 2, hb, L, L), lambda b, g: (b, 0, g, 0, 0)),
                      pl.BlockSpec((1, hb, 2, L), lambda b, g: (b, g, 0, 0)),
                      pl.BlockSpec((1, hb, 2, L), lambda b, g: (b, g, 0, 0))]
        out_shape += [jax.ShapeDtypeStruct((nseq, 2, H, L, L), F32),
                      jax.ShapeDtypeStruct((nseq, H, 2, L), F32),
                      jax.ShapeDtypeStruct((nseq, H, 2, L), F32)]
    return pl.pallas_call(
        functools.partial(_mlstm_kernel, n=n, has_init=init is not None, emit_state=emit_state),
        grid=(nseq, ng),
        in_specs=in_specs,
        out_specs=out_specs,
        out_shape=out_shape,
        scratch_shapes=[pltpu.VMEM((2 * hb, n, L), F32),
                        pltpu.VMEM((hb, 5, nc, 8, L), F32),
                        pltpu.VMEM((2 * hb, L, 2 * L), F32)],
        compiler_params=_cparams(("arbitrary", "arbitrary")),
        name="mlstm_init" if init is not None else "mlstm",
    )(*args)


def _router(logits):
    lane = _lane(logits.shape).astype(F32)
    big = 1e9
    gl = jnp.where(lane < N_GROUPS, logits, -jnp.inf)
    gmax = gl.max(axis=-1, keepdims=True)
    g_sel = jnp.where(gl == gmax, lane, big).min(axis=-1, keepdims=True)
    g_prob = 1.0 / jnp.exp(gl - gmax).sum(axis=-1, keepdims=True)
    lo = N_GROUPS + EXPERTS_PER_GROUP * g_sel
    el = jnp.where(lane >= lo, jnp.where(lane < lo + EXPERTS_PER_GROUP, logits, -jnp.inf), -jnp.inf)
    v1 = el.max(axis=-1, keepdims=True)
    i1 = jnp.where(el == v1, lane, big).min(axis=-1, keepdims=True)
    el2 = jnp.where(lane == i1, -jnp.inf, el)
    v2 = el2.max(axis=-1, keepdims=True)
    i2 = jnp.where(el2 == v2, lane, big).min(axis=-1, keepdims=True)
    e2 = jnp.exp(v2 - v1)
    w1 = g_prob / (1.0 + e2)
    w2 = g_prob * e2 / (1.0 + e2)
    return i1, i2, w1, w2


def _read_tokens(refs, is_prompt, rows):
    if len(refs) == 1:
        return refs[0][rows, :]
    return jnp.where(is_prompt, refs[0][rows, :], refs[1][rows, :])


def _post_kernel(*refs, groups):
    it = iter(refs)
    tok_refs = [[next(it) for _ in range(n)] for n in groups]
    w_refs = [next(it) for _ in range(len(groups) - 1)]
    mod_ref, g_ref, wr_ref, br_ref, ls_ref, sel_ref = [next(it) for _ in range(6)]
    xnew_ref, xn_ref, rt_ref, rtt_ref, cnt_ref = [next(it) for _ in range(5)]
    run_ref = next(it)
    i = pl.program_id(0)
    tm = xnew_ref.shape[0]
    is_prompt = i * tm < T_PROMPT
    r = _mod_row(i * tm)

    @pl.when(i == 0)
    def _():
        run_ref[...] = jnp.zeros(run_ref.shape, F32)

    sub = ls_ref.shape[0]
    for h in range(tm // sub):
        rows = slice(h * sub, (h + 1) * sub)
        acc = None
        for a_refs, w_ref in zip(tok_refs[1:], w_refs):
            d = _dot(_read_tokens(a_refs, is_prompt, rows).astype(BF16), w_ref[...])
            acc = d if acc is None else acc + d
        xnew = _read_tokens(tok_refs[0], is_prompt, rows) + _mod_part(mod_ref, r, 2) * acc
        xnew_ref[rows, :] = xnew
        xn = _rms(xnew, g_ref[...]) * (1.0 + _mod_part(mod_ref, r, 4)) + _mod_part(mod_ref, r, 3)
        _to_token_tiles(xn_ref, xn, h * sub)
        x_hi, x_lo = _split2(xn)
        both = _dot(x_hi, wr_ref[...])
        logits = both[:, :LANES] + both[:, LANES:] + _dot(x_lo, wr_ref[:, :LANES]) + br_ref[...]
        i1, i2, w1, w2 = _router(logits)

        lane = _lane(logits.shape).astype(F32)
        member = jnp.where(lane == i1, 1.0, jnp.where(lane == i2, 1.0, 0.0))
        before = _dot(ls_ref[...], member.astype(BF16)) + run_ref[...]
        rank1 = jnp.where(lane == i1, before, 0.0).sum(axis=-1, keepdims=True)
        rank2 = jnp.where(lane == i2, before, 0.0).sum(axis=-1, keepdims=True)
        run_ref[...] = run_ref[...] + member.sum(axis=0, keepdims=True)
        cols = (i1 - N_GROUPS, i2 - N_GROUPS, w1, w2, rank1, rank2)
        rt = jnp.zeros(logits.shape, F32)
        for k, c in enumerate(cols):
            rt = jnp.where(lane == k, c, rt)
        rt_ref[rows, :] = rt
        hi, mid, lo = _split3(rt)
        sel = sel_ref[...]
        rtt_ref[:, rows] = _dot_nt(sel, hi) + _dot_nt(sel, mid) + _dot_nt(sel, lo)
    cnt_ref[...] = run_ref[...]


def _post(tok_ops, w_list, mod, g, wr, br):
    t = T_ALL
    tm = POST_BLOCK
    npb = T_PROMPT // tm
    full = lambda a: pl.BlockSpec(a.shape, lambda i: (0,) * a.ndim)
    specs, args, groups = [], [], []
    for op in tok_ops:
        if isinstance(op, tuple):
            w = op[0].shape[1]
            specs += [pl.BlockSpec((tm, w), lambda i: (jnp.minimum(i, npb - 1), 0)),
                      pl.BlockSpec((tm, w), lambda i: (jnp.maximum(i - npb, 0), 0))]
            args += list(op)
            groups.append(2)
        else:
            specs.append(pl.BlockSpec((tm, op.shape[1]), lambda i: (i, 0)))
            args.append(op)
            groups.append(1)
    idx = np.arange(TOK_BLOCK)
    ls = jnp.asarray(idx[:, None] > idx[None, :], BF16)
    sel = jnp.asarray(np.arange(8)[:, None] == np.arange(LANES)[None, :], BF16)
    consts = [mod, g, wr, br, ls, sel]
    return pl.pallas_call(
        functools.partial(_post_kernel, groups=tuple(groups)),
        grid=(t // tm,),
        in_specs=specs + [full(w) for w in w_list] + [full(a) for a in consts],
        out_specs=[pl.BlockSpec((tm, D_MODEL), lambda i: (i, 0)),
                   pl.BlockSpec((tm * SLABS, LANES), lambda i: (i, 0)),
                   pl.BlockSpec((tm, LANES), lambda i: (i, 0)),
                   pl.BlockSpec((8, tm), lambda i: (0, i)),
                   pl.BlockSpec((1, LANES), lambda i: (0, 0))],
        out_shape=[jax.ShapeDtypeStruct((t, D_MODEL), F32),
                   jax.ShapeDtypeStruct((t * SLABS, LANES), F32),
                   jax.ShapeDtypeStruct((t, LANES), F32),
                   jax.ShapeDtypeStruct((8, t), F32),
                   jax.ShapeDtypeStruct((1, LANES), F32)],
        scratch_shapes=[pltpu.VMEM((1, LANES), F32)],
        compiler_params=_cparams(("arbitrary",)),
        name="post_mixer_router",
    )(*args, *w_list, *consts)


def _expert_kernel(be_ref, st_ref, nv_ref, xn_hbm, w1_ref, w3_ref, w2_ref, o_ref, xa, xb, sem, w1b, w3b, w2b):
    i = pl.program_id(0)
    nv = nv_ref[0]
    active = i < nv
    rows = EXPERT_ROWS

    def gather(blk, buf, s):
        for r in range(rows):
            tok = st_ref[blk * rows + r]
            pltpu.make_async_copy(xn_hbm.at[pl.ds(pl.multiple_of(tok * SLABS, SLABS), SLABS), :],
                                  buf.at[pl.ds(r * SLABS, SLABS), :], sem.at[s]).start()

    def wait(buf, s):
        pltpu.make_async_copy(xn_hbm.at[pl.ds(0, rows * SLABS), :], buf, sem.at[s]).wait()

    @pl.when(i == 0)
    def _():
        gather(0, xa, 0)

    changed = jnp.logical_or(i == 0, be_ref[i] != be_ref[jnp.maximum(i - 1, 0)])

    @pl.when(jnp.logical_and(changed, active))
    def _():
        w1b[...] = w1_ref[0, 0].astype(BF16)
        w3b[...] = w3_ref[0, 0].astype(BF16)
        w2b[...] = w2_ref[0, 0].astype(BF16)

    nxt = jnp.minimum(i + 1, nv - 1)

    def step(cur, s_cur, oth, s_oth):
        wait(cur, s_cur)
        gather(nxt, oth, s_oth)
        x = _from_token_tiles(cur, 0, rows).astype(BF16)
        h1 = _dot(x, w1b[...])
        h3 = _dot(x, w3b[...])
        hid = (h1 * jax.nn.sigmoid(h1)) * h3
        _to_token_tiles(o_ref, _dot(hid.astype(BF16), w2b[...]))

    @pl.when(jnp.logical_and(active, i % 2 == 0))
    def _():
        step(xa, 0, xb, 1)

    @pl.when(jnp.logical_and(active, i % 2 == 1))
    def _():
        step(xb, 1, xa, 0)

    @pl.when(jnp.logical_not(active))
    def _():
        o_ref[...] = jnp.zeros(o_ref.shape, F32)

    @pl.when(i == nv - 1)
    def _():
        @pl.when(i % 2 == 0)
        def _():
            wait(xb, 1)

        @pl.when(i % 2 == 1)
        def _():
            wait(xa, 0)


def _experts(block_expert, slot_tok, n_valid, xn, w1, w3, w2, layer):
    nblk = block_expert.shape[0]
    rows = EXPERT_ROWS
    grid_spec = pltpu.PrefetchScalarGridSpec(
        num_scalar_prefetch=3,
        grid=(nblk,),
        in_specs=[pl.BlockSpec(memory_space=pl.ANY),
                  pl.BlockSpec((1, 1, D_MODEL, D_EXPERT), lambda i, be, st, nv: (layer, be[i], 0, 0)),
                  pl.BlockSpec((1, 1, D_MODEL, D_EXPERT), lambda i, be, st, nv: (layer, be[i], 0, 0)),
                  pl.BlockSpec((1, 1, D_EXPERT, D_MODEL), lambda i, be, st, nv: (layer, be[i], 0, 0))],
        out_specs=pl.BlockSpec((rows * SLABS, LANES), lambda i, be, st, nv: (i, 0)),
        scratch_shapes=[pltpu.VMEM((rows * SLABS, LANES), F32),
                        pltpu.VMEM((rows * SLABS, LANES), F32),
                        pltpu.SemaphoreType.DMA((2,)),
                        pltpu.VMEM((D_MODEL, D_EXPERT), BF16),
                        pltpu.VMEM((D_MODEL, D_EXPERT), BF16),
                        pltpu.VMEM((D_EXPERT, D_MODEL), BF16)])
    return pl.pallas_call(
        _expert_kernel,
        grid_spec=grid_spec,
        out_shape=jax.ShapeDtypeStruct((nblk * rows * SLABS, LANES), F32),
        compiler_params=_cparams(("arbitrary",)),
        name="moe_experts",
    )(block_expert, slot_tok, n_valid, xn, w1, w3, w2)


_COMBINE_BLOCK = 128


def _combine_kernel(dest_ref, yb_hbm, x_ref, rt_ref, mod_ref, gfin_ref, o_ref, ybuf, sem, *, final_norm, row0):
    i = pl.program_id(0)
    nblk = pl.num_programs(0)
    slot = i % 2
    tm = _COMBINE_BLOCK

    def issue(blk, s):
        def body(j, carry):
            for c in range(2):
                d = dest_ref[c * T_ALL + row0 + blk * tm + j]
                pltpu.make_async_copy(yb_hbm.at[pl.ds(pl.multiple_of(d * SLABS, SLABS), SLABS), :],
                                      ybuf.at[s, pl.ds(pl.multiple_of((c * tm + j) * SLABS, SLABS), SLABS), :],
                                      sem.at[s]).start()
            return carry
        lax.fori_loop(0, tm, body, 0, unroll=4)

    @pl.when(i == 0)
    def _():
        issue(0, 0)

    @pl.when(i + 1 < nblk)
    def _():
        issue(i + 1, 1 - slot)

    pltpu.make_async_copy(yb_hbm.at[pl.ds(0, 2 * tm * SLABS), :], ybuf.at[slot], sem.at[slot]).wait()
    r = _mod_row(row0 + i * tm)
    rt = rt_ref[...]
    yv = ybuf.at[slot]
    y = rt[:, 2:3] * _from_token_tiles(yv, 0, tm) + rt[:, 3:4] * _from_token_tiles(yv, tm, tm)
    out = x_ref[...] + _mod_part(mod_ref, r, 5) * y
    if final_norm:
        out = _rms(out, gfin_ref[...])
    o_ref[...] = out


def _combine(dest, yb, x, rt, mod, gfin, final_norm, row0=0, t=T_ALL):
    tm = _COMBINE_BLOCK
    blk0 = row0 // tm
    grid_spec = pltpu.PrefetchScalarGridSpec(
        num_scalar_prefetch=1,
        grid=(t // tm,),
        in_specs=[pl.BlockSpec(memory_space=pl.ANY),
                  pl.BlockSpec((tm, D_MODEL), lambda i, d: (blk0 + i, 0)),
                  pl.BlockSpec((tm, LANES), lambda i, d: (blk0 + i, 0)),
                  pl.BlockSpec(mod.shape, lambda i, d: (0, 0)),
                  pl.BlockSpec(gfin.shape, lambda i, d: (0, 0))],
        out_specs=pl.BlockSpec((tm, D_MODEL), lambda i, d: (i, 0)),
        scratch_shapes=[pltpu.VMEM((2, 2 * tm * SLABS, LANES), F32),
                        pltpu.SemaphoreType.DMA((2,))])
    return pl.pallas_call(
        functools.partial(_combine_kernel, final_norm=final_norm, row0=row0),
        grid_spec=grid_spec,
        out_shape=jax.ShapeDtypeStruct((t, D_MODEL), F32),
        compiler_params=_cparams(("arbitrary",)),
        name="moe_combine",
    )(dest, yb, x, rt, mod, gfin)


def _combine_proj_kernel(dest_ref, yb_hbm, x_ref, rt_ref, mod0_ref, mod1_ref, g_ref, w_ref,
                         xo_ref, q_ref, k_ref, v_ref, ya, yb, sem, *, row0):
    i = pl.program_id(0)
    nblk = pl.num_programs(0)
    tm = x_ref.shape[0]

    def gather(blk, buf, s):
        for j in range(tm):
            for c in range(2):
                d = dest_ref[c * T_ALL + row0 + blk * tm + j]
                pltpu.make_async_copy(yb_hbm.at[pl.ds(pl.multiple_of(d * SLABS, SLABS), SLABS), :],
                                      buf.at[pl.ds((c * tm + j) * SLABS, SLABS), :], sem.at[s]).start()

    def wait(buf, s):
        pltpu.make_async_copy(yb_hbm.at[pl.ds(0, 2 * tm * SLABS), :], buf, sem.at[s]).wait()

    @pl.when(i == 0)
    def _():
        gather(0, ya, 0)

    nxt = jnp.minimum(i + 1, nblk - 1)

    def step(cur, s_cur, oth, s_oth):
        wait(cur, s_cur)
        gather(nxt, oth, s_oth)
        r = _mod_row(row0 + i * tm)
        rt = rt_ref[...]
        y = rt[:, 2:3] * _from_token_tiles(cur, 0, tm) + rt[:, 3:4] * _from_token_tiles(cur, tm, tm)
        x = x_ref[...] + _mod_part(mod0_ref, r, 5) * y
        xo_ref[...] = x
        h = _rms(x, g_ref[...]) * (1.0 + _mod_part(mod1_ref, r, 1)) + _mod_part(mod1_ref, r, 0)
        hb = h.astype(BF16)
        for j, o_ref in enumerate((q_ref, k_ref, v_ref)):
            o_ref[...] = _dot(hb, w_ref[:, j * C_WIDTH:(j + 1) * C_WIDTH]).astype(o_ref.dtype)

    @pl.when(i % 2 == 0)
    def _():
        step(ya, 0, yb, 1)

    @pl.when(i % 2 == 1)
    def _():
        step(yb, 1, ya, 0)

    @pl.when(i == nblk - 1)
    def _():
        @pl.when(i % 2 == 0)
        def _():
            wait(yb, 1)

        @pl.when(i % 2 == 1)
        def _():
            wait(ya, 0)


def _combine_proj(dest, yb, x, rt, mod0, mod1, g, w, row0, t):
    tm = TOK_BLOCK
    blk0 = row0 // tm
    full = lambda a: pl.BlockSpec(a.shape, lambda i, d: (0,) * a.ndim)
    grid_spec = pltpu.PrefetchScalarGridSpec(
        num_scalar_prefetch=1,
        grid=(t // tm,),
        in_specs=[pl.BlockSpec(memory_space=pl.ANY),
                  pl.BlockSpec((tm, D_MODEL), lambda i, d: (blk0 + i, 0)),
                  pl.BlockSpec((tm, LANES), lambda i, d: (blk0 + i, 0)),
                  full(mod0), full(mod1), full(g), full(w)],
        out_specs=[pl.BlockSpec((tm, D_MODEL), lambda i, d: (i, 0))] * 4,
        scratch_shapes=[pltpu.VMEM((2 * tm * SLABS, LANES), F32),
                        pltpu.VMEM((2 * tm * SLABS, LANES), F32),
                        pltpu.SemaphoreType.DMA((2,))])
    return pl.pallas_call(
        functools.partial(_combine_proj_kernel, row0=row0),
        grid_spec=grid_spec,
        out_shape=[jax.ShapeDtypeStruct((t, D_MODEL), F32)] + [jax.ShapeDtypeStruct((t, C_WIDTH), BF16)] * 3,
        compiler_params=_cparams(("arbitrary",)),
        name="moe_combine_proj",
    )(dest, yb, x, rt, mod0, mod1, g, w)


def _moe_plan(rtt, cnt):
    t = rtt.shape[1]
    eid = rtt[0:2].astype(jnp.int32)
    rank = rtt[4:6].astype(jnp.int32)
    counts = cnt[0, N_GROUPS:N_GROUPS + N_EXPERTS].astype(jnp.int32)
    padded = (counts + EXPERT_ROWS - 1) // EXPERT_ROWS * EXPERT_ROWS
    seg_end = jnp.cumsum(padded)
    seg_start = seg_end - padded
    experts = jnp.arange(N_EXPERTS, dtype=jnp.int32)
    start = jnp.sum(jnp.where(eid[..., None] == experts, seg_start, 0), axis=-1)
    dest = (start + rank).reshape(-1)
    n_blocks = (2 * t + N_EXPERTS * (EXPERT_ROWS - 1) + EXPERT_ROWS - 1) // EXPERT_ROWS
    tok = jnp.tile(jnp.arange(t, dtype=jnp.int32), 2)
    slot_tok = (jnp.arange(n_blocks * EXPERT_ROWS, dtype=jnp.int32) % t).at[dest].set(
        tok, unique_indices=True, mode='promise_in_bounds')
    first_row = jnp.arange(n_blocks, dtype=jnp.int32) * EXPERT_ROWS
    block_expert = jnp.minimum(jnp.sum((seg_end[None, :] <= first_row[:, None]).astype(jnp.int32), axis=1),
                               N_EXPERTS - 1)
    n_valid = (seg_end[-1:] // EXPERT_ROWS).astype(jnp.int32)
    return dest, slot_tok, block_expert, n_valid


def _attn_c_prompt_kernel(q_ref, k_ref, v_ref, o_ref):
    for p in range(C_HEADS // 2):
        sl = slice(p * LANES, (p + 1) * LANES)
        kb = k_ref[:, sl].astype(BF16)
        vb = _with_ones(v_ref[:, sl])
        qp = q_ref[:, sl].astype(F32) * (C_HEAD_DIM ** -0.5)
        o_ref[:, sl] = _softmax_pair(qp, lambda qm: [_dot_nt(qm, kb)],
                                     lambda ps: _dot(ps[0], vb)).astype(o_ref.dtype)


def _attn_c_prompt(q, k, v):
    blk = pl.BlockSpec((SEQ, C_WIDTH), lambda b: (b, 0))
    return pl.pallas_call(
        _attn_c_prompt_kernel,
        grid=(BATCH,),
        in_specs=[blk, blk, blk],
        out_specs=blk,
        out_shape=jax.ShapeDtypeStruct((T_PROMPT, C_WIDTH), BF16),
        compiler_params=_cparams(("arbitrary",)),
        name="attn_c_prompt",
    )(q, k, v)


def _na_key_start(r0):
    rows = DEC_SEQ // GRID_W
    return jnp.minimum(jnp.clip(r0 - NA_ROWS // 2, 0, rows - NA_ROWS), rows - NA_KROWS)


def _na_block_plan():
    rows = DEC_SEQ // GRID_W
    nblk = rows // NA_QROWS
    plan = []
    for blk in (0, 1, nblk - 1):
        r0 = blk * NA_QROWS
        ks = min(int(np.clip(r0 - NA_ROWS // 2, 0, rows - NA_ROWS)), rows - NA_KROWS)
        per_row = []
        for i in range(NA_QROWS):
            r = r0 + i
            rs = int(np.clip(r - NA_ROWS // 2, 0, rows - NA_ROWS))
            start = ks - r + NA_ROWS - 1 + NA_KROWS
            ok = [rs <= ks + j < rs + NA_ROWS for j in range(NA_KROWS)]
            per_row.append((start, ok))
        plan.append(per_row)
    return plan


def _attn_c_sample_kernel(q_ref, k_ref, v_ref, ck_ref, cv_ref, toe_ref, o_ref, bias_ref):
    rows = DEC_SEQ // GRID_W
    nblk = rows // NA_QROWS
    w = GRID_W

    @pl.when(jnp.logical_and(pl.program_id(1) == 0, pl.program_id(2) == 0))
    def _():
        neg = jnp.full((w, w), NEG, F32)
        for t, per_row in enumerate(_na_block_plan()):
            for half in range(2):
                for i, (start, ok) in enumerate(per_row):
                    for j in range(0, NA_KROWS, 2):
                        pieces = [toe_ref[0, half, start + jj] if ok[jj] else neg for jj in (j, j + 1)]
                        bias_ref[t, half, i * w:(i + 1) * w, j * w:(j + 2) * w] = jnp.concatenate(pieces, axis=1)

    i = pl.program_id(2)
    r0 = i * NA_QROWS
    k0 = pl.multiple_of(_na_key_start(r0) * GRID_W, GRID_W)
    btype = jnp.where(i == 0, 0, jnp.where(i == nblk - 1, 2, 1))
    nk = NA_KROWS * GRID_W
    kw = k_ref[pl.ds(k0, nk), :].astype(BF16)
    vw = _with_ones(v_ref[pl.ds(k0, nk), :])
    kc = ck_ref[0].astype(BF16)
    vc = _with_ones(cv_ref[0])
    qp = q_ref[...].astype(F32) * (C_HEAD_DIM ** -0.5)
    lo = _lane(qp.shape) < 64
    outs = []
    for half in range(2):
        qm = jnp.where(lo if half == 0 else jnp.logical_not(lo), qp, 0.0).astype(BF16)
        s_win = _dot_nt(qm, kw) + bias_ref[btype, half]
        s_ctx = _dot_nt(qm, kc)
        m = jnp.maximum(s_win.max(axis=-1, keepdims=True), s_ctx.max(axis=-1, keepdims=True))
        e_win = jnp.exp(s_win - m).astype(BF16)
        e_ctx = jnp.exp(s_ctx - m).astype(BF16)
        o = _dot(e_win, vw) + _dot(e_ctx, vc)
        outs.append(o[:, :LANES] / o[:, LANES:])
    o_ref[...] = jnp.where(lo, outs[0], outs[1]).astype(o_ref.dtype)


def _attn_c_sample(q, k, v, ck, cv, toe):
    rows = DEC_SEQ // GRID_W
    nblk = rows // NA_QROWS
    qrows = NA_QROWS * GRID_W
    npair = C_HEADS // 2
    return pl.pallas_call(
        _attn_c_sample_kernel,
        grid=(npair, DEC_BATCH, nblk),
        in_specs=[pl.BlockSpec((qrows, LANES), lambda p, b, i: (b * nblk + i, p)),
                  pl.BlockSpec((DEC_SEQ, LANES), lambda p, b, i: (b, p)),
                  pl.BlockSpec((DEC_SEQ, LANES), lambda p, b, i: (b, p)),
                  pl.BlockSpec((1, PAST_LEN, LANES), lambda p, b, i: (b, 0, p)),
                  pl.BlockSpec((1, PAST_LEN, LANES), lambda p, b, i: (b, 0, p)),
                  pl.BlockSpec((1,) + toe.shape[1:], lambda p, b, i: (p, 0, 0, 0, 0))],
        out_specs=pl.BlockSpec((qrows, LANES), lambda p, b, i: (b * nblk + i, p)),
        out_shape=jax.ShapeDtypeStruct((T_SAMPLE, C_WIDTH), BF16),
        scratch_shapes=[pltpu.VMEM((3, 2, qrows, NA_KROWS * GRID_W), F32)],
        compiler_params=_cparams(("arbitrary", "arbitrary", "arbitrary")),
        name="attn_c_sample",
    )(q, k, v, ck, cv, toe)


def _na_toeplitz(rpb):
    w = GRID_W
    nd_r, nd_c = 2 * NA_ROWS - 1, 2 * NA_COLS - 1
    c = np.arange(w)
    cs = np.clip(c - NA_COLS // 2, 0, w - NA_COLS)
    col_ok = (c[None, :] >= cs[:, None]) & (c[None, :] < cs[:, None] + NA_COLS)
    dcol = c[None, :] - c[:, None] + NA_COLS - 1
    onehot = (np.arange(nd_c)[:, None, None] == dcol[None]).reshape(nd_c, w * w)
    toe = jnp.dot(rpb.reshape(C_HEADS * nd_r, nd_c), jnp.asarray(onehot, F32), precision=lax.Precision.HIGHEST)
    toe = jnp.where(col_ok[None, None], toe.reshape(C_HEADS, nd_r, w, w), NEG)
    toe = jnp.pad(toe, ((0, 0), (NA_KROWS, NA_KROWS), (0, 0), (0, 0)), constant_values=NEG)
    return toe.reshape(C_HEADS // 2, 2, nd_r + 2 * NA_KROWS, w, w)


def _rope_tables():
    half = A_HEAD_DIM // 2
    t = jnp.arange(DEC_SEQ)
    row = (t // GRID_W).astype(F32)
    colp = (t % GRID_W).astype(F32)
    freqs = 1.0 / (ROPE_BASE ** (jnp.arange(0, half, 2, dtype=F32) / half))
    d = np.arange(LANES) % A_HEAD_DIM
    pos = jnp.where(jnp.asarray(d < half)[None, :], row[:, None], colp[:, None])
    ang = pos * freqs[d % (half // 2)][None, :]
    sign = jnp.asarray(np.where((d % half) < half // 2, -1.0, 1.0), F32)[None, :]
    return jnp.cos(ang), jnp.sin(ang) * sign


def _head_avg_matrix():
    idx = np.arange(LANES) // A_HEAD_DIM
    return jnp.asarray((idx[:, None] == idx[None, :]).astype(np.float32) / A_HEAD_DIM, BF16)


def _tri_matrices():
    i = np.arange(MLSTM_CHUNK)
    upper = (i[:, None] <= i[None, :]).astype(np.float32)
    lower = (i[:, None] >= i[None, :]).astype(np.float32)
    return jnp.asarray(np.stack([upper, lower]), BF16)


def _router_weights(wg, bg, we, be):
    w = jnp.zeros((D_MODEL, LANES), F32).at[:, :N_GROUPS].set(wg).at[:, N_GROUPS:N_GROUPS + N_EXPERTS].set(we)
    b = jnp.zeros((1, LANES), F32).at[0, :N_GROUPS].set(bg).at[0, N_GROUPS:N_GROUPS + N_EXPERTS].set(be)
    hi = w.astype(BF16)
    lo = (w - hi.astype(F32)).astype(BF16)
    return jnp.concatenate([hi, lo], axis=1), b


def _moe(xn, rtt, cnt, w1, w3, w2, layer):
    dest, slot_tok, block_expert, n_valid = _moe_plan(rtt, cnt)
    return dest, _experts(block_expert, slot_tok, n_valid, xn, w1, w3, w2, layer)


def kernel(x_prompt, x_sample, cache_attn_k, cache_attn_v, state_mlstm_C, state_mlstm_n, state_mlstm_m,
           cache_na_k, cache_na_v, c, c_ctx, norm_mix, norm_ffn, norm_final, ada_w, ada_b,
           ab_w_in, ab_w_out, ab_q_norm, ab_k_norm, ab_gate_bias, ab_out_norm,
           na_w_in, na_w_out, na_rpb, moe_wg, moe_bg, moe_we, moe_be, moe_w1, moe_w3, moe_w2):
    xp = x_prompt.reshape(T_PROMPT, D_MODEL)
    xs = x_sample.reshape(T_SAMPLE, D_MODEL)
    cond =jnp.zeros((N_COND, D_MODEL), F32).at[0].set(c_ctx).at[1:1 + DEC_BATCH].set(c)
    mod = _modulation(cond, ada_w, ada_b)
    gfin = norm_final.reshape(1, D_MODEL)

    w_in = ab_w_in[0]
    o_aq, o_ak, o_av, o_bq, o_bk, o_bv, o_bo, o_bg = np.cumsum((0,) + (A_WIDTH, A_KV_WIDTH, A_KV_WIDTH,
                                                                       B_WIDTH, B_WIDTH, B_WIDTH, B_WIDTH))
    wb = jnp.concatenate([w_in[:, o_bq:o_bk], w_in[:, o_bv:o_bg]], axis=1).astype(BF16)
    wq = w_in[:, o_aq:o_ak].astype(BF16)
    wkv = w_in[:, o_ak:o_bq].astype(BF16)
    gate_rows = np.array([0, 8, 1, 9])
    wg = w_in[:, o_bg:o_bg + 4 * B_HEADS].reshape(D_MODEL, 4, B_HEADS)
    wgt = jnp.zeros((B_HEADS, 16, D_MODEL), F32).at[:, gate_rows, :].set(wg.transpose(2, 1, 0))
    wgt = jnp.concatenate([w_in[:, o_bk:o_bv].T, wgt.reshape(16 * B_HEADS, D_MODEL)], axis=0).astype(BF16)
    g_mix = norm_mix[0].reshape(1, D_MODEL)
    ob_p, oq_p, okv_p, okt_p, ogt_p = _proj_ab(xp, mod[0], g_mix, wb, wq, wkv, wgt, 0)
    ob_s, oq_s, okv_s, okt_s, ogt_s = _proj_ab(xs, mod[0], g_mix, wb, wq, wkv, wgt, T_PROMPT)

    qn = jnp.tile(ab_q_norm[0], 2).reshape(1, LANES)
    kn = jnp.tile(ab_k_norm[0], 2).reshape(1, LANES)
    bd = _head_avg_matrix()
    cos, sin = _rope_tables()
    a_p, new_k, new_v = _attn_a_prompt(oq_p, okv_p, qn, kn, bd)
    ck = cache_attn_k[:, 0].reshape(DEC_BATCH, PAST_LEN, A_KV_WIDTH)
    cv = cache_attn_v[:, 0].reshape(DEC_BATCH, PAST_LEN, A_KV_WIDTH)
    a_s = _attn_a_sample(oq_s, okv_s, ck, cv, cos, sin, qn, kn, bd)

    gb = ab_gate_bias[0]
    brow = jnp.zeros((B_HEADS, 16, LANES), F32).at[:, gate_rows, :].set(
        jnp.broadcast_to(gb.T[:, :, None], (B_HEADS, 4, LANES)))
    onorm = ab_out_norm[0].reshape(B_HEADS, 1, B_HEAD_DIM)
    tri = _tri_matrices()
    b_p, cT, nT, mT = _mlstm(ob_p, okt_p, ogt_p, brow, onorm, tri, n=SEQ, nseq=BATCH, emit_state=True)
    n0 = jnp.broadcast_to(state_mlstm_n[:, 0][..., None], state_mlstm_C[:, 0].shape)
    c0 = jnp.concatenate([state_mlstm_C[:, 0], n0], axis=-1)
    m0 = jnp.broadcast_to(state_mlstm_m[:, 0].transpose(0, 2, 1)[..., None], (DEC_BATCH, B_HEADS, 2, LANES))
    (b_s,) = _mlstm(ob_s, okt_s, ogt_s, brow, onorm, tri, n=DEC_SEQ, nseq=DEC_BATCH, init=(c0, m0))

    w_out = ab_w_out[0].astype(BF16)
    wr, br = _router_weights(moe_wg[0], moe_bg[0], moe_we[0], moe_be[0])
    x1, xn, rt, rtt, cnt = _post([(xp, xs), (a_p, a_s), (b_p, b_s)], [w_out[:A_WIDTH], w_out[A_WIDTH:]], mod[0],
                                 norm_ffn[0].reshape(1, D_MODEL), wr, br)
    dest, yb = _moe(xn, rtt, cnt, moe_w1, moe_w3, moe_w2, 0)

    g_mix = norm_mix[1].reshape(1, D_MODEL)
    w_in = na_w_in[0].astype(BF16)
    x_p, q_p, k_p, v_p = _combine_proj(dest, yb, x1, rt, mod[0], mod[1], g_mix, w_in, 0, T_PROMPT)
    x_s, q_s, k_s, v_s = _combine_proj(dest, yb, x1, rt, mod[0], mod[1], g_mix, w_in, T_PROMPT, T_SAMPLE)
    o_p = _attn_c_prompt(q_p, k_p, v_p)
    nck = cache_na_k[:, 0].reshape(DEC_BATCH, PAST_LEN, C_WIDTH)
    ncv = cache_na_v[:, 0].reshape(DEC_BATCH, PAST_LEN, C_WIDTH)
    o_s = _attn_c_sample(q_s, k_s, v_s, nck, ncv, _na_toeplitz(na_rpb[0]))
    wr, br = _router_weights(moe_wg[1], moe_bg[1], moe_we[1], moe_be[1])
    x1, xn, rt, rtt, cnt = _post([(x_p, x_s), (o_p, o_s)], [na_w_out[0].astype(BF16)], mod[1],
                                 norm_ffn[1].reshape(1, D_MODEL), wr, br)
    dest, yb = _moe(xn, rtt, cnt, moe_w1, moe_w3, moe_w2, 1)
    y_prompt = _combine(dest, yb, x1, rt, mod[1], gfin, True, 0, T_PROMPT).reshape(BATCH, SEQ, D_MODEL)
    y_sample = _combine(dest, yb, x1, rt, mod[1], gfin, True, T_PROMPT, T_SAMPLE).reshape(DEC_BATCH, DEC_SEQ, D_MODEL)
    new_attn_k = new_k.reshape(BATCH, 1, SEQ, A_KV_HEADS, A_HEAD_DIM)
    new_attn_v = new_v.reshape(BATCH, 1, SEQ, A_KV_HEADS, A_HEAD_DIM)
    new_mlstm_C = cT[:, None]
    new_mlstm_n = nT.transpose(0, 2, 1, 3)[:, None]
    new_mlstm_m = mT[..., 0].transpose(0, 2, 1)[:, None]
    new_na_k = k_p.astype(F32).reshape(BATCH, 1, SEQ, C_HEADS, C_HEAD_DIM)
    new_na_v = v_p.astype(F32).reshape(BATCH, 1, SEQ, C_HEADS, C_HEAD_DIM)
    return (y_prompt, y_sample, new_attn_k, new_attn_v, new_mlstm_C, new_mlstm_n, new_mlstm_m,
            new_na_k, new_na_v)
```

```python
import functools

import numpy as np
import jax
import jax.numpy as jnp
from jax import lax
from jax.experimental import pallas as pl
from jax.experimental.pallas import tpu as pltpu

F32 = jnp.float32
BF16 = jnp.bfloat16

D_MODEL = 1024
BATCH = 32
SEQ = 256
DEC_BATCH = 4
DEC_SEQ = 2048
PAST_LEN = 256
GRID_W = 64
A_HEADS = 8
A_KV_HEADS = 2
A_HEAD_DIM = 64
A_WIDTH = A_HEADS * A_HEAD_DIM
A_KV_WIDTH = A_KV_HEADS * A_HEAD_DIM
B_HEADS = 4
B_HEAD_DIM = 128
B_WIDTH = B_HEADS * B_HEAD_DIM
MLSTM_CHUNK = 128
C_HEADS = 16
C_HEAD_DIM = 64
C_WIDTH = C_HEADS * C_HEAD_DIM
NA_ROWS = 8
NA_COLS = 16
N_GROUPS = 4
EXPERTS_PER_GROUP = 8
N_EXPERTS = N_GROUPS * EXPERTS_PER_GROUP
D_EXPERT = 512
MOE_BLOCK = 128
ROPE_BASE = 10000.0
NORM_EPS = 1e-6

T_PROMPT = BATCH * SEQ
T_SAMPLE = DEC_BATCH * DEC_SEQ
T_ALL = T_PROMPT + T_SAMPLE
N_COND = 8
LANES = 128
SLABS = D_MODEL // LANES
TOK_BLOCK = 256
POST_BLOCK = 2 * TOK_BLOCK
MLSTM_HEADS_PER_STEP = 2
EXPERT_ROWS = 512
NA_QROWS = 4
NA_KROWS = 12
NEG = -1e30
VMEM_LIMIT = 56 * 1024 * 1024


def _cparams(sem):
    return pltpu.CompilerParams(dimension_semantics=sem, vmem_limit_bytes=VMEM_LIMIT)


def _split2(x):
    hi = x.astype(BF16)
    lo = (x - hi.astype(F32)).astype(BF16)
    return hi, lo


def _split3(x):
    hi = x.astype(BF16)
    r = x - hi.astype(F32)
    mid = r.astype(BF16)
    lo = (r - mid.astype(F32)).astype(BF16)
    return hi, mid, lo


def _dot(a, b):
    return jnp.dot(a, b, preferred_element_type=F32)


def _dot_nt(a, b):
    return lax.dot_general(a, b, (((1,), (1,)), ((), ())), preferred_element_type=F32)


def _dot_exact_rhs(x, b):
    hi, mid, lo = _split3(x)
    return _dot(hi, b) + _dot(mid, b) + _dot(lo, b)


def _rms(x, g):
    ms = jnp.mean(x * x, axis=-1, keepdims=True)
    return (x * lax.rsqrt(ms + NORM_EPS)) * g


def _mod_row(tok0):
    return jnp.where(tok0 < T_PROMPT, 0, 1 + (tok0 - T_PROMPT) // DEC_SEQ)


def _mod_part(mod_ref, r, idx):
    return mod_ref[pl.ds(r, 1), idx * D_MODEL:(idx + 1) * D_MODEL]


def _head_rms(x, w, bd):
    hi, lo = _split2(x * x)
    ms = _dot(hi, bd) + _dot(lo, bd)
    return (x * lax.rsqrt(ms + NORM_EPS)) * w


def _to_token_tiles(ref, x, tile0=0):
    m = x.shape[0]
    for s in range(SLABS):
        ref[pl.ds(tile0 * SLABS + s, m, stride=SLABS), :] = x[:, s * LANES:(s + 1) * LANES]


def _from_token_tiles(ref, tile0, m):
    return jnp.concatenate([ref[pl.ds(tile0 * SLABS + s, m, stride=SLABS), :] for s in range(SLABS)], axis=1)


def _lane(shape):
    return lax.broadcasted_iota(jnp.int32, shape, len(shape) - 1)


def _dup_half(x, g):
    xr = pltpu.roll(x, 64, 1)
    lo = _lane(x.shape) < 64
    return jnp.where(lo, x, xr) if g == 0 else jnp.where(lo, xr, x)


def _rope(x, cos, sin_signed):
    lane = _lane(x.shape)
    partner = jnp.where((lane % 32) < 16, pltpu.roll(x, LANES - 16, 1), pltpu.roll(x, 16, 1))
    return x * cos + partner * sin_signed


def _with_ones(v):
    return jnp.concatenate([v.astype(BF16), jnp.ones(v.shape, BF16)], axis=1)


def _softmax_pair(qp, score_fn, value_fn):
    lo = _lane(qp.shape) < 64
    outs = []
    for half in range(2):
        qm = jnp.where(lo if half == 0 else jnp.logical_not(lo), qp, 0.0).astype(BF16)
        ss = score_fn(qm)
        m = ss[0].max(axis=-1, keepdims=True)
        for s in ss[1:]:
            m = jnp.maximum(m, s.max(axis=-1, keepdims=True))
        o = value_fn([jnp.exp(s - m).astype(BF16) for s in ss])
        outs.append(o[:, :LANES] / o[:, LANES:])
    return jnp.where(lo, outs[0], outs[1])


def _mod_kernel(cond_ref, w_ref, b_ref, o_ref):
    c = cond_ref[...]
    s = c * jax.nn.sigmoid(c)
    s_hi, s_lo = _split2(s)
    w_hi, w_lo = _split2(w_ref[0])
    o_ref[0] = _dot(s_hi, w_hi) + _dot(s_lo, w_hi) + _dot(s_hi, w_lo) + b_ref[0]


def _modulation(cond, ada_w, ada_b):
    depth, d, n = ada_w.shape
    tn = 1536
    return pl.pallas_call(
        _mod_kernel,
        grid=(depth, n // tn),
        in_specs=[pl.BlockSpec((N_COND, d), lambda l, j: (0, 0)),
                  pl.BlockSpec((1, d, tn), lambda l, j: (l, 0, j)),
                  pl.BlockSpec((1, 1, tn), lambda l, j: (l, 0, j))],
        out_specs=pl.BlockSpec((1, N_COND, tn), lambda l, j: (l, 0, j)),
        out_shape=jax.ShapeDtypeStruct((depth, N_COND, n), F32),
        compiler_params=_cparams(("arbitrary", "arbitrary")),
        name="adaln_modulation",
    )(cond, ada_w, ada_b.reshape(depth, 1, n))


def _norm_mod(x_ref, mod_ref, g_ref, shift_idx, scale_idx, row0):
    r = _mod_row(row0 + pl.program_id(0) * x_ref.shape[0])
    h = _rms(x_ref[...], g_ref[...])
    return h * (1.0 + _mod_part(mod_ref, r, scale_idx)) + _mod_part(mod_ref, r, shift_idx)


def _proj_ab_kernel(x_ref, mod_ref, g_ref, wb_ref, wq_ref, wkv_ref, wgt_ref,
                    ob_ref, oq_ref, okv_ref, okt_ref, ogt_ref, *, row0):
    tm = x_ref.shape[0]
    hb = _norm_mod(x_ref, mod_ref, g_ref, 0, 1, row0).astype(BF16)
    ob_ref[...] = _dot(hb, wb_ref[...]).astype(ob_ref.dtype)
    oq_ref[...] = _dot(hb, wq_ref[...]).astype(oq_ref.dtype)
    okv_ref[...] = _dot(hb, wkv_ref[...]).astype(okv_ref.dtype)
    gt = _dot_nt(wgt_ref[...], hb)
    for j in range(tm // LANES):
        okt_ref[j] = gt[:B_WIDTH, j * LANES:(j + 1) * LANES].astype(okt_ref.dtype)
        ogt_ref[j] = gt[B_WIDTH:, j * LANES:(j + 1) * LANES]


def _proj_ab(x, mod, g, wb, wq, wkv, wgt, row0):
    t = x.shape[0]
    tm = TOK_BLOCK
    full = lambda a: pl.BlockSpec(a.shape, lambda i: (0,) * a.ndim)
    return pl.pallas_call(
        functools.partial(_proj_ab_kernel, row0=row0),
        grid=(t // tm,),
        in_specs=[pl.BlockSpec((tm, D_MODEL), lambda i: (i, 0)), full(mod), full(g),
                  full(wb), full(wq), full(wkv), full(wgt)],
        out_specs=[pl.BlockSpec((tm, wb.shape[1]), lambda i: (i, 0)),
                   pl.BlockSpec((tm, A_WIDTH), lambda i: (i, 0)),
                   pl.BlockSpec((tm, 2 * A_KV_WIDTH), lambda i: (i, 0)),
                   pl.BlockSpec((tm // LANES, B_WIDTH, LANES), lambda i: (i, 0, 0)),
                   pl.BlockSpec((tm // LANES, wgt.shape[0] - B_WIDTH, LANES), lambda i: (i, 0, 0))],
        out_shape=[jax.ShapeDtypeStruct((t, wb.shape[1]), BF16),
                   jax.ShapeDtypeStruct((t, A_WIDTH), BF16),
                   jax.ShapeDtypeStruct((t, 2 * A_KV_WIDTH), BF16),
                   jax.ShapeDtypeStruct((t // LANES, B_WIDTH, LANES), BF16),
                   jax.ShapeDtypeStruct((t // LANES, wgt.shape[0] - B_WIDTH, LANES), F32)],
        compiler_params=_cparams(("arbitrary",)),
        name="proj_ab",
    )(x, mod, g, wb, wq, wkv, wgt)


def _gqa_block(q_ref, qn, bd, kd_ref, vd_ref, o_ref, rope=None):
    for p in range(A_HEADS // 2):
        g = p // (A_HEADS // 2 // A_KV_HEADS)
        qp = _head_rms(q_ref[:, p * LANES:(p + 1) * LANES].astype(F32), qn, bd)
        if rope is not None:
            qp = _rope(qp, rope[0], rope[1])
        qp = qp * (A_HEAD_DIM ** -0.5)
        o_ref[:, p * LANES:(p + 1) * LANES] = _softmax_pair(
            qp, lambda qm: [_dot_nt(qm, kd_ref[g])], lambda ps: _dot(ps[0], vd_ref[g])).astype(o_ref.dtype)


def _attn_a_prompt_kernel(q_ref, kv_ref, qn_ref, kn_ref, bd_ref, o_ref, knew_ref, vnew_ref, kd_ref, vd_ref):
    bd = bd_ref[...]
    k = _head_rms(kv_ref[:, :LANES].astype(F32), kn_ref[...], bd)
    v = kv_ref[:, LANES:].astype(F32)
    knew_ref[...] = k
    vnew_ref[...] = v
    for g in range(A_KV_HEADS):
        kd_ref[g] = _dup_half(k, g).astype(BF16)
        vd_ref[g] = _with_ones(_dup_half(v, g))
    _gqa_block(q_ref, qn_ref[...], bd, kd_ref, vd_ref, o_ref)


def _attn_a_prompt(q, kv, qn, kn, bd):
    nb = BATCH
    full = lambda a: pl.BlockSpec(a.shape, lambda b: (0,) * a.ndim)
    return pl.pallas_call(
        _attn_a_prompt_kernel,
        grid=(nb,),
        in_specs=[pl.BlockSpec((SEQ, A_WIDTH), lambda b: (b, 0)),
                  pl.BlockSpec((SEQ, 2 * A_KV_WIDTH), lambda b: (b, 0)),
                  full(qn), full(kn), full(bd)],
        out_specs=[pl.BlockSpec((SEQ, A_WIDTH), lambda b: (b, 0)),
                   pl.BlockSpec((SEQ, A_KV_WIDTH), lambda b: (b, 0)),
                   pl.BlockSpec((SEQ, A_KV_WIDTH), lambda b: (b, 0))],
        out_shape=[jax.ShapeDtypeStruct((T_PROMPT, A_WIDTH), BF16),
                   jax.ShapeDtypeStruct((T_PROMPT, A_KV_WIDTH), F32),
                   jax.ShapeDtypeStruct((T_PROMPT, A_KV_WIDTH), F32)],
        scratch_shapes=[pltpu.VMEM((A_KV_HEADS, SEQ, LANES), BF16),
                        pltpu.VMEM((A_KV_HEADS, SEQ, 2 * LANES), BF16)],
        compiler_params=_cparams(("arbitrary",)),
        name="attn_a_prompt",
    )(q, kv, qn, kn, bd)


_A_QBLOCK = 256


def _attn_a_sample_kernel(q_ref, kv_ref, ck_ref, cv_ref, cos_ref, sin_ref, cosq_ref, sinq_ref,
                          qn_ref, kn_ref, bd_ref, o_ref, kd_ref, vd_ref):
    bd = bd_ref[...]

    @pl.when(pl.program_id(1) == 0)
    def _():
        for g in range(A_KV_HEADS):
            kd_ref[g, :PAST_LEN] = _dup_half(ck_ref[0], g).astype(BF16)
            vd_ref[g, :PAST_LEN] = _with_ones(_dup_half(cv_ref[0], g))
        rows = 256
        for c in range(DEC_SEQ // rows):
            sl = slice(c * rows, (c + 1) * rows)
            k = _head_rms(kv_ref[sl, :LANES].astype(F32), kn_ref[...], bd)
            k = _rope(k, cos_ref[sl, :], sin_ref[sl, :])
            v = kv_ref[sl, LANES:].astype(F32)
            dst = slice(PAST_LEN + c * rows, PAST_LEN + (c + 1) * rows)
            for g in range(A_KV_HEADS):
                kd_ref[g, dst] = _dup_half(k, g).astype(BF16)
                vd_ref[g, dst] = _with_ones(_dup_half(v, g))

    _gqa_block(q_ref, qn_ref[...], bd, kd_ref, vd_ref, o_ref, rope=(cosq_ref[...], sinq_ref[...]))


def _attn_a_sample(q, kv, ck, cv, cos, sin, qn, kn, bd):
    nq = DEC_SEQ // _A_QBLOCK
    full = lambda a: pl.BlockSpec(a.shape, lambda b, i: (0,) * a.ndim)
    tk = PAST_LEN + DEC_SEQ
    return pl.pallas_call(
        _attn_a_sample_kernel,
        grid=(DEC_BATCH, nq),
        in_specs=[pl.BlockSpec((_A_QBLOCK, A_WIDTH), lambda b, i: (b * nq + i, 0)),
                  pl.BlockSpec((DEC_SEQ, 2 * A_KV_WIDTH), lambda b, i: (b, 0)),
                  pl.BlockSpec((1, PAST_LEN, A_KV_WIDTH), lambda b, i: (b, 0, 0)),
                  pl.BlockSpec((1, PAST_LEN, A_KV_WIDTH), lambda b, i: (b, 0, 0)),
                  full(cos), full(sin),
                  pl.BlockSpec((_A_QBLOCK, LANES), lambda b, i: (i, 0)),
                  pl.BlockSpec((_A_QBLOCK, LANES), lambda b, i: (i, 0)),
                  full(qn), full(kn), full(bd)],
        out_specs=pl.BlockSpec((_A_QBLOCK, A_WIDTH), lambda b, i: (b * nq + i, 0)),
        out_shape=jax.ShapeDtypeStruct((T_SAMPLE, A_WIDTH), BF16),
        scratch_shapes=[pltpu.VMEM((A_KV_HEADS, tk, LANES), BF16),
                        pltpu.VMEM((A_KV_HEADS, tk, 2 * LANES), BF16)],
        compiler_params=_cparams(("arbitrary", "arbitrary")),
        name="attn_a_sample",
    )(q, kv, ck, cv, cos, sin, cos, sin, qn, kn, bd)


def _log_sigmoid(x):
    return -(jnp.maximum(-x, 0.0) + jnp.log1p(jnp.exp(-jnp.abs(x))))


def _mlstm_kernel(*refs, n, has_init, emit_state):
    it = iter(refs)
    q_ref, v_ref, og_ref, kt_ref, gt_ref, brow_ref, onorm_ref, tri_ref = [next(it) for _ in range(8)]
    if has_init:
        c0_ref, m0_ref = [next(it) for _ in range(2)]
    out_ref = next(it)
    if emit_state:
        cT_ref, nT_ref, mT_ref = [next(it) for _ in range(3)]
    h_ref, row_ref, c_ref = [next(it) for _ in range(3)]

    L = MLSTM_CHUNK
    D = B_HEAD_DIM
    nc = n // L
    hb = MLSTM_HEADS_PER_STEP

    lane = lax.broadcasted_iota(jnp.int32, (nc * 8, L), 1)
    is_fwd = lax.broadcasted_iota(jnp.int32, (nc * 8, L), 0) % 8 == 0

    def running_max(x, suffix):
        for sh in (1, 2, 4, 8, 16, 32, 64):
            if suffix:
                x = jnp.where(lane < L - sh, jnp.maximum(x, pltpu.roll(x, L - sh, 1)), x)
            else:
                x = jnp.where(lane >= sh, jnp.maximum(x, pltpu.roll(x, sh, 1)), x)
        return x

    for hh in range(hb):
        gt = gt_ref[:, 16 * hh:16 * hh + 16, :] + brow_ref[hh][None]
        li = gt[:, 0:8, :].reshape(nc * 8, L)
        lf = _log_sigmoid(gt[:, 8:16, :]).reshape(nc * 8, L)
        cum = jnp.where(is_fwd, _dot_exact_rhs(lf, tri_ref[0]), _dot_exact_rhs(lf, tri_ref[1]))
        a = li - cum
        planes = (cum, a, jnp.broadcast_to(lf.sum(axis=-1, keepdims=True), (nc * 8, L)),
                  jnp.where(is_fwd, running_max(a, False), running_max(a, True)),
                  jnp.broadcast_to(a.max(axis=-1, keepdims=True), (nc * 8, L)))
        for p, val in enumerate(planes):
            row_ref[hh, p] = val.reshape(nc, 8, L)

    chains = [(hh, d) for hh in range(hb) for d in range(2)]
    if has_init:
        m_init = []
        for hh, d in chains:
            c_ref[2 * hh + d] = c0_ref[0, d, hh]
            m_init.append(m0_ref[0, hh, d:d + 1, :])
        m_init = tuple(m_init)
    else:
        c_ref[...] = jnp.zeros(c_ref.shape, F32)
        m_init = tuple(jnp.zeros((1, L), F32) for _ in chains)

    t_idx = lax.broadcasted_iota(jnp.int32, (L, L), 0)
    s_idx = lax.broadcasted_iota(jnp.int32, (L, L), 1)
    masks = (s_idx <= t_idx, s_idx >= t_idx)
    ones = jnp.ones((L, L), F32)

    def step(c, hh, d, m):
        r0 = pl.multiple_of(c * L, L)
        hl = slice(hh * L, (hh + 1) * L)
        qb = q_ref[pl.ds(r0, L), hl].astype(BF16)
        kst = kt_ref[c, hl, :].astype(F32) * (D ** -0.5)
        v_ext = jnp.concatenate([v_ref[pl.ds(r0, L), hl], ones], axis=1).astype(BF16)
        cum, a_row, tot, amax_run, amax = [row_ref[hh, p, c][d:d + 1, :] for p in range(5)]
        m_cum = jnp.broadcast_to(cum, (L, L)).T
        m_run = jnp.broadcast_to(amax_run, (L, L)).T
        dlog = jnp.where(masks[d], m_cum + a_row, -jnp.inf)
        inter = m_cum + m
        m_t = jnp.maximum(inter, m_cum + m_run)
        w_in = jnp.exp(dlog - m_t)
        w_st = jnp.exp(inter - m_t)
        a = _dot(qb, kst.astype(BF16)) * w_in
        ci = 2 * hh + d
        cext = c_ref[ci]
        p_state = _dot(qb, cext.astype(BF16))
        p_intra = _dot(a.astype(BF16), v_ext)
        num = w_st * p_state[:, :D] + p_intra[:, :D]
        den = w_st * p_state[:, D:] + p_intra[:, D:]
        h_ref[ci, pl.ds(r0, L), :] = num / jnp.maximum(jnp.abs(den), jnp.exp(-m_t))
        m_new = jnp.maximum(tot + m, amax + tot)
        ws = jnp.exp(a_row + tot - m_new)
        wc = jnp.exp(tot + m - m_new)
        c_ref[ci] = jnp.concatenate([wc, wc], axis=1) * cext + _dot((kst * ws).astype(BF16), v_ext)
        return m_new

    def body(i, carry):
        return tuple(step(i if d == 0 else nc - 1 - i, hh, d, m) for (hh, d), m in zip(chains, carry))

    m_fin = lax.fori_loop(0, nc, body, m_init)

    for hh in range(hb):
        hm = h_ref[2 * hh] + h_ref[2 * hh + 1]
        hl = slice(hh * L, (hh + 1) * L)
        gate = jax.nn.sigmoid(og_ref[:, hl].astype(F32))
        out_ref[:, hl] = (_rms(hm, onorm_ref[hh]) * gate).astype(out_ref.dtype)

    if emit_state:
        for k, (hh, d) in enumerate(chains):
            cext = c_ref[2 * hh + d]
            cT_ref[0, d, hh] = cext[:, :D]
            nT_ref[0, hh, d:d + 1, :] = cext[:, D:].T[0:1, :]
            mT_ref[0, hh, d:d + 1, :] = m_fin[k]


def _mlstm(ob, okt, ogt, brow, onorm, tri, *, n, nseq, init=None, emit_state=False):
    L = MLSTM_CHUNK
    nc = n // L
    H = B_HEADS
    hb = MLSTM_HEADS_PER_STEP
    ng = H // hb
    col = lambda part: (lambda b, g: (b, part * ng + g))
    in_specs = [pl.BlockSpec((n, hb * L), col(0)), pl.BlockSpec((n, hb * L), col(1)),
                pl.BlockSpec((n, hb * L), col(2)),
                pl.BlockSpec((nc, hb * L, L), lambda b, g: (b, g, 0)),
                pl.BlockSpec((nc, 16 * hb, L), lambda b, g: (b, g, 0)),
                pl.BlockSpec((hb, 16, L), lambda b, g: (g, 0, 0)),
                pl.BlockSpec((hb, 1, L), lambda b, g: (g, 0, 0)),
                pl.BlockSpec(tri.shape, lambda b, g: (0, 0, 0))]
    args = [ob, ob, ob, okt, ogt, brow, onorm, tri]
    if init is not None:
        in_specs += [pl.BlockSpec((1, 2, hb, L, 2 * L), lambda b, g: (b, 0, g, 0, 0)),
                     pl.BlockSpec((1, hb, 2, L), lambda b, g: (b, g, 0, 0))]
        args += list(init)
    out_specs = [pl.BlockSpec((n, hb * L), lambda b, g: (b, g))]
    out_shape = [jax.ShapeDtypeStruct((nseq * n, B_WIDTH), BF16)]
    if emit_state:
        out_specs += [pl.BlockSpec((1, 2, hb, L, L), lambda b, g: (b, 0, g, 0, 0)),
                      pl.BlockSpec((1, hb, 2, L), lambda b, g: (b, g, 0, 0)),
                      pl.BlockSpec((1, hb, 2, L), lambda b, g: (b, g, 0, 0))]
        out_shape += [jax.ShapeDtypeStruct((nseq, 2, H, L, L), F32),
                      jax.ShapeDtypeStruct((nseq, H, 2, L), F32),
                      jax.ShapeDtypeStruct((nseq, H, 2, L), F32)]
    return pl.pallas_call(
        functools.partial(_mlstm_kernel, n=n, has_init=init is not None, emit_state=emit_state),
        grid=(nseq, ng),
        in_specs=in_specs,
        out_specs=out_specs,
        out_shape=out_shape,
        scratch_shapes=[pltpu.VMEM((2 * hb, n, L), F32),
                        pltpu.VMEM((hb, 5, nc, 8, L), F32),
                        pltpu.VMEM((2 * hb, L, 2 * L), F32)],
        compiler_params=_cparams(("arbitrary", "arbitrary")),
        name="mlstm_init" if init is not None else "mlstm",
    )(*args)


def _router(logits):
    lane = _lane(logits.shape).astype(F32)
    big = 1e9
    gl = jnp.where(lane < N_GROUPS, logits, -jnp.inf)
    gmax = gl.max(axis=-1, keepdims=True)
    g_sel = jnp.where(gl == gmax, lane, big).min(axis=-1, keepdims=True)
    g_prob = 1.0 / jnp.exp(gl - gmax).sum(axis=-1, keepdims=True)
    lo = N_GROUPS + EXPERTS_PER_GROUP * g_sel
    el = jnp.where(lane >= lo, jnp.where(lane < lo + EXPERTS_PER_GROUP, logits, -jnp.inf), -jnp.inf)
    v1 = el.max(axis=-1, keepdims=True)
    i1 = jnp.where(el == v1, lane, big).min(axis=-1, keepdims=True)
    el2 = jnp.where(lane == i1, -jnp.inf, el)
    v2 = el2.max(axis=-1, keepdims=True)
    i2 = jnp.where(el2 == v2, lane, big).min(axis=-1, keepdims=True)
    e2 = jnp.exp(v2 - v1)
    w1 = g_prob / (1.0 + e2)
    w2 = g_prob * e2 / (1.0 + e2)
    return i1, i2, w1, w2


def _read_tokens(refs, is_prompt, rows):
    if len(refs) == 1:
        return refs[0][rows, :]
    return jnp.where(is_prompt, refs[0][rows, :], refs[1][rows, :])


def _post_kernel(*refs, groups):
    it = iter(refs)
    tok_refs = [[next(it) for _ in range(n)] for n in groups]
    w_refs = [next(it) for _ in range(len(groups) - 1)]
    mod_ref, g_ref, wr_ref, br_ref, ls_ref, sel_ref = [next(it) for _ in range(6)]
    xnew_ref, xn_ref, rt_ref, rtt_ref, cnt_ref = [next(it) for _ in range(5)]
    run_ref = next(it)
    i = pl.program_id(0)
    tm = xnew_ref.shape[0]
    is_prompt = i * tm < T_PROMPT
    r = _mod_row(i * tm)

    @pl.when(i == 0)
    def _():
        run_ref[...] = jnp.zeros(run_ref.shape, F32)

    sub = ls_ref.shape[0]
    for h in range(tm // sub):
        rows = slice(h * sub, (h + 1) * sub)
        acc = None
        for a_refs, w_ref in zip(tok_refs[1:], w_refs):
            d = _dot(_read_tokens(a_refs, is_prompt, rows).astype(BF16), w_ref[...])
            acc = d if acc is None else acc + d
        xnew = _read_tokens(tok_refs[0], is_prompt, rows) + _mod_part(mod_ref, r, 2) * acc
        xnew_ref[rows, :] = xnew
        xn = _rms(xnew, g_ref[...]) * (1.0 + _mod_part(mod_ref, r, 4)) + _mod_part(mod_ref, r, 3)
        _to_token_tiles(xn_ref, xn, h * sub)
        x_hi, x_lo = _split2(xn)
        both = _dot(x_hi, wr_ref[...])
        logits = both[:, :LANES] + both[:, LANES:] + _dot(x_lo, wr_ref[:, :LANES]) + br_ref[...]
        i1, i2, w1, w2 = _router(logits)

        lane = _lane(logits.shape).astype(F32)
        member = jnp.where(lane == i1, 1.0, jnp.where(lane == i2, 1.0, 0.0))
        before = _dot(ls_ref[...], member.astype(BF16)) + run_ref[...]
        rank1 = jnp.where(lane == i1, before, 0.0).sum(axis=-1, keepdims=True)
        rank2 = jnp.where(lane == i2, before, 0.0).sum(axis=-1, keepdims=True)
        run_ref[...] = run_ref[...] + member.sum(axis=0, keepdims=True)
        cols = (i1 - N_GROUPS, i2 - N_GROUPS, w1, w2, rank1, rank2)
        rt = jnp.zeros(logits.shape, F32)
        for k, c in enumerate(cols):
            rt = jnp.where(lane == k, c, rt)
        rt_ref[rows, :] = rt
        hi, mid, lo = _split3(rt)
        sel = sel_ref[...]
        rtt_ref[:, rows] = _dot_nt(sel, hi) + _dot_nt(sel, mid) + _dot_nt(sel, lo)
    cnt_ref[...] = run_ref[...]


def _post(tok_ops, w_list, mod, g, wr, br):
    t = T_ALL
    tm = POST_BLOCK
    npb = T_PROMPT // tm
    full = lambda a: pl.BlockSpec(a.shape, lambda i: (0,) * a.ndim)
    specs, args, groups = [], [], []
    for op in tok_ops:
        if isinstance(op, tuple):
            w = op[0].shape[1]
            specs += [pl.BlockSpec((tm, w), lambda i: (jnp.minimum(i, npb - 1), 0)),
                      pl.BlockSpec((tm, w), lambda i: (jnp.maximum(i - npb, 0), 0))]
            args += list(op)
            groups.append(2)
        else:
            specs.append(pl.BlockSpec((tm, op.shape[1]), lambda i: (i, 0)))
            args.append(op)
            groups.append(1)
    idx = np.arange(TOK_BLOCK)
    ls = jnp.asarray(idx[:, None] > idx[None, :], BF16)
    sel = jnp.asarray(np.arange(8)[:, None] == np.arange(LANES)[None, :], BF16)
    consts = [mod, g, wr, br, ls, sel]
    return pl.pallas_call(
        functools.partial(_post_kernel, groups=tuple(groups)),
        grid=(t // tm,),
        in_specs=specs + [full(w) for w in w_list] + [full(a) for a in consts],
        out_specs=[pl.BlockSpec((tm, D_MODEL), lambda i: (i, 0)),
                   pl.BlockSpec((tm * SLABS, LANES), lambda i: (i, 0)),
                   pl.BlockSpec((tm, LANES), lambda i: (i, 0)),
                   pl.BlockSpec((8, tm), lambda i: (0, i)),
                   pl.BlockSpec((1, LANES), lambda i: (0, 0))],
        out_shape=[jax.ShapeDtypeStruct((t, D_MODEL), F32),
                   jax.ShapeDtypeStruct((t * SLABS, LANES), F32),
                   jax.ShapeDtypeStruct((t, LANES), F32),
                   jax.ShapeDtypeStruct((8, t), F32),
                   jax.ShapeDtypeStruct((1, LANES), F32)],
        scratch_shapes=[pltpu.VMEM((1, LANES), F32)],
        compiler_params=_cparams(("arbitrary",)),
        name="post_mixer_router",
    )(*args, *w_list, *consts)


def _expert_kernel(be_ref, st_ref, nv_ref, xn_hbm, w1_ref, w3_ref, w2_ref, o_ref, xa, xb, sem, w1b, w3b, w2b):
    i = pl.program_id(0)
    nv = nv_ref[0]
    active = i < nv
    rows = EXPERT_ROWS

    def gather(blk, buf, s):
        for r in range(rows):
            tok = st_ref[blk * rows + r]
            pltpu.make_async_copy(xn_hbm.at[pl.ds(pl.multiple_of(tok * SLABS, SLABS), SLABS), :],
                                  buf.at[pl.ds(r * SLABS, SLABS), :], sem.at[s]).start(priority=r % 2)

    def wait(buf, s):
        pltpu.make_async_copy(xn_hbm.at[pl.ds(0, rows * SLABS), :], buf, sem.at[s]).wait()

    @pl.when(i == 0)
    def _():
        gather(0, xa, 0)

    changed = jnp.logical_or(i == 0, be_ref[i] != be_ref[jnp.maximum(i - 1, 0)])

    @pl.when(jnp.logical_and(changed, active))
    def _():
        w1b[...] = w1_ref[0, 0].astype(BF16)
        w3b[...] = w3_ref[0, 0].astype(BF16)
        w2b[...] = w2_ref[0, 0].astype(BF16)

    nxt = jnp.minimum(i + 1, nv - 1)

    def step(cur, s_cur, oth, s_oth):
        wait(cur, s_cur)
        gather(nxt, oth, s_oth)
        x = _from_token_tiles(cur, 0, rows).astype(BF16)
        h1 = _dot(x, w1b[...])
        h3 = _dot(x, w3b[...])
        hid = (h1 * jax.nn.sigmoid(h1)) * h3
        _to_token_tiles(o_ref, _dot(hid.astype(BF16), w2b[...]))

    @pl.when(jnp.logical_and(active, i % 2 == 0))
    def _():
        step(xa, 0, xb, 1)

    @pl.when(jnp.logical_and(active, i % 2 == 1))
    def _():
        step(xb, 1, xa, 0)

    @pl.when(jnp.logical_not(active))
    def _():
        o_ref[...] = jnp.zeros(o_ref.shape, F32)

    @pl.when(i == nv - 1)
    def _():
        @pl.when(i % 2 == 0)
        def _():
            wait(xb, 1)

        @pl.when(i % 2 == 1)
        def _():
            wait(xa, 0)


def _experts(block_expert, slot_tok, n_valid, xn, w1, w3, w2, layer):
    nblk = block_expert.shape[0]
    rows = EXPERT_ROWS
    grid_spec = pltpu.PrefetchScalarGridSpec(
        num_scalar_prefetch=3,
        grid=(nblk,),
        in_specs=[pl.BlockSpec(memory_space=pl.ANY),
                  pl.BlockSpec((1, 1, D_MODEL, D_EXPERT), lambda i, be, st, nv: (layer, be[i], 0, 0)),
                  pl.BlockSpec((1, 1, D_MODEL, D_EXPERT), lambda i, be, st, nv: (layer, be[i], 0, 0)),
                  pl.BlockSpec((1, 1, D_EXPERT, D_MODEL), lambda i, be, st, nv: (layer, be[i], 0, 0))],
        out_specs=pl.BlockSpec((rows * SLABS, LANES), lambda i, be, st, nv: (i, 0)),
        scratch_shapes=[pltpu.VMEM((rows * SLABS, LANES), F32),
                        pltpu.VMEM((rows * SLABS, LANES), F32),
                        pltpu.SemaphoreType.DMA((2,)),
                        pltpu.VMEM((D_MODEL, D_EXPERT), BF16),
                        pltpu.VMEM((D_MODEL, D_EXPERT), BF16),
                        pltpu.VMEM((D_EXPERT, D_MODEL), BF16)])
    return pl.pallas_call(
        _expert_kernel,
        grid_spec=grid_spec,
        out_shape=jax.ShapeDtypeStruct((nblk * rows * SLABS, LANES), F32),
        compiler_params=_cparams(("arbitrary",)),
        name="moe_experts",
    )(block_expert, slot_tok, n_valid, xn, w1, w3, w2)


_COMBINE_BLOCK = 128


def _combine_kernel(dest_ref, yb_hbm, x_ref, rt_ref, mod_ref, gfin_ref, o_ref, ybuf, sem, *, final_norm, row0):
    i = pl.program_id(0)
    nblk = pl.num_programs(0)
    slot = i % 2
    tm = _COMBINE_BLOCK

    def issue(blk, s):
        def body(j, carry):
            for c in range(2):
                d = dest_ref[c * T_ALL + row0 + blk * tm + j]
                pltpu.make_async_copy(yb_hbm.at[pl.ds(pl.multiple_of(d * SLABS, SLABS), SLABS), :],
                                      ybuf.at[s, pl.ds(pl.multiple_of((c * tm + j) * SLABS, SLABS), SLABS), :],
                                      sem.at[s]).start(priority=c)
            return carry
        lax.fori_loop(0, tm, body, 0, unroll=4)

    @pl.when(i == 0)
    def _():
        issue(0, 0)

    @pl.when(i + 1 < nblk)
    def _():
        issue(i + 1, 1 - slot)

    pltpu.make_async_copy(yb_hbm.at[pl.ds(0, 2 * tm * SLABS), :], ybuf.at[slot], sem.at[slot]).wait()
    r = _mod_row(row0 + i * tm)
    rt = rt_ref[...]
    yv = ybuf.at[slot]
    y = rt[:, 2:3] * _from_token_tiles(yv, 0, tm) + rt[:, 3:4] * _from_token_tiles(yv, tm, tm)
    out = x_ref[...] + _mod_part(mod_ref, r, 5) * y
    if final_norm:
        out = _rms(out, gfin_ref[...])
    o_ref[...] = out


def _combine(dest, yb, x, rt, mod, gfin, final_norm, row0=0, t=T_ALL):
    tm = _COMBINE_BLOCK
    blk0 = row0 // tm
    grid_spec = pltpu.PrefetchScalarGridSpec(
        num_scalar_prefetch=1,
        grid=(t // tm,),
        in_specs=[pl.BlockSpec(memory_space=pl.ANY),
                  pl.BlockSpec((tm, D_MODEL), lambda i, d: (blk0 + i, 0)),
                  pl.BlockSpec((tm, LANES), lambda i, d: (blk0 + i, 0)),
                  pl.BlockSpec(mod.shape, lambda i, d: (0, 0)),
                  pl.BlockSpec(gfin.shape, lambda i, d: (0, 0))],
        out_specs=pl.BlockSpec((tm, D_MODEL), lambda i, d: (i, 0)),
        scratch_shapes=[pltpu.VMEM((2, 2 * tm * SLABS, LANES), F32),
                        pltpu.SemaphoreType.DMA((2,))])
    return pl.pallas_call(
        functools.partial(_combine_kernel, final_norm=final_norm, row0=row0),
        grid_spec=grid_spec,
        out_shape=jax.ShapeDtypeStruct((t, D_MODEL), F32),
        compiler_params=_cparams(("arbitrary",)),
        name="moe_combine",
    )(dest, yb, x, rt, mod, gfin)


def _combine_proj_kernel(dest_ref, yb_hbm, x_ref, rt_ref, mod0_ref, mod1_ref, g_ref, w_ref,
                         xo_ref, q_ref, k_ref, v_ref, ya, yb, sem, *, row0):
    i = pl.program_id(0)
    nblk = pl.num_programs(0)
    tm = x_ref.shape[0]

    def gather(blk, buf, s):
        for j in range(tm):
            for c in range(2):
                d = dest_ref[c * T_ALL + row0 + blk * tm + j]
                pltpu.make_async_copy(yb_hbm.at[pl.ds(pl.multiple_of(d * SLABS, SLABS), SLABS), :],
                                      buf.at[pl.ds((c * tm + j) * SLABS, SLABS), :], sem.at[s]).start(priority=c)

    def wait(buf, s):
        pltpu.make_async_copy(yb_hbm.at[pl.ds(0, 2 * tm * SLABS), :], buf, sem.at[s]).wait()

    @pl.when(i == 0)
    def _():
        gather(0, ya, 0)

    nxt = jnp.minimum(i + 1, nblk - 1)

    def step(cur, s_cur, oth, s_oth):
        wait(cur, s_cur)
        gather(nxt, oth, s_oth)
        r = _mod_row(row0 + i * tm)
        rt = rt_ref[...]
        y = rt[:, 2:3] * _from_token_tiles(cur, 0, tm) + rt[:, 3:4] * _from_token_tiles(cur, tm, tm)
        x = x_ref[...] + _mod_part(mod0_ref, r, 5) * y
        xo_ref[...] = x
        h = _rms(x, g_ref[...]) * (1.0 + _mod_part(mod1_ref, r, 1)) + _mod_part(mod1_ref, r, 0)
        hb = h.astype(BF16)
        for j, o_ref in enumerate((q_ref, k_ref, v_ref)):
            o_ref[...] = _dot(hb, w_ref[:, j * C_WIDTH:(j + 1) * C_WIDTH]).astype(o_ref.dtype)

    @pl.when(i % 2 == 0)
    def _():
        step(ya, 0, yb, 1)

    @pl.when(i % 2 == 1)
    def _():
        step(yb, 1, ya, 0)

    @pl.when(i == nblk - 1)
    def _():
        @pl.when(i % 2 == 0)
        def _():
            wait(yb, 1)

        @pl.when(i % 2 == 1)
        def _():
            wait(ya, 0)


def _combine_proj(dest, yb, x, rt, mod0, mod1, g, w, row0, t):
    tm = TOK_BLOCK
    blk0 = row0 // tm
    full = lambda a: pl.BlockSpec(a.shape, lambda i, d: (0,) * a.ndim)
    grid_spec = pltpu.PrefetchScalarGridSpec(
        num_scalar_prefetch=1,
        grid=(t // tm,),
        in_specs=[pl.BlockSpec(memory_space=pl.ANY),
                  pl.BlockSpec((tm, D_MODEL), lambda i, d: (blk0 + i, 0)),
                  pl.BlockSpec((tm, LANES), lambda i, d: (blk0 + i, 0)),
                  full(mod0), full(mod1), full(g), full(w)],
        out_specs=[pl.BlockSpec((tm, D_MODEL), lambda i, d: (i, 0))] * 4,
        scratch_shapes=[pltpu.VMEM((2 * tm * SLABS, LANES), F32),
                        pltpu.VMEM((2 * tm * SLABS, LANES), F32),
                        pltpu.SemaphoreType.DMA((2,))])
    return pl.pallas_call(
        functools.partial(_combine_proj_kernel, row0=row0),
        grid_spec=grid_spec,
        out_shape=[jax.ShapeDtypeStruct((t, D_MODEL), F32)] + [jax.ShapeDtypeStruct((t, C_WIDTH), BF16)] * 3,
        compiler_params=_cparams(("arbitrary",)),
        name="moe_combine_proj",
    )(dest, yb, x, rt, mod0, mod1, g, w)


def _moe_plan(rtt, cnt):
    t = rtt.shape[1]
    eid = rtt[0:2].astype(jnp.int32)
    rank = rtt[4:6].astype(jnp.int32)
    counts = cnt[0, N_GROUPS:N_GROUPS + N_EXPERTS].astype(jnp.int32)
    padded = (counts + EXPERT_ROWS - 1) // EXPERT_ROWS * EXPERT_ROWS
    seg_end = jnp.cumsum(padded)
    seg_start = seg_end - padded
    experts = jnp.arange(N_EXPERTS, dtype=jnp.int32)
    start = jnp.sum(jnp.where(eid[..., None] == experts, seg_start, 0), axis=-1)
    dest = (start + rank).reshape(-1)
    n_blocks = (2 * t + N_EXPERTS * (EXPERT_ROWS - 1) + EXPERT_ROWS - 1) // EXPERT_ROWS
    tok = jnp.tile(jnp.arange(t, dtype=jnp.int32), 2)
    slot_tok = (jnp.arange(n_blocks * EXPERT_ROWS, dtype=jnp.int32) % t).at[dest].set(
        tok, unique_indices=True, mode='promise_in_bounds')
    first_row = jnp.arange(n_blocks, dtype=jnp.int32) * EXPERT_ROWS
    block_expert = jnp.minimum(jnp.sum((seg_end[None, :] <= first_row[:, None]).astype(jnp.int32), axis=1),
                               N_EXPERTS - 1)
    n_valid = (seg_end[-1:] // EXPERT_ROWS).astype(jnp.int32)
    return dest, slot_tok, block_expert, n_valid


def _attn_c_prompt_kernel(q_ref, k_ref, v_ref, o_ref):
    for p in range(C_HEADS // 2):
        sl = slice(p * LANES, (p + 1) * LANES)
        kb = k_ref[:, sl].astype(BF16)
        vb = _with_ones(v_ref[:, sl])
        qp = q_ref[:, sl].astype(F32) * (C_HEAD_DIM ** -0.5)
        o_ref[:, sl] = _softmax_pair(qp, lambda qm: [_dot_nt(qm, kb)],
                                     lambda ps: _dot(ps[0], vb)).astype(o_ref.dtype)


def _attn_c_prompt(q, k, v):
    blk = pl.BlockSpec((SEQ, C_WIDTH), lambda b: (b, 0))
    return pl.pallas_call(
        _attn_c_prompt_kernel,
        grid=(BATCH,),
        in_specs=[blk, blk, blk],
        out_specs=blk,
        out_shape=jax.ShapeDtypeStruct((T_PROMPT, C_WIDTH), BF16),
        compiler_params=_cparams(("arbitrary",)),
        name="attn_c_prompt",
    )(q, k, v)


def _na_key_start(r0):
    rows = DEC_SEQ // GRID_W
    return jnp.minimum(jnp.clip(r0 - NA_ROWS // 2, 0, rows - NA_ROWS), rows - NA_KROWS)


def _na_block_plan():
    rows = DEC_SEQ // GRID_W
    nblk = rows // NA_QROWS
    plan = []
    for blk in (0, 1, nblk - 1):
        r0 = blk * NA_QROWS
        ks = min(int(np.clip(r0 - NA_ROWS // 2, 0, rows - NA_ROWS)), rows - NA_KROWS)
        per_row = []
        for i in range(NA_QROWS):
            r = r0 + i
            rs = int(np.clip(r - NA_ROWS // 2, 0, rows - NA_ROWS))
            start = ks - r + NA_ROWS - 1 + NA_KROWS
            ok = [rs <= ks + j < rs + NA_ROWS for j in range(NA_KROWS)]
            per_row.append((start, ok))
        plan.append(per_row)
    return plan


def _attn_c_sample_kernel(q_ref, k_ref, v_ref, ck_ref, cv_ref, toe_ref, o_ref, bias_ref):
    rows = DEC_SEQ // GRID_W
    nblk = rows // NA_QROWS
    w = GRID_W

    @pl.when(jnp.logical_and(pl.program_id(1) == 0, pl.program_id(2) == 0))
    def _():
        neg = jnp.full((w, w), NEG, F32)
        for t, per_row in enumerate(_na_block_plan()):
            for half in range(2):
                for i, (start, ok) in enumerate(per_row):
                    for j in range(0, NA_KROWS, 2):
                        pieces = [toe_ref[0, half, start + jj] if ok[jj] else neg for jj in (j, j + 1)]
                        bias_ref[t, half, i * w:(i + 1) * w, j * w:(j + 2) * w] = jnp.concatenate(pieces, axis=1)

    i = pl.program_id(2)
    r0 = i * NA_QROWS
    k0 = pl.multiple_of(_na_key_start(r0) * GRID_W, GRID_W)
    btype = jnp.where(i == 0, 0, jnp.where(i == nblk - 1, 2, 1))
    nk = NA_KROWS * GRID_W
    kw = k_ref[pl.ds(k0, nk), :].astype(BF16)
    vw = v_ref[pl.ds(k0, nk), :].astype(BF16)
    kc = ck_ref[0].astype(BF16)
    vc = cv_ref[0].astype(BF16)
    qp = q_ref[...].astype(F32) * (C_HEAD_DIM ** -0.5)
    lo = _lane(qp.shape) < 64
    outs = []
    for half in range(2):
        qm = jnp.where(lo if half == 0 else jnp.logical_not(lo), qp, 0.0).astype(BF16)
        s_win = _dot_nt(qm, kw) + bias_ref[btype, half]
        s_ctx = _dot_nt(qm, kc)
        m = jnp.maximum(s_win.max(axis=-1, keepdims=True), s_ctx.max(axis=-1, keepdims=True))
        e_win = jnp.exp(s_win - m)
        e_ctx = jnp.exp(s_ctx - m)
        l = e_win.sum(axis=-1, keepdims=True) + e_ctx.sum(axis=-1, keepdims=True)
        o = _dot(e_win.astype(BF16), vw) + _dot(e_ctx.astype(BF16), vc)
        outs.append(o / l)
    o_ref[...] = jnp.where(lo, outs[0], outs[1]).astype(o_ref.dtype)


def _attn_c_sample(q, k, v, ck, cv, toe):
    rows = DEC_SEQ // GRID_W
    nblk = rows // NA_QROWS
    qrows = NA_QROWS * GRID_W
    npair = C_HEADS // 2
    return pl.pallas_call(
        _attn_c_sample_kernel,
        grid=(npair, DEC_BATCH, nblk),
        in_specs=[pl.BlockSpec((qrows, LANES), lambda p, b, i: (b * nblk + i, p)),
                  pl.BlockSpec((DEC_SEQ, LANES), lambda p, b, i: (b, p)),
                  pl.BlockSpec((DEC_SEQ, LANES), lambda p, b, i: (b, p)),
                  pl.BlockSpec((1, PAST_LEN, LANES), lambda p, b, i: (b, 0, p)),
                  pl.BlockSpec((1, PAST_LEN, LANES), lambda p, b, i: (b, 0, p)),
                  pl.BlockSpec((1,) + toe.shape[1:], lambda p, b, i: (p, 0, 0, 0, 0))],
        out_specs=pl.BlockSpec((qrows, LANES), lambda p, b, i: (b * nblk + i, p)),
        out_shape=jax.ShapeDtypeStruct((T_SAMPLE, C_WIDTH), BF16),
        scratch_shapes=[pltpu.VMEM((3, 2, qrows, NA_KROWS * GRID_W), F32)],
        compiler_params=_cparams(("arbitrary", "arbitrary", "arbitrary")),
        name="attn_c_sample",
    )(q, k, v, ck, cv, toe)


def _na_toeplitz(rpb):
    w = GRID_W
    nd_r, nd_c = 2 * NA_ROWS - 1, 2 * NA_COLS - 1
    c = np.arange(w)
    cs = np.clip(c - NA_COLS // 2, 0, w - NA_COLS)
    col_ok = (c[None, :] >= cs[:, None]) & (c[None, :] < cs[:, None] + NA_COLS)
    dcol = c[None, :] - c[:, None] + NA_COLS - 1
    onehot = (np.arange(nd_c)[:, None, None] == dcol[None]).reshape(nd_c, w * w)
    toe = jnp.dot(rpb.reshape(C_HEADS * nd_r, nd_c), jnp.asarray(onehot, F32), precision=lax.Precision.HIGHEST)
    toe = jnp.where(col_ok[None, None], toe.reshape(C_HEADS, nd_r, w, w), NEG)
    toe = jnp.pad(toe, ((0, 0), (NA_KROWS, NA_KROWS), (0, 0), (0, 0)), constant_values=NEG)
    return toe.reshape(C_HEADS // 2, 2, nd_r + 2 * NA_KROWS, w, w)


def _rope_tables():
    half = A_HEAD_DIM // 2
    t = jnp.arange(DEC_SEQ)
    row = (t // GRID_W).astype(F32)
    colp = (t % GRID_W).astype(F32)
    freqs = 1.0 / (ROPE_BASE ** (jnp.arange(0, half, 2, dtype=F32) / half))
    d = np.arange(LANES) % A_HEAD_DIM
    pos = jnp.where(jnp.asarray(d < half)[None, :], row[:, None], colp[:, None])
    ang = pos * freqs[d % (half // 2)][None, :]
    sign = jnp.asarray(np.where((d % half) < half // 2, -1.0, 1.0), F32)[None, :]
    return jnp.cos(ang), jnp.sin(ang) * sign


def _head_avg_matrix():
    idx = np.arange(LANES) // A_HEAD_DIM
    return jnp.asarray((idx[:, None] == idx[None, :]).astype(np.float32) / A_HEAD_DIM, BF16)


def _tri_matrices():
    i = np.arange(MLSTM_CHUNK)
    upper = (i[:, None] <= i[None, :]).astype(np.float32)
    lower = (i[:, None] >= i[None, :]).astype(np.float32)
    return jnp.asarray(np.stack([upper, lower]), BF16)


def _router_weights(wg, bg, we, be):
    w = jnp.zeros((D_MODEL, LANES), F32).at[:, :N_GROUPS].set(wg).at[:, N_GROUPS:N_GROUPS + N_EXPERTS].set(we)
    b = jnp.zeros((1, LANES), F32).at[0, :N_GROUPS].set(bg).at[0, N_GROUPS:N_GROUPS + N_EXPERTS].set(be)
    hi = w.astype(BF16)
    lo = (w - hi.astype(F32)).astype(BF16)
    return jnp.concatenate([hi, lo], axis=1), b


def _moe(xn, rtt, cnt, w1, w3, w2, layer):
    dest, slot_tok, block_expert, n_valid = _moe_plan(rtt, cnt)
    return dest, _experts(block_expert, slot_tok, n_valid, xn, w1, w3, w2, layer)


def kernel(x_prompt, x_sample, cache_attn_k, cache_attn_v, state_mlstm_C, state_mlstm_n, state_mlstm_m,
           cache_na_k, cache_na_v, c, c_ctx, norm_mix, norm_ffn, norm_final, ada_w, ada_b,
           ab_w_in, ab_w_out, ab_q_norm, ab_k_norm, ab_gate_bias, ab_out_norm,
           na_w_in, na_w_out, na_rpb, moe_wg, moe_bg, moe_we, moe_be, moe_w1, moe_w3, moe_w2):
    xp = x_prompt.reshape(T_PROMPT, D_MODEL)
    xs = x_sample.reshape(T_SAMPLE, D_MODEL)
    cond =jnp.zeros((N_COND, D_MODEL), F32).at[0].set(c_ctx).at[1:1 + DEC_BATCH].set(c)
    mod = _modulation(cond, ada_w, ada_b)
    gfin = norm_final.reshape(1, D_MODEL)

    w_in = ab_w_in[0]
    o_aq, o_ak, o_av, o_bq, o_bk, o_bv, o_bo, o_bg = np.cumsum((0,) + (A_WIDTH, A_KV_WIDTH, A_KV_WIDTH,
                                                                       B_WIDTH, B_WIDTH, B_WIDTH, B_WIDTH))
    wb = jnp.concatenate([w_in[:, o_bq:o_bk], w_in[:, o_bv:o_bg]], axis=1).astype(BF16)
    wq = w_in[:, o_aq:o_ak].astype(BF16)
    wkv = w_in[:, o_ak:o_bq].astype(BF16)
    gate_rows = np.array([0, 8, 1, 9])
    wg = w_in[:, o_bg:o_bg + 4 * B_HEADS].reshape(D_MODEL, 4, B_HEADS)
    wgt = jnp.zeros((B_HEADS, 16, D_MODEL), F32).at[:, gate_rows, :].set(wg.transpose(2, 1, 0))
    wgt = jnp.concatenate([w_in[:, o_bk:o_bv].T, wgt.reshape(16 * B_HEADS, D_MODEL)], axis=0).astype(BF16)
    g_mix = norm_mix[0].reshape(1, D_MODEL)
    ob_p, oq_p, okv_p, okt_p, ogt_p = _proj_ab(xp, mod[0], g_mix, wb, wq, wkv, wgt, 0)
    ob_s, oq_s, okv_s, okt_s, ogt_s = _proj_ab(xs, mod[0], g_mix, wb, wq, wkv, wgt, T_PROMPT)

    qn = jnp.tile(ab_q_norm[0], 2).reshape(1, LANES)
    kn = jnp.tile(ab_k_norm[0], 2).reshape(1, LANES)
    bd = _head_avg_matrix()
    cos, sin = _rope_tables()
    a_p, new_k, new_v = _attn_a_prompt(oq_p, okv_p, qn, kn, bd)
    ck = cache_attn_k[:, 0].reshape(DEC_BATCH, PAST_LEN, A_KV_WIDTH)
    cv = cache_attn_v[:, 0].reshape(DEC_BATCH, PAST_LEN, A_KV_WIDTH)
    a_s = _attn_a_sample(oq_s, okv_s, ck, cv, cos, sin, qn, kn, bd)

    gb = ab_gate_bias[0]
    brow = jnp.zeros((B_HEADS, 16, LANES), F32).at[:, gate_rows, :].set(
        jnp.broadcast_to(gb.T[:, :, None], (B_HEADS, 4, LANES)))
    onorm = ab_out_norm[0].reshape(B_HEADS, 1, B_HEAD_DIM)
    tri = _tri_matrices()
    b_p, cT, nT, mT = _mlstm(ob_p, okt_p, ogt_p, brow, onorm, tri, n=SEQ, nseq=BATCH, emit_state=True)
    n0 = jnp.broadcast_to(state_mlstm_n[:, 0][..., None], state_mlstm_C[:, 0].shape)
    c0 = jnp.concatenate([state_mlstm_C[:, 0], n0], axis=-1)
    m0 = jnp.broadcast_to(state_mlstm_m[:, 0].transpose(0, 2, 1)[..., None], (DEC_BATCH, B_HEADS, 2, LANES))
    (b_s,) = _mlstm(ob_s, okt_s, ogt_s, brow, onorm, tri, n=DEC_SEQ, nseq=DEC_BATCH, init=(c0, m0))

    w_out = ab_w_out[0].astype(BF16)
    wr, br = _router_weights(moe_wg[0], moe_bg[0], moe_we[0], moe_be[0])
    x1, xn, rt, rtt, cnt = _post([(xp, xs), (a_p, a_s), (b_p, b_s)], [w_out[:A_WIDTH], w_out[A_WIDTH:]], mod[0],
                                 norm_ffn[0].reshape(1, D_MODEL), wr, br)
    dest, yb = _moe(xn, rtt, cnt, moe_w1, moe_w3, moe_w2, 0)

    g_mix = norm_mix[1].reshape(1, D_MODEL)
    w_in = na_w_in[0].astype(BF16)
    x_p, q_p, k_p, v_p = _combine_proj(dest, yb, x1, rt, mod[0], mod[1], g_mix, w_in, 0, T_PROMPT)
    x_s, q_s, k_s, v_s = _combine_proj(dest, yb, x1, rt, mod[0], mod[1], g_mix, w_in, T_PROMPT, T_SAMPLE)
    o_p = _attn_c_prompt(q_p, k_p, v_p)
    nck = cache_na_k[:, 0].reshape(DEC_BATCH, PAST_LEN, C_WIDTH)
    ncv = cache_na_v[:, 0].reshape(DEC_BATCH, PAST_LEN, C_WIDTH)
    o_s = _attn_c_sample(q_s, k_s, v_s, nck, ncv, _na_toeplitz(na_rpb[0]))
    wr, br = _router_weights(moe_wg[1], moe_bg[1], moe_we[1], moe_be[1])
    x1, xn, rt, rtt, cnt = _post([(x_p, x_s), (o_p, o_s)], [na_w_out[0].astype(BF16)], mod[1],
                                 norm_ffn[1].reshape(1, D_MODEL), wr, br)
    dest, yb = _moe(xn, rtt, cnt, moe_w1, moe_w3, moe_w2, 1)
    y_prompt = _combine(dest, yb, x1, rt, mod[1], gfin, True, 0, T_PROMPT).reshape(BATCH, SEQ, D_MODEL)
    y_sample = _combine(dest, yb, x1, rt, mod[1], gfin, True, T_PROMPT, T_SAMPLE).reshape(DEC_BATCH, DEC_SEQ, D_MODEL)
    new_attn_k = new_k.reshape(BATCH, 1, SEQ, A_KV_HEADS, A_HEAD_DIM)
    new_attn_v = new_v.reshape(BATCH, 1, SEQ, A_KV_HEADS, A_HEAD_DIM)
    new_mlstm_C = cT[:, None]
    new_mlstm_n = nT.transpose(0, 2, 1, 3)[:, None]
    new_mlstm_m = mT[..., 0].transpose(0, 2, 1)[:, None]
    new_na_k = k_p.astype(F32).reshape(BATCH, 1, SEQ, C_HEADS, C_HEAD_DIM)
    new_na_v = v_p.astype(F32).reshape(BATCH, 1, SEQ, C_HEADS, C_HEAD_DIM)
    return (y_prompt, y_sample, new_attn_k, new_attn_v, new_mlstm_C, new_mlstm_n, new_mlstm_m,
            new_na_k, new_na_v)
```

```python
import functools

import numpy as np
import jax
import jax.numpy as jnp
from jax import lax
from jax.experimental import pallas as pl
from jax.experimental.pallas import tpu as pltpu

F32 = jnp.float32
BF16 = jnp.bfloat16

D_MODEL = 1024
BATCH = 32
SEQ = 256
DEC_BATCH = 4
DEC_SEQ = 2048
PAST_LEN = 256
GRID_W = 64
A_HEADS = 8
A_KV_HEADS = 2
A_HEAD_DIM = 64
A_WIDTH = A_HEADS * A_HEAD_DIM
A_KV_WIDTH = A_KV_HEADS * A_HEAD_DIM
B_HEADS = 4
B_HEAD_DIM = 128
B_WIDTH = B_HEADS * B_HEAD_DIM
MLSTM_CHUNK = 128
C_HEADS = 16
C_HEAD_DIM = 64
C_WIDTH = C_HEADS * C_HEAD_DIM
NA_ROWS = 8
NA_COLS = 16
N_GROUPS = 4
EXPERTS_PER_GROUP = 8
N_EXPERTS = N_GROUPS * EXPERTS_PER_GROUP
D_EXPERT = 512
MOE_BLOCK = 128
ROPE_BASE = 10000.0
NORM_EPS = 1e-6

T_PROMPT = BATCH * SEQ
T_SAMPLE = DEC_BATCH * DEC_SEQ
T_ALL = T_PROMPT + T_SAMPLE
N_COND = 8
LANES = 128
SLABS = D_MODEL // LANES // 2
U32 = jnp.uint32
TOK_BLOCK = 256
POST_BLOCK = 2 * TOK_BLOCK
MLSTM_HEADS_PER_STEP = 2
EXPERT_ROWS = 512
NA_QROWS = 4
NA_KROWS = 12
NEG = -1e30
VMEM_LIMIT = 56 * 1024 * 1024


def _cparams(sem):
    return pltpu.CompilerParams(dimension_semantics=sem, vmem_limit_bytes=VMEM_LIMIT)


def _split2(x):
    hi = x.astype(BF16)
    lo = (x - hi.astype(F32)).astype(BF16)
    return hi, lo


def _split3(x):
    hi = x.astype(BF16)
    r = x - hi.astype(F32)
    mid = r.astype(BF16)
    lo = (r - mid.astype(F32)).astype(BF16)
    return hi, mid, lo


def _dot(a, b):
    return jnp.dot(a, b, preferred_element_type=F32)


def _dot_nt(a, b):
    return lax.dot_general(a, b, (((1,), (1,)), ((), ())), preferred_element_type=F32)


def _dot_exact_rhs(x, b):
    hi, mid, lo = _split3(x)
    return _dot(hi, b) + _dot(mid, b) + _dot(lo, b)


def _rms(x, g):
    ms = jnp.mean(x * x, axis=-1, keepdims=True)
    return (x * lax.rsqrt(ms + NORM_EPS)) * g


def _mod_row(tok0):
    return jnp.where(tok0 < T_PROMPT, 0, 1 + (tok0 - T_PROMPT) // DEC_SEQ)


def _mod_part(mod_ref, r, idx):
    return mod_ref[pl.ds(r, 1), idx * D_MODEL:(idx + 1) * D_MODEL]


def _head_rms(x, w, bd):
    hi, lo = _split2(x * x)
    ms = _dot(hi, bd) + _dot(lo, bd)
    return (x * lax.rsqrt(ms + NORM_EPS)) * w


def _to_token_tiles(ref, x, tile0=0):
    m = x.shape[0]
    bits = lambda v: pltpu.bitcast(v.astype(BF16).astype(F32), U32)
    for s in range(SLABS):
        lo = bits(x[:, s * LANES:(s + 1) * LANES]) >> 16
        hi = bits(x[:, (s + SLABS) * LANES:(s + SLABS + 1) * LANES]) & U32(0xFFFF0000)
        ref[pl.ds(tile0 * SLABS + s, m, stride=SLABS), :] = lo | hi


def _from_token_tiles(ref, tile0, m):
    lo, hi = [], []
    for s in range(SLABS):
        w = ref[pl.ds(tile0 * SLABS + s, m, stride=SLABS), :]
        lo.append(pltpu.bitcast(w << 16, F32))
        hi.append(pltpu.bitcast(w & U32(0xFFFF0000), F32))
    return jnp.concatenate(lo + hi, axis=1)


def _lane(shape):
    return lax.broadcasted_iota(jnp.int32, shape, len(shape) - 1)


def _dup_half(x, g):
    xr = pltpu.roll(x, 64, 1)
    lo = _lane(x.shape) < 64
    return jnp.where(lo, x, xr) if g == 0 else jnp.where(lo, xr, x)


def _rope(x, cos, sin_signed):
    lane = _lane(x.shape)
    partner = jnp.where((lane % 32) < 16, pltpu.roll(x, LANES - 16, 1), pltpu.roll(x, 16, 1))
    return x * cos + partner * sin_signed


def _with_ones(v):
    return jnp.concatenate([v.astype(BF16), jnp.ones(v.shape, BF16)], axis=1)


def _softmax_pair(qp, score_fn, value_fn):
    lo = _lane(qp.shape) < 64
    outs = []
    for half in range(2):
        qm = jnp.where(lo if half == 0 else jnp.logical_not(lo), qp, 0.0).astype(BF16)
        ss = score_fn(qm)
        m = ss[0].max(axis=-1, keepdims=True)
        for s in ss[1:]:
            m = jnp.maximum(m, s.max(axis=-1, keepdims=True))
        o = value_fn([jnp.exp(s - m).astype(BF16) for s in ss])
        outs.append(o[:, :LANES] / o[:, LANES:])
    return jnp.where(lo, outs[0], outs[1])


def _mod_kernel(cond_ref, w_ref, b_ref, o_ref):
    c = cond_ref[...]
    s = c * jax.nn.sigmoid(c)
    s_hi, s_lo = _split2(s)
    w_hi, w_lo = _split2(w_ref[0])
    o_ref[0] = _dot(s_hi, w_hi) + _dot(s_lo, w_hi) + _dot(s_hi, w_lo) + b_ref[0]


def _modulation(cond, ada_w, ada_b):
    depth, d, n = ada_w.shape
    tn = 1536
    return pl.pallas_call(
        _mod_kernel,
        grid=(depth, n // tn),
        in_specs=[pl.BlockSpec((N_COND, d), lambda l, j: (0, 0)),
                  pl.BlockSpec((1, d, tn), lambda l, j: (l, 0, j)),
                  pl.BlockSpec((1, 1, tn), lambda l, j: (l, 0, j))],
        out_specs=pl.BlockSpec((1, N_COND, tn), lambda l, j: (l, 0, j)),
        out_shape=jax.ShapeDtypeStruct((depth, N_COND, n), F32),
        compiler_params=_cparams(("arbitrary", "arbitrary")),
        name="adaln_modulation",
    )(cond, ada_w, ada_b.reshape(depth, 1, n))


def _norm_mod(x_ref, mod_ref, g_ref, shift_idx, scale_idx, row0):
    r = _mod_row(row0 + pl.program_id(0) * x_ref.shape[0])
    h = _rms(x_ref[...], g_ref[...])
    return h * (1.0 + _mod_part(mod_ref, r, scale_idx)) + _mod_part(mod_ref, r, shift_idx)


def _proj_ab_kernel(x_ref, mod_ref, g_ref, wb_ref, wq_ref, wkv_ref, wgt_ref,
                    ob_ref, oq_ref, okv_ref, okt_ref, ogt_ref, *, row0):
    tm = x_ref.shape[0]
    hb = _norm_mod(x_ref, mod_ref, g_ref, 0, 1, row0).astype(BF16)
    ob_ref[...] = _dot(hb, wb_ref[...]).astype(ob_ref.dtype)
    oq_ref[...] = _dot(hb, wq_ref[...]).astype(oq_ref.dtype)
    okv_ref[...] = _dot(hb, wkv_ref[...]).astype(okv_ref.dtype)
    gt = _dot_nt(wgt_ref[...], hb)
    for j in range(tm // LANES):
        okt_ref[j] = gt[:B_WIDTH, j * LANES:(j + 1) * LANES].astype(okt_ref.dtype)
        ogt_ref[j] = gt[B_WIDTH:, j * LANES:(j + 1) * LANES]


def _proj_ab(x, mod, g, wb, wq, wkv, wgt, row0):
    t = x.shape[0]
    tm = TOK_BLOCK
    full = lambda a: pl.BlockSpec(a.shape, lambda i: (0,) * a.ndim)
    return pl.pallas_call(
        functools.partial(_proj_ab_kernel, row0=row0),
        grid=(t // tm,),
        in_specs=[pl.BlockSpec((tm, D_MODEL), lambda i: (i, 0)), full(mod), full(g),
                  full(wb), full(wq), full(wkv), full(wgt)],
        out_specs=[pl.BlockSpec((tm, wb.shape[1]), lambda i: (i, 0)),
                   pl.BlockSpec((tm, A_WIDTH), lambda i: (i, 0)),
                   pl.BlockSpec((tm, 2 * A_KV_WIDTH), lambda i: (i, 0)),
                   pl.BlockSpec((tm // LANES, B_WIDTH, LANES), lambda i: (i, 0, 0)),
                   pl.BlockSpec((tm // LANES, wgt.shape[0] - B_WIDTH, LANES), lambda i: (i, 0, 0))],
        out_shape=[jax.ShapeDtypeStruct((t, wb.shape[1]), BF16),
                   jax.ShapeDtypeStruct((t, A_WIDTH), BF16),
                   jax.ShapeDtypeStruct((t, 2 * A_KV_WIDTH), BF16),
                   jax.ShapeDtypeStruct((t // LANES, B_WIDTH, LANES), BF16),
                   jax.ShapeDtypeStruct((t // LANES, wgt.shape[0] - B_WIDTH, LANES), F32)],
        compiler_params=_cparams(("arbitrary",)),
        name="proj_ab",
    )(x, mod, g, wb, wq, wkv, wgt)


def _gqa_block(q_ref, qn, bd, kd_ref, vd_ref, o_ref, rope=None):
    for p in range(A_HEADS // 2):
        g = p // (A_HEADS // 2 // A_KV_HEADS)
        qp = _head_rms(q_ref[:, p * LANES:(p + 1) * LANES].astype(F32), qn, bd)
        if rope is not None:
            qp = _rope(qp, rope[0], rope[1])
        qp = qp * (A_HEAD_DIM ** -0.5)
        o_ref[:, p * LANES:(p + 1) * LANES] = _softmax_pair(
            qp, lambda qm: [_dot_nt(qm, kd_ref[g])], lambda ps: _dot(ps[0], vd_ref[g])).astype(o_ref.dtype)


def _attn_a_prompt_kernel(q_ref, kv_ref, qn_ref, kn_ref, bd_ref, o_ref, knew_ref, vnew_ref, kd_ref, vd_ref):
    bd = bd_ref[...]
    k = _head_rms(kv_ref[:, :LANES].astype(F32), kn_ref[...], bd)
    v = kv_ref[:, LANES:].astype(F32)
    knew_ref[...] = k
    vnew_ref[...] = v
    for g in range(A_KV_HEADS):
        kd_ref[g] = _dup_half(k, g).astype(BF16)
        vd_ref[g] = _with_ones(_dup_half(v, g))
    _gqa_block(q_ref, qn_ref[...], bd, kd_ref, vd_ref, o_ref)


def _attn_a_prompt(q, kv, qn, kn, bd):
    nb = BATCH
    full = lambda a: pl.BlockSpec(a.shape, lambda b: (0,) * a.ndim)
    return pl.pallas_call(
        _attn_a_prompt_kernel,
        grid=(nb,),
        in_specs=[pl.BlockSpec((SEQ, A_WIDTH), lambda b: (b, 0)),
                  pl.BlockSpec((SEQ, 2 * A_KV_WIDTH), lambda b: (b, 0)),
                  full(qn), full(kn), full(bd)],
        out_specs=[pl.BlockSpec((SEQ, A_WIDTH), lambda b: (b, 0)),
                   pl.BlockSpec((SEQ, A_KV_WIDTH), lambda b: (b, 0)),
                   pl.BlockSpec((SEQ, A_KV_WIDTH), lambda b: (b, 0))],
        out_shape=[jax.ShapeDtypeStruct((T_PROMPT, A_WIDTH), BF16),
                   jax.ShapeDtypeStruct((T_PROMPT, A_KV_WIDTH), F32),
                   jax.ShapeDtypeStruct((T_PROMPT, A_KV_WIDTH), F32)],
        scratch_shapes=[pltpu.VMEM((A_KV_HEADS, SEQ, LANES), BF16),
                        pltpu.VMEM((A_KV_HEADS, SEQ, 2 * LANES), BF16)],
        compiler_params=_cparams(("arbitrary",)),
        name="attn_a_prompt",
    )(q, kv, qn, kn, bd)


_A_QBLOCK = 256


def _attn_a_sample_kernel(q_ref, kv_ref, ck_ref, cv_ref, cos_ref, sin_ref, cosq_ref, sinq_ref,
                          qn_ref, kn_ref, bd_ref, o_ref, kd_ref, vd_ref):
    bd = bd_ref[...]

    @pl.when(pl.program_id(1) == 0)
    def _():
        for g in range(A_KV_HEADS):
            kd_ref[g, :PAST_LEN] = _dup_half(ck_ref[0], g).astype(BF16)
            vd_ref[g, :PAST_LEN] = _with_ones(_dup_half(cv_ref[0], g))
        rows = 256
        for c in range(DEC_SEQ // rows):
            sl = slice(c * rows, (c + 1) * rows)
            k = _head_rms(kv_ref[sl, :LANES].astype(F32), kn_ref[...], bd)
            k = _rope(k, cos_ref[sl, :], sin_ref[sl, :])
            v = kv_ref[sl, LANES:].astype(F32)
            dst = slice(PAST_LEN + c * rows, PAST_LEN + (c + 1) * rows)
            for g in range(A_KV_HEADS):
                kd_ref[g, dst] = _dup_half(k, g).astype(BF16)
                vd_ref[g, dst] = _with_ones(_dup_half(v, g))

    _gqa_block(q_ref, qn_ref[...], bd, kd_ref, vd_ref, o_ref, rope=(cosq_ref[...], sinq_ref[...]))


def _attn_a_sample(q, kv, ck, cv, cos, sin, qn, kn, bd):
    nq = DEC_SEQ // _A_QBLOCK
    full = lambda a: pl.BlockSpec(a.shape, lambda b, i: (0,) * a.ndim)
    tk = PAST_LEN + DEC_SEQ
    return pl.pallas_call(
        _attn_a_sample_kernel,
        grid=(DEC_BATCH, nq),
        in_specs=[pl.BlockSpec((_A_QBLOCK, A_WIDTH), lambda b, i: (b * nq + i, 0)),
                  pl.BlockSpec((DEC_SEQ, 2 * A_KV_WIDTH), lambda b, i: (b, 0)),
                  pl.BlockSpec((1, PAST_LEN, A_KV_WIDTH), lambda b, i: (b, 0, 0)),
                  pl.BlockSpec((1, PAST_LEN, A_KV_WIDTH), lambda b, i: (b, 0, 0)),
                  full(cos), full(sin),
                  pl.BlockSpec((_A_QBLOCK, LANES), lambda b, i: (i, 0)),
                  pl.BlockSpec((_A_QBLOCK, LANES), lambda b, i: (i, 0)),
                  full(qn), full(kn), full(bd)],
        out_specs=pl.BlockSpec((_A_QBLOCK, A_WIDTH), lambda b, i: (b * nq + i, 0)),
        out_shape=jax.ShapeDtypeStruct((T_SAMPLE, A_WIDTH), BF16),
        scratch_shapes=[pltpu.VMEM((A_KV_HEADS, tk, LANES), BF16),
                        pltpu.VMEM((A_KV_HEADS, tk, 2 * LANES), BF16)],
        compiler_params=_cparams(("arbitrary", "arbitrary")),
        name="attn_a_sample",
    )(q, kv, ck, cv, cos, sin, cos, sin, qn, kn, bd)


def _log_sigmoid(x):
    return -(jnp.maximum(-x, 0.0) + jnp.log1p(jnp.exp(-jnp.abs(x))))


def _mlstm_kernel(*refs, n, has_init, emit_state):
    it = iter(refs)
    q_ref, v_ref, og_ref, kt_ref, gt_ref, brow_ref, onorm_ref, tri_ref = [next(it) for _ in range(8)]
    if has_init:
        c0_ref, m0_ref = [next(it) for _ in range(2)]
    out_ref = next(it)
    if emit_state:
        cT_ref, nT_ref, mT_ref = [next(it) for _ in range(3)]
    h_ref, row_ref, c_ref = [next(it) for _ in range(3)]

    L = MLSTM_CHUNK
    D = B_HEAD_DIM
    nc = n // L
    hb = MLSTM_HEADS_PER_STEP

    lane = lax.broadcasted_iota(jnp.int32, (nc * 8, L), 1)
    is_fwd = lax.broadcasted_iota(jnp.int32, (nc * 8, L), 0) % 8 == 0

    def running_max(x, suffix):
        for sh in (1, 2, 4, 8, 16, 32, 64):
            if suffix:
                x = jnp.where(lane < L - sh, jnp.maximum(x, pltpu.roll(x, L - sh, 1)), x)
            else:
                x = jnp.where(lane >= sh, jnp.maximum(x, pltpu.roll(x, sh, 1)), x)
        return x

    for hh in range(hb):
        gt = gt_ref[:, 16 * hh:16 * hh + 16, :] + brow_ref[hh][None]
        li = gt[:, 0:8, :].reshape(nc * 8, L)
        lf = _log_sigmoid(gt[:, 8:16, :]).reshape(nc * 8, L)
        cum = jnp.where(is_fwd, _dot_exact_rhs(lf, tri_ref[0]), _dot_exact_rhs(lf, tri_ref[1]))
        a = li - cum
        planes = (cum, a, jnp.broadcast_to(lf.sum(axis=-1, keepdims=True), (nc * 8, L)),
                  jnp.where(is_fwd, running_max(a, False), running_max(a, True)),
                  jnp.broadcast_to(a.max(axis=-1, keepdims=True), (nc * 8, L)))
        for p, val in enumerate(planes):
            row_ref[hh, p] = val.reshape(nc, 8, L)

    chains = [(hh, d) for hh in range(hb) for d in range(2)]
    if has_init:
        m_init = []
        for hh, d in chains:
            c_ref[2 * hh + d] = c0_ref[0, d, hh]
            m_init.append(m0_ref[0, hh, d:d + 1, :])
        m_init = tuple(m_init)
    else:
        c_ref[...] = jnp.zeros(c_ref.shape, F32)
        m_init = tuple(jnp.zeros((1, L), F32) for _ in chains)

    t_idx = lax.broadcasted_iota(jnp.int32, (L, L), 0)
    s_idx = lax.broadcasted_iota(jnp.int32, (L, L), 1)
    masks = (s_idx <= t_idx, s_idx >= t_idx)
    ones = jnp.ones((L, L), F32)

    def step(c, hh, d, m):
        r0 = pl.multiple_of(c * L, L)
        hl = slice(hh * L, (hh + 1) * L)
        qb = q_ref[pl.ds(r0, L), hl].astype(BF16)
        kst = kt_ref[c, hl, :].astype(F32) * (D ** -0.5)
        v_ext = jnp.concatenate([v_ref[pl.ds(r0, L), hl], ones], axis=1).astype(BF16)
        cum, a_row, tot, amax_run, amax = [row_ref[hh, p, c][d:d + 1, :] for p in range(5)]
        m_cum = jnp.broadcast_to(cum, (L, L)).T
        m_run = jnp.broadcast_to(amax_run, (L, L)).T
        dlog = jnp.where(masks[d], m_cum + a_row, -jnp.inf)
        inter = m_cum + m
        m_t = jnp.maximum(inter, m_cum + m_run)
        w_in = jnp.exp(dlog - m_t)
        w_st = jnp.exp(inter - m_t)
        a = _dot(qb, kst.astype(BF16)) * w_in
        ci = 2 * hh + d
        cext = c_ref[ci]
        p_state = _dot(qb, cext.astype(BF16))
        p_intra = _dot(a.astype(BF16), v_ext)
        num = w_st * p_state[:, :D] + p_intra[:, :D]
        den = w_st * p_state[:, D:] + p_intra[:, D:]
        h_ref[ci, pl.ds(r0, L), :] = num / jnp.maximum(jnp.abs(den), jnp.exp(-m_t))
        m_new = jnp.maximum(tot + m, amax + tot)
        ws = jnp.exp(a_row + tot - m_new)
        wc = jnp.exp(tot + m - m_new)
        c_ref[ci] = jnp.concatenate([wc, wc], axis=1) * cext + _dot((kst * ws).astype(BF16), v_ext)
        return m_new

    def body(i, carry):
        return tuple(step(i if d == 0 else nc - 1 - i, hh, d, m) for (hh, d), m in zip(chains, carry))

    m_fin = lax.fori_loop(0, nc, body, m_init)

    for hh in range(hb):
        hm = h_ref[2 * hh] + h_ref[2 * hh + 1]
        hl = slice(hh * L, (hh + 1) * L)
        gate = jax.nn.sigmoid(og_ref[:, hl].astype(F32))
        out_ref[:, hl] = (_rms(hm, onorm_ref[hh]) * gate).astype(out_ref.dtype)

    if emit_state:
        for k, (hh, d) in enumerate(chains):
            cext = c_ref[2 * hh + d]
            cT_ref[0, d, hh] = cext[:, :D]
            nT_ref[0, hh, d:d + 1, :] = cext[:, D:].T[0:1, :]
            mT_ref[0, hh, d:d + 1, :] = m_fin[k]


def _mlstm(ob, okt, ogt, brow, onorm, tri, *, n, nseq, init=None, emit_state=False):
    L = MLSTM_CHUNK
    nc = n // L
    H = B_HEADS
    hb = MLSTM_HEADS_PER_STEP
    ng = H // hb
    col = lambda part: (lambda b, g: (b, part * ng + g))
    in_specs = [pl.BlockSpec((n, hb * L), col(0)), pl.BlockSpec((n, hb * L), col(1)),
                pl.BlockSpec((n, hb * L), col(2)),
                pl.BlockSpec((nc, hb * L, L), lambda b, g: (b, g, 0)),
                pl.BlockSpec((nc, 16 * hb, L), lambda b, g: (b, g, 0)),
                pl.BlockSpec((hb, 16, L), lambda b, g: (g, 0, 0)),
                pl.BlockSpec((hb, 1, L), lambda b, g: (g, 0, 0)),
                pl.BlockSpec(tri.shape, lambda b, g: (0, 0, 0))]
    args = [ob, ob, ob, okt, ogt, brow, onorm, tri]
    if init is not None:
        in_specs += [pl.BlockSpec((1, 2, hb, L, 2 * L), lambda b, g: (b, 0, g, 0, 0)),
                     pl.BlockSpec((1, hb, 2, L), lambda b, g: (b, g, 0, 0))]
        args += list(init)
    out_specs = [pl.BlockSpec((n, hb * L), lambda b, g: (b, g))]
    out_shape = [jax.ShapeDtypeStruct((nseq * n, B_WIDTH), BF16)]
    if emit_state:
        out_specs += [pl.BlockSpec((1, 2, hb, L, L), lambda b, g: (b, 0, g, 0, 0)),
                      pl.BlockSpec((1, hb, 2, L), lambda b, g: (b, g, 0, 0)),
                      pl.BlockSpec((1, hb, 2, L), lambda b, g: (b, g, 0, 0))]
        out_shape += [jax.ShapeDtypeStruct((nseq, 2, H, L, L), F32),
                      jax.ShapeDtypeStruct((nseq, H, 2, L), F32),
                      jax.ShapeDtypeStruct((nseq, H, 2, L), F32)]
    return pl.pallas_call(
        functools.partial(_mlstm_kernel, n=n, has_init=init is not None, emit_state=emit_state),
        grid=(nseq, ng),
        in_specs=in_specs,
        out_specs=out_specs,
        out_shape=out_shape,
        scratch_shapes=[pltpu.VMEM((2 * hb, n, L), F32),
                        pltpu.VMEM((hb, 5, nc, 8, L), F32),
                        pltpu.VMEM((2 * hb, L, 2 * L), F32)],
        compiler_params=_cparams(("arbitrary", "arbitrary")),
        name="mlstm_init" if init is not None else "mlstm",
    )(*args)


def _router(logits):
    lane = _lane(logits.shape).astype(F32)
    big = 1e9
    gl = jnp.where(lane < N_GROUPS, logits, -jnp.inf)
    gmax = gl.max(axis=-1, keepdims=True)
    g_sel = jnp.where(gl == gmax, lane, big).min(axis=-1, keepdims=True)
    g_prob = 1.0 / jnp.exp(gl - gmax).sum(axis=-1, keepdims=True)
    lo = N_GROUPS + EXPERTS_PER_GROUP * g_sel
    el = jnp.where(lane >= lo, jnp.where(lane < lo + EXPERTS_PER_GROUP, logits, -jnp.inf), -jnp.inf)
    v1 = el.max(axis=-1, keepdims=True)
    i1 = jnp.where(el == v1, lane, big).min(axis=-1, keepdims=True)
    el2 = jnp.where(lane == i1, -jnp.inf, el)
    v2 = el2.max(axis=-1, keepdims=True)
    i2 = jnp.where(el2 == v2, lane, big).min(axis=-1, keepdims=True)
    e2 = jnp.exp(v2 - v1)
    w1 = g_prob / (1.0 + e2)
    w2 = g_prob * e2 / (1.0 + e2)
    return i1, i2, w1, w2


def _read_tokens(refs, is_prompt, rows):
    if len(refs) == 1:
        return refs[0][rows, :]
    return jnp.where(is_prompt, refs[0][rows, :], refs[1][rows, :])


def _post_kernel(*refs, groups):
    it = iter(refs)
    tok_refs = [[next(it) for _ in range(n)] for n in groups]
    w_refs = [next(it) for _ in range(len(groups) - 1)]
    mod_ref, g_ref, wr_ref, br_ref, ls_ref, sel_ref = [next(it) for _ in range(6)]
    xnew_ref, xn_ref, rt_ref, rtt_ref, cnt_ref = [next(it) for _ in range(5)]
    run_ref = next(it)
    i = pl.program_id(0)
    tm = xnew_ref.shape[0]
    is_prompt = i * tm < T_PROMPT
    r = _mod_row(i * tm)

    @pl.when(i == 0)
    def _():
        run_ref[...] = jnp.zeros(run_ref.shape, F32)

    sub = ls_ref.shape[0]
    for h in range(tm // sub):
        rows = slice(h * sub, (h + 1) * sub)
        acc = None
        for a_refs, w_ref in zip(tok_refs[1:], w_refs):
            d = _dot(_read_tokens(a_refs, is_prompt, rows).astype(BF16), w_ref[...])
            acc = d if acc is None else acc + d
        xnew = _read_tokens(tok_refs[0], is_prompt, rows) + _mod_part(mod_ref, r, 2) * acc
        xnew_ref[rows, :] = xnew
        xn = _rms(xnew, g_ref[...]) * (1.0 + _mod_part(mod_ref, r, 4)) + _mod_part(mod_ref, r, 3)
        _to_token_tiles(xn_ref, xn, h * sub)
        x_hi, x_lo = _split2(xn)
        both = _dot(x_hi, wr_ref[...])
        logits = both[:, :LANES] + both[:, LANES:] + _dot(x_lo, wr_ref[:, :LANES]) + br_ref[...]
        i1, i2, w1, w2 = _router(logits)

        lane = _lane(logits.shape).astype(F32)
        member = jnp.where(lane == i1, 1.0, jnp.where(lane == i2, 1.0, 0.0))
        before = _dot(ls_ref[...], member.astype(BF16)) + run_ref[...]
        rank1 = jnp.where(lane == i1, before, 0.0).sum(axis=-1, keepdims=True)
        rank2 = jnp.where(lane == i2, before, 0.0).sum(axis=-1, keepdims=True)
        run_ref[...] = run_ref[...] + member.sum(axis=0, keepdims=True)
        cols = (i1 - N_GROUPS, i2 - N_GROUPS, w1, w2, rank1, rank2)
        rt = jnp.zeros(logits.shape, F32)
        for k, c in enumerate(cols):
            rt = jnp.where(lane == k, c, rt)
        rt_ref[rows, :] = rt
        hi, mid, lo = _split3(rt)
        sel = sel_ref[...]
        rtt_ref[:, rows] = _dot_nt(sel, hi) + _dot_nt(sel, mid) + _dot_nt(sel, lo)
    cnt_ref[...] = run_ref[...]


def _post(tok_ops, w_list, mod, g, wr, br):
    t = T_ALL
    tm = POST_BLOCK
    npb = T_PROMPT // tm
    full = lambda a: pl.BlockSpec(a.shape, lambda i: (0,) * a.ndim)
    specs, args, groups = [], [], []
    for op in tok_ops:
        if isinstance(op, tuple):
            w = op[0].shape[1]
            specs += [pl.BlockSpec((tm, w), lambda i: (jnp.minimum(i, npb - 1), 0)),
                      pl.BlockSpec((tm, w), lambda i: (jnp.maximum(i - npb, 0), 0))]
            args += list(op)
            groups.append(2)
        else:
            specs.append(pl.BlockSpec((tm, op.shape[1]), lambda i: (i, 0)))
            args.append(op)
            groups.append(1)
    idx = np.arange(TOK_BLOCK)
    ls = jnp.asarray(idx[:, None] > idx[None, :], BF16)
    sel = jnp.asarray(np.arange(8)[:, None] == np.arange(LANES)[None, :], BF16)
    consts = [mod, g, wr, br, ls, sel]
    return pl.pallas_call(
        functools.partial(_post_kernel, groups=tuple(groups)),
        grid=(t // tm,),
        in_specs=specs + [full(w) for w in w_list] + [full(a) for a in consts],
        out_specs=[pl.BlockSpec((tm, D_MODEL), lambda i: (i, 0)),
                   pl.BlockSpec((tm * SLABS, LANES), lambda i: (i, 0)),
                   pl.BlockSpec((tm, LANES), lambda i: (i, 0)),
                   pl.BlockSpec((8, tm), lambda i: (0, i)),
                   pl.BlockSpec((1, LANES), lambda i: (0, 0))],
        out_shape=[jax.ShapeDtypeStruct((t, D_MODEL), F32),
                   jax.ShapeDtypeStruct((t * SLABS, LANES), U32),
                   jax.ShapeDtypeStruct((t, LANES), F32),
                   jax.ShapeDtypeStruct((8, t), F32),
                   jax.ShapeDtypeStruct((1, LANES), F32)],
        scratch_shapes=[pltpu.VMEM((1, LANES), F32)],
        compiler_params=_cparams(("arbitrary",)),
        name="post_mixer_router",
    )(*args, *w_list, *consts)


def _expert_kernel(be_ref, st_ref, nv_ref, xn_hbm, w1_ref, w3_ref, w2_ref, o_ref, xa, xb, sem, w1b, w3b, w2b):
    i = pl.program_id(0)
    nv = nv_ref[0]
    active = i < nv
    rows = EXPERT_ROWS

    def gather(blk, buf, s):
        for r in range(rows):
            tok = st_ref[blk * rows + r]
            pltpu.make_async_copy(xn_hbm.at[pl.ds(pl.multiple_of(tok * SLABS, SLABS), SLABS), :],
                                  buf.at[pl.ds(r * SLABS, SLABS), :], sem.at[s]).start(priority=r % 2)

    def wait(buf, s):
        pltpu.make_async_copy(xn_hbm.at[pl.ds(0, rows * SLABS), :], buf, sem.at[s]).wait()

    @pl.when(i == 0)
    def _():
        gather(0, xa, 0)

    changed = jnp.logical_or(i == 0, be_ref[i] != be_ref[jnp.maximum(i - 1, 0)])

    @pl.when(jnp.logical_and(changed, active))
    def _():
        w1b[...] = w1_ref[0, 0].astype(BF16)
        w3b[...] = w3_ref[0, 0].astype(BF16)
        w2b[...] = w2_ref[0, 0].astype(BF16)

    nxt = jnp.minimum(i + 1, nv - 1)

    def step(cur, s_cur, oth, s_oth):
        wait(cur, s_cur)
        gather(nxt, oth, s_oth)
        x = _from_token_tiles(cur, 0, rows).astype(BF16)
        h1 = _dot(x, w1b[...])
        h3 = _dot(x, w3b[...])
        hid = (h1 * jax.nn.sigmoid(h1)) * h3
        _to_token_tiles(o_ref, _dot(hid.astype(BF16), w2b[...]))

    @pl.when(jnp.logical_and(active, i % 2 == 0))
    def _():
        step(xa, 0, xb, 1)

    @pl.when(jnp.logical_and(active, i % 2 == 1))
    def _():
        step(xb, 1, xa, 0)

    @pl.when(jnp.logical_not(active))
    def _():
        o_ref[...] = jnp.zeros(o_ref.shape, o_ref.dtype)

    @pl.when(i == nv - 1)
    def _():
        @pl.when(i % 2 == 0)
        def _():
            wait(xb, 1)

        @pl.when(i % 2 == 1)
        def _():
            wait(xa, 0)


def _experts(block_expert, slot_tok, n_valid, xn, w1, w3, w2, layer):
    nblk = block_expert.shape[0]
    rows = EXPERT_ROWS
    grid_spec = pltpu.PrefetchScalarGridSpec(
        num_scalar_prefetch=3,
        grid=(nblk,),
        in_specs=[pl.BlockSpec(memory_space=pl.ANY),
                  pl.BlockSpec((1, 1, D_MODEL, D_EXPERT), lambda i, be, st, nv: (layer, be[i], 0, 0)),
                  pl.BlockSpec((1, 1, D_MODEL, D_EXPERT), lambda i, be, st, nv: (layer, be[i], 0, 0)),
                  pl.BlockSpec((1, 1, D_EXPERT, D_MODEL), lambda i, be, st, nv: (layer, be[i], 0, 0))],
        out_specs=pl.BlockSpec((rows * SLABS, LANES), lambda i, be, st, nv: (i, 0)),
        scratch_shapes=[pltpu.VMEM((rows * SLABS, LANES), U32),
                        pltpu.VMEM((rows * SLABS, LANES), U32),
                        pltpu.SemaphoreType.DMA((2,)),
                        pltpu.VMEM((D_MODEL, D_EXPERT), BF16),
                        pltpu.VMEM((D_MODEL, D_EXPERT), BF16),
                        pltpu.VMEM((D_EXPERT, D_MODEL), BF16)])
    return pl.pallas_call(
        _expert_kernel,
        grid_spec=grid_spec,
        out_shape=jax.ShapeDtypeStruct((nblk * rows * SLABS, LANES), U32),
        compiler_params=_cparams(("arbitrary",)),
        name="moe_experts",
    )(block_expert, slot_tok, n_valid, xn, w1, w3, w2)


def _combine_kernel(*refs, row0, project, final_norm):
    it = iter(refs)
    dest_ref, yb_hbm, x_ref, rt_ref, mod0_ref = [next(it) for _ in range(5)]
    if project:
        mod1_ref, g_ref, w_ref = [next(it) for _ in range(3)]
    if final_norm:
        gfin_ref = next(it)
    xo_ref = next(it)
    if project:
        q_ref, k_ref, v_ref = [next(it) for _ in range(3)]
    ya, yb, sem = [next(it) for _ in range(3)]
    i = pl.program_id(0)
    nblk = pl.num_programs(0)
    tm = x_ref.shape[0]

    def gather(blk, buf, s):
        for j in range(tm):
            for c in range(2):
                d = dest_ref[c * T_ALL + row0 + blk * tm + j]
                pltpu.make_async_copy(yb_hbm.at[pl.ds(pl.multiple_of(d * SLABS, SLABS), SLABS), :],
                                      buf.at[pl.ds((c * tm + j) * SLABS, SLABS), :], sem.at[s]).start(priority=c)

    def wait(buf, s):
        pltpu.make_async_copy(yb_hbm.at[pl.ds(0, 2 * tm * SLABS), :], buf, sem.at[s]).wait()

    @pl.when(i == 0)
    def _():
        gather(0, ya, 0)

    nxt = jnp.minimum(i + 1, nblk - 1)

    def step(cur, s_cur, oth, s_oth):
        wait(cur, s_cur)
        gather(nxt, oth, s_oth)
        r = _mod_row(row0 + i * tm)
        rt = rt_ref[...]
        y = rt[:, 2:3] * _from_token_tiles(cur, 0, tm) + rt[:, 3:4] * _from_token_tiles(cur, tm, tm)
        x = x_ref[...] + _mod_part(mod0_ref, r, 5) * y
        xo_ref[...] = _rms(x, gfin_ref[...]) if final_norm else x
        if project:
            h = _rms(x, g_ref[...]) * (1.0 + _mod_part(mod1_ref, r, 1)) + _mod_part(mod1_ref, r, 0)
            hb = h.astype(BF16)
            for j, o_ref in enumerate((q_ref, k_ref, v_ref)):
                o_ref[...] = _dot(hb, w_ref[:, j * C_WIDTH:(j + 1) * C_WIDTH]).astype(o_ref.dtype)

    @pl.when(i % 2 == 0)
    def _():
        step(ya, 0, yb, 1)

    @pl.when(i % 2 == 1)
    def _():
        step(yb, 1, ya, 0)

    @pl.when(i == nblk - 1)
    def _():
        @pl.when(i % 2 == 0)
        def _():
            wait(yb, 1)

        @pl.when(i % 2 == 1)
        def _():
            wait(ya, 0)


def _combine(dest, yb, x, rt, mod0, row0, t, proj=None, gfin=None):
    tm = TOK_BLOCK
    blk0 = row0 // tm
    full = lambda a: pl.BlockSpec(a.shape, lambda i, d: (0,) * a.ndim)
    extra = list(proj) if proj is not None else []
    if gfin is not None:
        extra.append(gfin)
    n_out = 4 if proj is not None else 1
    out_shape = [jax.ShapeDtypeStruct((t, D_MODEL), F32)] + [jax.ShapeDtypeStruct((t, C_WIDTH), BF16)] * (n_out - 1)
    grid_spec = pltpu.PrefetchScalarGridSpec(
        num_scalar_prefetch=1,
        grid=(t // tm,),
        in_specs=[pl.BlockSpec(memory_space=pl.ANY),
                  pl.BlockSpec((tm, D_MODEL), lambda i, d: (blk0 + i, 0)),
                  pl.BlockSpec((tm, LANES), lambda i, d: (blk0 + i, 0)),
                  full(mod0)] + [full(a) for a in extra],
        out_specs=[pl.BlockSpec((tm, D_MODEL), lambda i, d: (i, 0))] * n_out,
        scratch_shapes=[pltpu.VMEM((2 * tm * SLABS, LANES), U32),
                        pltpu.VMEM((2 * tm * SLABS, LANES), U32),
                        pltpu.SemaphoreType.DMA((2,))])
    return pl.pallas_call(
        functools.partial(_combine_kernel, row0=row0, project=proj is not None, final_norm=gfin is not None),
        grid_spec=grid_spec,
        out_shape=out_shape,
        compiler_params=_cparams(("arbitrary",)),
        name="moe_combine_proj" if proj is not None else "moe_combine",
    )(dest, yb, x, rt, mod0, *extra)


def _moe_plan(rtt, cnt):
    t = rtt.shape[1]
    eid = rtt[0:2].astype(jnp.int32)
    rank = rtt[4:6].astype(jnp.int32)
    counts = cnt[0, N_GROUPS:N_GROUPS + N_EXPERTS].astype(jnp.int32)
    padded = (counts + EXPERT_ROWS - 1) // EXPERT_ROWS * EXPERT_ROWS
    seg_end = jnp.cumsum(padded)
    seg_start = seg_end - padded
    experts = jnp.arange(N_EXPERTS, dtype=jnp.int32)
    start = jnp.sum(jnp.where(eid[..., None] == experts, seg_start, 0), axis=-1)
    dest = (start + rank).reshape(-1)
    n_blocks = (2 * t + N_EXPERTS * (EXPERT_ROWS - 1) + EXPERT_ROWS - 1) // EXPERT_ROWS
    tok = jnp.tile(jnp.arange(t, dtype=jnp.int32), 2)
    slot_tok = (jnp.arange(n_blocks * EXPERT_ROWS, dtype=jnp.int32) % t).at[dest].set(
        tok, unique_indices=True, mode='promise_in_bounds')
    first_row = jnp.arange(n_blocks, dtype=jnp.int32) * EXPERT_ROWS
    block_expert = jnp.minimum(jnp.sum((seg_end[None, :] <= first_row[:, None]).astype(jnp.int32), axis=1),
                               N_EXPERTS - 1)
    n_valid = (seg_end[-1:] // EXPERT_ROWS).astype(jnp.int32)
    return dest, slot_tok, block_expert, n_valid


def _attn_c_prompt_kernel(q_ref, k_ref, v_ref, o_ref):
    for p in range(C_HEADS // 2):
        sl = slice(p * LANES, (p + 1) * LANES)
        kb = k_ref[:, sl].astype(BF16)
        vb = _with_ones(v_ref[:, sl])
        qp = q_ref[:, sl].astype(F32) * (C_HEAD_DIM ** -0.5)
        o_ref[:, sl] = _softmax_pair(qp, lambda qm: [_dot_nt(qm, kb)],
                                     lambda ps: _dot(ps[0], vb)).astype(o_ref.dtype)


def _attn_c_prompt(q, k, v):
    blk = pl.BlockSpec((SEQ, C_WIDTH), lambda b: (b, 0))
    return pl.pallas_call(
        _attn_c_prompt_kernel,
        grid=(BATCH,),
        in_specs=[blk, blk, blk],
        out_specs=blk,
        out_shape=jax.ShapeDtypeStruct((T_PROMPT, C_WIDTH), BF16),
        compiler_params=_cparams(("arbitrary",)),
        name="attn_c_prompt",
    )(q, k, v)


def _na_key_start(r0):
    rows = DEC_SEQ // GRID_W
    return jnp.minimum(jnp.clip(r0 - NA_ROWS // 2, 0, rows - NA_ROWS), rows - NA_KROWS)


def _na_block_plan():
    rows = DEC_SEQ // GRID_W
    nblk = rows // NA_QROWS
    plan = []
    for blk in (0, 1, nblk - 1):
        r0 = blk * NA_QROWS
        ks = min(int(np.clip(r0 - NA_ROWS // 2, 0, rows - NA_ROWS)), rows - NA_KROWS)
        per_row = []
        for i in range(NA_QROWS):
            r = r0 + i
            rs = int(np.clip(r - NA_ROWS // 2, 0, rows - NA_ROWS))
            start = ks - r + NA_ROWS - 1 + NA_KROWS
            ok = [rs <= ks + j < rs + NA_ROWS for j in range(NA_KROWS)]
            per_row.append((start, ok))
        plan.append(per_row)
    return plan


def _attn_c_sample_kernel(q_ref, k_ref, v_ref, ck_ref, cv_ref, toe_ref, o_ref, bias_ref):
    rows = DEC_SEQ // GRID_W
    nblk = rows // NA_QROWS
    w = GRID_W

    @pl.when(jnp.logical_and(pl.program_id(1) == 0, pl.program_id(2) == 0))
    def _():
        neg = jnp.full((w, w), NEG, F32)
        for t, per_row in enumerate(_na_block_plan()):
            for half in range(2):
                for i, (start, ok) in enumerate(per_row):
                    for j in range(0, NA_KROWS, 2):
                        pieces = [toe_ref[0, half, start + jj] if ok[jj] else neg for jj in (j, j + 1)]
                        bias_ref[t, half, i * w:(i + 1) * w, j * w:(j + 2) * w] = jnp.concatenate(pieces, axis=1)

    i = pl.program_id(2)
    r0 = i * NA_QROWS
    k0 = pl.multiple_of(_na_key_start(r0) * GRID_W, GRID_W)
    btype = jnp.where(i == 0, 0, jnp.where(i == nblk - 1, 2, 1))
    nk = NA_KROWS * GRID_W
    kw = k_ref[pl.ds(k0, nk), :].astype(BF16)
    vw = v_ref[pl.ds(k0, nk), :].astype(BF16)
    kc = ck_ref[0].astype(BF16)
    vc = cv_ref[0].astype(BF16)
    qp = q_ref[...].astype(F32) * (C_HEAD_DIM ** -0.5)
    lo = _lane(qp.shape) < 64
    outs = []
    for half in range(2):
        qm = jnp.where(lo if half == 0 else jnp.logical_not(lo), qp, 0.0).astype(BF16)
        s_win = _dot_nt(qm, kw) + bias_ref[btype, half]
        s_ctx = _dot_nt(qm, kc)
        m = jnp.maximum(s_win.max(axis=-1, keepdims=True), s_ctx.max(axis=-1, keepdims=True))
        e_win = jnp.exp(s_win - m)
        e_ctx = jnp.exp(s_ctx - m)
        l = e_win.sum(axis=-1, keepdims=True) + e_ctx.sum(axis=-1, keepdims=True)
        o = _dot(e_win.astype(BF16), vw) + _dot(e_ctx.astype(BF16), vc)
        outs.append(o / l)
    o_ref[...] = jnp.where(lo, outs[0], outs[1]).astype(o_ref.dtype)


def _attn_c_sample(q, k, v, ck, cv, toe):
    rows = DEC_SEQ // GRID_W
    nblk = rows // NA_QROWS
    qrows = NA_QROWS * GRID_W
    npair = C_HEADS // 2
    return pl.pallas_call(
        _attn_c_sample_kernel,
        grid=(npair, DEC_BATCH, nblk),
        in_specs=[pl.BlockSpec((qrows, LANES), lambda p, b, i: (b * nblk + i, p)),
                  pl.BlockSpec((DEC_SEQ, LANES), lambda p, b, i: (b, p)),
                  pl.BlockSpec((DEC_SEQ, LANES), lambda p, b, i: (b, p)),
                  pl.BlockSpec((1, PAST_LEN, LANES), lambda p, b, i: (b, 0, p)),
                  pl.BlockSpec((1, PAST_LEN, LANES), lambda p, b, i: (b, 0, p)),
                  pl.BlockSpec((1,) + toe.shape[1:], lambda p, b, i: (p, 0, 0, 0, 0))],
        out_specs=pl.BlockSpec((qrows, LANES), lambda p, b, i: (b * nblk + i, p)),
        out_shape=jax.ShapeDtypeStruct((T_SAMPLE, C_WIDTH), BF16),
        scratch_shapes=[pltpu.VMEM((3, 2, qrows, NA_KROWS * GRID_W), F32)],
        compiler_params=_cparams(("arbitrary", "arbitrary", "arbitrary")),
        name="attn_c_sample",
    )(q, k, v, ck, cv, toe)


def _na_toeplitz(rpb):
    w = GRID_W
    nd_r, nd_c = 2 * NA_ROWS - 1, 2 * NA_COLS - 1
    c = np.arange(w)
    cs = np.clip(c - NA_COLS // 2, 0, w - NA_COLS)
    col_ok = (c[None, :] >= cs[:, None]) & (c[None, :] < cs[:, None] + NA_COLS)
    dcol = c[None, :] - c[:, None] + NA_COLS - 1
    onehot = (np.arange(nd_c)[:, None, None] == dcol[None]).reshape(nd_c, w * w)
    toe = jnp.dot(rpb.reshape(C_HEADS * nd_r, nd_c), jnp.asarray(onehot, F32), precision=lax.Precision.HIGHEST)
    toe = jnp.where(col_ok[None, None], toe.reshape(C_HEADS, nd_r, w, w), NEG)
    toe = jnp.pad(toe, ((0, 0), (NA_KROWS, NA_KROWS), (0, 0), (0, 0)), constant_values=NEG)
    return toe.reshape(C_HEADS // 2, 2, nd_r + 2 * NA_KROWS, w, w)


def _rope_tables():
    half = A_HEAD_DIM // 2
    t = jnp.arange(DEC_SEQ)
    row = (t // GRID_W).astype(F32)
    colp = (t % GRID_W).astype(F32)
    freqs = 1.0 / (ROPE_BASE ** (jnp.arange(0, half, 2, dtype=F32) / half))
    d = np.arange(LANES) % A_HEAD_DIM
    pos = jnp.where(jnp.asarray(d < half)[None, :], row[:, None], colp[:, None])
    ang = pos * freqs[d % (half // 2)][None, :]
    sign = jnp.asarray(np.where((d % half) < half // 2, -1.0, 1.0), F32)[None, :]
    return jnp.cos(ang), jnp.sin(ang) * sign


def _head_avg_matrix():
    idx = np.arange(LANES) // A_HEAD_DIM
    return jnp.asarray((idx[:, None] == idx[None, :]).astype(np.float32) / A_HEAD_DIM, BF16)


def _tri_matrices():
    i = np.arange(MLSTM_CHUNK)
    upper = (i[:, None] <= i[None, :]).astype(np.float32)
    lower = (i[:, None] >= i[None, :]).astype(np.float32)
    return jnp.asarray(np.stack([upper, lower]), BF16)


def _router_weights(wg, bg, we, be):
    w = jnp.zeros((D_MODEL, LANES), F32).at[:, :N_GROUPS].set(wg).at[:, N_GROUPS:N_GROUPS + N_EXPERTS].set(we)
    b = jnp.zeros((1, LANES), F32).at[0, :N_GROUPS].set(bg).at[0, N_GROUPS:N_GROUPS + N_EXPERTS].set(be)
    hi = w.astype(BF16)
    lo = (w - hi.astype(F32)).astype(BF16)
    return jnp.concatenate([hi, lo], axis=1), b


def _moe(xn, rtt, cnt, w1, w3, w2, layer):
    dest, slot_tok, block_expert, n_valid = _moe_plan(rtt, cnt)
    return dest, _experts(block_expert, slot_tok, n_valid, xn, w1, w3, w2, layer)


def kernel(x_prompt, x_sample, cache_attn_k, cache_attn_v, state_mlstm_C, state_mlstm_n, state_mlstm_m,
           cache_na_k, cache_na_v, c, c_ctx, norm_mix, norm_ffn, norm_final, ada_w, ada_b,
           ab_w_in, ab_w_out, ab_q_norm, ab_k_norm, ab_gate_bias, ab_out_norm,
           na_w_in, na_w_out, na_rpb, moe_wg, moe_bg, moe_we, moe_be, moe_w1, moe_w3, moe_w2):
    xp = x_prompt.reshape(T_PROMPT, D_MODEL)
    xs = x_sample.reshape(T_SAMPLE, D_MODEL)
    cond =jnp.zeros((N_COND, D_MODEL), F32).at[0].set(c_ctx).at[1:1 + DEC_BATCH].set(c)
    mod = _modulation(cond, ada_w, ada_b)
    gfin = norm_final.reshape(1, D_MODEL)

    w_in = ab_w_in[0]
    o_aq, o_ak, o_av, o_bq, o_bk, o_bv, o_bo, o_bg = np.cumsum((0,) + (A_WIDTH, A_KV_WIDTH, A_KV_WIDTH,
                                                                       B_WIDTH, B_WIDTH, B_WIDTH, B_WIDTH))
    wb = jnp.concatenate([w_in[:, o_bq:o_bk], w_in[:, o_bv:o_bg]], axis=1).astype(BF16)
    wq = w_in[:, o_aq:o_ak].astype(BF16)
    wkv = w_in[:, o_ak:o_bq].astype(BF16)
    gate_rows = np.array([0, 8, 1, 9])
    wg = w_in[:, o_bg:o_bg + 4 * B_HEADS].reshape(D_MODEL, 4, B_HEADS)
    wgt = jnp.zeros((B_HEADS, 16, D_MODEL), F32).at[:, gate_rows, :].set(wg.transpose(2, 1, 0))
    wgt = jnp.concatenate([w_in[:, o_bk:o_bv].T, wgt.reshape(16 * B_HEADS, D_MODEL)], axis=0).astype(BF16)
    g_mix = norm_mix[0].reshape(1, D_MODEL)
    ob_p, oq_p, okv_p, okt_p, ogt_p = _proj_ab(xp, mod[0], g_mix, wb, wq, wkv, wgt, 0)
    ob_s, oq_s, okv_s, okt_s, ogt_s = _proj_ab(xs, mod[0], g_mix, wb, wq, wkv, wgt, T_PROMPT)

    qn = jnp.tile(ab_q_norm[0], 2).reshape(1, LANES)
    kn = jnp.tile(ab_k_norm[0], 2).reshape(1, LANES)
    bd = _head_avg_matrix()
    cos, sin = _rope_tables()
    a_p, new_k, new_v = _attn_a_prompt(oq_p, okv_p, qn, kn, bd)
    ck = cache_attn_k[:, 0].reshape(DEC_BATCH, PAST_LEN, A_KV_WIDTH)
    cv = cache_attn_v[:, 0].reshape(DEC_BATCH, PAST_LEN, A_KV_WIDTH)
    a_s = _attn_a_sample(oq_s, okv_s, ck, cv, cos, sin, qn, kn, bd)

    gb = ab_gate_bias[0]
    brow = jnp.zeros((B_HEADS, 16, LANES), F32).at[:, gate_rows, :].set(
        jnp.broadcast_to(gb.T[:, :, None], (B_HEADS, 4, LANES)))
    onorm = ab_out_norm[0].reshape(B_HEADS, 1, B_HEAD_DIM)
    tri = _tri_matrices()
    b_p, cT, nT, mT = _mlstm(ob_p, okt_p, ogt_p, brow, onorm, tri, n=SEQ, nseq=BATCH, emit_state=True)
    n0 = jnp.broadcast_to(state_mlstm_n[:, 0][..., None], state_mlstm_C[:, 0].shape)
    c0 = jnp.concatenate([state_mlstm_C[:, 0], n0], axis=-1)
    m0 = jnp.broadcast_to(state_mlstm_m[:, 0].transpose(0, 2, 1)[..., None], (DEC_BATCH, B_HEADS, 2, LANES))
    (b_s,) = _mlstm(ob_s, okt_s, ogt_s, brow, onorm, tri, n=DEC_SEQ, nseq=DEC_BATCH, init=(c0, m0))

    w_out = ab_w_out[0].astype(BF16)
    wr, br = _router_weights(moe_wg[0], moe_bg[0], moe_we[0], moe_be[0])
    x1, xn, rt, rtt, cnt = _post([(xp, xs), (a_p, a_s), (b_p, b_s)], [w_out[:A_WIDTH], w_out[A_WIDTH:]], mod[0],
                                 norm_ffn[0].reshape(1, D_MODEL), wr, br)
    dest, yb = _moe(xn, rtt, cnt, moe_w1, moe_w3, moe_w2, 0)

    g_mix = norm_mix[1].reshape(1, D_MODEL)
    w_in = na_w_in[0].astype(BF16)
    proj = (mod[1], g_mix, w_in)
    x_p, q_p, k_p, v_p = _combine(dest, yb, x1, rt, mod[0], 0, T_PROMPT, proj=proj)
    x_s, q_s, k_s, v_s = _combine(dest, yb, x1, rt, mod[0], T_PROMPT, T_SAMPLE, proj=proj)
    o_p = _attn_c_prompt(q_p, k_p, v_p)
    nck = cache_na_k[:, 0].reshape(DEC_BATCH, PAST_LEN, C_WIDTH)
    ncv = cache_na_v[:, 0].reshape(DEC_BATCH, PAST_LEN, C_WIDTH)
    o_s = _attn_c_sample(q_s, k_s, v_s, nck, ncv, _na_toeplitz(na_rpb[0]))
    wr, br = _router_weights(moe_wg[1], moe_bg[1], moe_we[1], moe_be[1])
    x1, xn, rt, rtt, cnt = _post([(x_p, x_s), (o_p, o_s)], [na_w_out[0].astype(BF16)], mod[1],
                                 norm_ffn[1].reshape(1, D_MODEL), wr, br)
    dest, yb = _moe(xn, rtt, cnt, moe_w1, moe_w3, moe_w2, 1)
    (y_prompt,) = _combine(dest, yb, x1, rt, mod[1], 0, T_PROMPT, gfin=gfin)
    (y_sample,) = _combine(dest, yb, x1, rt, mod[1], T_PROMPT, T_SAMPLE, gfin=gfin)
    y_prompt = y_prompt.reshape(BATCH, SEQ, D_MODEL)
    y_sample = y_sample.reshape(DEC_BATCH, DEC_SEQ, D_MODEL)
    new_attn_k = new_k.reshape(BATCH, 1, SEQ, A_KV_HEADS, A_HEAD_DIM)
    new_attn_v = new_v.reshape(BATCH, 1, SEQ, A_KV_HEADS, A_HEAD_DIM)
    new_mlstm_C = cT[:, None]
    new_mlstm_n = nT.transpose(0, 2, 1, 3)[:, None]
    new_mlstm_m = mT[..., 0].transpose(0, 2, 1)[:, None]
    new_na_k = k_p.astype(F32).reshape(BATCH, 1, SEQ, C_HEADS, C_HEAD_DIM)
    new_na_v = v_p.astype(F32).reshape(BATCH, 1, SEQ, C_HEADS, C_HEAD_DIM)
    return (y_prompt, y_sample, new_attn_k, new_attn_v, new_mlstm_C, new_mlstm_n, new_mlstm_m,
            new_na_k, new_na_v)
```

```python
import functools

import numpy as np
import jax
import jax.numpy as jnp
from jax import lax
from jax.experimental import pallas as pl
from jax.experimental.pallas import tpu as pltpu

F32 = jnp.float32
BF16 = jnp.bfloat16

D_MODEL = 1024
BATCH = 32
SEQ = 256
DEC_BATCH = 4
DEC_SEQ = 2048
PAST_LEN = 256
GRID_W = 64
A_HEADS = 8
A_KV_HEADS = 2
A_HEAD_DIM = 64
A_WIDTH = A_HEADS * A_HEAD_DIM
A_KV_WIDTH = A_KV_HEADS * A_HEAD_DIM
B_HEADS = 4
B_HEAD_DIM = 128
B_WIDTH = B_HEADS * B_HEAD_DIM
MLSTM_CHUNK = 128
C_HEADS = 16
C_HEAD_DIM = 64
C_WIDTH = C_HEADS * C_HEAD_DIM
NA_ROWS = 8
NA_COLS = 16
N_GROUPS = 4
EXPERTS_PER_GROUP = 8
N_EXPERTS = N_GROUPS * EXPERTS_PER_GROUP
D_EXPERT = 512
MOE_BLOCK = 128
ROPE_BASE = 10000.0
NORM_EPS = 1e-6

T_PROMPT = BATCH * SEQ
T_SAMPLE = DEC_BATCH * DEC_SEQ
T_ALL = T_PROMPT + T_SAMPLE
N_COND = 8
LANES = 128
SLABS = D_MODEL // LANES // 2
U32 = jnp.uint32
TOK_BLOCK = 256
POST_BLOCK = 2 * TOK_BLOCK
MLSTM_HEADS_PER_STEP = 2
EXPERT_ROWS = 512
NA_QROWS = 4
NA_KROWS = 12
NEG = -1e30
VMEM_LIMIT = 56 * 1024 * 1024


def _cparams(sem):
    return pltpu.CompilerParams(dimension_semantics=sem, vmem_limit_bytes=VMEM_LIMIT)


def _split2(x):
    hi = x.astype(BF16)
    lo = (x - hi.astype(F32)).astype(BF16)
    return hi, lo


def _split3(x):
    hi = x.astype(BF16)
    r = x - hi.astype(F32)
    mid = r.astype(BF16)
    lo = (r - mid.astype(F32)).astype(BF16)
    return hi, mid, lo


def _dot(a, b):
    return jnp.dot(a, b, preferred_element_type=F32)


def _dot_nt(a, b):
    return lax.dot_general(a, b, (((1,), (1,)), ((), ())), preferred_element_type=F32)


def _dot_exact_rhs(x, b):
    hi, mid, lo = _split3(x)
    return _dot(hi, b) + _dot(mid, b) + _dot(lo, b)


def _rms(x, g):
    ms = jnp.mean(x * x, axis=-1, keepdims=True)
    return (x * lax.rsqrt(ms + NORM_EPS)) * g


def _mod_row(tok0):
    return jnp.where(tok0 < T_PROMPT, 0, 1 + (tok0 - T_PROMPT) // DEC_SEQ)


def _mod_part(mod_ref, r, idx):
    return mod_ref[pl.ds(r, 1), idx * D_MODEL:(idx + 1) * D_MODEL]


def _head_rms(x, w, bd):
    hi, lo = _split2(x * x)
    ms = _dot(hi, bd) + _dot(lo, bd)
    return (x * lax.rsqrt(ms + NORM_EPS)) * w


def _to_token_tiles(ref, x, tile0=0):
    m = x.shape[0]
    bits = lambda v: pltpu.bitcast(v.astype(BF16).astype(F32), U32)
    for s in range(SLABS):
        lo = bits(x[:, s * LANES:(s + 1) * LANES]) >> 16
        hi = bits(x[:, (s + SLABS) * LANES:(s + SLABS + 1) * LANES]) & U32(0xFFFF0000)
        ref[pl.ds(tile0 * SLABS + s, m, stride=SLABS), :] = lo | hi


def _from_token_tiles(ref, tile0, m):
    lo, hi = [], []
    for s in range(SLABS):
        w = ref[pl.ds(tile0 * SLABS + s, m, stride=SLABS), :]
        lo.append(pltpu.bitcast(w << 16, F32))
        hi.append(pltpu.bitcast(w & U32(0xFFFF0000), F32))
    return jnp.concatenate(lo + hi, axis=1)


def _lane(shape):
    return lax.broadcasted_iota(jnp.int32, shape, len(shape) - 1)


def _dup_half(x, g):
    xr = pltpu.roll(x, 64, 1)
    lo = _lane(x.shape) < 64
    return jnp.where(lo, x, xr) if g == 0 else jnp.where(lo, xr, x)


def _rope(x, cos, sin_signed):
    lane = _lane(x.shape)
    partner = jnp.where((lane % 32) < 16, pltpu.roll(x, LANES - 16, 1), pltpu.roll(x, 16, 1))
    return x * cos + partner * sin_signed


def _with_ones(v):
    return jnp.concatenate([v.astype(BF16), jnp.ones(v.shape, BF16)], axis=1)


def _softmax_pair(qp, score_fn, value_fn):
    lo = _lane(qp.shape) < 64
    outs = []
    for half in range(2):
        qm = jnp.where(lo if half == 0 else jnp.logical_not(lo), qp, 0.0).astype(BF16)
        ss = score_fn(qm)
        m = ss[0].max(axis=-1, keepdims=True)
        for s in ss[1:]:
            m = jnp.maximum(m, s.max(axis=-1, keepdims=True))
        o = value_fn([jnp.exp(s - m).astype(BF16) for s in ss])
        outs.append(o[:, :LANES] / o[:, LANES:])
    return jnp.where(lo, outs[0], outs[1])


def _mod_kernel(cond_ref, w_ref, b_ref, o_ref):
    c = cond_ref[...]
    s = c * jax.nn.sigmoid(c)
    s_hi, s_lo = _split2(s)
    w_hi, w_lo = _split2(w_ref[0])
    o_ref[0] = _dot(s_hi, w_hi) + _dot(s_lo, w_hi) + _dot(s_hi, w_lo) + b_ref[0]


def _modulation(cond, ada_w, ada_b):
    depth, d, n = ada_w.shape
    tn = 1536
    return pl.pallas_call(
        _mod_kernel,
        grid=(depth, n // tn),
        in_specs=[pl.BlockSpec((N_COND, d), lambda l, j: (0, 0)),
                  pl.BlockSpec((1, d, tn), lambda l, j: (l, 0, j)),
                  pl.BlockSpec((1, 1, tn), lambda l, j: (l, 0, j))],
        out_specs=pl.BlockSpec((1, N_COND, tn), lambda l, j: (l, 0, j)),
        out_shape=jax.ShapeDtypeStruct((depth, N_COND, n), F32),
        compiler_params=_cparams(("arbitrary", "arbitrary")),
        name="adaln_modulation",
    )(cond, ada_w, ada_b.reshape(depth, 1, n))


def _norm_mod(x_ref, mod_ref, g_ref, shift_idx, scale_idx, row0):
    r = _mod_row(row0 + pl.program_id(0) * x_ref.shape[0])
    h = _rms(x_ref[...], g_ref[...])
    return h * (1.0 + _mod_part(mod_ref, r, scale_idx)) + _mod_part(mod_ref, r, shift_idx)


def _proj_ab_kernel(x_ref, mod_ref, g_ref, wb_ref, wq_ref, wkv_ref, wgt_ref,
                    ob_ref, oq_ref, okv_ref, okt_ref, ogt_ref, *, row0):
    tm = x_ref.shape[0]
    hb = _norm_mod(x_ref, mod_ref, g_ref, 0, 1, row0).astype(BF16)
    ob_ref[...] = _dot(hb, wb_ref[...]).astype(ob_ref.dtype)
    oq_ref[...] = _dot(hb, wq_ref[...]).astype(oq_ref.dtype)
    okv_ref[...] = _dot(hb, wkv_ref[...]).astype(okv_ref.dtype)
    gt = _dot_nt(wgt_ref[...], hb)
    for j in range(tm // LANES):
        okt_ref[j] = gt[:B_WIDTH, j * LANES:(j + 1) * LANES].astype(okt_ref.dtype)
        ogt_ref[j] = gt[B_WIDTH:, j * LANES:(j + 1) * LANES]


def _proj_ab(x, mod, g, wb, wq, wkv, wgt, row0):
    t = x.shape[0]
    tm = TOK_BLOCK
    full = lambda a: pl.BlockSpec(a.shape, lambda i: (0,) * a.ndim)
    return pl.pallas_call(
        functools.partial(_proj_ab_kernel, row0=row0),
        grid=(t // tm,),
        in_specs=[pl.BlockSpec((tm, D_MODEL), lambda i: (i, 0)), full(mod), full(g),
                  full(wb), full(wq), full(wkv), full(wgt)],
        out_specs=[pl.BlockSpec((tm, wb.shape[1]), lambda i: (i, 0)),
                   pl.BlockSpec((tm, A_WIDTH), lambda i: (i, 0)),
                   pl.BlockSpec((tm, 2 * A_KV_WIDTH), lambda i: (i, 0)),
                   pl.BlockSpec((tm // LANES, B_WIDTH, LANES), lambda i: (i, 0, 0)),
                   pl.BlockSpec((tm // LANES, wgt.shape[0] - B_WIDTH, LANES), lambda i: (i, 0, 0))],
        out_shape=[jax.ShapeDtypeStruct((t, wb.shape[1]), BF16),
                   jax.ShapeDtypeStruct((t, A_WIDTH), BF16),
                   jax.ShapeDtypeStruct((t, 2 * A_KV_WIDTH), BF16),
                   jax.ShapeDtypeStruct((t // LANES, B_WIDTH, LANES), BF16),
                   jax.ShapeDtypeStruct((t // LANES, wgt.shape[0] - B_WIDTH, LANES), F32)],
        compiler_params=_cparams(("arbitrary",)),
        name="proj_ab",
    )(x, mod, g, wb, wq, wkv, wgt)


def _gqa_block(q_ref, qn, bd, kd_ref, vd_ref, o_ref, rope=None):
    for p in range(A_HEADS // 2):
        g = p // (A_HEADS // 2 // A_KV_HEADS)
        qp = _head_rms(q_ref[:, p * LANES:(p + 1) * LANES].astype(F32), qn, bd)
        if rope is not None:
            qp = _rope(qp, rope[0], rope[1])
        qp = qp * (A_HEAD_DIM ** -0.5)
        o_ref[:, p * LANES:(p + 1) * LANES] = _softmax_pair(
            qp, lambda qm: [_dot_nt(qm, kd_ref[g])], lambda ps: _dot(ps[0], vd_ref[g])).astype(o_ref.dtype)


def _attn_a_prompt_kernel(q_ref, kv_ref, qn_ref, kn_ref, bd_ref, o_ref, knew_ref, vnew_ref, kd_ref, vd_ref):
    bd = bd_ref[...]
    k = _head_rms(kv_ref[:, :LANES].astype(F32), kn_ref[...], bd)
    v = kv_ref[:, LANES:].astype(F32)
    knew_ref[...] = k
    vnew_ref[...] = v
    for g in range(A_KV_HEADS):
        kd_ref[g] = _dup_half(k, g).astype(BF16)
        vd_ref[g] = _with_ones(_dup_half(v, g))
    _gqa_block(q_ref, qn_ref[...], bd, kd_ref, vd_ref, o_ref)


def _attn_a_prompt(q, kv, qn, kn, bd):
    nb = BATCH
    full = lambda a: pl.BlockSpec(a.shape, lambda b: (0,) * a.ndim)
    return pl.pallas_call(
        _attn_a_prompt_kernel,
        grid=(nb,),
        in_specs=[pl.BlockSpec((SEQ, A_WIDTH), lambda b: (b, 0)),
                  pl.BlockSpec((SEQ, 2 * A_KV_WIDTH), lambda b: (b, 0)),
                  full(qn), full(kn), full(bd)],
        out_specs=[pl.BlockSpec((SEQ, A_WIDTH), lambda b: (b, 0)),
                   pl.BlockSpec((SEQ, A_KV_WIDTH), lambda b: (b, 0)),
                   pl.BlockSpec((SEQ, A_KV_WIDTH), lambda b: (b, 0))],
        out_shape=[jax.ShapeDtypeStruct((T_PROMPT, A_WIDTH), BF16),
                   jax.ShapeDtypeStruct((T_PROMPT, A_KV_WIDTH), F32),
                   jax.ShapeDtypeStruct((T_PROMPT, A_KV_WIDTH), F32)],
        scratch_shapes=[pltpu.VMEM((A_KV_HEADS, SEQ, LANES), BF16),
                        pltpu.VMEM((A_KV_HEADS, SEQ, 2 * LANES), BF16)],
        compiler_params=_cparams(("arbitrary",)),
        name="attn_a_prompt",
    )(q, kv, qn, kn, bd)


_A_QBLOCK = 256


def _attn_a_sample_kernel(q_ref, kv_ref, ck_ref, cv_ref, cos_ref, sin_ref, cosq_ref, sinq_ref,
                          qn_ref, kn_ref, bd_ref, o_ref, kd_ref, vd_ref):
    bd = bd_ref[...]

    @pl.when(pl.program_id(1) == 0)
    def _():
        for g in range(A_KV_HEADS):
            kd_ref[g, :PAST_LEN] = _dup_half(ck_ref[0], g).astype(BF16)
            vd_ref[g, :PAST_LEN] = _with_ones(_dup_half(cv_ref[0], g))
        rows = 256
        for c in range(DEC_SEQ // rows):
            sl = slice(c * rows, (c + 1) * rows)
            k = _head_rms(kv_ref[sl, :LANES].astype(F32), kn_ref[...], bd)
            k = _rope(k, cos_ref[sl, :], sin_ref[sl, :])
            v = kv_ref[sl, LANES:].astype(F32)
            dst = slice(PAST_LEN + c * rows, PAST_LEN + (c + 1) * rows)
            for g in range(A_KV_HEADS):
                kd_ref[g, dst] = _dup_half(k, g).astype(BF16)
                vd_ref[g, dst] = _with_ones(_dup_half(v, g))

    _gqa_block(q_ref, qn_ref[...], bd, kd_ref, vd_ref, o_ref, rope=(cosq_ref[...], sinq_ref[...]))


def _attn_a_sample(q, kv, ck, cv, cos, sin, qn, kn, bd):
    nq = DEC_SEQ // _A_QBLOCK
    full = lambda a: pl.BlockSpec(a.shape, lambda b, i: (0,) * a.ndim)
    tk = PAST_LEN + DEC_SEQ
    return pl.pallas_call(
        _attn_a_sample_kernel,
        grid=(DEC_BATCH, nq),
        in_specs=[pl.BlockSpec((_A_QBLOCK, A_WIDTH), lambda b, i: (b * nq + i, 0)),
                  pl.BlockSpec((DEC_SEQ, 2 * A_KV_WIDTH), lambda b, i: (b, 0)),
                  pl.BlockSpec((1, PAST_LEN, A_KV_WIDTH), lambda b, i: (b, 0, 0)),
                  pl.BlockSpec((1, PAST_LEN, A_KV_WIDTH), lambda b, i: (b, 0, 0)),
                  full(cos), full(sin),
                  pl.BlockSpec((_A_QBLOCK, LANES), lambda b, i: (i, 0)),
                  pl.BlockSpec((_A_QBLOCK, LANES), lambda b, i: (i, 0)),
                  full(qn), full(kn), full(bd)],
        out_specs=pl.BlockSpec((_A_QBLOCK, A_WIDTH), lambda b, i: (b * nq + i, 0)),
        out_shape=jax.ShapeDtypeStruct((T_SAMPLE, A_WIDTH), BF16),
        scratch_shapes=[pltpu.VMEM((A_KV_HEADS, tk, LANES), BF16),
                        pltpu.VMEM((A_KV_HEADS, tk, 2 * LANES), BF16)],
        compiler_params=_cparams(("arbitrary", "arbitrary")),
        name="attn_a_sample",
    )(q, kv, ck, cv, cos, sin, cos, sin, qn, kn, bd)


def _log_sigmoid(x):
    return -(jnp.maximum(-x, 0.0) + jnp.log1p(jnp.exp(-jnp.abs(x))))


def _mlstm_kernel(*refs, n, has_init, emit_state):
    it = iter(refs)
    q_ref, v_ref, og_ref, kt_ref, gt_ref, brow_ref, onorm_ref, tri_ref = [next(it) for _ in range(8)]
    if has_init:
        c0_ref, m0_ref = [next(it) for _ in range(2)]
    out_ref = next(it)
    if emit_state:
        cT_ref, nT_ref, mT_ref = [next(it) for _ in range(3)]
    h_ref, row_ref, c_ref = [next(it) for _ in range(3)]

    L = MLSTM_CHUNK
    D = B_HEAD_DIM
    nc = n // L
    hb = MLSTM_HEADS_PER_STEP

    lane = lax.broadcasted_iota(jnp.int32, (nc * 8, L), 1)
    is_fwd = lax.broadcasted_iota(jnp.int32, (nc * 8, L), 0) % 8 == 0

    def running_max(x, suffix):
        for sh in (1, 2, 4, 8, 16, 32, 64):
            if suffix:
                x = jnp.where(lane < L - sh, jnp.maximum(x, pltpu.roll(x, L - sh, 1)), x)
            else:
                x = jnp.where(lane >= sh, jnp.maximum(x, pltpu.roll(x, sh, 1)), x)
        return x

    for hh in range(hb):
        gt = gt_ref[:, 16 * hh:16 * hh + 16, :] + brow_ref[hh][None]
        li = gt[:, 0:8, :].reshape(nc * 8, L)
        lf = _log_sigmoid(gt[:, 8:16, :]).reshape(nc * 8, L)
        cum = jnp.where(is_fwd, _dot_exact_rhs(lf, tri_ref[0]), _dot_exact_rhs(lf, tri_ref[1]))
        a = li - cum
        planes = (cum, a, jnp.broadcast_to(lf.sum(axis=-1, keepdims=True), (nc * 8, L)),
                  jnp.where(is_fwd, running_max(a, False), running_max(a, True)),
                  jnp.broadcast_to(a.max(axis=-1, keepdims=True), (nc * 8, L)))
        for p, val in enumerate(planes):
            row_ref[hh, p] = val.reshape(nc, 8, L)

    chains = [(hh, d) for hh in range(hb) for d in range(2)]
    if has_init:
        m_init = []
        for hh, d in chains:
            c_ref[2 * hh + d] = c0_ref[0, d, hh]
            m_init.append(m0_ref[0, hh, d:d + 1, :])
        m_init = tuple(m_init)
    else:
        c_ref[...] = jnp.zeros(c_ref.shape, F32)
        m_init = tuple(jnp.zeros((1, L), F32) for _ in chains)

    t_idx = lax.broadcasted_iota(jnp.int32, (L, L), 0)
    s_idx = lax.broadcasted_iota(jnp.int32, (L, L), 1)
    masks = (s_idx <= t_idx, s_idx >= t_idx)
    ones = jnp.ones((L, L), F32)

    def step(c, hh, d, m):
        r0 = pl.multiple_of(c * L, L)
        hl = slice(hh * L, (hh + 1) * L)
        qb = q_ref[pl.ds(r0, L), hl].astype(BF16)
        kst = kt_ref[c, hl, :].astype(F32) * (D ** -0.5)
        v_ext = jnp.concatenate([v_ref[pl.ds(r0, L), hl], ones], axis=1).astype(BF16)
        cum, a_row, tot, amax_run, amax = [row_ref[hh, p, c][d:d + 1, :] for p in range(5)]
        m_cum = jnp.broadcast_to(cum, (L, L)).T
        m_run = jnp.broadcast_to(amax_run, (L, L)).T
        dlog = jnp.where(masks[d], m_cum + a_row, -jnp.inf)
        inter = m_cum + m
        m_t = jnp.maximum(inter, m_cum + m_run)
        w_in = jnp.exp(dlog - m_t)
        w_st = jnp.exp(inter - m_t)
        a = _dot(qb, kst.astype(BF16)) * w_in
        ci = 2 * hh + d
        cext = c_ref[ci]
        p_state = _dot(qb, cext.astype(BF16))
        p_intra = _dot(a.astype(BF16), v_ext)
        num = w_st * p_state[:, :D] + p_intra[:, :D]
        den = w_st * p_state[:, D:] + p_intra[:, D:]
        h_ref[ci, pl.ds(r0, L), :] = num / jnp.maximum(jnp.abs(den), jnp.exp(-m_t))
        m_new = jnp.maximum(tot + m, amax + tot)
        ws = jnp.exp(a_row + tot - m_new)
        wc = jnp.exp(tot + m - m_new)
        c_ref[ci] = jnp.concatenate([wc, wc], axis=1) * cext + _dot((kst * ws).astype(BF16), v_ext)
        return m_new

    def body(i, carry):
        return tuple(step(i if d == 0 else nc - 1 - i, hh, d, m) for (hh, d), m in zip(chains, carry))

    m_fin = lax.fori_loop(0, nc, body, m_init)

    for hh in range(hb):
        hm = h_ref[2 * hh] + h_ref[2 * hh + 1]
        hl = slice(hh * L, (hh + 1) * L)
        gate = jax.nn.sigmoid(og_ref[:, hl].astype(F32))
        out_ref[:, hl] = (_rms(hm, onorm_ref[hh]) * gate).astype(out_ref.dtype)

    if emit_state:
        for k, (hh, d) in enumerate(chains):
            cext = c_ref[2 * hh + d]
            cT_ref[0, d, hh] = cext[:, :D]
            nT_ref[0, hh, d:d + 1, :] = cext[:, D:].T[0:1, :]
            mT_ref[0, hh, d:d + 1, :] = m_fin[k]


def _mlstm(ob, okt, ogt, brow, onorm, tri, *, n, nseq, init=None, emit_state=False):
    L = MLSTM_CHUNK
    nc = n // L
    H = B_HEADS
    hb = MLSTM_HEADS_PER_STEP
    ng = H // hb
    col = lambda part: (lambda b, g: (b, part * ng + g))
    in_specs = [pl.BlockSpec((n, hb * L), col(0)), pl.BlockSpec((n, hb * L), col(1)),
                pl.BlockSpec((n, hb * L), col(2)),
                pl.BlockSpec((nc, hb * L, L), lambda b, g: (b, g, 0)),
                pl.BlockSpec((nc, 16 * hb, L), lambda b, g: (b, g, 0)),
                pl.BlockSpec((hb, 16, L), lambda b, g: (g, 0, 0)),
                pl.BlockSpec((hb, 1, L), lambda b, g: (g, 0, 0)),
                pl.BlockSpec(tri.shape, lambda b, g: (0, 0, 0))]
    args = [ob, ob, ob, okt, ogt, brow, onorm, tri]
    if init is not None:
        in_specs += [pl.BlockSpec((1, 2, hb, L, 2 * L), lambda b, g: (b, 0, g, 0, 0)),
                     pl.BlockSpec((1, hb, 2, L), lambda b, g: (b, g, 0, 0))]
        args += list(init)
    out_specs = [pl.BlockSpec((n, hb * L), lambda b, g: (b, g))]
    out_shape = [jax.ShapeDtypeStruct((nseq * n, B_WIDTH), BF16)]
    if emit_state:
        out_specs += [pl.BlockSpec((1, 2, hb, L, L), lambda b, g: (b, 0, g, 0, 0)),
                      pl.BlockSpec((1, hb, 2, L), lambda b, g: (b, g, 0, 0)),
                      pl.BlockSpec((1, hb, 2, L), lambda b, g: (b, g, 0, 0))]
        out_shape += [jax.ShapeDtypeStruct((nseq, 2, H, L, L), F32),
                      jax.ShapeDtypeStruct((nseq, H, 2, L), F32),
                      jax.ShapeDtypeStruct((nseq, H, 2, L), F32)]
    return pl.pallas_call(
        functools.partial(_mlstm_kernel, n=n, has_init=init is not None, emit_state=emit_state),
        grid=(nseq, ng),
        in_specs=in_specs,
        out_specs=out_specs,
        out_shape=out_shape,
        scratch_shapes=[pltpu.VMEM((2 * hb, n, L), F32),
                        pltpu.VMEM((hb, 5, nc, 8, L), F32),
                        pltpu.VMEM((2 * hb, L, 2 * L), F32)],
        compiler_params=_cparams(("arbitrary", "arbitrary")),
        name="mlstm_init" if init is not None else "mlstm",
    )(*args)


def _router(logits):
    lane = _lane(logits.shape).astype(F32)
    big = 1e9
    gl = jnp.where(lane < N_GROUPS, logits, -jnp.inf)
    gmax = gl.max(axis=-1, keepdims=True)
    g_sel = jnp.where(gl == gmax, lane, big).min(axis=-1, keepdims=True)
    g_prob = 1.0 / jnp.exp(gl - gmax).sum(axis=-1, keepdims=True)
    lo = N_GROUPS + EXPERTS_PER_GROUP * g_sel
    el = jnp.where(lane >= lo, jnp.where(lane < lo + EXPERTS_PER_GROUP, logits, -jnp.inf), -jnp.inf)
    v1 = el.max(axis=-1, keepdims=True)
    i1 = jnp.where(el == v1, lane, big).min(axis=-1, keepdims=True)
    el2 = jnp.where(lane == i1, -jnp.inf, el)
    v2 = el2.max(axis=-1, keepdims=True)
    i2 = jnp.where(el2 == v2, lane, big).min(axis=-1, keepdims=True)
    e2 = jnp.exp(v2 - v1)
    w1 = g_prob / (1.0 + e2)
    w2 = g_prob * e2 / (1.0 + e2)
    return i1, i2, w1, w2


def _read_tokens(refs, is_prompt, rows):
    if len(refs) == 1:
        return refs[0][rows, :]
    return jnp.where(is_prompt, refs[0][rows, :], refs[1][rows, :])


def _post_kernel(*refs, groups):
    it = iter(refs)
    tok_refs = [[next(it) for _ in range(n)] for n in groups]
    w_refs = [next(it) for _ in range(len(groups) - 1)]
    mod_ref, g_ref, wr_ref, br_ref, ls_ref, sel_ref, su_ref, tokrow_ref = [next(it) for _ in range(8)]
    xnew_ref, xn_ref, rt_ref, rtt_ref, cnt_ref, order_ref, tab_ref = [next(it) for _ in range(7)]
    run_ref = next(it)
    i = pl.program_id(0)
    tm = xnew_ref.shape[0]
    is_prompt = i * tm < T_PROMPT
    r = _mod_row(i * tm)

    @pl.when(i == 0)
    def _():
        run_ref[...] = jnp.zeros(run_ref.shape, F32)

    sub = ls_ref.shape[0]
    for h in range(tm // sub):
        rows = slice(h * sub, (h + 1) * sub)
        acc = None
        for a_refs, w_ref in zip(tok_refs[1:], w_refs):
            d = _dot(_read_tokens(a_refs, is_prompt, rows).astype(BF16), w_ref[...])
            acc = d if acc is None else acc + d
        xnew = _read_tokens(tok_refs[0], is_prompt, rows) + _mod_part(mod_ref, r, 2) * acc
        xnew_ref[rows, :] = xnew
        xn = _rms(xnew, g_ref[...]) * (1.0 + _mod_part(mod_ref, r, 4)) + _mod_part(mod_ref, r, 3)
        _to_token_tiles(xn_ref, xn, h * sub)
        x_hi, x_lo = _split2(xn)
        both = _dot(x_hi, wr_ref[...])
        logits = both[:, :LANES] + both[:, LANES:] + _dot(x_lo, wr_ref[:, :LANES]) + br_ref[...]
        i1, i2, w1, w2 = _router(logits)

        lane = _lane(logits.shape).astype(F32)
        member = jnp.where(lane == i1, 1.0, jnp.where(lane == i2, 1.0, 0.0))
        local = _dot(ls_ref[...], member.astype(BF16))
        run = run_ref[...]
        before = local + run
        rank1 = jnp.where(lane == i1, before, 0.0).sum(axis=-1, keepdims=True)
        rank2 = jnp.where(lane == i2, before, 0.0).sum(axis=-1, keepdims=True)
        n_here = member.sum(axis=0, keepdims=True)
        run_ref[...] = run + n_here

        off = _dot(jnp.broadcast_to(n_here, (8, LANES)).astype(BF16), su_ref[...])[0:1]
        pos = off + local
        p1 = jnp.where(lane == i1, pos, 0.0).sum(axis=-1, keepdims=True)
        p2 = jnp.where(lane == i2, pos, 0.0).sum(axis=-1, keepdims=True)
        slot = lax.broadcasted_iota(jnp.int32, (sub, 2 * sub), 1).astype(F32)
        onehot = jnp.where(slot == p1, 1.0, jnp.where(slot == p2, 1.0, 0.0)).astype(BF16)
        order_ref[:, h * 2 * sub:(h + 1) * 2 * sub] = _dot(tokrow_ref[...], onehot) + (i * tm + h * sub).astype(F32)
        row8 = lax.broadcasted_iota(jnp.int32, (8, LANES), 0)
        tab_ref[h] = jnp.where(row8 == 0, n_here, jnp.where(row8 == 1, run, jnp.where(row8 == 2, off, 0.0)))
        cols = (i1 - N_GROUPS, i2 - N_GROUPS, w1, w2, rank1, rank2)
        rt = jnp.zeros(logits.shape, F32)
        for k, c in enumerate(cols):
            rt = jnp.where(lane == k, c, rt)
        rt_ref[rows, :] = rt
        hi, mid, lo = _split3(rt)
        sel = sel_ref[...]
        rtt_ref[:, rows] = _dot_nt(sel, hi) + _dot_nt(sel, mid) + _dot_nt(sel, lo)
    cnt_ref[...] = run_ref[...]


def _post(tok_ops, w_list, mod, g, wr, br):
    t = T_ALL
    tm = POST_BLOCK
    npb = T_PROMPT // tm
    full = lambda a: pl.BlockSpec(a.shape, lambda i: (0,) * a.ndim)
    specs, args, groups = [], [], []
    for op in tok_ops:
        if isinstance(op, tuple):
            w = op[0].shape[1]
            specs += [pl.BlockSpec((tm, w), lambda i: (jnp.minimum(i, npb - 1), 0)),
                      pl.BlockSpec((tm, w), lambda i: (jnp.maximum(i - npb, 0), 0))]
            args += list(op)
            groups.append(2)
        else:
            specs.append(pl.BlockSpec((tm, op.shape[1]), lambda i: (i, 0)))
            args.append(op)
            groups.append(1)
    sub = TOK_BLOCK
    idx = np.arange(sub)
    ls = jnp.asarray(idx[:, None] > idx[None, :], BF16)
    sel = jnp.asarray(np.arange(8)[:, None] == np.arange(LANES)[None, :], BF16)
    lanes = np.arange(LANES)
    su = jnp.asarray(lanes[:, None] < lanes[None, :], BF16)
    tokrow = jnp.asarray(np.where(np.arange(8)[:, None] == 0, idx[None, :], 0), BF16)
    consts = [mod, g, wr, br, ls, sel, su, tokrow]
    return pl.pallas_call(
        functools.partial(_post_kernel, groups=tuple(groups)),
        grid=(t // tm,),
        in_specs=specs + [full(w) for w in w_list] + [full(a) for a in consts],
        out_specs=[pl.BlockSpec((tm, D_MODEL), lambda i: (i, 0)),
                   pl.BlockSpec((tm * SLABS, LANES), lambda i: (i, 0)),
                   pl.BlockSpec((tm, LANES), lambda i: (i, 0)),
                   pl.BlockSpec((8, tm), lambda i: (0, i)),
                   pl.BlockSpec((1, LANES), lambda i: (0, 0)),
                   pl.BlockSpec((8, 2 * tm), lambda i: (0, i)),
                   pl.BlockSpec((tm // sub, 8, LANES), lambda i: (i, 0, 0))],
        out_shape=[jax.ShapeDtypeStruct((t, D_MODEL), F32),
                   jax.ShapeDtypeStruct((t * SLABS, LANES), U32),
                   jax.ShapeDtypeStruct((t, LANES), F32),
                   jax.ShapeDtypeStruct((8, t), F32),
                   jax.ShapeDtypeStruct((1, LANES), F32),
                   jax.ShapeDtypeStruct((8, 2 * t), F32),
                   jax.ShapeDtypeStruct((t // sub, 8, LANES), F32)],
        scratch_shapes=[pltpu.VMEM((1, LANES), F32)],
        compiler_params=_cparams(("arbitrary",)),
        name="post_mixer_router",
    )(*args, *w_list, *consts)


def _expert_kernel(be_ref, st_ref, nv_ref, xn_hbm, w1_ref, w3_ref, w2_ref, o_ref, xa, xb, sem, w1b, w3b, w2b):
    i = pl.program_id(0)
    nv = nv_ref[0]
    active = i < nv
    rows = EXPERT_ROWS

    def gather(blk, buf, s):
        for r in range(rows):
            tok = st_ref[blk * rows + r]
            pltpu.make_async_copy(xn_hbm.at[pl.ds(pl.multiple_of(tok * SLABS, SLABS), SLABS), :],
                                  buf.at[pl.ds(r * SLABS, SLABS), :], sem.at[s]).start(priority=r % 2)

    def wait(buf, s):
        pltpu.make_async_copy(xn_hbm.at[pl.ds(0, rows * SLABS), :], buf, sem.at[s]).wait()

    @pl.when(i == 0)
    def _():
        gather(0, xa, 0)

    changed = jnp.logical_or(i == 0, be_ref[i] != be_ref[jnp.maximum(i - 1, 0)])

    @pl.when(jnp.logical_and(changed, active))
    def _():
        w1b[...] = w1_ref[0, 0].astype(BF16)
        w3b[...] = w3_ref[0, 0].astype(BF16)
        w2b[...] = w2_ref[0, 0].astype(BF16)

    nxt = jnp.minimum(i + 1, nv - 1)

    def step(cur, s_cur, oth, s_oth):
        wait(cur, s_cur)
        gather(nxt, oth, s_oth)
        x = _from_token_tiles(cur, 0, rows).astype(BF16)
        h1 = _dot(x, w1b[...])
        h3 = _dot(x, w3b[...])
        hid = (h1 * jax.nn.sigmoid(h1)) * h3
        _to_token_tiles(o_ref, _dot(hid.astype(BF16), w2b[...]))

    @pl.when(jnp.logical_and(active, i % 2 == 0))
    def _():
        step(xa, 0, xb, 1)

    @pl.when(jnp.logical_and(active, i % 2 == 1))
    def _():
        step(xb, 1, xa, 0)

    @pl.when(jnp.logical_not(active))
    def _():
        o_ref[...] = jnp.zeros(o_ref.shape, o_ref.dtype)

    @pl.when(i == nv - 1)
    def _():
        @pl.when(i % 2 == 0)
        def _():
            wait(xb, 1)

        @pl.when(i % 2 == 1)
        def _():
            wait(xa, 0)


def _experts(block_expert, slot_tok, n_valid, xn, w1, w3, w2, layer):
    nblk = block_expert.shape[0]
    rows = EXPERT_ROWS
    grid_spec = pltpu.PrefetchScalarGridSpec(
        num_scalar_prefetch=3,
        grid=(nblk,),
        in_specs=[pl.BlockSpec(memory_space=pl.ANY),
                  pl.BlockSpec((1, 1, D_MODEL, D_EXPERT), lambda i, be, st, nv: (layer, be[i], 0, 0)),
                  pl.BlockSpec((1, 1, D_MODEL, D_EXPERT), lambda i, be, st, nv: (layer, be[i], 0, 0)),
                  pl.BlockSpec((1, 1, D_EXPERT, D_MODEL), lambda i, be, st, nv: (layer, be[i], 0, 0))],
        out_specs=pl.BlockSpec((rows * SLABS, LANES), lambda i, be, st, nv: (i, 0)),
        scratch_shapes=[pltpu.VMEM((rows * SLABS, LANES), U32),
                        pltpu.VMEM((rows * SLABS, LANES), U32),
                        pltpu.SemaphoreType.DMA((2,)),
                        pltpu.VMEM((D_MODEL, D_EXPERT), BF16),
                        pltpu.VMEM((D_MODEL, D_EXPERT), BF16),
                        pltpu.VMEM((D_EXPERT, D_MODEL), BF16)])
    return pl.pallas_call(
        _expert_kernel,
        grid_spec=grid_spec,
        out_shape=jax.ShapeDtypeStruct((nblk * rows * SLABS, LANES), U32),
        compiler_params=_cparams(("arbitrary",)),
        name="moe_experts",
    )(block_expert, slot_tok, n_valid, xn, w1, w3, w2)


def _combine_kernel(*refs, row0, project, final_norm):
    it = iter(refs)
    dest_ref, yb_hbm, x_ref, rt_ref, mod0_ref = [next(it) for _ in range(5)]
    if project:
        mod1_ref, g_ref, w_ref = [next(it) for _ in range(3)]
    if final_norm:
        gfin_ref = next(it)
    xo_ref = next(it)
    if project:
        q_ref, k_ref, v_ref = [next(it) for _ in range(3)]
    ya, yb, sem = [next(it) for _ in range(3)]
    i = pl.program_id(0)
    nblk = pl.num_programs(0)
    tm = x_ref.shape[0]

    def gather(blk, buf, s):
        for j in range(tm):
            for c in range(2):
                d = dest_ref[c * T_ALL + row0 + blk * tm + j]
                pltpu.make_async_copy(yb_hbm.at[pl.ds(pl.multiple_of(d * SLABS, SLABS), SLABS), :],
                                      buf.at[pl.ds((c * tm + j) * SLABS, SLABS), :], sem.at[s]).start(priority=c)

    def wait(buf, s):
        pltpu.make_async_copy(yb_hbm.at[pl.ds(0, 2 * tm * SLABS), :], buf, sem.at[s]).wait()

    @pl.when(i == 0)
    def _():
        gather(0, ya, 0)

    nxt = jnp.minimum(i + 1, nblk - 1)

    def step(cur, s_cur, oth, s_oth):
        wait(cur, s_cur)
        gather(nxt, oth, s_oth)
        r = _mod_row(row0 + i * tm)
        rt = rt_ref[...]
        y = rt[:, 2:3] * _from_token_tiles(cur, 0, tm) + rt[:, 3:4] * _from_token_tiles(cur, tm, tm)
        x = x_ref[...] + _mod_part(mod0_ref, r, 5) * y
        xo_ref[...] = _rms(x, gfin_ref[...]) if final_norm else x
        if project:
            h = _rms(x, g_ref[...]) * (1.0 + _mod_part(mod1_ref, r, 1)) + _mod_part(mod1_ref, r, 0)
            hb = h.astype(BF16)
            for j, o_ref in enumerate((q_ref, k_ref, v_ref)):
                o_ref[...] = _dot(hb, w_ref[:, j * C_WIDTH:(j + 1) * C_WIDTH]).astype(o_ref.dtype)

    @pl.when(i % 2 == 0)
    def _():
        step(ya, 0, yb, 1)

    @pl.when(i % 2 == 1)
    def _():
        step(yb, 1, ya, 0)

    @pl.when(i == nblk - 1)
    def _():
        @pl.when(i % 2 == 0)
        def _():
            wait(yb, 1)

        @pl.when(i % 2 == 1)
        def _():
            wait(ya, 0)


def _combine(dest, yb, x, rt, mod0, row0, t, proj=None, gfin=None):
    tm = TOK_BLOCK
    blk0 = row0 // tm
    full = lambda a: pl.BlockSpec(a.shape, lambda i, d: (0,) * a.ndim)
    extra = list(proj) if proj is not None else []
    if gfin is not None:
        extra.append(gfin)
    n_out = 4 if proj is not None else 1
    out_shape = [jax.ShapeDtypeStruct((t, D_MODEL), F32)] + [jax.ShapeDtypeStruct((t, C_WIDTH), BF16)] * (n_out - 1)
    grid_spec = pltpu.PrefetchScalarGridSpec(
        num_scalar_prefetch=1,
        grid=(t // tm,),
        in_specs=[pl.BlockSpec(memory_space=pl.ANY),
                  pl.BlockSpec((tm, D_MODEL), lambda i, d: (blk0 + i, 0)),
                  pl.BlockSpec((tm, LANES), lambda i, d: (blk0 + i, 0)),
                  full(mod0)] + [full(a) for a in extra],
        out_specs=[pl.BlockSpec((tm, D_MODEL), lambda i, d: (i, 0))] * n_out,
        scratch_shapes=[pltpu.VMEM((2 * tm * SLABS, LANES), U32),
                        pltpu.VMEM((2 * tm * SLABS, LANES), U32),
                        pltpu.SemaphoreType.DMA((2,))])
    return pl.pallas_call(
        functools.partial(_combine_kernel, row0=row0, project=proj is not None, final_norm=gfin is not None),
        grid_spec=grid_spec,
        out_shape=out_shape,
        compiler_params=_cparams(("arbitrary",)),
        name="moe_combine_proj" if proj is not None else "moe_combine",
    )(dest, yb, x, rt, mod0, *extra)


def _moe_plan(rtt, cnt, order, tab):
    t = rtt.shape[1]
    eid = rtt[0:2].astype(jnp.int32)
    rank = rtt[4:6].astype(jnp.int32)
    counts = cnt[0, N_GROUPS:N_GROUPS + N_EXPERTS].astype(jnp.int32)
    padded = (counts + EXPERT_ROWS - 1) // EXPERT_ROWS * EXPERT_ROWS
    seg_end = jnp.cumsum(padded)
    seg_start = seg_end - padded
    experts = jnp.arange(N_EXPERTS, dtype=jnp.int32)
    start = jnp.sum(jnp.where(eid[..., None] == experts, seg_start, 0), axis=-1)
    dest = (start + rank).reshape(-1)
    n_blocks = (2 * t + N_EXPERTS * (EXPERT_ROWS - 1) + EXPERT_ROWS - 1) // EXPERT_ROWS
    first_row = jnp.arange(n_blocks, dtype=jnp.int32) * EXPERT_ROWS
    block_expert = jnp.minimum(jnp.sum((seg_end[None, :] <= first_row[:, None]).astype(jnp.int32), axis=1),
                               N_EXPERTS - 1)
    n_valid = (seg_end[-1:] // EXPERT_ROWS).astype(jnp.int32)

    nsb = tab.shape[0]
    n_sb = tab[:, 0, N_GROUPS:N_GROUPS + N_EXPERTS].astype(jnp.int32).T[block_expert]
    run_sb = tab[:, 1, N_GROUPS:N_GROUPS + N_EXPERTS].astype(jnp.int32).T[block_expert]
    off_sb = tab[:, 2, N_GROUPS:N_GROUPS + N_EXPERTS].astype(jnp.int32).T[block_expert]
    rank = (first_row - seg_start[block_expert])[:, None] + jnp.arange(EXPERT_ROWS, dtype=jnp.int32)[None, :]
    real = rank < counts[block_expert][:, None]
    sb = jnp.sum(((run_sb + n_sb)[:, None, :] <= rank[:, :, None]).astype(jnp.int32), axis=-1)
    sb = jnp.minimum(sb, nsb - 1)
    pick = sb[:, :, None] == jnp.arange(nsb, dtype=jnp.int32)[None, None, :]
    shift = jnp.sum(jnp.where(pick, (off_sb - run_sb)[:, None, :], 0), axis=-1)
    src = jnp.clip(sb * (2 * TOK_BLOCK) + shift + rank, 0, 2 * t - 1)
    slot = first_row[:, None] + jnp.arange(EXPERT_ROWS, dtype=jnp.int32)[None, :]
    slot_tok = jnp.where(real, order[0].astype(jnp.int32)[src], slot % t).reshape(-1)
    return dest, slot_tok, block_expert, n_valid


def _attn_c_prompt_kernel(q_ref, k_ref, v_ref, o_ref):
    for p in range(C_HEADS // 2):
        sl = slice(p * LANES, (p + 1) * LANES)
        kb = k_ref[:, sl].astype(BF16)
        vb = _with_ones(v_ref[:, sl])
        qp = q_ref[:, sl].astype(F32) * (C_HEAD_DIM ** -0.5)
        o_ref[:, sl] = _softmax_pair(qp, lambda qm: [_dot_nt(qm, kb)],
                                     lambda ps: _dot(ps[0], vb)).astype(o_ref.dtype)


def _attn_c_prompt(q, k, v):
    blk = pl.BlockSpec((SEQ, C_WIDTH), lambda b: (b, 0))
    return pl.pallas_call(
        _attn_c_prompt_kernel,
        grid=(BATCH,),
        in_specs=[blk, blk, blk],
        out_specs=blk,
        out_shape=jax.ShapeDtypeStruct((T_PROMPT, C_WIDTH), BF16),
        compiler_params=_cparams(("arbitrary",)),
        name="attn_c_prompt",
    )(q, k, v)


def _na_key_start(r0):
    rows = DEC_SEQ // GRID_W
    return jnp.minimum(jnp.clip(r0 - NA_ROWS // 2, 0, rows - NA_ROWS), rows - NA_KROWS)


def _na_block_plan():
    rows = DEC_SEQ // GRID_W
    nblk = rows // NA_QROWS
    plan = []
    for blk in (0, 1, nblk - 1):
        r0 = blk * NA_QROWS
        ks = min(int(np.clip(r0 - NA_ROWS // 2, 0, rows - NA_ROWS)), rows - NA_KROWS)
        per_row = []
        for i in range(NA_QROWS):
            r = r0 + i
            rs = int(np.clip(r - NA_ROWS // 2, 0, rows - NA_ROWS))
            start = ks - r + NA_ROWS - 1 + NA_KROWS
            ok = [rs <= ks + j < rs + NA_ROWS for j in range(NA_KROWS)]
            per_row.append((start, ok))
        plan.append(per_row)
    return plan


def _attn_c_sample_kernel(q_ref, k_ref, v_ref, ck_ref, cv_ref, toe_ref, o_ref, bias_ref):
    rows = DEC_SEQ // GRID_W
    nblk = rows // NA_QROWS
    w = GRID_W

    @pl.when(jnp.logical_and(pl.program_id(1) == 0, pl.program_id(2) == 0))
    def _():
        neg = jnp.full((w, w), NEG, F32)
        for t, per_row in enumerate(_na_block_plan()):
            for half in range(2):
                for i, (start, ok) in enumerate(per_row):
                    for j in range(0, NA_KROWS, 2):
                        pieces = [toe_ref[0, half, start + jj] if ok[jj] else neg for jj in (j, j + 1)]
                        bias_ref[t, half, i * w:(i + 1) * w, j * w:(j + 2) * w] = jnp.concatenate(pieces, axis=1)

    i = pl.program_id(2)
    r0 = i * NA_QROWS
    k0 = pl.multiple_of(_na_key_start(r0) * GRID_W, GRID_W)
    btype = jnp.where(i == 0, 0, jnp.where(i == nblk - 1, 2, 1))
    nk = NA_KROWS * GRID_W
    kw = k_ref[pl.ds(k0, nk), :].astype(BF16)
    vw = v_ref[pl.ds(k0, nk), :].astype(BF16)
    kc = ck_ref[0].astype(BF16)
    vc = cv_ref[0].astype(BF16)
    qp = q_ref[...].astype(F32) * (C_HEAD_DIM ** -0.5)
    lo = _lane(qp.shape) < 64
    outs = []
    for half in range(2):
        qm = jnp.where(lo if half == 0 else jnp.logical_not(lo), qp, 0.0).astype(BF16)
        s_win = _dot_nt(qm, kw) + bias_ref[btype, half]
        s_ctx = _dot_nt(qm, kc)
        m = jnp.maximum(s_win.max(axis=-1, keepdims=True), s_ctx.max(axis=-1, keepdims=True))
        e_win = jnp.exp(s_win - m)
        e_ctx = jnp.exp(s_ctx - m)
        l = e_win.sum(axis=-1, keepdims=True) + e_ctx.sum(axis=-1, keepdims=True)
        o = _dot(e_win.astype(BF16), vw) + _dot(e_ctx.astype(BF16), vc)
        outs.append(o / l)
    o_ref[...] = jnp.where(lo, outs[0], outs[1]).astype(o_ref.dtype)


def _attn_c_sample(q, k, v, ck, cv, toe):
    rows = DEC_SEQ // GRID_W
    nblk = rows // NA_QROWS
    qrows = NA_QROWS * GRID_W
    npair = C_HEADS // 2
    return pl.pallas_call(
        _attn_c_sample_kernel,
        grid=(npair, DEC_BATCH, nblk),
        in_specs=[pl.BlockSpec((qrows, LANES), lambda p, b, i: (b * nblk + i, p)),
                  pl.BlockSpec((DEC_SEQ, LANES), lambda p, b, i: (b, p)),
                  pl.BlockSpec((DEC_SEQ, LANES), lambda p, b, i: (b, p)),
                  pl.BlockSpec((1, PAST_LEN, LANES), lambda p, b, i: (b, 0, p)),
                  pl.BlockSpec((1, PAST_LEN, LANES), lambda p, b, i: (b, 0, p)),
                  pl.BlockSpec((1,) + toe.shape[1:], lambda p, b, i: (p, 0, 0, 0, 0))],
        out_specs=pl.BlockSpec((qrows, LANES), lambda p, b, i: (b * nblk + i, p)),
        out_shape=jax.ShapeDtypeStruct((T_SAMPLE, C_WIDTH), BF16),
        scratch_shapes=[pltpu.VMEM((3, 2, qrows, NA_KROWS * GRID_W), F32)],
        compiler_params=_cparams(("arbitrary", "arbitrary", "arbitrary")),
        name="attn_c_sample",
    )(q, k, v, ck, cv, toe)


def _na_toeplitz(rpb):
    w = GRID_W
    nd_r, nd_c = 2 * NA_ROWS - 1, 2 * NA_COLS - 1
    c = np.arange(w)
    cs = np.clip(c - NA_COLS // 2, 0, w - NA_COLS)
    col_ok = (c[None, :] >= cs[:, None]) & (c[None, :] < cs[:, None] + NA_COLS)
    dcol = c[None, :] - c[:, None] + NA_COLS - 1
    onehot = (np.arange(nd_c)[:, None, None] == dcol[None]).reshape(nd_c, w * w)
    toe = jnp.dot(rpb.reshape(C_HEADS * nd_r, nd_c), jnp.asarray(onehot, F32), precision=lax.Precision.HIGHEST)
    toe = jnp.where(col_ok[None, None], toe.reshape(C_HEADS, nd_r, w, w), NEG)
    toe = jnp.pad(toe, ((0, 0), (NA_KROWS, NA_KROWS), (0, 0), (0, 0)), constant_values=NEG)
    return toe.reshape(C_HEADS // 2, 2, nd_r + 2 * NA_KROWS, w, w)


def _rope_tables():
    half = A_HEAD_DIM // 2
    t = jnp.arange(DEC_SEQ)
    row = (t // GRID_W).astype(F32)
    colp = (t % GRID_W).astype(F32)
    freqs = 1.0 / (ROPE_BASE ** (jnp.arange(0, half, 2, dtype=F32) / half))
    d = np.arange(LANES) % A_HEAD_DIM
    pos = jnp.where(jnp.asarray(d < half)[None, :], row[:, None], colp[:, None])
    ang = pos * freqs[d % (half // 2)][None, :]
    sign = jnp.asarray(np.where((d % half) < half // 2, -1.0, 1.0), F32)[None, :]
    return jnp.cos(ang), jnp.sin(ang) * sign


def _head_avg_matrix():
    idx = np.arange(LANES) // A_HEAD_DIM
    return jnp.asarray((idx[:, None] == idx[None, :]).astype(np.float32) / A_HEAD_DIM, BF16)


def _tri_matrices():
    i = np.arange(MLSTM_CHUNK)
    upper = (i[:, None] <= i[None, :]).astype(np.float32)
    lower = (i[:, None] >= i[None, :]).astype(np.float32)
    return jnp.asarray(np.stack([upper, lower]), BF16)


def _router_weights(wg, bg, we, be):
    w = jnp.zeros((D_MODEL, LANES), F32).at[:, :N_GROUPS].set(wg).at[:, N_GROUPS:N_GROUPS + N_EXPERTS].set(we)
    b = jnp.zeros((1, LANES), F32).at[0, :N_GROUPS].set(bg).at[0, N_GROUPS:N_GROUPS + N_EXPERTS].set(be)
    hi = w.astype(BF16)
    lo = (w - hi.astype(F32)).astype(BF16)
    return jnp.concatenate([hi, lo], axis=1), b


def _moe(xn, rtt, cnt, order, tab, w1, w3, w2, layer):
    dest, slot_tok, block_expert, n_valid = _moe_plan(rtt, cnt, order, tab)
    return dest, _experts(block_expert, slot_tok, n_valid, xn, w1, w3, w2, layer)


def kernel(x_prompt, x_sample, cache_attn_k, cache_attn_v, state_mlstm_C, state_mlstm_n, state_mlstm_m,
           cache_na_k, cache_na_v, c, c_ctx, norm_mix, norm_ffn, norm_final, ada_w, ada_b,
           ab_w_in, ab_w_out, ab_q_norm, ab_k_norm, ab_gate_bias, ab_out_norm,
           na_w_in, na_w_out, na_rpb, moe_wg, moe_bg, moe_we, moe_be, moe_w1, moe_w3, moe_w2):
    xp = x_prompt.reshape(T_PROMPT, D_MODEL)
    xs = x_sample.reshape(T_SAMPLE, D_MODEL)
    cond =jnp.zeros((N_COND, D_MODEL), F32).at[0].set(c_ctx).at[1:1 + DEC_BATCH].set(c)
    mod = _modulation(cond, ada_w, ada_b)
    gfin = norm_final.reshape(1, D_MODEL)

    w_in = ab_w_in[0]
    o_aq, o_ak, o_av, o_bq, o_bk, o_bv, o_bo, o_bg = np.cumsum((0,) + (A_WIDTH, A_KV_WIDTH, A_KV_WIDTH,
                                                                       B_WIDTH, B_WIDTH, B_WIDTH, B_WIDTH))
    wb = jnp.concatenate([w_in[:, o_bq:o_bk], w_in[:, o_bv:o_bg]], axis=1).astype(BF16)
    wq = w_in[:, o_aq:o_ak].astype(BF16)
    wkv = w_in[:, o_ak:o_bq].astype(BF16)
    gate_rows = np.array([0, 8, 1, 9])
    wg = w_in[:, o_bg:o_bg + 4 * B_HEADS].reshape(D_MODEL, 4, B_HEADS)
    wgt = jnp.zeros((B_HEADS, 16, D_MODEL), F32).at[:, gate_rows, :].set(wg.transpose(2, 1, 0))
    wgt = jnp.concatenate([w_in[:, o_bk:o_bv].T, wgt.reshape(16 * B_HEADS, D_MODEL)], axis=0).astype(BF16)
    g_mix = norm_mix[0].reshape(1, D_MODEL)
    ob_p, oq_p, okv_p, okt_p, ogt_p = _proj_ab(xp, mod[0], g_mix, wb, wq, wkv, wgt, 0)
    ob_s, oq_s, okv_s, okt_s, ogt_s = _proj_ab(xs, mod[0], g_mix, wb, wq, wkv, wgt, T_PROMPT)

    qn = jnp.tile(ab_q_norm[0], 2).reshape(1, LANES)
    kn = jnp.tile(ab_k_norm[0], 2).reshape(1, LANES)
    bd = _head_avg_matrix()
    cos, sin = _rope_tables()
    a_p, new_k, new_v = _attn_a_prompt(oq_p, okv_p, qn, kn, bd)
    ck = cache_attn_k[:, 0].reshape(DEC_BATCH, PAST_LEN, A_KV_WIDTH)
    cv = cache_attn_v[:, 0].reshape(DEC_BATCH, PAST_LEN, A_KV_WIDTH)
    a_s = _attn_a_sample(oq_s, okv_s, ck, cv, cos, sin, qn, kn, bd)

    gb = ab_gate_bias[0]
    brow = jnp.zeros((B_HEADS, 16, LANES), F32).at[:, gate_rows, :].set(
        jnp.broadcast_to(gb.T[:, :, None], (B_HEADS, 4, LANES)))
    onorm = ab_out_norm[0].reshape(B_HEADS, 1, B_HEAD_DIM)
    tri = _tri_matrices()
    b_p, cT, nT, mT = _mlstm(ob_p, okt_p, ogt_p, brow, onorm, tri, n=SEQ, nseq=BATCH, emit_state=True)
    n0 = jnp.broadcast_to(state_mlstm_n[:, 0][..., None], state_mlstm_C[:, 0].shape)
    c0 = jnp.concatenate([state_mlstm_C[:, 0], n0], axis=-1)
    m0 = jnp.broadcast_to(state_mlstm_m[:, 0].transpose(0, 2, 1)[..., None], (DEC_BATCH, B_HEADS, 2, LANES))
    (b_s,) = _mlstm(ob_s, okt_s, ogt_s, brow, onorm, tri, n=DEC_SEQ, nseq=DEC_BATCH, init=(c0, m0))

    w_out = ab_w_out[0].astype(BF16)
    wr, br = _router_weights(moe_wg[0], moe_bg[0], moe_we[0], moe_be[0])
    x1, xn, rt, *plan = _post([(xp, xs), (a_p, a_s), (b_p, b_s)], [w_out[:A_WIDTH], w_out[A_WIDTH:]], mod[0],
                              norm_ffn[0].reshape(1, D_MODEL), wr, br)
    dest, yb = _moe(xn, *plan, moe_w1, moe_w3, moe_w2, 0)

    g_mix = norm_mix[1].reshape(1, D_MODEL)
    w_in = na_w_in[0].astype(BF16)
    proj = (mod[1], g_mix, w_in)
    x_p, q_p, k_p, v_p = _combine(dest, yb, x1, rt, mod[0], 0, T_PROMPT, proj=proj)
    x_s, q_s, k_s, v_s = _combine(dest, yb, x1, rt, mod[0], T_PROMPT, T_SAMPLE, proj=proj)
    o_p = _attn_c_prompt(q_p, k_p, v_p)
    nck = cache_na_k[:, 0].reshape(DEC_BATCH, PAST_LEN, C_WIDTH)
    ncv = cache_na_v[:, 0].reshape(DEC_BATCH, PAST_LEN, C_WIDTH)
    o_s = _attn_c_sample(q_s, k_s, v_s, nck, ncv, _na_toeplitz(na_rpb[0]))
    wr, br = _router_weights(moe_wg[1], moe_bg[1], moe_we[1], moe_be[1])
    x1, xn, rt, *plan = _post([(x_p, x_s), (o_p, o_s)], [na_w_out[0].astype(BF16)], mod[1],
                              norm_ffn[1].reshape(1, D_MODEL), wr, br)
    dest, yb = _moe(xn, *plan, moe_w1, moe_w3, moe_w2, 1)
    (y_prompt,) = _combine(dest, yb, x1, rt, mod[1], 0, T_PROMPT, gfin=gfin)
    (y_sample,) = _combine(dest, yb, x1, rt, mod[1], T_PROMPT, T_SAMPLE, gfin=gfin)
    y_prompt = y_prompt.reshape(BATCH, SEQ, D_MODEL)
    y_sample = y_sample.reshape(DEC_BATCH, DEC_SEQ, D_MODEL)
    new_attn_k = new_k.reshape(BATCH, 1, SEQ, A_KV_HEADS, A_HEAD_DIM)
    new_attn_v = new_v.reshape(BATCH, 1, SEQ, A_KV_HEADS, A_HEAD_DIM)
    new_mlstm_C = cT[:, None]
    new_mlstm_n = nT.transpose(0, 2, 1, 3)[:, None]
    new_mlstm_m = mT[..., 0].transpose(0, 2, 1)[:, None]
    new_na_k = k_p.astype(F32).reshape(BATCH, 1, SEQ, C_HEADS, C_HEAD_DIM)
    new_na_v = v_p.astype(F32).reshape(BATCH, 1, SEQ, C_HEADS, C_HEAD_DIM)
    return (y_prompt, y_sample, new_attn_k, new_attn_v, new_mlstm_C, new_mlstm_n, new_mlstm_m,
            new_na_k, new_na_v)
```

```python
import functools

import numpy as np
import jax
import jax.numpy as jnp
from jax import lax
from jax.experimental import pallas as pl
from jax.experimental.pallas import tpu as pltpu

F32 = jnp.float32
BF16 = jnp.bfloat16

D_MODEL = 1024
BATCH = 32
SEQ = 256
DEC_BATCH = 4
DEC_SEQ = 2048
PAST_LEN = 256
GRID_W = 64
A_HEADS = 8
A_KV_HEADS = 2
A_HEAD_DIM = 64
A_WIDTH = A_HEADS * A_HEAD_DIM
A_KV_WIDTH = A_KV_HEADS * A_HEAD_DIM
B_HEADS = 4
B_HEAD_DIM = 128
B_WIDTH = B_HEADS * B_HEAD_DIM
MLSTM_CHUNK = 128
C_HEADS = 16
C_HEAD_DIM = 64
C_WIDTH = C_HEADS * C_HEAD_DIM
NA_ROWS = 8
NA_COLS = 16
N_GROUPS = 4
EXPERTS_PER_GROUP = 8
N_EXPERTS = N_GROUPS * EXPERTS_PER_GROUP
D_EXPERT = 512
MOE_BLOCK = 128
ROPE_BASE = 10000.0
NORM_EPS = 1e-6

T_PROMPT = BATCH * SEQ
T_SAMPLE = DEC_BATCH * DEC_SEQ
T_ALL = T_PROMPT + T_SAMPLE
N_COND = 8
LANES = 128
SLABS = D_MODEL // LANES // 2
U32 = jnp.uint32
TOK_BLOCK = 256
POST_BLOCK = 2 * TOK_BLOCK
MLSTM_HEADS_PER_STEP = 2
EXPERT_ROWS = 512
NA_QROWS = 4
NA_KROWS = 12
NEG = -1e30
VMEM_LIMIT = 56 * 1024 * 1024


def _cparams(sem):
    return pltpu.CompilerParams(dimension_semantics=sem, vmem_limit_bytes=VMEM_LIMIT)


def _split2(x):
    hi = x.astype(BF16)
    lo = (x - hi.astype(F32)).astype(BF16)
    return hi, lo


def _split3(x):
    hi = x.astype(BF16)
    r = x - hi.astype(F32)
    mid = r.astype(BF16)
    lo = (r - mid.astype(F32)).astype(BF16)
    return hi, mid, lo


def _dot(a, b):
    return jnp.dot(a, b, preferred_element_type=F32)


def _dot_nt(a, b):
    return lax.dot_general(a, b, (((1,), (1,)), ((), ())), preferred_element_type=F32)


def _dot_exact_rhs(x, b):
    hi, mid, lo = _split3(x)
    return _dot(hi, b) + _dot(mid, b) + _dot(lo, b)


def _rms(x, g):
    ms = jnp.mean(x * x, axis=-1, keepdims=True)
    return (x * lax.rsqrt(ms + NORM_EPS)) * g


def _mod_row(tok0):
    return jnp.where(tok0 < T_PROMPT, 0, 1 + (tok0 - T_PROMPT) // DEC_SEQ)


def _mod_part(mod_ref, r, idx):
    return mod_ref[pl.ds(r, 1), idx * D_MODEL:(idx + 1) * D_MODEL]


def _head_rms(x, w, bd):
    hi, lo = _split2(x * x)
    ms = _dot(hi, bd) + _dot(lo, bd)
    return (x * lax.rsqrt(ms + NORM_EPS)) * w


def _to_token_tiles(ref, x, tile0=0):
    m = x.shape[0]
    bits = lambda v: pltpu.bitcast(v.astype(BF16).astype(F32), U32)
    for s in range(SLABS):
        lo = bits(x[:, s * LANES:(s + 1) * LANES]) >> 16
        hi = bits(x[:, (s + SLABS) * LANES:(s + SLABS + 1) * LANES]) & U32(0xFFFF0000)
        ref[pl.ds(tile0 * SLABS + s, m, stride=SLABS), :] = lo | hi


def _from_token_tiles(ref, tile0, m):
    lo, hi = [], []
    for s in range(SLABS):
        w = ref[pl.ds(tile0 * SLABS + s, m, stride=SLABS), :]
        lo.append(pltpu.bitcast(w << 16, F32))
        hi.append(pltpu.bitcast(w & U32(0xFFFF0000), F32))
    return jnp.concatenate(lo + hi, axis=1)


def _lane(shape):
    return lax.broadcasted_iota(jnp.int32, shape, len(shape) - 1)


def _dup_half(x, g):
    xr = pltpu.roll(x, 64, 1)
    lo = _lane(x.shape) < 64
    return jnp.where(lo, x, xr) if g == 0 else jnp.where(lo, xr, x)


def _rope(x, cos, sin_signed):
    lane = _lane(x.shape)
    partner = jnp.where((lane % 32) < 16, pltpu.roll(x, LANES - 16, 1), pltpu.roll(x, 16, 1))
    return x * cos + partner * sin_signed


def _with_ones(v):
    return jnp.concatenate([v.astype(BF16), jnp.ones(v.shape, BF16)], axis=1)


def _softmax_pair(qp, score_fn, value_fn):
    lo = _lane(qp.shape) < 64
    outs = []
    for half in range(2):
        qm = jnp.where(lo if half == 0 else jnp.logical_not(lo), qp, 0.0).astype(BF16)
        ss = score_fn(qm)
        m = ss[0].max(axis=-1, keepdims=True)
        for s in ss[1:]:
            m = jnp.maximum(m, s.max(axis=-1, keepdims=True))
        o = value_fn([jnp.exp(s - m).astype(BF16) for s in ss])
        outs.append(o[:, :LANES] / o[:, LANES:])
    return jnp.where(lo, outs[0], outs[1])


def _mod_kernel(cond_ref, w_ref, b_ref, o_ref):
    c = cond_ref[...]
    s = c * jax.nn.sigmoid(c)
    s_hi, s_lo = _split2(s)
    w_hi, w_lo = _split2(w_ref[0])
    o_ref[0] = _dot(s_hi, w_hi) + _dot(s_lo, w_hi) + _dot(s_hi, w_lo) + b_ref[0]


def _modulation(cond, ada_w, ada_b):
    depth, d, n = ada_w.shape
    tn = 1536
    return pl.pallas_call(
        _mod_kernel,
        grid=(depth, n // tn),
        in_specs=[pl.BlockSpec((N_COND, d), lambda l, j: (0, 0)),
                  pl.BlockSpec((1, d, tn), lambda l, j: (l, 0, j)),
                  pl.BlockSpec((1, 1, tn), lambda l, j: (l, 0, j))],
        out_specs=pl.BlockSpec((1, N_COND, tn), lambda l, j: (l, 0, j)),
        out_shape=jax.ShapeDtypeStruct((depth, N_COND, n), F32),
        compiler_params=_cparams(("arbitrary", "arbitrary")),
        name="adaln_modulation",
    )(cond, ada_w, ada_b.reshape(depth, 1, n))


def _norm_mod(x_ref, mod_ref, g_ref, shift_idx, scale_idx, row0):
    r = _mod_row(row0 + pl.program_id(0) * x_ref.shape[0])
    h = _rms(x_ref[...], g_ref[...])
    return h * (1.0 + _mod_part(mod_ref, r, scale_idx)) + _mod_part(mod_ref, r, shift_idx)


def _proj_ab_kernel(x_ref, mod_ref, g_ref, wb_ref, wq_ref, wkv_ref, wgt_ref,
                    ob_ref, oq_ref, okv_ref, okt_ref, ogt_ref, *, row0):
    tm = x_ref.shape[0]
    hb = _norm_mod(x_ref, mod_ref, g_ref, 0, 1, row0).astype(BF16)
    ob_ref[...] = _dot(hb, wb_ref[...]).astype(ob_ref.dtype)
    oq_ref[...] = _dot(hb, wq_ref[...]).astype(oq_ref.dtype)
    okv_ref[...] = _dot(hb, wkv_ref[...]).astype(okv_ref.dtype)
    gt = _dot_nt(wgt_ref[...], hb)
    for j in range(tm // LANES):
        okt_ref[j] = gt[:B_WIDTH, j * LANES:(j + 1) * LANES].astype(okt_ref.dtype)
        ogt_ref[j] = gt[B_WIDTH:, j * LANES:(j + 1) * LANES]


def _proj_ab(x, mod, g, wb, wq, wkv, wgt, row0):
    t = x.shape[0]
    tm = TOK_BLOCK
    full = lambda a: pl.BlockSpec(a.shape, lambda i: (0,) * a.ndim)
    return pl.pallas_call(
        functools.partial(_proj_ab_kernel, row0=row0),
        grid=(t // tm,),
        in_specs=[pl.BlockSpec((tm, D_MODEL), lambda i: (i, 0)), full(mod), full(g),
                  full(wb), full(wq), full(wkv), full(wgt)],
        out_specs=[pl.BlockSpec((tm, wb.shape[1]), lambda i: (i, 0)),
                   pl.BlockSpec((tm, A_WIDTH), lambda i: (i, 0)),
                   pl.BlockSpec((tm, 2 * A_KV_WIDTH), lambda i: (i, 0)),
                   pl.BlockSpec((tm // LANES, B_WIDTH, LANES), lambda i: (i, 0, 0)),
                   pl.BlockSpec((tm // LANES, wgt.shape[0] - B_WIDTH, LANES), lambda i: (i, 0, 0))],
        out_shape=[jax.ShapeDtypeStruct((t, wb.shape[1]), BF16),
                   jax.ShapeDtypeStruct((t, A_WIDTH), BF16),
                   jax.ShapeDtypeStruct((t, 2 * A_KV_WIDTH), BF16),
                   jax.ShapeDtypeStruct((t // LANES, B_WIDTH, LANES), BF16),
                   jax.ShapeDtypeStruct((t // LANES, wgt.shape[0] - B_WIDTH, LANES), F32)],
        compiler_params=_cparams(("arbitrary",)),
        name="proj_ab",
    )(x, mod, g, wb, wq, wkv, wgt)


def _gqa_block(q_ref, qn, bd, kd_ref, vd_ref, o_ref, rope=None):
    for p in range(A_HEADS // 2):
        g = p // (A_HEADS // 2 // A_KV_HEADS)
        qp = _head_rms(q_ref[:, p * LANES:(p + 1) * LANES].astype(F32), qn, bd)
        if rope is not None:
            qp = _rope(qp, rope[0], rope[1])
        qp = qp * (A_HEAD_DIM ** -0.5)
        o_ref[:, p * LANES:(p + 1) * LANES] = _softmax_pair(
            qp, lambda qm: [_dot_nt(qm, kd_ref[g])], lambda ps: _dot(ps[0], vd_ref[g])).astype(o_ref.dtype)


def _attn_a_prompt_kernel(q_ref, kv_ref, qn_ref, kn_ref, bd_ref, o_ref, knew_ref, vnew_ref, kd_ref, vd_ref):
    bd = bd_ref[...]
    k = _head_rms(kv_ref[:, :LANES].astype(F32), kn_ref[...], bd)
    v = kv_ref[:, LANES:].astype(F32)
    knew_ref[...] = k
    vnew_ref[...] = v
    for g in range(A_KV_HEADS):
        kd_ref[g] = _dup_half(k, g).astype(BF16)
        vd_ref[g] = _with_ones(_dup_half(v, g))
    _gqa_block(q_ref, qn_ref[...], bd, kd_ref, vd_ref, o_ref)


def _attn_a_prompt(q, kv, qn, kn, bd):
    nb = BATCH
    full = lambda a: pl.BlockSpec(a.shape, lambda b: (0,) * a.ndim)
    return pl.pallas_call(
        _attn_a_prompt_kernel,
        grid=(nb,),
        in_specs=[pl.BlockSpec((SEQ, A_WIDTH), lambda b: (b, 0)),
                  pl.BlockSpec((SEQ, 2 * A_KV_WIDTH), lambda b: (b, 0)),
                  full(qn), full(kn), full(bd)],
        out_specs=[pl.BlockSpec((SEQ, A_WIDTH), lambda b: (b, 0)),
                   pl.BlockSpec((SEQ, A_KV_WIDTH), lambda b: (b, 0)),
                   pl.BlockSpec((SEQ, A_KV_WIDTH), lambda b: (b, 0))],
        out_shape=[jax.ShapeDtypeStruct((T_PROMPT, A_WIDTH), BF16),
                   jax.ShapeDtypeStruct((T_PROMPT, A_KV_WIDTH), F32),
                   jax.ShapeDtypeStruct((T_PROMPT, A_KV_WIDTH), F32)],
        scratch_shapes=[pltpu.VMEM((A_KV_HEADS, SEQ, LANES), BF16),
                        pltpu.VMEM((A_KV_HEADS, SEQ, 2 * LANES), BF16)],
        compiler_params=_cparams(("arbitrary",)),
        name="attn_a_prompt",
    )(q, kv, qn, kn, bd)


_A_QBLOCK = 256


def _attn_a_sample_kernel(q_ref, kv_ref, ck_ref, cv_ref, cos_ref, sin_ref, cosq_ref, sinq_ref,
                          qn_ref, kn_ref, bd_ref, o_ref, kd_ref, vd_ref):
    bd = bd_ref[...]

    @pl.when(pl.program_id(1) == 0)
    def _():
        for g in range(A_KV_HEADS):
            kd_ref[g, :PAST_LEN] = _dup_half(ck_ref[0], g).astype(BF16)
            vd_ref[g, :PAST_LEN] = _with_ones(_dup_half(cv_ref[0], g))
        rows = 256
        for c in range(DEC_SEQ // rows):
            sl = slice(c * rows, (c + 1) * rows)
            k = _head_rms(kv_ref[sl, :LANES].astype(F32), kn_ref[...], bd)
            k = _rope(k, cos_ref[sl, :], sin_ref[sl, :])
            v = kv_ref[sl, LANES:].astype(F32)
            dst = slice(PAST_LEN + c * rows, PAST_LEN + (c + 1) * rows)
            for g in range(A_KV_HEADS):
                kd_ref[g, dst] = _dup_half(k, g).astype(BF16)
                vd_ref[g, dst] = _with_ones(_dup_half(v, g))

    _gqa_block(q_ref, qn_ref[...], bd, kd_ref, vd_ref, o_ref, rope=(cosq_ref[...], sinq_ref[...]))


def _attn_a_sample(q, kv, ck, cv, cos, sin, qn, kn, bd):
    nq = DEC_SEQ // _A_QBLOCK
    full = lambda a: pl.BlockSpec(a.shape, lambda b, i: (0,) * a.ndim)
    tk = PAST_LEN + DEC_SEQ
    return pl.pallas_call(
        _attn_a_sample_kernel,
        grid=(DEC_BATCH, nq),
        in_specs=[pl.BlockSpec((_A_QBLOCK, A_WIDTH), lambda b, i: (b * nq + i, 0)),
                  pl.BlockSpec((DEC_SEQ, 2 * A_KV_WIDTH), lambda b, i: (b, 0)),
                  pl.BlockSpec((1, PAST_LEN, A_KV_WIDTH), lambda b, i: (b, 0, 0)),
                  pl.BlockSpec((1, PAST_LEN, A_KV_WIDTH), lambda b, i: (b, 0, 0)),
                  full(cos), full(sin),
                  pl.BlockSpec((_A_QBLOCK, LANES), lambda b, i: (i, 0)),
                  pl.BlockSpec((_A_QBLOCK, LANES), lambda b, i: (i, 0)),
                  full(qn), full(kn), full(bd)],
        out_specs=pl.BlockSpec((_A_QBLOCK, A_WIDTH), lambda b, i: (b * nq + i, 0)),
        out_shape=jax.ShapeDtypeStruct((T_SAMPLE, A_WIDTH), BF16),
        scratch_shapes=[pltpu.VMEM((A_KV_HEADS, tk, LANES), BF16),
                        pltpu.VMEM((A_KV_HEADS, tk, 2 * LANES), BF16)],
        compiler_params=_cparams(("arbitrary", "arbitrary")),
        name="attn_a_sample",
    )(q, kv, ck, cv, cos, sin, cos, sin, qn, kn, bd)


def _log_sigmoid(x):
    return -(jnp.maximum(-x, 0.0) + jnp.log1p(jnp.exp(-jnp.abs(x))))


def _mlstm_kernel(*refs, n, has_init, emit_state):
    it = iter(refs)
    q_ref, v_ref, og_ref, kt_ref, gt_ref, brow_ref, onorm_ref, tri_ref = [next(it) for _ in range(8)]
    if has_init:
        c0_ref, m0_ref = [next(it) for _ in range(2)]
    out_ref = next(it)
    if emit_state:
        cT_ref, nT_ref, mT_ref = [next(it) for _ in range(3)]
    h_ref, row_ref, c_ref = [next(it) for _ in range(3)]

    L = MLSTM_CHUNK
    D = B_HEAD_DIM
    nc = n // L
    hb = MLSTM_HEADS_PER_STEP

    lane = lax.broadcasted_iota(jnp.int32, (nc * 8, L), 1)
    is_fwd = lax.broadcasted_iota(jnp.int32, (nc * 8, L), 0) % 8 == 0

    def running_max(x, suffix):
        for sh in (1, 2, 4, 8, 16, 32, 64):
            if suffix:
                x = jnp.where(lane < L - sh, jnp.maximum(x, pltpu.roll(x, L - sh, 1)), x)
            else:
                x = jnp.where(lane >= sh, jnp.maximum(x, pltpu.roll(x, sh, 1)), x)
        return x

    for hh in range(hb):
        gt = gt_ref[:, 16 * hh:16 * hh + 16, :] + brow_ref[hh][None]
        li = gt[:, 0:8, :].reshape(nc * 8, L)
        lf = _log_sigmoid(gt[:, 8:16, :]).reshape(nc * 8, L)
        cum = jnp.where(is_fwd, _dot_exact_rhs(lf, tri_ref[0]), _dot_exact_rhs(lf, tri_ref[1]))
        a = li - cum
        planes = (cum, a, jnp.broadcast_to(lf.sum(axis=-1, keepdims=True), (nc * 8, L)),
                  jnp.where(is_fwd, running_max(a, False), running_max(a, True)),
                  jnp.broadcast_to(a.max(axis=-1, keepdims=True), (nc * 8, L)))
        for p, val in enumerate(planes):
            row_ref[hh, p] = val.reshape(nc, 8, L)

    chains = [(hh, d) for hh in range(hb) for d in range(2)]
    if has_init:
        m_init = []
        for hh, d in chains:
            c_ref[2 * hh + d] = c0_ref[0, d, hh]
            m_init.append(m0_ref[0, hh, d:d + 1, :])
        m_init = tuple(m_init)
    else:
        c_ref[...] = jnp.zeros(c_ref.shape, F32)
        m_init = tuple(jnp.zeros((1, L), F32) for _ in chains)

    t_idx = lax.broadcasted_iota(jnp.int32, (L, L), 0)
    s_idx = lax.broadcasted_iota(jnp.int32, (L, L), 1)
    masks = (s_idx <= t_idx, s_idx >= t_idx)
    ones = jnp.ones((L, L), F32)

    def step(c, hh, d, m):
        r0 = pl.multiple_of(c * L, L)
        hl = slice(hh * L, (hh + 1) * L)
        qb = q_ref[pl.ds(r0, L), hl].astype(BF16)
        kst = kt_ref[c, hl, :].astype(F32) * (D ** -0.5)
        v_ext = jnp.concatenate([v_ref[pl.ds(r0, L), hl], ones], axis=1).astype(BF16)
        cum, a_row, tot, amax_run, amax = [row_ref[hh, p, c][d:d + 1, :] for p in range(5)]
        m_cum = jnp.broadcast_to(cum, (L, L)).T
        m_run = jnp.broadcast_to(amax_run, (L, L)).T
        dlog = jnp.where(masks[d], m_cum + a_row, -jnp.inf)
        inter = m_cum + m
        m_t = jnp.maximum(inter, m_cum + m_run)
        w_in = jnp.exp(dlog - m_t)
        w_st = jnp.exp(inter - m_t)
        a = _dot(qb, kst.astype(BF16)) * w_in
        ci = 2 * hh + d
        cext = c_ref[ci]
        p_state = _dot(qb, cext.astype(BF16))
        p_intra = _dot(a.astype(BF16), v_ext)
        num = w_st * p_state[:, :D] + p_intra[:, :D]
        den = w_st * p_state[:, D:] + p_intra[:, D:]
        h_ref[ci, pl.ds(r0, L), :] = num / jnp.maximum(jnp.abs(den), jnp.exp(-m_t))
        m_new = jnp.maximum(tot + m, amax + tot)
        ws = jnp.exp(a_row + tot - m_new)
        wc = jnp.exp(tot + m - m_new)
        c_ref[ci] = jnp.concatenate([wc, wc], axis=1) * cext + _dot((kst * ws).astype(BF16), v_ext)
        return m_new

    def body(i, carry):
        return tuple(step(i if d == 0 else nc - 1 - i, hh, d, m) for (hh, d), m in zip(chains, carry))

    m_fin = lax.fori_loop(0, nc, body, m_init)

    for hh in range(hb):
        hm = h_ref[2 * hh] + h_ref[2 * hh + 1]
        hl = slice(hh * L, (hh + 1) * L)
        gate = jax.nn.sigmoid(og_ref[:, hl].astype(F32))
        out_ref[:, hl] = (_rms(hm, onorm_ref[hh]) * gate).astype(out_ref.dtype)

    if emit_state:
        for k, (hh, d) in enumerate(chains):
            cext = c_ref[2 * hh + d]
            cT_ref[0, d, hh] = cext[:, :D]
            nT_ref[0, hh, d:d + 1, :] = cext[:, D:].T[0:1, :]
            mT_ref[0, hh, d:d + 1, :] = m_fin[k]


def _mlstm(ob, okt, ogt, brow, onorm, tri, *, n, nseq, init=None, emit_state=False):
    L = MLSTM_CHUNK
    nc = n // L
    H = B_HEADS
    hb = MLSTM_HEADS_PER_STEP
    ng = H // hb
    col = lambda part: (lambda b, g: (b, part * ng + g))
    in_specs = [pl.BlockSpec((n, hb * L), col(0)), pl.BlockSpec((n, hb * L), col(1)),
                pl.BlockSpec((n, hb * L), col(2)),
                pl.BlockSpec((nc, hb * L, L), lambda b, g: (b, g, 0)),
                pl.BlockSpec((nc, 16 * hb, L), lambda b, g: (b, g, 0)),
                pl.BlockSpec((hb, 16, L), lambda b, g: (g, 0, 0)),
                pl.BlockSpec((hb, 1, L), lambda b, g: (g, 0, 0)),
                pl.BlockSpec(tri.shape, lambda b, g: (0, 0, 0))]
    args = [ob, ob, ob, okt, ogt, brow, onorm, tri]
    if init is not None:
        in_specs += [pl.BlockSpec((1, 2, hb, L, 2 * L), lambda b, g: (b, 0, g, 0, 0)),
                     pl.BlockSpec((1, hb, 2, L), lambda b, g: (b, g, 0, 0))]
        args += list(init)
    out_specs = [pl.BlockSpec((n, hb * L), lambda b, g: (b, g))]
    out_shape = [jax.ShapeDtypeStruct((nseq * n, B_WIDTH), BF16)]
    if emit_state:
        out_specs += [pl.BlockSpec((1, 2, hb, L, L), lambda b, g: (b, 0, g, 0, 0)),
                      pl.BlockSpec((1, hb, 2, L), lambda b, g: (b, g, 0, 0)),
                      pl.BlockSpec((1, hb, 2, L), lambda b, g: (b, g, 0, 0))]
        out_shape += [jax.ShapeDtypeStruct((nseq, 2, H, L, L), F32),
                      jax.ShapeDtypeStruct((nseq, H, 2, L), F32),
                      jax.ShapeDtypeStruct((nseq, H, 2, L), F32)]
    return pl.pallas_call(
        functools.partial(_mlstm_kernel, n=n, has_init=init is not None, emit_state=emit_state),
        grid=(nseq, ng),
        in_specs=in_specs,
        out_specs=out_specs,
        out_shape=out_shape,
        scratch_shapes=[pltpu.VMEM((2 * hb, n, L), F32),
                        pltpu.VMEM((hb, 5, nc, 8, L), F32),
                        pltpu.VMEM((2 * hb, L, 2 * L), F32)],
        compiler_params=_cparams(("arbitrary", "arbitrary")),
        name="mlstm_init" if init is not None else "mlstm",
    )(*args)


def _router(logits):
    lane = _lane(logits.shape).astype(F32)
    big = 1e9
    gl = jnp.where(lane < N_GROUPS, logits, -jnp.inf)
    gmax = gl.max(axis=-1, keepdims=True)
    g_sel = jnp.where(gl == gmax, lane, big).min(axis=-1, keepdims=True)
    g_prob = 1.0 / jnp.exp(gl - gmax).sum(axis=-1, keepdims=True)
    lo = N_GROUPS + EXPERTS_PER_GROUP * g_sel
    el = jnp.where(lane >= lo, jnp.where(lane < lo + EXPERTS_PER_GROUP, logits, -jnp.inf), -jnp.inf)
    v1 = el.max(axis=-1, keepdims=True)
    i1 = jnp.where(el == v1, lane, big).min(axis=-1, keepdims=True)
    el2 = jnp.where(lane == i1, -jnp.inf, el)
    v2 = el2.max(axis=-1, keepdims=True)
    i2 = jnp.where(el2 == v2, lane, big).min(axis=-1, keepdims=True)
    e2 = jnp.exp(v2 - v1)
    w1 = g_prob / (1.0 + e2)
    w2 = g_prob * e2 / (1.0 + e2)
    return i1, i2, w1, w2


def _read_tokens(refs, is_prompt, rows):
    if len(refs) == 1:
        return refs[0][rows, :]
    return jnp.where(is_prompt, refs[0][rows, :], refs[1][rows, :])


def _post_kernel(*refs, groups):
    it = iter(refs)
    tok_refs = [[next(it) for _ in range(n)] for n in groups]
    w_refs = [next(it) for _ in range(len(groups) - 1)]
    mod_ref, g_ref, wr_ref, br_ref, ls_ref, sel_ref, su_ref, tokrow_ref = [next(it) for _ in range(8)]
    xnew_ref, xn_ref, rt_ref, rtt_ref, cnt_ref, order_ref, tab_ref = [next(it) for _ in range(7)]
    run_ref = next(it)
    i = pl.program_id(0)
    tm = xnew_ref.shape[0]
    is_prompt = i * tm < T_PROMPT
    r = _mod_row(i * tm)

    @pl.when(i == 0)
    def _():
        run_ref[...] = jnp.zeros(run_ref.shape, F32)

    sub = ls_ref.shape[0]
    for h in range(tm // sub):
        rows = slice(h * sub, (h + 1) * sub)
        acc = None
        for a_refs, w_ref in zip(tok_refs[1:], w_refs):
            d = _dot(_read_tokens(a_refs, is_prompt, rows).astype(BF16), w_ref[...])
            acc = d if acc is None else acc + d
        xnew = _read_tokens(tok_refs[0], is_prompt, rows) + _mod_part(mod_ref, r, 2) * acc
        xnew_ref[rows, :] = xnew
        xn = _rms(xnew, g_ref[...]) * (1.0 + _mod_part(mod_ref, r, 4)) + _mod_part(mod_ref, r, 3)
        _to_token_tiles(xn_ref, xn, h * sub)
        x_hi, x_lo = _split2(xn)
        both = _dot(x_hi, wr_ref[...])
        logits = both[:, :LANES] + both[:, LANES:] + _dot(x_lo, wr_ref[:, :LANES]) + br_ref[...]
        i1, i2, w1, w2 = _router(logits)

        lane = _lane(logits.shape).astype(F32)
        member = jnp.where(lane == i1, 1.0, jnp.where(lane == i2, 1.0, 0.0))
        local = _dot(ls_ref[...], member.astype(BF16))
        run = run_ref[...]
        before = local + run
        rank1 = jnp.where(lane == i1, before, 0.0).sum(axis=-1, keepdims=True)
        rank2 = jnp.where(lane == i2, before, 0.0).sum(axis=-1, keepdims=True)
        n_here = member.sum(axis=0, keepdims=True)
        run_ref[...] = run + n_here

        off = _dot(jnp.broadcast_to(n_here, (8, LANES)).astype(BF16), su_ref[...])[0:1]
        pos = off + local
        p1 = jnp.where(lane == i1, pos, 0.0).sum(axis=-1, keepdims=True)
        p2 = jnp.where(lane == i2, pos, 0.0).sum(axis=-1, keepdims=True)
        slot = lax.broadcasted_iota(jnp.int32, (sub, 2 * sub), 1).astype(F32)
        onehot = jnp.where(slot == p1, 1.0, jnp.where(slot == p2, 1.0, 0.0)).astype(BF16)
        order_ref[:, h * 2 * sub:(h + 1) * 2 * sub] = _dot(tokrow_ref[...], onehot) + (i * tm + h * sub).astype(F32)
        row8 = lax.broadcasted_iota(jnp.int32, (8, LANES), 0)
        tab_ref[h] = jnp.where(row8 == 0, n_here, jnp.where(row8 == 1, run, jnp.where(row8 == 2, off, 0.0)))
        cols = (i1 - N_GROUPS, i2 - N_GROUPS, w1, w2, rank1, rank2)
        rt = jnp.zeros(logits.shape, F32)
        for k, c in enumerate(cols):
            rt = jnp.where(lane == k, c, rt)
        rt_ref[rows, :] = rt
        hi, mid, lo = _split3(rt)
        sel = sel_ref[...]
        rtt_ref[:, rows] = _dot_nt(sel, hi) + _dot_nt(sel, mid) + _dot_nt(sel, lo)
    cnt_ref[...] = run_ref[...]


def _post(tok_ops, w_list, mod, g, wr, br):
    t = T_ALL
    tm = POST_BLOCK
    npb = T_PROMPT // tm
    full = lambda a: pl.BlockSpec(a.shape, lambda i: (0,) * a.ndim)
    specs, args, groups = [], [], []
    for op in tok_ops:
        if isinstance(op, tuple):
            w = op[0].shape[1]
            specs += [pl.BlockSpec((tm, w), lambda i: (jnp.minimum(i, npb - 1), 0)),
                      pl.BlockSpec((tm, w), lambda i: (jnp.maximum(i - npb, 0), 0))]
            args += list(op)
            groups.append(2)
        else:
            specs.append(pl.BlockSpec((tm, op.shape[1]), lambda i: (i, 0)))
            args.append(op)
            groups.append(1)
    sub = TOK_BLOCK
    idx = np.arange(sub)
    ls = jnp.asarray(idx[:, None] > idx[None, :], BF16)
    sel = jnp.asarray(np.arange(8)[:, None] == np.arange(LANES)[None, :], BF16)
    lanes = np.arange(LANES)
    su = jnp.asarray(lanes[:, None] < lanes[None, :], BF16)
    tokrow = jnp.asarray(np.where(np.arange(8)[:, None] == 0, idx[None, :], 0), BF16)
    consts = [mod, g, wr, br, ls, sel, su, tokrow]
    return pl.pallas_call(
        functools.partial(_post_kernel, groups=tuple(groups)),
        grid=(t // tm,),
        in_specs=specs + [full(w) for w in w_list] + [full(a) for a in consts],
        out_specs=[pl.BlockSpec((tm, D_MODEL), lambda i: (i, 0)),
                   pl.BlockSpec((tm * SLABS, LANES), lambda i: (i, 0)),
                   pl.BlockSpec((tm, LANES), lambda i: (i, 0)),
                   pl.BlockSpec((8, tm), lambda i: (0, i)),
                   pl.BlockSpec((1, LANES), lambda i: (0, 0)),
                   pl.BlockSpec((8, 2 * tm), lambda i: (0, i)),
                   pl.BlockSpec((tm // sub, 8, LANES), lambda i: (i, 0, 0))],
        out_shape=[jax.ShapeDtypeStruct((t, D_MODEL), F32),
                   jax.ShapeDtypeStruct((t * SLABS, LANES), U32),
                   jax.ShapeDtypeStruct((t, LANES), F32),
                   jax.ShapeDtypeStruct((8, t), F32),
                   jax.ShapeDtypeStruct((1, LANES), F32),
                   jax.ShapeDtypeStruct((8, 2 * t), F32),
                   jax.ShapeDtypeStruct((t // sub, 8, LANES), F32)],
        scratch_shapes=[pltpu.VMEM((1, LANES), F32)],
        compiler_params=_cparams(("arbitrary",)),
        name="post_mixer_router",
    )(*args, *w_list, *consts)


def _expert_kernel(be_ref, src_ref, tok_ref, nv_ref, xn_hbm, w1_ref, w3_ref, w2_ref, o_ref,
                   xa, xb, sem, w1b, w3b, w2b):
    i = pl.program_id(0)
    nv = nv_ref[0]
    active = i < nv
    rows = EXPERT_ROWS

    def gather(blk, buf, s):
        for r in range(rows):
            tok = tok_ref[src_ref[blk * rows + r]]
            pltpu.make_async_copy(xn_hbm.at[pl.ds(pl.multiple_of(tok * SLABS, SLABS), SLABS), :],
                                  buf.at[pl.ds(r * SLABS, SLABS), :], sem.at[s]).start(priority=r % 2)

    def wait(buf, s):
        pltpu.make_async_copy(xn_hbm.at[pl.ds(0, rows * SLABS), :], buf, sem.at[s]).wait()

    @pl.when(i == 0)
    def _():
        gather(0, xa, 0)

    changed = jnp.logical_or(i == 0, be_ref[i] != be_ref[jnp.maximum(i - 1, 0)])

    @pl.when(jnp.logical_and(changed, active))
    def _():
        w1b[...] = w1_ref[0, 0].astype(BF16)
        w3b[...] = w3_ref[0, 0].astype(BF16)
        w2b[...] = w2_ref[0, 0].astype(BF16)

    nxt = jnp.minimum(i + 1, nv - 1)

    def step(cur, s_cur, oth, s_oth):
        wait(cur, s_cur)
        gather(nxt, oth, s_oth)
        x = _from_token_tiles(cur, 0, rows).astype(BF16)
        h1 = _dot(x, w1b[...])
        h3 = _dot(x, w3b[...])
        hid = (h1 * jax.nn.sigmoid(h1)) * h3
        _to_token_tiles(o_ref, _dot(hid.astype(BF16), w2b[...]))

    @pl.when(jnp.logical_and(active, i % 2 == 0))
    def _():
        step(xa, 0, xb, 1)

    @pl.when(jnp.logical_and(active, i % 2 == 1))
    def _():
        step(xb, 1, xa, 0)

    @pl.when(jnp.logical_not(active))
    def _():
        o_ref[...] = jnp.zeros(o_ref.shape, o_ref.dtype)

    @pl.when(i == nv - 1)
    def _():
        @pl.when(i % 2 == 0)
        def _():
            wait(xb, 1)

        @pl.when(i % 2 == 1)
        def _():
            wait(xa, 0)


def _experts(block_expert, slot_src, pair_tok, n_valid, xn, w1, w3, w2, layer):
    nblk = block_expert.shape[0]
    rows = EXPERT_ROWS
    weights = lambda i, be, *_: (layer, be[i], 0, 0)
    grid_spec = pltpu.PrefetchScalarGridSpec(
        num_scalar_prefetch=4,
        grid=(nblk,),
        in_specs=[pl.BlockSpec(memory_space=pl.ANY),
                  pl.BlockSpec((1, 1, D_MODEL, D_EXPERT), weights),
                  pl.BlockSpec((1, 1, D_MODEL, D_EXPERT), weights),
                  pl.BlockSpec((1, 1, D_EXPERT, D_MODEL), weights)],
        out_specs=pl.BlockSpec((rows * SLABS, LANES), lambda i, *_: (i, 0)),
        scratch_shapes=[pltpu.VMEM((rows * SLABS, LANES), U32),
                        pltpu.VMEM((rows * SLABS, LANES), U32),
                        pltpu.SemaphoreType.DMA((2,)),
                        pltpu.VMEM((D_MODEL, D_EXPERT), BF16),
                        pltpu.VMEM((D_MODEL, D_EXPERT), BF16),
                        pltpu.VMEM((D_EXPERT, D_MODEL), BF16)])
    return pl.pallas_call(
        _expert_kernel,
        grid_spec=grid_spec,
        out_shape=jax.ShapeDtypeStruct((nblk * rows * SLABS, LANES), U32),
        compiler_params=_cparams(("arbitrary",)),
        name="moe_experts",
    )(block_expert, slot_src, pair_tok, n_valid, xn, w1, w3, w2)


def _combine_kernel(*refs, row0, project, final_norm):
    it = iter(refs)
    dest_ref, yb_hbm, x_ref, rt_ref, mod0_ref = [next(it) for _ in range(5)]
    if project:
        mod1_ref, g_ref, w_ref = [next(it) for _ in range(3)]
    if final_norm:
        gfin_ref = next(it)
    xo_ref = next(it)
    if project:
        q_ref, k_ref, v_ref = [next(it) for _ in range(3)]
    ya, yb, sem = [next(it) for _ in range(3)]
    i = pl.program_id(0)
    nblk = pl.num_programs(0)
    tm = x_ref.shape[0]

    def gather(blk, buf, s):
        for j in range(tm):
            for c in range(2):
                d = dest_ref[c * T_ALL + row0 + blk * tm + j]
                pltpu.make_async_copy(yb_hbm.at[pl.ds(pl.multiple_of(d * SLABS, SLABS), SLABS), :],
                                      buf.at[pl.ds((c * tm + j) * SLABS, SLABS), :], sem.at[s]).start(priority=c)

    def wait(buf, s):
        pltpu.make_async_copy(yb_hbm.at[pl.ds(0, 2 * tm * SLABS), :], buf, sem.at[s]).wait()

    @pl.when(i == 0)
    def _():
        gather(0, ya, 0)

    nxt = jnp.minimum(i + 1, nblk - 1)

    def step(cur, s_cur, oth, s_oth):
        wait(cur, s_cur)
        gather(nxt, oth, s_oth)
        r = _mod_row(row0 + i * tm)
        rt = rt_ref[...]
        y = rt[:, 2:3] * _from_token_tiles(cur, 0, tm) + rt[:, 3:4] * _from_token_tiles(cur, tm, tm)
        x = x_ref[...] + _mod_part(mod0_ref, r, 5) * y
        xo_ref[...] = _rms(x, gfin_ref[...]) if final_norm else x
        if project:
            h = _rms(x, g_ref[...]) * (1.0 + _mod_part(mod1_ref, r, 1)) + _mod_part(mod1_ref, r, 0)
            hb = h.astype(BF16)
            for j, o_ref in enumerate((q_ref, k_ref, v_ref)):
                o_ref[...] = _dot(hb, w_ref[:, j * C_WIDTH:(j + 1) * C_WIDTH]).astype(o_ref.dtype)

    @pl.when(i % 2 == 0)
    def _():
        step(ya, 0, yb, 1)

    @pl.when(i % 2 == 1)
    def _():
        step(yb, 1, ya, 0)

    @pl.when(i == nblk - 1)
    def _():
        @pl.when(i % 2 == 0)
        def _():
            wait(yb, 1)

        @pl.when(i % 2 == 1)
        def _():
            wait(ya, 0)


def _combine(dest, yb, x, rt, mod0, row0, t, proj=None, gfin=None):
    tm = TOK_BLOCK
    blk0 = row0 // tm
    full = lambda a: pl.BlockSpec(a.shape, lambda i, d: (0,) * a.ndim)
    extra = list(proj) if proj is not None else []
    if gfin is not None:
        extra.append(gfin)
    n_out = 4 if proj is not None else 1
    out_shape = [jax.ShapeDtypeStruct((t, D_MODEL), F32)] + [jax.ShapeDtypeStruct((t, C_WIDTH), BF16)] * (n_out - 1)
    grid_spec = pltpu.PrefetchScalarGridSpec(
        num_scalar_prefetch=1,
        grid=(t // tm,),
        in_specs=[pl.BlockSpec(memory_space=pl.ANY),
                  pl.BlockSpec((tm, D_MODEL), lambda i, d: (blk0 + i, 0)),
                  pl.BlockSpec((tm, LANES), lambda i, d: (blk0 + i, 0)),
                  full(mod0)] + [full(a) for a in extra],
        out_specs=[pl.BlockSpec((tm, D_MODEL), lambda i, d: (i, 0))] * n_out,
        scratch_shapes=[pltpu.VMEM((2 * tm * SLABS, LANES), U32),
                        pltpu.VMEM((2 * tm * SLABS, LANES), U32),
                        pltpu.SemaphoreType.DMA((2,))])
    return pl.pallas_call(
        functools.partial(_combine_kernel, row0=row0, project=proj is not None, final_norm=gfin is not None),
        grid_spec=grid_spec,
        out_shape=out_shape,
        compiler_params=_cparams(("arbitrary",)),
        name="moe_combine_proj" if proj is not None else "moe_combine",
    )(dest, yb, x, rt, mod0, *extra)


def _moe_plan(rtt, cnt, order, tab):
    t = rtt.shape[1]
    eid = rtt[0:2].astype(jnp.int32)
    rank = rtt[4:6].astype(jnp.int32)
    counts = cnt[0, N_GROUPS:N_GROUPS + N_EXPERTS].astype(jnp.int32)
    padded = (counts + EXPERT_ROWS - 1) // EXPERT_ROWS * EXPERT_ROWS
    seg_end = jnp.cumsum(padded)
    seg_start = seg_end - padded
    experts = jnp.arange(N_EXPERTS, dtype=jnp.int32)
    start = jnp.sum(jnp.where(eid[..., None] == experts, seg_start, 0), axis=-1)
    dest = (start + rank).reshape(-1)
    n_blocks = (2 * t + N_EXPERTS * (EXPERT_ROWS - 1) + EXPERT_ROWS - 1) // EXPERT_ROWS
    first_row = jnp.arange(n_blocks, dtype=jnp.int32) * EXPERT_ROWS
    block_expert = jnp.minimum(jnp.sum((seg_end[None, :] <= first_row[:, None]).astype(jnp.int32), axis=1),
                               N_EXPERTS - 1)
    n_valid = (seg_end[-1:] // EXPERT_ROWS).astype(jnp.int32)

    nsb = tab.shape[0]
    n_sb = tab[:, 0, N_GROUPS:N_GROUPS + N_EXPERTS].astype(jnp.int32).T[block_expert]
    run_sb = tab[:, 1, N_GROUPS:N_GROUPS + N_EXPERTS].astype(jnp.int32).T[block_expert]
    off_sb = tab[:, 2, N_GROUPS:N_GROUPS + N_EXPERTS].astype(jnp.int32).T[block_expert]
    rank = (first_row - seg_start[block_expert])[:, None] + jnp.arange(EXPERT_ROWS, dtype=jnp.int32)[None, :]
    real = rank < counts[block_expert][:, None]
    sb = jnp.sum(((run_sb + n_sb)[:, None, :] <= rank[:, :, None]).astype(jnp.int32), axis=-1)
    sb = jnp.minimum(sb, nsb - 1)
    pick = sb[:, :, None] == jnp.arange(nsb, dtype=jnp.int32)[None, None, :]
    shift = jnp.sum(jnp.where(pick, (off_sb - run_sb)[:, None, :], 0), axis=-1)
    src = jnp.clip(sb * (2 * TOK_BLOCK) + shift + rank, 0, 2 * t - 1)
    slot = first_row[:, None] + jnp.arange(EXPERT_ROWS, dtype=jnp.int32)[None, :]
    slot_src = jnp.where(real, src, slot % (2 * t)).reshape(-1)
    return dest, slot_src, order[0].astype(jnp.int32), block_expert, n_valid


def _attn_c_prompt_kernel(q_ref, k_ref, v_ref, o_ref):
    for p in range(C_HEADS // 2):
        sl = slice(p * LANES, (p + 1) * LANES)
        kb = k_ref[:, sl].astype(BF16)
        vb = _with_ones(v_ref[:, sl])
        qp = q_ref[:, sl].astype(F32) * (C_HEAD_DIM ** -0.5)
        o_ref[:, sl] = _softmax_pair(qp, lambda qm: [_dot_nt(qm, kb)],
                                     lambda ps: _dot(ps[0], vb)).astype(o_ref.dtype)


def _attn_c_prompt(q, k, v):
    blk = pl.BlockSpec((SEQ, C_WIDTH), lambda b: (b, 0))
    return pl.pallas_call(
        _attn_c_prompt_kernel,
        grid=(BATCH,),
        in_specs=[blk, blk, blk],
        out_specs=blk,
        out_shape=jax.ShapeDtypeStruct((T_PROMPT, C_WIDTH), BF16),
        compiler_params=_cparams(("arbitrary",)),
        name="attn_c_prompt",
    )(q, k, v)


def _na_key_start(r0):
    rows = DEC_SEQ // GRID_W
    return jnp.minimum(jnp.clip(r0 - NA_ROWS // 2, 0, rows - NA_ROWS), rows - NA_KROWS)


def _na_block_plan():
    rows = DEC_SEQ // GRID_W
    nblk = rows // NA_QROWS
    plan = []
    for blk in (0, 1, nblk - 1):
        r0 = blk * NA_QROWS
        ks = min(int(np.clip(r0 - NA_ROWS // 2, 0, rows - NA_ROWS)), rows - NA_KROWS)
        per_row = []
        for i in range(NA_QROWS):
            r = r0 + i
            rs = int(np.clip(r - NA_ROWS // 2, 0, rows - NA_ROWS))
            start = ks - r + NA_ROWS - 1 + NA_KROWS
            ok = [rs <= ks + j < rs + NA_ROWS for j in range(NA_KROWS)]
            per_row.append((start, ok))
        plan.append(per_row)
    return plan


def _attn_c_sample_kernel(q_ref, k_ref, v_ref, ck_ref, cv_ref, toe_ref, o_ref, bias_ref):
    rows = DEC_SEQ // GRID_W
    nblk = rows // NA_QROWS
    w = GRID_W

    @pl.when(jnp.logical_and(pl.program_id(1) == 0, pl.program_id(2) == 0))
    def _():
        neg = jnp.full((w, w), NEG, F32)
        for t, per_row in enumerate(_na_block_plan()):
            for half in range(2):
                for i, (start, ok) in enumerate(per_row):
                    for j in range(0, NA_KROWS, 2):
                        pieces = [toe_ref[0, half, start + jj] if ok[jj] else neg for jj in (j, j + 1)]
                        bias_ref[t, half, i * w:(i + 1) * w, j * w:(j + 2) * w] = jnp.concatenate(pieces, axis=1)

    i = pl.program_id(2)
    r0 = i * NA_QROWS
    k0 = pl.multiple_of(_na_key_start(r0) * GRID_W, GRID_W)
    btype = jnp.where(i == 0, 0, jnp.where(i == nblk - 1, 2, 1))
    nk = NA_KROWS * GRID_W
    kw = k_ref[pl.ds(k0, nk), :].astype(BF16)
    vw = v_ref[pl.ds(k0, nk), :].astype(BF16)
    kc = ck_ref[0].astype(BF16)
    vc = cv_ref[0].astype(BF16)
    qp = q_ref[...].astype(F32) * (C_HEAD_DIM ** -0.5)
    lo = _lane(qp.shape) < 64
    outs = []
    for half in range(2):
        qm = jnp.where(lo if half == 0 else jnp.logical_not(lo), qp, 0.0).astype(BF16)
        s_win = _dot_nt(qm, kw) + bias_ref[btype, half]
        s_ctx = _dot_nt(qm, kc)
        m = jnp.maximum(s_win.max(axis=-1, keepdims=True), s_ctx.max(axis=-1, keepdims=True))
        e_win = jnp.exp(s_win - m)
        e_ctx = jnp.exp(s_ctx - m)
        l = e_win.sum(axis=-1, keepdims=True) + e_ctx.sum(axis=-1, keepdims=True)
        o = _dot(e_win.astype(BF16), vw) + _dot(e_ctx.astype(BF16), vc)
        outs.append(o / l)
    o_ref[...] = jnp.where(lo, outs[0], outs[1]).astype(o_ref.dtype)


def _attn_c_sample(q, k, v, ck, cv, toe):
    rows = DEC_SEQ // GRID_W
    nblk = rows // NA_QROWS
    qrows = NA_QROWS * GRID_W
    npair = C_HEADS // 2
    return pl.pallas_call(
        _attn_c_sample_kernel,
        grid=(npair, DEC_BATCH, nblk),
        in_specs=[pl.BlockSpec((qrows, LANES), lambda p, b, i: (b * nblk + i, p)),
                  pl.BlockSpec((DEC_SEQ, LANES), lambda p, b, i: (b, p)),
                  pl.BlockSpec((DEC_SEQ, LANES), lambda p, b, i: (b, p)),
                  pl.BlockSpec((1, PAST_LEN, LANES), lambda p, b, i: (b, 0, p)),
                  pl.BlockSpec((1, PAST_LEN, LANES), lambda p, b, i: (b, 0, p)),
                  pl.BlockSpec((1,) + toe.shape[1:], lambda p, b, i: (p, 0, 0, 0, 0))],
        out_specs=pl.BlockSpec((qrows, LANES), lambda p, b, i: (b * nblk + i, p)),
        out_shape=jax.ShapeDtypeStruct((T_SAMPLE, C_WIDTH), BF16),
        scratch_shapes=[pltpu.VMEM((3, 2, qrows, NA_KROWS * GRID_W), F32)],
        compiler_params=_cparams(("arbitrary", "arbitrary", "arbitrary")),
        name="attn_c_sample",
    )(q, k, v, ck, cv, toe)


def _na_toeplitz(rpb):
    w = GRID_W
    nd_r, nd_c = 2 * NA_ROWS - 1, 2 * NA_COLS - 1
    c = np.arange(w)
    cs = np.clip(c - NA_COLS // 2, 0, w - NA_COLS)
    col_ok = (c[None, :] >= cs[:, None]) & (c[None, :] < cs[:, None] + NA_COLS)
    dcol = c[None, :] - c[:, None] + NA_COLS - 1
    onehot = (np.arange(nd_c)[:, None, None] == dcol[None]).reshape(nd_c, w * w)
    toe = jnp.dot(rpb.reshape(C_HEADS * nd_r, nd_c), jnp.asarray(onehot, F32), precision=lax.Precision.HIGHEST)
    toe = jnp.where(col_ok[None, None], toe.reshape(C_HEADS, nd_r, w, w), NEG)
    toe = jnp.pad(toe, ((0, 0), (NA_KROWS, NA_KROWS), (0, 0), (0, 0)), constant_values=NEG)
    return toe.reshape(C_HEADS // 2, 2, nd_r + 2 * NA_KROWS, w, w)


def _rope_tables():
    half = A_HEAD_DIM // 2
    t = jnp.arange(DEC_SEQ)
    row = (t // GRID_W).astype(F32)
    colp = (t % GRID_W).astype(F32)
    freqs = 1.0 / (ROPE_BASE ** (jnp.arange(0, half, 2, dtype=F32) / half))
    d = np.arange(LANES) % A_HEAD_DIM
    pos = jnp.where(jnp.asarray(d < half)[None, :], row[:, None], colp[:, None])
    ang = pos * freqs[d % (half // 2)][None, :]
    sign = jnp.asarray(np.where((d % half) < half // 2, -1.0, 1.0), F32)[None, :]
    return jnp.cos(ang), jnp.sin(ang) * sign


def _head_avg_matrix():
    idx = np.arange(LANES) // A_HEAD_DIM
    return jnp.asarray((idx[:, None] == idx[None, :]).astype(np.float32) / A_HEAD_DIM, BF16)


def _tri_matrices():
    i = np.arange(MLSTM_CHUNK)
    upper = (i[:, None] <= i[None, :]).astype(np.float32)
    lower = (i[:, None] >= i[None, :]).astype(np.float32)
    return jnp.asarray(np.stack([upper, lower]), BF16)


def _router_weights(wg, bg, we, be):
    w = jnp.zeros((D_MODEL, LANES), F32).at[:, :N_GROUPS].set(wg).at[:, N_GROUPS:N_GROUPS + N_EXPERTS].set(we)
    b = jnp.zeros((1, LANES), F32).at[0, :N_GROUPS].set(bg).at[0, N_GROUPS:N_GROUPS + N_EXPERTS].set(be)
    hi = w.astype(BF16)
    lo = (w - hi.astype(F32)).astype(BF16)
    return jnp.concatenate([hi, lo], axis=1), b


def _moe(xn, rtt, cnt, order, tab, w1, w3, w2, layer):
    dest, slot_src, pair_tok, block_expert, n_valid = _moe_plan(rtt, cnt, order, tab)
    return dest, _experts(block_expert, slot_src, pair_tok, n_valid, xn, w1, w3, w2, layer)


def kernel(x_prompt, x_sample, cache_attn_k, cache_attn_v, state_mlstm_C, state_mlstm_n, state_mlstm_m,
           cache_na_k, cache_na_v, c, c_ctx, norm_mix, norm_ffn, norm_final, ada_w, ada_b,
           ab_w_in, ab_w_out, ab_q_norm, ab_k_norm, ab_gate_bias, ab_out_norm,
           na_w_in, na_w_out, na_rpb, moe_wg, moe_bg, moe_we, moe_be, moe_w1, moe_w3, moe_w2):
    xp = x_prompt.reshape(T_PROMPT, D_MODEL)
    xs = x_sample.reshape(T_SAMPLE, D_MODEL)
    cond =jnp.zeros((N_COND, D_MODEL), F32).at[0].set(c_ctx).at[1:1 + DEC_BATCH].set(c)
    mod = _modulation(cond, ada_w, ada_b)
    gfin = norm_final.reshape(1, D_MODEL)

    w_in = ab_w_in[0]
    o_aq, o_ak, o_av, o_bq, o_bk, o_bv, o_bo, o_bg = np.cumsum((0,) + (A_WIDTH, A_KV_WIDTH, A_KV_WIDTH,
                                                                       B_WIDTH, B_WIDTH, B_WIDTH, B_WIDTH))
    wb = jnp.concatenate([w_in[:, o_bq:o_bk], w_in[:, o_bv:o_bg]], axis=1).astype(BF16)
    wq = w_in[:, o_aq:o_ak].astype(BF16)
    wkv = w_in[:, o_ak:o_bq].astype(BF16)
    gate_rows = np.array([0, 8, 1, 9])
    wg = w_in[:, o_bg:o_bg + 4 * B_HEADS].reshape(D_MODEL, 4, B_HEADS)
    wgt = jnp.zeros((B_HEADS, 16, D_MODEL), F32).at[:, gate_rows, :].set(wg.transpose(2, 1, 0))
    wgt = jnp.concatenate([w_in[:, o_bk:o_bv].T, wgt.reshape(16 * B_HEADS, D_MODEL)], axis=0).astype(BF16)
    g_mix = norm_mix[0].reshape(1, D_MODEL)
    ob_p, oq_p, okv_p, okt_p, ogt_p = _proj_ab(xp, mod[0], g_mix, wb, wq, wkv, wgt, 0)
    ob_s, oq_s, okv_s, okt_s, ogt_s = _proj_ab(xs, mod[0], g_mix, wb, wq, wkv, wgt, T_PROMPT)

    qn = jnp.tile(ab_q_norm[0], 2).reshape(1, LANES)
    kn = jnp.tile(ab_k_norm[0], 2).reshape(1, LANES)
    bd = _head_avg_matrix()
    cos, sin = _rope_tables()
    a_p, new_k, new_v = _attn_a_prompt(oq_p, okv_p, qn, kn, bd)
    ck = cache_attn_k[:, 0].reshape(DEC_BATCH, PAST_LEN, A_KV_WIDTH)
    cv = cache_attn_v[:, 0].reshape(DEC_BATCH, PAST_LEN, A_KV_WIDTH)
    a_s = _attn_a_sample(oq_s, okv_s, ck, cv, cos, sin, qn, kn, bd)

    gb = ab_gate_bias[0]
    brow = jnp.zeros((B_HEADS, 16, LANES), F32).at[:, gate_rows, :].set(
        jnp.broadcast_to(gb.T[:, :, None], (B_HEADS, 4, LANES)))
    onorm = ab_out_norm[0].reshape(B_HEADS, 1, B_HEAD_DIM)
    tri = _tri_matrices()
    b_p, cT, nT, mT = _mlstm(ob_p, okt_p, ogt_p, brow, onorm, tri, n=SEQ, nseq=BATCH, emit_state=True)
    n0 = jnp.broadcast_to(state_mlstm_n[:, 0][..., None], state_mlstm_C[:, 0].shape)
    c0 = jnp.concatenate([state_mlstm_C[:, 0], n0], axis=-1)
    m0 = jnp.broadcast_to(state_mlstm_m[:, 0].transpose(0, 2, 1)[..., None], (DEC_BATCH, B_HEADS, 2, LANES))
    (b_s,) = _mlstm(ob_s, okt_s, ogt_s, brow, onorm, tri, n=DEC_SEQ, nseq=DEC_BATCH, init=(c0, m0))

    w_out = ab_w_out[0].astype(BF16)
    wr, br = _router_weights(moe_wg[0], moe_bg[0], moe_we[0], moe_be[0])
    x1, xn, rt, *plan = _post([(xp, xs), (a_p, a_s), (b_p, b_s)], [w_out[:A_WIDTH], w_out[A_WIDTH:]], mod[0],
                              norm_ffn[0].reshape(1, D_MODEL), wr, br)
    dest, yb = _moe(xn, *plan, moe_w1, moe_w3, moe_w2, 0)

    g_mix = norm_mix[1].reshape(1, D_MODEL)
    w_in = na_w_in[0].astype(BF16)
    proj = (mod[1], g_mix, w_in)
    x_p, q_p, k_p, v_p = _combine(dest, yb, x1, rt, mod[0], 0, T_PROMPT, proj=proj)
    x_s, q_s, k_s, v_s = _combine(dest, yb, x1, rt, mod[0], T_PROMPT, T_SAMPLE, proj=proj)
    o_p = _attn_c_prompt(q_p, k_p, v_p)
    nck = cache_na_k[:, 0].reshape(DEC_BATCH, PAST_LEN, C_WIDTH)
    ncv = cache_na_v[:, 0].reshape(DEC_BATCH, PAST_LEN, C_WIDTH)
    o_s = _attn_c_sample(q_s, k_s, v_s, nck, ncv, _na_toeplitz(na_rpb[0]))
    wr, br = _router_weights(moe_wg[1], moe_bg[1], moe_we[1], moe_be[1])
    x1, xn, rt, *plan = _post([(x_p, x_s), (o_p, o_s)], [na_w_out[0].astype(BF16)], mod[1],
                              norm_ffn[1].reshape(1, D_MODEL), wr, br)
    dest, yb = _moe(xn, *plan, moe_w1, moe_w3, moe_w2, 1)
    (y_prompt,) = _combine(dest, yb, x1, rt, mod[1], 0, T_PROMPT, gfin=gfin)
    (y_sample,) = _combine(dest, yb, x1, rt, mod[1], T_PROMPT, T_SAMPLE, gfin=gfin)
    y_prompt = y_prompt.reshape(BATCH, SEQ, D_MODEL)
    y_sample = y_sample.reshape(DEC_BATCH, DEC_SEQ, D_MODEL)
    new_attn_k = new_k.reshape(BATCH, 1, SEQ, A_KV_HEADS, A_HEAD_DIM)
    new_attn_v = new_v.reshape(BATCH, 1, SEQ, A_KV_HEADS, A_HEAD_DIM)
    new_mlstm_C = cT[:, None]
    new_mlstm_n = nT.transpose(0, 2, 1, 3)[:, None]
    new_mlstm_m = mT[..., 0].transpose(0, 2, 1)[:, None]
    new_na_k = k_p.astype(F32).reshape(BATCH, 1, SEQ, C_HEADS, C_HEAD_DIM)
    new_na_v = v_p.astype(F32).reshape(BATCH, 1, SEQ, C_HEADS, C_HEAD_DIM)
    return (y_prompt, y_sample, new_attn_k, new_attn_v, new_mlstm_C, new_mlstm_n, new_mlstm_m,
            new_na_k, new_na_v)
```

```python
import functools

import numpy as np
import jax
import jax.numpy as jnp
from jax import lax
from jax.experimental import pallas as pl
from jax.experimental.pallas import tpu as pltpu

F32 = jnp.float32
BF16 = jnp.bfloat16

D_MODEL = 1024
BATCH = 32
SEQ = 256
DEC_BATCH = 4
DEC_SEQ = 2048
PAST_LEN = 256
GRID_W = 64
A_HEADS = 8
A_KV_HEADS = 2
A_HEAD_DIM = 64
A_WIDTH = A_HEADS * A_HEAD_DIM
A_KV_WIDTH = A_KV_HEADS * A_HEAD_DIM
B_HEADS = 4
B_HEAD_DIM = 128
B_WIDTH = B_HEADS * B_HEAD_DIM
MLSTM_CHUNK = 128
C_HEADS = 16
C_HEAD_DIM = 64
C_WIDTH = C_HEADS * C_HEAD_DIM
NA_ROWS = 8
NA_COLS = 16
N_GROUPS = 4
EXPERTS_PER_GROUP = 8
N_EXPERTS = N_GROUPS * EXPERTS_PER_GROUP
D_EXPERT = 512
MOE_BLOCK = 128
ROPE_BASE = 10000.0
NORM_EPS = 1e-6

T_PROMPT = BATCH * SEQ
T_SAMPLE = DEC_BATCH * DEC_SEQ
T_ALL = T_PROMPT + T_SAMPLE
N_COND = 8
LANES = 128
SLABS = D_MODEL // LANES // 2
U32 = jnp.uint32
TOK_BLOCK = 256
POST_BLOCK = 2 * TOK_BLOCK
MLSTM_HEADS_PER_STEP = 4
EXPERT_ROWS = 512
NA_QROWS = 4
NA_KROWS = 12
NEG = -1e30
VMEM_LIMIT = 56 * 1024 * 1024


def _cparams(sem):
    return pltpu.CompilerParams(dimension_semantics=sem, vmem_limit_bytes=VMEM_LIMIT)


def _split2(x):
    hi = x.astype(BF16)
    lo = (x - hi.astype(F32)).astype(BF16)
    return hi, lo


def _split3(x):
    hi = x.astype(BF16)
    r = x - hi.astype(F32)
    mid = r.astype(BF16)
    lo = (r - mid.astype(F32)).astype(BF16)
    return hi, mid, lo


def _dot(a, b):
    return jnp.dot(a, b, preferred_element_type=F32)


def _dot_nt(a, b):
    return lax.dot_general(a, b, (((1,), (1,)), ((), ())), preferred_element_type=F32)


def _dot_exact_rhs(x, b):
    hi, mid, lo = _split3(x)
    return _dot(hi, b) + _dot(mid, b) + _dot(lo, b)


def _rms(x, g):
    ms = jnp.mean(x * x, axis=-1, keepdims=True)
    return (x * lax.rsqrt(ms + NORM_EPS)) * g


def _mod_row(tok0):
    return jnp.where(tok0 < T_PROMPT, 0, 1 + (tok0 - T_PROMPT) // DEC_SEQ)


def _mod_part(mod_ref, r, idx):
    return mod_ref[pl.ds(r, 1), idx * D_MODEL:(idx + 1) * D_MODEL]


def _head_rms(x, w, bd):
    hi, lo = _split2(x * x)
    ms = _dot(hi, bd) + _dot(lo, bd)
    return (x * lax.rsqrt(ms + NORM_EPS)) * w


def _to_token_tiles(ref, x, tile0=0):
    m = x.shape[0]
    bits = lambda v: pltpu.bitcast(v.astype(BF16).astype(F32), U32)
    for s in range(SLABS):
        lo = bits(x[:, s * LANES:(s + 1) * LANES]) >> 16
        hi = bits(x[:, (s + SLABS) * LANES:(s + SLABS + 1) * LANES]) & U32(0xFFFF0000)
        ref[pl.ds(tile0 * SLABS + s, m, stride=SLABS), :] = lo | hi


def _from_token_tiles(ref, tile0, m):
    lo, hi = [], []
    for s in range(SLABS):
        w = ref[pl.ds(tile0 * SLABS + s, m, stride=SLABS), :]
        lo.append(pltpu.bitcast(w << 16, F32))
        hi.append(pltpu.bitcast(w & U32(0xFFFF0000), F32))
    return jnp.concatenate(lo + hi, axis=1)


def _lane(shape):
    return lax.broadcasted_iota(jnp.int32, shape, len(shape) - 1)


def _dup_half(x, g):
    xr = pltpu.roll(x, 64, 1)
    lo = _lane(x.shape) < 64
    return jnp.where(lo, x, xr) if g == 0 else jnp.where(lo, xr, x)


def _rope(x, cos, sin_signed):
    lane = _lane(x.shape)
    partner = jnp.where((lane % 32) < 16, pltpu.roll(x, LANES - 16, 1), pltpu.roll(x, 16, 1))
    return x * cos + partner * sin_signed


def _with_ones(v):
    return jnp.concatenate([v.astype(BF16), jnp.ones(v.shape, BF16)], axis=1)


def _softmax_pair(qp, score_fn, value_fn):
    lo = _lane(qp.shape) < 64
    outs = []
    for half in range(2):
        qm = jnp.where(lo if half == 0 else jnp.logical_not(lo), qp, 0.0).astype(BF16)
        ss = score_fn(qm)
        m = ss[0].max(axis=-1, keepdims=True)
        for s in ss[1:]:
            m = jnp.maximum(m, s.max(axis=-1, keepdims=True))
        o = value_fn([jnp.exp(s - m).astype(BF16) for s in ss])
        outs.append(o[:, :LANES] / o[:, LANES:])
    return jnp.where(lo, outs[0], outs[1])


def _mod_kernel(cond_ref, w_ref, b_ref, o_ref):
    c = cond_ref[...]
    s = c * jax.nn.sigmoid(c)
    s_hi, s_lo = _split2(s)
    w_hi, w_lo = _split2(w_ref[0])
    o_ref[0] = _dot(s_hi, w_hi) + _dot(s_lo, w_hi) + _dot(s_hi, w_lo) + b_ref[0]


def _modulation(cond, ada_w, ada_b):
    depth, d, n = ada_w.shape
    tn = 1536
    return pl.pallas_call(
        _mod_kernel,
        grid=(depth, n // tn),
        in_specs=[pl.BlockSpec((N_COND, d), lambda l, j: (0, 0)),
                  pl.BlockSpec((1, d, tn), lambda l, j: (l, 0, j)),
                  pl.BlockSpec((1, 1, tn), lambda l, j: (l, 0, j))],
        out_specs=pl.BlockSpec((1, N_COND, tn), lambda l, j: (l, 0, j)),
        out_shape=jax.ShapeDtypeStruct((depth, N_COND, n), F32),
        compiler_params=_cparams(("arbitrary", "arbitrary")),
        name="adaln_modulation",
    )(cond, ada_w, ada_b.reshape(depth, 1, n))


def _norm_mod(x_ref, mod_ref, g_ref, shift_idx, scale_idx, row0):
    r = _mod_row(row0 + pl.program_id(0) * x_ref.shape[0])
    h = _rms(x_ref[...], g_ref[...])
    return h * (1.0 + _mod_part(mod_ref, r, scale_idx)) + _mod_part(mod_ref, r, shift_idx)


def _proj_ab_kernel(x_ref, mod_ref, g_ref, wb_ref, wq_ref, wkv_ref, wgt_ref,
                    ob_ref, oq_ref, okv_ref, okt_ref, ogt_ref, *, row0):
    tm = x_ref.shape[0]
    hb = _norm_mod(x_ref, mod_ref, g_ref, 0, 1, row0).astype(BF16)
    ob_ref[...] = _dot(hb, wb_ref[...]).astype(ob_ref.dtype)
    oq_ref[...] = _dot(hb, wq_ref[...]).astype(oq_ref.dtype)
    okv_ref[...] = _dot(hb, wkv_ref[...]).astype(okv_ref.dtype)
    gt = _dot_nt(wgt_ref[...], hb)
    for j in range(tm // LANES):
        okt_ref[j] = gt[:B_WIDTH, j * LANES:(j + 1) * LANES].astype(okt_ref.dtype)
        ogt_ref[j] = gt[B_WIDTH:, j * LANES:(j + 1) * LANES]


def _proj_ab(x, mod, g, wb, wq, wkv, wgt, row0):
    t = x.shape[0]
    tm = TOK_BLOCK
    full = lambda a: pl.BlockSpec(a.shape, lambda i: (0,) * a.ndim)
    return pl.pallas_call(
        functools.partial(_proj_ab_kernel, row0=row0),
        grid=(t // tm,),
        in_specs=[pl.BlockSpec((tm, D_MODEL), lambda i: (i, 0)), full(mod), full(g),
                  full(wb), full(wq), full(wkv), full(wgt)],
        out_specs=[pl.BlockSpec((tm, wb.shape[1]), lambda i: (i, 0)),
                   pl.BlockSpec((tm, A_WIDTH), lambda i: (i, 0)),
                   pl.BlockSpec((tm, 2 * A_KV_WIDTH), lambda i: (i, 0)),
                   pl.BlockSpec((tm // LANES, B_WIDTH, LANES), lambda i: (i, 0, 0)),
                   pl.BlockSpec((tm // LANES, wgt.shape[0] - B_WIDTH, LANES), lambda i: (i, 0, 0))],
        out_shape=[jax.ShapeDtypeStruct((t, wb.shape[1]), BF16),
                   jax.ShapeDtypeStruct((t, A_WIDTH), BF16),
                   jax.ShapeDtypeStruct((t, 2 * A_KV_WIDTH), BF16),
                   jax.ShapeDtypeStruct((t // LANES, B_WIDTH, LANES), BF16),
                   jax.ShapeDtypeStruct((t // LANES, wgt.shape[0] - B_WIDTH, LANES), F32)],
        compiler_params=_cparams(("arbitrary",)),
        name="proj_ab",
    )(x, mod, g, wb, wq, wkv, wgt)


def _gqa_block(q_ref, qn, bd, kd_ref, vd_ref, o_ref, rope=None):
    for p in range(A_HEADS // 2):
        g = p // (A_HEADS // 2 // A_KV_HEADS)
        qp = _head_rms(q_ref[:, p * LANES:(p + 1) * LANES].astype(F32), qn, bd)
        if rope is not None:
            qp = _rope(qp, rope[0], rope[1])
        qp = qp * (A_HEAD_DIM ** -0.5)
        o_ref[:, p * LANES:(p + 1) * LANES] = _softmax_pair(
            qp, lambda qm: [_dot_nt(qm, kd_ref[g])], lambda ps: _dot(ps[0], vd_ref[g])).astype(o_ref.dtype)


def _attn_a_prompt_kernel(q_ref, kv_ref, qn_ref, kn_ref, bd_ref, o_ref, knew_ref, vnew_ref, kd_ref, vd_ref):
    bd = bd_ref[...]
    k = _head_rms(kv_ref[:, :LANES].astype(F32), kn_ref[...], bd)
    v = kv_ref[:, LANES:].astype(F32)
    knew_ref[...] = k
    vnew_ref[...] = v
    for g in range(A_KV_HEADS):
        kd_ref[g] = _dup_half(k, g).astype(BF16)
        vd_ref[g] = _with_ones(_dup_half(v, g))
    _gqa_block(q_ref, qn_ref[...], bd, kd_ref, vd_ref, o_ref)


def _attn_a_prompt(q, kv, qn, kn, bd):
    nb = BATCH
    full = lambda a: pl.BlockSpec(a.shape, lambda b: (0,) * a.ndim)
    return pl.pallas_call(
        _attn_a_prompt_kernel,
        grid=(nb,),
        in_specs=[pl.BlockSpec((SEQ, A_WIDTH), lambda b: (b, 0)),
                  pl.BlockSpec((SEQ, 2 * A_KV_WIDTH), lambda b: (b, 0)),
                  full(qn), full(kn), full(bd)],
        out_specs=[pl.BlockSpec((SEQ, A_WIDTH), lambda b: (b, 0)),
                   pl.BlockSpec((SEQ, A_KV_WIDTH), lambda b: (b, 0)),
                   pl.BlockSpec((SEQ, A_KV_WIDTH), lambda b: (b, 0))],
        out_shape=[jax.ShapeDtypeStruct((T_PROMPT, A_WIDTH), BF16),
                   jax.ShapeDtypeStruct((T_PROMPT, A_KV_WIDTH), F32),
                   jax.ShapeDtypeStruct((T_PROMPT, A_KV_WIDTH), F32)],
        scratch_shapes=[pltpu.VMEM((A_KV_HEADS, SEQ, LANES), BF16),
                        pltpu.VMEM((A_KV_HEADS, SEQ, 2 * LANES), BF16)],
        compiler_params=_cparams(("arbitrary",)),
        name="attn_a_prompt",
    )(q, kv, qn, kn, bd)


_A_QBLOCK = 256


def _attn_a_sample_kernel(q_ref, kv_ref, ck_ref, cv_ref, cos_ref, sin_ref, cosq_ref, sinq_ref,
                          qn_ref, kn_ref, bd_ref, o_ref, kd_ref, vd_ref):
    bd = bd_ref[...]

    @pl.when(pl.program_id(1) == 0)
    def _():
        for g in range(A_KV_HEADS):
            kd_ref[g, :PAST_LEN] = _dup_half(ck_ref[0], g).astype(BF16)
            vd_ref[g, :PAST_LEN] = _with_ones(_dup_half(cv_ref[0], g))
        rows = 256
        for c in range(DEC_SEQ // rows):
            sl = slice(c * rows, (c + 1) * rows)
            k = _head_rms(kv_ref[sl, :LANES].astype(F32), kn_ref[...], bd)
            k = _rope(k, cos_ref[sl, :], sin_ref[sl, :])
            v = kv_ref[sl, LANES:].astype(F32)
            dst = slice(PAST_LEN + c * rows, PAST_LEN + (c + 1) * rows)
            for g in range(A_KV_HEADS):
                kd_ref[g, dst] = _dup_half(k, g).astype(BF16)
                vd_ref[g, dst] = _with_ones(_dup_half(v, g))

    _gqa_block(q_ref, qn_ref[...], bd, kd_ref, vd_ref, o_ref, rope=(cosq_ref[...], sinq_ref[...]))


def _attn_a_sample(q, kv, ck, cv, cos, sin, qn, kn, bd):
    nq = DEC_SEQ // _A_QBLOCK
    full = lambda a: pl.BlockSpec(a.shape, lambda b, i: (0,) * a.ndim)
    tk = PAST_LEN + DEC_SEQ
    return pl.pallas_call(
        _attn_a_sample_kernel,
        grid=(DEC_BATCH, nq),
        in_specs=[pl.BlockSpec((_A_QBLOCK, A_WIDTH), lambda b, i: (b * nq + i, 0)),
                  pl.BlockSpec((DEC_SEQ, 2 * A_KV_WIDTH), lambda b, i: (b, 0)),
                  pl.BlockSpec((1, PAST_LEN, A_KV_WIDTH), lambda b, i: (b, 0, 0)),
                  pl.BlockSpec((1, PAST_LEN, A_KV_WIDTH), lambda b, i: (b, 0, 0)),
                  full(cos), full(sin),
                  pl.BlockSpec((_A_QBLOCK, LANES), lambda b, i: (i, 0)),
                  pl.BlockSpec((_A_QBLOCK, LANES), lambda b, i: (i, 0)),
                  full(qn), full(kn), full(bd)],
        out_specs=pl.BlockSpec((_A_QBLOCK, A_WIDTH), lambda b, i: (b * nq + i, 0)),
        out_shape=jax.ShapeDtypeStruct((T_SAMPLE, A_WIDTH), BF16),
        scratch_shapes=[pltpu.VMEM((A_KV_HEADS, tk, LANES), BF16),
                        pltpu.VMEM((A_KV_HEADS, tk, 2 * LANES), BF16)],
        compiler_params=_cparams(("arbitrary", "arbitrary")),
        name="attn_a_sample",
    )(q, kv, ck, cv, cos, sin, cos, sin, qn, kn, bd)


def _log_sigmoid(x):
    return -(jnp.maximum(-x, 0.0) + jnp.log1p(jnp.exp(-jnp.abs(x))))


def _mlstm_kernel(*refs, n, has_init, emit_state):
    it = iter(refs)
    q_ref, v_ref, og_ref, kt_ref, gt_ref, brow_ref, onorm_ref, tri_ref = [next(it) for _ in range(8)]
    if has_init:
        c0_ref, m0_ref = [next(it) for _ in range(2)]
    out_ref = next(it)
    if emit_state:
        cT_ref, nT_ref, mT_ref = [next(it) for _ in range(3)]
    h_ref, row_ref, c_ref = [next(it) for _ in range(3)]

    L = MLSTM_CHUNK
    D = B_HEAD_DIM
    nc = n // L
    hb = MLSTM_HEADS_PER_STEP

    lane = lax.broadcasted_iota(jnp.int32, (nc * 8, L), 1)
    is_fwd = lax.broadcasted_iota(jnp.int32, (nc * 8, L), 0) % 8 == 0

    def running_max(x, suffix):
        for sh in (1, 2, 4, 8, 16, 32, 64):
            if suffix:
                x = jnp.where(lane < L - sh, jnp.maximum(x, pltpu.roll(x, L - sh, 1)), x)
            else:
                x = jnp.where(lane >= sh, jnp.maximum(x, pltpu.roll(x, sh, 1)), x)
        return x

    for hh in range(hb):
        gt = gt_ref[:, 16 * hh:16 * hh + 16, :] + brow_ref[hh][None]
        li = gt[:, 0:8, :].reshape(nc * 8, L)
        lf = _log_sigmoid(gt[:, 8:16, :]).reshape(nc * 8, L)
        cum = jnp.where(is_fwd, _dot_exact_rhs(lf, tri_ref[0]), _dot_exact_rhs(lf, tri_ref[1]))
        a = li - cum
        planes = (cum, a, jnp.broadcast_to(lf.sum(axis=-1, keepdims=True), (nc * 8, L)),
                  jnp.where(is_fwd, running_max(a, False), running_max(a, True)),
                  jnp.broadcast_to(a.max(axis=-1, keepdims=True), (nc * 8, L)))
        for p, val in enumerate(planes):
            row_ref[hh, p] = val.reshape(nc, 8, L)

    chains = [(hh, d) for hh in range(hb) for d in range(2)]
    if has_init:
        m_init = []
        for hh, d in chains:
            c_ref[2 * hh + d] = c0_ref[0, d, hh]
            m_init.append(m0_ref[0, hh, d:d + 1, :])
        m_init = tuple(m_init)
    else:
        c_ref[...] = jnp.zeros(c_ref.shape, F32)
        m_init = tuple(jnp.zeros((1, L), F32) for _ in chains)

    t_idx = lax.broadcasted_iota(jnp.int32, (L, L), 0)
    s_idx = lax.broadcasted_iota(jnp.int32, (L, L), 1)
    masks = (s_idx <= t_idx, s_idx >= t_idx)
    ones = jnp.ones((L, L), F32)

    def step(c, hh, d, m):
        r0 = pl.multiple_of(c * L, L)
        hl = slice(hh * L, (hh + 1) * L)
        qb = q_ref[pl.ds(r0, L), hl].astype(BF16)
        kst = kt_ref[c, hl, :].astype(F32) * (D ** -0.5)
        v_ext = jnp.concatenate([v_ref[pl.ds(r0, L), hl], ones], axis=1).astype(BF16)
        cum, a_row, tot, amax_run, amax = [row_ref[hh, p, c][d:d + 1, :] for p in range(5)]
        m_cum = jnp.broadcast_to(cum, (L, L)).T
        m_run = jnp.broadcast_to(amax_run, (L, L)).T
        dlog = jnp.where(masks[d], m_cum + a_row, -jnp.inf)
        inter = m_cum + m
        m_t = jnp.maximum(inter, m_cum + m_run)
        w_in = jnp.exp(dlog - m_t)
        w_st = jnp.exp(inter - m_t)
        a = _dot(qb, kst.astype(BF16)) * w_in
        ci = 2 * hh + d
        cext = c_ref[ci]
        p_state = _dot(qb, cext.astype(BF16))
        p_intra = _dot(a.astype(BF16), v_ext)
        num = w_st * p_state[:, :D] + p_intra[:, :D]
        den = w_st * p_state[:, D:] + p_intra[:, D:]
        h_ref[ci, pl.ds(r0, L), :] = num / jnp.maximum(jnp.abs(den), jnp.exp(-m_t))
        m_new = jnp.maximum(tot + m, amax + tot)
        ws = jnp.exp(a_row + tot - m_new)
        wc = jnp.exp(tot + m - m_new)
        c_ref[ci] = jnp.concatenate([wc, wc], axis=1) * cext + _dot((kst * ws).astype(BF16), v_ext)
        return m_new

    def body(i, carry):
        return tuple(step(i if d == 0 else nc - 1 - i, hh, d, m) for (hh, d), m in zip(chains, carry))

    m_fin = lax.fori_loop(0, nc, body, m_init)

    for hh in range(hb):
        hm = h_ref[2 * hh] + h_ref[2 * hh + 1]
        hl = slice(hh * L, (hh + 1) * L)
        gate = jax.nn.sigmoid(og_ref[:, hl].astype(F32))
        out_ref[:, hl] = (_rms(hm, onorm_ref[hh]) * gate).astype(out_ref.dtype)

    if emit_state:
        for k, (hh, d) in enumerate(chains):
            cext = c_ref[2 * hh + d]
            cT_ref[0, d, hh] = cext[:, :D]
            nT_ref[0, hh, d:d + 1, :] = cext[:, D:].T[0:1, :]
            mT_ref[0, hh, d:d + 1, :] = m_fin[k]


def _mlstm(ob, okt, ogt, brow, onorm, tri, *, n, nseq, init=None, emit_state=False):
    L = MLSTM_CHUNK
    nc = n // L
    H = B_HEADS
    hb = MLSTM_HEADS_PER_STEP
    ng = H // hb
    col = lambda part: (lambda b, g: (b, part * ng + g))
    in_specs = [pl.BlockSpec((n, hb * L), col(0)), pl.BlockSpec((n, hb * L), col(1)),
                pl.BlockSpec((n, hb * L), col(2)),
                pl.BlockSpec((nc, hb * L, L), lambda b, g: (b, g, 0)),
                pl.BlockSpec((nc, 16 * hb, L), lambda b, g: (b, g, 0)),
                pl.BlockSpec((hb, 16, L), lambda b, g: (g, 0, 0)),
                pl.BlockSpec((hb, 1, L), lambda b, g: (g, 0, 0)),
                pl.BlockSpec(tri.shape, lambda b, g: (0, 0, 0))]
    args = [ob, ob, ob, okt, ogt, brow, onorm, tri]
    if init is not None:
        in_specs += [pl.BlockSpec((1, 2, hb, L, 2 * L), lambda b, g: (b, 0, g, 0, 0)),
                     pl.BlockSpec((1, hb, 2, L), lambda b, g: (b, g, 0, 0))]
        args += list(init)
    out_specs = [pl.BlockSpec((n, hb * L), lambda b, g: (b, g))]
    out_shape = [jax.ShapeDtypeStruct((nseq * n, B_WIDTH), BF16)]
    if emit_state:
        out_specs += [pl.BlockSpec((1, 2, hb, L, L), lambda b, g: (b, 0, g, 0, 0)),
                      pl.BlockSpec((1, hb, 2, L), lambda b, g: (b, g, 0, 0)),
                      pl.BlockSpec((1, hb, 2, L), lambda b, g: (b, g, 0, 0))]
        out_shape += [jax.ShapeDtypeStruct((nseq, 2, H, L, L), F32),
                      jax.ShapeDtypeStruct((nseq, H, 2, L), F32),
                      jax.ShapeDtypeStruct((nseq, H, 2, L), F32)]
    return pl.pallas_call(
        functools.partial(_mlstm_kernel, n=n, has_init=init is not None, emit_state=emit_state),
        grid=(nseq, ng),
        in_specs=in_specs,
        out_specs=out_specs,
        out_shape=out_shape,
        scratch_shapes=[pltpu.VMEM((2 * hb, n, L), F32),
                        pltpu.VMEM((hb, 5, nc, 8, L), F32),
                        pltpu.VMEM((2 * hb, L, 2 * L), F32)],
        compiler_params=_cparams(("arbitrary", "arbitrary")),
        name="mlstm_init" if init is not None else "mlstm",
    )(*args)


def _router(logits):
    lane = _lane(logits.shape).astype(F32)
    big = 1e9
    gl = jnp.where(lane < N_GROUPS, logits, -jnp.inf)
    gmax = gl.max(axis=-1, keepdims=True)
    g_sel = jnp.where(gl == gmax, lane, big).min(axis=-1, keepdims=True)
    g_prob = 1.0 / jnp.exp(gl - gmax).sum(axis=-1, keepdims=True)
    lo = N_GROUPS + EXPERTS_PER_GROUP * g_sel
    el = jnp.where(lane >= lo, jnp.where(lane < lo + EXPERTS_PER_GROUP, logits, -jnp.inf), -jnp.inf)
    v1 = el.max(axis=-1, keepdims=True)
    i1 = jnp.where(el == v1, lane, big).min(axis=-1, keepdims=True)
    el2 = jnp.where(lane == i1, -jnp.inf, el)
    v2 = el2.max(axis=-1, keepdims=True)
    i2 = jnp.where(el2 == v2, lane, big).min(axis=-1, keepdims=True)
    e2 = jnp.exp(v2 - v1)
    w1 = g_prob / (1.0 + e2)
    w2 = g_prob * e2 / (1.0 + e2)
    return i1, i2, w1, w2


def _read_tokens(refs, is_prompt, rows):
    if len(refs) == 1:
        return refs[0][rows, :]
    return jnp.where(is_prompt, refs[0][rows, :], refs[1][rows, :])


def _post_kernel(*refs, groups):
    it = iter(refs)
    tok_refs = [[next(it) for _ in range(n)] for n in groups]
    w_refs = [next(it) for _ in range(len(groups) - 1)]
    mod_ref, g_ref, wr_ref, br_ref, ls_ref, sel_ref, su_ref, tokrow_ref = [next(it) for _ in range(8)]
    xnew_ref, xn_ref, rt_ref, rtt_ref, cnt_ref, order_ref, tab_ref = [next(it) for _ in range(7)]
    run_ref = next(it)
    i = pl.program_id(0)
    tm = xnew_ref.shape[0]
    is_prompt = i * tm < T_PROMPT
    r = _mod_row(i * tm)

    @pl.when(i == 0)
    def _():
        run_ref[...] = jnp.zeros(run_ref.shape, F32)

    sub = ls_ref.shape[0]
    for h in range(tm // sub):
        rows = slice(h * sub, (h + 1) * sub)
        acc = None
        for a_refs, w_ref in zip(tok_refs[1:], w_refs):
            d = _dot(_read_tokens(a_refs, is_prompt, rows).astype(BF16), w_ref[...])
            acc = d if acc is None else acc + d
        xnew = _read_tokens(tok_refs[0], is_prompt, rows) + _mod_part(mod_ref, r, 2) * acc
        xnew_ref[rows, :] = xnew
        xn = _rms(xnew, g_ref[...]) * (1.0 + _mod_part(mod_ref, r, 4)) + _mod_part(mod_ref, r, 3)
        _to_token_tiles(xn_ref, xn, h * sub)
        x_hi, x_lo = _split2(xn)
        both = _dot(x_hi, wr_ref[...])
        logits = both[:, :LANES] + both[:, LANES:] + _dot(x_lo, wr_ref[:, :LANES]) + br_ref[...]
        i1, i2, w1, w2 = _router(logits)

        lane = _lane(logits.shape).astype(F32)
        member = jnp.where(lane == i1, 1.0, jnp.where(lane == i2, 1.0, 0.0))
        local = _dot(ls_ref[...], member.astype(BF16))
        run = run_ref[...]
        before = local + run
        rank1 = jnp.where(lane == i1, before, 0.0).sum(axis=-1, keepdims=True)
        rank2 = jnp.where(lane == i2, before, 0.0).sum(axis=-1, keepdims=True)
        n_here = member.sum(axis=0, keepdims=True)
        run_ref[...] = run + n_here

        off = _dot(jnp.broadcast_to(n_here, (8, LANES)).astype(BF16), su_ref[...])[0:1]
        pos = off + local
        p1 = jnp.where(lane == i1, pos, 0.0).sum(axis=-1, keepdims=True)
        p2 = jnp.where(lane == i2, pos, 0.0).sum(axis=-1, keepdims=True)
        slot = lax.broadcasted_iota(jnp.int32, (sub, 2 * sub), 1).astype(F32)
        onehot = jnp.where(slot == p1, 1.0, jnp.where(slot == p2, 1.0, 0.0)).astype(BF16)
        order_ref[:, h * 2 * sub:(h + 1) * 2 * sub] = _dot(tokrow_ref[...], onehot) + (i * tm + h * sub).astype(F32)
        row8 = lax.broadcasted_iota(jnp.int32, (8, LANES), 0)
        tab_ref[h] = jnp.where(row8 == 0, n_here, jnp.where(row8 == 1, run, jnp.where(row8 == 2, off, 0.0)))
        cols = (i1 - N_GROUPS, i2 - N_GROUPS, w1, w2, rank1, rank2)
        rt = jnp.zeros(logits.shape, F32)
        for k, c in enumerate(cols):
            rt = jnp.where(lane == k, c, rt)
        rt_ref[rows, :] = rt
        hi, mid, lo = _split3(rt)
        sel = sel_ref[...]
        rtt_ref[:, rows] = _dot_nt(sel, hi) + _dot_nt(sel, mid) + _dot_nt(sel, lo)
    cnt_ref[...] = run_ref[...]


def _post(tok_ops, w_list, mod, g, wr, br):
    t = T_ALL
    tm = POST_BLOCK
    npb = T_PROMPT // tm
    full = lambda a: pl.BlockSpec(a.shape, lambda i: (0,) * a.ndim)
    specs, args, groups = [], [], []
    for op in tok_ops:
        if isinstance(op, tuple):
            w = op[0].shape[1]
            specs += [pl.BlockSpec((tm, w), lambda i: (jnp.minimum(i, npb - 1), 0)),
                      pl.BlockSpec((tm, w), lambda i: (jnp.maximum(i - npb, 0), 0))]
            args += list(op)
            groups.append(2)
        else:
            specs.append(pl.BlockSpec((tm, op.shape[1]), lambda i: (i, 0)))
            args.append(op)
            groups.append(1)
    sub = TOK_BLOCK
    idx = np.arange(sub)
    ls = jnp.asarray(idx[:, None] > idx[None, :], BF16)
    sel = jnp.asarray(np.arange(8)[:, None] == np.arange(LANES)[None, :], BF16)
    lanes = np.arange(LANES)
    su = jnp.asarray(lanes[:, None] < lanes[None, :], BF16)
    tokrow = jnp.asarray(np.where(np.arange(8)[:, None] == 0, idx[None, :], 0), BF16)
    consts = [mod, g, wr, br, ls, sel, su, tokrow]
    return pl.pallas_call(
        functools.partial(_post_kernel, groups=tuple(groups)),
        grid=(t // tm,),
        in_specs=specs + [full(w) for w in w_list] + [full(a) for a in consts],
        out_specs=[pl.BlockSpec((tm, D_MODEL), lambda i: (i, 0)),
                   pl.BlockSpec((tm * SLABS, LANES), lambda i: (i, 0)),
                   pl.BlockSpec((tm, LANES), lambda i: (i, 0)),
                   pl.BlockSpec((8, tm), lambda i: (0, i)),
                   pl.BlockSpec((1, LANES), lambda i: (0, 0)),
                   pl.BlockSpec((8, 2 * tm), lambda i: (0, i)),
                   pl.BlockSpec((tm // sub, 8, LANES), lambda i: (i, 0, 0))],
        out_shape=[jax.ShapeDtypeStruct((t, D_MODEL), F32),
                   jax.ShapeDtypeStruct((t * SLABS, LANES), U32),
                   jax.ShapeDtypeStruct((t, LANES), F32),
                   jax.ShapeDtypeStruct((8, t), F32),
                   jax.ShapeDtypeStruct((1, LANES), F32),
                   jax.ShapeDtypeStruct((8, 2 * t), F32),
                   jax.ShapeDtypeStruct((t // sub, 8, LANES), F32)],
        scratch_shapes=[pltpu.VMEM((1, LANES), F32)],
        compiler_params=_cparams(("arbitrary",)),
        name="post_mixer_router",
    )(*args, *w_list, *consts)


def _expert_kernel(be_ref, src_ref, tok_ref, nv_ref, xn_hbm, w1_ref, w3_ref, w2_ref, o_ref,
                   xa, xb, sem, w1b, w3b, w2b):
    i = pl.program_id(0)
    nv = nv_ref[0]
    active = i < nv
    rows = EXPERT_ROWS

    def gather(blk, buf, s):
        for r in range(rows):
            tok = tok_ref[src_ref[blk * rows + r]]
            pltpu.make_async_copy(xn_hbm.at[pl.ds(pl.multiple_of(tok * SLABS, SLABS), SLABS), :],
                                  buf.at[pl.ds(r * SLABS, SLABS), :], sem.at[s]).start(priority=r % 2)

    def wait(buf, s):
        pltpu.make_async_copy(xn_hbm.at[pl.ds(0, rows * SLABS), :], buf, sem.at[s]).wait()

    @pl.when(i == 0)
    def _():
        gather(0, xa, 0)

    changed = jnp.logical_or(i == 0, be_ref[i] != be_ref[jnp.maximum(i - 1, 0)])

    @pl.when(jnp.logical_and(changed, active))
    def _():
        w1b[...] = w1_ref[0, 0].astype(BF16)
        w3b[...] = w3_ref[0, 0].astype(BF16)
        w2b[...] = w2_ref[0, 0].astype(BF16)

    nxt = jnp.minimum(i + 1, nv - 1)

    def step(cur, s_cur, oth, s_oth):
        wait(cur, s_cur)
        gather(nxt, oth, s_oth)
        x = _from_token_tiles(cur, 0, rows).astype(BF16)
        h1 = _dot(x, w1b[...])
        h3 = _dot(x, w3b[...])
        hid = (h1 * jax.nn.sigmoid(h1)) * h3
        _to_token_tiles(o_ref, _dot(hid.astype(BF16), w2b[...]))

    @pl.when(jnp.logical_and(active, i % 2 == 0))
    def _():
        step(xa, 0, xb, 1)

    @pl.when(jnp.logical_and(active, i % 2 == 1))
    def _():
        step(xb, 1, xa, 0)

    @pl.when(jnp.logical_not(active))
    def _():
        o_ref[...] = jnp.zeros(o_ref.shape, o_ref.dtype)

    @pl.when(i == nv - 1)
    def _():
        @pl.when(i % 2 == 0)
        def _():
            wait(xb, 1)

        @pl.when(i % 2 == 1)
        def _():
            wait(xa, 0)


def _experts(block_expert, slot_src, pair_tok, n_valid, xn, w1, w3, w2, layer):
    nblk = block_expert.shape[0]
    rows = EXPERT_ROWS
    weights = lambda i, be, *_: (layer, be[i], 0, 0)
    grid_spec = pltpu.PrefetchScalarGridSpec(
        num_scalar_prefetch=4,
        grid=(nblk,),
        in_specs=[pl.BlockSpec(memory_space=pl.ANY),
                  pl.BlockSpec((1, 1, D_MODEL, D_EXPERT), weights),
                  pl.BlockSpec((1, 1, D_MODEL, D_EXPERT), weights),
                  pl.BlockSpec((1, 1, D_EXPERT, D_MODEL), weights)],
        out_specs=pl.BlockSpec((rows * SLABS, LANES), lambda i, *_: (i, 0)),
        scratch_shapes=[pltpu.VMEM((rows * SLABS, LANES), U32),
                        pltpu.VMEM((rows * SLABS, LANES), U32),
                        pltpu.SemaphoreType.DMA((2,)),
                        pltpu.VMEM((D_MODEL, D_EXPERT), BF16),
                        pltpu.VMEM((D_MODEL, D_EXPERT), BF16),
                        pltpu.VMEM((D_EXPERT, D_MODEL), BF16)])
    return pl.pallas_call(
        _expert_kernel,
        grid_spec=grid_spec,
        out_shape=jax.ShapeDtypeStruct((nblk * rows * SLABS, LANES), U32),
        compiler_params=_cparams(("arbitrary",)),
        name="moe_experts",
    )(block_expert, slot_src, pair_tok, n_valid, xn, w1, w3, w2)


def _combine_kernel(*refs, row0, project, final_norm):
    it = iter(refs)
    dest_ref, yb_hbm, x_ref, rt_ref, mod0_ref = [next(it) for _ in range(5)]
    if project:
        mod1_ref, g_ref, w_ref = [next(it) for _ in range(3)]
    if final_norm:
        gfin_ref = next(it)
    xo_ref = next(it)
    if project:
        q_ref, k_ref, v_ref = [next(it) for _ in range(3)]
    ya, yb, sem = [next(it) for _ in range(3)]
    i = pl.program_id(0)
    nblk = pl.num_programs(0)
    tm = x_ref.shape[0]

    def gather(blk, buf, s):
        for j in range(tm):
            for c in range(2):
                d = dest_ref[c * T_ALL + row0 + blk * tm + j]
                pltpu.make_async_copy(yb_hbm.at[pl.ds(pl.multiple_of(d * SLABS, SLABS), SLABS), :],
                                      buf.at[pl.ds((c * tm + j) * SLABS, SLABS), :], sem.at[s]).start(priority=c)

    def wait(buf, s):
        pltpu.make_async_copy(yb_hbm.at[pl.ds(0, 2 * tm * SLABS), :], buf, sem.at[s]).wait()

    @pl.when(i == 0)
    def _():
        gather(0, ya, 0)

    nxt = jnp.minimum(i + 1, nblk - 1)

    def step(cur, s_cur, oth, s_oth):
        wait(cur, s_cur)
        gather(nxt, oth, s_oth)
        r = _mod_row(row0 + i * tm)
        rt = rt_ref[...]
        y = rt[:, 2:3] * _from_token_tiles(cur, 0, tm) + rt[:, 3:4] * _from_token_tiles(cur, tm, tm)
        x = x_ref[...] + _mod_part(mod0_ref, r, 5) * y
        xo_ref[...] = _rms(x, gfin_ref[...]) if final_norm else x
        if project:
            h = _rms(x, g_ref[...]) * (1.0 + _mod_part(mod1_ref, r, 1)) + _mod_part(mod1_ref, r, 0)
            hb = h.astype(BF16)
            for j, o_ref in enumerate((q_ref, k_ref, v_ref)):
                o_ref[...] = _dot(hb, w_ref[:, j * C_WIDTH:(j + 1) * C_WIDTH]).astype(o_ref.dtype)

    @pl.when(i % 2 == 0)
    def _():
        step(ya, 0, yb, 1)

    @pl.when(i % 2 == 1)
    def _():
        step(yb, 1, ya, 0)

    @pl.when(i == nblk - 1)
    def _():
        @pl.when(i % 2 == 0)
        def _():
            wait(yb, 1)

        @pl.when(i % 2 == 1)
        def _():
            wait(ya, 0)


def _combine(dest, yb, x, rt, mod0, row0, t, proj=None, gfin=None):
    tm = TOK_BLOCK
    blk0 = row0 // tm
    full = lambda a: pl.BlockSpec(a.shape, lambda i, d: (0,) * a.ndim)
    extra = list(proj) if proj is not None else []
    if gfin is not None:
        extra.append(gfin)
    n_out = 4 if proj is not None else 1
    out_shape = [jax.ShapeDtypeStruct((t, D_MODEL), F32)] + [jax.ShapeDtypeStruct((t, C_WIDTH), BF16)] * (n_out - 1)
    grid_spec = pltpu.PrefetchScalarGridSpec(
        num_scalar_prefetch=1,
        grid=(t // tm,),
        in_specs=[pl.BlockSpec(memory_space=pl.ANY),
                  pl.BlockSpec((tm, D_MODEL), lambda i, d: (blk0 + i, 0)),
                  pl.BlockSpec((tm, LANES), lambda i, d: (blk0 + i, 0)),
                  full(mod0)] + [full(a) for a in extra],
        out_specs=[pl.BlockSpec((tm, D_MODEL), lambda i, d: (i, 0))] * n_out,
        scratch_shapes=[pltpu.VMEM((2 * tm * SLABS, LANES), U32),
                        pltpu.VMEM((2 * tm * SLABS, LANES), U32),
                        pltpu.SemaphoreType.DMA((2,))])
    return pl.pallas_call(
        functools.partial(_combine_kernel, row0=row0, project=proj is not None, final_norm=gfin is not None),
        grid_spec=grid_spec,
        out_shape=out_shape,
        compiler_params=_cparams(("arbitrary",)),
        name="moe_combine_proj" if proj is not None else "moe_combine",
    )(dest, yb, x, rt, mod0, *extra)


def _moe_plan(rtt, cnt, order, tab):
    t = rtt.shape[1]
    eid = rtt[0:2].astype(jnp.int32)
    rank = rtt[4:6].astype(jnp.int32)
    counts = cnt[0, N_GROUPS:N_GROUPS + N_EXPERTS].astype(jnp.int32)
    padded = (counts + EXPERT_ROWS - 1) // EXPERT_ROWS * EXPERT_ROWS
    seg_end = jnp.cumsum(padded)
    seg_start = seg_end - padded
    experts = jnp.arange(N_EXPERTS, dtype=jnp.int32)
    start = jnp.sum(jnp.where(eid[..., None] == experts, seg_start, 0), axis=-1)
    dest = (start + rank).reshape(-1)
    n_blocks = (2 * t + N_EXPERTS * (EXPERT_ROWS - 1) + EXPERT_ROWS - 1) // EXPERT_ROWS
    first_row = jnp.arange(n_blocks, dtype=jnp.int32) * EXPERT_ROWS
    block_expert = jnp.minimum(jnp.sum((seg_end[None, :] <= first_row[:, None]).astype(jnp.int32), axis=1),
                               N_EXPERTS - 1)
    n_valid = (seg_end[-1:] // EXPERT_ROWS).astype(jnp.int32)

    nsb = tab.shape[0]
    n_sb = tab[:, 0, N_GROUPS:N_GROUPS + N_EXPERTS].astype(jnp.int32).T[block_expert]
    run_sb = tab[:, 1, N_GROUPS:N_GROUPS + N_EXPERTS].astype(jnp.int32).T[block_expert]
    off_sb = tab[:, 2, N_GROUPS:N_GROUPS + N_EXPERTS].astype(jnp.int32).T[block_expert]
    rank = (first_row - seg_start[block_expert])[:, None] + jnp.arange(EXPERT_ROWS, dtype=jnp.int32)[None, :]
    real = rank < counts[block_expert][:, None]
    sb = jnp.sum(((run_sb + n_sb)[:, None, :] <= rank[:, :, None]).astype(jnp.int32), axis=-1)
    sb = jnp.minimum(sb, nsb - 1)
    pick = sb[:, :, None] == jnp.arange(nsb, dtype=jnp.int32)[None, None, :]
    shift = jnp.sum(jnp.where(pick, (off_sb - run_sb)[:, None, :], 0), axis=-1)
    src = jnp.clip(sb * (2 * TOK_BLOCK) + shift + rank, 0, 2 * t - 1)
    slot = first_row[:, None] + jnp.arange(EXPERT_ROWS, dtype=jnp.int32)[None, :]
    slot_src = jnp.where(real, src, slot % (2 * t)).reshape(-1)
    return dest, slot_src, order[0].astype(jnp.int32), block_expert, n_valid


def _attn_c_prompt_kernel(q_ref, k_ref, v_ref, o_ref):
    for p in range(C_HEADS // 2):
        sl = slice(p * LANES, (p + 1) * LANES)
        kb = k_ref[:, sl].astype(BF16)
        vb = _with_ones(v_ref[:, sl])
        qp = q_ref[:, sl].astype(F32) * (C_HEAD_DIM ** -0.5)
        o_ref[:, sl] = _softmax_pair(qp, lambda qm: [_dot_nt(qm, kb)],
                                     lambda ps: _dot(ps[0], vb)).astype(o_ref.dtype)


def _attn_c_prompt(q, k, v):
    blk = pl.BlockSpec((SEQ, C_WIDTH), lambda b: (b, 0))
    return pl.pallas_call(
        _attn_c_prompt_kernel,
        grid=(BATCH,),
        in_specs=[blk, blk, blk],
        out_specs=blk,
        out_shape=jax.ShapeDtypeStruct((T_PROMPT, C_WIDTH), BF16),
        compiler_params=_cparams(("arbitrary",)),
        name="attn_c_prompt",
    )(q, k, v)


def _na_key_start(r0):
    rows = DEC_SEQ // GRID_W
    return jnp.minimum(jnp.clip(r0 - NA_ROWS // 2, 0, rows - NA_ROWS), rows - NA_KROWS)


def _na_block_plan():
    rows = DEC_SEQ // GRID_W
    nblk = rows // NA_QROWS
    plan = []
    for blk in (0, 1, nblk - 1):
        r0 = blk * NA_QROWS
        ks = min(int(np.clip(r0 - NA_ROWS // 2, 0, rows - NA_ROWS)), rows - NA_KROWS)
        per_row = []
        for i in range(NA_QROWS):
            r = r0 + i
            rs = int(np.clip(r - NA_ROWS // 2, 0, rows - NA_ROWS))
            start = ks - r + NA_ROWS - 1 + NA_KROWS
            ok = [rs <= ks + j < rs + NA_ROWS for j in range(NA_KROWS)]
            per_row.append((start, ok))
        plan.append(per_row)
    return plan


def _attn_c_sample_kernel(q_ref, k_ref, v_ref, ck_ref, cv_ref, toe_ref, o_ref, bias_ref):
    rows = DEC_SEQ // GRID_W
    nblk = rows // NA_QROWS
    w = GRID_W

    @pl.when(jnp.logical_and(pl.program_id(1) == 0, pl.program_id(2) == 0))
    def _():
        neg = jnp.full((w, w), NEG, F32)
        for t, per_row in enumerate(_na_block_plan()):
            for half in range(2):
                for i, (start, ok) in enumerate(per_row):
                    for j in range(0, NA_KROWS, 2):
                        pieces = [toe_ref[0, half, start + jj] if ok[jj] else neg for jj in (j, j + 1)]
                        bias_ref[t, half, i * w:(i + 1) * w, j * w:(j + 2) * w] = jnp.concatenate(pieces, axis=1)

    i = pl.program_id(2)
    r0 = i * NA_QROWS
    k0 = pl.multiple_of(_na_key_start(r0) * GRID_W, GRID_W)
    btype = jnp.where(i == 0, 0, jnp.where(i == nblk - 1, 2, 1))
    nk = NA_KROWS * GRID_W
    kw = k_ref[pl.ds(k0, nk), :].astype(BF16)
    vw = v_ref[pl.ds(k0, nk), :].astype(BF16)
    kc = ck_ref[0].astype(BF16)
    vc = cv_ref[0].astype(BF16)
    qp = q_ref[...].astype(F32) * (C_HEAD_DIM ** -0.5)
    lo = _lane(qp.shape) < 64
    outs = []
    for half in range(2):
        qm = jnp.where(lo if half == 0 else jnp.logical_not(lo), qp, 0.0).astype(BF16)
        s_win = _dot_nt(qm, kw) + bias_ref[btype, half]
        s_ctx = _dot_nt(qm, kc)
        m = jnp.maximum(s_win.max(axis=-1, keepdims=True), s_ctx.max(axis=-1, keepdims=True))
        e_win = jnp.exp(s_win - m)
        e_ctx = jnp.exp(s_ctx - m)
        l = e_win.sum(axis=-1, keepdims=True) + e_ctx.sum(axis=-1, keepdims=True)
        o = _dot(e_win.astype(BF16), vw) + _dot(e_ctx.astype(BF16), vc)
        outs.append(o / l)
    o_ref[...] = jnp.where(lo, outs[0], outs[1]).astype(o_ref.dtype)


def _attn_c_sample(q, k, v, ck, cv, toe):
    rows = DEC_SEQ // GRID_W
    nblk = rows // NA_QROWS
    qrows = NA_QROWS * GRID_W
    npair = C_HEADS // 2
    return pl.pallas_call(
        _attn_c_sample_kernel,
        grid=(npair, DEC_BATCH, nblk),
        in_specs=[pl.BlockSpec((qrows, LANES), lambda p, b, i: (b * nblk + i, p)),
                  pl.BlockSpec((DEC_SEQ, LANES), lambda p, b, i: (b, p)),
                  pl.BlockSpec((DEC_SEQ, LANES), lambda p, b, i: (b, p)),
                  pl.BlockSpec((1, PAST_LEN, LANES), lambda p, b, i: (b, 0, p)),
                  pl.BlockSpec((1, PAST_LEN, LANES), lambda p, b, i: (b, 0, p)),
                  pl.BlockSpec((1,) + toe.shape[1:], lambda p, b, i: (p, 0, 0, 0, 0))],
        out_specs=pl.BlockSpec((qrows, LANES), lambda p, b, i: (b * nblk + i, p)),
        out_shape=jax.ShapeDtypeStruct((T_SAMPLE, C_WIDTH), BF16),
        scratch_shapes=[pltpu.VMEM((3, 2, qrows, NA_KROWS * GRID_W), F32)],
        compiler_params=_cparams(("arbitrary", "arbitrary", "arbitrary")),
        name="attn_c_sample",
    )(q, k, v, ck, cv, toe)


def _na_toeplitz(rpb):
    w = GRID_W
    nd_r, nd_c = 2 * NA_ROWS - 1, 2 * NA_COLS - 1
    c = np.arange(w)
    cs = np.clip(c - NA_COLS // 2, 0, w - NA_COLS)
    col_ok = (c[None, :] >= cs[:, None]) & (c[None, :] < cs[:, None] + NA_COLS)
    dcol = c[None, :] - c[:, None] + NA_COLS - 1
    onehot = (np.arange(nd_c)[:, None, None] == dcol[None]).reshape(nd_c, w * w)
    toe = jnp.dot(rpb.reshape(C_HEADS * nd_r, nd_c), jnp.asarray(onehot, F32), precision=lax.Precision.HIGHEST)
    toe = jnp.where(col_ok[None, None], toe.reshape(C_HEADS, nd_r, w, w), NEG)
    toe = jnp.pad(toe, ((0, 0), (NA_KROWS, NA_KROWS), (0, 0), (0, 0)), constant_values=NEG)
    return toe.reshape(C_HEADS // 2, 2, nd_r + 2 * NA_KROWS, w, w)


def _rope_tables():
    half = A_HEAD_DIM // 2
    t = jnp.arange(DEC_SEQ)
    row = (t // GRID_W).astype(F32)
    colp = (t % GRID_W).astype(F32)
    freqs = 1.0 / (ROPE_BASE ** (jnp.arange(0, half, 2, dtype=F32) / half))
    d = np.arange(LANES) % A_HEAD_DIM
    pos = jnp.where(jnp.asarray(d < half)[None, :], row[:, None], colp[:, None])
    ang = pos * freqs[d % (half // 2)][None, :]
    sign = jnp.asarray(np.where((d % half) < half // 2, -1.0, 1.0), F32)[None, :]
    return jnp.cos(ang), jnp.sin(ang) * sign


def _head_avg_matrix():
    idx = np.arange(LANES) // A_HEAD_DIM
    return jnp.asarray((idx[:, None] == idx[None, :]).astype(np.float32) / A_HEAD_DIM, BF16)


def _tri_matrices():
    i = np.arange(MLSTM_CHUNK)
    upper = (i[:, None] <= i[None, :]).astype(np.float32)
    lower = (i[:, None] >= i[None, :]).astype(np.float32)
    return jnp.asarray(np.stack([upper, lower]), BF16)


def _router_weights(wg, bg, we, be):
    w = jnp.zeros((D_MODEL, LANES), F32).at[:, :N_GROUPS].set(wg).at[:, N_GROUPS:N_GROUPS + N_EXPERTS].set(we)
    b = jnp.zeros((1, LANES), F32).at[0, :N_GROUPS].set(bg).at[0, N_GROUPS:N_GROUPS + N_EXPERTS].set(be)
    hi = w.astype(BF16)
    lo = (w - hi.astype(F32)).astype(BF16)
    return jnp.concatenate([hi, lo], axis=1), b


def _moe(xn, rtt, cnt, order, tab, w1, w3, w2, layer):
    dest, slot_src, pair_tok, block_expert, n_valid = _moe_plan(rtt, cnt, order, tab)
    return dest, _experts(block_expert, slot_src, pair_tok, n_valid, xn, w1, w3, w2, layer)


def kernel(x_prompt, x_sample, cache_attn_k, cache_attn_v, state_mlstm_C, state_mlstm_n, state_mlstm_m,
           cache_na_k, cache_na_v, c, c_ctx, norm_mix, norm_ffn, norm_final, ada_w, ada_b,
           ab_w_in, ab_w_out, ab_q_norm, ab_k_norm, ab_gate_bias, ab_out_norm,
           na_w_in, na_w_out, na_rpb, moe_wg, moe_bg, moe_we, moe_be, moe_w1, moe_w3, moe_w2):
    xp = x_prompt.reshape(T_PROMPT, D_MODEL)
    xs = x_sample.reshape(T_SAMPLE, D_MODEL)
    cond =jnp.zeros((N_COND, D_MODEL), F32).at[0].set(c_ctx).at[1:1 + DEC_BATCH].set(c)
    mod = _modulation(cond, ada_w, ada_b)
    gfin = norm_final.reshape(1, D_MODEL)

    w_in = ab_w_in[0]
    o_aq, o_ak, o_av, o_bq, o_bk, o_bv, o_bo, o_bg = np.cumsum((0,) + (A_WIDTH, A_KV_WIDTH, A_KV_WIDTH,
                                                                       B_WIDTH, B_WIDTH, B_WIDTH, B_WIDTH))
    wb = jnp.concatenate([w_in[:, o_bq:o_bk], w_in[:, o_bv:o_bg]], axis=1).astype(BF16)
    wq = w_in[:, o_aq:o_ak].astype(BF16)
    wkv = w_in[:, o_ak:o_bq].astype(BF16)
    gate_rows = np.array([0, 8, 1, 9])
    wg = w_in[:, o_bg:o_bg + 4 * B_HEADS].reshape(D_MODEL, 4, B_HEADS)
    wgt = jnp.zeros((B_HEADS, 16, D_MODEL), F32).at[:, gate_rows, :].set(wg.transpose(2, 1, 0))
    wgt = jnp.concatenate([w_in[:, o_bk:o_bv].T, wgt.reshape(16 * B_HEADS, D_MODEL)], axis=0).astype(BF16)
    g_mix = norm_mix[0].reshape(1, D_MODEL)
    ob_p, oq_p, okv_p, okt_p, ogt_p = _proj_ab(xp, mod[0], g_mix, wb, wq, wkv, wgt, 0)
    ob_s, oq_s, okv_s, okt_s, ogt_s = _proj_ab(xs, mod[0], g_mix, wb, wq, wkv, wgt, T_PROMPT)

    qn = jnp.tile(ab_q_norm[0], 2).reshape(1, LANES)
    kn = jnp.tile(ab_k_norm[0], 2).reshape(1, LANES)
    bd = _head_avg_matrix()
    cos, sin = _rope_tables()
    a_p, new_k, new_v = _attn_a_prompt(oq_p, okv_p, qn, kn, bd)
    ck = cache_attn_k[:, 0].reshape(DEC_BATCH, PAST_LEN, A_KV_WIDTH)
    cv = cache_attn_v[:, 0].reshape(DEC_BATCH, PAST_LEN, A_KV_WIDTH)
    a_s = _attn_a_sample(oq_s, okv_s, ck, cv, cos, sin, qn, kn, bd)

    gb = ab_gate_bias[0]
    brow = jnp.zeros((B_HEADS, 16, LANES), F32).at[:, gate_rows, :].set(
        jnp.broadcast_to(gb.T[:, :, None], (B_HEADS, 4, LANES)))
    onorm = ab_out_norm[0].reshape(B_HEADS, 1, B_HEAD_DIM)
    tri = _tri_matrices()
    b_p, cT, nT, mT = _mlstm(ob_p, okt_p, ogt_p, brow, onorm, tri, n=SEQ, nseq=BATCH, emit_state=True)
    n0 = jnp.broadcast_to(state_mlstm_n[:, 0][..., None], state_mlstm_C[:, 0].shape)
    c0 = jnp.concatenate([state_mlstm_C[:, 0], n0], axis=-1)
    m0 = jnp.broadcast_to(state_mlstm_m[:, 0].transpose(0, 2, 1)[..., None], (DEC_BATCH, B_HEADS, 2, LANES))
    (b_s,) = _mlstm(ob_s, okt_s, ogt_s, brow, onorm, tri, n=DEC_SEQ, nseq=DEC_BATCH, init=(c0, m0))

    w_out = ab_w_out[0].astype(BF16)
    wr, br = _router_weights(moe_wg[0], moe_bg[0], moe_we[0], moe_be[0])
    x1, xn, rt, *plan = _post([(xp, xs), (a_p, a_s), (b_p, b_s)], [w_out[:A_WIDTH], w_out[A_WIDTH:]], mod[0],
                              norm_ffn[0].reshape(1, D_MODEL), wr, br)
    dest, yb = _moe(xn, *plan, moe_w1, moe_w3, moe_w2, 0)

    g_mix = norm_mix[1].reshape(1, D_MODEL)
    w_in = na_w_in[0].astype(BF16)
    proj = (mod[1], g_mix, w_in)
    x_p, q_p, k_p, v_p = _combine(dest, yb, x1, rt, mod[0], 0, T_PROMPT, proj=proj)
    x_s, q_s, k_s, v_s = _combine(dest, yb, x1, rt, mod[0], T_PROMPT, T_SAMPLE, proj=proj)
    o_p = _attn_c_prompt(q_p, k_p, v_p)
    nck = cache_na_k[:, 0].reshape(DEC_BATCH, PAST_LEN, C_WIDTH)
    ncv = cache_na_v[:, 0].reshape(DEC_BATCH, PAST_LEN, C_WIDTH)
    o_s = _attn_c_sample(q_s, k_s, v_s, nck, ncv, _na_toeplitz(na_rpb[0]))
    wr, br = _router_weights(moe_wg[1], moe_bg[1], moe_we[1], moe_be[1])
    x1, xn, rt, *plan = _post([(x_p, x_s), (o_p, o_s)], [na_w_out[0].astype(BF16)], mod[1],
                              norm_ffn[1].reshape(1, D_MODEL), wr, br)
    dest, yb = _moe(xn, *plan, moe_w1, moe_w3, moe_w2, 1)
    (y_prompt,) = _combine(dest, yb, x1, rt, mod[1], 0, T_PROMPT, gfin=gfin)
    (y_sample,) = _combine(dest, yb, x1, rt, mod[1], T_PROMPT, T_SAMPLE, gfin=gfin)
    y_prompt = y_prompt.reshape(BATCH, SEQ, D_MODEL)
    y_sample = y_sample.reshape(DEC_BATCH, DEC_SEQ, D_MODEL)
    new_attn_k = new_k.reshape(BATCH, 1, SEQ, A_KV_HEADS, A_HEAD_DIM)
    new_attn_v = new_v.reshape(BATCH, 1, SEQ, A_KV_HEADS, A_HEAD_DIM)
    new_mlstm_C = cT[:, None]
    new_mlstm_n = nT.transpose(0, 2, 1, 3)[:, None]
    new_mlstm_m = mT[..., 0].transpose(0, 2, 1)[:, None]
    new_na_k = k_p.astype(F32).reshape(BATCH, 1, SEQ, C_HEADS, C_HEAD_DIM)
    new_na_v = v_p.astype(F32).reshape(BATCH, 1, SEQ, C_HEADS, C_HEAD_DIM)
    return (y_prompt, y_sample, new_attn_k, new_attn_v, new_mlstm_C, new_mlstm_n, new_mlstm_m,
            new_na_k, new_na_v)
```

```python
import functools

import numpy as np
import jax
import jax.numpy as jnp
from jax import lax
from jax.experimental import pallas as pl
from jax.experimental.pallas import tpu as pltpu

F32 = jnp.float32
BF16 = jnp.bfloat16

D_MODEL = 1024
BATCH = 32
SEQ = 256
DEC_BATCH = 4
DEC_SEQ = 2048
PAST_LEN = 256
GRID_W = 64
A_HEADS = 8
A_KV_HEADS = 2
A_HEAD_DIM = 64
A_WIDTH = A_HEADS * A_HEAD_DIM
A_KV_WIDTH = A_KV_HEADS * A_HEAD_DIM
B_HEADS = 4
B_HEAD_DIM = 128
B_WIDTH = B_HEADS * B_HEAD_DIM
MLSTM_CHUNK = 128
C_HEADS = 16
C_HEAD_DIM = 64
C_WIDTH = C_HEADS * C_HEAD_DIM
NA_ROWS = 8
NA_COLS = 16
N_GROUPS = 4
EXPERTS_PER_GROUP = 8
N_EXPERTS = N_GROUPS * EXPERTS_PER_GROUP
D_EXPERT = 512
ROPE_BASE = 10000.0
NORM_EPS = 1e-6

T_PROMPT = BATCH * SEQ
T_SAMPLE = DEC_BATCH * DEC_SEQ
T_ALL = T_PROMPT + T_SAMPLE
N_COND = 8
LANES = 128
SLABS = D_MODEL // LANES // 2
U32 = jnp.uint32
TOK_BLOCK = 256
POST_BLOCK = 2 * TOK_BLOCK
MLSTM_HEADS_PER_STEP = 4
EXPERT_ROWS = 512
NA_QROWS = 4
NA_KROWS = 12
NEG = -1e30
V7X_VMEM_BYTES = 64 * 1024 * 1024
VMEM_LIMIT = V7X_VMEM_BYTES - 8 * 1024 * 1024


def _cparams(sem):
    return pltpu.CompilerParams(dimension_semantics=sem, vmem_limit_bytes=VMEM_LIMIT)


def _split2(x):
    hi = x.astype(BF16)
    lo = (x - hi.astype(F32)).astype(BF16)
    return hi, lo


def _split3(x):
    hi = x.astype(BF16)
    r = x - hi.astype(F32)
    mid = r.astype(BF16)
    lo = (r - mid.astype(F32)).astype(BF16)
    return hi, mid, lo


def _dot(a, b):
    return jnp.dot(a, b, preferred_element_type=F32)


def _dot_nt(a, b):
    return lax.dot_general(a, b, (((1,), (1,)), ((), ())), preferred_element_type=F32)


def _dot_exact_rhs(x, b):
    hi, mid, lo = _split3(x)
    return _dot(hi, b) + _dot(mid, b) + _dot(lo, b)


def _rms(x, g):
    ms = jnp.mean(x * x, axis=-1, keepdims=True)
    return (x * lax.rsqrt(ms + NORM_EPS)) * g


def _mod_row(tok0):
    return jnp.where(tok0 < T_PROMPT, 0, 1 + (tok0 - T_PROMPT) // DEC_SEQ)


def _mod_part(mod_ref, r, idx):
    return mod_ref[pl.ds(r, 1), idx * D_MODEL:(idx + 1) * D_MODEL]


def _head_rms(x, w, bd):
    hi, lo = _split2(x * x)
    ms = _dot(hi, bd) + _dot(lo, bd)
    return (x * lax.rsqrt(ms + NORM_EPS)) * w


def _to_token_tiles(ref, x, tile0=0):
    m = x.shape[0]
    bits = lambda v: pltpu.bitcast(v.astype(BF16).astype(F32), U32)
    for s in range(SLABS):
        lo = bits(x[:, s * LANES:(s + 1) * LANES]) >> 16
        hi = bits(x[:, (s + SLABS) * LANES:(s + SLABS + 1) * LANES]) & U32(0xFFFF0000)
        ref[pl.ds(tile0 * SLABS + s, m, stride=SLABS), :] = lo | hi


def _from_token_tiles(ref, tile0, m):
    lo, hi = [], []
    for s in range(SLABS):
        w = ref[pl.ds(tile0 * SLABS + s, m, stride=SLABS), :]
        lo.append(pltpu.bitcast(w << 16, F32))
        hi.append(pltpu.bitcast(w & U32(0xFFFF0000), F32))
    return jnp.concatenate(lo + hi, axis=1)


def _lane(shape):
    return lax.broadcasted_iota(jnp.int32, shape, len(shape) - 1)


def _dup_half(x, g):
    xr = pltpu.roll(x, 64, 1)
    lo = _lane(x.shape) < 64
    return jnp.where(lo, x, xr) if g == 0 else jnp.where(lo, xr, x)


def _rope(x, cos, sin_signed):
    lane = _lane(x.shape)
    partner = jnp.where((lane % 32) < 16, pltpu.roll(x, LANES - 16, 1), pltpu.roll(x, 16, 1))
    return x * cos + partner * sin_signed


def _with_ones(v):
    return jnp.concatenate([v.astype(BF16), jnp.ones(v.shape, BF16)], axis=1)


def _softmax_pair(qp, score_fn, value_fn):
    lo = _lane(qp.shape) < 64
    outs = []
    for half in range(2):
        qm = jnp.where(lo if half == 0 else jnp.logical_not(lo), qp, 0.0).astype(BF16)
        ss = score_fn(qm)
        m = ss[0].max(axis=-1, keepdims=True)
        for s in ss[1:]:
            m = jnp.maximum(m, s.max(axis=-1, keepdims=True))
        o = value_fn([jnp.exp(s - m).astype(BF16) for s in ss])
        outs.append(o[:, :LANES] / o[:, LANES:])
    return jnp.where(lo, outs[0], outs[1])


def _mod_kernel(cond_ref, w_ref, b_ref, o_ref):
    c = cond_ref[...]
    s = c * jax.nn.sigmoid(c)
    s_hi, s_lo = _split2(s)
    w_hi, w_lo = _split2(w_ref[0])
    o_ref[0] = _dot(s_hi, w_hi) + _dot(s_lo, w_hi) + _dot(s_hi, w_lo) + b_ref[0]


def _modulation(cond, ada_w, ada_b):
    depth, d, n = ada_w.shape
    tn = 1536
    return pl.pallas_call(
        _mod_kernel,
        grid=(depth, n // tn),
        in_specs=[pl.BlockSpec((N_COND, d), lambda l, j: (0, 0)),
                  pl.BlockSpec((1, d, tn), lambda l, j: (l, 0, j)),
                  pl.BlockSpec((1, 1, tn), lambda l, j: (l, 0, j))],
        out_specs=pl.BlockSpec((1, N_COND, tn), lambda l, j: (l, 0, j)),
        out_shape=jax.ShapeDtypeStruct((depth, N_COND, n), F32),
        compiler_params=_cparams(("arbitrary", "arbitrary")),
        name="adaln_modulation",
    )(cond, ada_w, ada_b.reshape(depth, 1, n))


def _norm_mod(x_ref, mod_ref, g_ref, shift_idx, scale_idx, row0):
    r = _mod_row(row0 + pl.program_id(0) * x_ref.shape[0])
    h = _rms(x_ref[...], g_ref[...])
    return h * (1.0 + _mod_part(mod_ref, r, scale_idx)) + _mod_part(mod_ref, r, shift_idx)


def _proj_ab_kernel(x_ref, mod_ref, g_ref, wb_ref, wq_ref, wkv_ref, wgt_ref,
                    ob_ref, oq_ref, okv_ref, okt_ref, ogt_ref, *, row0):
    tm = x_ref.shape[0]
    hb = _norm_mod(x_ref, mod_ref, g_ref, 0, 1, row0).astype(BF16)
    ob_ref[...] = _dot(hb, wb_ref[...]).astype(ob_ref.dtype)
    oq_ref[...] = _dot(hb, wq_ref[...]).astype(oq_ref.dtype)
    okv_ref[...] = _dot(hb, wkv_ref[...]).astype(okv_ref.dtype)
    gt = _dot_nt(wgt_ref[...], hb)
    for j in range(tm // LANES):
        okt_ref[j] = gt[:B_WIDTH, j * LANES:(j + 1) * LANES].astype(okt_ref.dtype)
        ogt_ref[j] = gt[B_WIDTH:, j * LANES:(j + 1) * LANES]


def _proj_ab(x, mod, g, wb, wq, wkv, wgt, row0):
    t = x.shape[0]
    tm = TOK_BLOCK
    full = lambda a: pl.BlockSpec(a.shape, lambda i: (0,) * a.ndim)
    return pl.pallas_call(
        functools.partial(_proj_ab_kernel, row0=row0),
        grid=(t // tm,),
        in_specs=[pl.BlockSpec((tm, D_MODEL), lambda i: (i, 0)), full(mod), full(g),
                  full(wb), full(wq), full(wkv), full(wgt)],
        out_specs=[pl.BlockSpec((tm, wb.shape[1]), lambda i: (i, 0)),
                   pl.BlockSpec((tm, A_WIDTH), lambda i: (i, 0)),
                   pl.BlockSpec((tm, 2 * A_KV_WIDTH), lambda i: (i, 0)),
                   pl.BlockSpec((tm // LANES, B_WIDTH, LANES), lambda i: (i, 0, 0)),
                   pl.BlockSpec((tm // LANES, wgt.shape[0] - B_WIDTH, LANES), lambda i: (i, 0, 0))],
        out_shape=[jax.ShapeDtypeStruct((t, wb.shape[1]), BF16),
                   jax.ShapeDtypeStruct((t, A_WIDTH), BF16),
                   jax.ShapeDtypeStruct((t, 2 * A_KV_WIDTH), BF16),
                   jax.ShapeDtypeStruct((t // LANES, B_WIDTH, LANES), BF16),
                   jax.ShapeDtypeStruct((t // LANES, wgt.shape[0] - B_WIDTH, LANES), F32)],
        compiler_params=_cparams(("arbitrary",)),
        name="proj_ab",
    )(x, mod, g, wb, wq, wkv, wgt)


def _gqa_block(q_ref, qn, bd, kd_ref, vd_ref, o_ref, rope=None):
    for p in range(A_HEADS // 2):
        g = p // (A_HEADS // 2 // A_KV_HEADS)
        qp = _head_rms(q_ref[:, p * LANES:(p + 1) * LANES].astype(F32), qn, bd)
        if rope is not None:
            qp = _rope(qp, rope[0], rope[1])
        qp = qp * (A_HEAD_DIM ** -0.5)
        o_ref[:, p * LANES:(p + 1) * LANES] = _softmax_pair(
            qp, lambda qm: [_dot_nt(qm, kd_ref[g])], lambda ps: _dot(ps[0], vd_ref[g])).astype(o_ref.dtype)


def _attn_a_prompt_kernel(q_ref, kv_ref, qn_ref, kn_ref, bd_ref, o_ref, knew_ref, vnew_ref, kd_ref, vd_ref):
    bd = bd_ref[...]
    k = _head_rms(kv_ref[:, :LANES].astype(F32), kn_ref[...], bd)
    v = kv_ref[:, LANES:].astype(F32)
    knew_ref[...] = k
    vnew_ref[...] = v
    for g in range(A_KV_HEADS):
        kd_ref[g] = _dup_half(k, g).astype(BF16)
        vd_ref[g] = _with_ones(_dup_half(v, g))
    _gqa_block(q_ref, qn_ref[...], bd, kd_ref, vd_ref, o_ref)


def _attn_a_prompt(q, kv, qn, kn, bd):
    nb = BATCH
    full = lambda a: pl.BlockSpec(a.shape, lambda b: (0,) * a.ndim)
    return pl.pallas_call(
        _attn_a_prompt_kernel,
        grid=(nb,),
        in_specs=[pl.BlockSpec((SEQ, A_WIDTH), lambda b: (b, 0)),
                  pl.BlockSpec((SEQ, 2 * A_KV_WIDTH), lambda b: (b, 0)),
                  full(qn), full(kn), full(bd)],
        out_specs=[pl.BlockSpec((SEQ, A_WIDTH), lambda b: (b, 0)),
                   pl.BlockSpec((SEQ, A_KV_WIDTH), lambda b: (b, 0)),
                   pl.BlockSpec((SEQ, A_KV_WIDTH), lambda b: (b, 0))],
        out_shape=[jax.ShapeDtypeStruct((T_PROMPT, A_WIDTH), BF16),
                   jax.ShapeDtypeStruct((T_PROMPT, A_KV_WIDTH), F32),
                   jax.ShapeDtypeStruct((T_PROMPT, A_KV_WIDTH), F32)],
        scratch_shapes=[pltpu.VMEM((A_KV_HEADS, SEQ, LANES), BF16),
                        pltpu.VMEM((A_KV_HEADS, SEQ, 2 * LANES), BF16)],
        compiler_params=_cparams(("arbitrary",)),
        name="attn_a_prompt",
    )(q, kv, qn, kn, bd)


_A_QBLOCK = 256


def _attn_a_sample_kernel(q_ref, kv_ref, ck_ref, cv_ref, cos_ref, sin_ref, cosq_ref, sinq_ref,
                          qn_ref, kn_ref, bd_ref, o_ref, kd_ref, vd_ref):
    bd = bd_ref[...]

    @pl.when(pl.program_id(1) == 0)
    def _():
        for g in range(A_KV_HEADS):
            kd_ref[g, :PAST_LEN] = _dup_half(ck_ref[0], g).astype(BF16)
            vd_ref[g, :PAST_LEN] = _with_ones(_dup_half(cv_ref[0], g))
        rows = 256
        for c in range(DEC_SEQ // rows):
            sl = slice(c * rows, (c + 1) * rows)
            k = _head_rms(kv_ref[sl, :LANES].astype(F32), kn_ref[...], bd)
            k = _rope(k, cos_ref[sl, :], sin_ref[sl, :])
            v = kv_ref[sl, LANES:].astype(F32)
            dst = slice(PAST_LEN + c * rows, PAST_LEN + (c + 1) * rows)
            for g in range(A_KV_HEADS):
                kd_ref[g, dst] = _dup_half(k, g).astype(BF16)
                vd_ref[g, dst] = _with_ones(_dup_half(v, g))

    _gqa_block(q_ref, qn_ref[...], bd, kd_ref, vd_ref, o_ref, rope=(cosq_ref[...], sinq_ref[...]))


def _attn_a_sample(q, kv, ck, cv, cos, sin, qn, kn, bd):
    nq = DEC_SEQ // _A_QBLOCK
    full = lambda a: pl.BlockSpec(a.shape, lambda b, i: (0,) * a.ndim)
    tk = PAST_LEN + DEC_SEQ
    return pl.pallas_call(
        _attn_a_sample_kernel,
        grid=(DEC_BATCH, nq),
        in_specs=[pl.BlockSpec((_A_QBLOCK, A_WIDTH), lambda b, i: (b * nq + i, 0)),
                  pl.BlockSpec((DEC_SEQ, 2 * A_KV_WIDTH), lambda b, i: (b, 0)),
                  pl.BlockSpec((1, PAST_LEN, A_KV_WIDTH), lambda b, i: (b, 0, 0)),
                  pl.BlockSpec((1, PAST_LEN, A_KV_WIDTH), lambda b, i: (b, 0, 0)),
                  full(cos), full(sin),
                  pl.BlockSpec((_A_QBLOCK, LANES), lambda b, i: (i, 0)),
                  pl.BlockSpec((_A_QBLOCK, LANES), lambda b, i: (i, 0)),
                  full(qn), full(kn), full(bd)],
        out_specs=pl.BlockSpec((_A_QBLOCK, A_WIDTH), lambda b, i: (b * nq + i, 0)),
        out_shape=jax.ShapeDtypeStruct((T_SAMPLE, A_WIDTH), BF16),
        scratch_shapes=[pltpu.VMEM((A_KV_HEADS, tk, LANES), BF16),
                        pltpu.VMEM((A_KV_HEADS, tk, 2 * LANES), BF16)],
        compiler_params=_cparams(("arbitrary", "arbitrary")),
        name="attn_a_sample",
    )(q, kv, ck, cv, cos, sin, cos, sin, qn, kn, bd)


def _log_sigmoid(x):
    return -(jnp.maximum(-x, 0.0) + jnp.log1p(jnp.exp(-jnp.abs(x))))


def _mlstm_kernel(*refs, n, has_init, emit_state):
    it = iter(refs)
    q_ref, v_ref, og_ref, kt_ref, gt_ref, brow_ref, onorm_ref, tri_ref = [next(it) for _ in range(8)]
    if has_init:
        c0_ref, m0_ref = [next(it) for _ in range(2)]
    out_ref = next(it)
    if emit_state:
        cT_ref, nT_ref, mT_ref = [next(it) for _ in range(3)]
    h_ref, row_ref, c_ref = [next(it) for _ in range(3)]

    L = MLSTM_CHUNK
    D = B_HEAD_DIM
    nc = n // L
    hb = MLSTM_HEADS_PER_STEP

    lane = lax.broadcasted_iota(jnp.int32, (nc * 8, L), 1)
    is_fwd = lax.broadcasted_iota(jnp.int32, (nc * 8, L), 0) % 8 == 0

    def running_max(x, suffix):
        for sh in (1, 2, 4, 8, 16, 32, 64):
            if suffix:
                x = jnp.where(lane < L - sh, jnp.maximum(x, pltpu.roll(x, L - sh, 1)), x)
            else:
                x = jnp.where(lane >= sh, jnp.maximum(x, pltpu.roll(x, sh, 1)), x)
        return x

    for hh in range(hb):
        gt = gt_ref[:, 16 * hh:16 * hh + 16, :] + brow_ref[hh][None]
        li = gt[:, 0:8, :].reshape(nc * 8, L)
        lf = _log_sigmoid(gt[:, 8:16, :]).reshape(nc * 8, L)
        cum = jnp.where(is_fwd, _dot_exact_rhs(lf, tri_ref[0]), _dot_exact_rhs(lf, tri_ref[1]))
        a = li - cum
        planes = (cum, a, jnp.broadcast_to(lf.sum(axis=-1, keepdims=True), (nc * 8, L)),
                  jnp.where(is_fwd, running_max(a, False), running_max(a, True)),
                  jnp.broadcast_to(a.max(axis=-1, keepdims=True), (nc * 8, L)))
        for p, val in enumerate(planes):
            row_ref[hh, p] = val.reshape(nc, 8, L)

    chains = [(hh, d) for hh in range(hb) for d in range(2)]
    if has_init:
        m_init = []
        for hh, d in chains:
            c_ref[2 * hh + d] = c0_ref[0, d, hh]
            m_init.append(m0_ref[0, hh, d:d + 1, :])
        m_init = tuple(m_init)
    else:
        c_ref[...] = jnp.zeros(c_ref.shape, F32)
        m_init = tuple(jnp.zeros((1, L), F32) for _ in chains)

    t_idx = lax.broadcasted_iota(jnp.int32, (L, L), 0)
    s_idx = lax.broadcasted_iota(jnp.int32, (L, L), 1)
    masks = (s_idx <= t_idx, s_idx >= t_idx)
    ones = jnp.ones((L, L), F32)

    def step(c, hh, d, m):
        r0 = pl.multiple_of(c * L, L)
        hl = slice(hh * L, (hh + 1) * L)
        qb = q_ref[pl.ds(r0, L), hl].astype(BF16)
        kst = kt_ref[c, hl, :].astype(F32) * (D ** -0.5)
        v_ext = jnp.concatenate([v_ref[pl.ds(r0, L), hl], ones], axis=1).astype(BF16)
        cum, a_row, tot, amax_run, amax = [row_ref[hh, p, c][d:d + 1, :] for p in range(5)]
        m_cum = jnp.broadcast_to(cum, (L, L)).T
        m_run = jnp.broadcast_to(amax_run, (L, L)).T
        dlog = jnp.where(masks[d], m_cum + a_row, -jnp.inf)
        inter = m_cum + m
        m_t = jnp.maximum(inter, m_cum + m_run)
        w_in = jnp.exp(dlog - m_t)
        w_st = jnp.exp(inter - m_t)
        a = _dot(qb, kst.astype(BF16)) * w_in
        ci = 2 * hh + d
        cext = c_ref[ci]
        p_state = _dot(qb, cext.astype(BF16))
        p_intra = _dot(a.astype(BF16), v_ext)
        num = w_st * p_state[:, :D] + p_intra[:, :D]
        den = w_st * p_state[:, D:] + p_intra[:, D:]
        h_ref[ci, pl.ds(r0, L), :] = num / jnp.maximum(jnp.abs(den), jnp.exp(-m_t))
        m_new = jnp.maximum(tot + m, amax + tot)
        ws = jnp.exp(a_row + tot - m_new)
        wc = jnp.exp(tot + m - m_new)
        c_ref[ci] = jnp.concatenate([wc, wc], axis=1) * cext + _dot((kst * ws).astype(BF16), v_ext)
        return m_new

    def body(i, carry):
        return tuple(step(i if d == 0 else nc - 1 - i, hh, d, m) for (hh, d), m in zip(chains, carry))

    m_fin = lax.fori_loop(0, nc, body, m_init)

    for hh in range(hb):
        hm = h_ref[2 * hh] + h_ref[2 * hh + 1]
        hl = slice(hh * L, (hh + 1) * L)
        gate = jax.nn.sigmoid(og_ref[:, hl].astype(F32))
        out_ref[:, hl] = (_rms(hm, onorm_ref[hh]) * gate).astype(out_ref.dtype)

    if emit_state:
        for k, (hh, d) in enumerate(chains):
            cext = c_ref[2 * hh + d]
            cT_ref[0, d, hh] = cext[:, :D]
            nT_ref[0, hh, d:d + 1, :] = cext[:, D:].T[0:1, :]
            mT_ref[0, hh, d:d + 1, :] = m_fin[k]


def _mlstm(ob, okt, ogt, brow, onorm, tri, *, n, nseq, init=None, emit_state=False):
    L = MLSTM_CHUNK
    nc = n // L
    H = B_HEADS
    hb = MLSTM_HEADS_PER_STEP
    ng = H // hb
    col = lambda part: (lambda b, g: (b, part * ng + g))
    in_specs = [pl.BlockSpec((n, hb * L), col(0)), pl.BlockSpec((n, hb * L), col(1)),
                pl.BlockSpec((n, hb * L), col(2)),
                pl.BlockSpec((nc, hb * L, L), lambda b, g: (b, g, 0)),
                pl.BlockSpec((nc, 16 * hb, L), lambda b, g: (b, g, 0)),
                pl.BlockSpec((hb, 16, L), lambda b, g: (g, 0, 0)),
                pl.BlockSpec((hb, 1, L), lambda b, g: (g, 0, 0)),
                pl.BlockSpec(tri.shape, lambda b, g: (0, 0, 0))]
    args = [ob, ob, ob, okt, ogt, brow, onorm, tri]
    if init is not None:
        in_specs += [pl.BlockSpec((1, 2, hb, L, 2 * L), lambda b, g: (b, 0, g, 0, 0)),
                     pl.BlockSpec((1, hb, 2, L), lambda b, g: (b, g, 0, 0))]
        args += list(init)
    out_specs = [pl.BlockSpec((n, hb * L), lambda b, g: (b, g))]
    out_shape = [jax.ShapeDtypeStruct((nseq * n, B_WIDTH), BF16)]
    if emit_state:
        out_specs += [pl.BlockSpec((1, 2, hb, L, L), lambda b, g: (b, 0, g, 0, 0)),
                      pl.BlockSpec((1, hb, 2, L), lambda b, g: (b, g, 0, 0)),
                      pl.BlockSpec((1, hb, 2, L), lambda b, g: (b, g, 0, 0))]
        out_shape += [jax.ShapeDtypeStruct((nseq, 2, H, L, L), F32),
                      jax.ShapeDtypeStruct((nseq, H, 2, L), F32),
                      jax.ShapeDtypeStruct((nseq, H, 2, L), F32)]
    return pl.pallas_call(
        functools.partial(_mlstm_kernel, n=n, has_init=init is not None, emit_state=emit_state),
        grid=(nseq, ng),
        in_specs=in_specs,
        out_specs=out_specs,
        out_shape=out_shape,
        scratch_shapes=[pltpu.VMEM((2 * hb, n, L), F32),
                        pltpu.VMEM((hb, 5, nc, 8, L), F32),
                        pltpu.VMEM((2 * hb, L, 2 * L), F32)],
        compiler_params=_cparams(("arbitrary", "arbitrary")),
        name="mlstm_init" if init is not None else "mlstm",
    )(*args)


def _router(logits):
    lane = _lane(logits.shape).astype(F32)
    big = 1e9
    gl = jnp.where(lane < N_GROUPS, logits, -jnp.inf)
    gmax = gl.max(axis=-1, keepdims=True)
    g_sel = jnp.where(gl == gmax, lane, big).min(axis=-1, keepdims=True)
    g_prob = 1.0 / jnp.exp(gl - gmax).sum(axis=-1, keepdims=True)
    lo = N_GROUPS + EXPERTS_PER_GROUP * g_sel
    el = jnp.where(lane >= lo, jnp.where(lane < lo + EXPERTS_PER_GROUP, logits, -jnp.inf), -jnp.inf)
    v1 = el.max(axis=-1, keepdims=True)
    i1 = jnp.where(el == v1, lane, big).min(axis=-1, keepdims=True)
    el2 = jnp.where(lane == i1, -jnp.inf, el)
    v2 = el2.max(axis=-1, keepdims=True)
    i2 = jnp.where(el2 == v2, lane, big).min(axis=-1, keepdims=True)
    e2 = jnp.exp(v2 - v1)
    w1 = g_prob / (1.0 + e2)
    w2 = g_prob * e2 / (1.0 + e2)
    return i1, i2, w1, w2


def _read_tokens(refs, is_prompt, rows):
    if len(refs) == 1:
        return refs[0][rows, :]
    return jnp.where(is_prompt, refs[0][rows, :], refs[1][rows, :])


def _post_kernel(*refs, groups):
    it = iter(refs)
    tok_refs = [[next(it) for _ in range(n)] for n in groups]
    w_refs = [next(it) for _ in range(len(groups) - 1)]
    mod_ref, g_ref, wr_ref, br_ref, ls_ref, sel_ref, su_ref, tokrow_ref = [next(it) for _ in range(8)]
    xnew_ref, xn_ref, rt_ref, rtt_ref, cnt_ref, order_ref, tab_ref = [next(it) for _ in range(7)]
    run_ref = next(it)
    i = pl.program_id(0)
    tm = xnew_ref.shape[0]
    is_prompt = i * tm < T_PROMPT
    r = _mod_row(i * tm)

    @pl.when(i == 0)
    def _():
        run_ref[...] = jnp.zeros(run_ref.shape, F32)

    sub = ls_ref.shape[0]
    for h in range(tm // sub):
        rows = slice(h * sub, (h + 1) * sub)
        acc = None
        for a_refs, w_ref in zip(tok_refs[1:], w_refs):
            d = _dot(_read_tokens(a_refs, is_prompt, rows).astype(BF16), w_ref[...])
            acc = d if acc is None else acc + d
        xnew = _read_tokens(tok_refs[0], is_prompt, rows) + _mod_part(mod_ref, r, 2) * acc
        xnew_ref[rows, :] = xnew
        xn = _rms(xnew, g_ref[...]) * (1.0 + _mod_part(mod_ref, r, 4)) + _mod_part(mod_ref, r, 3)
        _to_token_tiles(xn_ref, xn, h * sub)
        x_hi, x_lo = _split2(xn)
        both = _dot(x_hi, wr_ref[...])
        logits = both[:, :LANES] + both[:, LANES:] + _dot(x_lo, wr_ref[:, :LANES]) + br_ref[...]
        i1, i2, w1, w2 = _router(logits)

        lane = _lane(logits.shape).astype(F32)
        member = jnp.where(lane == i1, 1.0, jnp.where(lane == i2, 1.0, 0.0))
        local = _dot(ls_ref[...], member.astype(BF16))
        run = run_ref[...]
        before = local + run
        rank1 = jnp.where(lane == i1, before, 0.0).sum(axis=-1, keepdims=True)
        rank2 = jnp.where(lane == i2, before, 0.0).sum(axis=-1, keepdims=True)
        n_here = member.sum(axis=0, keepdims=True)
        run_ref[...] = run + n_here

        off = _dot(jnp.broadcast_to(n_here, (8, LANES)).astype(BF16), su_ref[...])[0:1]
        pos = off + local
        p1 = jnp.where(lane == i1, pos, 0.0).sum(axis=-1, keepdims=True)
        p2 = jnp.where(lane == i2, pos, 0.0).sum(axis=-1, keepdims=True)
        slot = lax.broadcasted_iota(jnp.int32, (sub, 2 * sub), 1).astype(F32)
        onehot = jnp.where(slot == p1, 1.0, jnp.where(slot == p2, 1.0, 0.0)).astype(BF16)
        order_ref[:, h * 2 * sub:(h + 1) * 2 * sub] = _dot(tokrow_ref[...], onehot) + (i * tm + h * sub).astype(F32)
        row8 = lax.broadcasted_iota(jnp.int32, (8, LANES), 0)
        tab_ref[h] = jnp.where(row8 == 0, n_here, jnp.where(row8 == 1, run, jnp.where(row8 == 2, off, 0.0)))
        cols = (i1 - N_GROUPS, i2 - N_GROUPS, w1, w2, rank1, rank2)
        rt = jnp.zeros(logits.shape, F32)
        for k, c in enumerate(cols):
            rt = jnp.where(lane == k, c, rt)
        rt_ref[rows, :] = rt
        hi, mid, lo = _split3(rt)
        sel = sel_ref[...]
        rtt_ref[:, rows] = _dot_nt(sel, hi) + _dot_nt(sel, mid) + _dot_nt(sel, lo)
    cnt_ref[...] = run_ref[...]


def _post(tok_ops, w_list, mod, g, wr, br):
    t = T_ALL
    tm = POST_BLOCK
    npb = T_PROMPT // tm
    full = lambda a: pl.BlockSpec(a.shape, lambda i: (0,) * a.ndim)
    specs, args, groups = [], [], []
    for op in tok_ops:
        if isinstance(op, tuple):
            w = op[0].shape[1]
            specs += [pl.BlockSpec((tm, w), lambda i: (jnp.minimum(i, npb - 1), 0)),
                      pl.BlockSpec((tm, w), lambda i: (jnp.maximum(i - npb, 0), 0))]
            args += list(op)
            groups.append(2)
        else:
            specs.append(pl.BlockSpec((tm, op.shape[1]), lambda i: (i, 0)))
            args.append(op)
            groups.append(1)
    sub = TOK_BLOCK
    idx = np.arange(sub)
    ls = jnp.asarray(idx[:, None] > idx[None, :], BF16)
    sel = jnp.asarray(np.arange(8)[:, None] == np.arange(LANES)[None, :], BF16)
    lanes = np.arange(LANES)
    su = jnp.asarray(lanes[:, None] < lanes[None, :], BF16)
    tokrow = jnp.asarray(np.where(np.arange(8)[:, None] == 0, idx[None, :], 0), BF16)
    consts = [mod, g, wr, br, ls, sel, su, tokrow]
    return pl.pallas_call(
        functools.partial(_post_kernel, groups=tuple(groups)),
        grid=(t // tm,),
        in_specs=specs + [full(w) for w in w_list] + [full(a) for a in consts],
        out_specs=[pl.BlockSpec((tm, D_MODEL), lambda i: (i, 0)),
                   pl.BlockSpec((tm * SLABS, LANES), lambda i: (i, 0)),
                   pl.BlockSpec((tm, LANES), lambda i: (i, 0)),
                   pl.BlockSpec((8, tm), lambda i: (0, i)),
                   pl.BlockSpec((1, LANES), lambda i: (0, 0)),
                   pl.BlockSpec((8, 2 * tm), lambda i: (0, i)),
                   pl.BlockSpec((tm // sub, 8, LANES), lambda i: (i, 0, 0))],
        out_shape=[jax.ShapeDtypeStruct((t, D_MODEL), F32),
                   jax.ShapeDtypeStruct((t * SLABS, LANES), U32),
                   jax.ShapeDtypeStruct((t, LANES), F32),
                   jax.ShapeDtypeStruct((8, t), F32),
                   jax.ShapeDtypeStruct((1, LANES), F32),
                   jax.ShapeDtypeStruct((8, 2 * t), F32),
                   jax.ShapeDtypeStruct((t // sub, 8, LANES), F32)],
        scratch_shapes=[pltpu.VMEM((1, LANES), F32)],
        compiler_params=_cparams(("arbitrary",)),
        name="post_mixer_router",
    )(*args, *w_list, *consts)


def _expert_kernel(be_ref, src_ref, tok_ref, nv_ref, xn_hbm, w1_ref, w3_ref, w2_ref, o_ref,
                   xa, xb, sem, w1b, w3b, w2b):
    i = pl.program_id(0)
    nv = nv_ref[0]
    active = i < nv
    rows = EXPERT_ROWS

    def gather(blk, buf, s):
        for r in range(rows):
            tok = tok_ref[src_ref[blk * rows + r]]
            pltpu.make_async_copy(xn_hbm.at[pl.ds(pl.multiple_of(tok * SLABS, SLABS), SLABS), :],
                                  buf.at[pl.ds(r * SLABS, SLABS), :], sem.at[s]).start(priority=r % 2)

    def wait(buf, s):
        pltpu.make_async_copy(xn_hbm.at[pl.ds(0, rows * SLABS), :], buf, sem.at[s]).wait()

    @pl.when(i == 0)
    def _():
        gather(0, xa, 0)

    changed = jnp.logical_or(i == 0, be_ref[i] != be_ref[jnp.maximum(i - 1, 0)])

    @pl.when(jnp.logical_and(changed, active))
    def _():
        w1b[...] = w1_ref[0, 0].astype(BF16)
        w3b[...] = w3_ref[0, 0].astype(BF16)
        w2b[...] = w2_ref[0, 0].astype(BF16)

    nxt = jnp.minimum(i + 1, nv - 1)

    def step(cur, s_cur, oth, s_oth):
        wait(cur, s_cur)
        gather(nxt, oth, s_oth)
        x = _from_token_tiles(cur, 0, rows).astype(BF16)
        h1 = _dot(x, w1b[...])
        h3 = _dot(x, w3b[...])
        hid = (h1 * jax.nn.sigmoid(h1)) * h3
        _to_token_tiles(o_ref, _dot(hid.astype(BF16), w2b[...]))

    @pl.when(jnp.logical_and(active, i % 2 == 0))
    def _():
        step(xa, 0, xb, 1)

    @pl.when(jnp.logical_and(active, i % 2 == 1))
    def _():
        step(xb, 1, xa, 0)

    @pl.when(jnp.logical_not(active))
    def _():
        o_ref[...] = jnp.zeros(o_ref.shape, o_ref.dtype)

    @pl.when(i == nv - 1)
    def _():
        @pl.when(i % 2 == 0)
        def _():
            wait(xb, 1)

        @pl.when(i % 2 == 1)
        def _():
            wait(xa, 0)


def _experts(block_expert, slot_src, pair_tok, n_valid, xn, w1, w3, w2, layer):
    nblk = block_expert.shape[0]
    rows = EXPERT_ROWS
    weights = lambda i, be, *_: (layer, be[i], 0, 0)
    grid_spec = pltpu.PrefetchScalarGridSpec(
        num_scalar_prefetch=4,
        grid=(nblk,),
        in_specs=[pl.BlockSpec(memory_space=pl.ANY),
                  pl.BlockSpec((1, 1, D_MODEL, D_EXPERT), weights),
                  pl.BlockSpec((1, 1, D_MODEL, D_EXPERT), weights),
                  pl.BlockSpec((1, 1, D_EXPERT, D_MODEL), weights)],
        out_specs=pl.BlockSpec((rows * SLABS, LANES), lambda i, *_: (i, 0)),
        scratch_shapes=[pltpu.VMEM((rows * SLABS, LANES), U32),
                        pltpu.VMEM((rows * SLABS, LANES), U32),
                        pltpu.SemaphoreType.DMA((2,)),
                        pltpu.VMEM((D_MODEL, D_EXPERT), BF16),
                        pltpu.VMEM((D_MODEL, D_EXPERT), BF16),
                        pltpu.VMEM((D_EXPERT, D_MODEL), BF16)])
    return pl.pallas_call(
        _expert_kernel,
        grid_spec=grid_spec,
        out_shape=jax.ShapeDtypeStruct((nblk * rows * SLABS, LANES), U32),
        compiler_params=_cparams(("arbitrary",)),
        name="moe_experts",
    )(block_expert, slot_src, pair_tok, n_valid, xn, w1, w3, w2)


def _combine_kernel(*refs, row0, project, final_norm):
    it = iter(refs)
    dest_ref, yb_hbm, x_ref, rt_ref, mod0_ref = [next(it) for _ in range(5)]
    if project:
        mod1_ref, g_ref, w_ref = [next(it) for _ in range(3)]
    if final_norm:
        gfin_ref = next(it)
    xo_ref = next(it)
    if project:
        q_ref, k_ref, v_ref = [next(it) for _ in range(3)]
    ya, yb, sem = [next(it) for _ in range(3)]
    i = pl.program_id(0)
    nblk = pl.num_programs(0)
    tm = x_ref.shape[0]

    def gather(blk, buf, s):
        for j in range(tm):
            for c in range(2):
                d = dest_ref[c * T_ALL + row0 + blk * tm + j]
                pltpu.make_async_copy(yb_hbm.at[pl.ds(pl.multiple_of(d * SLABS, SLABS), SLABS), :],
                                      buf.at[pl.ds((c * tm + j) * SLABS, SLABS), :], sem.at[s]).start(priority=c)

    def wait(buf, s):
        pltpu.make_async_copy(yb_hbm.at[pl.ds(0, 2 * tm * SLABS), :], buf, sem.at[s]).wait()

    @pl.when(i == 0)
    def _():
        gather(0, ya, 0)

    nxt = jnp.minimum(i + 1, nblk - 1)

    def step(cur, s_cur, oth, s_oth):
        wait(cur, s_cur)
        gather(nxt, oth, s_oth)
        r = _mod_row(row0 + i * tm)
        rt = rt_ref[...]
        y = rt[:, 2:3] * _from_token_tiles(cur, 0, tm) + rt[:, 3:4] * _from_token_tiles(cur, tm, tm)
        x = x_ref[...] + _mod_part(mod0_ref, r, 5) * y
        xo_ref[...] = _rms(x, gfin_ref[...]) if final_norm else x
        if project:
            h = _rms(x, g_ref[...]) * (1.0 + _mod_part(mod1_ref, r, 1)) + _mod_part(mod1_ref, r, 0)
            hb = h.astype(BF16)
            for j, o_ref in enumerate((q_ref, k_ref, v_ref)):
                o_ref[...] = _dot(hb, w_ref[:, j * C_WIDTH:(j + 1) * C_WIDTH]).astype(o_ref.dtype)

    @pl.when(i % 2 == 0)
    def _():
        step(ya, 0, yb, 1)

    @pl.when(i % 2 == 1)
    def _():
        step(yb, 1, ya, 0)

    @pl.when(i == nblk - 1)
    def _():
        @pl.when(i % 2 == 0)
        def _():
            wait(yb, 1)

        @pl.when(i % 2 == 1)
        def _():
            wait(ya, 0)


def _combine(dest, yb, x, rt, mod0, row0, t, proj=None, gfin=None):
    tm = TOK_BLOCK
    blk0 = row0 // tm
    full = lambda a: pl.BlockSpec(a.shape, lambda i, d: (0,) * a.ndim)
    extra = list(proj) if proj is not None else []
    if gfin is not None:
        extra.append(gfin)
    n_out = 4 if proj is not None else 1
    out_shape = [jax.ShapeDtypeStruct((t, D_MODEL), F32)] + [jax.ShapeDtypeStruct((t, C_WIDTH), BF16)] * (n_out - 1)
    grid_spec = pltpu.PrefetchScalarGridSpec(
        num_scalar_prefetch=1,
        grid=(t // tm,),
        in_specs=[pl.BlockSpec(memory_space=pl.ANY),
                  pl.BlockSpec((tm, D_MODEL), lambda i, d: (blk0 + i, 0)),
                  pl.BlockSpec((tm, LANES), lambda i, d: (blk0 + i, 0)),
                  full(mod0)] + [full(a) for a in extra],
        out_specs=[pl.BlockSpec((tm, D_MODEL), lambda i, d: (i, 0))] * n_out,
        scratch_shapes=[pltpu.VMEM((2 * tm * SLABS, LANES), U32),
                        pltpu.VMEM((2 * tm * SLABS, LANES), U32),
                        pltpu.SemaphoreType.DMA((2,))])
    return pl.pallas_call(
        functools.partial(_combine_kernel, row0=row0, project=proj is not None, final_norm=gfin is not None),
        grid_spec=grid_spec,
        out_shape=out_shape,
        compiler_params=_cparams(("arbitrary",)),
        name="moe_combine_proj" if proj is not None else "moe_combine",
    )(dest, yb, x, rt, mod0, *extra)


def _moe_plan(rtt, cnt, order, tab):
    t = rtt.shape[1]
    eid = rtt[0:2].astype(jnp.int32)
    rank = rtt[4:6].astype(jnp.int32)
    counts = cnt[0, N_GROUPS:N_GROUPS + N_EXPERTS].astype(jnp.int32)
    padded = (counts + EXPERT_ROWS - 1) // EXPERT_ROWS * EXPERT_ROWS
    seg_end = jnp.cumsum(padded)
    seg_start = seg_end - padded
    experts = jnp.arange(N_EXPERTS, dtype=jnp.int32)
    start = jnp.sum(jnp.where(eid[..., None] == experts, seg_start, 0), axis=-1)
    dest = (start + rank).reshape(-1)
    n_blocks = (2 * t + N_EXPERTS * (EXPERT_ROWS - 1) + EXPERT_ROWS - 1) // EXPERT_ROWS
    first_row = jnp.arange(n_blocks, dtype=jnp.int32) * EXPERT_ROWS
    block_expert = jnp.minimum(jnp.sum((seg_end[None, :] <= first_row[:, None]).astype(jnp.int32), axis=1),
                               N_EXPERTS - 1)
    n_valid = (seg_end[-1:] // EXPERT_ROWS).astype(jnp.int32)

    nsb = tab.shape[0]
    n_sb = tab[:, 0, N_GROUPS:N_GROUPS + N_EXPERTS].astype(jnp.int32).T[block_expert]
    run_sb = tab[:, 1, N_GROUPS:N_GROUPS + N_EXPERTS].astype(jnp.int32).T[block_expert]
    off_sb = tab[:, 2, N_GROUPS:N_GROUPS + N_EXPERTS].astype(jnp.int32).T[block_expert]
    rank = (first_row - seg_start[block_expert])[:, None] + jnp.arange(EXPERT_ROWS, dtype=jnp.int32)[None, :]
    real = rank < counts[block_expert][:, None]
    sb = jnp.sum(((run_sb + n_sb)[:, None, :] <= rank[:, :, None]).astype(jnp.int32), axis=-1)
    sb = jnp.minimum(sb, nsb - 1)
    pick = sb[:, :, None] == jnp.arange(nsb, dtype=jnp.int32)[None, None, :]
    shift = jnp.sum(jnp.where(pick, (off_sb - run_sb)[:, None, :], 0), axis=-1)
    src = jnp.clip(sb * (2 * TOK_BLOCK) + shift + rank, 0, 2 * t - 1)
    slot = first_row[:, None] + jnp.arange(EXPERT_ROWS, dtype=jnp.int32)[None, :]
    slot_src = jnp.where(real, src, slot % (2 * t)).reshape(-1)
    return dest, slot_src, order[0].astype(jnp.int32), block_expert, n_valid


def _attn_c_prompt_kernel(q_ref, k_ref, v_ref, o_ref):
    for p in range(C_HEADS // 2):
        sl = slice(p * LANES, (p + 1) * LANES)
        kb = k_ref[:, sl].astype(BF16)
        vb = _with_ones(v_ref[:, sl])
        qp = q_ref[:, sl].astype(F32) * (C_HEAD_DIM ** -0.5)
        o_ref[:, sl] = _softmax_pair(qp, lambda qm: [_dot_nt(qm, kb)],
                                     lambda ps: _dot(ps[0], vb)).astype(o_ref.dtype)


def _attn_c_prompt(q, k, v):
    blk = pl.BlockSpec((SEQ, C_WIDTH), lambda b: (b, 0))
    return pl.pallas_call(
        _attn_c_prompt_kernel,
        grid=(BATCH,),
        in_specs=[blk, blk, blk],
        out_specs=blk,
        out_shape=jax.ShapeDtypeStruct((T_PROMPT, C_WIDTH), BF16),
        compiler_params=_cparams(("arbitrary",)),
        name="attn_c_prompt",
    )(q, k, v)


def _na_key_start(r0):
    rows = DEC_SEQ // GRID_W
    return jnp.minimum(jnp.clip(r0 - NA_ROWS // 2, 0, rows - NA_ROWS), rows - NA_KROWS)


def _na_block_plan():
    rows = DEC_SEQ // GRID_W
    nblk = rows // NA_QROWS
    plan = []
    for blk in (0, 1, nblk - 1):
        r0 = blk * NA_QROWS
        ks = min(int(np.clip(r0 - NA_ROWS // 2, 0, rows - NA_ROWS)), rows - NA_KROWS)
        per_row = []
        for i in range(NA_QROWS):
            r = r0 + i
            rs = int(np.clip(r - NA_ROWS // 2, 0, rows - NA_ROWS))
            start = ks - r + NA_ROWS - 1 + NA_KROWS
            ok = [rs <= ks + j < rs + NA_ROWS for j in range(NA_KROWS)]
            per_row.append((start, ok))
        plan.append(per_row)
    return plan


def _attn_c_sample_kernel(q_ref, k_ref, v_ref, ck_ref, cv_ref, toe_ref, o_ref, bias_ref):
    rows = DEC_SEQ // GRID_W
    nblk = rows // NA_QROWS
    w = GRID_W

    @pl.when(jnp.logical_and(pl.program_id(1) == 0, pl.program_id(2) == 0))
    def _():
        neg = jnp.full((w, w), NEG, F32)
        for t, per_row in enumerate(_na_block_plan()):
            for half in range(2):
                for i, (start, ok) in enumerate(per_row):
                    for j in range(0, NA_KROWS, 2):
                        pieces = [toe_ref[0, half, start + jj] if ok[jj] else neg for jj in (j, j + 1)]
                        bias_ref[t, half, i * w:(i + 1) * w, j * w:(j + 2) * w] = jnp.concatenate(pieces, axis=1)

    i = pl.program_id(2)
    r0 = i * NA_QROWS
    k0 = pl.multiple_of(_na_key_start(r0) * GRID_W, GRID_W)
    btype = jnp.where(i == 0, 0, jnp.where(i == nblk - 1, 2, 1))
    nk = NA_KROWS * GRID_W
    kw = k_ref[pl.ds(k0, nk), :].astype(BF16)
    vw = v_ref[pl.ds(k0, nk), :].astype(BF16)
    kc = ck_ref[0].astype(BF16)
    vc = cv_ref[0].astype(BF16)
    qp = q_ref[...].astype(F32) * (C_HEAD_DIM ** -0.5)
    lo = _lane(qp.shape) < 64
    outs = []
    for half in range(2):
        qm = jnp.where(lo if half == 0 else jnp.logical_not(lo), qp, 0.0).astype(BF16)
        s_win = _dot_nt(qm, kw) + bias_ref[btype, half]
        s_ctx = _dot_nt(qm, kc)
        m = jnp.maximum(s_win.max(axis=-1, keepdims=True), s_ctx.max(axis=-1, keepdims=True))
        e_win = jnp.exp(s_win - m)
        e_ctx = jnp.exp(s_ctx - m)
        l = e_win.sum(axis=-1, keepdims=True) + e_ctx.sum(axis=-1, keepdims=True)
        o = _dot(e_win.astype(BF16), vw) + _dot(e_ctx.astype(BF16), vc)
        outs.append(o / l)
    o_ref[...] = jnp.where(lo, outs[0], outs[1]).astype(o_ref.dtype)


def _attn_c_sample(q, k, v, ck, cv, toe):
    rows = DEC_SEQ // GRID_W
    nblk = rows // NA_QROWS
    qrows = NA_QROWS * GRID_W
    npair = C_HEADS // 2
    return pl.pallas_call(
        _attn_c_sample_kernel,
        grid=(npair, DEC_BATCH, nblk),
        in_specs=[pl.BlockSpec((qrows, LANES), lambda p, b, i: (b * nblk + i, p)),
                  pl.BlockSpec((DEC_SEQ, LANES), lambda p, b, i: (b, p)),
                  pl.BlockSpec((DEC_SEQ, LANES), lambda p, b, i: (b, p)),
                  pl.BlockSpec((1, PAST_LEN, LANES), lambda p, b, i: (b, 0, p)),
                  pl.BlockSpec((1, PAST_LEN, LANES), lambda p, b, i: (b, 0, p)),
                  pl.BlockSpec((1,) + toe.shape[1:], lambda p, b, i: (p, 0, 0, 0, 0))],
        out_specs=pl.BlockSpec((qrows, LANES), lambda p, b, i: (b * nblk + i, p)),
        out_shape=jax.ShapeDtypeStruct((T_SAMPLE, C_WIDTH), BF16),
        scratch_shapes=[pltpu.VMEM((3, 2, qrows, NA_KROWS * GRID_W), F32)],
        compiler_params=_cparams(("arbitrary", "arbitrary", "arbitrary")),
        name="attn_c_sample",
    )(q, k, v, ck, cv, toe)


def _na_toeplitz(rpb):
    w = GRID_W
    nd_r, nd_c = 2 * NA_ROWS - 1, 2 * NA_COLS - 1
    c = np.arange(w)
    cs = np.clip(c - NA_COLS // 2, 0, w - NA_COLS)
    col_ok = (c[None, :] >= cs[:, None]) & (c[None, :] < cs[:, None] + NA_COLS)
    dcol = c[None, :] - c[:, None] + NA_COLS - 1
    onehot = (np.arange(nd_c)[:, None, None] == dcol[None]).reshape(nd_c, w * w)
    toe = jnp.dot(rpb.reshape(C_HEADS * nd_r, nd_c), jnp.asarray(onehot, F32), precision=lax.Precision.HIGHEST)
    toe = jnp.where(col_ok[None, None], toe.reshape(C_HEADS, nd_r, w, w), NEG)
    toe = jnp.pad(toe, ((0, 0), (NA_KROWS, NA_KROWS), (0, 0), (0, 0)), constant_values=NEG)
    return toe.reshape(C_HEADS // 2, 2, nd_r + 2 * NA_KROWS, w, w)


def _rope_tables():
    half = A_HEAD_DIM // 2
    t = jnp.arange(DEC_SEQ)
    row = (t // GRID_W).astype(F32)
    colp = (t % GRID_W).astype(F32)
    freqs = 1.0 / (ROPE_BASE ** (jnp.arange(0, half, 2, dtype=F32) / half))
    d = np.arange(LANES) % A_HEAD_DIM
    pos = jnp.where(jnp.asarray(d < half)[None, :], row[:, None], colp[:, None])
    ang = pos * freqs[d % (half // 2)][None, :]
    sign = jnp.asarray(np.where((d % half) < half // 2, -1.0, 1.0), F32)[None, :]
    return jnp.cos(ang), jnp.sin(ang) * sign


def _head_avg_matrix():
    idx = np.arange(LANES) // A_HEAD_DIM
    return jnp.asarray((idx[:, None] == idx[None, :]).astype(np.float32) / A_HEAD_DIM, BF16)


def _tri_matrices():
    i = np.arange(MLSTM_CHUNK)
    upper = (i[:, None] <= i[None, :]).astype(np.float32)
    lower = (i[:, None] >= i[None, :]).astype(np.float32)
    return jnp.asarray(np.stack([upper, lower]), BF16)


def _router_weights(wg, bg, we, be):
    w = jnp.zeros((D_MODEL, LANES), F32).at[:, :N_GROUPS].set(wg).at[:, N_GROUPS:N_GROUPS + N_EXPERTS].set(we)
    b = jnp.zeros((1, LANES), F32).at[0, :N_GROUPS].set(bg).at[0, N_GROUPS:N_GROUPS + N_EXPERTS].set(be)
    hi = w.astype(BF16)
    lo = (w - hi.astype(F32)).astype(BF16)
    return jnp.concatenate([hi, lo], axis=1), b


def _moe(xn, rtt, cnt, order, tab, w1, w3, w2, layer):
    dest, slot_src, pair_tok, block_expert, n_valid = _moe_plan(rtt, cnt, order, tab)
    return dest, _experts(block_expert, slot_src, pair_tok, n_valid, xn, w1, w3, w2, layer)


def kernel(x_prompt, x_sample, cache_attn_k, cache_attn_v, state_mlstm_C, state_mlstm_n, state_mlstm_m,
           cache_na_k, cache_na_v, c, c_ctx, norm_mix, norm_ffn, norm_final, ada_w, ada_b,
           ab_w_in, ab_w_out, ab_q_norm, ab_k_norm, ab_gate_bias, ab_out_norm,
           na_w_in, na_w_out, na_rpb, moe_wg, moe_bg, moe_we, moe_be, moe_w1, moe_w3, moe_w2):
    xp = x_prompt.reshape(T_PROMPT, D_MODEL)
    xs = x_sample.reshape(T_SAMPLE, D_MODEL)
    cond =jnp.zeros((N_COND, D_MODEL), F32).at[0].set(c_ctx).at[1:1 + DEC_BATCH].set(c)
    mod = _modulation(cond, ada_w, ada_b)
    gfin = norm_final.reshape(1, D_MODEL)

    w_in = ab_w_in[0]
    o_aq, o_ak, o_av, o_bq, o_bk, o_bv, o_bo, o_bg = np.cumsum((0,) + (A_WIDTH, A_KV_WIDTH, A_KV_WIDTH,
                                                                       B_WIDTH, B_WIDTH, B_WIDTH, B_WIDTH))
    wb = jnp.concatenate([w_in[:, o_bq:o_bk], w_in[:, o_bv:o_bg]], axis=1).astype(BF16)
    wq = w_in[:, o_aq:o_ak].astype(BF16)
    wkv = w_in[:, o_ak:o_bq].astype(BF16)
    gate_rows = np.array([0, 8, 1, 9])
    wg = w_in[:, o_bg:o_bg + 4 * B_HEADS].reshape(D_MODEL, 4, B_HEADS)
    wgt = jnp.zeros((B_HEADS, 16, D_MODEL), F32).at[:, gate_rows, :].set(wg.transpose(2, 1, 0))
    wgt = jnp.concatenate([w_in[:, o_bk:o_bv].T, wgt.reshape(16 * B_HEADS, D_MODEL)], axis=0).astype(BF16)
    g_mix = norm_mix[0].reshape(1, D_MODEL)
    ob_p, oq_p, okv_p, okt_p, ogt_p = _proj_ab(xp, mod[0], g_mix, wb, wq, wkv, wgt, 0)
    ob_s, oq_s, okv_s, okt_s, ogt_s = _proj_ab(xs, mod[0], g_mix, wb, wq, wkv, wgt, T_PROMPT)

    qn = jnp.tile(ab_q_norm[0], 2).reshape(1, LANES)
    kn = jnp.tile(ab_k_norm[0], 2).reshape(1, LANES)
    bd = _head_avg_matrix()
    cos, sin = _rope_tables()
    a_p, new_k, new_v = _attn_a_prompt(oq_p, okv_p, qn, kn, bd)
    ck = cache_attn_k[:, 0].reshape(DEC_BATCH, PAST_LEN, A_KV_WIDTH)
    cv = cache_attn_v[:, 0].reshape(DEC_BATCH, PAST_LEN, A_KV_WIDTH)
    a_s = _attn_a_sample(oq_s, okv_s, ck, cv, cos, sin, qn, kn, bd)

    gb = ab_gate_bias[0]
    brow = jnp.zeros((B_HEADS, 16, LANES), F32).at[:, gate_rows, :].set(
        jnp.broadcast_to(gb.T[:, :, None], (B_HEADS, 4, LANES)))
    onorm = ab_out_norm[0].reshape(B_HEADS, 1, B_HEAD_DIM)
    tri = _tri_matrices()
    b_p, cT, nT, mT = _mlstm(ob_p, okt_p, ogt_p, brow, onorm, tri, n=SEQ, nseq=BATCH, emit_state=True)
    n0 = jnp.broadcast_to(state_mlstm_n[:, 0][..., None], state_mlstm_C[:, 0].shape)
    c0 = jnp.concatenate([state_mlstm_C[:, 0], n0], axis=-1)
    m0 = jnp.broadcast_to(state_mlstm_m[:, 0].transpose(0, 2, 1)[..., None], (DEC_BATCH, B_HEADS, 2, LANES))
    (b_s,) = _mlstm(ob_s, okt_s, ogt_s, brow, onorm, tri, n=DEC_SEQ, nseq=DEC_BATCH, init=(c0, m0))

    w_out = ab_w_out[0].astype(BF16)
    wr, br = _router_weights(moe_wg[0], moe_bg[0], moe_we[0], moe_be[0])
    x1, xn, rt, *plan = _post([(xp, xs), (a_p, a_s), (b_p, b_s)], [w_out[:A_WIDTH], w_out[A_WIDTH:]], mod[0],
                              norm_ffn[0].reshape(1, D_MODEL), wr, br)
    dest, yb = _moe(xn, *plan, moe_w1, moe_w3, moe_w2, 0)

    g_mix = norm_mix[1].reshape(1, D_MODEL)
    w_in = na_w_in[0].astype(BF16)
    proj = (mod[1], g_mix, w_in)
    x_p, q_p, k_p, v_p = _combine(dest, yb, x1, rt, mod[0], 0, T_PROMPT, proj=proj)
    x_s, q_s, k_s, v_s = _combine(dest, yb, x1, rt, mod[0], T_PROMPT, T_SAMPLE, proj=proj)
    o_p = _attn_c_prompt(q_p, k_p, v_p)
    nck = cache_na_k[:, 0].reshape(DEC_BATCH, PAST_LEN, C_WIDTH)
    ncv = cache_na_v[:, 0].reshape(DEC_BATCH, PAST_LEN, C_WIDTH)
    o_s = _attn_c_sample(q_s, k_s, v_s, nck, ncv, _na_toeplitz(na_rpb[0]))
    wr, br = _router_weights(moe_wg[1], moe_bg[1], moe_we[1], moe_be[1])
    x1, xn, rt, *plan = _post([(x_p, x_s), (o_p, o_s)], [na_w_out[0].astype(BF16)], mod[1],
                              norm_ffn[1].reshape(1, D_MODEL), wr, br)
    dest, yb = _moe(xn, *plan, moe_w1, moe_w3, moe_w2, 1)
    (y_prompt,) = _combine(dest, yb, x1, rt, mod[1], 0, T_PROMPT, gfin=gfin)
    (y_sample,) = _combine(dest, yb, x1, rt, mod[1], T_PROMPT, T_SAMPLE, gfin=gfin)
    y_prompt = y_prompt.reshape(BATCH, SEQ, D_MODEL)
    y_sample = y_sample.reshape(DEC_BATCH, DEC_SEQ, D_MODEL)
    new_attn_k = new_k.reshape(BATCH, 1, SEQ, A_KV_HEADS, A_HEAD_DIM)
    new_attn_v = new_v.reshape(BATCH, 1, SEQ, A_KV_HEADS, A_HEAD_DIM)
    new_mlstm_C = cT[:, None]
    new_mlstm_n = nT.transpose(0, 2, 1, 3)[:, None]
    new_mlstm_m = mT[..., 0].transpose(0, 2, 1)[:, None]
    new_na_k = k_p.astype(F32).reshape(BATCH, 1, SEQ, C_HEADS, C_HEAD_DIM)
    new_na_v = v_p.astype(F32).reshape(BATCH, 1, SEQ, C_HEADS, C_HEAD_DIM)
    return (y_prompt, y_sample, new_attn_k, new_attn_v, new_mlstm_C, new_mlstm_n, new_mlstm_m,
            new_na_k, new_na_v)
```

```python
import functools

import numpy as np
import jax
import jax.numpy as jnp
from jax import lax
from jax.experimental import pallas as pl
from jax.experimental.pallas import tpu as pltpu

F32 = jnp.float32
BF16 = jnp.bfloat16

D_MODEL = 1024
BATCH = 32
SEQ = 256
DEC_BATCH = 4
DEC_SEQ = 2048
PAST_LEN = 256
GRID_W = 64
A_HEADS = 8
A_KV_HEADS = 2
A_HEAD_DIM = 64
A_WIDTH = A_HEADS * A_HEAD_DIM
A_KV_WIDTH = A_KV_HEADS * A_HEAD_DIM
B_HEADS = 4
B_HEAD_DIM = 128
B_WIDTH = B_HEADS * B_HEAD_DIM
MLSTM_CHUNK = 128
C_HEADS = 16
C_HEAD_DIM = 64
C_WIDTH = C_HEADS * C_HEAD_DIM
NA_ROWS = 8
NA_COLS = 16
N_GROUPS = 4
EXPERTS_PER_GROUP = 8
N_EXPERTS = N_GROUPS * EXPERTS_PER_GROUP
D_EXPERT = 512
ROPE_BASE = 10000.0
NORM_EPS = 1e-6

T_PROMPT = BATCH * SEQ
T_SAMPLE = DEC_BATCH * DEC_SEQ
T_ALL = T_PROMPT + T_SAMPLE
N_COND = 8
LANES = 128
SLABS = D_MODEL // LANES // 2
U32 = jnp.uint32
TOK_BLOCK = 256
POST_BLOCK = 2 * TOK_BLOCK
MLSTM_HEADS_PER_STEP = 4
EXPERT_ROWS = 512
NA_QROWS = 4
NA_KROWS = 12
NEG = -1e30
V7X_VMEM_BYTES = 64 * 1024 * 1024
VMEM_LIMIT = V7X_VMEM_BYTES - 8 * 1024 * 1024


def _cparams(sem):
    return pltpu.CompilerParams(dimension_semantics=sem, vmem_limit_bytes=VMEM_LIMIT)


def _split2(x):
    hi = x.astype(BF16)
    lo = (x - hi.astype(F32)).astype(BF16)
    return hi, lo


def _split3(x):
    hi = x.astype(BF16)
    r = x - hi.astype(F32)
    mid = r.astype(BF16)
    lo = (r - mid.astype(F32)).astype(BF16)
    return hi, mid, lo


def _dot(a, b):
    return jnp.dot(a, b, preferred_element_type=F32)


def _dot_nt(a, b):
    return lax.dot_general(a, b, (((1,), (1,)), ((), ())), preferred_element_type=F32)


def _dot_exact_rhs(x, b):
    hi, mid, lo = _split3(x)
    return _dot(hi, b) + _dot(mid, b) + _dot(lo, b)


def _rms(x, g):
    ms = jnp.mean(x * x, axis=-1, keepdims=True)
    return (x * lax.rsqrt(ms + NORM_EPS)) * g


def _mod_row(tok0):
    return jnp.where(tok0 < T_PROMPT, 0, 1 + (tok0 - T_PROMPT) // DEC_SEQ)


def _mod_part(mod_ref, r, idx):
    return mod_ref[pl.ds(r, 1), idx * D_MODEL:(idx + 1) * D_MODEL]


def _head_rms(x, w, bd):
    hi, lo = _split2(x * x)
    ms = _dot(hi, bd) + _dot(lo, bd)
    return (x * lax.rsqrt(ms + NORM_EPS)) * w


def _to_token_tiles(ref, x, tile0=0):
    m = x.shape[0]
    bits = lambda v: pltpu.bitcast(v.astype(BF16).astype(F32), U32)
    for s in range(SLABS):
        lo = bits(x[:, s * LANES:(s + 1) * LANES]) >> 16
        hi = bits(x[:, (s + SLABS) * LANES:(s + SLABS + 1) * LANES]) & U32(0xFFFF0000)
        ref[pl.ds(tile0 * SLABS + s, m, stride=SLABS), :] = lo | hi


def _from_token_tiles(ref, tile0, m):
    lo, hi = [], []
    for s in range(SLABS):
        w = ref[pl.ds(tile0 * SLABS + s, m, stride=SLABS), :]
        lo.append(pltpu.bitcast(w << 16, F32))
        hi.append(pltpu.bitcast(w & U32(0xFFFF0000), F32))
    return jnp.concatenate(lo + hi, axis=1)


def _lane(shape):
    return lax.broadcasted_iota(jnp.int32, shape, len(shape) - 1)


def _dup_half(x, g):
    xr = pltpu.roll(x, 64, 1)
    lo = _lane(x.shape) < 64
    return jnp.where(lo, x, xr) if g == 0 else jnp.where(lo, xr, x)


def _rope(x, cos, sin_signed):
    lane = _lane(x.shape)
    partner = jnp.where((lane % 32) < 16, pltpu.roll(x, LANES - 16, 1), pltpu.roll(x, 16, 1))
    return x * cos + partner * sin_signed


def _with_ones(v):
    return jnp.concatenate([v.astype(BF16), jnp.ones(v.shape, BF16)], axis=1)


def _softmax_pair(qp, score_fn, value_fn):
    lo = _lane(qp.shape) < 64
    outs = []
    for half in range(2):
        qm = jnp.where(lo if half == 0 else jnp.logical_not(lo), qp, 0.0).astype(BF16)
        ss = score_fn(qm)
        m = ss[0].max(axis=-1, keepdims=True)
        for s in ss[1:]:
            m = jnp.maximum(m, s.max(axis=-1, keepdims=True))
        o = value_fn([jnp.exp(s - m).astype(BF16) for s in ss])
        outs.append(o[:, :LANES] / o[:, LANES:])
    return jnp.where(lo, outs[0], outs[1])


def _mod_kernel(cond_ref, w_ref, b_ref, o_ref):
    c = cond_ref[...]
    s = c * jax.nn.sigmoid(c)
    s_hi, s_lo = _split2(s)
    w_hi, w_lo = _split2(w_ref[0])
    o_ref[0] = _dot(s_hi, w_hi) + _dot(s_lo, w_hi) + _dot(s_hi, w_lo) + b_ref[0]


def _modulation(cond, ada_w, ada_b):
    depth, d, n = ada_w.shape
    tn = 1536
    return pl.pallas_call(
        _mod_kernel,
        grid=(depth, n // tn),
        in_specs=[pl.BlockSpec((N_COND, d), lambda l, j: (0, 0)),
                  pl.BlockSpec((1, d, tn), lambda l, j: (l, 0, j)),
                  pl.BlockSpec((1, 1, tn), lambda l, j: (l, 0, j))],
        out_specs=pl.BlockSpec((1, N_COND, tn), lambda l, j: (l, 0, j)),
        out_shape=jax.ShapeDtypeStruct((depth, N_COND, n), F32),
        compiler_params=_cparams(("arbitrary", "arbitrary")),
        name="adaln_modulation",
    )(cond, ada_w, ada_b.reshape(depth, 1, n))


def _norm_mod(x_ref, mod_ref, g_ref, shift_idx, scale_idx, row0):
    r = _mod_row(row0 + pl.program_id(0) * x_ref.shape[0])
    h = _rms(x_ref[...], g_ref[...])
    return h * (1.0 + _mod_part(mod_ref, r, scale_idx)) + _mod_part(mod_ref, r, shift_idx)


def _proj_ab_kernel(x_ref, mod_ref, g_ref, wb_ref, wq_ref, wkv_ref, wgt_ref,
                    ob_ref, oq_ref, okv_ref, okt_ref, ogt_ref, *, row0):
    tm = x_ref.shape[0]
    hb = _norm_mod(x_ref, mod_ref, g_ref, 0, 1, row0).astype(BF16)
    ob_ref[...] = _dot(hb, wb_ref[...]).astype(ob_ref.dtype)
    oq_ref[...] = _dot(hb, wq_ref[...]).astype(oq_ref.dtype)
    okv_ref[...] = _dot(hb, wkv_ref[...]).astype(okv_ref.dtype)
    gt = _dot_nt(wgt_ref[...], hb)
    for j in range(tm // LANES):
        okt_ref[j] = gt[:B_WIDTH, j * LANES:(j + 1) * LANES].astype(okt_ref.dtype)
        ogt_ref[j] = gt[B_WIDTH:, j * LANES:(j + 1) * LANES]


def _proj_ab(x, mod, g, wb, wq, wkv, wgt, row0):
    t = x.shape[0]
    tm = 2 * TOK_BLOCK
    full = lambda a: pl.BlockSpec(a.shape, lambda i: (0,) * a.ndim)
    return pl.pallas_call(
        functools.partial(_proj_ab_kernel, row0=row0),
        grid=(t // tm,),
        in_specs=[pl.BlockSpec((tm, D_MODEL), lambda i: (i, 0)), full(mod), full(g),
                  full(wb), full(wq), full(wkv), full(wgt)],
        out_specs=[pl.BlockSpec((tm, wb.shape[1]), lambda i: (i, 0)),
                   pl.BlockSpec((tm, A_WIDTH), lambda i: (i, 0)),
                   pl.BlockSpec((tm, 2 * A_KV_WIDTH), lambda i: (i, 0)),
                   pl.BlockSpec((tm // LANES, B_WIDTH, LANES), lambda i: (i, 0, 0)),
                   pl.BlockSpec((tm // LANES, wgt.shape[0] - B_WIDTH, LANES), lambda i: (i, 0, 0))],
        out_shape=[jax.ShapeDtypeStruct((t, wb.shape[1]), BF16),
                   jax.ShapeDtypeStruct((t, A_WIDTH), BF16),
                   jax.ShapeDtypeStruct((t, 2 * A_KV_WIDTH), BF16),
                   jax.ShapeDtypeStruct((t // LANES, B_WIDTH, LANES), BF16),
                   jax.ShapeDtypeStruct((t // LANES, wgt.shape[0] - B_WIDTH, LANES), F32)],
        compiler_params=_cparams(("arbitrary",)),
        name="proj_ab",
    )(x, mod, g, wb, wq, wkv, wgt)


def _gqa_block(q_ref, qn, bd, kd_ref, vd_ref, o_ref, rope=None):
    for p in range(A_HEADS // 2):
        g = p // (A_HEADS // 2 // A_KV_HEADS)
        qp = _head_rms(q_ref[:, p * LANES:(p + 1) * LANES].astype(F32), qn, bd)
        if rope is not None:
            qp = _rope(qp, rope[0], rope[1])
        qp = qp * (A_HEAD_DIM ** -0.5)
        o_ref[:, p * LANES:(p + 1) * LANES] = _softmax_pair(
            qp, lambda qm: [_dot_nt(qm, kd_ref[g])], lambda ps: _dot(ps[0], vd_ref[g])).astype(o_ref.dtype)


def _attn_a_prompt_kernel(q_ref, kv_ref, qn_ref, kn_ref, bd_ref, o_ref, knew_ref, vnew_ref, kd_ref, vd_ref):
    bd = bd_ref[...]
    k = _head_rms(kv_ref[:, :LANES].astype(F32), kn_ref[...], bd)
    v = kv_ref[:, LANES:].astype(F32)
    knew_ref[...] = k
    vnew_ref[...] = v
    for g in range(A_KV_HEADS):
        kd_ref[g] = _dup_half(k, g).astype(BF16)
        vd_ref[g] = _with_ones(_dup_half(v, g))
    _gqa_block(q_ref, qn_ref[...], bd, kd_ref, vd_ref, o_ref)


def _attn_a_prompt(q, kv, qn, kn, bd):
    nb = BATCH
    full = lambda a: pl.BlockSpec(a.shape, lambda b: (0,) * a.ndim)
    return pl.pallas_call(
        _attn_a_prompt_kernel,
        grid=(nb,),
        in_specs=[pl.BlockSpec((SEQ, A_WIDTH), lambda b: (b, 0)),
                  pl.BlockSpec((SEQ, 2 * A_KV_WIDTH), lambda b: (b, 0)),
                  full(qn), full(kn), full(bd)],
        out_specs=[pl.BlockSpec((SEQ, A_WIDTH), lambda b: (b, 0)),
                   pl.BlockSpec((SEQ, A_KV_WIDTH), lambda b: (b, 0)),
                   pl.BlockSpec((SEQ, A_KV_WIDTH), lambda b: (b, 0))],
        out_shape=[jax.ShapeDtypeStruct((T_PROMPT, A_WIDTH), BF16),
                   jax.ShapeDtypeStruct((T_PROMPT, A_KV_WIDTH), F32),
                   jax.ShapeDtypeStruct((T_PROMPT, A_KV_WIDTH), F32)],
        scratch_shapes=[pltpu.VMEM((A_KV_HEADS, SEQ, LANES), BF16),
                        pltpu.VMEM((A_KV_HEADS, SEQ, 2 * LANES), BF16)],
        compiler_params=_cparams(("arbitrary",)),
        name="attn_a_prompt",
    )(q, kv, qn, kn, bd)


_A_QBLOCK = 256


def _attn_a_sample_kernel(q_ref, kv_ref, ck_ref, cv_ref, cos_ref, sin_ref, cosq_ref, sinq_ref,
                          qn_ref, kn_ref, bd_ref, o_ref, kd_ref, vd_ref):
    bd = bd_ref[...]

    @pl.when(pl.program_id(1) == 0)
    def _():
        for g in range(A_KV_HEADS):
            kd_ref[g, :PAST_LEN] = _dup_half(ck_ref[0], g).astype(BF16)
            vd_ref[g, :PAST_LEN] = _with_ones(_dup_half(cv_ref[0], g))
        rows = 256
        for c in range(DEC_SEQ // rows):
            sl = slice(c * rows, (c + 1) * rows)
            k = _head_rms(kv_ref[sl, :LANES].astype(F32), kn_ref[...], bd)
            k = _rope(k, cos_ref[sl, :], sin_ref[sl, :])
            v = kv_ref[sl, LANES:].astype(F32)
            dst = slice(PAST_LEN + c * rows, PAST_LEN + (c + 1) * rows)
            for g in range(A_KV_HEADS):
                kd_ref[g, dst] = _dup_half(k, g).astype(BF16)
                vd_ref[g, dst] = _with_ones(_dup_half(v, g))

    _gqa_block(q_ref, qn_ref[...], bd, kd_ref, vd_ref, o_ref, rope=(cosq_ref[...], sinq_ref[...]))


def _attn_a_sample(q, kv, ck, cv, cos, sin, qn, kn, bd):
    nq = DEC_SEQ // _A_QBLOCK
    full = lambda a: pl.BlockSpec(a.shape, lambda b, i: (0,) * a.ndim)
    tk = PAST_LEN + DEC_SEQ
    return pl.pallas_call(
        _attn_a_sample_kernel,
        grid=(DEC_BATCH, nq),
        in_specs=[pl.BlockSpec((_A_QBLOCK, A_WIDTH), lambda b, i: (b * nq + i, 0)),
                  pl.BlockSpec((DEC_SEQ, 2 * A_KV_WIDTH), lambda b, i: (b, 0)),
                  pl.BlockSpec((1, PAST_LEN, A_KV_WIDTH), lambda b, i: (b, 0, 0)),
                  pl.BlockSpec((1, PAST_LEN, A_KV_WIDTH), lambda b, i: (b, 0, 0)),
                  full(cos), full(sin),
                  pl.BlockSpec((_A_QBLOCK, LANES), lambda b, i: (i, 0)),
                  pl.BlockSpec((_A_QBLOCK, LANES), lambda b, i: (i, 0)),
                  full(qn), full(kn), full(bd)],
        out_specs=pl.BlockSpec((_A_QBLOCK, A_WIDTH), lambda b, i: (b * nq + i, 0)),
        out_shape=jax.ShapeDtypeStruct((T_SAMPLE, A_WIDTH), BF16),
        scratch_shapes=[pltpu.VMEM((A_KV_HEADS, tk, LANES), BF16),
                        pltpu.VMEM((A_KV_HEADS, tk, 2 * LANES), BF16)],
        compiler_params=_cparams(("arbitrary", "arbitrary")),
        name="attn_a_sample",
    )(q, kv, ck, cv, cos, sin, cos, sin, qn, kn, bd)


def _log_sigmoid(x):
    return -(jnp.maximum(-x, 0.0) + jnp.log1p(jnp.exp(-jnp.abs(x))))


def _mlstm_kernel(*refs, n, has_init, emit_state):
    it = iter(refs)
    q_ref, v_ref, og_ref, kt_ref, gt_ref, brow_ref, onorm_ref, tri_ref = [next(it) for _ in range(8)]
    if has_init:
        c0_ref, m0_ref = [next(it) for _ in range(2)]
    out_ref = next(it)
    if emit_state:
        cT_ref, nT_ref, mT_ref = [next(it) for _ in range(3)]
    h_ref, row_ref, c_ref = [next(it) for _ in range(3)]

    L = MLSTM_CHUNK
    D = B_HEAD_DIM
    nc = n // L
    hb = MLSTM_HEADS_PER_STEP

    lane = lax.broadcasted_iota(jnp.int32, (nc * 8, L), 1)
    is_fwd = lax.broadcasted_iota(jnp.int32, (nc * 8, L), 0) % 8 == 0

    def running_max(x, suffix):
        for sh in (1, 2, 4, 8, 16, 32, 64):
            if suffix:
                x = jnp.where(lane < L - sh, jnp.maximum(x, pltpu.roll(x, L - sh, 1)), x)
            else:
                x = jnp.where(lane >= sh, jnp.maximum(x, pltpu.roll(x, sh, 1)), x)
        return x

    for hh in range(hb):
        gt = gt_ref[:, 16 * hh:16 * hh + 16, :] + brow_ref[hh][None]
        li = gt[:, 0:8, :].reshape(nc * 8, L)
        lf = _log_sigmoid(gt[:, 8:16, :]).reshape(nc * 8, L)
        cum = jnp.where(is_fwd, _dot_exact_rhs(lf, tri_ref[0]), _dot_exact_rhs(lf, tri_ref[1]))
        a = li - cum
        planes = (cum, a, jnp.broadcast_to(lf.sum(axis=-1, keepdims=True), (nc * 8, L)),
                  jnp.where(is_fwd, running_max(a, False), running_max(a, True)),
                  jnp.broadcast_to(a.max(axis=-1, keepdims=True), (nc * 8, L)))
        for p, val in enumerate(planes):
            row_ref[hh, p] = val.reshape(nc, 8, L)

    chains = [(hh, d) for hh in range(hb) for d in range(2)]
    if has_init:
        m_init = []
        for hh, d in chains:
            c_ref[2 * hh + d] = c0_ref[0, d, hh]
            m_init.append(m0_ref[0, hh, d:d + 1, :])
        m_init = tuple(m_init)
    else:
        c_ref[...] = jnp.zeros(c_ref.shape, F32)
        m_init = tuple(jnp.zeros((1, L), F32) for _ in chains)

    t_idx = lax.broadcasted_iota(jnp.int32, (L, L), 0)
    s_idx = lax.broadcasted_iota(jnp.int32, (L, L), 1)
    masks = (s_idx <= t_idx, s_idx >= t_idx)
    ones = jnp.ones((L, L), F32)

    def step(c, hh, d, m):
        r0 = pl.multiple_of(c * L, L)
        hl = slice(hh * L, (hh + 1) * L)
        qb = q_ref[pl.ds(r0, L), hl].astype(BF16)
        kst = kt_ref[c, hl, :].astype(F32) * (D ** -0.5)
        v_ext = jnp.concatenate([v_ref[pl.ds(r0, L), hl], ones], axis=1).astype(BF16)
        cum, a_row, tot, amax_run, amax = [row_ref[hh, p, c][d:d + 1, :] for p in range(5)]
        m_cum = jnp.broadcast_to(cum, (L, L)).T
        m_run = jnp.broadcast_to(amax_run, (L, L)).T
        dlog = jnp.where(masks[d], m_cum + a_row, -jnp.inf)
        inter = m_cum + m
        m_t = jnp.maximum(inter, m_cum + m_run)
        w_in = jnp.exp(dlog - m_t)
        w_st = jnp.exp(inter - m_t)
        a = _dot(qb, kst.astype(BF16)) * w_in
        ci = 2 * hh + d
        cext = c_ref[ci]
        p_state = _dot(qb, cext.astype(BF16))
        p_intra = _dot(a.astype(BF16), v_ext)
        num = w_st * p_state[:, :D] + p_intra[:, :D]
        den = w_st * p_state[:, D:] + p_intra[:, D:]
        h_ref[ci, pl.ds(r0, L), :] = num / jnp.maximum(jnp.abs(den), jnp.exp(-m_t))
        m_new = jnp.maximum(tot + m, amax + tot)
        ws = jnp.exp(a_row + tot - m_new)
        wc = jnp.exp(tot + m - m_new)
        c_ref[ci] = jnp.concatenate([wc, wc], axis=1) * cext + _dot((kst * ws).astype(BF16), v_ext)
        return m_new

    def body(i, carry):
        return tuple(step(i if d == 0 else nc - 1 - i, hh, d, m) for (hh, d), m in zip(chains, carry))

    m_fin = lax.fori_loop(0, nc, body, m_init)

    for hh in range(hb):
        hm = h_ref[2 * hh] + h_ref[2 * hh + 1]
        hl = slice(hh * L, (hh + 1) * L)
        gate = jax.nn.sigmoid(og_ref[:, hl].astype(F32))
        out_ref[:, hl] = (_rms(hm, onorm_ref[hh]) * gate).astype(out_ref.dtype)

    if emit_state:
        for k, (hh, d) in enumerate(chains):
            cext = c_ref[2 * hh + d]
            cT_ref[0, d, hh] = cext[:, :D]
            nT_ref[0, hh, d:d + 1, :] = cext[:, D:].T[0:1, :]
            mT_ref[0, hh, d:d + 1, :] = m_fin[k]


def _mlstm(ob, okt, ogt, brow, onorm, tri, *, n, nseq, init=None, emit_state=False):
    L = MLSTM_CHUNK
    nc = n // L
    H = B_HEADS
    hb = MLSTM_HEADS_PER_STEP
    ng = H // hb
    col = lambda part: (lambda b, g: (b, part * ng + g))
    in_specs = [pl.BlockSpec((n, hb * L), col(0)), pl.BlockSpec((n, hb * L), col(1)),
                pl.BlockSpec((n, hb * L), col(2)),
                pl.BlockSpec((nc, hb * L, L), lambda b, g: (b, g, 0)),
                pl.BlockSpec((nc, 16 * hb, L), lambda b, g: (b, g, 0)),
                pl.BlockSpec((hb, 16, L), lambda b, g: (g, 0, 0)),
                pl.BlockSpec((hb, 1, L), lambda b, g: (g, 0, 0)),
                pl.BlockSpec(tri.shape, lambda b, g: (0, 0, 0))]
    args = [ob, ob, ob, okt, ogt, brow, onorm, tri]
    if init is not None:
        in_specs += [pl.BlockSpec((1, 2, hb, L, 2 * L), lambda b, g: (b, 0, g, 0, 0)),
                     pl.BlockSpec((1, hb, 2, L), lambda b, g: (b, g, 0, 0))]
        args += list(init)
    out_specs = [pl.BlockSpec((n, hb * L), lambda b, g: (b, g))]
    out_shape = [jax.ShapeDtypeStruct((nseq * n, B_WIDTH), BF16)]
    if emit_state:
        out_specs += [pl.BlockSpec((1, 2, hb, L, L), lambda b, g: (b, 0, g, 0, 0)),
                      pl.BlockSpec((1, hb, 2, L), lambda b, g: (b, g, 0, 0)),
                      pl.BlockSpec((1, hb, 2, L), lambda b, g: (b, g, 0, 0))]
        out_shape += [jax.ShapeDtypeStruct((nseq, 2, H, L, L), F32),
                      jax.ShapeDtypeStruct((nseq, H, 2, L), F32),
                      jax.ShapeDtypeStruct((nseq, H, 2, L), F32)]
    return pl.pallas_call(
        functools.partial(_mlstm_kernel, n=n, has_init=init is not None, emit_state=emit_state),
        grid=(nseq, ng),
        in_specs=in_specs,
        out_specs=out_specs,
        out_shape=out_shape,
        scratch_shapes=[pltpu.VMEM((2 * hb, n, L), F32),
                        pltpu.VMEM((hb, 5, nc, 8, L), F32),
                        pltpu.VMEM((2 * hb, L, 2 * L), F32)],
        compiler_params=_cparams(("arbitrary", "arbitrary")),
        name="mlstm_init" if init is not None else "mlstm",
    )(*args)


def _router(logits):
    lane = _lane(logits.shape).astype(F32)
    big = 1e9
    gl = jnp.where(lane < N_GROUPS, logits, -jnp.inf)
    gmax = gl.max(axis=-1, keepdims=True)
    g_sel = jnp.where(gl == gmax, lane, big).min(axis=-1, keepdims=True)
    g_prob = 1.0 / jnp.exp(gl - gmax).sum(axis=-1, keepdims=True)
    lo = N_GROUPS + EXPERTS_PER_GROUP * g_sel
    el = jnp.where(lane >= lo, jnp.where(lane < lo + EXPERTS_PER_GROUP, logits, -jnp.inf), -jnp.inf)
    v1 = el.max(axis=-1, keepdims=True)
    i1 = jnp.where(el == v1, lane, big).min(axis=-1, keepdims=True)
    el2 = jnp.where(lane == i1, -jnp.inf, el)
    v2 = el2.max(axis=-1, keepdims=True)
    i2 = jnp.where(el2 == v2, lane, big).min(axis=-1, keepdims=True)
    e2 = jnp.exp(v2 - v1)
    w1 = g_prob / (1.0 + e2)
    w2 = g_prob * e2 / (1.0 + e2)
    return i1, i2, w1, w2


def _read_tokens(refs, is_prompt, rows):
    if len(refs) == 1:
        return refs[0][rows, :]
    return jnp.where(is_prompt, refs[0][rows, :], refs[1][rows, :])


def _post_kernel(*refs, groups):
    it = iter(refs)
    tok_refs = [[next(it) for _ in range(n)] for n in groups]
    w_refs = [next(it) for _ in range(len(groups) - 1)]
    mod_ref, g_ref, wr_ref, br_ref, ls_ref, sel_ref, su_ref, tokrow_ref = [next(it) for _ in range(8)]
    xnew_ref, xn_ref, rt_ref, rtt_ref, cnt_ref, order_ref, tab_ref = [next(it) for _ in range(7)]
    run_ref = next(it)
    i = pl.program_id(0)
    tm = xnew_ref.shape[0]
    is_prompt = i * tm < T_PROMPT
    r = _mod_row(i * tm)

    @pl.when(i == 0)
    def _():
        run_ref[...] = jnp.zeros(run_ref.shape, F32)

    sub = ls_ref.shape[0]
    for h in range(tm // sub):
        rows = slice(h * sub, (h + 1) * sub)
        acc = None
        for a_refs, w_ref in zip(tok_refs[1:], w_refs):
            d = _dot(_read_tokens(a_refs, is_prompt, rows).astype(BF16), w_ref[...])
            acc = d if acc is None else acc + d
        xnew = _read_tokens(tok_refs[0], is_prompt, rows) + _mod_part(mod_ref, r, 2) * acc
        xnew_ref[rows, :] = xnew
        xn = _rms(xnew, g_ref[...]) * (1.0 + _mod_part(mod_ref, r, 4)) + _mod_part(mod_ref, r, 3)
        _to_token_tiles(xn_ref, xn, h * sub)
        x_hi, x_lo = _split2(xn)
        both = _dot(x_hi, wr_ref[...])
        logits = both[:, :LANES] + both[:, LANES:] + _dot(x_lo, wr_ref[:, :LANES]) + br_ref[...]
        i1, i2, w1, w2 = _router(logits)

        lane = _lane(logits.shape).astype(F32)
        member = jnp.where(lane == i1, 1.0, jnp.where(lane == i2, 1.0, 0.0))
        local = _dot(ls_ref[...], member.astype(BF16))
        run = run_ref[...]
        before = local + run
        rank1 = jnp.where(lane == i1, before, 0.0).sum(axis=-1, keepdims=True)
        rank2 = jnp.where(lane == i2, before, 0.0).sum(axis=-1, keepdims=True)
        n_here = member.sum(axis=0, keepdims=True)
        run_ref[...] = run + n_here

        off = _dot(jnp.broadcast_to(n_here, (8, LANES)).astype(BF16), su_ref[...])[0:1]
        pos = off + local
        p1 = jnp.where(lane == i1, pos, 0.0).sum(axis=-1, keepdims=True)
        p2 = jnp.where(lane == i2, pos, 0.0).sum(axis=-1, keepdims=True)
        slot = lax.broadcasted_iota(jnp.int32, (sub, 2 * sub), 1).astype(F32)
        onehot = jnp.where(slot == p1, 1.0, jnp.where(slot == p2, 1.0, 0.0)).astype(BF16)
        order_ref[:, h * 2 * sub:(h + 1) * 2 * sub] = _dot(tokrow_ref[...], onehot) + (i * tm + h * sub).astype(F32)
        row8 = lax.broadcasted_iota(jnp.int32, (8, LANES), 0)
        tab_ref[h] = jnp.where(row8 == 0, n_here, jnp.where(row8 == 1, run, jnp.where(row8 == 2, off, 0.0)))
        cols = (i1 - N_GROUPS, i2 - N_GROUPS, w1, w2, rank1, rank2)
        rt = jnp.zeros(logits.shape, F32)
        for k, c in enumerate(cols):
            rt = jnp.where(lane == k, c, rt)
        rt_ref[rows, :] = rt
        hi, mid, lo = _split3(rt)
        sel = sel_ref[...]
        rtt_ref[:, rows] = _dot_nt(sel, hi) + _dot_nt(sel, mid) + _dot_nt(sel, lo)
    cnt_ref[...] = run_ref[...]


def _post(tok_ops, w_list, mod, g, wr, br):
    t = T_ALL
    tm = POST_BLOCK
    npb = T_PROMPT // tm
    full = lambda a: pl.BlockSpec(a.shape, lambda i: (0,) * a.ndim)
    specs, args, groups = [], [], []
    for op in tok_ops:
        if isinstance(op, tuple):
            w = op[0].shape[1]
            specs += [pl.BlockSpec((tm, w), lambda i: (jnp.minimum(i, npb - 1), 0)),
                      pl.BlockSpec((tm, w), lambda i: (jnp.maximum(i - npb, 0), 0))]
            args += list(op)
            groups.append(2)
        else:
            specs.append(pl.BlockSpec((tm, op.shape[1]), lambda i: (i, 0)))
            args.append(op)
            groups.append(1)
    sub = TOK_BLOCK
    idx = np.arange(sub)
    ls = jnp.asarray(idx[:, None] > idx[None, :], BF16)
    sel = jnp.asarray(np.arange(8)[:, None] == np.arange(LANES)[None, :], BF16)
    lanes = np.arange(LANES)
    su = jnp.asarray(lanes[:, None] < lanes[None, :], BF16)
    tokrow = jnp.asarray(np.where(np.arange(8)[:, None] == 0, idx[None, :], 0), BF16)
    consts = [mod, g, wr, br, ls, sel, su, tokrow]
    return pl.pallas_call(
        functools.partial(_post_kernel, groups=tuple(groups)),
        grid=(t // tm,),
        in_specs=specs + [full(w) for w in w_list] + [full(a) for a in consts],
        out_specs=[pl.BlockSpec((tm, D_MODEL), lambda i: (i, 0)),
                   pl.BlockSpec((tm * SLABS, LANES), lambda i: (i, 0)),
                   pl.BlockSpec((tm, LANES), lambda i: (i, 0)),
                   pl.BlockSpec((8, tm), lambda i: (0, i)),
                   pl.BlockSpec((1, LANES), lambda i: (0, 0)),
                   pl.BlockSpec((8, 2 * tm), lambda i: (0, i)),
                   pl.BlockSpec((tm // sub, 8, LANES), lambda i: (i, 0, 0))],
        out_shape=[jax.ShapeDtypeStruct((t, D_MODEL), F32),
                   jax.ShapeDtypeStruct((t * SLABS, LANES), U32),
                   jax.ShapeDtypeStruct((t, LANES), F32),
                   jax.ShapeDtypeStruct((8, t), F32),
                   jax.ShapeDtypeStruct((1, LANES), F32),
                   jax.ShapeDtypeStruct((8, 2 * t), F32),
                   jax.ShapeDtypeStruct((t // sub, 8, LANES), F32)],
        scratch_shapes=[pltpu.VMEM((1, LANES), F32)],
        compiler_params=_cparams(("arbitrary",)),
        name="post_mixer_router",
    )(*args, *w_list, *consts)


def _expert_kernel(be_ref, src_ref, tok_ref, nv_ref, xn_hbm, w1_ref, w3_ref, w2_ref, o_ref,
                   xa, xb, sem, w1b, w3b, w2b):
    i = pl.program_id(0)
    nv = nv_ref[0]
    active = i < nv
    rows = EXPERT_ROWS

    def gather(blk, buf, s):
        for r in range(rows):
            tok = tok_ref[src_ref[blk * rows + r]]
            pltpu.make_async_copy(xn_hbm.at[pl.ds(pl.multiple_of(tok * SLABS, SLABS), SLABS), :],
                                  buf.at[pl.ds(r * SLABS, SLABS), :], sem.at[s]).start(priority=r % 2)

    def wait(buf, s):
        pltpu.make_async_copy(xn_hbm.at[pl.ds(0, rows * SLABS), :], buf, sem.at[s]).wait()

    @pl.when(i == 0)
    def _():
        gather(0, xa, 0)

    changed = jnp.logical_or(i == 0, be_ref[i] != be_ref[jnp.maximum(i - 1, 0)])

    @pl.when(jnp.logical_and(changed, active))
    def _():
        w1b[...] = w1_ref[0, 0].astype(BF16)
        w3b[...] = w3_ref[0, 0].astype(BF16)
        w2b[...] = w2_ref[0, 0].astype(BF16)

    nxt = jnp.minimum(i + 1, nv - 1)

    def step(cur, s_cur, oth, s_oth):
        wait(cur, s_cur)
        gather(nxt, oth, s_oth)
        x = _from_token_tiles(cur, 0, rows).astype(BF16)
        h1 = _dot(x, w1b[...])
        h3 = _dot(x, w3b[...])
        hid = (h1 * jax.nn.sigmoid(h1)) * h3
        _to_token_tiles(o_ref, _dot(hid.astype(BF16), w2b[...]))

    @pl.when(jnp.logical_and(active, i % 2 == 0))
    def _():
        step(xa, 0, xb, 1)

    @pl.when(jnp.logical_and(active, i % 2 == 1))
    def _():
        step(xb, 1, xa, 0)

    @pl.when(jnp.logical_not(active))
    def _():
        o_ref[...] = jnp.zeros(o_ref.shape, o_ref.dtype)

    @pl.when(i == nv - 1)
    def _():
        @pl.when(i % 2 == 0)
        def _():
            wait(xb, 1)

        @pl.when(i % 2 == 1)
        def _():
            wait(xa, 0)


def _experts(block_expert, slot_src, pair_tok, n_valid, xn, w1, w3, w2, layer):
    nblk = block_expert.shape[0]
    rows = EXPERT_ROWS
    weights = lambda i, be, *_: (layer, be[i], 0, 0)
    grid_spec = pltpu.PrefetchScalarGridSpec(
        num_scalar_prefetch=4,
        grid=(nblk,),
        in_specs=[pl.BlockSpec(memory_space=pl.ANY),
                  pl.BlockSpec((1, 1, D_MODEL, D_EXPERT), weights),
                  pl.BlockSpec((1, 1, D_MODEL, D_EXPERT), weights),
                  pl.BlockSpec((1, 1, D_EXPERT, D_MODEL), weights)],
        out_specs=pl.BlockSpec((rows * SLABS, LANES), lambda i, *_: (i, 0)),
        scratch_shapes=[pltpu.VMEM((rows * SLABS, LANES), U32),
                        pltpu.VMEM((rows * SLABS, LANES), U32),
                        pltpu.SemaphoreType.DMA((2,)),
                        pltpu.VMEM((D_MODEL, D_EXPERT), BF16),
                        pltpu.VMEM((D_MODEL, D_EXPERT), BF16),
                        pltpu.VMEM((D_EXPERT, D_MODEL), BF16)])
    return pl.pallas_call(
        _expert_kernel,
        grid_spec=grid_spec,
        out_shape=jax.ShapeDtypeStruct((nblk * rows * SLABS, LANES), U32),
        compiler_params=_cparams(("arbitrary",)),
        name="moe_experts",
    )(block_expert, slot_src, pair_tok, n_valid, xn, w1, w3, w2)


def _combine_kernel(*refs, row0, project, final_norm):
    it = iter(refs)
    dest_ref, yb_hbm, x_ref, rt_ref, mod0_ref = [next(it) for _ in range(5)]
    if project:
        mod1_ref, g_ref, w_ref = [next(it) for _ in range(3)]
    if final_norm:
        gfin_ref = next(it)
    xo_ref = next(it)
    if project:
        q_ref, k_ref, v_ref = [next(it) for _ in range(3)]
    ya, yb, sem = [next(it) for _ in range(3)]
    i = pl.program_id(0)
    nblk = pl.num_programs(0)
    tm = x_ref.shape[0]

    def gather(blk, buf, s):
        for j in range(tm):
            for c in range(2):
                d = dest_ref[c * T_ALL + row0 + blk * tm + j]
                pltpu.make_async_copy(yb_hbm.at[pl.ds(pl.multiple_of(d * SLABS, SLABS), SLABS), :],
                                      buf.at[pl.ds((c * tm + j) * SLABS, SLABS), :], sem.at[s]).start(priority=c)

    def wait(buf, s):
        pltpu.make_async_copy(yb_hbm.at[pl.ds(0, 2 * tm * SLABS), :], buf, sem.at[s]).wait()

    @pl.when(i == 0)
    def _():
        gather(0, ya, 0)

    nxt = jnp.minimum(i + 1, nblk - 1)

    def step(cur, s_cur, oth, s_oth):
        wait(cur, s_cur)
        gather(nxt, oth, s_oth)
        r = _mod_row(row0 + i * tm)
        rt = rt_ref[...]
        y = rt[:, 2:3] * _from_token_tiles(cur, 0, tm) + rt[:, 3:4] * _from_token_tiles(cur, tm, tm)
        x = x_ref[...] + _mod_part(mod0_ref, r, 5) * y
        xo_ref[...] = _rms(x, gfin_ref[...]) if final_norm else x
        if project:
            h = _rms(x, g_ref[...]) * (1.0 + _mod_part(mod1_ref, r, 1)) + _mod_part(mod1_ref, r, 0)
            hb = h.astype(BF16)
            for j, o_ref in enumerate((q_ref, k_ref, v_ref)):
                o_ref[...] = _dot(hb, w_ref[:, j * C_WIDTH:(j + 1) * C_WIDTH]).astype(o_ref.dtype)

    @pl.when(i % 2 == 0)
    def _():
        step(ya, 0, yb, 1)

    @pl.when(i % 2 == 1)
    def _():
        step(yb, 1, ya, 0)

    @pl.when(i == nblk - 1)
    def _():
        @pl.when(i % 2 == 0)
        def _():
            wait(yb, 1)

        @pl.when(i % 2 == 1)
        def _():
            wait(ya, 0)


def _combine(dest, yb, x, rt, mod0, row0, t, proj=None, gfin=None):
    tm = TOK_BLOCK
    blk0 = row0 // tm
    full = lambda a: pl.BlockSpec(a.shape, lambda i, d: (0,) * a.ndim)
    extra = list(proj) if proj is not None else []
    if gfin is not None:
        extra.append(gfin)
    n_out = 4 if proj is not None else 1
    out_shape = [jax.ShapeDtypeStruct((t, D_MODEL), F32)] + [jax.ShapeDtypeStruct((t, C_WIDTH), BF16)] * (n_out - 1)
    grid_spec = pltpu.PrefetchScalarGridSpec(
        num_scalar_prefetch=1,
        grid=(t // tm,),
        in_specs=[pl.BlockSpec(memory_space=pl.ANY),
                  pl.BlockSpec((tm, D_MODEL), lambda i, d: (blk0 + i, 0)),
                  pl.BlockSpec((tm, LANES), lambda i, d: (blk0 + i, 0)),
                  full(mod0)] + [full(a) for a in extra],
        out_specs=[pl.BlockSpec((tm, D_MODEL), lambda i, d: (i, 0))] * n_out,
        scratch_shapes=[pltpu.VMEM((2 * tm * SLABS, LANES), U32),
                        pltpu.VMEM((2 * tm * SLABS, LANES), U32),
                        pltpu.SemaphoreType.DMA((2,))])
    return pl.pallas_call(
        functools.partial(_combine_kernel, row0=row0, project=proj is not None, final_norm=gfin is not None),
        grid_spec=grid_spec,
        out_shape=out_shape,
        compiler_params=_cparams(("arbitrary",)),
        name="moe_combine_proj" if proj is not None else "moe_combine",
    )(dest, yb, x, rt, mod0, *extra)


def _moe_plan(rtt, cnt, order, tab):
    t = rtt.shape[1]
    eid = rtt[0:2].astype(jnp.int32)
    rank = rtt[4:6].astype(jnp.int32)
    counts = cnt[0, N_GROUPS:N_GROUPS + N_EXPERTS].astype(jnp.int32)
    padded = (counts + EXPERT_ROWS - 1) // EXPERT_ROWS * EXPERT_ROWS
    seg_end = jnp.cumsum(padded)
    seg_start = seg_end - padded
    experts = jnp.arange(N_EXPERTS, dtype=jnp.int32)
    start = jnp.sum(jnp.where(eid[..., None] == experts, seg_start, 0), axis=-1)
    dest = (start + rank).reshape(-1)
    n_blocks = (2 * t + N_EXPERTS * (EXPERT_ROWS - 1) + EXPERT_ROWS - 1) // EXPERT_ROWS
    first_row = jnp.arange(n_blocks, dtype=jnp.int32) * EXPERT_ROWS
    block_expert = jnp.minimum(jnp.sum((seg_end[None, :] <= first_row[:, None]).astype(jnp.int32), axis=1),
                               N_EXPERTS - 1)
    n_valid = (seg_end[-1:] // EXPERT_ROWS).astype(jnp.int32)

    nsb = tab.shape[0]
    n_sb = tab[:, 0, N_GROUPS:N_GROUPS + N_EXPERTS].astype(jnp.int32).T[block_expert]
    run_sb = tab[:, 1, N_GROUPS:N_GROUPS + N_EXPERTS].astype(jnp.int32).T[block_expert]
    off_sb = tab[:, 2, N_GROUPS:N_GROUPS + N_EXPERTS].astype(jnp.int32).T[block_expert]
    rank = (first_row - seg_start[block_expert])[:, None] + jnp.arange(EXPERT_ROWS, dtype=jnp.int32)[None, :]
    real = rank < counts[block_expert][:, None]
    sb = jnp.sum(((run_sb + n_sb)[:, None, :] <= rank[:, :, None]).astype(jnp.int32), axis=-1)
    sb = jnp.minimum(sb, nsb - 1)
    pick = sb[:, :, None] == jnp.arange(nsb, dtype=jnp.int32)[None, None, :]
    shift = jnp.sum(jnp.where(pick, (off_sb - run_sb)[:, None, :], 0), axis=-1)
    src = jnp.clip(sb * (2 * TOK_BLOCK) + shift + rank, 0, 2 * t - 1)
    slot = first_row[:, None] + jnp.arange(EXPERT_ROWS, dtype=jnp.int32)[None, :]
    slot_src = jnp.where(real, src, slot % (2 * t)).reshape(-1)
    return dest, slot_src, order[0].astype(jnp.int32), block_expert, n_valid


def _attn_c_prompt_kernel(q_ref, k_ref, v_ref, o_ref):
    for p in range(C_HEADS // 2):
        sl = slice(p * LANES, (p + 1) * LANES)
        kb = k_ref[:, sl].astype(BF16)
        vb = _with_ones(v_ref[:, sl])
        qp = q_ref[:, sl].astype(F32) * (C_HEAD_DIM ** -0.5)
        o_ref[:, sl] = _softmax_pair(qp, lambda qm: [_dot_nt(qm, kb)],
                                     lambda ps: _dot(ps[0], vb)).astype(o_ref.dtype)


def _attn_c_prompt(q, k, v):
    blk = pl.BlockSpec((SEQ, C_WIDTH), lambda b: (b, 0))
    return pl.pallas_call(
        _attn_c_prompt_kernel,
        grid=(BATCH,),
        in_specs=[blk, blk, blk],
        out_specs=blk,
        out_shape=jax.ShapeDtypeStruct((T_PROMPT, C_WIDTH), BF16),
        compiler_params=_cparams(("arbitrary",)),
        name="attn_c_prompt",
    )(q, k, v)


def _na_key_start(r0):
    rows = DEC_SEQ // GRID_W
    return jnp.minimum(jnp.clip(r0 - NA_ROWS // 2, 0, rows - NA_ROWS), rows - NA_KROWS)


def _na_block_plan():
    rows = DEC_SEQ // GRID_W
    nblk = rows // NA_QROWS
    plan = []
    for blk in (0, 1, nblk - 1):
        r0 = blk * NA_QROWS
        ks = min(int(np.clip(r0 - NA_ROWS // 2, 0, rows - NA_ROWS)), rows - NA_KROWS)
        per_row = []
        for i in range(NA_QROWS):
            r = r0 + i
            rs = int(np.clip(r - NA_ROWS // 2, 0, rows - NA_ROWS))
            start = ks - r + NA_ROWS - 1 + NA_KROWS
            ok = [rs <= ks + j < rs + NA_ROWS for j in range(NA_KROWS)]
            per_row.append((start, ok))
        plan.append(per_row)
    return plan


def _attn_c_sample_kernel(q_ref, k_ref, v_ref, ck_ref, cv_ref, toe_ref, o_ref, bias_ref):
    rows = DEC_SEQ // GRID_W
    nblk = rows // NA_QROWS
    w = GRID_W

    @pl.when(jnp.logical_and(pl.program_id(1) == 0, pl.program_id(2) == 0))
    def _():
        neg = jnp.full((w, w), NEG, F32)
        for t, per_row in enumerate(_na_block_plan()):
            for half in range(2):
                for i, (start, ok) in enumerate(per_row):
                    for j in range(0, NA_KROWS, 2):
                        pieces = [toe_ref[0, half, start + jj] if ok[jj] else neg for jj in (j, j + 1)]
                        bias_ref[t, half, i * w:(i + 1) * w, j * w:(j + 2) * w] = jnp.concatenate(pieces, axis=1)

    i = pl.program_id(2)
    r0 = i * NA_QROWS
    k0 = pl.multiple_of(_na_key_start(r0) * GRID_W, GRID_W)
    btype = jnp.where(i == 0, 0, jnp.where(i == nblk - 1, 2, 1))
    nk = NA_KROWS * GRID_W
    kw = k_ref[pl.ds(k0, nk), :].astype(BF16)
    vw = v_ref[pl.ds(k0, nk), :].astype(BF16)
    kc = ck_ref[0].astype(BF16)
    vc = cv_ref[0].astype(BF16)
    qp = q_ref[...].astype(F32) * (C_HEAD_DIM ** -0.5)
    lo = _lane(qp.shape) < 64
    outs = []
    for half in range(2):
        qm = jnp.where(lo if half == 0 else jnp.logical_not(lo), qp, 0.0).astype(BF16)
        s_win = _dot_nt(qm, kw) + bias_ref[btype, half]
        s_ctx = _dot_nt(qm, kc)
        m = jnp.maximum(s_win.max(axis=-1, keepdims=True), s_ctx.max(axis=-1, keepdims=True))
        e_win = jnp.exp(s_win - m)
        e_ctx = jnp.exp(s_ctx - m)
        l = e_win.sum(axis=-1, keepdims=True) + e_ctx.sum(axis=-1, keepdims=True)
        o = _dot(e_win.astype(BF16), vw) + _dot(e_ctx.astype(BF16), vc)
        outs.append(o / l)
    o_ref[...] = jnp.where(lo, outs[0], outs[1]).astype(o_ref.dtype)


def _attn_c_sample(q, k, v, ck, cv, toe):
    rows = DEC_SEQ // GRID_W
    nblk = rows // NA_QROWS
    qrows = NA_QROWS * GRID_W
    npair = C_HEADS // 2
    return pl.pallas_call(
        _attn_c_sample_kernel,
        grid=(npair, DEC_BATCH, nblk),
        in_specs=[pl.BlockSpec((qrows, LANES), lambda p, b, i: (b * nblk + i, p)),
                  pl.BlockSpec((DEC_SEQ, LANES), lambda p, b, i: (b, p)),
                  pl.BlockSpec((DEC_SEQ, LANES), lambda p, b, i: (b, p)),
                  pl.BlockSpec((1, PAST_LEN, LANES), lambda p, b, i: (b, 0, p)),
                  pl.BlockSpec((1, PAST_LEN, LANES), lambda p, b, i: (b, 0, p)),
                  pl.BlockSpec((1,) + toe.shape[1:], lambda p, b, i: (p, 0, 0, 0, 0))],
        out_specs=pl.BlockSpec((qrows, LANES), lambda p, b, i: (b * nblk + i, p)),
        out_shape=jax.ShapeDtypeStruct((T_SAMPLE, C_WIDTH), BF16),
        scratch_shapes=[pltpu.VMEM((3, 2, qrows, NA_KROWS * GRID_W), F32)],
        compiler_params=_cparams(("arbitrary", "arbitrary", "arbitrary")),
        name="attn_c_sample",
    )(q, k, v, ck, cv, toe)


def _na_toeplitz(rpb):
    w = GRID_W
    nd_r, nd_c = 2 * NA_ROWS - 1, 2 * NA_COLS - 1
    c = np.arange(w)
    cs = np.clip(c - NA_COLS // 2, 0, w - NA_COLS)
    col_ok = (c[None, :] >= cs[:, None]) & (c[None, :] < cs[:, None] + NA_COLS)
    dcol = c[None, :] - c[:, None] + NA_COLS - 1
    onehot = (np.arange(nd_c)[:, None, None] == dcol[None]).reshape(nd_c, w * w)
    toe = jnp.dot(rpb.reshape(C_HEADS * nd_r, nd_c), jnp.asarray(onehot, F32), precision=lax.Precision.HIGHEST)
    toe = jnp.where(col_ok[None, None], toe.reshape(C_HEADS, nd_r, w, w), NEG)
    toe = jnp.pad(toe, ((0, 0), (NA_KROWS, NA_KROWS), (0, 0), (0, 0)), constant_values=NEG)
    return toe.reshape(C_HEADS // 2, 2, nd_r + 2 * NA_KROWS, w, w)


def _rope_tables():
    half = A_HEAD_DIM // 2
    t = jnp.arange(DEC_SEQ)
    row = (t // GRID_W).astype(F32)
    colp = (t % GRID_W).astype(F32)
    freqs = 1.0 / (ROPE_BASE ** (jnp.arange(0, half, 2, dtype=F32) / half))
    d = np.arange(LANES) % A_HEAD_DIM
    pos = jnp.where(jnp.asarray(d < half)[None, :], row[:, None], colp[:, None])
    ang = pos * freqs[d % (half // 2)][None, :]
    sign = jnp.asarray(np.where((d % half) < half // 2, -1.0, 1.0), F32)[None, :]
    return jnp.cos(ang), jnp.sin(ang) * sign


def _head_avg_matrix():
    idx = np.arange(LANES) // A_HEAD_DIM
    return jnp.asarray((idx[:, None] == idx[None, :]).astype(np.float32) / A_HEAD_DIM, BF16)


def _tri_matrices():
    i = np.arange(MLSTM_CHUNK)
    upper = (i[:, None] <= i[None, :]).astype(np.float32)
    lower = (i[:, None] >= i[None, :]).astype(np.float32)
    return jnp.asarray(np.stack([upper, lower]), BF16)


def _router_weights(wg, bg, we, be):
    w = jnp.zeros((D_MODEL, LANES), F32).at[:, :N_GROUPS].set(wg).at[:, N_GROUPS:N_GROUPS + N_EXPERTS].set(we)
    b = jnp.zeros((1, LANES), F32).at[0, :N_GROUPS].set(bg).at[0, N_GROUPS:N_GROUPS + N_EXPERTS].set(be)
    hi = w.astype(BF16)
    lo = (w - hi.astype(F32)).astype(BF16)
    return jnp.concatenate([hi, lo], axis=1), b


def _moe(xn, rtt, cnt, order, tab, w1, w3, w2, layer):
    dest, slot_src, pair_tok, block_expert, n_valid = _moe_plan(rtt, cnt, order, tab)
    return dest, _experts(block_expert, slot_src, pair_tok, n_valid, xn, w1, w3, w2, layer)


def kernel(x_prompt, x_sample, cache_attn_k, cache_attn_v, state_mlstm_C, state_mlstm_n, state_mlstm_m,
           cache_na_k, cache_na_v, c, c_ctx, norm_mix, norm_ffn, norm_final, ada_w, ada_b,
           ab_w_in, ab_w_out, ab_q_norm, ab_k_norm, ab_gate_bias, ab_out_norm,
           na_w_in, na_w_out, na_rpb, moe_wg, moe_bg, moe_we, moe_be, moe_w1, moe_w3, moe_w2):
    xp = x_prompt.reshape(T_PROMPT, D_MODEL)
    xs = x_sample.reshape(T_SAMPLE, D_MODEL)
    cond =jnp.zeros((N_COND, D_MODEL), F32).at[0].set(c_ctx).at[1:1 + DEC_BATCH].set(c)
    mod = _modulation(cond, ada_w, ada_b)
    gfin = norm_final.reshape(1, D_MODEL)

    w_in = ab_w_in[0]
    o_aq, o_ak, o_av, o_bq, o_bk, o_bv, o_bo, o_bg = np.cumsum((0,) + (A_WIDTH, A_KV_WIDTH, A_KV_WIDTH,
                                                                       B_WIDTH, B_WIDTH, B_WIDTH, B_WIDTH))
    wb = jnp.concatenate([w_in[:, o_bq:o_bk], w_in[:, o_bv:o_bg]], axis=1).astype(BF16)
    wq = w_in[:, o_aq:o_ak].astype(BF16)
    wkv = w_in[:, o_ak:o_bq].astype(BF16)
    gate_rows = np.array([0, 8, 1, 9])
    wg = w_in[:, o_bg:o_bg + 4 * B_HEADS].reshape(D_MODEL, 4, B_HEADS)
    wgt = jnp.zeros((B_HEADS, 16, D_MODEL), F32).at[:, gate_rows, :].set(wg.transpose(2, 1, 0))
    wgt = jnp.concatenate([w_in[:, o_bk:o_bv].T, wgt.reshape(16 * B_HEADS, D_MODEL)], axis=0).astype(BF16)
    g_mix = norm_mix[0].reshape(1, D_MODEL)
    ob_p, oq_p, okv_p, okt_p, ogt_p = _proj_ab(xp, mod[0], g_mix, wb, wq, wkv, wgt, 0)
    ob_s, oq_s, okv_s, okt_s, ogt_s = _proj_ab(xs, mod[0], g_mix, wb, wq, wkv, wgt, T_PROMPT)

    qn = jnp.tile(ab_q_norm[0], 2).reshape(1, LANES)
    kn = jnp.tile(ab_k_norm[0], 2).reshape(1, LANES)
    bd = _head_avg_matrix()
    cos, sin = _rope_tables()
    a_p, new_k, new_v = _attn_a_prompt(oq_p, okv_p, qn, kn, bd)
    ck = cache_attn_k[:, 0].reshape(DEC_BATCH, PAST_LEN, A_KV_WIDTH)
    cv = cache_attn_v[:, 0].reshape(DEC_BATCH, PAST_LEN, A_KV_WIDTH)
    a_s = _attn_a_sample(oq_s, okv_s, ck, cv, cos, sin, qn, kn, bd)

    gb = ab_gate_bias[0]
    brow = jnp.zeros((B_HEADS, 16, LANES), F32).at[:, gate_rows, :].set(
        jnp.broadcast_to(gb.T[:, :, None], (B_HEADS, 4, LANES)))
    onorm = ab_out_norm[0].reshape(B_HEADS, 1, B_HEAD_DIM)
    tri = _tri_matrices()
    b_p, cT, nT, mT = _mlstm(ob_p, okt_p, ogt_p, brow, onorm, tri, n=SEQ, nseq=BATCH, emit_state=True)
    n0 = jnp.broadcast_to(state_mlstm_n[:, 0][..., None], state_mlstm_C[:, 0].shape)
    c0 = jnp.concatenate([state_mlstm_C[:, 0], n0], axis=-1)
    m0 = jnp.broadcast_to(state_mlstm_m[:, 0].transpose(0, 2, 1)[..., None], (DEC_BATCH, B_HEADS, 2, LANES))
    (b_s,) = _mlstm(ob_s, okt_s, ogt_s, brow, onorm, tri, n=DEC_SEQ, nseq=DEC_BATCH, init=(c0, m0))

    w_out = ab_w_out[0].astype(BF16)
    wr, br = _router_weights(moe_wg[0], moe_bg[0], moe_we[0], moe_be[0])
    x1, xn, rt, *plan = _post([(xp, xs), (a_p, a_s), (b_p, b_s)], [w_out[:A_WIDTH], w_out[A_WIDTH:]], mod[0],
                              norm_ffn[0].reshape(1, D_MODEL), wr, br)
    dest, yb = _moe(xn, *plan, moe_w1, moe_w3, moe_w2, 0)

    g_mix = norm_mix[1].reshape(1, D_MODEL)
    w_in = na_w_in[0].astype(BF16)
    proj = (mod[1], g_mix, w_in)
    x_p, q_p, k_p, v_p = _combine(dest, yb, x1, rt, mod[0], 0, T_PROMPT, proj=proj)
    x_s, q_s, k_s, v_s = _combine(dest, yb, x1, rt, mod[0], T_PROMPT, T_SAMPLE, proj=proj)
    o_p = _attn_c_prompt(q_p, k_p, v_p)
    nck = cache_na_k[:, 0].reshape(DEC_BATCH, PAST_LEN, C_WIDTH)
    ncv = cache_na_v[:, 0].reshape(DEC_BATCH, PAST_LEN, C_WIDTH)
    o_s = _attn_c_sample(q_s, k_s, v_s, nck, ncv, _na_toeplitz(na_rpb[0]))
    wr, br = _router_weights(moe_wg[1], moe_bg[1], moe_we[1], moe_be[1])
    x1, xn, rt, *plan = _post([(x_p, x_s), (o_p, o_s)], [na_w_out[0].astype(BF16)], mod[1],
                              norm_ffn[1].reshape(1, D_MODEL), wr, br)
    dest, yb = _moe(xn, *plan, moe_w1, moe_w3, moe_w2, 1)
    (y_prompt,) = _combine(dest, yb, x1, rt, mod[1], 0, T_PROMPT, gfin=gfin)
    (y_sample,) = _combine(dest, yb, x1, rt, mod[1], T_PROMPT, T_SAMPLE, gfin=gfin)
    y_prompt = y_prompt.reshape(BATCH, SEQ, D_MODEL)
    y_sample = y_sample.reshape(DEC_BATCH, DEC_SEQ, D_MODEL)
    new_attn_k = new_k.reshape(BATCH, 1, SEQ, A_KV_HEADS, A_HEAD_DIM)
    new_attn_v = new_v.reshape(BATCH, 1, SEQ, A_KV_HEADS, A_HEAD_DIM)
    new_mlstm_C = cT[:, None]
    new_mlstm_n = nT.transpose(0, 2, 1, 3)[:, None]
    new_mlstm_m = mT[..., 0].transpose(0, 2, 1)[:, None]
    new_na_k = k_p.astype(F32).reshape(BATCH, 1, SEQ, C_HEADS, C_HEAD_DIM)
    new_na_v = v_p.astype(F32).reshape(BATCH, 1, SEQ, C_HEADS, C_HEAD_DIM)
    return (y_prompt, y_sample, new_attn_k, new_attn_v, new_mlstm_C, new_mlstm_n, new_mlstm_m,
            new_na_k, new_na_v)
```
